```python
import math
import jax, jax.numpy as jnp
from jax import lax
import numpy as np

D_MODEL = 2048
BATCH = 8
SEQ = 2048
DEPTH = 4

CHUNK = 64
Q_BLOCK = 128
HEAD_DIM = 128
D_MIX = D_MODEL
D_SB = D_MIX // 2
D_CH = D_MIX - D_SB
N_HEADS_SB = D_SB // HEAD_DIM
N_HEADS_CH = D_CH // HEAD_DIM
LEFT_CHUNKS = 8
BAND = LEFT_CHUNKS + 1
REL_CLIP = 256
N_REL = REL_CLIP + CHUNK
D_IN = 4 * D_SB + 4 * D_CH
NORM_EPS = 1e-6
NEG_BIG = -1e30

kernel_name = "hybrid_stickbreak_chunkband_trunk"


def rms_norm(x, g):
    xf = x.astype(jnp.float32)
    y = xf * lax.rsqrt(jnp.mean(xf * xf, axis=-1, keepdims=True) + NORM_EPS)
    return (y * g.astype(jnp.float32)).astype(x.dtype)


def split_heads(t, n_heads):
    b, s, _ = t.shape
    return t.reshape(b, s, n_heads, HEAD_DIM).transpose(0, 2, 1, 3)


def merge_heads(t):
    b, h, s, d = t.shape
    return t.transpose(0, 2, 1, 3).reshape(b, s, h * d)


def stick_breaking_attention(q, k, v):
    seq = q.shape[2]
    scale = q.shape[-1] ** -0.5
    outs = []
    for blk in range(seq // Q_BLOCK):
        t0 = blk * Q_BLOCK
        t1 = t0 + Q_BLOCK
        qb = q[:, :, t0:t1]
        kb = k[:, :, :t1]
        vb = v[:, :, :t1]
        z = jnp.einsum('bhtd,bhsd->bhts', qb, kb).astype(jnp.float32) * scale
        t_idx = jnp.arange(t0, t1)[:, None]
        s_idx = jnp.arange(t1)[None, :]
        causal = s_idx < t_idx
        log_stay = jnp.where(causal, jax.nn.log_sigmoid(-z), 0.0)
        after = lax.cumsum(log_stay, axis=3, reverse=True) - log_stay
        w = jnp.where(causal, jnp.exp(jax.nn.log_sigmoid(z) + after), 0.0)
        outs.append(jnp.einsum('bhts,bhsd->bhtd', w.astype(v.dtype), vb))
    return jnp.concatenate(outs, axis=2)


def chunk_band_attention(q, k, v, q_gain, k_gain, rel_table):
    b, h, seq, d = q.shape
    nc = seq // CHUNK
    q = rms_norm(q, q_gain)
    k = rms_norm(k, k_gain)
    qc = q.reshape(b, h, nc, CHUNK, d)
    pad = ((0, 0), (0, 0), (LEFT_CHUNKS * CHUNK, 0), (0, 0))
    kc = jnp.pad(k, pad).reshape(b, h, nc + LEFT_CHUNKS, CHUNK, d)
    vc = jnp.pad(v, pad).reshape(b, h, nc + LEFT_CHUNKS, CHUNK, d)
    band_idx = jnp.arange(nc)[:, None] + jnp.arange(BAND)[None, :]
    kband = kc[:, :, band_idx].reshape(b, h, nc, BAND * CHUNK, d)
    vband = vc[:, :, band_idx].reshape(b, h, nc, BAND * CHUNK, d)
    scores = jnp.einsum('bhcid,bhcpd->bhcip', qc, kband).astype(jnp.float32) * (d ** -0.5)
    i_pos = np.arange(CHUNK)[:, None]
    p_pos = np.arange(BAND * CHUNK)[None, :]
    dist = LEFT_CHUNKS * CHUNK + i_pos - p_pos
    rel_idx = np.clip(dist, -(CHUNK - 1), REL_CLIP) + (CHUNK - 1)
    bias = rel_table.astype(jnp.float32)[:, rel_idx]
    scores = scores + bias[None, :, None]
    valid = jnp.repeat(band_idx >= LEFT_CHUNKS, CHUNK, axis=1)
    scores = jnp.where(valid[None, None, :, None, :], scores, NEG_BIG)
    probs = jax.nn.softmax(scores, axis=-1)
    out = jnp.einsum('bhcip,bhcpd->bhcid', probs.astype(v.dtype), vband)
    return out.reshape(b, h, seq, d)


def _fwd_setup_inputs(seed: int = 0) -> dict:
    key = jax.random.key(seed)
    ks = jax.random.split(key, 8)
    x = jax.random.normal(ks[0], (BATCH, SEQ, D_MODEL), jnp.float32)
    norm_g = 1.0 + 0.02 * jax.random.normal(ks[1], (DEPTH, D_MODEL), jnp.float32)
    w_in = jax.random.normal(ks[2], (DEPTH, D_MODEL, D_IN), jnp.float32) * D_MODEL ** -0.5
    q_norm_g = 1.0 + 0.02 * jax.random.normal(ks[3], (DEPTH, HEAD_DIM), jnp.float32)
    k_norm_g = 1.0 + 0.02 * jax.random.normal(ks[4], (DEPTH, HEAD_DIM), jnp.float32)
    rel_bias = 0.1 * jax.random.normal(ks[5], (DEPTH, N_HEADS_CH, N_REL), jnp.float32)
    w_out = jax.random.normal(ks[6], (DEPTH, D_MIX, D_MODEL), jnp.float32) * D_MIX ** -0.5
    return {"x": x, "norm_g": norm_g, "w_in": w_in, "q_norm_g": q_norm_g,
            "k_norm_g": k_norm_g, "rel_bias": rel_bias, "w_out": w_out}


def _fwd_reference(x, norm_g, w_in, q_norm_g, k_norm_g, rel_bias, w_out):
    splits = np.cumsum([D_SB, D_SB, D_SB, D_SB, D_CH, D_CH, D_CH])
    for layer in range(DEPTH):
        h = rms_norm(x, norm_g[layer])
        proj = jnp.einsum('bsd,de->bse', h, w_in[layer])
        qa, ka, va, ga, qb, kb, vb, gb = jnp.split(proj, splits, axis=-1)
        ya = stick_breaking_attention(split_heads(qa, N_HEADS_SB),
                                      split_heads(ka, N_HEADS_SB),
                                      split_heads(va, N_HEADS_SB))
        yb = chunk_band_attention(split_heads(qb, N_HEADS_CH),
                                  split_heads(kb, N_HEADS_CH),
                                  split_heads(vb, N_HEADS_CH),
                                  q_norm_g[layer], k_norm_g[layer], rel_bias[layer])
        mixed = jnp.concatenate([merge_heads(ya) * jax.nn.silu(ga),
                                 merge_heads(yb) * jax.nn.silu(gb)], axis=-1)
        x = x + jnp.einsum('bse,ed->bsd', mixed, w_out[layer])
    return x


import jax as _jax
import jax.numpy as _jnp

TWIN_FORMAT = 'train_step'
FWD_PARAMS = ['x', 'norm_g', 'w_in', 'q_norm_g', 'k_norm_g', 'rel_bias', 'w_out']
TWIN_WEIGHTS = ['norm_g', 'w_in', 'q_norm_g', 'k_norm_g', 'rel_bias', 'w_out']
TWIN_DIFF_INPUT = 'x'
TWIN_INPUTS = ['x', 'norm_g', 'w_in', 'q_norm_g', 'k_norm_g', 'rel_bias', 'w_out', 'loss_target', 'm_norm_g', 'm_w_in', 'm_q_norm_g', 'm_k_norm_g', 'm_rel_bias', 'm_w_out', 'v_norm_g', 'v_w_in', 'v_q_norm_g', 'v_k_norm_g', 'v_rel_bias', 'v_w_out']
TWIN_OUTPUTS = ['loss', 'grad_x', 'grad_norm_g', 'grad_w_in', 'grad_q_norm_g', 'grad_k_norm_g', 'grad_rel_bias', 'grad_w_out', 'delta_norm_g', 'delta_w_in', 'delta_q_norm_g', 'delta_k_norm_g', 'delta_rel_bias', 'delta_w_out', 'new_m_norm_g', 'new_m_w_in', 'new_m_q_norm_g', 'new_m_k_norm_g', 'new_m_rel_bias', 'new_m_w_out', 'new_v_norm_g', 'new_v_w_in', 'new_v_q_norm_g', 'new_v_k_norm_g', 'new_v_rel_bias', 'new_v_w_out']
TWIN_LEAF_KINDS = {'loss': 'loss', 'grad_x': 'grad_x', 'grad_norm_g': 'grad_w', 'grad_w_in': 'grad_w', 'grad_q_norm_g': 'grad_w', 'grad_k_norm_g': 'grad_w', 'grad_rel_bias': 'grad_w', 'grad_w_out': 'grad_w', 'delta_norm_g': 'delta_w', 'delta_w_in': 'delta_w', 'delta_q_norm_g': 'delta_w', 'delta_k_norm_g': 'delta_w', 'delta_rel_bias': 'delta_w', 'delta_w_out': 'delta_w', 'new_m_norm_g': 'new_m', 'new_m_w_in': 'new_m', 'new_m_q_norm_g': 'new_m', 'new_m_k_norm_g': 'new_m', 'new_m_rel_bias': 'new_m', 'new_m_w_out': 'new_m', 'new_v_norm_g': 'new_v', 'new_v_w_in': 'new_v', 'new_v_q_norm_g': 'new_v', 'new_v_k_norm_g': 'new_v', 'new_v_rel_bias': 'new_v', 'new_v_w_out': 'new_v'}


def _forward(args):
    return _fwd_reference(*[args[k] for k in FWD_PARAMS])


def _output_shape():
    out = _jax.eval_shape(lambda: _forward(_fwd_setup_inputs(0)))
    return out.shape, out.dtype

N_MICROBATCH = 1
ADAM_LR = 0.001
ADAM_B1 = 0.9
ADAM_B2 = 0.999
ADAM_EPS = 1e-08
ADAM_WD = 0.01
ADAM_STEP = 10
PER_EXAMPLE_BATCH_AXIS = {'x': 0, 'loss_target': 0}
SHARED_INPUTS = []
_WEIGHT_DTYPES = {'norm_g': _jnp.float32, 'w_in': _jnp.float32, 'q_norm_g': _jnp.float32, 'k_norm_g': _jnp.float32, 'rel_bias': _jnp.float32, 'w_out': _jnp.float32}
MOMENT_SCALE = {'norm_g': 1.460306e+00, 'w_in': 3.283970e-02, 'q_norm_g': 1.000078e-01, 'k_norm_g': 9.995956e-02, 'rel_bias': 6.278455e-03, 'w_out': 3.420715e-02}


def _to_microbatches(a, axis):
    t = _jnp.moveaxis(a, axis, 0)
    t = t.reshape((N_MICROBATCH, t.shape[0] // N_MICROBATCH) + t.shape[1:])
    return _jnp.moveaxis(t, 1, axis + 1)


def setup_inputs(seed: int = 0) -> dict:
    inp = _fwd_setup_inputs(seed)
    key = _jax.random.fold_in(_jax.random.key(seed), 7919)
    shape, _ = _output_shape()
    out = dict(inp)
    out["loss_target"] = _jax.random.normal(_jax.random.fold_in(key, 0), shape, _jnp.float32)
    for i, name in enumerate(TWIN_WEIGHTS):
        w = inp[name].astype(_jnp.float32)
        if MOMENT_SCALE is None:
            s = _jnp.sqrt(_jnp.mean(_jnp.square(w)) + 1e-30)
        else:
            s = MOMENT_SCALE[name]
        km, kv = _jax.random.split(_jax.random.fold_in(key, i + 1))
        out[name] = w
        out["m_" + name] = s * _jax.random.normal(km, w.shape, _jnp.float32)
        out["v_" + name] = (s * s) * _jax.random.uniform(kv, w.shape, _jnp.float32, 0.5, 1.5)
    if N_MICROBATCH > 1:
        for name, axis in PER_EXAMPLE_BATCH_AXIS.items():
            out[name] = _to_microbatches(out[name], axis)
    return {'x': out['x'], 'norm_g': out['norm_g'], 'w_in': out['w_in'], 'q_norm_g': out['q_norm_g'], 'k_norm_g': out['k_norm_g'], 'rel_bias': out['rel_bias'], 'w_out': out['w_out'], 'loss_target': out['loss_target'], 'm_norm_g': out['m_norm_g'], 'm_w_in': out['m_w_in'], 'm_q_norm_g': out['m_q_norm_g'], 'm_k_norm_g': out['m_k_norm_g'], 'm_rel_bias': out['m_rel_bias'], 'm_w_out': out['m_w_out'], 'v_norm_g': out['v_norm_g'], 'v_w_in': out['v_w_in'], 'v_q_norm_g': out['v_q_norm_g'], 'v_k_norm_g': out['v_k_norm_g'], 'v_rel_bias': out['v_rel_bias'], 'v_w_out': out['v_w_out']}


def _loss(weights, diff, rest, loss_target):
    with _jax.named_scope("forward"):
        args = {**rest, TWIN_DIFF_INPUT: diff, **{k: w.astype(_WEIGHT_DTYPES[k]) for k, w in weights.items()}}
        y = _forward(args)
    with _jax.named_scope("loss_head"):
        err = _jnp.square(y.astype(_jnp.float32) - loss_target)
        return 0.5 * _jnp.sum(_jnp.mean(err, axis=-1)) if err.ndim else 0.5 * err


def _adamw(w, g, m, v):
    m = ADAM_B1 * m + (1.0 - ADAM_B1) * g
    v = ADAM_B2 * v + (1.0 - ADAM_B2) * _jnp.square(g)
    m_hat = m / (1.0 - ADAM_B1 ** ADAM_STEP)
    v_hat = v / (1.0 - ADAM_B2 ** ADAM_STEP)
    delta = -ADAM_LR * (m_hat / (_jnp.sqrt(v_hat) + ADAM_EPS) + ADAM_WD * w)
    return delta, m, v


def reference(x, norm_g, w_in, q_norm_g, k_norm_g, rel_bias, w_out, loss_target, m_norm_g, m_w_in, m_q_norm_g, m_k_norm_g, m_rel_bias, m_w_out, v_norm_g, v_w_in, v_q_norm_g, v_k_norm_g, v_rel_bias, v_w_out):
    given = dict(x=x, norm_g=norm_g, w_in=w_in, q_norm_g=q_norm_g, k_norm_g=k_norm_g, rel_bias=rel_bias, w_out=w_out, loss_target=loss_target, m_norm_g=m_norm_g, m_w_in=m_w_in, m_q_norm_g=m_q_norm_g, m_k_norm_g=m_k_norm_g, m_rel_bias=m_rel_bias, m_w_out=m_w_out, v_norm_g=v_norm_g, v_w_in=v_w_in, v_q_norm_g=v_q_norm_g, v_k_norm_g=v_k_norm_g, v_rel_bias=v_rel_bias, v_w_out=v_w_out)
    weights = {n: given[n] for n in TWIN_WEIGHTS}
    shared = {n: given[n] for n in SHARED_INPUTS}
    per_example = {n: given[n] for n in ['x']}
    grad_fn = _jax.value_and_grad(_loss, argnums=(0, 1))

    def one_microbatch(ex, loss_target):
        ex = dict(ex)
        diff = ex.pop(TWIN_DIFF_INPUT)
        return grad_fn(weights, diff, {**shared, **ex}, loss_target)

    if N_MICROBATCH == 1:
        loss, (grad_w, grad_x) = one_microbatch(per_example, given["loss_target"])
    else:
        def body(carry, xs):
            loss_sum, grad_sum = carry
            l_k, (gw_k, gx_k) = one_microbatch(xs[0], xs[1])
            with _jax.named_scope("update"):
                return (loss_sum + l_k, _jax.tree.map(_jnp.add, grad_sum, gw_k)), gx_k

        init = (_jnp.zeros((), _jnp.float32), _jax.tree.map(_jnp.zeros_like, weights))
        (loss, grad_w), grad_x = _jax.lax.scan(body, init, (per_example, given["loss_target"]))
    with _jax.named_scope("update"):
        delta_w, new_m, new_v = {}, {}, {}
        for n in TWIN_WEIGHTS:
            delta_w[n], new_m[n], new_v[n] = _adamw(weights[n], grad_w[n], given["m_" + n], given["v_" + n])
    return (loss, grad_x, *[grad_w[n] for n in TWIN_WEIGHTS], *[delta_w[n] for n in TWIN_WEIGHTS],
            *[new_m[n] for n in TWIN_WEIGHTS], *[new_v[n] for n in TWIN_WEIGHTS])
```

```python
import functools

import jax
import jax.numpy as jnp
import numpy as np
from jax import lax
from jax.experimental import pallas as pl
from jax.experimental.pallas import tpu as pltpu

F32 = jnp.float32
BF16 = jnp.bfloat16

HEAD = 128
CHUNK = 64
LEFT_CHUNKS = 8
REL_CLIP = 256
N_REL = REL_CLIP + CHUNK
NORM_EPS = 1e-6
NEG_BIG = -1e30
TQ = 256
PAD = LEFT_CHUNKS * CHUNK
WIN = PAD + TQ
EXT = 1024
SCALE = HEAD ** -0.5

ADAM_LR = 0.001
ADAM_B1 = 0.9
ADAM_B2 = 0.999
ADAM_EPS = 1e-08
ADAM_WD = 0.01
ADAM_STEP = 10

ANY = pl.BlockSpec(memory_space=pl.ANY)
MESH = pl.DeviceIdType.MESH


def _params(sem=None, vmem_mb=None):
    kw = {}
    if sem is not None:
        kw["dimension_semantics"] = sem
    if vmem_mb is not None:
        kw["vmem_limit_bytes"] = vmem_mb << 20
    return pltpu.CompilerParams(**kw)


def _dot(a, b):
    return jnp.dot(a, b, preferred_element_type=F32)


def _dot_nt(a, b):
    return lax.dot_general(a, b, (((1,), (1,)), ((), ())), preferred_element_type=F32)


def _dot_tn(a, b):
    return lax.dot_general(a, b, (((0,), (0,)), ((), ())), preferred_element_type=F32)


def _split_dot(x, m):
    hi = x.astype(BF16)
    lo = (x - hi.astype(F32)).astype(BF16)
    return _dot(hi, m) + _dot(lo, m)


def _silu_parts(g):
    sg = 1.0 / (1.0 + jnp.exp(-g))
    return g * sg, sg * (1.0 + g * (1.0 - sg))


def _cast_bf16(w, name):
    nl, r, c = w.shape
    tr = min(r, 512)

    def body(w_ref, o_ref):
        o_ref[...] = w_ref[...].astype(BF16)

    spec = pl.BlockSpec((None, tr, c), lambda l, i: (l, i, 0))
    return pl.pallas_call(
        body, name=name, grid=(nl, r // tr), in_specs=[spec], out_specs=spec,
        out_shape=jax.ShapeDtypeStruct((nl, r, c), BF16),
        compiler_params=_params(("parallel", "parallel")))(w)


def _add_rows(terms, out_dtype, name):
    r, c = terms[0].shape
    tr = min(r, 256)

    def body(*refs):
        acc = refs[0][...].astype(F32)
        for t in refs[1:-1]:
            acc = acc + t[...].astype(F32)
        refs[-1][...] = acc.astype(out_dtype)

    spec = pl.BlockSpec((tr, c), lambda i: (i, 0))
    return pl.pallas_call(
        body, name=name, grid=(r // tr,), in_specs=[spec] * len(terms), out_specs=spec,
        out_shape=jax.ShapeDtypeStruct((r, c), out_dtype),
        compiler_params=_params(("parallel",)))(*terms)


def _adamw(w, g, m, v, name):
    nl, r, c = w.shape
    tr = min(r, 256)
    c1 = 1.0 / (1.0 - ADAM_B1 ** ADAM_STEP)
    c2 = 1.0 / (1.0 - ADAM_B2 ** ADAM_STEP)

    def body(w_ref, g_ref, m_ref, v_ref, d_ref, nm_ref, nv_ref):
        gg = g_ref[...]
        nm = ADAM_B1 * m_ref[...] + (1.0 - ADAM_B1) * gg
        nv = ADAM_B2 * v_ref[...] + (1.0 - ADAM_B2) * (gg * gg)
        upd = (nm * c1) / (jnp.sqrt(nv * c2) + ADAM_EPS) + ADAM_WD * w_ref[...]
        d_ref[...] = -ADAM_LR * upd
        nm_ref[...] = nm
        nv_ref[...] = nv

    spec = pl.BlockSpec((None, tr, c), lambda l, i: (l, i, 0))
    shp = jax.ShapeDtypeStruct((nl, r, c), F32)
    return pl.pallas_call(
        body, name=name, grid=(nl, r // tr), in_specs=[spec] * 4, out_specs=[spec] * 3,
        out_shape=[shp] * 3, compiler_params=_params(("parallel", "parallel"), 40))(w, g, m, v)


def _rmsnorm_fwd(x, g):
    s, d = x.shape
    tm = min(s, 256)

    def body(x_ref, g_ref, h_ref):
        xv = x_ref[...]
        r = lax.rsqrt(jnp.mean(xv * xv, axis=1, keepdims=True) + NORM_EPS)
        h_ref[...] = (xv * r * g_ref[...]).astype(BF16)

    return pl.pallas_call(
        body, name="rmsnorm_fwd", grid=(s // tm,),
        in_specs=[pl.BlockSpec((tm, d), lambda i: (i, 0)), pl.BlockSpec((1, d), lambda i: (0, 0))],
        out_specs=pl.BlockSpec((tm, d), lambda i: (i, 0)),
        out_shape=jax.ShapeDtypeStruct((s, d), BF16),
        compiler_params=_params(("parallel",)))(x, g)


def _in_proj(h, w):
    s, d = h.shape
    nblk, _, nb = w.shape
    tm, tn = min(s, 512), min(nb, 1024)
    per = nb // tn

    def body(h_ref, w_ref, o_ref):
        o_ref[...] = _dot(h_ref[...], w_ref[...])

    return pl.pallas_call(
        body, name="in_proj", grid=(nblk * per, s // tm),
        in_specs=[pl.BlockSpec((tm, d), lambda n, m: (m, 0)),
                  pl.BlockSpec((None, d, tn), lambda n, m: (n // per, 0, n % per))],
        out_specs=pl.BlockSpec((tm, tn), lambda n, m: (m, n)),
        out_shape=jax.ShapeDtypeStruct((s, nblk * nb), F32),
        compiler_params=_params(("parallel", "parallel"), 40))(h, w)


def _tri(op):
    r = lax.broadcasted_iota(jnp.int32, (TQ, TQ), 0)
    c = lax.broadcasted_iota(jnp.int32, (TQ, TQ), 1)
    return op(r, c)


def _sb_logs(qb, kb, causal, diag):
    z = _dot_nt(qb, kb) * SCALE
    e = jnp.exp(-jnp.abs(z))
    l1p = jnp.log(1.0 + e)
    ls = jnp.minimum(-z, 0.0) - l1p
    lsig = jnp.minimum(z, 0.0) - l1p
    if diag:
        ls = jnp.where(causal, ls, 0.0)
    return z, e, ls, lsig


def _attn_a_fwd(proj, nh):
    s = proj.shape[0]
    nq = s // TQ

    def body(q_ref, k_ref, v_ref, o_ref, lt_ref):
        i = pl.program_id(1)
        qb = q_ref[...].astype(BF16)
        causal = _tri(lambda r, c: r > c)
        m_after = causal.astype(BF16)

        def tile(j, carry, acc, diag):
            off = pl.multiple_of(j * TQ, TQ)
            kb = k_ref[pl.ds(off, TQ), :].astype(BF16)
            vb = v_ref[pl.ds(off, TQ), :].astype(BF16)
            _, _, ls, lsig = _sb_logs(qb, kb, causal, diag)
            w = jnp.exp(lsig + _split_dot(ls, m_after) + carry)
            if diag:
                w = jnp.where(causal, w, 0.0)
            acc = acc + _dot(w.astype(BF16), vb)
            return carry + jnp.sum(ls, axis=1, keepdims=True), acc

        carry, acc = tile(i, jnp.zeros((TQ, 1), F32), jnp.zeros((TQ, HEAD), F32), True)

        def step(t, ca):
            return tile(i - 1 - t, ca[0], ca[1], False)

        carry, acc = lax.fori_loop(0, i, step, (carry, acc))
        o_ref[...] = acc
        lt_ref[...] = carry

    return pl.pallas_call(
        body, name="attn_a_fwd", grid=(nh, nq),
        in_specs=[pl.BlockSpec((TQ, HEAD), lambda h, i: (i, h)),
                  pl.BlockSpec((s, HEAD), lambda h, i: (0, nh + h)),
                  pl.BlockSpec((s, HEAD), lambda h, i: (0, 2 * nh + h))],
        out_specs=[pl.BlockSpec((TQ, HEAD), lambda h, i: (i, h)),
                   pl.BlockSpec((None, TQ, 1), lambda h, i: (h, i, 0))],
        out_shape=[jax.ShapeDtypeStruct((s, nh * HEAD), F32), jax.ShapeDtypeStruct((nh, s, 1), F32)],
        compiler_params=_params(("parallel", "arbitrary")))(proj, proj, proj)


def _band_valid(i):
    cl = lax.broadcasted_iota(jnp.int32, (TQ, WIN), 0) // CHUNK
    kl = lax.broadcasted_iota(jnp.int32, (TQ, WIN), 1) // CHUNK
    first = LEFT_CHUNKS - (TQ // CHUNK) * i
    return (kl >= cl) & (kl <= cl + LEFT_CHUNKS) & (kl >= first)


def _build_bias(e_ref, bias_ref):
    e8 = jnp.broadcast_to(e_ref[...], (8, EXT))
    row = lax.broadcasted_iota(jnp.int32, (8, EXT), 0)
    t8 = jnp.zeros((8, EXT), F32)
    for b in range(8):
        t8 = jnp.where(row == b, pltpu.roll(e8, b, 1) if b else e8, t8)
    for a in range(TQ // 8):
        sl = pltpu.roll(t8, 8 * a, 1) if a else t8
        bias_ref[pl.ds(8 * a, 8), :] = sl[:, :WIN]


def _reduce_bias_grad(db_ref):
    acc = jnp.zeros((8, EXT), F32)
    for a in range(TQ // 8):
        sl = db_ref[pl.ds(8 * a, 8), :]
        acc = acc + (pltpu.roll(sl, EXT - 8 * a, 1) if a else sl)
    row = lax.broadcasted_iota(jnp.int32, (8, EXT), 0)
    tot = jnp.zeros((8, EXT), F32)
    for b in range(8):
        tot = tot + jnp.where(row == b, pltpu.roll(acc, EXT - b, 1) if b else acc, 0.0)
    return jnp.sum(tot, axis=0, keepdims=True)


def _band_fill(k_ref, v_ref, kg_ref, kn_pad, v_pad, s):
    k = k_ref[...]
    rk = lax.rsqrt(jnp.mean(k * k, axis=1, keepdims=True) + NORM_EPS)
    kn_pad[pl.ds(0, PAD), :] = jnp.zeros((PAD, HEAD), BF16)
    kn_pad[pl.ds(PAD, s), :] = (k * rk * kg_ref[...]).astype(BF16)
    v_pad[pl.ds(0, PAD), :] = jnp.zeros((PAD, HEAD), BF16)
    v_pad[pl.ds(PAD, s), :] = v_ref[...].astype(BF16)


def _band_probs(q_ref, qg_ref, kn_pad, bias_ref, i):
    q = q_ref[...]
    rq = lax.rsqrt(jnp.mean(q * q, axis=1, keepdims=True) + NORM_EPS)
    qhat = q * rq
    qn = (qhat * qg_ref[...]).astype(BF16)
    off = pl.multiple_of(i * TQ, TQ)
    kw = kn_pad[pl.ds(off, WIN), :]
    sc = _dot_nt(qn, kw) * SCALE + bias_ref[...]
    sc = jnp.where(_band_valid(i), sc, NEG_BIG)
    p = jnp.exp(sc - jnp.max(sc, axis=1, keepdims=True))
    pn = p / jnp.sum(p, axis=1, keepdims=True)
    return rq, qhat, qn, kw, off, pn


def _attn_b_fwd(proj, qg, kg, ext, nh):
    s = proj.shape[0]
    nq = s // TQ

    def body(q_ref, k_ref, v_ref, qg_ref, kg_ref, e_ref, o_ref, kn_pad, v_pad, bias_ref):
        i = pl.program_id(1)

        @pl.when(i == 0)
        def _():
            _band_fill(k_ref, v_ref, kg_ref, kn_pad, v_pad, s)
            _build_bias(e_ref, bias_ref)

        _, _, _, _, off, pn = _band_probs(q_ref, qg_ref, kn_pad, bias_ref, i)
        o_ref[...] = _dot(pn.astype(BF16), v_pad[pl.ds(off, WIN), :])

    vec = pl.BlockSpec((1, HEAD), lambda h, i: (0, 0))
    return pl.pallas_call(
        body, name="attn_b_fwd", grid=(nh, nq),
        in_specs=[pl.BlockSpec((TQ, HEAD), lambda h, i: (i, 4 * nh + h)),
                  pl.BlockSpec((s, HEAD), lambda h, i: (0, 5 * nh + h)),
                  pl.BlockSpec((s, HEAD), lambda h, i: (0, 6 * nh + h)),
                  vec, vec,
                  pl.BlockSpec((None, 1, EXT), lambda h, i: (h, 0, 0))],
        out_specs=pl.BlockSpec((TQ, HEAD), lambda h, i: (i, h)),
        out_shape=jax.ShapeDtypeStruct((s, nh * HEAD), F32),
        scratch_shapes=[pltpu.VMEM((s + PAD, HEAD), BF16), pltpu.VMEM((s + PAD, HEAD), BF16),
                        pltpu.VMEM((TQ, WIN), F32)],
        compiler_params=_params(("parallel", "arbitrary")))(proj, proj, proj, qg, kg, ext)


def _out_proj(x, ya, yb, proj, w):
    s, d = x.shape
    ds_ = ya.shape[1]
    tm = min(s, 256)

    def body(x_ref, ya_ref, yb_ref, ga_ref, gb_ref, w_ref, o_ref, mix_ref):
        ma = (ya_ref[...] * _silu_parts(ga_ref[...])[0]).astype(BF16)
        mb = (yb_ref[...] * _silu_parts(gb_ref[...])[0]).astype(BF16)
        mix_ref[:, :ds_] = ma
        mix_ref[:, ds_:] = mb
        o_ref[...] = x_ref[...] + _dot(ma, w_ref[pl.ds(0, ds_), :]) + _dot(mb, w_ref[pl.ds(ds_, ds_), :])

    row = lambda width: pl.BlockSpec((tm, width), lambda i: (i, 0))
    return pl.pallas_call(
        body, name="out_proj", grid=(s // tm,),
        in_specs=[row(d), row(ds_), row(ds_),
                  pl.BlockSpec((tm, ds_), lambda i: (i, 3)), pl.BlockSpec((tm, ds_), lambda i: (i, 7)),
                  pl.BlockSpec((2 * ds_, d), lambda i: (0, 0))],
        out_specs=[row(d), row(2 * ds_)],
        out_shape=[jax.ShapeDtypeStruct((s, d), F32), jax.ShapeDtypeStruct((s, 2 * ds_), BF16)],
        compiler_params=_params(("parallel",), 48))(x, ya, yb, proj, proj, w)


def _loss_head(y, tgt):
    s, d = y.shape
    tm = min(s, 256)

    def body(y_ref, t_ref, dy_ref, l_ref):
        @pl.when(pl.program_id(0) == 0)
        def _():
            l_ref[...] = jnp.zeros_like(l_ref)

        err = y_ref[...] - t_ref[...]
        dy_ref[...] = err * (1.0 / d)
        l_ref[...] += 0.5 * jnp.sum(jnp.mean(err * err, axis=1, keepdims=True), axis=0, keepdims=True)

    row = pl.BlockSpec((tm, d), lambda i: (i, 0))
    return pl.pallas_call(
        body, name="loss_head", grid=(s // tm,), in_specs=[row, row],
        out_specs=[row, pl.BlockSpec((8, 128), lambda i: (0, 0))],
        out_shape=[jax.ShapeDtypeStruct((s, d), F32), jax.ShapeDtypeStruct((8, 128), F32)],
        compiler_params=_params(("arbitrary",)))(y, tgt)


def _out_proj_bwd(dxo, ya, yb, proj, w):
    s, d = dxo.shape
    ds_ = ya.shape[1]
    tm = min(s, 256)

    def body(dx_ref, ya_ref, yb_ref, ga_ref, gb_ref, w_ref, dya_ref, dyb_ref, dga_ref, dgb_ref):
        dxb = dx_ref[...].astype(BF16)
        for y_ref, g_ref, lo, dy_ref, dg_ref in ((ya_ref, ga_ref, 0, dya_ref, dga_ref),
                                                 (yb_ref, gb_ref, ds_, dyb_ref, dgb_ref)):
            dmix = _dot_nt(dxb, w_ref[pl.ds(lo, ds_), :])
            act, dact = _silu_parts(g_ref[...])
            dy_ref[...] = dmix * act
            dg_ref[...] = (dmix * y_ref[...] * dact).astype(BF16)

    row = lambda width: pl.BlockSpec((tm, width), lambda i: (i, 0))
    return pl.pallas_call(
        body, name="out_proj_bwd", grid=(s // tm,),
        in_specs=[row(d), row(ds_), row(ds_),
                  pl.BlockSpec((tm, ds_), lambda i: (i, 3)), pl.BlockSpec((tm, ds_), lambda i: (i, 7)),
                  pl.BlockSpec((2 * ds_, d), lambda i: (0, 0))],
        out_specs=[row(ds_)] * 4,
        out_shape=[jax.ShapeDtypeStruct((s, ds_), F32)] * 2 + [jax.ShapeDtypeStruct((s, ds_), BF16)] * 2,
        compiler_params=_params(("parallel",), 48))(dxo, ya, yb, proj, proj, w)


def _wgrad(a, b, nblk, col_blocks, name):
    s, m = a.shape
    n = b.shape[1]
    if col_blocks:
        tr = min(m, 512)
        nb = n // nblk
        tn = min(nb, 1024)
        per = nb // tn
        out_shape = (nblk, m, nb)
        out_spec = pl.BlockSpec((None, tr, tn), lambda j, r: (j // per, r, j % per))
    else:
        tn = min(n, 1024)
        tr = m // nblk
        out_shape = (nblk, tr, n)
        out_spec = pl.BlockSpec((None, tr, tn), lambda j, r: (r, 0, j))

    def body(a_ref, b_ref, o_ref):
        o_ref[...] = _dot_tn(a_ref[...].astype(BF16), b_ref[...].astype(BF16)).astype(BF16)

    return pl.pallas_call(
        body, name=name, grid=(n // tn, m // tr),
        in_specs=[pl.BlockSpec((s, tr), lambda j, r: (0, r)), pl.BlockSpec((s, tn), lambda j, r: (0, j))],
        out_specs=out_spec, out_shape=jax.ShapeDtypeStruct(out_shape, BF16),
        compiler_params=_params(("parallel", "parallel"), 48))(a, b)


def _attn_a_bwd(proj, lt, dya, nh):
    s = proj.shape[0]
    nq = s // TQ

    def body(q_ref, k_ref, v_ref, lt_ref, do_ref, dq_ref, dk_ref, dv_ref, dk_acc, dv_acc):
        i = pl.program_id(1)

        @pl.when(i == 0)
        def _():
            dk_acc[...] = jnp.zeros_like(dk_acc)
            dv_acc[...] = jnp.zeros_like(dv_acc)

        qb = q_ref[...].astype(BF16)
        dob = do_ref[...].astype(BF16)
        total = lt_ref[...]
        causal = _tri(lambda r, c: r > c)
        m_upto = _tri(lambda r, c: r <= c).astype(BF16)
        m_before = _tri(lambda r, c: r < c).astype(BF16)

        def tile(j, c_ls, c_g, dq, diag):
            off = pl.multiple_of(j * TQ, TQ)
            kb = k_ref[pl.ds(off, TQ), :].astype(BF16)
            vb = v_ref[pl.ds(off, TQ), :].astype(BF16)
            z, e, ls, lsig = _sb_logs(qb, kb, causal, diag)
            w = jnp.exp(lsig + (total - (c_ls + _split_dot(ls, m_upto))))
            if diag:
                w = jnp.where(causal, w, 0.0)
            g = w * _dot_nt(dob, vb)
            before = c_g + _split_dot(g, m_before)
            rinv = 1.0 / (1.0 + e)
            beta = jnp.where(z >= 0.0, rinv, e * rinv)
            dz = g * (1.0 - beta) - beta * before
            if diag:
                dz = jnp.where(causal, dz, 0.0)
            dzb = (dz * SCALE).astype(BF16)
            dq = dq + _dot(dzb, kb)
            dk_acc[pl.ds(off, TQ), :] += _dot_tn(dzb, qb)
            dv_acc[pl.ds(off, TQ), :] += _dot_tn(w.astype(BF16), dob)
            return (c_ls + jnp.sum(ls, axis=1, keepdims=True),
                    c_g + jnp.sum(g, axis=1, keepdims=True), dq)

        zero = jnp.zeros((TQ, 1), F32)

        def step(j, ca):
            return tile(j, ca[0], ca[1], ca[2], False)

        carry = lax.fori_loop(0, i, step, (zero, zero, jnp.zeros((TQ, HEAD), F32)))
        carry = tile(i, carry[0], carry[1], carry[2], True)
        dq_ref[...] = carry[2].astype(BF16)

        @pl.when(i == nq - 1)
        def _():
            dk_ref[...] = dk_acc[...].astype(BF16)
            dv_ref[...] = dv_acc[...].astype(BF16)

    blk = pl.BlockSpec((TQ, HEAD), lambda h, i: (i, h))
    col = pl.BlockSpec((s, HEAD), lambda h, i: (0, h))
    shp = jax.ShapeDtypeStruct((s, nh * HEAD), BF16)
    return pl.pallas_call(
        body, name="attn_a_bwd", grid=(nh, nq),
        in_specs=[blk,
                  pl.BlockSpec((s, HEAD), lambda h, i: (0, nh + h)),
                  pl.BlockSpec((s, HEAD), lambda h, i: (0, 2 * nh + h)),
                  pl.BlockSpec((None, TQ, 1), lambda h, i: (h, i, 0)), blk],
        out_specs=[blk, col, col], out_shape=[shp] * 3,
        scratch_shapes=[pltpu.VMEM((s, HEAD), F32), pltpu.VMEM((s, HEAD), F32)],
        compiler_params=_params(("parallel", "arbitrary")))(proj, proj, proj, lt, dya)


def _attn_b_bwd(proj, dyb, qg, kg, ext, nh):
    s = proj.shape[0]
    nq = s // TQ

    def body(q_ref, k_ref, v_ref, do_ref, qg_ref, kg_ref, e_ref,
             dq_ref, dk_ref, dv_ref, dqg_ref, dkg_ref, de_ref,
             kn_pad, v_pad, bias_ref, db_acc, dkn_acc, dv_acc):
        i = pl.program_id(1)

        @pl.when(i == 0)
        def _():
            _band_fill(k_ref, v_ref, kg_ref, kn_pad, v_pad, s)
            _build_bias(e_ref, bias_ref)
            db_acc[...] = jnp.zeros_like(db_acc)
            dkn_acc[...] = jnp.zeros_like(dkn_acc)
            dv_acc[...] = jnp.zeros_like(dv_acc)
            dqg_ref[...] = jnp.zeros_like(dqg_ref)

        rq, qhat, qn, kw, off, pn = _band_probs(q_ref, qg_ref, kn_pad, bias_ref, i)
        dob = do_ref[...].astype(BF16)
        dp = _dot_nt(dob, v_pad[pl.ds(off, WIN), :])
        dsc = pn * (dp - jnp.sum(pn * dp, axis=1, keepdims=True))
        db_acc[:, :WIN] += dsc
        dsb = (dsc * SCALE).astype(BF16)
        dqn = _dot(dsb, kw)
        dkn_acc[pl.ds(off, WIN), :] += _dot_tn(dsb, qn)
        dv_acc[pl.ds(off, WIN), :] += _dot_tn(pn.astype(BF16), dob)
        dqh = dqn * qg_ref[...]
        dq_ref[...] = (rq * (dqh - qhat * jnp.mean(dqh * qhat, axis=1, keepdims=True))).astype(BF16)
        dqg_ref[...] += jnp.sum(dqn * qhat, axis=0, keepdims=True)

        @pl.when(i == nq - 1)
        def _():
            k = k_ref[...]
            rk = lax.rsqrt(jnp.mean(k * k, axis=1, keepdims=True) + NORM_EPS)
            khat = k * rk
            dkn = dkn_acc[pl.ds(PAD, s), :]
            dkh = dkn * kg_ref[...]
            dk_ref[...] = (rk * (dkh - khat * jnp.mean(dkh * khat, axis=1, keepdims=True))).astype(BF16)
            dkg_ref[...] = jnp.sum(dkn * khat, axis=0, keepdims=True)
            dv_ref[...] = dv_acc[pl.ds(PAD, s), :].astype(BF16)
            de_ref[...] = _reduce_bias_grad(db_acc)

    blk = pl.BlockSpec((TQ, HEAD), lambda h, i: (i, h))
    col = pl.BlockSpec((s, HEAD), lambda h, i: (0, h))
    vec = pl.BlockSpec((1, HEAD), lambda h, i: (0, 0))
    hvec = pl.BlockSpec((None, 1, HEAD), lambda h, i: (h, 0, 0))
    hext = pl.BlockSpec((None, 1, EXT), lambda h, i: (h, 0, 0))
    shp = jax.ShapeDtypeStruct((s, nh * HEAD), BF16)
    return pl.pallas_call(
        body, name="attn_b_bwd", grid=(nh, nq),
        in_specs=[pl.BlockSpec((TQ, HEAD), lambda h, i: (i, 4 * nh + h)),
                  pl.BlockSpec((s, HEAD), lambda h, i: (0, 5 * nh + h)),
                  pl.BlockSpec((s, HEAD), lambda h, i: (0, 6 * nh + h)),
                  blk, vec, vec, hext],
        out_specs=[blk, col, col, hvec, hvec, hext],
        out_shape=[shp] * 3 + [jax.ShapeDtypeStruct((nh, 1, HEAD), F32)] * 2
        + [jax.ShapeDtypeStruct((nh, 1, EXT), F32)],
        scratch_shapes=[pltpu.VMEM((s + PAD, HEAD), BF16), pltpu.VMEM((s + PAD, HEAD), BF16),
                        pltpu.VMEM((TQ, WIN), F32), pltpu.VMEM((TQ, EXT), F32),
                        pltpu.VMEM((s + PAD, HEAD), F32), pltpu.VMEM((s + PAD, HEAD), F32)],
        compiler_params=_params(("parallel", "arbitrary")))(proj, proj, proj, dyb, qg, kg, ext)


def _in_proj_bwd(dproj, w, x, dxo, g):
    s, d = x.shape
    nblk, _, nb = w.shape
    tm, tk = min(s, 256), min(nb, 1024)
    per = nb // tk
    nk = nblk * per

    def body(dp_ref, w_ref, x_ref, dxo_ref, g_ref, dx_ref, dg_ref, acc):
        m, k = pl.program_id(0), pl.program_id(1)

        @pl.when(k == 0)
        def _():
            acc[...] = jnp.zeros_like(acc)

        @pl.when((k == 0) & (m == 0))
        def _():
            dg_ref[...] = jnp.zeros_like(dg_ref)

        acc[...] += _dot_nt(dp_ref[...], w_ref[...])

        @pl.when(k == nk - 1)
        def _():
            xv = x_ref[...]
            r = lax.rsqrt(jnp.mean(xv * xv, axis=1, keepdims=True) + NORM_EPS)
            xhat = xv * r
            dh = acc[...]
            dxh = dh * g_ref[...]
            dx_ref[...] = dxo_ref[...] + r * (dxh - xhat * jnp.mean(dxh * xhat, axis=1, keepdims=True))
            dg_ref[...] += jnp.sum(dh * xhat, axis=0, keepdims=True)

    row = pl.BlockSpec((tm, d), lambda m, k: (m, 0))
    return pl.pallas_call(
        body, name="in_proj_bwd", grid=(s // tm, nk),
        in_specs=[pl.BlockSpec((tm, tk), lambda m, k: (m, k)),
                  pl.BlockSpec((None, d, tk), lambda m, k: (k // per, 0, k % per)),
                  row, row, pl.BlockSpec((1, d), lambda m, k: (0, 0))],
        out_specs=[row, pl.BlockSpec((8, d), lambda m, k: (0, 0))],
        out_shape=[jax.ShapeDtypeStruct((s, d), F32), jax.ShapeDtypeStruct((8, d), F32)],
        scratch_shapes=[pltpu.VMEM((tm, d), F32)],
        compiler_params=_params(("arbitrary", "arbitrary"), 48))(dproj, w, x, dxo, g)


def _place():
    x, y, c = lax.axis_index("x"), lax.axis_index("y"), lax.axis_index("c")
    chips = [(1 - x, y), (x, 1 - y), (1 - x, 1 - y)]
    return x, y, c, chips


def _comm_call(body, name, ins, out_shape, n_remote, n_local):
    return pl.pallas_call(
        body, name=name, in_specs=[ANY] * len(ins), out_specs=[ANY] * len(out_shape), out_shape=out_shape,
        scratch_shapes=[pltpu.SemaphoreType.DMA((n_remote,)), pltpu.SemaphoreType.DMA((n_remote,)),
                        pltpu.SemaphoreType.DMA((n_local,))])(*ins)


def _rcopy(src, dst, send_sems, recv_sems, k, dev):
    return pltpu.make_async_remote_copy(src_ref=src, dst_ref=dst, send_sem=send_sems.at[k], recv_sem=recv_sems.at[k],
                                        device_id=dev, device_id_type=MESH)


def _gather_weights(wib, wob, rbp):
    nl = wib.shape[0]
    n_ici = nl * 2 * 3

    def body(wi_ref, wo_ref, rb_ref, *rest):
        wi_full, wo_full, rb_full = rest[:nl], rest[nl:2 * nl], rest[2 * nl]
        send_sems, recv_sems, loc_sems = rest[2 * nl + 1:]
        x, y, c, chips = _place()
        me, sib, b = (x, y, c), (x, y, 1 - c), 2 * x + y
        pairs = []
        for l in range(nl):
            pairs += [(wi_ref.at[l], wi_full[l]), (wo_ref.at[l], wo_full[l])]
        local = [pltpu.make_async_copy(src, full.at[b], loc_sems.at[t]) for t, (src, full) in enumerate(pairs)]
        local.append(pltpu.make_async_copy(rb_ref, rb_full.at[b], loc_sems.at[len(pairs)]))
        for cp in local:
            cp.start()

        def half(full, blk, core):
            hr = full.shape[1] // 2
            return full.at[blk].at[pl.ds(core * hr, hr)]

        sent = []
        for t, (src, full) in enumerate(pairs):
            hr = src.shape[0] // 2
            for j, chip in enumerate(chips):
                cp = _rcopy(src.at[pl.ds(c * hr, hr)], half(full, b, c), send_sems, recv_sems, 3 * t + j, (*chip, c))
                cp.start()
                sent.append(cp)
        for j, chip in enumerate(chips):
            cp = _rcopy(rb_ref, rb_full.at[b], send_sems, recv_sems, 2 * n_ici + j, (*chip, c))
            cp.start()
            sent.append(cp)
        for t, (src, full) in enumerate(pairs):
            for j, chip in enumerate(chips):
                got = half(full, 2 * chip[0] + chip[1], c)
                _rcopy(got, got, send_sems, recv_sems, 3 * t + j, me).wait_recv()
                cp = _rcopy(got, got, send_sems, recv_sems, n_ici + 3 * t + j, sib)
                cp.start()
                sent.append(cp)
        for t, (src, full) in enumerate(pairs):
            for j, chip in enumerate(chips):
                got = half(full, 2 * chip[0] + chip[1], 1 - c)
                _rcopy(got, got, send_sems, recv_sems, n_ici + 3 * t + j, me).wait_recv()
        for j, chip in enumerate(chips):
            got = rb_full.at[2 * chip[0] + chip[1]]
            _rcopy(got, got, send_sems, recv_sems, 2 * n_ici + j, me).wait_recv()
        for cp in sent:
            cp.wait_send()
        for cp in local:
            cp.wait()

    out_shape = ([jax.ShapeDtypeStruct((4,) + wib.shape[1:], BF16)] * nl
                 + [jax.ShapeDtypeStruct((4,) + wob.shape[1:], BF16)] * nl
                 + [jax.ShapeDtypeStruct((4,) + rbp.shape, F32)])
    out = _comm_call(body, "gather_weights", [wib, wob, rbp], out_shape, 2 * n_ici + 3, 2 * nl + 1)
    return out[:nl], out[nl:2 * nl], out[2 * nl]


def _reduce_sibling(parts):
    n = len(parts)

    def body(*refs):
        ins, mine, theirs = refs[:n], refs[n:2 * n], refs[2 * n:3 * n]
        send_sems, recv_sems, loc_sems = refs[3 * n:]
        x, y, c, _ = _place()
        sib = (x, y, 1 - c)
        local, sent = [], []
        for t in range(n):
            hr = ins[t].shape[1] // 2
            cp = pltpu.make_async_copy(ins[t].at[:, pl.ds(c * hr, hr), :], mine[t], loc_sems.at[t])
            cp.start()
            local.append(cp)
            cp = _rcopy(ins[t].at[:, pl.ds((1 - c) * hr, hr), :], theirs[t], send_sems, recv_sems, t, sib)
            cp.start()
            sent.append(cp)
        for cp in sent:
            cp.wait()
        for cp in local:
            cp.wait()

    half = [jax.ShapeDtypeStruct((p.shape[0], p.shape[1] // 2, p.shape[2]), p.dtype) for p in parts]
    out = _comm_call(body, "reduce_sibling", parts, half + half, n, n)
    return out[:n], out[n:]


def _reduce_chips(sums):
    n = len(sums)

    def body(*refs):
        ins, own, got = refs[:n], refs[n:2 * n], refs[2 * n:3 * n]
        send_sems, recv_sems, loc_sems = refs[3 * n:]
        x, y, c, chips = _place()
        local, sent = [], []
        for t in range(n):
            cp = pltpu.make_async_copy(ins[t].at[2 * x + y], own[t], loc_sems.at[t])
            cp.start()
            local.append(cp)
            for j, chip in enumerate(chips):
                cp = _rcopy(ins[t].at[2 * chip[0] + chip[1]], got[t].at[j], send_sems, recv_sems, 3 * t + j, (*chip, c))
                cp.start()
                sent.append(cp)
        for cp in sent:
            cp.wait()
        for cp in local:
            cp.wait()

    own = [jax.ShapeDtypeStruct(p.shape[1:], p.dtype) for p in sums]
    got = [jax.ShapeDtypeStruct((3,) + p.shape[1:], p.dtype) for p in sums]
    out = _comm_call(body, "reduce_chips", sums, own + got, 3 * n, n)
    return out[:n], out[n:]


def _share_halves(tots, nl):
    n = len(tots)

    def body(*refs):
        ins, g_in, g_out = refs[:n], refs[n], refs[n + 1]
        send_sems, recv_sems, loc_sems = refs[n + 2:]
        x, y, c, _ = _place()
        sib = (x, y, 1 - c)
        local, sent = [], []
        for t in range(n):
            dst = (g_in if t < nl else g_out).at[t % nl].at[c]
            cp = pltpu.make_async_copy(ins[t], dst, loc_sems.at[t])
            cp.start()
            local.append(cp)
            cp = _rcopy(ins[t], dst, send_sems, recv_sems, t, sib)
            cp.start()
            sent.append(cp)
        for cp in sent:
            cp.wait()
        for cp in local:
            cp.wait()

    out_shape = [jax.ShapeDtypeStruct((nl, 2) + tots[0].shape, F32), jax.ShapeDtypeStruct((nl, 2) + tots[nl].shape, F32)]
    return _comm_call(body, "share_halves", tots, out_shape, n, n)


def _gather_small(packed):
    def body(p_ref, all_ref, send_sems, recv_sems, loc_sems):
        x, y, c, _ = _place()
        me = 4 * x + 2 * y + c
        local = pltpu.make_async_copy(p_ref, all_ref.at[me], loc_sems.at[0])
        local.start()
        sent = []
        for k in range(1, 8):
            px, py, pc = x ^ (k >> 2), y ^ ((k >> 1) & 1), c ^ (k & 1)
            cp = _rcopy(p_ref, all_ref.at[me], send_sems, recv_sems, k - 1, (px, py, pc))
            cp.start()
            sent.append(cp)
        for k in range(1, 8):
            px, py, pc = x ^ (k >> 2), y ^ ((k >> 1) & 1), c ^ (k & 1)
            got = all_ref.at[4 * px + 2 * py + pc]
            _rcopy(got, got, send_sems, recv_sems, k - 1, (x, y, c)).wait_recv()
        for cp in sent:
            cp.wait_send()
        local.wait()

    return _comm_call(body, "gather_small", [packed], [jax.ShapeDtypeStruct((8,) + packed.shape, F32)], 7, 1)[0]


def _sum_devices(allp):
    n, r, c = allp.shape

    def body(a_ref, o_ref):
        acc = a_ref[0]
        for k in range(1, n):
            acc = acc + a_ref[k]
        o_ref[...] = acc

    return pl.pallas_call(body, name="sum_devices", out_shape=jax.ShapeDtypeStruct((r, c), F32))(allp)


def _ext_index():
    u = np.arange(EXT)
    dist = np.where(u < WIN, PAD - u, PAD + EXT - u)
    return np.clip(dist, -(CHUNK - 1), REL_CLIP) + (CHUNK - 1)


def _pack(parts, rows):
    flat = jnp.concatenate([p.reshape(-1) for p in parts])
    return jnp.pad(flat, (0, rows * 128 - flat.shape[0])).reshape(rows, 128)


def _unpack(packed, shapes):
    flat, out, at = packed.reshape(-1), [], 0
    for shp in shapes:
        size = int(np.prod(shp))
        out.append(flat[at:at + size].reshape(shp))
        at += size
    return out


def kernel(x, norm_g, w_in, q_norm_g, k_norm_g, rel_bias, w_out, loss_target, m_norm_g, m_w_in, m_q_norm_g, m_k_norm_g, m_rel_bias, m_w_out, v_norm_g, v_w_in, v_q_norm_g, v_k_norm_g, v_rel_bias, v_w_out):
    nl, d, nb = w_in.shape
    s = x.shape[1]
    ds_ = d // 2
    nh = ds_ // HEAD
    rb = w_out.shape[1]
    nrel = rel_bias.shape[2]
    bx = lax.axis_index("x") * 2 + lax.axis_index("y")

    rb_rows = -(-(nl * nh * nrel) // 1024) * 8
    wi_full, wo_full, rel_all = _gather_weights(
        _cast_bf16(w_in, "cast_w_in"), _cast_bf16(w_out, "cast_w_out"), _pack([rel_bias], rb_rows))
    rel_full = jnp.concatenate(
        [rel_all[j].reshape(-1)[:nl * nh * nrel].reshape(nl, nh, nrel) for j in range(4)], axis=2)
    ext_idx = _ext_index()
    onehot = jnp.asarray(ext_idx[:, None] == np.arange(N_REL)[None, :], F32)
    ext = jnp.einsum("lhr,ur->lhu", rel_full, onehot, precision=lax.Precision.HIGHEST).reshape(nl, nh, 1, EXT)

    xs, hs, projs, yas, lts, ybs, mixes = [], [], [], [], [], [], []
    xc = x[0]
    for l in range(nl):
        h = _rmsnorm_fwd(xc, norm_g[l:l + 1])
        proj = _in_proj(h, wi_full[l])
        ya, lt = _attn_a_fwd(proj, nh)
        yb = _attn_b_fwd(proj, q_norm_g[l:l + 1], k_norm_g[l:l + 1], ext[l], nh)
        xs.append(xc)
        xc, mix = _out_proj(xc, ya, yb, proj, wo_full[l].reshape(4 * rb, d))
        hs.append(h); projs.append(proj); yas.append(ya); lts.append(lt); ybs.append(yb); mixes.append(mix)
    dx, loss_tile = _loss_head(xc, loss_target[0])

    p_wi, p_wo, small = [None] * nl, [None] * nl, [None] * nl
    for l in reversed(range(nl)):
        wo = wo_full[l].reshape(4 * rb, d)
        p_wo[l] = _wgrad(mixes[l], dx, 4, False, "wgrad_out")
        dya, dyb, dga, dgb = _out_proj_bwd(dx, yas[l], ybs[l], projs[l], wo)
        dqa, dka, dva = _attn_a_bwd(projs[l], lts[l], dya, nh)
        dqb, dkb, dvb, dqg, dkg, dext = _attn_b_bwd(projs[l], dyb, q_norm_g[l:l + 1], k_norm_g[l:l + 1], ext[l], nh)
        dproj = jnp.concatenate([dqa, dka, dva, dga, dqb, dkb, dvb, dgb], axis=1)
        p_wi[l] = _wgrad(hs[l], dproj, 4, True, "wgrad_in")
        dx, dng = _in_proj_bwd(dproj, wi_full[l], xs[l], dx, norm_g[l:l + 1])
        small[l] = (dng[0], jnp.sum(dqg, axis=0).reshape(-1), jnp.sum(dkg, axis=0).reshape(-1), dext.reshape(nh, EXT))
    grad_x = dx[None]

    parts = p_wi + p_wo
    mine, theirs = _reduce_sibling(parts)
    sums = []
    for t in range(2 * nl):
        shp = mine[t].shape
        flat = _add_rows([mine[t].reshape(-1, shp[2]), theirs[t].reshape(-1, shp[2])], BF16, "add_sibling")
        sums.append(flat.reshape(shp))
    own, got = _reduce_chips(sums)
    tots = [_add_rows([own[t], got[t][0], got[t][1], got[t][2]], F32, "add_chips") for t in range(2 * nl)]
    g_wi, g_wo = _share_halves(tots, nl)
    g_wi, g_wo = g_wi.reshape(nl, d, nb), g_wo.reshape(nl, rb, d)

    small_shapes = [(nl, d), (nl, HEAD), (nl, HEAD), (nl, nh, EXT), (1,)]
    small_parts = [jnp.stack([sm[i] for sm in small]) for i in range(4)] + [loss_tile[0, :1]]
    rows = -(-sum(int(np.prod(sh)) for sh in small_shapes) // 1024) * 8
    tot = _sum_devices(_gather_small(_pack(small_parts, rows)))
    g_ng, g_qg, g_kg, g_ext, loss = _unpack(tot, small_shapes)
    g_rel_full = jnp.einsum("lhu,ur->lhr", g_ext, onehot, precision=lax.Precision.HIGHEST)
    g_rel = lax.dynamic_slice_in_dim(g_rel_full, bx * nrel, nrel, axis=2)

    d_wi, nm_wi, nv_wi = _adamw(w_in, g_wi, m_w_in, v_w_in, "adamw_w_in")
    d_wo, nm_wo, nv_wo = _adamw(w_out, g_wo, m_w_out, v_w_out, "adamw_w_out")
    sm_shapes = [(nl, d), (nl, HEAD), (nl, HEAD), (nl, nh, nrel)]
    sm_rows = -(-sum(int(np.prod(sh)) for sh in sm_shapes) // 1024) * 8
    packs = [_pack(group, sm_rows)[None] for group in (
        (norm_g, q_norm_g, k_norm_g, rel_bias), (g_ng, g_qg, g_kg, g_rel),
        (m_norm_g, m_q_norm_g, m_k_norm_g, m_rel_bias), (v_norm_g, v_q_norm_g, v_k_norm_g, v_rel_bias))]
    d_sm, nm_sm, nv_sm = [_unpack(a[0], sm_shapes) for a in _adamw(*packs, "adamw_small")]

    return (loss[0], grad_x, g_ng, g_wi, g_qg, g_kg, g_rel, g_wo,
            d_sm[0], d_wi, d_sm[1], d_sm[2], d_sm[3], d_wo,
            nm_sm[0], nm_wi, nm_sm[1], nm_sm[2], nm_sm[3], nm_wo,
            nv_sm[0], nv_wi, nv_sm[1], nv_sm[2], nv_sm[3], nv_wo)
```

```python
import functools

import jax
import jax.numpy as jnp
import numpy as np
from jax import lax
from jax.experimental import pallas as pl
from jax.experimental.pallas import tpu as pltpu

F32 = jnp.float32
BF16 = jnp.bfloat16

HEAD = 128
CHUNK = 64
LEFT_CHUNKS = 8
REL_CLIP = 256
N_REL = REL_CLIP + CHUNK
NORM_EPS = 1e-6
NEG_BIG = -1e30
TQ = 256
PAD = LEFT_CHUNKS * CHUNK
WIN = PAD + TQ
EXT = 1024
SCALE = HEAD ** -0.5

ADAM_LR = 0.001
ADAM_B1 = 0.9
ADAM_B2 = 0.999
ADAM_EPS = 1e-08
ADAM_WD = 0.01
ADAM_STEP = 10

ANY = pl.BlockSpec(memory_space=pl.ANY)
MESH = pl.DeviceIdType.MESH


def _params(sem=None, vmem_mb=None):
    kw = {}
    if sem is not None:
        kw["dimension_semantics"] = sem
    if vmem_mb is not None:
        kw["vmem_limit_bytes"] = vmem_mb << 20
    return pltpu.CompilerParams(**kw)


def _dot(a, b):
    return jnp.dot(a, b, preferred_element_type=F32)


def _dot_nt(a, b):
    return lax.dot_general(a, b, (((1,), (1,)), ((), ())), preferred_element_type=F32)


def _dot_tn(a, b):
    return lax.dot_general(a, b, (((0,), (0,)), ((), ())), preferred_element_type=F32)


def _split_dot(x, m):
    hi = x.astype(BF16)
    lo = (x - hi.astype(F32)).astype(BF16)
    return _dot(hi, m) + _dot(lo, m)


def _silu_parts(g):
    sg = 1.0 / (1.0 + jnp.exp(-g))
    return g * sg, sg * (1.0 + g * (1.0 - sg))


def _idx(*vals):
    return jnp.stack([jnp.asarray(v, jnp.int32) for v in vals])


def _cast_block(w, l, blk, name):
    _, r, c = w.shape
    tr = min(r, 512)

    def body(b_ref, w_ref, o_ref):
        o_ref[...] = w_ref[...].astype(BF16)

    spec = pltpu.PrefetchScalarGridSpec(
        num_scalar_prefetch=1, grid=(r // tr,),
        in_specs=[pl.BlockSpec((None, tr, c), lambda i, b: (l, i, 0))],
        out_specs=pl.BlockSpec((None, tr, c), lambda i, b: (b[0], i, 0)))
    return pl.pallas_call(body, name=name, grid_spec=spec, out_shape=jax.ShapeDtypeStruct((4, r, c), BF16),
                          compiler_params=_params(("parallel",)))(_idx(blk), w)


def _add_sibling(p, theirs, core):
    nblk, r, c = p.shape
    hr = r // 2
    tr = min(hr, 256)
    per = hr // tr

    def body(c_ref, p_ref, t_ref, o_ref):
        o_ref[...] = (p_ref[...].astype(F32) + t_ref[...].astype(F32)).astype(BF16)

    blk = pl.BlockSpec((None, tr, c), lambda j, i, cr: (j, i, 0))
    spec = pltpu.PrefetchScalarGridSpec(
        num_scalar_prefetch=1, grid=(nblk, per),
        in_specs=[pl.BlockSpec((None, tr, c), lambda j, i, cr: (j, cr[0] * per + i, 0)), blk], out_specs=blk)
    return pl.pallas_call(body, name="add_sibling", grid_spec=spec, out_shape=jax.ShapeDtypeStruct((nblk, hr, c), BF16),
                          compiler_params=_params(("parallel", "parallel")))(_idx(core), p, theirs)


def _add_chips(sums, got, blk, core):
    _, hr, c = sums.shape
    tr = min(hr, 256)

    def body(i_ref, s_ref, g0_ref, g1_ref, g2_ref, o_ref):
        o_ref[...] = ((s_ref[...].astype(F32) + g0_ref[...].astype(F32))
                      + g1_ref[...].astype(F32)) + g2_ref[...].astype(F32)

    at = lambda j: pl.BlockSpec((None, tr, c), lambda i, ir: (j, i, 0))
    spec = pltpu.PrefetchScalarGridSpec(
        num_scalar_prefetch=1, grid=(hr // tr,),
        in_specs=[pl.BlockSpec((None, tr, c), lambda i, ir: (ir[0], i, 0)), at(0), at(1), at(2)],
        out_specs=pl.BlockSpec((None, tr, c), lambda i, ir: (ir[1], i, 0)))
    return pl.pallas_call(body, name="add_chips", grid_spec=spec, out_shape=jax.ShapeDtypeStruct((2, hr, c), F32),
                          compiler_params=_params(("parallel",)))(_idx(blk, core), sums, got, got, got)


def _adamw_layer(l, w, g, m, v, prev, name):
    nl, r, c = w.shape
    tr = min(r, 256)
    c1 = 1.0 / (1.0 - ADAM_B1 ** ADAM_STEP)
    c2 = 1.0 / (1.0 - ADAM_B2 ** ADAM_STEP)

    def body(w_ref, g_ref, m_ref, v_ref, *rest):
        go_ref, d_ref, nm_ref, nv_ref = rest[-4:]
        gg = g_ref[...]
        nm = ADAM_B1 * m_ref[...] + (1.0 - ADAM_B1) * gg
        nv = ADAM_B2 * v_ref[...] + (1.0 - ADAM_B2) * (gg * gg)
        upd = (nm * c1) / (jnp.sqrt(nv * c2) + ADAM_EPS) + ADAM_WD * w_ref[...]
        go_ref[...] = gg
        d_ref[...] = -ADAM_LR * upd
        nm_ref[...] = nm
        nv_ref[...] = nv

    lay = pl.BlockSpec((None, tr, c), lambda i: (l, i, 0))
    shp = jax.ShapeDtypeStruct((nl, r, c), F32)
    return pl.pallas_call(
        body, name=name, grid=(r // tr,),
        in_specs=[lay, pl.BlockSpec((tr, c), lambda i: (i, 0)), lay, lay] + [ANY] * len(prev),
        out_specs=[lay] * 4, out_shape=[shp] * 4, input_output_aliases={4 + k: k for k in range(len(prev))},
        compiler_params=_params(("parallel",), 40))(w, g, m, v, *prev)


def _rmsnorm_fwd(x, g):
    s, d = x.shape
    tm = min(s, 256)

    def body(x_ref, g_ref, h_ref):
        xv = x_ref[...]
        r = lax.rsqrt(jnp.mean(xv * xv, axis=1, keepdims=True) + NORM_EPS)
        h_ref[...] = (xv * r * g_ref[...]).astype(BF16)

    return pl.pallas_call(
        body, name="rmsnorm_fwd", grid=(s // tm,),
        in_specs=[pl.BlockSpec((tm, d), lambda i: (i, 0)), pl.BlockSpec((1, d), lambda i: (0, 0))],
        out_specs=pl.BlockSpec((tm, d), lambda i: (i, 0)),
        out_shape=jax.ShapeDtypeStruct((s, d), BF16),
        compiler_params=_params(("parallel",)))(x, g)


def _in_proj(h, w):
    s, d = h.shape
    nblk, _, nb = w.shape
    tm, tn = min(s, 512), min(nb, 1024)
    per = nb // tn

    def body(h_ref, w_ref, o_ref):
        o_ref[...] = _dot(h_ref[...], w_ref[...])

    return pl.pallas_call(
        body, name="in_proj", grid=(nblk * per, s // tm),
        in_specs=[pl.BlockSpec((tm, d), lambda n, m: (m, 0)),
                  pl.BlockSpec((None, d, tn), lambda n, m: (n // per, 0, n % per))],
        out_specs=pl.BlockSpec((tm, tn), lambda n, m: (m, n)),
        out_shape=jax.ShapeDtypeStruct((s, nblk * nb), F32),
        compiler_params=_params(("parallel", "parallel"), 40))(h, w)


def _heads_per_step(nh):
    return 2 if nh % 2 == 0 else 1


def _head(hh):
    return slice(hh * HEAD, (hh + 1) * HEAD)


def _tri(op):
    r = lax.broadcasted_iota(jnp.int32, (TQ, TQ), 0)
    c = lax.broadcasted_iota(jnp.int32, (TQ, TQ), 1)
    return op(r, c)


def _sb_logs(qb, kb, causal, diag):
    z = _dot_nt(qb, kb) * SCALE
    e = jnp.exp(-jnp.abs(z))
    l1p = jnp.log(1.0 + e)
    ls = jnp.minimum(-z, 0.0) - l1p
    lsig = jnp.minimum(z, 0.0) - l1p
    if diag:
        ls = jnp.where(causal, ls, 0.0)
    return z, e, ls, lsig


def _attn_a_fwd(proj, nh):
    s = proj.shape[0]
    nq = s // TQ
    hp = _heads_per_step(nh)
    ng = nh // hp

    def body(q_ref, k_ref, v_ref, o_ref, lt_ref, acc_ref):
        i = pl.program_id(1)
        qbs = [q_ref[:, _head(hh)].astype(BF16) for hh in range(hp)]
        causal = _tri(lambda r, c: r > c)
        m_after = causal.astype(BF16)
        acc_ref[...] = jnp.zeros_like(acc_ref)

        def tile(j, carries, diag):
            off = pl.multiple_of(j * TQ, TQ)
            out = []
            for hh in range(hp):
                kb = k_ref[pl.ds(off, TQ), _head(hh)].astype(BF16)
                vb = v_ref[pl.ds(off, TQ), _head(hh)].astype(BF16)
                _, _, ls, lsig = _sb_logs(qbs[hh], kb, causal, diag)
                w = jnp.exp(lsig + _split_dot(ls, m_after) + carries[hh])
                if diag:
                    w = jnp.where(causal, w, 0.0)
                acc_ref[:, _head(hh)] += _dot(w.astype(BF16), vb)
                out.append(carries[hh] + jnp.sum(ls, axis=1, keepdims=True))
            return tuple(out)

        carries = tile(i, (jnp.zeros((TQ, 1), F32),) * hp, True)
        carries = lax.fori_loop(0, i, lambda t, ca: tile(i - 1 - t, ca, False), carries)
        o_ref[...] = acc_ref[...]
        for hh in range(hp):
            lt_ref[hh] = carries[hh]

    wd = hp * HEAD
    return pl.pallas_call(
        body, name="attn_a_fwd", grid=(ng, nq),
        in_specs=[pl.BlockSpec((TQ, wd), lambda h, i: (i, h)),
                  pl.BlockSpec((s, wd), lambda h, i: (0, ng + h)),
                  pl.BlockSpec((s, wd), lambda h, i: (0, 2 * ng + h))],
        out_specs=[pl.BlockSpec((TQ, wd), lambda h, i: (i, h)),
                   pl.BlockSpec((hp, TQ, 1), lambda h, i: (h, i, 0))],
        out_shape=[jax.ShapeDtypeStruct((s, nh * HEAD), F32), jax.ShapeDtypeStruct((nh, s, 1), F32)],
        scratch_shapes=[pltpu.VMEM((TQ, wd), F32)],
        compiler_params=_params(("parallel", "arbitrary")))(proj, proj, proj)


def _band_valid(i):
    cl = lax.broadcasted_iota(jnp.int32, (TQ, WIN), 0) // CHUNK
    kl = lax.broadcasted_iota(jnp.int32, (TQ, WIN), 1) // CHUNK
    first = LEFT_CHUNKS - (TQ // CHUNK) * i
    return (kl >= cl) & (kl <= cl + LEFT_CHUNKS) & (kl >= first)


def _build_bias(e_ref, bias_ref):
    e8 = jnp.broadcast_to(e_ref[...], (8, EXT))
    row = lax.broadcasted_iota(jnp.int32, (8, EXT), 0)
    t8 = jnp.zeros((8, EXT), F32)
    for b in range(8):
        t8 = jnp.where(row == b, pltpu.roll(e8, b, 1) if b else e8, t8)
    for a in range(TQ // 8):
        sl = pltpu.roll(t8, 8 * a, 1) if a else t8
        bias_ref[pl.ds(8 * a, 8), :] = sl[:, :WIN]


def _reduce_bias_grad(db_ref):
    acc = jnp.zeros((8, EXT), F32)
    for a in range(TQ // 8):
        sl = db_ref[pl.ds(8 * a, 8), :]
        acc = acc + (pltpu.roll(sl, EXT - 8 * a, 1) if a else sl)
    row = lax.broadcasted_iota(jnp.int32, (8, EXT), 0)
    tot = jnp.zeros((8, EXT), F32)
    for b in range(8):
        tot = tot + jnp.where(row == b, pltpu.roll(acc, EXT - b, 1) if b else acc, 0.0)
    return jnp.sum(tot, axis=0, keepdims=True)


def _band_fill(k_ref, v_ref, kg_ref, kn_pad, v_pad, s):
    k = k_ref[...]
    rk = lax.rsqrt(jnp.mean(k * k, axis=1, keepdims=True) + NORM_EPS)
    kn_pad[pl.ds(0, PAD), :] = jnp.zeros((PAD, HEAD), BF16)
    kn_pad[pl.ds(PAD, s), :] = (k * rk * kg_ref[...]).astype(BF16)
    v_pad[pl.ds(0, PAD), :] = jnp.zeros((PAD, HEAD), BF16)
    v_pad[pl.ds(PAD, s), :] = v_ref[...].astype(BF16)


def _band_probs(q_ref, qg_ref, kn_pad, bias_ref, i):
    q = q_ref[...]
    rq = lax.rsqrt(jnp.mean(q * q, axis=1, keepdims=True) + NORM_EPS)
    qhat = q * rq
    qn = (qhat * qg_ref[...]).astype(BF16)
    off = pl.multiple_of(i * TQ, TQ)
    kw = kn_pad[pl.ds(off, WIN), :]
    sc = _dot_nt(qn, kw) * SCALE + bias_ref[...]
    sc = jnp.where(_band_valid(i), sc, NEG_BIG)
    p = jnp.exp(sc - jnp.max(sc, axis=1, keepdims=True))
    pn = p / jnp.sum(p, axis=1, keepdims=True)
    return rq, qhat, qn, kw, off, pn


def _attn_b_fwd(proj, qg, kg, ext, nh):
    s = proj.shape[0]
    nq = s // TQ

    def body(q_ref, k_ref, v_ref, qg_ref, kg_ref, e_ref, o_ref, kn_pad, v_pad, bias_ref):
        i = pl.program_id(1)

        @pl.when(i == 0)
        def _():
            _band_fill(k_ref, v_ref, kg_ref, kn_pad, v_pad, s)
            _build_bias(e_ref, bias_ref)

        _, _, _, _, off, pn = _band_probs(q_ref, qg_ref, kn_pad, bias_ref, i)
        o_ref[...] = _dot(pn.astype(BF16), v_pad[pl.ds(off, WIN), :])

    vec = pl.BlockSpec((1, HEAD), lambda h, i: (0, 0))
    return pl.pallas_call(
        body, name="attn_b_fwd", grid=(nh, nq),
        in_specs=[pl.BlockSpec((TQ, HEAD), lambda h, i: (i, 4 * nh + h)),
                  pl.BlockSpec((s, HEAD), lambda h, i: (0, 5 * nh + h)),
                  pl.BlockSpec((s, HEAD), lambda h, i: (0, 6 * nh + h)),
                  vec, vec,
                  pl.BlockSpec((None, 1, EXT), lambda h, i: (h, 0, 0))],
        out_specs=pl.BlockSpec((TQ, HEAD), lambda h, i: (i, h)),
        out_shape=jax.ShapeDtypeStruct((s, nh * HEAD), F32),
        scratch_shapes=[pltpu.VMEM((s + PAD, HEAD), BF16), pltpu.VMEM((s + PAD, HEAD), BF16),
                        pltpu.VMEM((TQ, WIN), F32)],
        compiler_params=_params(("parallel", "arbitrary")))(proj, proj, proj, qg, kg, ext)


def _out_proj(x, ya, yb, proj, w):
    s, d = x.shape
    ds_ = ya.shape[1]
    tm = min(s, 256)

    def body(x_ref, ya_ref, yb_ref, ga_ref, gb_ref, w_ref, o_ref, mix_ref):
        ma = (ya_ref[...] * _silu_parts(ga_ref[...])[0]).astype(BF16)
        mb = (yb_ref[...] * _silu_parts(gb_ref[...])[0]).astype(BF16)
        mix_ref[:, :ds_] = ma
        mix_ref[:, ds_:] = mb
        o_ref[...] = x_ref[...] + _dot(ma, w_ref[pl.ds(0, ds_), :]) + _dot(mb, w_ref[pl.ds(ds_, ds_), :])

    row = lambda width: pl.BlockSpec((tm, width), lambda i: (i, 0))
    return pl.pallas_call(
        body, name="out_proj", grid=(s // tm,),
        in_specs=[row(d), row(ds_), row(ds_),
                  pl.BlockSpec((tm, ds_), lambda i: (i, 3)), pl.BlockSpec((tm, ds_), lambda i: (i, 7)),
                  pl.BlockSpec((2 * ds_, d), lambda i: (0, 0))],
        out_specs=[row(d), row(2 * ds_)],
        out_shape=[jax.ShapeDtypeStruct((s, d), F32), jax.ShapeDtypeStruct((s, 2 * ds_), BF16)],
        compiler_params=_params(("parallel",), 48))(x, ya, yb, proj, proj, w)


def _loss_head(y, tgt):
    s, d = y.shape
    tm = min(s, 256)

    def body(y_ref, t_ref, dy_ref, l_ref):
        @pl.when(pl.program_id(0) == 0)
        def _():
            l_ref[...] = jnp.zeros_like(l_ref)

        err = y_ref[...] - t_ref[...]
        dy_ref[...] = err * (1.0 / d)
        l_ref[...] += 0.5 * jnp.sum(jnp.mean(err * err, axis=1, keepdims=True), axis=0, keepdims=True)

    row = pl.BlockSpec((tm, d), lambda i: (i, 0))
    return pl.pallas_call(
        body, name="loss_head", grid=(s // tm,), in_specs=[row, row],
        out_specs=[row, pl.BlockSpec((8, 128), lambda i: (0, 0))],
        out_shape=[jax.ShapeDtypeStruct((s, d), F32), jax.ShapeDtypeStruct((8, 128), F32)],
        compiler_params=_params(("arbitrary",)))(y, tgt)


def _out_proj_bwd(dxo, ya, yb, proj, w):
    s, d = dxo.shape
    ds_ = ya.shape[1]
    tm = min(s, 256)

    def body(dx_ref, ya_ref, yb_ref, ga_ref, gb_ref, w_ref, dya_ref, dyb_ref, dga_ref, dgb_ref):
        dxb = dx_ref[...].astype(BF16)
        for y_ref, g_ref, lo, dy_ref, dg_ref in ((ya_ref, ga_ref, 0, dya_ref, dga_ref),
                                                 (yb_ref, gb_ref, ds_, dyb_ref, dgb_ref)):
            dmix = _dot_nt(dxb, w_ref[pl.ds(lo, ds_), :])
            act, dact = _silu_parts(g_ref[...])
            dy_ref[...] = dmix * act
            dg_ref[...] = (dmix * y_ref[...] * dact).astype(BF16)

    row = lambda width: pl.BlockSpec((tm, width), lambda i: (i, 0))
    return pl.pallas_call(
        body, name="out_proj_bwd", grid=(s // tm,),
        in_specs=[row(d), row(ds_), row(ds_),
                  pl.BlockSpec((tm, ds_), lambda i: (i, 3)), pl.BlockSpec((tm, ds_), lambda i: (i, 7)),
                  pl.BlockSpec((2 * ds_, d), lambda i: (0, 0))],
        out_specs=[row(ds_)] * 4,
        out_shape=[jax.ShapeDtypeStruct((s, ds_), F32)] * 2 + [jax.ShapeDtypeStruct((s, ds_), BF16)] * 2,
        compiler_params=_params(("parallel",), 48))(dxo, ya, yb, proj, proj, w)


def _wgrad(a, b, nblk, col_blocks, name):
    s, m = a.shape
    n = b.shape[1]
    if col_blocks:
        tr = min(m, 512)
        nb = n // nblk
        tn = min(nb, 1024)
        per = nb // tn
        out_shape = (nblk, m, nb)
        out_spec = pl.BlockSpec((None, tr, tn), lambda j, r: (j // per, r, j % per))
    else:
        tn = min(n, 1024)
        tr = m // nblk
        out_shape = (nblk, tr, n)
        out_spec = pl.BlockSpec((None, tr, tn), lambda j, r: (r, 0, j))

    def body(a_ref, b_ref, o_ref):
        o_ref[...] = _dot_tn(a_ref[...].astype(BF16), b_ref[...].astype(BF16)).astype(BF16)

    return pl.pallas_call(
        body, name=name, grid=(n // tn, m // tr),
        in_specs=[pl.BlockSpec((s, tr), lambda j, r: (0, r)), pl.BlockSpec((s, tn), lambda j, r: (0, j))],
        out_specs=out_spec, out_shape=jax.ShapeDtypeStruct(out_shape, BF16),
        compiler_params=_params(("parallel", "parallel"), 48))(a, b)


def _attn_a_bwd(proj, lt, dya, nh):
    s = proj.shape[0]
    nq = s // TQ
    hp = _heads_per_step(nh)
    ng = nh // hp

    def body(q_ref, k_ref, v_ref, lt_ref, do_ref, dq_ref, dk_ref, dv_ref, dq_acc, dk_acc, dv_acc):
        i = pl.program_id(1)

        @pl.when(i == 0)
        def _():
            dk_acc[...] = jnp.zeros_like(dk_acc)
            dv_acc[...] = jnp.zeros_like(dv_acc)

        dq_acc[...] = jnp.zeros_like(dq_acc)
        qbs = [q_ref[:, _head(hh)].astype(BF16) for hh in range(hp)]
        dobs = [do_ref[:, _head(hh)].astype(BF16) for hh in range(hp)]
        totals = [lt_ref[hh] for hh in range(hp)]
        causal = _tri(lambda r, c: r > c)
        m_upto = _tri(lambda r, c: r <= c).astype(BF16)
        m_before = _tri(lambda r, c: r < c).astype(BF16)

        def tile(j, carries, diag):
            off = pl.multiple_of(j * TQ, TQ)
            out = []
            for hh in range(hp):
                c_ls, c_g = carries[2 * hh], carries[2 * hh + 1]
                kb = k_ref[pl.ds(off, TQ), _head(hh)].astype(BF16)
                vb = v_ref[pl.ds(off, TQ), _head(hh)].astype(BF16)
                z, e, ls, lsig = _sb_logs(qbs[hh], kb, causal, diag)
                w = jnp.exp(lsig + (totals[hh] - (c_ls + _split_dot(ls, m_upto))))
                if diag:
                    w = jnp.where(causal, w, 0.0)
                g = w * _dot_nt(dobs[hh], vb)
                before = c_g + _split_dot(g, m_before)
                rinv = 1.0 / (1.0 + e)
                beta = jnp.where(z >= 0.0, rinv, e * rinv)
                dz = g * (1.0 - beta) - beta * before
                if diag:
                    dz = jnp.where(causal, dz, 0.0)
                dzb = (dz * SCALE).astype(BF16)
                dq_acc[:, _head(hh)] += _dot(dzb, kb)
                dk_acc[pl.ds(off, TQ), _head(hh)] += _dot_tn(dzb, qbs[hh])
                dv_acc[pl.ds(off, TQ), _head(hh)] += _dot_tn(w.astype(BF16), dobs[hh])
                out += [c_ls + jnp.sum(ls, axis=1, keepdims=True), c_g + jnp.sum(g, axis=1, keepdims=True)]
            return tuple(out)

        carries = lax.fori_loop(0, i, lambda j, ca: tile(j, ca, False), (jnp.zeros((TQ, 1), F32),) * (2 * hp))
        tile(i, carries, True)
        dq_ref[...] = dq_acc[...].astype(BF16)

        @pl.when(i == nq - 1)
        def _():
            dk_ref[...] = dk_acc[...].astype(BF16)
            dv_ref[...] = dv_acc[...].astype(BF16)

    wd = hp * HEAD
    blk = pl.BlockSpec((TQ, wd), lambda h, i: (i, h))
    col = pl.BlockSpec((s, wd), lambda h, i: (0, h))
    shp = jax.ShapeDtypeStruct((s, nh * HEAD), BF16)
    return pl.pallas_call(
        body, name="attn_a_bwd", grid=(ng, nq),
        in_specs=[blk,
                  pl.BlockSpec((s, wd), lambda h, i: (0, ng + h)),
                  pl.BlockSpec((s, wd), lambda h, i: (0, 2 * ng + h)),
                  pl.BlockSpec((hp, TQ, 1), lambda h, i: (h, i, 0)), blk],
        out_specs=[blk, col, col], out_shape=[shp] * 3,
        scratch_shapes=[pltpu.VMEM((TQ, wd), F32), pltpu.VMEM((s, wd), F32), pltpu.VMEM((s, wd), F32)],
        compiler_params=_params(("parallel", "arbitrary")))(proj, proj, proj, lt, dya)


def _attn_b_bwd(proj, dyb, qg, kg, ext, nh):
    s = proj.shape[0]
    nq = s // TQ

    def body(q_ref, k_ref, v_ref, do_ref, qg_ref, kg_ref, e_ref,
             dq_ref, dk_ref, dv_ref, dqg_ref, dkg_ref, de_ref,
             kn_pad, v_pad, bias_ref, db_acc, dkn_acc, dv_acc):
        i = pl.program_id(1)

        @pl.when(i == 0)
        def _():
            _band_fill(k_ref, v_ref, kg_ref, kn_pad, v_pad, s)
            _build_bias(e_ref, bias_ref)
            db_acc[...] = jnp.zeros_like(db_acc)
            dkn_acc[...] = jnp.zeros_like(dkn_acc)
            dv_acc[...] = jnp.zeros_like(dv_acc)
            dqg_ref[...] = jnp.zeros_like(dqg_ref)

        rq, qhat, qn, kw, off, pn = _band_probs(q_ref, qg_ref, kn_pad, bias_ref, i)
        dob = do_ref[...].astype(BF16)
        dp = _dot_nt(dob, v_pad[pl.ds(off, WIN), :])
        dsc = pn * (dp - jnp.sum(pn * dp, axis=1, keepdims=True))
        db_acc[:, :WIN] += dsc
        dsb = (dsc * SCALE).astype(BF16)
        dqn = _dot(dsb, kw)
        dkn_acc[pl.ds(off, WIN), :] += _dot_tn(dsb, qn)
        dv_acc[pl.ds(off, WIN), :] += _dot_tn(pn.astype(BF16), dob)
        dqh = dqn * qg_ref[...]
        dq_ref[...] = (rq * (dqh - qhat * jnp.mean(dqh * qhat, axis=1, keepdims=True))).astype(BF16)
        dqg_ref[...] += jnp.sum(dqn * qhat, axis=0, keepdims=True)

        @pl.when(i == nq - 1)
        def _():
            k = k_ref[...]
            rk = lax.rsqrt(jnp.mean(k * k, axis=1, keepdims=True) + NORM_EPS)
            khat = k * rk
            dkn = dkn_acc[pl.ds(PAD, s), :]
            dkh = dkn * kg_ref[...]
            dk_ref[...] = (rk * (dkh - khat * jnp.mean(dkh * khat, axis=1, keepdims=True))).astype(BF16)
            dkg_ref[...] = jnp.sum(dkn * khat, axis=0, keepdims=True)
            dv_ref[...] = dv_acc[pl.ds(PAD, s), :].astype(BF16)
            de_ref[...] = _reduce_bias_grad(db_acc)

    blk = pl.BlockSpec((TQ, HEAD), lambda h, i: (i, h))
    col = pl.BlockSpec((s, HEAD), lambda h, i: (0, h))
    vec = pl.BlockSpec((1, HEAD), lambda h, i: (0, 0))
    hvec = pl.BlockSpec((None, 1, HEAD), lambda h, i: (h, 0, 0))
    hext = pl.BlockSpec((None, 1, EXT), lambda h, i: (h, 0, 0))
    shp = jax.ShapeDtypeStruct((s, nh * HEAD), BF16)
    return pl.pallas_call(
        body, name="attn_b_bwd", grid=(nh, nq),
        in_specs=[pl.BlockSpec((TQ, HEAD), lambda h, i: (i, 4 * nh + h)),
                  pl.BlockSpec((s, HEAD), lambda h, i: (0, 5 * nh + h)),
                  pl.BlockSpec((s, HEAD), lambda h, i: (0, 6 * nh + h)),
                  blk, vec, vec, hext],
        out_specs=[blk, col, col, hvec, hvec, hext],
        out_shape=[shp] * 3 + [jax.ShapeDtypeStruct((nh, 1, HEAD), F32)] * 2
        + [jax.ShapeDtypeStruct((nh, 1, EXT), F32)],
        scratch_shapes=[pltpu.VMEM((s + PAD, HEAD), BF16), pltpu.VMEM((s + PAD, HEAD), BF16),
                        pltpu.VMEM((TQ, WIN), F32), pltpu.VMEM((TQ, EXT), F32),
                        pltpu.VMEM((s + PAD, HEAD), F32), pltpu.VMEM((s + PAD, HEAD), F32)],
        compiler_params=_params(("parallel", "arbitrary")))(proj, proj, proj, dyb, qg, kg, ext)


def _in_proj_bwd(dproj, w, x, dxo, g):
    s, d = x.shape
    nblk, _, nb = w.shape
    tm, tk = min(s, 256), min(nb, 1024)
    per = nb // tk
    nk = nblk * per

    def body(dp_ref, w_ref, x_ref, dxo_ref, g_ref, dx_ref, dg_ref, acc):
        m, k = pl.program_id(0), pl.program_id(1)

        @pl.when(k == 0)
        def _():
            acc[...] = jnp.zeros_like(acc)

        @pl.when((k == 0) & (m == 0))
        def _():
            dg_ref[...] = jnp.zeros_like(dg_ref)

        acc[...] += _dot_nt(dp_ref[...], w_ref[...])

        @pl.when(k == nk - 1)
        def _():
            xv = x_ref[...]
            r = lax.rsqrt(jnp.mean(xv * xv, axis=1, keepdims=True) + NORM_EPS)
            xhat = xv * r
            dh = acc[...]
            dxh = dh * g_ref[...]
            dx_ref[...] = dxo_ref[...] + r * (dxh - xhat * jnp.mean(dxh * xhat, axis=1, keepdims=True))
            dg_ref[...] += jnp.sum(dh * xhat, axis=0, keepdims=True)

    row = pl.BlockSpec((tm, d), lambda m, k: (m, 0))
    return pl.pallas_call(
        body, name="in_proj_bwd", grid=(s // tm, nk),
        in_specs=[pl.BlockSpec((tm, tk), lambda m, k: (m, k)),
                  pl.BlockSpec((None, d, tk), lambda m, k: (k // per, 0, k % per)),
                  row, row, pl.BlockSpec((1, d), lambda m, k: (0, 0))],
        out_specs=[row, pl.BlockSpec((8, d), lambda m, k: (0, 0))],
        out_shape=[jax.ShapeDtypeStruct((s, d), F32), jax.ShapeDtypeStruct((8, d), F32)],
        scratch_shapes=[pltpu.VMEM((tm, d), F32)],
        compiler_params=_params(("arbitrary", "arbitrary"), 48))(dproj, w, x, dxo, g)


def _place():
    x, y, c = lax.axis_index("x"), lax.axis_index("y"), lax.axis_index("c")
    chips = [(1 - x, y), (x, 1 - y), (1 - x, 1 - y)]
    return x, y, c, chips


def _comm_call(body, name, ins, out_shape, n_remote, n_local, aliases=None):
    return pl.pallas_call(
        body, name=name, in_specs=[ANY] * len(ins), out_specs=[ANY] * len(out_shape), out_shape=out_shape,
        input_output_aliases=aliases or {},
        scratch_shapes=[pltpu.SemaphoreType.DMA((n_remote,)), pltpu.SemaphoreType.DMA((n_remote,)),
                        pltpu.SemaphoreType.DMA((n_local,))])(*ins)


def _rcopy(src, dst, send_sems, recv_sems, k, dev):
    return pltpu.make_async_remote_copy(src_ref=src, dst_ref=dst, send_sem=send_sems.at[k], recv_sem=recv_sems.at[k],
                                        device_id=dev, device_id_type=MESH)


def _gather_weights(fulls, rbp):
    n = len(fulls)
    n_ici = 3 * n

    def body(*refs):
        rb_ref, fulls_ref, rb_full = refs[n], refs[n + 1:2 * n + 1], refs[2 * n + 1]
        send_sems, recv_sems, loc_sems = refs[2 * n + 2:]
        x, y, c, chips = _place()
        me, sib, b = (x, y, c), (x, y, 1 - c), 2 * x + y
        local = pltpu.make_async_copy(rb_ref, rb_full.at[b], loc_sems.at[0])
        local.start()

        def half(full, blk, core):
            hr = full.shape[1] // 2
            return full.at[blk].at[pl.ds(core * hr, hr)]

        sent = []
        for t, full in enumerate(fulls_ref):
            for j, chip in enumerate(chips):
                cp = _rcopy(half(full, b, c), half(full, b, c), send_sems, recv_sems, 3 * t + j, (*chip, c))
                cp.start()
                sent.append(cp)
        for j, chip in enumerate(chips):
            cp = _rcopy(rb_ref, rb_full.at[b], send_sems, recv_sems, 2 * n_ici + j, (*chip, c))
            cp.start()
            sent.append(cp)
        for t, full in enumerate(fulls_ref):
            for j, chip in enumerate(chips):
                got = half(full, 2 * chip[0] + chip[1], c)
                _rcopy(got, got, send_sems, recv_sems, 3 * t + j, me).wait_recv()
                cp = _rcopy(got, got, send_sems, recv_sems, n_ici + 3 * t + j, sib)
                cp.start()
                sent.append(cp)
        for t, full in enumerate(fulls_ref):
            for j, chip in enumerate(chips):
                got = half(full, 2 * chip[0] + chip[1], 1 - c)
                _rcopy(got, got, send_sems, recv_sems, n_ici + 3 * t + j, me).wait_recv()
        for j, chip in enumerate(chips):
            got = rb_full.at[2 * chip[0] + chip[1]]
            _rcopy(got, got, send_sems, recv_sems, 2 * n_ici + j, me).wait_recv()
        for cp in sent:
            cp.wait_send()
        local.wait()

    out_shape = [jax.ShapeDtypeStruct(f.shape, f.dtype) for f in fulls] + [jax.ShapeDtypeStruct((4,) + rbp.shape, F32)]
    out = _comm_call(body, "gather_weights", list(fulls) + [rbp], out_shape, 2 * n_ici + 3, 1,
                     aliases={t: t for t in range(n)})
    return out[:n], out[n]


def _reduce_sibling(parts):
    n = len(parts)

    def body(*refs):
        ins, theirs = refs[:n], refs[n:2 * n]
        send_sems, recv_sems, _ = refs[2 * n:]
        x, y, c, _ = _place()
        sent = []
        for t in range(n):
            hr = ins[t].shape[1] // 2
            cp = _rcopy(ins[t].at[:, pl.ds((1 - c) * hr, hr), :], theirs[t], send_sems, recv_sems, t, (x, y, 1 - c))
            cp.start()
            sent.append(cp)
        for cp in sent:
            cp.wait()

    half = [jax.ShapeDtypeStruct((p.shape[0], p.shape[1] // 2, p.shape[2]), p.dtype) for p in parts]
    return _comm_call(body, "reduce_sibling", parts, half, n, 1)


def _reduce_chips(sums):
    n = len(sums)

    def body(*refs):
        ins, got = refs[:n], refs[n:2 * n]
        send_sems, recv_sems, _ = refs[2 * n:]
        x, y, c, chips = _place()
        sent = []
        for t in range(n):
            for j, chip in enumerate(chips):
                cp = _rcopy(ins[t].at[2 * chip[0] + chip[1]], got[t].at[j], send_sems, recv_sems, 3 * t + j, (*chip, c))
                cp.start()
                sent.append(cp)
        for cp in sent:
            cp.wait()

    got = [jax.ShapeDtypeStruct((3,) + p.shape[1:], p.dtype) for p in sums]
    return _comm_call(body, "reduce_chips", sums, got, 3 * n, 1)


def _share_halves(tots):
    n = len(tots)

    def body(*refs):
        g = refs[n:2 * n]
        send_sems, recv_sems, _ = refs[2 * n:]
        x, y, c, _ = _place()
        sent = []
        for t in range(n):
            cp = _rcopy(g[t].at[c], g[t].at[c], send_sems, recv_sems, t, (x, y, 1 - c))
            cp.start()
            sent.append(cp)
        for cp in sent:
            cp.wait()

    out_shape = [jax.ShapeDtypeStruct(t.shape, t.dtype) for t in tots]
    return _comm_call(body, "share_halves", tots, out_shape, n, 1, aliases={t: t for t in range(n)})


def _gather_small(packed):
    def body(p_ref, all_ref, send_sems, recv_sems, loc_sems):
        x, y, c, _ = _place()
        me = 4 * x + 2 * y + c
        local = pltpu.make_async_copy(p_ref, all_ref.at[me], loc_sems.at[0])
        local.start()
        sent = []
        for k in range(1, 8):
            px, py, pc = x ^ (k >> 2), y ^ ((k >> 1) & 1), c ^ (k & 1)
            cp = _rcopy(p_ref, all_ref.at[me], send_sems, recv_sems, k - 1, (px, py, pc))
            cp.start()
            sent.append(cp)
        for k in range(1, 8):
            px, py, pc = x ^ (k >> 2), y ^ ((k >> 1) & 1), c ^ (k & 1)
            got = all_ref.at[4 * px + 2 * py + pc]
            _rcopy(got, got, send_sems, recv_sems, k - 1, (x, y, c)).wait_recv()
        for cp in sent:
            cp.wait_send()
        local.wait()

    return _comm_call(body, "gather_small", [packed], [jax.ShapeDtypeStruct((8,) + packed.shape, F32)], 7, 1)[0]


def _sum_devices(allp):
    n, r, c = allp.shape

    def body(a_ref, o_ref):
        acc = a_ref[0]
        for k in range(1, n):
            acc = acc + a_ref[k]
        o_ref[...] = acc

    return pl.pallas_call(body, name="sum_devices", out_shape=jax.ShapeDtypeStruct((r, c), F32))(allp)


def _ext_index():
    u = np.arange(EXT)
    dist = np.where(u < WIN, PAD - u, PAD + EXT - u)
    return np.clip(dist, -(CHUNK - 1), REL_CLIP) + (CHUNK - 1)


def _pack(parts, rows):
    flat = jnp.concatenate([p.reshape(-1) for p in parts])
    return jnp.pad(flat, (0, rows * 128 - flat.shape[0])).reshape(rows, 128)


def _unpack(packed, shapes):
    flat, out, at = packed.reshape(-1), [], 0
    for shp in shapes:
        size = int(np.prod(shp))
        out.append(flat[at:at + size].reshape(shp))
        at += size
    return out


def kernel(x, norm_g, w_in, q_norm_g, k_norm_g, rel_bias, w_out, loss_target, m_norm_g, m_w_in, m_q_norm_g, m_k_norm_g, m_rel_bias, m_w_out, v_norm_g, v_w_in, v_q_norm_g, v_k_norm_g, v_rel_bias, v_w_out):
    nl, d, nb = w_in.shape
    s = x.shape[1]
    ds_ = d // 2
    nh = ds_ // HEAD
    rb = w_out.shape[1]
    nrel = rel_bias.shape[2]
    bx = lax.axis_index("x") * 2 + lax.axis_index("y")

    rb_rows = -(-(nl * nh * nrel) // 1024) * 8
    cx = lax.axis_index("c")
    fulls = ([_cast_block(w_in, l, bx, "cast_w_in") for l in range(nl)]
             + [_cast_block(w_out, l, bx, "cast_w_out") for l in range(nl)])
    fulls, rel_all = _gather_weights(fulls, _pack([rel_bias], rb_rows))
    wi_full, wo_full = fulls[:nl], fulls[nl:]
    rel_full = jnp.concatenate(
        [rel_all[j].reshape(-1)[:nl * nh * nrel].reshape(nl, nh, nrel) for j in range(4)], axis=2)
    ext_idx = _ext_index()
    onehot = jnp.asarray(ext_idx[:, None] == np.arange(N_REL)[None, :], F32)
    ext = jnp.einsum("lhr,ur->lhu", rel_full, onehot, precision=lax.Precision.HIGHEST).reshape(nl, nh, 1, EXT)

    xs, hs, projs, yas, lts, ybs, mixes = [], [], [], [], [], [], []
    xc = x[0]
    for l in range(nl):
        h = _rmsnorm_fwd(xc, norm_g[l:l + 1])
        proj = _in_proj(h, wi_full[l])
        ya, lt = _attn_a_fwd(proj, nh)
        yb = _attn_b_fwd(proj, q_norm_g[l:l + 1], k_norm_g[l:l + 1], ext[l], nh)
        xs.append(xc)
        xc, mix = _out_proj(xc, ya, yb, proj, wo_full[l].reshape(4 * rb, d))
        hs.append(h); projs.append(proj); yas.append(ya); lts.append(lt); ybs.append(yb); mixes.append(mix)
    dx, loss_tile = _loss_head(xc, loss_target[0])

    p_wi, p_wo, small = [None] * nl, [None] * nl, [None] * nl
    for l in reversed(range(nl)):
        wo = wo_full[l].reshape(4 * rb, d)
        p_wo[l] = _wgrad(mixes[l], dx, 4, False, "wgrad_out")
        dya, dyb, dga, dgb = _out_proj_bwd(dx, yas[l], ybs[l], projs[l], wo)
        dqa, dka, dva = _attn_a_bwd(projs[l], lts[l], dya, nh)
        dqb, dkb, dvb, dqg, dkg, dext = _attn_b_bwd(projs[l], dyb, q_norm_g[l:l + 1], k_norm_g[l:l + 1], ext[l], nh)
        dproj = jnp.concatenate([dqa, dka, dva, dga, dqb, dkb, dvb, dgb], axis=1)
        p_wi[l] = _wgrad(hs[l], dproj, 4, True, "wgrad_in")
        dx, dng = _in_proj_bwd(dproj, wi_full[l], xs[l], dx, norm_g[l:l + 1])
        small[l] = (dng[0], jnp.sum(dqg, axis=0).reshape(-1), jnp.sum(dkg, axis=0).reshape(-1), dext.reshape(nh, EXT))
    grad_x = dx[None]

    parts = p_wi + p_wo
    theirs = _reduce_sibling(parts)
    sums = [_add_sibling(parts[t], theirs[t], cx) for t in range(2 * nl)]
    got = _reduce_chips(sums)
    tots = _share_halves([_add_chips(sums[t], got[t], bx, cx) for t in range(2 * nl)])
    g_wi = [t.reshape(d, nb) for t in tots[:nl]]
    g_wo = [t.reshape(rb, d) for t in tots[nl:]]

    small_shapes = [(nl, d), (nl, HEAD), (nl, HEAD), (nl, nh, EXT), (1,)]
    small_parts = [jnp.stack([sm[i] for sm in small]) for i in range(4)] + [loss_tile[0, :1]]
    rows = -(-sum(int(np.prod(sh)) for sh in small_shapes) // 1024) * 8
    tot = _sum_devices(_gather_small(_pack(small_parts, rows)))
    g_ng, g_qg, g_kg, g_ext, loss = _unpack(tot, small_shapes)
    g_rel_full = jnp.einsum("lhu,ur->lhr", g_ext, onehot, precision=lax.Precision.HIGHEST)
    g_rel = lax.dynamic_slice_in_dim(g_rel_full, bx * nrel, nrel, axis=2)

    res_wi, res_wo = (), ()
    for l in range(nl):
        res_wi = _adamw_layer(l, w_in, g_wi[l], m_w_in, v_w_in, res_wi, "adamw_w_in")
        res_wo = _adamw_layer(l, w_out, g_wo[l], m_w_out, v_w_out, res_wo, "adamw_w_out")
    g_wi, d_wi, nm_wi, nv_wi = res_wi
    g_wo, d_wo, nm_wo, nv_wo = res_wo
    sm_shapes = [(nl, d), (nl, HEAD), (nl, HEAD), (nl, nh, nrel)]
    sm_rows = -(-sum(int(np.prod(sh)) for sh in sm_shapes) // 1024) * 8
    pw, pg, pm, pv = [_pack(group, sm_rows) for group in (
        (norm_g, q_norm_g, k_norm_g, rel_bias), (g_ng, g_qg, g_kg, g_rel),
        (m_norm_g, m_q_norm_g, m_k_norm_g, m_rel_bias), (v_norm_g, v_q_norm_g, v_k_norm_g, v_rel_bias))]
    d_sm, nm_sm, nv_sm = [_unpack(a[0], sm_shapes)
                          for a in _adamw_layer(0, pw[None], pg, pm[None], pv[None], (), "adamw_small")[1:]]

    return (loss[0], grad_x, g_ng, g_wi, g_qg, g_kg, g_rel, g_wo,
            d_sm[0], d_wi, d_sm[1], d_sm[2], d_sm[3], d_wo,
            nm_sm[0], nm_wi, nm_sm[1], nm_sm[2], nm_sm[3], nm_wo,
            nv_sm[0], nv_wi, nv_sm[1], nv_sm[2], nv_sm[3], nv_wo)
```

```python
from typing import Callable, NamedTuple

import jax
import jax.numpy as jnp
import numpy as np
from jax import lax
from jax.experimental import pallas as pl
from jax.experimental.pallas import tpu as pltpu

F32 = jnp.float32
BF16 = jnp.bfloat16

HEAD = 128
CHUNK = 64
LEFT_CHUNKS = 8
REL_CLIP = 256
N_REL = REL_CLIP + CHUNK
NORM_EPS = 1e-6
NEG_BIG = -1e30
TQ = 256
PAD = LEFT_CHUNKS * CHUNK
WIN = PAD + TQ
EXT = 1024
SCALE = HEAD ** -0.5

ADAM_LR = 0.001
ADAM_B1 = 0.9
ADAM_B2 = 0.999
ADAM_EPS = 1e-08
ADAM_WD = 0.01
ADAM_STEP = 10

ANY = pl.BlockSpec(memory_space=pl.ANY)
MESH = pl.DeviceIdType.MESH


def _params(sem=None, vmem_mb=None):
    kw = {}
    if sem is not None:
        kw["dimension_semantics"] = sem
    if vmem_mb is not None:
        kw["vmem_limit_bytes"] = vmem_mb << 20
    return pltpu.CompilerParams(**kw)


class _Comm(NamedTuple):
    ins: tuple
    outs: tuple
    aliases: dict
    n_sems: int
    start: Callable
    finish: Callable


def _call(body, *, name, grid, in_specs, out_specs, out_shape, args, scratch=(), sem=None, vmem_mb=None, comm=None):
    if comm is None:
        out = pl.pallas_call(body, name=name, grid=grid, in_specs=in_specs, out_specs=out_specs, out_shape=out_shape,
                             scratch_shapes=list(scratch), compiler_params=_params(sem, vmem_mb))(*args)
        return out, ()
    n_in, n_out, n_ci, n_co = len(in_specs), len(out_shape), len(comm.ins), len(comm.outs)

    def hosted(*refs):
        ins, cins = refs[:n_in], refs[n_in:n_in + n_ci]
        outs, couts = refs[n_in + n_ci:n_in + n_ci + n_out], refs[n_in + n_ci + n_out:n_in + n_ci + n_out + n_co]
        rest = refs[n_in + n_ci + n_out + n_co:]
        send_sems, recv_sems = rest[-2:]
        first, last = None, None
        for ax, size in enumerate(grid):
            at = pl.program_id(ax)
            first = (at == 0) if first is None else first & (at == 0)
            last = (at == size - 1) if last is None else last & (at == size - 1)

        @pl.when(first)
        def _():
            comm.start(cins, couts, send_sems, recv_sems)

        body(*ins, *outs, *rest[:-2])

        @pl.when(last)
        def _():
            comm.finish(cins, couts, send_sems, recv_sems)

    out = pl.pallas_call(
        hosted, name=name, grid=grid, in_specs=list(in_specs) + [ANY] * n_ci, out_specs=list(out_specs) + [ANY] * n_co,
        out_shape=list(out_shape) + list(comm.outs),
        input_output_aliases={n_in + k: n_out + v for k, v in comm.aliases.items()},
        scratch_shapes=list(scratch) + [pltpu.SemaphoreType.DMA((comm.n_sems,)), pltpu.SemaphoreType.DMA((comm.n_sems,))],
        compiler_params=_params(("arbitrary",) * len(grid), vmem_mb))(*args, *comm.ins)
    return out[:n_out], out[n_out:]


def _dot(a, b):
    return jnp.dot(a, b, preferred_element_type=F32)


def _dot_nt(a, b):
    return lax.dot_general(a, b, (((1,), (1,)), ((), ())), preferred_element_type=F32)


def _dot_tn(a, b):
    return lax.dot_general(a, b, (((0,), (0,)), ((), ())), preferred_element_type=F32)


def _split_dot(x, m):
    hi = x.astype(BF16)
    lo = (x - hi.astype(F32)).astype(BF16)
    return _dot(hi, m) + _dot(lo, m)


def _silu_parts(g):
    sg = 1.0 / (1.0 + jnp.exp(-g))
    return g * sg, sg * (1.0 + g * (1.0 - sg))


def _idx(*vals):
    return jnp.stack([jnp.asarray(v, jnp.int32) for v in vals])


def _cast_block(w, l, blk, name):
    _, r, c = w.shape
    tr = min(r, 512)

    def body(b_ref, w_ref, o_ref):
        o_ref[...] = w_ref[...].astype(BF16)

    spec = pltpu.PrefetchScalarGridSpec(
        num_scalar_prefetch=1, grid=(r // tr,),
        in_specs=[pl.BlockSpec((None, tr, c), lambda i, b: (l, i, 0))],
        out_specs=pl.BlockSpec((None, tr, c), lambda i, b: (b[0], i, 0)))
    return pl.pallas_call(body, name=name, grid_spec=spec, out_shape=jax.ShapeDtypeStruct((4, r, c), BF16),
                          compiler_params=_params(("parallel",)))(_idx(blk), w)


def _add_sibling(p, theirs, core):
    nblk, r, c = p.shape
    hr = r // 2
    tr = min(hr, 256)
    per = hr // tr

    def body(c_ref, p_ref, t_ref, o_ref):
        o_ref[...] = (p_ref[...].astype(F32) + t_ref[...].astype(F32)).astype(BF16)

    blk = pl.BlockSpec((None, tr, c), lambda j, i, cr: (j, i, 0))
    spec = pltpu.PrefetchScalarGridSpec(
        num_scalar_prefetch=1, grid=(nblk, per),
        in_specs=[pl.BlockSpec((None, tr, c), lambda j, i, cr: (j, cr[0] * per + i, 0)), blk], out_specs=blk)
    return pl.pallas_call(body, name="add_sibling", grid_spec=spec, out_shape=jax.ShapeDtypeStruct((nblk, hr, c), BF16),
                          compiler_params=_params(("parallel", "parallel")))(_idx(core), p, theirs)


def _add_chips(sums, got, blk, core):
    _, hr, c = sums.shape
    tr = min(hr, 256)

    def body(i_ref, s_ref, g0_ref, g1_ref, g2_ref, o_ref):
        o_ref[...] = ((s_ref[...].astype(F32) + g0_ref[...].astype(F32))
                      + g1_ref[...].astype(F32)) + g2_ref[...].astype(F32)

    at = lambda j: pl.BlockSpec((None, tr, c), lambda i, ir: (j, i, 0))
    spec = pltpu.PrefetchScalarGridSpec(
        num_scalar_prefetch=1, grid=(hr // tr,),
        in_specs=[pl.BlockSpec((None, tr, c), lambda i, ir: (ir[0], i, 0)), at(0), at(1), at(2)],
        out_specs=pl.BlockSpec((None, tr, c), lambda i, ir: (ir[1], i, 0)))
    return pl.pallas_call(body, name="add_chips", grid_spec=spec, out_shape=jax.ShapeDtypeStruct((2, hr, c), F32),
                          compiler_params=_params(("parallel",)))(_idx(blk, core), sums, got, got, got)


def _adamw_layer(l, w, g, m, v, prev, name):
    nl, r, c = w.shape
    tr = min(r, 256)
    c1 = 1.0 / (1.0 - ADAM_B1 ** ADAM_STEP)
    c2 = 1.0 / (1.0 - ADAM_B2 ** ADAM_STEP)

    def body(w_ref, g_ref, m_ref, v_ref, *rest):
        go_ref, d_ref, nm_ref, nv_ref = rest[-4:]
        gg = g_ref[...]
        nm = ADAM_B1 * m_ref[...] + (1.0 - ADAM_B1) * gg
        nv = ADAM_B2 * v_ref[...] + (1.0 - ADAM_B2) * (gg * gg)
        upd = (nm * c1) / (jnp.sqrt(nv * c2) + ADAM_EPS) + ADAM_WD * w_ref[...]
        go_ref[...] = gg
        d_ref[...] = -ADAM_LR * upd
        nm_ref[...] = nm
        nv_ref[...] = nv

    lay = pl.BlockSpec((None, tr, c), lambda i: (l, i, 0))
    shp = jax.ShapeDtypeStruct((nl, r, c), F32)
    return pl.pallas_call(
        body, name=name, grid=(r // tr,),
        in_specs=[lay, pl.BlockSpec((tr, c), lambda i: (i, 0)), lay, lay] + [ANY] * len(prev),
        out_specs=[lay] * 4, out_shape=[shp] * 4, input_output_aliases={4 + k: k for k in range(len(prev))},
        compiler_params=_params(("parallel",), 40))(w, g, m, v, *prev)


def _rmsnorm_fwd(x, g):
    s, d = x.shape
    tm = min(s, 256)

    def body(x_ref, g_ref, h_ref):
        xv = x_ref[...]
        r = lax.rsqrt(jnp.mean(xv * xv, axis=1, keepdims=True) + NORM_EPS)
        h_ref[...] = (xv * r * g_ref[...]).astype(BF16)

    return pl.pallas_call(
        body, name="rmsnorm_fwd", grid=(s // tm,),
        in_specs=[pl.BlockSpec((tm, d), lambda i: (i, 0)), pl.BlockSpec((1, d), lambda i: (0, 0))],
        out_specs=pl.BlockSpec((tm, d), lambda i: (i, 0)),
        out_shape=jax.ShapeDtypeStruct((s, d), BF16),
        compiler_params=_params(("parallel",)))(x, g)


def _in_proj(h, w):
    s, d = h.shape
    nblk, _, nb = w.shape
    tm, tn = min(s, 512), min(nb, 1024)
    per = nb // tn

    def body(h_ref, w_ref, o_ref):
        o_ref[...] = _dot(h_ref[...], w_ref[...])

    return pl.pallas_call(
        body, name="in_proj", grid=(nblk * per, s // tm),
        in_specs=[pl.BlockSpec((tm, d), lambda n, m: (m, 0)),
                  pl.BlockSpec((None, d, tn), lambda n, m: (n // per, 0, n % per))],
        out_specs=pl.BlockSpec((tm, tn), lambda n, m: (m, n)),
        out_shape=jax.ShapeDtypeStruct((s, nblk * nb), F32),
        compiler_params=_params(("parallel", "parallel"), 40))(h, w)


def _heads_per_step(nh):
    return 2 if nh % 2 == 0 else 1


def _head(hh):
    return slice(hh * HEAD, (hh + 1) * HEAD)


def _tri(op):
    r = lax.broadcasted_iota(jnp.int32, (TQ, TQ), 0)
    c = lax.broadcasted_iota(jnp.int32, (TQ, TQ), 1)
    return op(r, c)


def _sb_logs(qb, kb, causal, diag):
    z = _dot_nt(qb, kb) * SCALE
    e = jnp.exp(-jnp.abs(z))
    l1p = jnp.log(1.0 + e)
    ls = jnp.minimum(-z, 0.0) - l1p
    lsig = jnp.minimum(z, 0.0) - l1p
    if diag:
        ls = jnp.where(causal, ls, 0.0)
    return z, e, ls, lsig


def _attn_a_fwd(proj, nh, comm=None):
    s = proj.shape[0]
    nq = s // TQ
    hp = _heads_per_step(nh)
    ng = nh // hp

    def body(q_ref, k_ref, v_ref, o_ref, lt_ref, acc_ref):
        i = pl.program_id(1)
        qbs = [q_ref[:, _head(hh)].astype(BF16) for hh in range(hp)]
        causal = _tri(lambda r, c: r > c)
        m_after = causal.astype(BF16)
        acc_ref[...] = jnp.zeros_like(acc_ref)

        def tile(j, carries, diag):
            off = pl.multiple_of(j * TQ, TQ)
            out = []
            for hh in range(hp):
                kb = k_ref[pl.ds(off, TQ), _head(hh)].astype(BF16)
                vb = v_ref[pl.ds(off, TQ), _head(hh)].astype(BF16)
                _, _, ls, lsig = _sb_logs(qbs[hh], kb, causal, diag)
                w = jnp.exp(lsig + _split_dot(ls, m_after) + carries[hh])
                if diag:
                    w = jnp.where(causal, w, 0.0)
                acc_ref[:, _head(hh)] += _dot(w.astype(BF16), vb)
                out.append(carries[hh] + jnp.sum(ls, axis=1, keepdims=True))
            return tuple(out)

        carries = tile(i, (jnp.zeros((TQ, 1), F32),) * hp, True)
        carries = lax.fori_loop(0, i, lambda t, ca: tile(i - 1 - t, ca, False), carries)
        o_ref[...] = acc_ref[...]
        for hh in range(hp):
            lt_ref[hh] = carries[hh]

    wd = hp * HEAD
    return _call(
        body, name="attn_a_fwd", grid=(ng, nq),
        in_specs=[pl.BlockSpec((TQ, wd), lambda h, i: (i, h)),
                  pl.BlockSpec((s, wd), lambda h, i: (0, ng + h)),
                  pl.BlockSpec((s, wd), lambda h, i: (0, 2 * ng + h))],
        out_specs=[pl.BlockSpec((TQ, wd), lambda h, i: (i, h)),
                   pl.BlockSpec((hp, TQ, 1), lambda h, i: (h, i, 0))],
        out_shape=[jax.ShapeDtypeStruct((s, nh * HEAD), F32), jax.ShapeDtypeStruct((nh, s, 1), F32)],
        scratch=[pltpu.VMEM((TQ, wd), F32)], sem=("parallel", "arbitrary"), args=(proj, proj, proj), comm=comm)


def _band_valid(i):
    cl = lax.broadcasted_iota(jnp.int32, (TQ, WIN), 0) // CHUNK
    kl = lax.broadcasted_iota(jnp.int32, (TQ, WIN), 1) // CHUNK
    first = LEFT_CHUNKS - (TQ // CHUNK) * i
    return (kl >= cl) & (kl <= cl + LEFT_CHUNKS) & (kl >= first)


def _build_bias(e_ref, bias_ref):
    e8 = jnp.broadcast_to(e_ref[...], (8, EXT))
    row = lax.broadcasted_iota(jnp.int32, (8, EXT), 0)
    t8 = jnp.zeros((8, EXT), F32)
    for b in range(8):
        t8 = jnp.where(row == b, pltpu.roll(e8, b, 1) if b else e8, t8)
    for a in range(TQ // 8):
        sl = pltpu.roll(t8, 8 * a, 1) if a else t8
        bias_ref[pl.ds(8 * a, 8), :] = sl[:, :WIN]


def _reduce_bias_grad(db_ref):
    acc = jnp.zeros((8, EXT), F32)
    for a in range(TQ // 8):
        sl = db_ref[pl.ds(8 * a, 8), :]
        acc = acc + (pltpu.roll(sl, EXT - 8 * a, 1) if a else sl)
    row = lax.broadcasted_iota(jnp.int32, (8, EXT), 0)
    tot = jnp.zeros((8, EXT), F32)
    for b in range(8):
        tot = tot + jnp.where(row == b, pltpu.roll(acc, EXT - b, 1) if b else acc, 0.0)
    return jnp.sum(tot, axis=0, keepdims=True)


def _band_fill(k_ref, v_ref, kg_ref, kn_pad, v_pad, s):
    k = k_ref[...]
    rk = lax.rsqrt(jnp.mean(k * k, axis=1, keepdims=True) + NORM_EPS)
    kn_pad[pl.ds(0, PAD), :] = jnp.zeros((PAD, HEAD), BF16)
    kn_pad[pl.ds(PAD, s), :] = (k * rk * kg_ref[...]).astype(BF16)
    v_pad[pl.ds(0, PAD), :] = jnp.zeros((PAD, HEAD), BF16)
    v_pad[pl.ds(PAD, s), :] = v_ref[...].astype(BF16)


def _band_probs(q_ref, qg_ref, kn_pad, bias_ref, i):
    q = q_ref[...]
    rq = lax.rsqrt(jnp.mean(q * q, axis=1, keepdims=True) + NORM_EPS)
    qhat = q * rq
    qn = (qhat * qg_ref[...]).astype(BF16)
    off = pl.multiple_of(i * TQ, TQ)
    kw = kn_pad[pl.ds(off, WIN), :]
    sc = _dot_nt(qn, kw) * SCALE + bias_ref[...]
    sc = jnp.where(_band_valid(i), sc, NEG_BIG)
    p = jnp.exp(sc - jnp.max(sc, axis=1, keepdims=True))
    pn = p / jnp.sum(p, axis=1, keepdims=True)
    return rq, qhat, qn, kw, off, pn


def _attn_b_fwd(proj, qg, kg, ext, nh, comm=None):
    s = proj.shape[0]
    nq = s // TQ

    def body(q_ref, k_ref, v_ref, qg_ref, kg_ref, e_ref, o_ref, kn_pad, v_pad, bias_ref):
        i = pl.program_id(1)

        @pl.when(i == 0)
        def _():
            _band_fill(k_ref, v_ref, kg_ref, kn_pad, v_pad, s)
            _build_bias(e_ref, bias_ref)

        _, _, _, _, off, pn = _band_probs(q_ref, qg_ref, kn_pad, bias_ref, i)
        o_ref[...] = _dot(pn.astype(BF16), v_pad[pl.ds(off, WIN), :])

    vec = pl.BlockSpec((1, HEAD), lambda h, i: (0, 0))
    return _call(
        body, name="attn_b_fwd", grid=(nh, nq),
        in_specs=[pl.BlockSpec((TQ, HEAD), lambda h, i: (i, 4 * nh + h)),
                  pl.BlockSpec((s, HEAD), lambda h, i: (0, 5 * nh + h)),
                  pl.BlockSpec((s, HEAD), lambda h, i: (0, 6 * nh + h)),
                  vec, vec,
                  pl.BlockSpec((None, 1, EXT), lambda h, i: (h, 0, 0))],
        out_specs=[pl.BlockSpec((TQ, HEAD), lambda h, i: (i, h))],
        out_shape=[jax.ShapeDtypeStruct((s, nh * HEAD), F32)],
        scratch=[pltpu.VMEM((s + PAD, HEAD), BF16), pltpu.VMEM((s + PAD, HEAD), BF16), pltpu.VMEM((TQ, WIN), F32)],
        sem=("parallel", "arbitrary"), args=(proj, proj, proj, qg, kg, ext), comm=comm)


def _out_proj(x, ya, yb, proj, w):
    s, d = x.shape
    ds_ = ya.shape[1]
    tm = min(s, 256)

    def body(x_ref, ya_ref, yb_ref, ga_ref, gb_ref, w_ref, o_ref, mix_ref):
        ma = (ya_ref[...] * _silu_parts(ga_ref[...])[0]).astype(BF16)
        mb = (yb_ref[...] * _silu_parts(gb_ref[...])[0]).astype(BF16)
        mix_ref[:, :ds_] = ma
        mix_ref[:, ds_:] = mb
        o_ref[...] = x_ref[...] + _dot(ma, w_ref[pl.ds(0, ds_), :]) + _dot(mb, w_ref[pl.ds(ds_, ds_), :])

    row = lambda width: pl.BlockSpec((tm, width), lambda i: (i, 0))
    return pl.pallas_call(
        body, name="out_proj", grid=(s // tm,),
        in_specs=[row(d), row(ds_), row(ds_),
                  pl.BlockSpec((tm, ds_), lambda i: (i, 3)), pl.BlockSpec((tm, ds_), lambda i: (i, 7)),
                  pl.BlockSpec((2 * ds_, d), lambda i: (0, 0))],
        out_specs=[row(d), row(2 * ds_)],
        out_shape=[jax.ShapeDtypeStruct((s, d), F32), jax.ShapeDtypeStruct((s, 2 * ds_), BF16)],
        compiler_params=_params(("parallel",), 48))(x, ya, yb, proj, proj, w)


def _loss_head(y, tgt):
    s, d = y.shape
    tm = min(s, 256)

    def body(y_ref, t_ref, dy_ref, l_ref):
        @pl.when(pl.program_id(0) == 0)
        def _():
            l_ref[...] = jnp.zeros_like(l_ref)

        err = y_ref[...] - t_ref[...]
        dy_ref[...] = err * (1.0 / d)
        l_ref[...] += 0.5 * jnp.sum(jnp.mean(err * err, axis=1, keepdims=True), axis=0, keepdims=True)

    row = pl.BlockSpec((tm, d), lambda i: (i, 0))
    return pl.pallas_call(
        body, name="loss_head", grid=(s // tm,), in_specs=[row, row],
        out_specs=[row, pl.BlockSpec((8, 128), lambda i: (0, 0))],
        out_shape=[jax.ShapeDtypeStruct((s, d), F32), jax.ShapeDtypeStruct((8, 128), F32)],
        compiler_params=_params(("arbitrary",)))(y, tgt)


def _out_proj_bwd(dxo, ya, yb, proj, w):
    s, d = dxo.shape
    ds_ = ya.shape[1]
    tm = min(s, 256)

    def body(dx_ref, ya_ref, yb_ref, ga_ref, gb_ref, w_ref, dya_ref, dyb_ref, dga_ref, dgb_ref):
        dxb = dx_ref[...].astype(BF16)
        for y_ref, g_ref, lo, dy_ref, dg_ref in ((ya_ref, ga_ref, 0, dya_ref, dga_ref),
                                                 (yb_ref, gb_ref, ds_, dyb_ref, dgb_ref)):
            dmix = _dot_nt(dxb, w_ref[pl.ds(lo, ds_), :])
            act, dact = _silu_parts(g_ref[...])
            dy_ref[...] = dmix * act
            dg_ref[...] = (dmix * y_ref[...] * dact).astype(BF16)

    row = lambda width: pl.BlockSpec((tm, width), lambda i: (i, 0))
    return pl.pallas_call(
        body, name="out_proj_bwd", grid=(s // tm,),
        in_specs=[row(d), row(ds_), row(ds_),
                  pl.BlockSpec((tm, ds_), lambda i: (i, 3)), pl.BlockSpec((tm, ds_), lambda i: (i, 7)),
                  pl.BlockSpec((2 * ds_, d), lambda i: (0, 0))],
        out_specs=[row(ds_)] * 4,
        out_shape=[jax.ShapeDtypeStruct((s, ds_), F32)] * 2 + [jax.ShapeDtypeStruct((s, ds_), BF16)] * 2,
        compiler_params=_params(("parallel",), 48))(dxo, ya, yb, proj, proj, w)


def _wgrad(a, b, nblk, col_blocks, name):
    s, m = a.shape
    n = b.shape[1]
    if col_blocks:
        tr = min(m, 512)
        nb = n // nblk
        tn = min(nb, 1024)
        per = nb // tn
        out_shape = (nblk, m, nb)
        out_spec = pl.BlockSpec((None, tr, tn), lambda j, r: (j // per, r, j % per))
    else:
        tn = min(n, 1024)
        tr = m // nblk
        out_shape = (nblk, tr, n)
        out_spec = pl.BlockSpec((None, tr, tn), lambda j, r: (r, 0, j))

    def body(a_ref, b_ref, o_ref):
        o_ref[...] = _dot_tn(a_ref[...].astype(BF16), b_ref[...].astype(BF16)).astype(BF16)

    return pl.pallas_call(
        body, name=name, grid=(n // tn, m // tr),
        in_specs=[pl.BlockSpec((s, tr), lambda j, r: (0, r)), pl.BlockSpec((s, tn), lambda j, r: (0, j))],
        out_specs=out_spec, out_shape=jax.ShapeDtypeStruct(out_shape, BF16),
        compiler_params=_params(("parallel", "parallel"), 48))(a, b)


def _attn_a_bwd(proj, lt, dya, nh, comm=None):
    s = proj.shape[0]
    nq = s // TQ
    hp = _heads_per_step(nh)
    ng = nh // hp

    def body(q_ref, k_ref, v_ref, lt_ref, do_ref, dq_ref, dk_ref, dv_ref, dq_acc, dk_acc, dv_acc):
        i = pl.program_id(1)

        @pl.when(i == 0)
        def _():
            dk_acc[...] = jnp.zeros_like(dk_acc)
            dv_acc[...] = jnp.zeros_like(dv_acc)

        dq_acc[...] = jnp.zeros_like(dq_acc)
        qbs = [q_ref[:, _head(hh)].astype(BF16) for hh in range(hp)]
        dobs = [do_ref[:, _head(hh)].astype(BF16) for hh in range(hp)]
        totals = [lt_ref[hh] for hh in range(hp)]
        causal = _tri(lambda r, c: r > c)
        m_upto = _tri(lambda r, c: r <= c).astype(BF16)
        m_before = _tri(lambda r, c: r < c).astype(BF16)

        def tile(j, carries, diag):
            off = pl.multiple_of(j * TQ, TQ)
            out = []
            for hh in range(hp):
                c_ls, c_g = carries[2 * hh], carries[2 * hh + 1]
                kb = k_ref[pl.ds(off, TQ), _head(hh)].astype(BF16)
                vb = v_ref[pl.ds(off, TQ), _head(hh)].astype(BF16)
                z, e, ls, lsig = _sb_logs(qbs[hh], kb, causal, diag)
                w = jnp.exp(lsig + (totals[hh] - (c_ls + _split_dot(ls, m_upto))))
                if diag:
                    w = jnp.where(causal, w, 0.0)
                g = w * _dot_nt(dobs[hh], vb)
                before = c_g + _split_dot(g, m_before)
                rinv = 1.0 / (1.0 + e)
                beta = jnp.where(z >= 0.0, rinv, e * rinv)
                dz = g * (1.0 - beta) - beta * before
                if diag:
                    dz = jnp.where(causal, dz, 0.0)
                dzb = (dz * SCALE).astype(BF16)
                dq_acc[:, _head(hh)] += _dot(dzb, kb)
                dk_acc[pl.ds(off, TQ), _head(hh)] += _dot_tn(dzb, qbs[hh])
                dv_acc[pl.ds(off, TQ), _head(hh)] += _dot_tn(w.astype(BF16), dobs[hh])
                out += [c_ls + jnp.sum(ls, axis=1, keepdims=True), c_g + jnp.sum(g, axis=1, keepdims=True)]
            return tuple(out)

        carries = lax.fori_loop(0, i, lambda j, ca: tile(j, ca, False), (jnp.zeros((TQ, 1), F32),) * (2 * hp))
        tile(i, carries, True)
        dq_ref[...] = dq_acc[...].astype(BF16)

        @pl.when(i == nq - 1)
        def _():
            dk_ref[...] = dk_acc[...].astype(BF16)
            dv_ref[...] = dv_acc[...].astype(BF16)

    wd = hp * HEAD
    blk = pl.BlockSpec((TQ, wd), lambda h, i: (i, h))
    col = pl.BlockSpec((s, wd), lambda h, i: (0, h))
    shp = jax.ShapeDtypeStruct((s, nh * HEAD), BF16)
    return _call(
        body, name="attn_a_bwd", grid=(ng, nq),
        in_specs=[blk,
                  pl.BlockSpec((s, wd), lambda h, i: (0, ng + h)),
                  pl.BlockSpec((s, wd), lambda h, i: (0, 2 * ng + h)),
                  pl.BlockSpec((hp, TQ, 1), lambda h, i: (h, i, 0)), blk],
        out_specs=[blk, col, col], out_shape=[shp] * 3,
        scratch=[pltpu.VMEM((TQ, wd), F32), pltpu.VMEM((s, wd), F32), pltpu.VMEM((s, wd), F32)],
        sem=("parallel", "arbitrary"), args=(proj, proj, proj, lt, dya), comm=comm)


def _attn_b_bwd(proj, dyb, qg, kg, ext, nh):
    s = proj.shape[0]
    nq = s // TQ

    def body(q_ref, k_ref, v_ref, do_ref, qg_ref, kg_ref, e_ref,
             dq_ref, dk_ref, dv_ref, dqg_ref, dkg_ref, de_ref,
             kn_pad, v_pad, bias_ref, db_acc, dkn_acc, dv_acc):
        i = pl.program_id(1)

        @pl.when(i == 0)
        def _():
            _band_fill(k_ref, v_ref, kg_ref, kn_pad, v_pad, s)
            _build_bias(e_ref, bias_ref)
            db_acc[...] = jnp.zeros_like(db_acc)
            dkn_acc[...] = jnp.zeros_like(dkn_acc)
            dv_acc[...] = jnp.zeros_like(dv_acc)
            dqg_ref[...] = jnp.zeros_like(dqg_ref)

        rq, qhat, qn, kw, off, pn = _band_probs(q_ref, qg_ref, kn_pad, bias_ref, i)
        dob = do_ref[...].astype(BF16)
        dp = _dot_nt(dob, v_pad[pl.ds(off, WIN), :])
        dsc = pn * (dp - jnp.sum(pn * dp, axis=1, keepdims=True))
        db_acc[:, :WIN] += dsc
        dsb = (dsc * SCALE).astype(BF16)
        dqn = _dot(dsb, kw)
        dkn_acc[pl.ds(off, WIN), :] += _dot_tn(dsb, qn)
        dv_acc[pl.ds(off, WIN), :] += _dot_tn(pn.astype(BF16), dob)
        dqh = dqn * qg_ref[...]
        dq_ref[...] = (rq * (dqh - qhat * jnp.mean(dqh * qhat, axis=1, keepdims=True))).astype(BF16)
        dqg_ref[...] += jnp.sum(dqn * qhat, axis=0, keepdims=True)

        @pl.when(i == nq - 1)
        def _():
            k = k_ref[...]
            rk = lax.rsqrt(jnp.mean(k * k, axis=1, keepdims=True) + NORM_EPS)
            khat = k * rk
            dkn = dkn_acc[pl.ds(PAD, s), :]
            dkh = dkn * kg_ref[...]
            dk_ref[...] = (rk * (dkh - khat * jnp.mean(dkh * khat, axis=1, keepdims=True))).astype(BF16)
            dkg_ref[...] = jnp.sum(dkn * khat, axis=0, keepdims=True)
            dv_ref[...] = dv_acc[pl.ds(PAD, s), :].astype(BF16)
            de_ref[...] = _reduce_bias_grad(db_acc)

    blk = pl.BlockSpec((TQ, HEAD), lambda h, i: (i, h))
    col = pl.BlockSpec((s, HEAD), lambda h, i: (0, h))
    vec = pl.BlockSpec((1, HEAD), lambda h, i: (0, 0))
    hvec = pl.BlockSpec((None, 1, HEAD), lambda h, i: (h, 0, 0))
    hext = pl.BlockSpec((None, 1, EXT), lambda h, i: (h, 0, 0))
    shp = jax.ShapeDtypeStruct((s, nh * HEAD), BF16)
    return pl.pallas_call(
        body, name="attn_b_bwd", grid=(nh, nq),
        in_specs=[pl.BlockSpec((TQ, HEAD), lambda h, i: (i, 4 * nh + h)),
                  pl.BlockSpec((s, HEAD), lambda h, i: (0, 5 * nh + h)),
                  pl.BlockSpec((s, HEAD), lambda h, i: (0, 6 * nh + h)),
                  blk, vec, vec, hext],
        out_specs=[blk, col, col, hvec, hvec, hext],
        out_shape=[shp] * 3 + [jax.ShapeDtypeStruct((nh, 1, HEAD), F32)] * 2
        + [jax.ShapeDtypeStruct((nh, 1, EXT), F32)],
        scratch_shapes=[pltpu.VMEM((s + PAD, HEAD), BF16), pltpu.VMEM((s + PAD, HEAD), BF16),
                        pltpu.VMEM((TQ, WIN), F32), pltpu.VMEM((TQ, EXT), F32),
                        pltpu.VMEM((s + PAD, HEAD), F32), pltpu.VMEM((s + PAD, HEAD), F32)],
        compiler_params=_params(("parallel", "arbitrary")))(proj, proj, proj, dyb, qg, kg, ext)


def _in_proj_bwd(dproj, w, x, dxo, g):
    s, d = x.shape
    nblk, _, nb = w.shape
    tm, tk = min(s, 256), min(nb, 1024)
    per = nb // tk
    nk = nblk * per

    def body(dp_ref, w_ref, x_ref, dxo_ref, g_ref, dx_ref, dg_ref, acc):
        m, k = pl.program_id(0), pl.program_id(1)

        @pl.when(k == 0)
        def _():
            acc[...] = jnp.zeros_like(acc)

        @pl.when((k == 0) & (m == 0))
        def _():
            dg_ref[...] = jnp.zeros_like(dg_ref)

        acc[...] += _dot_nt(dp_ref[...], w_ref[...])

        @pl.when(k == nk - 1)
        def _():
            xv = x_ref[...]
            r = lax.rsqrt(jnp.mean(xv * xv, axis=1, keepdims=True) + NORM_EPS)
            xhat = xv * r
            dh = acc[...]
            dxh = dh * g_ref[...]
            dx_ref[...] = dxo_ref[...] + r * (dxh - xhat * jnp.mean(dxh * xhat, axis=1, keepdims=True))
            dg_ref[...] += jnp.sum(dh * xhat, axis=0, keepdims=True)

    row = pl.BlockSpec((tm, d), lambda m, k: (m, 0))
    return pl.pallas_call(
        body, name="in_proj_bwd", grid=(s // tm, nk),
        in_specs=[pl.BlockSpec((tm, tk), lambda m, k: (m, k)),
                  pl.BlockSpec((None, d, tk), lambda m, k: (k // per, 0, k % per)),
                  row, row, pl.BlockSpec((1, d), lambda m, k: (0, 0))],
        out_specs=[row, pl.BlockSpec((8, d), lambda m, k: (0, 0))],
        out_shape=[jax.ShapeDtypeStruct((s, d), F32), jax.ShapeDtypeStruct((8, d), F32)],
        scratch_shapes=[pltpu.VMEM((tm, d), F32)],
        compiler_params=_params(("arbitrary", "arbitrary"), 48))(dproj, w, x, dxo, g)


def _place():
    x, y, c = lax.axis_index("x"), lax.axis_index("y"), lax.axis_index("c")
    chips = [(1 - x, y), (x, 1 - y), (1 - x, 1 - y)]
    return x, y, c, chips


def _comm_call(body, name, ins, out_shape, n_remote, n_local, aliases=None):
    return pl.pallas_call(
        body, name=name, in_specs=[ANY] * len(ins), out_specs=[ANY] * len(out_shape), out_shape=out_shape,
        input_output_aliases=aliases or {},
        scratch_shapes=[pltpu.SemaphoreType.DMA((n_remote,)), pltpu.SemaphoreType.DMA((n_remote,)),
                        pltpu.SemaphoreType.DMA((n_local,))])(*ins)


def _rcopy(src, dst, send_sems, recv_sems, k, dev):
    return pltpu.make_async_remote_copy(src_ref=src, dst_ref=dst, send_sem=send_sems.at[k], recv_sem=recv_sems.at[k],
                                        device_id=dev, device_id_type=MESH)


def _run_comm(comm, name):
    n_ci, n_co = len(comm.ins), len(comm.outs)

    def body(*refs):
        cins, couts = refs[:n_ci], refs[n_ci:n_ci + n_co]
        send_sems, recv_sems = refs[n_ci + n_co:]
        comm.start(cins, couts, send_sems, recv_sems)
        comm.finish(cins, couts, send_sems, recv_sems)

    return pl.pallas_call(
        body, name=name, in_specs=[ANY] * n_ci, out_specs=[ANY] * n_co, out_shape=list(comm.outs),
        input_output_aliases=dict(comm.aliases),
        scratch_shapes=[pltpu.SemaphoreType.DMA((comm.n_sems,)), pltpu.SemaphoreType.DMA((comm.n_sems,))])(*comm.ins)


def _gather_comm(fulls, rbp=None):
    n = len(fulls)
    n_ici = 3 * n

    def half(full, blk, core):
        hr = full.shape[1] // 2
        return full.at[blk].at[pl.ds(core * hr, hr)]

    def ici(couts, send_sems, recv_sems, x, y, c, chips):
        b = 2 * x + y
        return [_rcopy(half(full, b, c), half(full, b, c), send_sems, recv_sems, 3 * t + j, (*chip, c))
                for t, full in enumerate(couts[:n]) for j, chip in enumerate(chips)]

    def small(cins, couts, send_sems, recv_sems, x, y, c, chips):
        b = 2 * x + y
        return ([_rcopy(cins[n], couts[n].at[b], send_sems, recv_sems, 2 * n_ici + j, (*chip, c))
                 for j, chip in enumerate(chips)],
                pltpu.make_async_copy(cins[n], couts[n].at[b], send_sems.at[2 * n_ici + 3]))

    def start(cins, couts, send_sems, recv_sems):
        place = _place()
        for cp in ici(couts, send_sems, recv_sems, *place):
            cp.start()
        if rbp is not None:
            remote, local = small(cins, couts, send_sems, recv_sems, *place)
            for cp in remote:
                cp.start()
            local.start()

    def finish(cins, couts, send_sems, recv_sems):
        x, y, c, chips = _place()
        me, sib = (x, y, c), (x, y, 1 - c)
        passed = []
        for t, full in enumerate(couts[:n]):
            for j, chip in enumerate(chips):
                got = half(full, 2 * chip[0] + chip[1], c)
                _rcopy(got, got, send_sems, recv_sems, 3 * t + j, me).wait_recv()
                cp = _rcopy(got, got, send_sems, recv_sems, n_ici + 3 * t + j, sib)
                cp.start()
                passed.append(cp)
        for t, full in enumerate(couts[:n]):
            for j, chip in enumerate(chips):
                got = half(full, 2 * chip[0] + chip[1], 1 - c)
                _rcopy(got, got, send_sems, recv_sems, n_ici + 3 * t + j, me).wait_recv()
        for cp in ici(couts, send_sems, recv_sems, x, y, c, chips) + passed:
            cp.wait_send()
        if rbp is not None:
            remote, local = small(cins, couts, send_sems, recv_sems, x, y, c, chips)
            for j, chip in enumerate(chips):
                got = couts[n].at[2 * chip[0] + chip[1]]
                _rcopy(got, got, send_sems, recv_sems, 2 * n_ici + j, me).wait_recv()
            for cp in remote:
                cp.wait_send()
            local.wait()

    outs = [jax.ShapeDtypeStruct(f.shape, f.dtype) for f in fulls]
    ins = list(fulls)
    if rbp is not None:
        ins.append(rbp)
        outs.append(jax.ShapeDtypeStruct((4,) + rbp.shape, F32))
    return _Comm(tuple(ins), tuple(outs), {t: t for t in range(n)}, 2 * n_ici + 4, start, finish)


def _chips_comm(sums):
    n = len(sums)

    def copies(cins, couts, send_sems, recv_sems):
        x, y, c, chips = _place()
        return [_rcopy(cins[t].at[2 * chip[0] + chip[1]], couts[t].at[j], send_sems, recv_sems, 3 * t + j, (*chip, c))
                for t in range(n) for j, chip in enumerate(chips)]

    def start(*refs):
        for cp in copies(*refs):
            cp.start()

    def finish(*refs):
        for cp in copies(*refs):
            cp.wait()

    outs = tuple(jax.ShapeDtypeStruct((3,) + p.shape[1:], p.dtype) for p in sums)
    return _Comm(tuple(sums), outs, {}, 3 * n, start, finish)


def _reduce_sibling(parts):
    n = len(parts)

    def body(*refs):
        ins, theirs = refs[:n], refs[n:2 * n]
        send_sems, recv_sems, _ = refs[2 * n:]
        x, y, c, _ = _place()
        sent = []
        for t in range(n):
            hr = ins[t].shape[1] // 2
            cp = _rcopy(ins[t].at[:, pl.ds((1 - c) * hr, hr), :], theirs[t], send_sems, recv_sems, t, (x, y, 1 - c))
            cp.start()
            sent.append(cp)
        for cp in sent:
            cp.wait()

    half = [jax.ShapeDtypeStruct((p.shape[0], p.shape[1] // 2, p.shape[2]), p.dtype) for p in parts]
    return _comm_call(body, "reduce_sibling", parts, half, n, 1)


def _share_halves(tots):
    n = len(tots)

    def body(*refs):
        g = refs[n:2 * n]
        send_sems, recv_sems, _ = refs[2 * n:]
        x, y, c, _ = _place()
        sent = []
        for t in range(n):
            cp = _rcopy(g[t].at[c], g[t].at[c], send_sems, recv_sems, t, (x, y, 1 - c))
            cp.start()
            sent.append(cp)
        for cp in sent:
            cp.wait()

    out_shape = [jax.ShapeDtypeStruct(t.shape, t.dtype) for t in tots]
    return _comm_call(body, "share_halves", tots, out_shape, n, 1, aliases={t: t for t in range(n)})


def _gather_small(packed):
    def body(p_ref, all_ref, send_sems, recv_sems, loc_sems):
        x, y, c, _ = _place()
        me = 4 * x + 2 * y + c
        local = pltpu.make_async_copy(p_ref, all_ref.at[me], loc_sems.at[0])
        local.start()
        sent = []
        for k in range(1, 8):
            px, py, pc = x ^ (k >> 2), y ^ ((k >> 1) & 1), c ^ (k & 1)
            cp = _rcopy(p_ref, all_ref.at[me], send_sems, recv_sems, k - 1, (px, py, pc))
            cp.start()
            sent.append(cp)
        for k in range(1, 8):
            px, py, pc = x ^ (k >> 2), y ^ ((k >> 1) & 1), c ^ (k & 1)
            got = all_ref.at[4 * px + 2 * py + pc]
            _rcopy(got, got, send_sems, recv_sems, k - 1, (x, y, c)).wait_recv()
        for cp in sent:
            cp.wait_send()
        local.wait()

    return _comm_call(body, "gather_small", [packed], [jax.ShapeDtypeStruct((8,) + packed.shape, F32)], 7, 1)[0]


def _sum_devices(allp):
    n, r, c = allp.shape

    def body(a_ref, o_ref):
        acc = a_ref[0]
        for k in range(1, n):
            acc = acc + a_ref[k]
        o_ref[...] = acc

    return pl.pallas_call(body, name="sum_devices", out_shape=jax.ShapeDtypeStruct((r, c), F32))(allp)


def _ext_index():
    u = np.arange(EXT)
    dist = np.where(u < WIN, PAD - u, PAD + EXT - u)
    return np.clip(dist, -(CHUNK - 1), REL_CLIP) + (CHUNK - 1)


def _pack(parts, rows):
    flat = jnp.concatenate([p.reshape(-1) for p in parts])
    return jnp.pad(flat, (0, rows * 128 - flat.shape[0])).reshape(rows, 128)


def _unpack(packed, shapes):
    flat, out, at = packed.reshape(-1), [], 0
    for shp in shapes:
        size = int(np.prod(shp))
        out.append(flat[at:at + size].reshape(shp))
        at += size
    return out


def kernel(x, norm_g, w_in, q_norm_g, k_norm_g, rel_bias, w_out, loss_target, m_norm_g, m_w_in, m_q_norm_g, m_k_norm_g, m_rel_bias, m_w_out, v_norm_g, v_w_in, v_q_norm_g, v_k_norm_g, v_rel_bias, v_w_out):
    nl, d, nb = w_in.shape
    s = x.shape[1]
    ds_ = d // 2
    nh = ds_ // HEAD
    rb = w_out.shape[1]
    nrel = rel_bias.shape[2]
    bx = lax.axis_index("x") * 2 + lax.axis_index("y")

    rb_rows = -(-(nl * nh * nrel) // 1024) * 8
    cx = lax.axis_index("c")
    wi_full = [_cast_block(w_in, l, bx, "cast_w_in") for l in range(nl)]
    wo_full = [_cast_block(w_out, l, bx, "cast_w_out") for l in range(nl)]
    wi_full[0], wo_full[0], rel_all = _run_comm(
        _gather_comm([wi_full[0], wo_full[0]], _pack([rel_bias], rb_rows)), "gather_first")
    rel_full = jnp.concatenate(
        [rel_all[j].reshape(-1)[:nl * nh * nrel].reshape(nl, nh, nrel) for j in range(4)], axis=2)
    ext_idx = _ext_index()
    onehot = jnp.asarray(ext_idx[:, None] == np.arange(N_REL)[None, :], F32)
    ext = jnp.einsum("lhr,ur->lhu", rel_full, onehot, precision=lax.Precision.HIGHEST).reshape(nl, nh, 1, EXT)

    xs, hs, projs, yas, lts, ybs, mixes = [], [], [], [], [], [], []
    xc = x[0]
    for l in range(nl):
        h = _rmsnorm_fwd(xc, norm_g[l:l + 1])
        proj = _in_proj(h, wi_full[l])
        nxt = l + 1 < nl
        (ya, lt), got = _attn_a_fwd(proj, nh, _gather_comm([wi_full[l + 1]]) if nxt else None)
        if nxt:
            wi_full[l + 1] = got[0]
        (yb,), got = _attn_b_fwd(proj, q_norm_g[l:l + 1], k_norm_g[l:l + 1], ext[l], nh,
                                 _gather_comm([wo_full[l + 1]]) if nxt else None)
        if nxt:
            wo_full[l + 1] = got[0]
        xs.append(xc)
        xc, mix = _out_proj(xc, ya, yb, proj, wo_full[l].reshape(4 * rb, d))
        hs.append(h); projs.append(proj); yas.append(ya); lts.append(lt); ybs.append(yb); mixes.append(mix)
    dx, loss_tile = _loss_head(xc, loss_target[0])

    small, g_wi, g_wo = [None] * nl, [None] * nl, [None] * nl
    pending = None

    def finish_layer(lay, sums, got):
        tots = _share_halves([_add_chips(sums[t], got[t], bx, cx) for t in range(2)])
        g_wi[lay], g_wo[lay] = tots[0].reshape(d, nb), tots[1].reshape(rb, d)

    for l in reversed(range(nl)):
        wo = wo_full[l].reshape(4 * rb, d)
        p_wo = _wgrad(mixes[l], dx, 4, False, "wgrad_out")
        dya, dyb, dga, dgb = _out_proj_bwd(dx, yas[l], ybs[l], projs[l], wo)
        (dqa, dka, dva), got = _attn_a_bwd(projs[l], lts[l], dya, nh, _chips_comm(pending[1]) if pending else None)
        if pending:
            finish_layer(*pending, got)
        dqb, dkb, dvb, dqg, dkg, dext = _attn_b_bwd(projs[l], dyb, q_norm_g[l:l + 1], k_norm_g[l:l + 1], ext[l], nh)
        dproj = jnp.concatenate([dqa, dka, dva, dga, dqb, dkb, dvb, dgb], axis=1)
        p_wi = _wgrad(hs[l], dproj, 4, True, "wgrad_in")
        dx, dng = _in_proj_bwd(dproj, wi_full[l], xs[l], dx, norm_g[l:l + 1])
        small[l] = (dng[0], jnp.sum(dqg, axis=0).reshape(-1), jnp.sum(dkg, axis=0).reshape(-1), dext.reshape(nh, EXT))
        parts = [p_wi, p_wo]
        theirs = _reduce_sibling(parts)
        pending = (l, [_add_sibling(parts[t], theirs[t], cx) for t in range(2)])
    grad_x = dx[None]
    finish_layer(*pending, _run_comm(_chips_comm(pending[1]), "reduce_chips"))

    small_shapes = [(nl, d), (nl, HEAD), (nl, HEAD), (nl, nh, EXT), (1,)]
    small_parts = [jnp.stack([sm[i] for sm in small]) for i in range(4)] + [loss_tile[0, :1]]
    rows = -(-sum(int(np.prod(sh)) for sh in small_shapes) // 1024) * 8
    tot = _sum_devices(_gather_small(_pack(small_parts, rows)))
    g_ng, g_qg, g_kg, g_ext, loss = _unpack(tot, small_shapes)
    g_rel_full = jnp.einsum("lhu,ur->lhr", g_ext, onehot, precision=lax.Precision.HIGHEST)
    g_rel = lax.dynamic_slice_in_dim(g_rel_full, bx * nrel, nrel, axis=2)

    res_wi, res_wo = (), ()
    for l in range(nl):
        res_wi = _adamw_layer(l, w_in, g_wi[l], m_w_in, v_w_in, res_wi, "adamw_w_in")
        res_wo = _adamw_layer(l, w_out, g_wo[l], m_w_out, v_w_out, res_wo, "adamw_w_out")
    g_wi, d_wi, nm_wi, nv_wi = res_wi
    g_wo, d_wo, nm_wo, nv_wo = res_wo
    sm_shapes = [(nl, d), (nl, HEAD), (nl, HEAD), (nl, nh, nrel)]
    sm_rows = -(-sum(int(np.prod(sh)) for sh in sm_shapes) // 1024) * 8
    pw, pg, pm, pv = [_pack(group, sm_rows) for group in (
        (norm_g, q_norm_g, k_norm_g, rel_bias), (g_ng, g_qg, g_kg, g_rel),
        (m_norm_g, m_q_norm_g, m_k_norm_g, m_rel_bias), (v_norm_g, v_q_norm_g, v_k_norm_g, v_rel_bias))]
    d_sm, nm_sm, nv_sm = [_unpack(a[0], sm_shapes)
                          for a in _adamw_layer(0, pw[None], pg, pm[None], pv[None], (), "adamw_small")[1:]]

    return (loss[0], grad_x, g_ng, g_wi, g_qg, g_kg, g_rel, g_wo,
            d_sm[0], d_wi, d_sm[1], d_sm[2], d_sm[3], d_wo,
            nm_sm[0], nm_wi, nm_sm[1], nm_sm[2], nm_sm[3], nm_wo,
            nv_sm[0], nv_wi, nv_sm[1], nv_sm[2], nv_sm[3], nv_wo)
```

```python
from typing import Callable, NamedTuple

import jax
import jax.numpy as jnp
import numpy as np
from jax import lax
from jax.experimental import pallas as pl
from jax.experimental.pallas import tpu as pltpu

F32 = jnp.float32
BF16 = jnp.bfloat16

HEAD = 128
CHUNK = 64
LEFT_CHUNKS = 8
REL_CLIP = 256
N_REL = REL_CLIP + CHUNK
NORM_EPS = 1e-6
NEG_BIG = -1e30
TQ = 256
PAD = LEFT_CHUNKS * CHUNK
WIN = PAD + TQ
EXT = 1024
SCALE = HEAD ** -0.5

ADAM_LR = 0.001
ADAM_B1 = 0.9
ADAM_B2 = 0.999
ADAM_EPS = 1e-08
ADAM_WD = 0.01
ADAM_STEP = 10

ANY = pl.BlockSpec(memory_space=pl.ANY)
MESH = pl.DeviceIdType.MESH


def _params(sem=None, vmem_mb=None):
    kw = {}
    if sem is not None:
        kw["dimension_semantics"] = sem
    if vmem_mb is not None:
        kw["vmem_limit_bytes"] = vmem_mb << 20
    return pltpu.CompilerParams(**kw)


class _Comm(NamedTuple):
    ins: tuple
    outs: tuple
    aliases: dict
    n_sems: int
    start: Callable
    finish: Callable


def _call(body, *, name, grid, in_specs, out_specs, out_shape, args, scratch=(), sem=None, vmem_mb=None, comm=None):
    if comm is None:
        out = pl.pallas_call(body, name=name, grid=grid, in_specs=in_specs, out_specs=out_specs, out_shape=out_shape,
                             scratch_shapes=list(scratch), compiler_params=_params(sem, vmem_mb))(*args)
        return out, ()
    n_in, n_out, n_ci, n_co = len(in_specs), len(out_shape), len(comm.ins), len(comm.outs)

    def hosted(*refs):
        ins, cins = refs[:n_in], refs[n_in:n_in + n_ci]
        outs, couts = refs[n_in + n_ci:n_in + n_ci + n_out], refs[n_in + n_ci + n_out:n_in + n_ci + n_out + n_co]
        rest = refs[n_in + n_ci + n_out + n_co:]
        send_sems, recv_sems = rest[-2:]
        first, last = None, None
        for ax, size in enumerate(grid):
            at = pl.program_id(ax)
            first = (at == 0) if first is None else first & (at == 0)
            last = (at == size - 1) if last is None else last & (at == size - 1)

        @pl.when(first)
        def _():
            comm.start(cins, couts, send_sems, recv_sems)

        body(*ins, *outs, *rest[:-2])

        @pl.when(last)
        def _():
            comm.finish(cins, couts, send_sems, recv_sems)

    out = pl.pallas_call(
        hosted, name=name, grid=grid, in_specs=list(in_specs) + [ANY] * n_ci, out_specs=list(out_specs) + [ANY] * n_co,
        out_shape=list(out_shape) + list(comm.outs),
        input_output_aliases={n_in + k: n_out + v for k, v in comm.aliases.items()},
        scratch_shapes=list(scratch) + [pltpu.SemaphoreType.DMA((comm.n_sems,)), pltpu.SemaphoreType.DMA((comm.n_sems,))],
        compiler_params=_params(("arbitrary",) * len(grid), vmem_mb))(*args, *comm.ins)
    return out[:n_out], out[n_out:]


def _dot(a, b):
    return jnp.dot(a, b, preferred_element_type=F32)


def _dot_nt(a, b):
    return lax.dot_general(a, b, (((1,), (1,)), ((), ())), preferred_element_type=F32)


def _dot_tn(a, b):
    return lax.dot_general(a, b, (((0,), (0,)), ((), ())), preferred_element_type=F32)


def _split_dot(x, m):
    hi = x.astype(BF16)
    lo = (x - hi.astype(F32)).astype(BF16)
    return _dot(hi, m) + _dot(lo, m)


def _silu_parts(g):
    sg = 1.0 / (1.0 + jnp.exp(-g))
    return g * sg, sg * (1.0 + g * (1.0 - sg))


def _idx(*vals):
    return jnp.stack([jnp.asarray(v, jnp.int32) for v in vals])


def _cast_block(w, l, blk, name):
    _, r, c = w.shape
    tr = min(r, 512)

    def body(b_ref, w_ref, o_ref):
        o_ref[...] = w_ref[...].astype(BF16)

    spec = pltpu.PrefetchScalarGridSpec(
        num_scalar_prefetch=1, grid=(r // tr,),
        in_specs=[pl.BlockSpec((None, tr, c), lambda i, b: (l, i, 0))],
        out_specs=pl.BlockSpec((None, tr, c), lambda i, b: (b[0], i, 0)))
    return pl.pallas_call(body, name=name, grid_spec=spec, out_shape=jax.ShapeDtypeStruct((4, r, c), BF16),
                          compiler_params=_params(("parallel",)))(_idx(blk), w)


def _add_sibling(p, theirs, core):
    nblk, r, c = p.shape
    hr = r // 2
    tr = min(hr, 256)
    per = hr // tr

    def body(c_ref, p_ref, t_ref, o_ref):
        o_ref[...] = (p_ref[...].astype(F32) + t_ref[...].astype(F32)).astype(BF16)

    blk = pl.BlockSpec((None, tr, c), lambda j, i, cr: (j, i, 0))
    spec = pltpu.PrefetchScalarGridSpec(
        num_scalar_prefetch=1, grid=(nblk, per),
        in_specs=[pl.BlockSpec((None, tr, c), lambda j, i, cr: (j, cr[0] * per + i, 0)), blk], out_specs=blk)
    return pl.pallas_call(body, name="add_sibling", grid_spec=spec, out_shape=jax.ShapeDtypeStruct((nblk, hr, c), BF16),
                          compiler_params=_params(("parallel", "parallel")))(_idx(core), p, theirs)


def _add_chips(sums, got, blk, core):
    _, hr, c = sums.shape
    tr = min(hr, 256)

    def body(i_ref, s_ref, g0_ref, g1_ref, g2_ref, o_ref):
        o_ref[...] = ((s_ref[...].astype(F32) + g0_ref[...].astype(F32))
                      + g1_ref[...].astype(F32)) + g2_ref[...].astype(F32)

    at = lambda j: pl.BlockSpec((None, tr, c), lambda i, ir: (j, i, 0))
    spec = pltpu.PrefetchScalarGridSpec(
        num_scalar_prefetch=1, grid=(hr // tr,),
        in_specs=[pl.BlockSpec((None, tr, c), lambda i, ir: (ir[0], i, 0)), at(0), at(1), at(2)],
        out_specs=pl.BlockSpec((None, tr, c), lambda i, ir: (ir[1], i, 0)))
    return pl.pallas_call(body, name="add_chips", grid_spec=spec, out_shape=jax.ShapeDtypeStruct((2, hr, c), F32),
                          compiler_params=_params(("parallel",)))(_idx(blk, core), sums, got, got, got)


def _adamw_layer(l, w, g, m, v, prev, name):
    nl, r, c = w.shape
    tr = min(r, 256)
    c1 = 1.0 / (1.0 - ADAM_B1 ** ADAM_STEP)
    c2 = 1.0 / (1.0 - ADAM_B2 ** ADAM_STEP)

    def body(w_ref, g_ref, m_ref, v_ref, *rest):
        go_ref, d_ref, nm_ref, nv_ref = rest[-4:]
        gg = g_ref[...]
        nm = ADAM_B1 * m_ref[...] + (1.0 - ADAM_B1) * gg
        nv = ADAM_B2 * v_ref[...] + (1.0 - ADAM_B2) * (gg * gg)
        upd = (nm * c1) / (jnp.sqrt(nv * c2) + ADAM_EPS) + ADAM_WD * w_ref[...]
        go_ref[...] = gg
        d_ref[...] = -ADAM_LR * upd
        nm_ref[...] = nm
        nv_ref[...] = nv

    lay = pl.BlockSpec((None, tr, c), lambda i: (l, i, 0))
    shp = jax.ShapeDtypeStruct((nl, r, c), F32)
    return pl.pallas_call(
        body, name=name, grid=(r // tr,),
        in_specs=[lay, pl.BlockSpec((tr, c), lambda i: (i, 0)), lay, lay] + [ANY] * len(prev),
        out_specs=[lay] * 4, out_shape=[shp] * 4, input_output_aliases={4 + k: k for k in range(len(prev))},
        compiler_params=_params(("parallel",), 40))(w, g, m, v, *prev)


def _rmsnorm_fwd(x, g):
    s, d = x.shape
    tm = min(s, 256)

    def body(x_ref, g_ref, h_ref):
        xv = x_ref[...]
        r = lax.rsqrt(jnp.mean(xv * xv, axis=1, keepdims=True) + NORM_EPS)
        h_ref[...] = (xv * r * g_ref[...]).astype(BF16)

    return pl.pallas_call(
        body, name="rmsnorm_fwd", grid=(s // tm,),
        in_specs=[pl.BlockSpec((tm, d), lambda i: (i, 0)), pl.BlockSpec((1, d), lambda i: (0, 0))],
        out_specs=pl.BlockSpec((tm, d), lambda i: (i, 0)),
        out_shape=jax.ShapeDtypeStruct((s, d), BF16),
        compiler_params=_params(("parallel",)))(x, g)


def _in_proj(h, w, comm=None):
    s, d = h.shape
    nblk, _, nb = w.shape
    tm, tn = min(s, 512), min(nb, 1024)
    per = nb // tn

    def body(h_ref, w_ref, o_ref):
        o_ref[...] = _dot(h_ref[...], w_ref[...])

    return _call(
        body, name="in_proj", grid=(nblk * per, s // tm),
        in_specs=[pl.BlockSpec((tm, d), lambda n, m: (m, 0)),
                  pl.BlockSpec((None, d, tn), lambda n, m: (n // per, 0, n % per))],
        out_specs=[pl.BlockSpec((tm, tn), lambda n, m: (m, n))],
        out_shape=[jax.ShapeDtypeStruct((s, nblk * nb), F32)],
        sem=("parallel", "parallel"), vmem_mb=40, args=(h, w), comm=comm)


def _heads_per_step(nh):
    return 2 if nh % 2 == 0 else 1


def _head(hh):
    return slice(hh * HEAD, (hh + 1) * HEAD)


def _tri(op):
    r = lax.broadcasted_iota(jnp.int32, (TQ, TQ), 0)
    c = lax.broadcasted_iota(jnp.int32, (TQ, TQ), 1)
    return op(r, c)


def _sb_logs(qb, kb, causal, diag):
    z = _dot_nt(qb, kb) * SCALE
    e = jnp.exp(-jnp.abs(z))
    l1p = jnp.log(1.0 + e)
    ls = jnp.minimum(-z, 0.0) - l1p
    lsig = jnp.minimum(z, 0.0) - l1p
    if diag:
        ls = jnp.where(causal, ls, 0.0)
    return z, e, ls, lsig


def _attn_a_fwd(proj, nh, comm=None):
    s = proj.shape[0]
    nq = s // TQ
    hp = _heads_per_step(nh)
    ng = nh // hp

    def body(q_ref, k_ref, v_ref, o_ref, lt_ref, acc_ref):
        i = pl.program_id(1)
        qbs = [q_ref[:, _head(hh)].astype(BF16) for hh in range(hp)]
        causal = _tri(lambda r, c: r > c)
        m_after = causal.astype(BF16)
        acc_ref[...] = jnp.zeros_like(acc_ref)

        def tile(j, carries, diag):
            off = pl.multiple_of(j * TQ, TQ)
            out = []
            for hh in range(hp):
                kb = k_ref[pl.ds(off, TQ), _head(hh)].astype(BF16)
                vb = v_ref[pl.ds(off, TQ), _head(hh)].astype(BF16)
                _, _, ls, lsig = _sb_logs(qbs[hh], kb, causal, diag)
                w = jnp.exp(lsig + _split_dot(ls, m_after) + carries[hh])
                if diag:
                    w = jnp.where(causal, w, 0.0)
                acc_ref[:, _head(hh)] += _dot(w.astype(BF16), vb)
                out.append(carries[hh] + jnp.sum(ls, axis=1, keepdims=True))
            return tuple(out)

        carries = tile(i, (jnp.zeros((TQ, 1), F32),) * hp, True)
        carries = lax.fori_loop(0, i, lambda t, ca: tile(i - 1 - t, ca, False), carries)
        o_ref[...] = acc_ref[...]
        for hh in range(hp):
            lt_ref[hh] = carries[hh]

    wd = hp * HEAD
    return _call(
        body, name="attn_a_fwd", grid=(ng, nq),
        in_specs=[pl.BlockSpec((TQ, wd), lambda h, i: (i, h)),
                  pl.BlockSpec((s, wd), lambda h, i: (0, ng + h)),
                  pl.BlockSpec((s, wd), lambda h, i: (0, 2 * ng + h))],
        out_specs=[pl.BlockSpec((TQ, wd), lambda h, i: (i, h)),
                   pl.BlockSpec((hp, TQ, 1), lambda h, i: (h, i, 0))],
        out_shape=[jax.ShapeDtypeStruct((s, nh * HEAD), F32), jax.ShapeDtypeStruct((nh, s, 1), F32)],
        scratch=[pltpu.VMEM((TQ, wd), F32)], sem=("parallel", "arbitrary"), args=(proj, proj, proj), comm=comm)


def _band_valid(i):
    cl = lax.broadcasted_iota(jnp.int32, (TQ, WIN), 0) // CHUNK
    kl = lax.broadcasted_iota(jnp.int32, (TQ, WIN), 1) // CHUNK
    first = LEFT_CHUNKS - (TQ // CHUNK) * i
    return (kl >= cl) & (kl <= cl + LEFT_CHUNKS) & (kl >= first)


def _build_bias(e_ref, bias_ref):
    e8 = jnp.broadcast_to(e_ref[...], (8, EXT))
    row = lax.broadcasted_iota(jnp.int32, (8, EXT), 0)
    t8 = jnp.zeros((8, EXT), F32)
    for b in range(8):
        t8 = jnp.where(row == b, pltpu.roll(e8, b, 1) if b else e8, t8)
    for a in range(TQ // 8):
        sl = pltpu.roll(t8, 8 * a, 1) if a else t8
        bias_ref[pl.ds(8 * a, 8), :] = sl[:, :WIN]


def _reduce_bias_grad(db_ref):
    acc = jnp.zeros((8, EXT), F32)
    for a in range(TQ // 8):
        sl = db_ref[pl.ds(8 * a, 8), :]
        acc = acc + (pltpu.roll(sl, EXT - 8 * a, 1) if a else sl)
    row = lax.broadcasted_iota(jnp.int32, (8, EXT), 0)
    tot = jnp.zeros((8, EXT), F32)
    for b in range(8):
        tot = tot + jnp.where(row == b, pltpu.roll(acc, EXT - b, 1) if b else acc, 0.0)
    return jnp.sum(tot, axis=0, keepdims=True)


def _band_fill(k_ref, v_ref, kg_ref, kn_pad, v_pad, s):
    k = k_ref[...]
    rk = lax.rsqrt(jnp.mean(k * k, axis=1, keepdims=True) + NORM_EPS)
    kn_pad[pl.ds(0, PAD), :] = jnp.zeros((PAD, HEAD), BF16)
    kn_pad[pl.ds(PAD, s), :] = (k * rk * kg_ref[...]).astype(BF16)
    v_pad[pl.ds(0, PAD), :] = jnp.zeros((PAD, HEAD), BF16)
    v_pad[pl.ds(PAD, s), :] = v_ref[...].astype(BF16)


def _band_probs(q_ref, qg_ref, kn_pad, bias_ref, i):
    q = q_ref[...]
    rq = lax.rsqrt(jnp.mean(q * q, axis=1, keepdims=True) + NORM_EPS)
    qhat = q * rq
    qn = (qhat * qg_ref[...]).astype(BF16)
    off = pl.multiple_of(i * TQ, TQ)
    kw = kn_pad[pl.ds(off, WIN), :]
    sc = _dot_nt(qn, kw) * SCALE + bias_ref[...]
    sc = jnp.where(_band_valid(i), sc, NEG_BIG)
    p = jnp.exp(sc - jnp.max(sc, axis=1, keepdims=True))
    pn = p / jnp.sum(p, axis=1, keepdims=True)
    return rq, qhat, qn, kw, off, pn


def _attn_b_fwd(proj, qg, kg, ext, nh, comm=None):
    s = proj.shape[0]
    nq = s // TQ

    def body(q_ref, k_ref, v_ref, qg_ref, kg_ref, e_ref, o_ref, kn_pad, v_pad, bias_ref):
        i = pl.program_id(1)

        @pl.when(i == 0)
        def _():
            _band_fill(k_ref, v_ref, kg_ref, kn_pad, v_pad, s)
            _build_bias(e_ref, bias_ref)

        _, _, _, _, off, pn = _band_probs(q_ref, qg_ref, kn_pad, bias_ref, i)
        o_ref[...] = _dot(pn.astype(BF16), v_pad[pl.ds(off, WIN), :])

    vec = pl.BlockSpec((1, HEAD), lambda h, i: (0, 0))
    return _call(
        body, name="attn_b_fwd", grid=(nh, nq),
        in_specs=[pl.BlockSpec((TQ, HEAD), lambda h, i: (i, 4 * nh + h)),
                  pl.BlockSpec((s, HEAD), lambda h, i: (0, 5 * nh + h)),
                  pl.BlockSpec((s, HEAD), lambda h, i: (0, 6 * nh + h)),
                  vec, vec,
                  pl.BlockSpec((None, 1, EXT), lambda h, i: (h, 0, 0))],
        out_specs=[pl.BlockSpec((TQ, HEAD), lambda h, i: (i, h))],
        out_shape=[jax.ShapeDtypeStruct((s, nh * HEAD), F32)],
        scratch=[pltpu.VMEM((s + PAD, HEAD), BF16), pltpu.VMEM((s + PAD, HEAD), BF16), pltpu.VMEM((TQ, WIN), F32)],
        sem=("parallel", "arbitrary"), args=(proj, proj, proj, qg, kg, ext), comm=comm)


def _out_proj(x, ya, yb, proj, w):
    s, d = x.shape
    ds_ = ya.shape[1]
    tm = min(s, 256)

    def body(x_ref, ya_ref, yb_ref, ga_ref, gb_ref, w_ref, o_ref, mix_ref):
        ma = (ya_ref[...] * _silu_parts(ga_ref[...])[0]).astype(BF16)
        mb = (yb_ref[...] * _silu_parts(gb_ref[...])[0]).astype(BF16)
        mix_ref[:, :ds_] = ma
        mix_ref[:, ds_:] = mb
        o_ref[...] = x_ref[...] + _dot(ma, w_ref[pl.ds(0, ds_), :]) + _dot(mb, w_ref[pl.ds(ds_, ds_), :])

    row = lambda width: pl.BlockSpec((tm, width), lambda i: (i, 0))
    return pl.pallas_call(
        body, name="out_proj", grid=(s // tm,),
        in_specs=[row(d), row(ds_), row(ds_),
                  pl.BlockSpec((tm, ds_), lambda i: (i, 3)), pl.BlockSpec((tm, ds_), lambda i: (i, 7)),
                  pl.BlockSpec((2 * ds_, d), lambda i: (0, 0))],
        out_specs=[row(d), row(2 * ds_)],
        out_shape=[jax.ShapeDtypeStruct((s, d), F32), jax.ShapeDtypeStruct((s, 2 * ds_), BF16)],
        compiler_params=_params(("parallel",), 48))(x, ya, yb, proj, proj, w)


def _loss_head(y, tgt):
    s, d = y.shape
    tm = min(s, 256)

    def body(y_ref, t_ref, dy_ref, l_ref):
        @pl.when(pl.program_id(0) == 0)
        def _():
            l_ref[...] = jnp.zeros_like(l_ref)

        err = y_ref[...] - t_ref[...]
        dy_ref[...] = err * (1.0 / d)
        l_ref[...] += 0.5 * jnp.sum(jnp.mean(err * err, axis=1, keepdims=True), axis=0, keepdims=True)

    row = pl.BlockSpec((tm, d), lambda i: (i, 0))
    return pl.pallas_call(
        body, name="loss_head", grid=(s // tm,), in_specs=[row, row],
        out_specs=[row, pl.BlockSpec((8, 128), lambda i: (0, 0))],
        out_shape=[jax.ShapeDtypeStruct((s, d), F32), jax.ShapeDtypeStruct((8, 128), F32)],
        compiler_params=_params(("arbitrary",)))(y, tgt)


def _out_proj_bwd(dxo, ya, yb, proj, w):
    s, d = dxo.shape
    ds_ = ya.shape[1]
    tm = min(s, 256)

    def body(dx_ref, ya_ref, yb_ref, ga_ref, gb_ref, w_ref, dya_ref, dyb_ref, dga_ref, dgb_ref):
        dxb = dx_ref[...].astype(BF16)
        for y_ref, g_ref, lo, dy_ref, dg_ref in ((ya_ref, ga_ref, 0, dya_ref, dga_ref),
                                                 (yb_ref, gb_ref, ds_, dyb_ref, dgb_ref)):
            dmix = _dot_nt(dxb, w_ref[pl.ds(lo, ds_), :])
            act, dact = _silu_parts(g_ref[...])
            dy_ref[...] = dmix * act
            dg_ref[...] = (dmix * y_ref[...] * dact).astype(BF16)

    row = lambda width: pl.BlockSpec((tm, width), lambda i: (i, 0))
    return pl.pallas_call(
        body, name="out_proj_bwd", grid=(s // tm,),
        in_specs=[row(d), row(ds_), row(ds_),
                  pl.BlockSpec((tm, ds_), lambda i: (i, 3)), pl.BlockSpec((tm, ds_), lambda i: (i, 7)),
                  pl.BlockSpec((2 * ds_, d), lambda i: (0, 0))],
        out_specs=[row(ds_)] * 4,
        out_shape=[jax.ShapeDtypeStruct((s, ds_), F32)] * 2 + [jax.ShapeDtypeStruct((s, ds_), BF16)] * 2,
        compiler_params=_params(("parallel",), 48))(dxo, ya, yb, proj, proj, w)


def _wgrad(a, b, nblk, col_blocks, name):
    s, m = a.shape
    n = b.shape[1]
    if col_blocks:
        tr = min(m, 512)
        nb = n // nblk
        tn = min(nb, 1024)
        per = nb // tn
        out_shape = (nblk, m, nb)
        out_spec = pl.BlockSpec((None, tr, tn), lambda j, r: (j // per, r, j % per))
    else:
        tn = min(n, 1024)
        tr = m // nblk
        out_shape = (nblk, tr, n)
        out_spec = pl.BlockSpec((None, tr, tn), lambda j, r: (r, 0, j))

    def body(a_ref, b_ref, o_ref):
        o_ref[...] = _dot_tn(a_ref[...].astype(BF16), b_ref[...].astype(BF16)).astype(BF16)

    return pl.pallas_call(
        body, name=name, grid=(n // tn, m // tr),
        in_specs=[pl.BlockSpec((s, tr), lambda j, r: (0, r)), pl.BlockSpec((s, tn), lambda j, r: (0, j))],
        out_specs=out_spec, out_shape=jax.ShapeDtypeStruct(out_shape, BF16),
        compiler_params=_params(("parallel", "parallel"), 48))(a, b)


def _attn_a_bwd(proj, lt, dya, nh, comm=None):
    s = proj.shape[0]
    nq = s // TQ
    hp = _heads_per_step(nh)
    ng = nh // hp

    def body(q_ref, k_ref, v_ref, lt_ref, do_ref, dq_ref, dk_ref, dv_ref, dq_acc, dk_acc, dv_acc):
        i = pl.program_id(1)

        @pl.when(i == 0)
        def _():
            dk_acc[...] = jnp.zeros_like(dk_acc)
            dv_acc[...] = jnp.zeros_like(dv_acc)

        dq_acc[...] = jnp.zeros_like(dq_acc)
        qbs = [q_ref[:, _head(hh)].astype(BF16) for hh in range(hp)]
        dobs = [do_ref[:, _head(hh)].astype(BF16) for hh in range(hp)]
        totals = [lt_ref[hh] for hh in range(hp)]
        causal = _tri(lambda r, c: r > c)
        m_upto = _tri(lambda r, c: r <= c).astype(BF16)
        m_before = _tri(lambda r, c: r < c).astype(BF16)

        def tile(j, carries, diag):
            off = pl.multiple_of(j * TQ, TQ)
            out = []
            for hh in range(hp):
                c_ls, c_g = carries[2 * hh], carries[2 * hh + 1]
                kb = k_ref[pl.ds(off, TQ), _head(hh)].astype(BF16)
                vb = v_ref[pl.ds(off, TQ), _head(hh)].astype(BF16)
                z, e, ls, lsig = _sb_logs(qbs[hh], kb, causal, diag)
                w = jnp.exp(lsig + (totals[hh] - (c_ls + _split_dot(ls, m_upto))))
                if diag:
                    w = jnp.where(causal, w, 0.0)
                g = w * _dot_nt(dobs[hh], vb)
                before = c_g + _dot(g.astype(BF16), m_before)
                rinv = 1.0 / (1.0 + e)
                beta = jnp.where(z >= 0.0, rinv, e * rinv)
                dz = g * (1.0 - beta) - beta * before
                if diag:
                    dz = jnp.where(causal, dz, 0.0)
                dzb = (dz * SCALE).astype(BF16)
                dq_acc[:, _head(hh)] += _dot(dzb, kb)
                dk_acc[pl.ds(off, TQ), _head(hh)] += _dot_tn(dzb, qbs[hh])
                dv_acc[pl.ds(off, TQ), _head(hh)] += _dot_tn(w.astype(BF16), dobs[hh])
                out += [c_ls + jnp.sum(ls, axis=1, keepdims=True), c_g + jnp.sum(g, axis=1, keepdims=True)]
            return tuple(out)

        carries = lax.fori_loop(0, i, lambda j, ca: tile(j, ca, False), (jnp.zeros((TQ, 1), F32),) * (2 * hp))
        tile(i, carries, True)
        dq_ref[...] = dq_acc[...].astype(BF16)

        @pl.when(i == nq - 1)
        def _():
            dk_ref[...] = dk_acc[...].astype(BF16)
            dv_ref[...] = dv_acc[...].astype(BF16)

    wd = hp * HEAD
    blk = pl.BlockSpec((TQ, wd), lambda h, i: (i, h))
    col = pl.BlockSpec((s, wd), lambda h, i: (0, h))
    shp = jax.ShapeDtypeStruct((s, nh * HEAD), BF16)
    return _call(
        body, name="attn_a_bwd", grid=(ng, nq),
        in_specs=[blk,
                  pl.BlockSpec((s, wd), lambda h, i: (0, ng + h)),
                  pl.BlockSpec((s, wd), lambda h, i: (0, 2 * ng + h)),
                  pl.BlockSpec((hp, TQ, 1), lambda h, i: (h, i, 0)), blk],
        out_specs=[blk, col, col], out_shape=[shp] * 3,
        scratch=[pltpu.VMEM((TQ, wd), F32), pltpu.VMEM((s, wd), F32), pltpu.VMEM((s, wd), F32)],
        sem=("parallel", "arbitrary"), args=(proj, proj, proj, lt, dya), comm=comm)


def _attn_b_bwd(proj, dyb, qg, kg, ext, nh, comm=None):
    s = proj.shape[0]
    nq = s // TQ

    def body(q_ref, k_ref, v_ref, do_ref, qg_ref, kg_ref, e_ref,
             dq_ref, dk_ref, dv_ref, dqg_ref, dkg_ref, de_ref,
             kn_pad, v_pad, bias_ref, db_acc, dkn_acc, dv_acc):
        i = pl.program_id(1)

        @pl.when(i == 0)
        def _():
            _band_fill(k_ref, v_ref, kg_ref, kn_pad, v_pad, s)
            _build_bias(e_ref, bias_ref)
            db_acc[...] = jnp.zeros_like(db_acc)
            dkn_acc[...] = jnp.zeros_like(dkn_acc)
            dv_acc[...] = jnp.zeros_like(dv_acc)
            dqg_ref[...] = jnp.zeros_like(dqg_ref)

        rq, qhat, qn, kw, off, pn = _band_probs(q_ref, qg_ref, kn_pad, bias_ref, i)
        dob = do_ref[...].astype(BF16)
        dp = _dot_nt(dob, v_pad[pl.ds(off, WIN), :])
        dsc = pn * (dp - jnp.sum(pn * dp, axis=1, keepdims=True))
        db_acc[:, :WIN] += dsc
        dsb = (dsc * SCALE).astype(BF16)
        dqn = _dot(dsb, kw)
        dkn_acc[pl.ds(off, WIN), :] += _dot_tn(dsb, qn)
        dv_acc[pl.ds(off, WIN), :] += _dot_tn(pn.astype(BF16), dob)
        dqh = dqn * qg_ref[...]
        dq_ref[...] = (rq * (dqh - qhat * jnp.mean(dqh * qhat, axis=1, keepdims=True))).astype(BF16)
        dqg_ref[...] += jnp.sum(dqn * qhat, axis=0, keepdims=True)

        @pl.when(i == nq - 1)
        def _():
            k = k_ref[...]
            rk = lax.rsqrt(jnp.mean(k * k, axis=1, keepdims=True) + NORM_EPS)
            khat = k * rk
            dkn = dkn_acc[pl.ds(PAD, s), :]
            dkh = dkn * kg_ref[...]
            dk_ref[...] = (rk * (dkh - khat * jnp.mean(dkh * khat, axis=1, keepdims=True))).astype(BF16)
            dkg_ref[...] = jnp.sum(dkn * khat, axis=0, keepdims=True)
            dv_ref[...] = dv_acc[pl.ds(PAD, s), :].astype(BF16)
            de_ref[...] = _reduce_bias_grad(db_acc)

    blk = pl.BlockSpec((TQ, HEAD), lambda h, i: (i, h))
    col = pl.BlockSpec((s, HEAD), lambda h, i: (0, h))
    vec = pl.BlockSpec((1, HEAD), lambda h, i: (0, 0))
    hvec = pl.BlockSpec((None, 1, HEAD), lambda h, i: (h, 0, 0))
    hext = pl.BlockSpec((None, 1, EXT), lambda h, i: (h, 0, 0))
    shp = jax.ShapeDtypeStruct((s, nh * HEAD), BF16)
    return _call(
        body, name="attn_b_bwd", grid=(nh, nq),
        in_specs=[pl.BlockSpec((TQ, HEAD), lambda h, i: (i, 4 * nh + h)),
                  pl.BlockSpec((s, HEAD), lambda h, i: (0, 5 * nh + h)),
                  pl.BlockSpec((s, HEAD), lambda h, i: (0, 6 * nh + h)),
                  blk, vec, vec, hext],
        out_specs=[blk, col, col, hvec, hvec, hext],
        out_shape=[shp] * 3 + [jax.ShapeDtypeStruct((nh, 1, HEAD), F32)] * 2
        + [jax.ShapeDtypeStruct((nh, 1, EXT), F32)],
        scratch=[pltpu.VMEM((s + PAD, HEAD), BF16), pltpu.VMEM((s + PAD, HEAD), BF16),
                 pltpu.VMEM((TQ, WIN), F32), pltpu.VMEM((TQ, EXT), F32),
                 pltpu.VMEM((s + PAD, HEAD), F32), pltpu.VMEM((s + PAD, HEAD), F32)],
        sem=("parallel", "arbitrary"), args=(proj, proj, proj, dyb, qg, kg, ext), comm=comm)


def _in_proj_bwd(dproj, w, x, dxo, g, comm=None):
    s, d = x.shape
    nblk, _, nb = w.shape
    tm, tk = min(s, 512), min(nb, 1024)
    per = nb // tk
    nk = nblk * per

    def body(dp_ref, w_ref, x_ref, dxo_ref, g_ref, dx_ref, dg_ref, acc):
        m, k = pl.program_id(0), pl.program_id(1)

        @pl.when(k == 0)
        def _():
            acc[...] = jnp.zeros_like(acc)

        @pl.when((k == 0) & (m == 0))
        def _():
            dg_ref[...] = jnp.zeros_like(dg_ref)

        acc[...] += _dot_nt(dp_ref[...], w_ref[...])

        @pl.when(k == nk - 1)
        def _():
            xv = x_ref[...]
            r = lax.rsqrt(jnp.mean(xv * xv, axis=1, keepdims=True) + NORM_EPS)
            xhat = xv * r
            dh = acc[...]
            dxh = dh * g_ref[...]
            dx_ref[...] = dxo_ref[...] + r * (dxh - xhat * jnp.mean(dxh * xhat, axis=1, keepdims=True))
            dg_ref[...] += jnp.sum(dh * xhat, axis=0, keepdims=True)

    row = pl.BlockSpec((tm, d), lambda m, k: (m, 0))
    return _call(
        body, name="in_proj_bwd", grid=(s // tm, nk),
        in_specs=[pl.BlockSpec((tm, tk), lambda m, k: (m, k)),
                  pl.BlockSpec((None, d, tk), lambda m, k: (k // per, 0, k % per)),
                  row, row, pl.BlockSpec((1, d), lambda m, k: (0, 0))],
        out_specs=[row, pl.BlockSpec((8, d), lambda m, k: (0, 0))],
        out_shape=[jax.ShapeDtypeStruct((s, d), F32), jax.ShapeDtypeStruct((8, d), F32)],
        scratch=[pltpu.VMEM((tm, d), F32)], sem=("arbitrary", "arbitrary"), vmem_mb=56,
        args=(dproj, w, x, dxo, g), comm=comm)


def _place():
    x, y, c = lax.axis_index("x"), lax.axis_index("y"), lax.axis_index("c")
    chips = [(1 - x, y), (x, 1 - y), (1 - x, 1 - y)]
    return x, y, c, chips


def _comm_call(body, name, ins, out_shape, n_remote, n_local, aliases=None):
    return pl.pallas_call(
        body, name=name, in_specs=[ANY] * len(ins), out_specs=[ANY] * len(out_shape), out_shape=out_shape,
        input_output_aliases=aliases or {},
        scratch_shapes=[pltpu.SemaphoreType.DMA((n_remote,)), pltpu.SemaphoreType.DMA((n_remote,)),
                        pltpu.SemaphoreType.DMA((n_local,))])(*ins)


def _rcopy(src, dst, send_sems, recv_sems, k, dev):
    return pltpu.make_async_remote_copy(src_ref=src, dst_ref=dst, send_sem=send_sems.at[k], recv_sem=recv_sems.at[k],
                                        device_id=dev, device_id_type=MESH)


def _run_comm(comm, name):
    n_ci, n_co = len(comm.ins), len(comm.outs)

    def body(*refs):
        cins, couts = refs[:n_ci], refs[n_ci:n_ci + n_co]
        send_sems, recv_sems = refs[n_ci + n_co:]
        comm.start(cins, couts, send_sems, recv_sems)
        comm.finish(cins, couts, send_sems, recv_sems)

    return pl.pallas_call(
        body, name=name, in_specs=[ANY] * n_ci, out_specs=[ANY] * n_co, out_shape=list(comm.outs),
        input_output_aliases=dict(comm.aliases),
        scratch_shapes=[pltpu.SemaphoreType.DMA((comm.n_sems,)), pltpu.SemaphoreType.DMA((comm.n_sems,))])(*comm.ins)


def _gather_comm(fulls, rbp=None, stage="all"):
    n = len(fulls)
    n_ici = 3 * n
    base = n_ici if stage == "all" else 0

    def half(full, blk, core):
        hr = full.shape[1] // 2
        return full.at[blk].at[pl.ds(core * hr, hr)]

    def ici(couts, send_sems, recv_sems, x, y, c, chips):
        b = 2 * x + y
        return [_rcopy(half(full, b, c), half(full, b, c), send_sems, recv_sems, 3 * t + j, (*chip, c))
                for t, full in enumerate(couts[:n]) for j, chip in enumerate(chips)]

    def landed(couts, send_sems, recv_sems, x, y, c, chips, core, first):
        return [_rcopy(half(full, 2 * chip[0] + chip[1], core), half(full, 2 * chip[0] + chip[1], core),
                       send_sems, recv_sems, first + 3 * t + j, (x, y, 1 - c))
                for t, full in enumerate(couts[:n]) for j, chip in enumerate(chips)]

    def small(cins, couts, send_sems, recv_sems, x, y, c, chips):
        b = 2 * x + y
        return ([_rcopy(cins[n], couts[n].at[b], send_sems, recv_sems, 2 * n_ici + j, (*chip, c))
                 for j, chip in enumerate(chips)],
                pltpu.make_async_copy(cins[n], couts[n].at[b], send_sems.at[2 * n_ici + 3]))

    def start(cins, couts, send_sems, recv_sems):
        x, y, c, chips = _place()
        if stage == "sibling":
            for cp in landed(couts, send_sems, recv_sems, x, y, c, chips, c, base):
                cp.start()
            return
        for cp in ici(couts, send_sems, recv_sems, x, y, c, chips):
            cp.start()
        if rbp is not None:
            remote, local = small(cins, couts, send_sems, recv_sems, x, y, c, chips)
            for cp in remote:
                cp.start()
            local.start()

    def finish(cins, couts, send_sems, recv_sems):
        x, y, c, chips = _place()
        passed = landed(couts, send_sems, recv_sems, x, y, c, chips, c, base)
        if stage != "sibling":
            for k, cp in enumerate(landed(couts, send_sems, recv_sems, x, y, c, chips, c, 0)):
                cp.wait_recv()
                if stage == "all":
                    passed[k].start()
            for cp in ici(couts, send_sems, recv_sems, x, y, c, chips):
                cp.wait_send()
        if stage != "chips":
            for cp in landed(couts, send_sems, recv_sems, x, y, c, chips, 1 - c, base):
                cp.wait_recv()
            for cp in passed:
                cp.wait_send()
        if rbp is not None:
            remote, local = small(cins, couts, send_sems, recv_sems, x, y, c, chips)
            for j, chip in enumerate(chips):
                got = couts[n].at[2 * chip[0] + chip[1]]
                _rcopy(got, got, send_sems, recv_sems, 2 * n_ici + j, (x, y, c)).wait_recv()
            for cp in remote:
                cp.wait_send()
            local.wait()

    outs = [jax.ShapeDtypeStruct(f.shape, f.dtype) for f in fulls]
    ins = list(fulls)
    if rbp is not None:
        ins.append(rbp)
        outs.append(jax.ShapeDtypeStruct((4,) + rbp.shape, F32))
    return _Comm(tuple(ins), tuple(outs), {t: t for t in range(n)}, 2 * n_ici + 4, start, finish)


def _chips_comm(sums):
    n = len(sums)

    def copies(cins, couts, send_sems, recv_sems):
        x, y, c, chips = _place()
        return [_rcopy(cins[t].at[2 * chip[0] + chip[1]], couts[t].at[j], send_sems, recv_sems, 3 * t + j, (*chip, c))
                for t in range(n) for j, chip in enumerate(chips)]

    def start(*refs):
        for cp in copies(*refs):
            cp.start()

    def finish(*refs):
        for cp in copies(*refs):
            cp.wait()

    outs = tuple(jax.ShapeDtypeStruct((3,) + p.shape[1:], p.dtype) for p in sums)
    return _Comm(tuple(sums), outs, {}, 3 * n, start, finish)


def _pair_comm(ins, outs, aliases, copies):
    def start(*refs):
        for cp in copies(*refs):
            cp.start()

    def finish(*refs):
        for cp in copies(*refs):
            cp.wait()

    return _Comm(tuple(ins), tuple(outs), aliases, len(ins), start, finish)


def _sibling_comm(parts):
    def copies(cins, couts, send_sems, recv_sems):
        x, y, c, _ = _place()
        return [_rcopy(p.at[:, pl.ds((1 - c) * (p.shape[1] // 2), p.shape[1] // 2), :], couts[t],
                       send_sems, recv_sems, t, (x, y, 1 - c)) for t, p in enumerate(cins)]

    half = [jax.ShapeDtypeStruct((p.shape[0], p.shape[1] // 2, p.shape[2]), p.dtype) for p in parts]
    return _pair_comm(parts, half, {}, copies)


def _halves_comm(tots):
    def copies(cins, couts, send_sems, recv_sems):
        x, y, c, _ = _place()
        return [_rcopy(g.at[c], g.at[c], send_sems, recv_sems, t, (x, y, 1 - c)) for t, g in enumerate(couts)]

    return _pair_comm(tots, [jax.ShapeDtypeStruct(t.shape, t.dtype) for t in tots],
                      {t: t for t in range(len(tots))}, copies)


def _gather_small(packed):
    def body(p_ref, all_ref, send_sems, recv_sems, loc_sems):
        x, y, c, _ = _place()
        me = 4 * x + 2 * y + c
        local = pltpu.make_async_copy(p_ref, all_ref.at[me], loc_sems.at[0])
        local.start()
        sent = []
        for k in range(1, 8):
            px, py, pc = x ^ (k >> 2), y ^ ((k >> 1) & 1), c ^ (k & 1)
            cp = _rcopy(p_ref, all_ref.at[me], send_sems, recv_sems, k - 1, (px, py, pc))
            cp.start()
            sent.append(cp)
        for k in range(1, 8):
            px, py, pc = x ^ (k >> 2), y ^ ((k >> 1) & 1), c ^ (k & 1)
            got = all_ref.at[4 * px + 2 * py + pc]
            _rcopy(got, got, send_sems, recv_sems, k - 1, (x, y, c)).wait_recv()
        for cp in sent:
            cp.wait_send()
        local.wait()

    return _comm_call(body, "gather_small", [packed], [jax.ShapeDtypeStruct((8,) + packed.shape, F32)], 7, 1)[0]


def _sum_devices(allp):
    n, r, c = allp.shape

    def body(a_ref, o_ref):
        acc = a_ref[0]
        for k in range(1, n):
            acc = acc + a_ref[k]
        o_ref[...] = acc

    return pl.pallas_call(body, name="sum_devices", out_shape=jax.ShapeDtypeStruct((r, c), F32))(allp)


def _ext_index():
    u = np.arange(EXT)
    dist = np.where(u < WIN, PAD - u, PAD + EXT - u)
    return np.clip(dist, -(CHUNK - 1), REL_CLIP) + (CHUNK - 1)


def _pack(parts, rows):
    flat = jnp.concatenate([p.reshape(-1) for p in parts])
    return jnp.pad(flat, (0, rows * 128 - flat.shape[0])).reshape(rows, 128)


def _unpack(packed, shapes):
    flat, out, at = packed.reshape(-1), [], 0
    for shp in shapes:
        size = int(np.prod(shp))
        out.append(flat[at:at + size].reshape(shp))
        at += size
    return out


def kernel(x, norm_g, w_in, q_norm_g, k_norm_g, rel_bias, w_out, loss_target, m_norm_g, m_w_in, m_q_norm_g, m_k_norm_g, m_rel_bias, m_w_out, v_norm_g, v_w_in, v_q_norm_g, v_k_norm_g, v_rel_bias, v_w_out):
    nl, d, nb = w_in.shape
    s = x.shape[1]
    ds_ = d // 2
    nh = ds_ // HEAD
    rb = w_out.shape[1]
    nrel = rel_bias.shape[2]
    bx = lax.axis_index("x") * 2 + lax.axis_index("y")

    rb_rows = -(-(nl * nh * nrel) // 1024) * 8
    cx = lax.axis_index("c")
    wi_full = [_cast_block(w_in, l, bx, "cast_w_in") for l in range(nl)]
    wo_full = [_cast_block(w_out, l, bx, "cast_w_out") for l in range(nl)]
    wi_full[0], wo_full[0], rel_all = _run_comm(
        _gather_comm([wi_full[0], wo_full[0]], _pack([rel_bias], rb_rows)), "gather_first")
    rel_full = jnp.concatenate(
        [rel_all[j].reshape(-1)[:nl * nh * nrel].reshape(nl, nh, nrel) for j in range(4)], axis=2)
    ext_idx = _ext_index()
    onehot = jnp.asarray(ext_idx[:, None] == np.arange(N_REL)[None, :], F32)
    ext = jnp.einsum("lhr,ur->lhu", rel_full, onehot, precision=lax.Precision.HIGHEST).reshape(nl, nh, 1, EXT)

    xs, hs, projs, yas, lts, ybs, mixes = [], [], [], [], [], [], []
    xc = x[0]
    for l in range(nl):
        h = _rmsnorm_fwd(xc, norm_g[l:l + 1])
        nxt = l + 1 < nl
        (proj,), got = _in_proj(h, wi_full[l], _gather_comm([wo_full[l + 1]], stage="chips") if nxt else None)
        if nxt:
            wo_full[l + 1] = got[0]
        (ya, lt), got = _attn_a_fwd(proj, nh, _gather_comm([wi_full[l + 1]], stage="chips") if nxt else None)
        if nxt:
            wi_full[l + 1] = got[0]
        (yb,), got = _attn_b_fwd(proj, q_norm_g[l:l + 1], k_norm_g[l:l + 1], ext[l], nh,
                                 _gather_comm([wi_full[l + 1], wo_full[l + 1]], stage="sibling") if nxt else None)
        if nxt:
            wi_full[l + 1], wo_full[l + 1] = got
        xs.append(xc)
        xc, mix = _out_proj(xc, ya, yb, proj, wo_full[l].reshape(4 * rb, d))
        hs.append(h); projs.append(proj); yas.append(ya); lts.append(lt); ybs.append(yb); mixes.append(mix)
    dx, loss_tile = _loss_head(xc, loss_target[0])

    small, g_wi, g_wo = [None] * nl, [None] * nl, [None] * nl
    pending = None

    def keep(lay, shared):
        g_wi[lay], g_wo[lay] = shared[0].reshape(d, nb), shared[1].reshape(rb, d)

    for l in reversed(range(nl)):
        wo = wo_full[l].reshape(4 * rb, d)
        p_wo = _wgrad(mixes[l], dx, 4, False, "wgrad_out")
        dya, dyb, dga, dgb = _out_proj_bwd(dx, yas[l], ybs[l], projs[l], wo)
        (dqa, dka, dva), got = _attn_a_bwd(projs[l], lts[l], dya, nh, _chips_comm(pending[1]) if pending else None)
        tots = [_add_chips(pending[1][t], got[t], bx, cx) for t in range(2)] if pending else None
        (dqb, dkb, dvb, dqg, dkg, dext), shared = _attn_b_bwd(
            projs[l], dyb, q_norm_g[l:l + 1], k_norm_g[l:l + 1], ext[l], nh, _halves_comm(tots) if pending else None)
        if pending:
            keep(pending[0], shared)
        dproj = jnp.concatenate([dqa, dka, dva, dga, dqb, dkb, dvb, dgb], axis=1)
        parts = [_wgrad(hs[l], dproj, 4, True, "wgrad_in"), p_wo]
        if l > 0:
            (dx, dng), theirs = _in_proj_bwd(dproj, wi_full[l], xs[l], dx, norm_g[l:l + 1], _sibling_comm(parts))
            pending = (l, [_add_sibling(parts[t], theirs[t], cx) for t in range(2)])
        else:
            theirs = _run_comm(_sibling_comm(parts), "reduce_sibling")
            sums = [_add_sibling(parts[t], theirs[t], cx) for t in range(2)]
            (dx, dng), got = _in_proj_bwd(dproj, wi_full[l], xs[l], dx, norm_g[l:l + 1], _chips_comm(sums))
            keep(0, _run_comm(_halves_comm([_add_chips(sums[t], got[t], bx, cx) for t in range(2)]), "share_halves"))
        small[l] = (dng[0], jnp.sum(dqg, axis=0).reshape(-1), jnp.sum(dkg, axis=0).reshape(-1), dext.reshape(nh, EXT))
    grad_x = dx[None]

    small_shapes = [(nl, d), (nl, HEAD), (nl, HEAD), (nl, nh, EXT), (1,)]
    small_parts = [jnp.stack([sm[i] for sm in small]) for i in range(4)] + [loss_tile[0, :1]]
    rows = -(-sum(int(np.prod(sh)) for sh in small_shapes) // 1024) * 8
    tot = _sum_devices(_gather_small(_pack(small_parts, rows)))
    g_ng, g_qg, g_kg, g_ext, loss = _unpack(tot, small_shapes)
    g_rel_full = jnp.einsum("lhu,ur->lhr", g_ext, onehot, precision=lax.Precision.HIGHEST)
    g_rel = lax.dynamic_slice_in_dim(g_rel_full, bx * nrel, nrel, axis=2)

    res_wi, res_wo = (), ()
    for l in range(nl):
        res_wi = _adamw_layer(l, w_in, g_wi[l], m_w_in, v_w_in, res_wi, "adamw_w_in")
        res_wo = _adamw_layer(l, w_out, g_wo[l], m_w_out, v_w_out, res_wo, "adamw_w_out")
    g_wi, d_wi, nm_wi, nv_wi = res_wi
    g_wo, d_wo, nm_wo, nv_wo = res_wo
    sm_shapes = [(nl, d), (nl, HEAD), (nl, HEAD), (nl, nh, nrel)]
    sm_rows = -(-sum(int(np.prod(sh)) for sh in sm_shapes) // 1024) * 8
    pw, pg, pm, pv = [_pack(group, sm_rows) for group in (
        (norm_g, q_norm_g, k_norm_g, rel_bias), (g_ng, g_qg, g_kg, g_rel),
        (m_norm_g, m_q_norm_g, m_k_norm_g, m_rel_bias), (v_norm_g, v_q_norm_g, v_k_norm_g, v_rel_bias))]
    d_sm, nm_sm, nv_sm = [_unpack(a[0], sm_shapes)
                          for a in _adamw_layer(0, pw[None], pg, pm[None], pv[None], (), "adamw_small")[1:]]

    return (loss[0], grad_x, g_ng, g_wi, g_qg, g_kg, g_rel, g_wo,
            d_sm[0], d_wi, d_sm[1], d_sm[2], d_sm[3], d_wo,
            nm_sm[0], nm_wi, nm_sm[1], nm_sm[2], nm_sm[3], nm_wo,
            nv_sm[0], nv_wi, nv_sm[1], nv_sm[2], nv_sm[3], nv_wo)
```

```python
from typing import Callable, NamedTuple

import jax
import jax.numpy as jnp
import numpy as np
from jax import lax
from jax.experimental import pallas as pl
from jax.experimental.pallas import tpu as pltpu

F32 = jnp.float32
BF16 = jnp.bfloat16

HEAD = 128
CHUNK = 64
LEFT_CHUNKS = 8
REL_CLIP = 256
N_REL = REL_CLIP + CHUNK
NORM_EPS = 1e-6
NEG_BIG = -1e30
TQ = 256
ROWS = 32
PAD = LEFT_CHUNKS * CHUNK
WIN = PAD + TQ
EXT = 1024
SCALE = HEAD ** -0.5

ADAM_LR = 0.001
ADAM_B1 = 0.9
ADAM_B2 = 0.999
ADAM_EPS = 1e-08
ADAM_WD = 0.01
ADAM_STEP = 10

ANY = pl.BlockSpec(memory_space=pl.ANY)
MESH = pl.DeviceIdType.MESH


def _params(sem=None, vmem_mb=None):
    kw = {}
    if sem is not None:
        kw["dimension_semantics"] = sem
    if vmem_mb is not None:
        kw["vmem_limit_bytes"] = vmem_mb << 20
    return pltpu.CompilerParams(**kw)


class _Comm(NamedTuple):
    ins: tuple
    outs: tuple
    aliases: dict
    n_sems: int
    start: Callable
    finish: Callable


def _call(body, *, name, grid, in_specs, out_specs, out_shape, args, scratch=(), sem=None, vmem_mb=None, comm=None):
    if comm is None:
        out = pl.pallas_call(body, name=name, grid=grid, in_specs=in_specs, out_specs=out_specs, out_shape=out_shape,
                             scratch_shapes=list(scratch), compiler_params=_params(sem, vmem_mb))(*args)
        return out, ()
    n_in, n_out, n_ci, n_co = len(in_specs), len(out_shape), len(comm.ins), len(comm.outs)

    def hosted(*refs):
        ins, cins = refs[:n_in], refs[n_in:n_in + n_ci]
        outs, couts = refs[n_in + n_ci:n_in + n_ci + n_out], refs[n_in + n_ci + n_out:n_in + n_ci + n_out + n_co]
        rest = refs[n_in + n_ci + n_out + n_co:]
        send_sems, recv_sems = rest[-2:]
        first, last = None, None
        for ax, size in enumerate(grid):
            at = pl.program_id(ax)
            first = (at == 0) if first is None else first & (at == 0)
            last = (at == size - 1) if last is None else last & (at == size - 1)

        @pl.when(first)
        def _():
            comm.start(cins, couts, send_sems, recv_sems)

        body(*ins, *outs, *rest[:-2])

        @pl.when(last)
        def _():
            comm.finish(cins, couts, send_sems, recv_sems)

    out = pl.pallas_call(
        hosted, name=name, grid=grid, in_specs=list(in_specs) + [ANY] * n_ci, out_specs=list(out_specs) + [ANY] * n_co,
        out_shape=list(out_shape) + list(comm.outs),
        input_output_aliases={n_in + k: n_out + v for k, v in comm.aliases.items()},
        scratch_shapes=list(scratch) + [pltpu.SemaphoreType.DMA((comm.n_sems,)), pltpu.SemaphoreType.DMA((comm.n_sems,))],
        compiler_params=_params(("arbitrary",) * len(grid), vmem_mb))(*args, *comm.ins)
    return out[:n_out], out[n_out:]


def _dot(a, b):
    return jnp.dot(a, b, preferred_element_type=F32)


def _dot_nt(a, b):
    return lax.dot_general(a, b, (((1,), (1,)), ((), ())), preferred_element_type=F32)


def _dot_tn(a, b):
    return lax.dot_general(a, b, (((0,), (0,)), ((), ())), preferred_element_type=F32)


def _split_dot(x, m):
    hi = x.astype(BF16)
    lo = (x - hi.astype(F32)).astype(BF16)
    return _dot(hi, m) + _dot(lo, m)


def _silu_parts(g):
    sg = 1.0 / (1.0 + jnp.exp(-g))
    return g * sg, sg * (1.0 + g * (1.0 - sg))


def _idx(*vals):
    return jnp.stack([jnp.asarray(v, jnp.int32) for v in vals])


def _cast_block(w, l, blk, name):
    _, r, c = w.shape
    tr = min(r, 512)

    def body(b_ref, w_ref, o_ref):
        o_ref[...] = w_ref[...].astype(BF16)

    spec = pltpu.PrefetchScalarGridSpec(
        num_scalar_prefetch=1, grid=(r // tr,),
        in_specs=[pl.BlockSpec((None, tr, c), lambda i, b: (l, i, 0))],
        out_specs=pl.BlockSpec((None, tr, c), lambda i, b: (b[0], i, 0)))
    return pl.pallas_call(body, name=name, grid_spec=spec, out_shape=jax.ShapeDtypeStruct((4, r, c), BF16),
                          compiler_params=_params(("parallel",)))(_idx(blk), w)


def _add_sibling(p, theirs, core):
    nblk, r, c = p.shape
    hr = r // 2
    tr = min(hr, 256)
    per = hr // tr

    def body(c_ref, p_ref, t_ref, o_ref):
        o_ref[...] = (p_ref[...].astype(F32) + t_ref[...].astype(F32)).astype(BF16)

    blk = pl.BlockSpec((None, tr, c), lambda j, i, cr: (j, i, 0))
    spec = pltpu.PrefetchScalarGridSpec(
        num_scalar_prefetch=1, grid=(nblk, per),
        in_specs=[pl.BlockSpec((None, tr, c), lambda j, i, cr: (j, cr[0] * per + i, 0)), blk], out_specs=blk)
    return pl.pallas_call(body, name="add_sibling", grid_spec=spec, out_shape=jax.ShapeDtypeStruct((nblk, hr, c), BF16),
                          compiler_params=_params(("parallel", "parallel")))(_idx(core), p, theirs)


def _add_chips(sums, got, blk, core):
    _, hr, c = sums.shape
    tr = min(hr, 256)

    def body(i_ref, s_ref, g0_ref, g1_ref, g2_ref, o_ref):
        o_ref[...] = ((s_ref[...].astype(F32) + g0_ref[...].astype(F32))
                      + g1_ref[...].astype(F32)) + g2_ref[...].astype(F32)

    at = lambda j: pl.BlockSpec((None, tr, c), lambda i, ir: (j, i, 0))
    spec = pltpu.PrefetchScalarGridSpec(
        num_scalar_prefetch=1, grid=(hr // tr,),
        in_specs=[pl.BlockSpec((None, tr, c), lambda i, ir: (ir[0], i, 0)), at(0), at(1), at(2)],
        out_specs=pl.BlockSpec((None, tr, c), lambda i, ir: (ir[1], i, 0)))
    return pl.pallas_call(body, name="add_chips", grid_spec=spec, out_shape=jax.ShapeDtypeStruct((2, hr, c), F32),
                          compiler_params=_params(("parallel",)))(_idx(blk, core), sums, got, got, got)


def _adamw_layer(l, w, g, m, v, prev, name):
    nl, r, c = w.shape
    tr = min(r, 256)
    c1 = 1.0 / (1.0 - ADAM_B1 ** ADAM_STEP)
    c2 = 1.0 / (1.0 - ADAM_B2 ** ADAM_STEP)

    def body(w_ref, g_ref, m_ref, v_ref, *rest):
        go_ref, d_ref, nm_ref, nv_ref = rest[-4:]
        gg = g_ref[...]
        nm = ADAM_B1 * m_ref[...] + (1.0 - ADAM_B1) * gg
        nv = ADAM_B2 * v_ref[...] + (1.0 - ADAM_B2) * (gg * gg)
        upd = (nm * c1) / (jnp.sqrt(nv * c2) + ADAM_EPS) + ADAM_WD * w_ref[...]
        go_ref[...] = gg
        d_ref[...] = -ADAM_LR * upd
        nm_ref[...] = nm
        nv_ref[...] = nv

    lay = pl.BlockSpec((None, tr, c), lambda i: (l, i, 0))
    shp = jax.ShapeDtypeStruct((nl, r, c), F32)
    return pl.pallas_call(
        body, name=name, grid=(r // tr,),
        in_specs=[lay, pl.BlockSpec((tr, c), lambda i: (i, 0)), lay, lay] + [ANY] * len(prev),
        out_specs=[lay] * 4, out_shape=[shp] * 4, input_output_aliases={4 + k: k for k in range(len(prev))},
        compiler_params=_params(("parallel",), 40))(w, g, m, v, *prev)


def _rmsnorm_fwd(x, g):
    s, d = x.shape
    tm = min(s, 256)

    def body(x_ref, g_ref, h_ref):
        xv = x_ref[...]
        r = lax.rsqrt(jnp.mean(xv * xv, axis=1, keepdims=True) + NORM_EPS)
        h_ref[...] = (xv * r * g_ref[...]).astype(BF16)

    return pl.pallas_call(
        body, name="rmsnorm_fwd", grid=(s // tm,),
        in_specs=[pl.BlockSpec((tm, d), lambda i: (i, 0)), pl.BlockSpec((1, d), lambda i: (0, 0))],
        out_specs=pl.BlockSpec((tm, d), lambda i: (i, 0)),
        out_shape=jax.ShapeDtypeStruct((s, d), BF16),
        compiler_params=_params(("parallel",)))(x, g)


def _in_proj(h, w, comm=None):
    s, d = h.shape
    nblk, _, nb = w.shape
    tm, tn = min(s, 512), min(nb, 1024)
    per = nb // tn

    def body(h_ref, w_ref, o_ref):
        o_ref[...] = _dot(h_ref[...], w_ref[...])

    return _call(
        body, name="in_proj", grid=(nblk * per, s // tm),
        in_specs=[pl.BlockSpec((tm, d), lambda n, m: (m, 0)),
                  pl.BlockSpec((None, d, tn), lambda n, m: (n // per, 0, n % per))],
        out_specs=[pl.BlockSpec((tm, tn), lambda n, m: (m, n))],
        out_shape=[jax.ShapeDtypeStruct((s, nblk * nb), F32)],
        sem=("parallel", "parallel"), vmem_mb=40, args=(h, w), comm=comm)


def _heads_per_step(nh):
    return 2 if nh % 2 == 0 else 1


def _head(hh):
    return slice(hh * HEAD, (hh + 1) * HEAD)


def _tri(op):
    r = lax.broadcasted_iota(jnp.int32, (TQ, TQ), 0)
    c = lax.broadcasted_iota(jnp.int32, (TQ, TQ), 1)
    return op(r, c)


def _staggered(groups, hp):
    for hh in range(hp):
        for fn in groups[0]:
            fn(hh)
    for group in groups[1:]:
        for hh in range(hp):
            for fn in group:
                fn(hh)


def _chunk_causal(r):
    row = lax.broadcasted_iota(jnp.int32, (ROWS, TQ), 0) + r * ROWS
    return row > lax.broadcasted_iota(jnp.int32, (ROWS, TQ), 1)


def _sb_logs(qk, causal):
    z = qk * SCALE
    l1p = jnp.log(1.0 + jnp.exp(-jnp.abs(z)))
    ls = jnp.minimum(-z, 0.0) - l1p
    if causal is not None:
        ls = jnp.where(causal, ls, 0.0)
    return ls, jnp.minimum(z, 0.0) - l1p


def _attn_a_fwd(proj, nh, comm=None):
    s = proj.shape[0]
    nq = s // TQ
    hp = _heads_per_step(nh)
    ng = nh // hp

    def body(q_ref, k_ref, v_ref, o_ref, lt_ref, acc_ref, qb_ref, m_ref, car_ref, sum_ref,
             z_ref, lsig_ref, aft_ref, hi_ref, lo_ref, w_ref):
        i = pl.program_id(1)
        qb_ref[...] = q_ref[...].astype(BF16)
        m_ref[...] = _tri(lambda r, c: r > c).astype(BF16)
        acc_ref[...] = jnp.zeros_like(acc_ref)
        car_ref[...] = jnp.zeros_like(car_ref)

        def tile(j, diag):
            off = pl.multiple_of(j * TQ, TQ)

            def scores(hh):
                kb = k_ref[pl.ds(off, TQ), _head(hh)].astype(BF16)
                z_ref[hh] = _dot_nt(qb_ref[:, _head(hh)], kb)

            def logs(hh):
                for r in range(TQ // ROWS):
                    sl = pl.ds(r * ROWS, ROWS)
                    ls, lsig = _sb_logs(z_ref[hh, sl, :], _chunk_causal(r) if diag else None)
                    lsig_ref[hh, sl, :] = lsig
                    hi = ls.astype(BF16)
                    hi_ref[hh, sl, :] = hi
                    lo_ref[hh, sl, :] = (ls - hi.astype(F32)).astype(BF16)
                    sum_ref[hh, sl, :] = jnp.sum(ls, axis=1, keepdims=True)

            def after(hh):
                aft_ref[hh] = _dot(hi_ref[hh], m_ref[...]) + _dot(lo_ref[hh], m_ref[...])

            def weights(hh):
                for r in range(TQ // ROWS):
                    sl = pl.ds(r * ROWS, ROWS)
                    w = jnp.exp(lsig_ref[hh, sl, :] + aft_ref[hh, sl, :] + car_ref[hh, sl, :])
                    if diag:
                        w = jnp.where(_chunk_causal(r), w, 0.0)
                    w_ref[hh, sl, :] = w.astype(BF16)
                    car_ref[hh, sl, :] += sum_ref[hh, sl, :]

            def values(hh):
                vb = v_ref[pl.ds(off, TQ), _head(hh)].astype(BF16)
                acc_ref[:, _head(hh)] += _dot(w_ref[hh], vb)

            _staggered([(scores,), (logs, after), (weights, values)], hp)

        tile(i, True)

        def step(t, carry):
            tile(i - 1 - t, False)
            return carry

        lax.fori_loop(0, i, step, 0)
        o_ref[...] = acc_ref[...]
        lt_ref[...] = car_ref[...]

    wd = hp * HEAD
    sq = lambda dt: pltpu.VMEM((hp, TQ, TQ), dt)
    return _call(
        body, name="attn_a_fwd", grid=(ng, nq),
        in_specs=[pl.BlockSpec((TQ, wd), lambda h, i: (i, h)),
                  pl.BlockSpec((s, wd), lambda h, i: (0, ng + h)),
                  pl.BlockSpec((s, wd), lambda h, i: (0, 2 * ng + h))],
        out_specs=[pl.BlockSpec((TQ, wd), lambda h, i: (i, h)),
                   pl.BlockSpec((hp, TQ, 1), lambda h, i: (h, i, 0))],
        out_shape=[jax.ShapeDtypeStruct((s, nh * HEAD), F32), jax.ShapeDtypeStruct((nh, s, 1), F32)],
        scratch=[pltpu.VMEM((TQ, wd), F32), pltpu.VMEM((TQ, wd), BF16), pltpu.VMEM((TQ, TQ), BF16),
                 pltpu.VMEM((hp, TQ, 1), F32), pltpu.VMEM((hp, TQ, 1), F32),
                 sq(F32), sq(F32), sq(F32), sq(BF16), sq(BF16), sq(BF16)],
        sem=("parallel", "arbitrary"), args=(proj, proj, proj), comm=comm)


def _band_valid(i):
    cl = lax.broadcasted_iota(jnp.int32, (TQ, WIN), 0) // CHUNK
    kl = lax.broadcasted_iota(jnp.int32, (TQ, WIN), 1) // CHUNK
    first = LEFT_CHUNKS - (TQ // CHUNK) * i
    return (kl >= cl) & (kl <= cl + LEFT_CHUNKS) & (kl >= first)


def _build_bias(e_ref, bias_ref):
    e8 = jnp.broadcast_to(e_ref[...], (8, EXT))
    row = lax.broadcasted_iota(jnp.int32, (8, EXT), 0)
    t8 = jnp.zeros((8, EXT), F32)
    for b in range(8):
        t8 = jnp.where(row == b, pltpu.roll(e8, b, 1) if b else e8, t8)
    for a in range(TQ // 8):
        sl = pltpu.roll(t8, 8 * a, 1) if a else t8
        bias_ref[pl.ds(8 * a, 8), :] = sl[:, :WIN]


def _reduce_bias_grad(db_ref):
    acc = jnp.zeros((8, EXT), F32)
    for a in range(TQ // 8):
        sl = db_ref[pl.ds(8 * a, 8), :]
        acc = acc + (pltpu.roll(sl, EXT - 8 * a, 1) if a else sl)
    row = lax.broadcasted_iota(jnp.int32, (8, EXT), 0)
    tot = jnp.zeros((8, EXT), F32)
    for b in range(8):
        tot = tot + jnp.where(row == b, pltpu.roll(acc, EXT - b, 1) if b else acc, 0.0)
    return jnp.sum(tot, axis=0, keepdims=True)


def _band_fill(k_ref, v_ref, kg_ref, kn_pad, v_pad, s):
    k = k_ref[...]
    rk = lax.rsqrt(jnp.mean(k * k, axis=1, keepdims=True) + NORM_EPS)
    kn_pad[pl.ds(0, PAD), :] = jnp.zeros((PAD, HEAD), BF16)
    kn_pad[pl.ds(PAD, s), :] = (k * rk * kg_ref[...]).astype(BF16)
    v_pad[pl.ds(0, PAD), :] = jnp.zeros((PAD, HEAD), BF16)
    v_pad[pl.ds(PAD, s), :] = v_ref[...].astype(BF16)


def _band_probs(q_ref, qg_ref, kn_pad, bias_ref, i):
    q = q_ref[...]
    rq = lax.rsqrt(jnp.mean(q * q, axis=1, keepdims=True) + NORM_EPS)
    qhat = q * rq
    qn = (qhat * qg_ref[...]).astype(BF16)
    off = pl.multiple_of(i * TQ, TQ)
    kw = kn_pad[pl.ds(off, WIN), :]
    sc = _dot_nt(qn, kw) * SCALE + bias_ref[...]
    sc = jnp.where(_band_valid(i), sc, NEG_BIG)
    p = jnp.exp(sc - jnp.max(sc, axis=1, keepdims=True))
    pn = p / jnp.sum(p, axis=1, keepdims=True)
    return rq, qhat, qn, kw, off, pn


def _attn_b_fwd(proj, qg, kg, ext, nh, comm=None):
    s = proj.shape[0]
    nq = s // TQ

    def body(q_ref, k_ref, v_ref, qg_ref, kg_ref, e_ref, o_ref, kn_pad, v_pad, bias_ref):
        i = pl.program_id(1)

        @pl.when(i == 0)
        def _():
            _band_fill(k_ref, v_ref, kg_ref, kn_pad, v_pad, s)
            _build_bias(e_ref, bias_ref)

        _, _, _, _, off, pn = _band_probs(q_ref, qg_ref, kn_pad, bias_ref, i)
        o_ref[...] = _dot(pn.astype(BF16), v_pad[pl.ds(off, WIN), :])

    vec = pl.BlockSpec((1, HEAD), lambda h, i: (0, 0))
    return _call(
        body, name="attn_b_fwd", grid=(nh, nq),
        in_specs=[pl.BlockSpec((TQ, HEAD), lambda h, i: (i, 4 * nh + h)),
                  pl.BlockSpec((s, HEAD), lambda h, i: (0, 5 * nh + h)),
                  pl.BlockSpec((s, HEAD), lambda h, i: (0, 6 * nh + h)),
                  vec, vec,
                  pl.BlockSpec((None, 1, EXT), lambda h, i: (h, 0, 0))],
        out_specs=[pl.BlockSpec((TQ, HEAD), lambda h, i: (i, h))],
        out_shape=[jax.ShapeDtypeStruct((s, nh * HEAD), F32)],
        scratch=[pltpu.VMEM((s + PAD, HEAD), BF16), pltpu.VMEM((s + PAD, HEAD), BF16), pltpu.VMEM((TQ, WIN), F32)],
        sem=("parallel", "arbitrary"), args=(proj, proj, proj, qg, kg, ext), comm=comm)


def _out_proj(x, ya, yb, proj, w):
    s, d = x.shape
    ds_ = ya.shape[1]
    tm = min(s, 256)

    def body(x_ref, ya_ref, yb_ref, ga_ref, gb_ref, w_ref, o_ref, mix_ref):
        ma = (ya_ref[...] * _silu_parts(ga_ref[...])[0]).astype(BF16)
        mb = (yb_ref[...] * _silu_parts(gb_ref[...])[0]).astype(BF16)
        mix_ref[:, :ds_] = ma
        mix_ref[:, ds_:] = mb
        o_ref[...] = x_ref[...] + _dot(ma, w_ref[pl.ds(0, ds_), :]) + _dot(mb, w_ref[pl.ds(ds_, ds_), :])

    row = lambda width: pl.BlockSpec((tm, width), lambda i: (i, 0))
    return pl.pallas_call(
        body, name="out_proj", grid=(s // tm,),
        in_specs=[row(d), row(ds_), row(ds_),
                  pl.BlockSpec((tm, ds_), lambda i: (i, 3)), pl.BlockSpec((tm, ds_), lambda i: (i, 7)),
                  pl.BlockSpec((2 * ds_, d), lambda i: (0, 0))],
        out_specs=[row(d), row(2 * ds_)],
        out_shape=[jax.ShapeDtypeStruct((s, d), F32), jax.ShapeDtypeStruct((s, 2 * ds_), BF16)],
        compiler_params=_params(("parallel",), 48))(x, ya, yb, proj, proj, w)


def _loss_head(y, tgt):
    s, d = y.shape
    tm = min(s, 256)

    def body(y_ref, t_ref, dy_ref, l_ref):
        @pl.when(pl.program_id(0) == 0)
        def _():
            l_ref[...] = jnp.zeros_like(l_ref)

        err = y_ref[...] - t_ref[...]
        dy_ref[...] = err * (1.0 / d)
        l_ref[...] += 0.5 * jnp.sum(jnp.mean(err * err, axis=1, keepdims=True), axis=0, keepdims=True)

    row = pl.BlockSpec((tm, d), lambda i: (i, 0))
    return pl.pallas_call(
        body, name="loss_head", grid=(s // tm,), in_specs=[row, row],
        out_specs=[row, pl.BlockSpec((8, 128), lambda i: (0, 0))],
        out_shape=[jax.ShapeDtypeStruct((s, d), F32), jax.ShapeDtypeStruct((8, 128), F32)],
        compiler_params=_params(("arbitrary",)))(y, tgt)


def _out_proj_bwd(dxo, ya, yb, proj, w):
    s, d = dxo.shape
    ds_ = ya.shape[1]
    tm = min(s, 256)

    def body(dx_ref, ya_ref, yb_ref, ga_ref, gb_ref, w_ref, dya_ref, dyb_ref, dga_ref, dgb_ref):
        dxb = dx_ref[...].astype(BF16)
        for y_ref, g_ref, lo, dy_ref, dg_ref in ((ya_ref, ga_ref, 0, dya_ref, dga_ref),
                                                 (yb_ref, gb_ref, ds_, dyb_ref, dgb_ref)):
            dmix = _dot_nt(dxb, w_ref[pl.ds(lo, ds_), :])
            act, dact = _silu_parts(g_ref[...])
            dy_ref[...] = dmix * act
            dg_ref[...] = (dmix * y_ref[...] * dact).astype(BF16)

    row = lambda width: pl.BlockSpec((tm, width), lambda i: (i, 0))
    return pl.pallas_call(
        body, name="out_proj_bwd", grid=(s // tm,),
        in_specs=[row(d), row(ds_), row(ds_),
                  pl.BlockSpec((tm, ds_), lambda i: (i, 3)), pl.BlockSpec((tm, ds_), lambda i: (i, 7)),
                  pl.BlockSpec((2 * ds_, d), lambda i: (0, 0))],
        out_specs=[row(ds_)] * 4,
        out_shape=[jax.ShapeDtypeStruct((s, ds_), F32)] * 2 + [jax.ShapeDtypeStruct((s, ds_), BF16)] * 2,
        compiler_params=_params(("parallel",), 48))(dxo, ya, yb, proj, proj, w)


def _wgrad(a, b, nblk, col_blocks, name):
    s, m = a.shape
    n = b.shape[1]
    if col_blocks:
        tr = min(m, 512)
        nb = n // nblk
        tn = min(nb, 1024)
        per = nb // tn
        out_shape = (nblk, m, nb)
        out_spec = pl.BlockSpec((None, tr, tn), lambda j, r: (j // per, r, j % per))
    else:
        tn = min(n, 1024)
        tr = m // nblk
        out_shape = (nblk, tr, n)
        out_spec = pl.BlockSpec((None, tr, tn), lambda j, r: (r, 0, j))

    def body(a_ref, b_ref, o_ref):
        o_ref[...] = _dot_tn(a_ref[...].astype(BF16), b_ref[...].astype(BF16)).astype(BF16)

    return pl.pallas_call(
        body, name=name, grid=(n // tn, m // tr),
        in_specs=[pl.BlockSpec((s, tr), lambda j, r: (0, r)), pl.BlockSpec((s, tn), lambda j, r: (0, j))],
        out_specs=out_spec, out_shape=jax.ShapeDtypeStruct(out_shape, BF16),
        compiler_params=_params(("parallel", "parallel"), 48))(a, b)


def _attn_a_bwd(proj, lt, dya, nh, comm=None):
    s = proj.shape[0]
    nq = s // TQ
    hp = _heads_per_step(nh)
    ng = nh // hp

    def body(q_ref, k_ref, v_ref, lt_ref, do_ref, dq_ref, dk_ref, dv_ref, dq_acc, dk_acc, dv_acc,
             qb_ref, dob_ref, upto_ref, before_ref, cls_ref, cg_ref, sls_ref, sg_ref,
             z_ref, dw_ref, lsig_ref, pre_ref, g_ref, hi_ref, lo_ref, wb_ref):
        i = pl.program_id(1)

        @pl.when(i == 0)
        def _():
            dk_acc[...] = jnp.zeros_like(dk_acc)
            dv_acc[...] = jnp.zeros_like(dv_acc)

        dq_acc[...] = jnp.zeros_like(dq_acc)
        qb_ref[...] = q_ref[...].astype(BF16)
        dob_ref[...] = do_ref[...].astype(BF16)
        upto_ref[...] = _tri(lambda r, c: r <= c).astype(BF16)
        before_ref[...] = _tri(lambda r, c: r < c).astype(BF16)
        cls_ref[...] = jnp.zeros_like(cls_ref)
        cg_ref[...] = jnp.zeros_like(cg_ref)

        def tile(j, diag):
            off = pl.multiple_of(j * TQ, TQ)
            keys = lambda hh: k_ref[pl.ds(off, TQ), _head(hh)].astype(BF16)

            def scores(hh):
                z_ref[hh] = _dot_nt(qb_ref[:, _head(hh)], keys(hh))
                dw_ref[hh] = _dot_nt(dob_ref[:, _head(hh)], v_ref[pl.ds(off, TQ), _head(hh)].astype(BF16))

            def logs(hh):
                for r in range(TQ // ROWS):
                    sl = pl.ds(r * ROWS, ROWS)
                    ls, lsig = _sb_logs(z_ref[hh, sl, :], _chunk_causal(r) if diag else None)
                    lsig_ref[hh, sl, :] = lsig
                    hi = ls.astype(BF16)
                    hi_ref[hh, sl, :] = hi
                    lo_ref[hh, sl, :] = (ls - hi.astype(F32)).astype(BF16)
                    sls_ref[hh, sl, :] = jnp.sum(ls, axis=1, keepdims=True)

            def upto(hh):
                pre_ref[hh] = _dot(hi_ref[hh], upto_ref[...]) + _dot(lo_ref[hh], upto_ref[...])

            def weights(hh):
                for r in range(TQ // ROWS):
                    sl = pl.ds(r * ROWS, ROWS)
                    w = jnp.exp(lsig_ref[hh, sl, :] + (lt_ref[hh, sl, :] - (cls_ref[hh, sl, :] + pre_ref[hh, sl, :])))
                    if diag:
                        w = jnp.where(_chunk_causal(r), w, 0.0)
                    wb_ref[hh, sl, :] = w.astype(BF16)
                    g = w * dw_ref[hh, sl, :]
                    g_ref[hh, sl, :] = g
                    hi_ref[hh, sl, :] = g.astype(BF16)
                    sg_ref[hh, sl, :] = jnp.sum(g, axis=1, keepdims=True)

            def earlier(hh):
                dw_ref[hh] = _dot(hi_ref[hh], before_ref[...])

            def logit_grads(hh):
                for r in range(TQ // ROWS):
                    sl = pl.ds(r * ROWS, ROWS)
                    z = z_ref[hh, sl, :] * SCALE
                    e = jnp.exp(-jnp.abs(z))
                    rinv = 1.0 / (1.0 + e)
                    beta = jnp.where(z >= 0.0, rinv, e * rinv)
                    dz = g_ref[hh, sl, :] * (1.0 - beta) - beta * (cg_ref[hh, sl, :] + dw_ref[hh, sl, :])
                    if diag:
                        dz = jnp.where(_chunk_causal(r), dz, 0.0)
                    lo_ref[hh, sl, :] = (dz * SCALE).astype(BF16)
                    cls_ref[hh, sl, :] += sls_ref[hh, sl, :]
                    cg_ref[hh, sl, :] += sg_ref[hh, sl, :]

            def grads(hh):
                dq_acc[:, _head(hh)] += _dot(lo_ref[hh], keys(hh))
                dk_acc[pl.ds(off, TQ), _head(hh)] += _dot_tn(lo_ref[hh], qb_ref[:, _head(hh)])
                dv_acc[pl.ds(off, TQ), _head(hh)] += _dot_tn(wb_ref[hh], dob_ref[:, _head(hh)])

            _staggered([(scores,), (logs, upto), (weights, earlier), (logit_grads, grads)], hp)

        def step(j, carry):
            tile(j, False)
            return carry

        lax.fori_loop(0, i, step, 0)
        tile(i, True)
        dq_ref[...] = dq_acc[...].astype(BF16)

        @pl.when(i == nq - 1)
        def _():
            dk_ref[...] = dk_acc[...].astype(BF16)
            dv_ref[...] = dv_acc[...].astype(BF16)

    wd = hp * HEAD
    sq = lambda dt: pltpu.VMEM((hp, TQ, TQ), dt)
    blk = pl.BlockSpec((TQ, wd), lambda h, i: (i, h))
    col = pl.BlockSpec((s, wd), lambda h, i: (0, h))
    shp = jax.ShapeDtypeStruct((s, nh * HEAD), BF16)
    return _call(
        body, name="attn_a_bwd", grid=(ng, nq),
        in_specs=[blk,
                  pl.BlockSpec((s, wd), lambda h, i: (0, ng + h)),
                  pl.BlockSpec((s, wd), lambda h, i: (0, 2 * ng + h)),
                  pl.BlockSpec((hp, TQ, 1), lambda h, i: (h, i, 0)), blk],
        out_specs=[blk, col, col], out_shape=[shp] * 3,
        scratch=[pltpu.VMEM((TQ, wd), F32), pltpu.VMEM((s, wd), F32), pltpu.VMEM((s, wd), F32),
                 pltpu.VMEM((TQ, wd), BF16), pltpu.VMEM((TQ, wd), BF16),
                 pltpu.VMEM((TQ, TQ), BF16), pltpu.VMEM((TQ, TQ), BF16)]
        + [pltpu.VMEM((hp, TQ, 1), F32)] * 4 + [sq(F32)] * 5 + [sq(BF16)] * 3,
        sem=("parallel", "arbitrary"), args=(proj, proj, proj, lt, dya), comm=comm)


def _attn_b_bwd(proj, dyb, qg, kg, ext, nh, comm=None):
    s = proj.shape[0]
    nq = s // TQ

    def body(q_ref, k_ref, v_ref, do_ref, qg_ref, kg_ref, e_ref,
             dq_ref, dk_ref, dv_ref, dqg_ref, dkg_ref, de_ref,
             kn_pad, v_pad, bias_ref, db_acc, dkn_acc, dv_acc):
        i = pl.program_id(1)

        @pl.when(i == 0)
        def _():
            _band_fill(k_ref, v_ref, kg_ref, kn_pad, v_pad, s)
            _build_bias(e_ref, bias_ref)
            db_acc[...] = jnp.zeros_like(db_acc)
            dkn_acc[...] = jnp.zeros_like(dkn_acc)
            dv_acc[...] = jnp.zeros_like(dv_acc)
            dqg_ref[...] = jnp.zeros_like(dqg_ref)

        rq, qhat, qn, kw, off, pn = _band_probs(q_ref, qg_ref, kn_pad, bias_ref, i)
        dob = do_ref[...].astype(BF16)
        dp = _dot_nt(dob, v_pad[pl.ds(off, WIN), :])
        dsc = pn * (dp - jnp.sum(pn * dp, axis=1, keepdims=True))
        db_acc[:, :WIN] += dsc
        dsb = (dsc * SCALE).astype(BF16)
        dqn = _dot(dsb, kw)
        dkn_acc[pl.ds(off, WIN), :] += _dot_tn(dsb, qn)
        dv_acc[pl.ds(off, WIN), :] += _dot_tn(pn.astype(BF16), dob)
        dqh = dqn * qg_ref[...]
        dq_ref[...] = (rq * (dqh - qhat * jnp.mean(dqh * qhat, axis=1, keepdims=True))).astype(BF16)
        dqg_ref[...] += jnp.sum(dqn * qhat, axis=0, keepdims=True)

        @pl.when(i == nq - 1)
        def _():
            k = k_ref[...]
            rk = lax.rsqrt(jnp.mean(k * k, axis=1, keepdims=True) + NORM_EPS)
            khat = k * rk
            dkn = dkn_acc[pl.ds(PAD, s), :]
            dkh = dkn * kg_ref[...]
            dk_ref[...] = (rk * (dkh - khat * jnp.mean(dkh * khat, axis=1, keepdims=True))).astype(BF16)
            dkg_ref[...] = jnp.sum(dkn * khat, axis=0, keepdims=True)
            dv_ref[...] = dv_acc[pl.ds(PAD, s), :].astype(BF16)
            de_ref[...] = _reduce_bias_grad(db_acc)

    blk = pl.BlockSpec((TQ, HEAD), lambda h, i: (i, h))
    col = pl.BlockSpec((s, HEAD), lambda h, i: (0, h))
    vec = pl.BlockSpec((1, HEAD), lambda h, i: (0, 0))
    hvec = pl.BlockSpec((None, 1, HEAD), lambda h, i: (h, 0, 0))
    hext = pl.BlockSpec((None, 1, EXT), lambda h, i: (h, 0, 0))
    shp = jax.ShapeDtypeStruct((s, nh * HEAD), BF16)
    return _call(
        body, name="attn_b_bwd", grid=(nh, nq),
        in_specs=[pl.BlockSpec((TQ, HEAD), lambda h, i: (i, 4 * nh + h)),
                  pl.BlockSpec((s, HEAD), lambda h, i: (0, 5 * nh + h)),
                  pl.BlockSpec((s, HEAD), lambda h, i: (0, 6 * nh + h)),
                  blk, vec, vec, hext],
        out_specs=[blk, col, col, hvec, hvec, hext],
        out_shape=[shp] * 3 + [jax.ShapeDtypeStruct((nh, 1, HEAD), F32)] * 2
        + [jax.ShapeDtypeStruct((nh, 1, EXT), F32)],
        scratch=[pltpu.VMEM((s + PAD, HEAD), BF16), pltpu.VMEM((s + PAD, HEAD), BF16),
                 pltpu.VMEM((TQ, WIN), F32), pltpu.VMEM((TQ, EXT), F32),
                 pltpu.VMEM((s + PAD, HEAD), F32), pltpu.VMEM((s + PAD, HEAD), F32)],
        sem=("parallel", "arbitrary"), args=(proj, proj, proj, dyb, qg, kg, ext), comm=comm)


def _in_proj_bwd(dproj, w, x, dxo, g, comm=None):
    s, d = x.shape
    nblk, _, nb = w.shape
    tm, tk = min(s, 512), min(nb, 1024)
    per = nb // tk
    nk = nblk * per

    def body(dp_ref, w_ref, x_ref, dxo_ref, g_ref, dx_ref, dg_ref, acc):
        m, k = pl.program_id(0), pl.program_id(1)

        @pl.when(k == 0)
        def _():
            acc[...] = jnp.zeros_like(acc)

        @pl.when((k == 0) & (m == 0))
        def _():
            dg_ref[...] = jnp.zeros_like(dg_ref)

        acc[...] += _dot_nt(dp_ref[...], w_ref[...])

        @pl.when(k == nk - 1)
        def _():
            xv = x_ref[...]
            r = lax.rsqrt(jnp.mean(xv * xv, axis=1, keepdims=True) + NORM_EPS)
            xhat = xv * r
            dh = acc[...]
            dxh = dh * g_ref[...]
            dx_ref[...] = dxo_ref[...] + r * (dxh - xhat * jnp.mean(dxh * xhat, axis=1, keepdims=True))
            dg_ref[...] += jnp.sum(dh * xhat, axis=0, keepdims=True)

    row = pl.BlockSpec((tm, d), lambda m, k: (m, 0))
    return _call(
        body, name="in_proj_bwd", grid=(s // tm, nk),
        in_specs=[pl.BlockSpec((tm, tk), lambda m, k: (m, k)),
                  pl.BlockSpec((None, d, tk), lambda m, k: (k // per, 0, k % per)),
                  row, row, pl.BlockSpec((1, d), lambda m, k: (0, 0))],
        out_specs=[row, pl.BlockSpec((8, d), lambda m, k: (0, 0))],
        out_shape=[jax.ShapeDtypeStruct((s, d), F32), jax.ShapeDtypeStruct((8, d), F32)],
        scratch=[pltpu.VMEM((tm, d), F32)], sem=("arbitrary", "arbitrary"), vmem_mb=56,
        args=(dproj, w, x, dxo, g), comm=comm)


def _place():
    x, y, c = lax.axis_index("x"), lax.axis_index("y"), lax.axis_index("c")
    chips = [(1 - x, y), (x, 1 - y), (1 - x, 1 - y)]
    return x, y, c, chips


def _comm_call(body, name, ins, out_shape, n_remote, n_local, aliases=None):
    return pl.pallas_call(
        body, name=name, in_specs=[ANY] * len(ins), out_specs=[ANY] * len(out_shape), out_shape=out_shape,
        input_output_aliases=aliases or {},
        scratch_shapes=[pltpu.SemaphoreType.DMA((n_remote,)), pltpu.SemaphoreType.DMA((n_remote,)),
                        pltpu.SemaphoreType.DMA((n_local,))])(*ins)


def _rcopy(src, dst, send_sems, recv_sems, k, dev):
    return pltpu.make_async_remote_copy(src_ref=src, dst_ref=dst, send_sem=send_sems.at[k], recv_sem=recv_sems.at[k],
                                        device_id=dev, device_id_type=MESH)


def _run_comm(comm, name):
    n_ci, n_co = len(comm.ins), len(comm.outs)

    def body(*refs):
        cins, couts = refs[:n_ci], refs[n_ci:n_ci + n_co]
        send_sems, recv_sems = refs[n_ci + n_co:]
        comm.start(cins, couts, send_sems, recv_sems)
        comm.finish(cins, couts, send_sems, recv_sems)

    return pl.pallas_call(
        body, name=name, in_specs=[ANY] * n_ci, out_specs=[ANY] * n_co, out_shape=list(comm.outs),
        input_output_aliases=dict(comm.aliases),
        scratch_shapes=[pltpu.SemaphoreType.DMA((comm.n_sems,)), pltpu.SemaphoreType.DMA((comm.n_sems,))])(*comm.ins)


def _gather_comm(fulls, rbp=None, stage="all"):
    n = len(fulls)
    n_ici = 3 * n
    base = n_ici if stage == "all" else 0

    def half(full, blk, core):
        hr = full.shape[1] // 2
        return full.at[blk].at[pl.ds(core * hr, hr)]

    def ici(couts, send_sems, recv_sems, x, y, c, chips):
        b = 2 * x + y
        return [_rcopy(half(full, b, c), half(full, b, c), send_sems, recv_sems, 3 * t + j, (*chip, c))
                for t, full in enumerate(couts[:n]) for j, chip in enumerate(chips)]

    def landed(couts, send_sems, recv_sems, x, y, c, chips, core, first):
        return [_rcopy(half(full, 2 * chip[0] + chip[1], core), half(full, 2 * chip[0] + chip[1], core),
                       send_sems, recv_sems, first + 3 * t + j, (x, y, 1 - c))
                for t, full in enumerate(couts[:n]) for j, chip in enumerate(chips)]

    def small(cins, couts, send_sems, recv_sems, x, y, c, chips):
        b = 2 * x + y
        return ([_rcopy(cins[n], couts[n].at[b], send_sems, recv_sems, 2 * n_ici + j, (*chip, c))
                 for j, chip in enumerate(chips)],
                pltpu.make_async_copy(cins[n], couts[n].at[b], send_sems.at[2 * n_ici + 3]))

    def start(cins, couts, send_sems, recv_sems):
        x, y, c, chips = _place()
        if stage == "sibling":
            for cp in landed(couts, send_sems, recv_sems, x, y, c, chips, c, base):
                cp.start()
            return
        for cp in ici(couts, send_sems, recv_sems, x, y, c, chips):
            cp.start()
        if rbp is not None:
            remote, local = small(cins, couts, send_sems, recv_sems, x, y, c, chips)
            for cp in remote:
                cp.start()
            local.start()

    def finish(cins, couts, send_sems, recv_sems):
        x, y, c, chips = _place()
        passed = landed(couts, send_sems, recv_sems, x, y, c, chips, c, base)
        if stage != "sibling":
            for k, cp in enumerate(landed(couts, send_sems, recv_sems, x, y, c, chips, c, 0)):
                cp.wait_recv()
                if stage == "all":
                    passed[k].start()
            for cp in ici(couts, send_sems, recv_sems, x, y, c, chips):
                cp.wait_send()
        if stage != "chips":
            for cp in landed(couts, send_sems, recv_sems, x, y, c, chips, 1 - c, base):
                cp.wait_recv()
            for cp in passed:
                cp.wait_send()
        if rbp is not None:
            remote, local = small(cins, couts, send_sems, recv_sems, x, y, c, chips)
            for j, chip in enumerate(chips):
                got = couts[n].at[2 * chip[0] + chip[1]]
                _rcopy(got, got, send_sems, recv_sems, 2 * n_ici + j, (x, y, c)).wait_recv()
            for cp in remote:
                cp.wait_send()
            local.wait()

    outs = [jax.ShapeDtypeStruct(f.shape, f.dtype) for f in fulls]
    ins = list(fulls)
    if rbp is not None:
        ins.append(rbp)
        outs.append(jax.ShapeDtypeStruct((4,) + rbp.shape, F32))
    return _Comm(tuple(ins), tuple(outs), {t: t for t in range(n)}, 2 * n_ici + 4, start, finish)


def _chips_comm(sums):
    n = len(sums)

    def copies(cins, couts, send_sems, recv_sems):
        x, y, c, chips = _place()
        return [_rcopy(cins[t].at[2 * chip[0] + chip[1]], couts[t].at[j], send_sems, recv_sems, 3 * t + j, (*chip, c))
                for t in range(n) for j, chip in enumerate(chips)]

    def start(*refs):
        for cp in copies(*refs):
            cp.start()

    def finish(*refs):
        for cp in copies(*refs):
            cp.wait()

    outs = tuple(jax.ShapeDtypeStruct((3,) + p.shape[1:], p.dtype) for p in sums)
    return _Comm(tuple(sums), outs, {}, 3 * n, start, finish)


def _pair_comm(ins, outs, aliases, copies):
    def start(*refs):
        for cp in copies(*refs):
            cp.start()

    def finish(*refs):
        for cp in copies(*refs):
            cp.wait()

    return _Comm(tuple(ins), tuple(outs), aliases, len(ins), start, finish)


def _sibling_comm(parts):
    def copies(cins, couts, send_sems, recv_sems):
        x, y, c, _ = _place()
        return [_rcopy(p.at[:, pl.ds((1 - c) * (p.shape[1] // 2), p.shape[1] // 2), :], couts[t],
                       send_sems, recv_sems, t, (x, y, 1 - c)) for t, p in enumerate(cins)]

    half = [jax.ShapeDtypeStruct((p.shape[0], p.shape[1] // 2, p.shape[2]), p.dtype) for p in parts]
    return _pair_comm(parts, half, {}, copies)


def _halves_comm(tots):
    def copies(cins, couts, send_sems, recv_sems):
        x, y, c, _ = _place()
        return [_rcopy(g.at[c], g.at[c], send_sems, recv_sems, t, (x, y, 1 - c)) for t, g in enumerate(couts)]

    return _pair_comm(tots, [jax.ShapeDtypeStruct(t.shape, t.dtype) for t in tots],
                      {t: t for t in range(len(tots))}, copies)


def _gather_small(packed):
    def body(p_ref, all_ref, send_sems, recv_sems, loc_sems):
        x, y, c, _ = _place()
        me = 4 * x + 2 * y + c
        local = pltpu.make_async_copy(p_ref, all_ref.at[me], loc_sems.at[0])
        local.start()
        sent = []
        for k in range(1, 8):
            px, py, pc = x ^ (k >> 2), y ^ ((k >> 1) & 1), c ^ (k & 1)
            cp = _rcopy(p_ref, all_ref.at[me], send_sems, recv_sems, k - 1, (px, py, pc))
            cp.start()
            sent.append(cp)
        for k in range(1, 8):
            px, py, pc = x ^ (k >> 2), y ^ ((k >> 1) & 1), c ^ (k & 1)
            got = all_ref.at[4 * px + 2 * py + pc]
            _rcopy(got, got, send_sems, recv_sems, k - 1, (x, y, c)).wait_recv()
        for cp in sent:
            cp.wait_send()
        local.wait()

    return _comm_call(body, "gather_small", [packed], [jax.ShapeDtypeStruct((8,) + packed.shape, F32)], 7, 1)[0]


def _sum_devices(allp):
    n, r, c = allp.shape

    def body(a_ref, o_ref):
        acc = a_ref[0]
        for k in range(1, n):
            acc = acc + a_ref[k]
        o_ref[...] = acc

    return pl.pallas_call(body, name="sum_devices", out_shape=jax.ShapeDtypeStruct((r, c), F32))(allp)


def _ext_index():
    u = np.arange(EXT)
    dist = np.where(u < WIN, PAD - u, PAD + EXT - u)
    return np.clip(dist, -(CHUNK - 1), REL_CLIP) + (CHUNK - 1)


def _pack(parts, rows):
    flat = jnp.concatenate([p.reshape(-1) for p in parts])
    return jnp.pad(flat, (0, rows * 128 - flat.shape[0])).reshape(rows, 128)


def _unpack(packed, shapes):
    flat, out, at = packed.reshape(-1), [], 0
    for shp in shapes:
        size = int(np.prod(shp))
        out.append(flat[at:at + size].reshape(shp))
        at += size
    return out


def kernel(x, norm_g, w_in, q_norm_g, k_norm_g, rel_bias, w_out, loss_target, m_norm_g, m_w_in, m_q_norm_g, m_k_norm_g, m_rel_bias, m_w_out, v_norm_g, v_w_in, v_q_norm_g, v_k_norm_g, v_rel_bias, v_w_out):
    nl, d, nb = w_in.shape
    s = x.shape[1]
    ds_ = d // 2
    nh = ds_ // HEAD
    rb = w_out.shape[1]
    nrel = rel_bias.shape[2]
    bx = lax.axis_index("x") * 2 + lax.axis_index("y")

    rb_rows = -(-(nl * nh * nrel) // 1024) * 8
    cx = lax.axis_index("c")
    wi_full = [_cast_block(w_in, l, bx, "cast_w_in") for l in range(nl)]
    wo_full = [_cast_block(w_out, l, bx, "cast_w_out") for l in range(nl)]
    wi_full[0], wo_full[0], rel_all = _run_comm(
        _gather_comm([wi_full[0], wo_full[0]], _pack([rel_bias], rb_rows)), "gather_first")
    rel_full = jnp.concatenate(
        [rel_all[j].reshape(-1)[:nl * nh * nrel].reshape(nl, nh, nrel) for j in range(4)], axis=2)
    ext_idx = _ext_index()
    onehot = jnp.asarray(ext_idx[:, None] == np.arange(N_REL)[None, :], F32)
    ext = jnp.einsum("lhr,ur->lhu", rel_full, onehot, precision=lax.Precision.HIGHEST).reshape(nl, nh, 1, EXT)

    xs, hs, projs, yas, lts, ybs, mixes = [], [], [], [], [], [], []
    xc = x[0]
    for l in range(nl):
        h = _rmsnorm_fwd(xc, norm_g[l:l + 1])
        nxt = l + 1 < nl
        (proj,), got = _in_proj(h, wi_full[l], _gather_comm([wo_full[l + 1]], stage="chips") if nxt else None)
        if nxt:
            wo_full[l + 1] = got[0]
        (ya, lt), got = _attn_a_fwd(proj, nh, _gather_comm([wi_full[l + 1]], stage="chips") if nxt else None)
        if nxt:
            wi_full[l + 1] = got[0]
        (yb,), got = _attn_b_fwd(proj, q_norm_g[l:l + 1], k_norm_g[l:l + 1], ext[l], nh,
                                 _gather_comm([wi_full[l + 1], wo_full[l + 1]], stage="sibling") if nxt else None)
        if nxt:
            wi_full[l + 1], wo_full[l + 1] = got
        xs.append(xc)
        xc, mix = _out_proj(xc, ya, yb, proj, wo_full[l].reshape(4 * rb, d))
        hs.append(h); projs.append(proj); yas.append(ya); lts.append(lt); ybs.append(yb); mixes.append(mix)
    dx, loss_tile = _loss_head(xc, loss_target[0])

    small, g_wi, g_wo = [None] * nl, [None] * nl, [None] * nl
    pending = None

    def keep(lay, shared):
        g_wi[lay], g_wo[lay] = shared[0].reshape(d, nb), shared[1].reshape(rb, d)

    for l in reversed(range(nl)):
        wo = wo_full[l].reshape(4 * rb, d)
        p_wo = _wgrad(mixes[l], dx, 4, False, "wgrad_out")
        dya, dyb, dga, dgb = _out_proj_bwd(dx, yas[l], ybs[l], projs[l], wo)
        (dqa, dka, dva), got = _attn_a_bwd(projs[l], lts[l], dya, nh, _chips_comm(pending[1]) if pending else None)
        tots = [_add_chips(pending[1][t], got[t], bx, cx) for t in range(2)] if pending else None
        (dqb, dkb, dvb, dqg, dkg, dext), shared = _attn_b_bwd(
            projs[l], dyb, q_norm_g[l:l + 1], k_norm_g[l:l + 1], ext[l], nh, _halves_comm(tots) if pending else None)
        if pending:
            keep(pending[0], shared)
        dproj = jnp.concatenate([dqa, dka, dva, dga, dqb, dkb, dvb, dgb], axis=1)
        parts = [_wgrad(hs[l], dproj, 4, True, "wgrad_in"), p_wo]
        if l > 0:
            (dx, dng), theirs = _in_proj_bwd(dproj, wi_full[l], xs[l], dx, norm_g[l:l + 1], _sibling_comm(parts))
            pending = (l, [_add_sibling(parts[t], theirs[t], cx) for t in range(2)])
        else:
            theirs = _run_comm(_sibling_comm(parts), "reduce_sibling")
            sums = [_add_sibling(parts[t], theirs[t], cx) for t in range(2)]
            (dx, dng), got = _in_proj_bwd(dproj, wi_full[l], xs[l], dx, norm_g[l:l + 1], _chips_comm(sums))
            keep(0, _run_comm(_halves_comm([_add_chips(sums[t], got[t], bx, cx) for t in range(2)]), "share_halves"))
        small[l] = (dng[0], jnp.sum(dqg, axis=0).reshape(-1), jnp.sum(dkg, axis=0).reshape(-1), dext.reshape(nh, EXT))
    grad_x = dx[None]

    small_shapes = [(nl, d), (nl, HEAD), (nl, HEAD), (nl, nh, EXT), (1,)]
    small_parts = [jnp.stack([sm[i] for sm in small]) for i in range(4)] + [loss_tile[0, :1]]
    rows = -(-sum(int(np.prod(sh)) for sh in small_shapes) // 1024) * 8
    tot = _sum_devices(_gather_small(_pack(small_parts, rows)))
    g_ng, g_qg, g_kg, g_ext, loss = _unpack(tot, small_shapes)
    g_rel_full = jnp.einsum("lhu,ur->lhr", g_ext, onehot, precision=lax.Precision.HIGHEST)
    g_rel = lax.dynamic_slice_in_dim(g_rel_full, bx * nrel, nrel, axis=2)

    res_wi, res_wo = (), ()
    for l in range(nl):
        res_wi = _adamw_layer(l, w_in, g_wi[l], m_w_in, v_w_in, res_wi, "adamw_w_in")
        res_wo = _adamw_layer(l, w_out, g_wo[l], m_w_out, v_w_out, res_wo, "adamw_w_out")
    g_wi, d_wi, nm_wi, nv_wi = res_wi
    g_wo, d_wo, nm_wo, nv_wo = res_wo
    sm_shapes = [(nl, d), (nl, HEAD), (nl, HEAD), (nl, nh, nrel)]
    sm_rows = -(-sum(int(np.prod(sh)) for sh in sm_shapes) // 1024) * 8
    pw, pg, pm, pv = [_pack(group, sm_rows) for group in (
        (norm_g, q_norm_g, k_norm_g, rel_bias), (g_ng, g_qg, g_kg, g_rel),
        (m_norm_g, m_q_norm_g, m_k_norm_g, m_rel_bias), (v_norm_g, v_q_norm_g, v_k_norm_g, v_rel_bias))]
    d_sm, nm_sm, nv_sm = [_unpack(a[0], sm_shapes)
                          for a in _adamw_layer(0, pw[None], pg, pm[None], pv[None], (), "adamw_small")[1:]]

    return (loss[0], grad_x, g_ng, g_wi, g_qg, g_kg, g_rel, g_wo,
            d_sm[0], d_wi, d_sm[1], d_sm[2], d_sm[3], d_wo,
            nm_sm[0], nm_wi, nm_sm[1], nm_sm[2], nm_sm[3], nm_wo,
            nv_sm[0], nv_wi, nv_sm[1], nv_sm[2], nv_sm[3], nv_wo)
```

```python
from typing import Callable, NamedTuple

import jax
import jax.numpy as jnp
import numpy as np
from jax import lax
from jax.experimental import pallas as pl
from jax.experimental.pallas import tpu as pltpu

F32 = jnp.float32
BF16 = jnp.bfloat16

HEAD = 128
CHUNK = 64
LEFT_CHUNKS = 8
REL_CLIP = 256
N_REL = REL_CLIP + CHUNK
NORM_EPS = 1e-6
NEG_BIG = -1e30
TQ = 256
ROWS = 32
PAD = LEFT_CHUNKS * CHUNK
WIN = PAD + TQ
EXT = 1024
SCALE = HEAD ** -0.5

ADAM_LR = 0.001
ADAM_B1 = 0.9
ADAM_B2 = 0.999
ADAM_EPS = 1e-08
ADAM_WD = 0.01
ADAM_STEP = 10

ANY = pl.BlockSpec(memory_space=pl.ANY)
MESH = pl.DeviceIdType.MESH


def _params(sem=None, vmem_mb=None):
    kw = {}
    if sem is not None:
        kw["dimension_semantics"] = sem
    if vmem_mb is not None:
        kw["vmem_limit_bytes"] = vmem_mb << 20
    return pltpu.CompilerParams(**kw)


class _Comm(NamedTuple):
    ins: tuple
    outs: tuple
    aliases: dict
    n_sems: int
    start: Callable
    finish: Callable


def _call(body, *, name, grid, in_specs, out_specs, out_shape, args, scratch=(), sem=None, vmem_mb=None, comm=None):
    if comm is None:
        out = pl.pallas_call(body, name=name, grid=grid, in_specs=in_specs, out_specs=out_specs, out_shape=out_shape,
                             scratch_shapes=list(scratch), compiler_params=_params(sem, vmem_mb))(*args)
        return out, ()
    n_in, n_out, n_ci, n_co = len(in_specs), len(out_shape), len(comm.ins), len(comm.outs)

    def hosted(*refs):
        ins, cins = refs[:n_in], refs[n_in:n_in + n_ci]
        outs, couts = refs[n_in + n_ci:n_in + n_ci + n_out], refs[n_in + n_ci + n_out:n_in + n_ci + n_out + n_co]
        rest = refs[n_in + n_ci + n_out + n_co:]
        send_sems, recv_sems = rest[-2:]
        first, last = None, None
        for ax, size in enumerate(grid):
            at = pl.program_id(ax)
            first = (at == 0) if first is None else first & (at == 0)
            last = (at == size - 1) if last is None else last & (at == size - 1)

        @pl.when(first)
        def _():
            comm.start(cins, couts, send_sems, recv_sems)

        body(*ins, *outs, *rest[:-2])

        @pl.when(last)
        def _():
            comm.finish(cins, couts, send_sems, recv_sems)

    out = pl.pallas_call(
        hosted, name=name, grid=grid, in_specs=list(in_specs) + [ANY] * n_ci, out_specs=list(out_specs) + [ANY] * n_co,
        out_shape=list(out_shape) + list(comm.outs),
        input_output_aliases={n_in + k: n_out + v for k, v in comm.aliases.items()},
        scratch_shapes=list(scratch) + [pltpu.SemaphoreType.DMA((comm.n_sems,)), pltpu.SemaphoreType.DMA((comm.n_sems,))],
        compiler_params=_params(("arbitrary",) * len(grid), vmem_mb))(*args, *comm.ins)
    return out[:n_out], out[n_out:]


def _dot(a, b):
    return jnp.dot(a, b, preferred_element_type=F32)


def _dot_nt(a, b):
    return lax.dot_general(a, b, (((1,), (1,)), ((), ())), preferred_element_type=F32)


def _dot_tn(a, b):
    return lax.dot_general(a, b, (((0,), (0,)), ((), ())), preferred_element_type=F32)


def _split_dot(x, m):
    hi = x.astype(BF16)
    lo = (x - hi.astype(F32)).astype(BF16)
    return _dot(hi, m) + _dot(lo, m)


def _silu_parts(g):
    sg = 1.0 / (1.0 + jnp.exp(-g))
    return g * sg, sg * (1.0 + g * (1.0 - sg))


def _idx(*vals):
    return jnp.stack([jnp.asarray(v, jnp.int32) for v in vals])


def _cast_block(w, l, blk, name):
    _, r, c = w.shape
    tr = min(r, 512)

    def body(b_ref, w_ref, o_ref):
        o_ref[...] = w_ref[...].astype(BF16)

    spec = pltpu.PrefetchScalarGridSpec(
        num_scalar_prefetch=1, grid=(r // tr,),
        in_specs=[pl.BlockSpec((None, tr, c), lambda i, b: (l, i, 0))],
        out_specs=pl.BlockSpec((None, tr, c), lambda i, b: (b[0], i, 0)))
    return pl.pallas_call(body, name=name, grid_spec=spec, out_shape=jax.ShapeDtypeStruct((4, r, c), BF16),
                          compiler_params=_params(("parallel",)))(_idx(blk), w)


def _add_sibling(p, theirs, core):
    nblk, r, c = p.shape
    hr = r // 2
    tr = min(hr, 256)
    per = hr // tr

    def body(c_ref, p_ref, t_ref, o_ref):
        o_ref[...] = (p_ref[...].astype(F32) + t_ref[...].astype(F32)).astype(BF16)

    blk = pl.BlockSpec((None, tr, c), lambda j, i, cr: (j, i, 0))
    spec = pltpu.PrefetchScalarGridSpec(
        num_scalar_prefetch=1, grid=(nblk, per),
        in_specs=[pl.BlockSpec((None, tr, c), lambda j, i, cr: (j, cr[0] * per + i, 0)), blk], out_specs=blk)
    return pl.pallas_call(body, name="add_sibling", grid_spec=spec, out_shape=jax.ShapeDtypeStruct((nblk, hr, c), BF16),
                          compiler_params=_params(("parallel", "parallel")))(_idx(core), p, theirs)


def _add_chips(sums, got, blk, core):
    _, hr, c = sums.shape
    tr = min(hr, 256)

    def body(i_ref, s_ref, g0_ref, g1_ref, g2_ref, o_ref):
        o_ref[...] = ((s_ref[...].astype(F32) + g0_ref[...].astype(F32))
                      + g1_ref[...].astype(F32)) + g2_ref[...].astype(F32)

    at = lambda j: pl.BlockSpec((None, tr, c), lambda i, ir: (j, i, 0))
    spec = pltpu.PrefetchScalarGridSpec(
        num_scalar_prefetch=1, grid=(hr // tr,),
        in_specs=[pl.BlockSpec((None, tr, c), lambda i, ir: (ir[0], i, 0)), at(0), at(1), at(2)],
        out_specs=pl.BlockSpec((None, tr, c), lambda i, ir: (ir[1], i, 0)))
    return pl.pallas_call(body, name="add_chips", grid_spec=spec, out_shape=jax.ShapeDtypeStruct((2, hr, c), F32),
                          compiler_params=_params(("parallel",)))(_idx(blk, core), sums, got, got, got)


def _adamw_layer(l, w, g, m, v, prev, name):
    nl, r, c = w.shape
    tr = min(r, 256)
    c1 = 1.0 / (1.0 - ADAM_B1 ** ADAM_STEP)
    c2 = 1.0 / (1.0 - ADAM_B2 ** ADAM_STEP)

    def body(w_ref, g_ref, m_ref, v_ref, *rest):
        go_ref, d_ref, nm_ref, nv_ref = rest[-4:]
        gg = g_ref[...]
        nm = ADAM_B1 * m_ref[...] + (1.0 - ADAM_B1) * gg
        nv = ADAM_B2 * v_ref[...] + (1.0 - ADAM_B2) * (gg * gg)
        upd = (nm * c1) / (jnp.sqrt(nv * c2) + ADAM_EPS) + ADAM_WD * w_ref[...]
        go_ref[...] = gg
        d_ref[...] = -ADAM_LR * upd
        nm_ref[...] = nm
        nv_ref[...] = nv

    lay = pl.BlockSpec((None, tr, c), lambda i: (l, i, 0))
    shp = jax.ShapeDtypeStruct((nl, r, c), F32)
    return pl.pallas_call(
        body, name=name, grid=(r // tr,),
        in_specs=[lay, pl.BlockSpec((tr, c), lambda i: (i, 0)), lay, lay] + [ANY] * len(prev),
        out_specs=[lay] * 4, out_shape=[shp] * 4, input_output_aliases={4 + k: k for k in range(len(prev))},
        compiler_params=_params(("parallel",), 40))(w, g, m, v, *prev)


def _rmsnorm_fwd(x, g):
    s, d = x.shape
    tm = min(s, 256)

    def body(x_ref, g_ref, h_ref):
        xv = x_ref[...]
        r = lax.rsqrt(jnp.mean(xv * xv, axis=1, keepdims=True) + NORM_EPS)
        h_ref[...] = (xv * r * g_ref[...]).astype(BF16)

    return pl.pallas_call(
        body, name="rmsnorm_fwd", grid=(s // tm,),
        in_specs=[pl.BlockSpec((tm, d), lambda i: (i, 0)), pl.BlockSpec((1, d), lambda i: (0, 0))],
        out_specs=pl.BlockSpec((tm, d), lambda i: (i, 0)),
        out_shape=jax.ShapeDtypeStruct((s, d), BF16),
        compiler_params=_params(("parallel",)))(x, g)


def _in_proj(h, w, comm=None):
    s, d = h.shape
    nblk, _, nb = w.shape
    tm, tn = min(s, 512), min(nb, 1024)
    per = nb // tn

    def body(h_ref, w_ref, o_ref, wt_ref):
        o_ref[...] = _dot(h_ref[...], w_ref[...])

        @pl.when(pl.program_id(1) == 0)
        def _():
            for c in range(tn // HEAD):
                wt_ref[pl.ds(c * HEAD, HEAD), :] = w_ref[:, c * HEAD:(c + 1) * HEAD].astype(F32).T.astype(BF16)

    return _call(
        body, name="in_proj", grid=(nblk * per, s // tm),
        in_specs=[pl.BlockSpec((tm, d), lambda n, m: (m, 0)),
                  pl.BlockSpec((None, d, tn), lambda n, m: (n // per, 0, n % per))],
        out_specs=[pl.BlockSpec((tm, tn), lambda n, m: (m, n)), pl.BlockSpec((tn, d), lambda n, m: (n, 0))],
        out_shape=[jax.ShapeDtypeStruct((s, nblk * nb), F32), jax.ShapeDtypeStruct((nblk * nb, d), BF16)],
        sem=("parallel", "arbitrary"), vmem_mb=48, args=(h, w), comm=comm)


def _heads_per_step(nh):
    return 2 if nh % 2 == 0 else 1


def _head(hh):
    return slice(hh * HEAD, (hh + 1) * HEAD)


def _tri(op):
    r = lax.broadcasted_iota(jnp.int32, (TQ, TQ), 0)
    c = lax.broadcasted_iota(jnp.int32, (TQ, TQ), 1)
    return op(r, c)


def _staggered(groups, hp):
    for hh in range(hp):
        for fn in groups[0]:
            fn(hh)
    for group in groups[1:]:
        for hh in range(hp):
            for fn in group:
                fn(hh)


def _transpose_tiles(src_ref, dst_ref, hp, nq):
    for hh in range(hp):
        for t in range(nq):
            dst_ref[hh, t] = src_ref[pl.ds(t * TQ, TQ), _head(hh)].T.astype(BF16)


def _chunk_causal(r):
    row = lax.broadcasted_iota(jnp.int32, (ROWS, TQ), 0) + r * ROWS
    return row > lax.broadcasted_iota(jnp.int32, (ROWS, TQ), 1)


def _sb_logs(qk, causal):
    z = qk * SCALE
    l1p = jnp.log(1.0 + jnp.exp(-jnp.abs(z)))
    ls = jnp.minimum(-z, 0.0) - l1p
    if causal is not None:
        ls = jnp.where(causal, ls, 0.0)
    return ls, jnp.minimum(z, 0.0) - l1p


def _attn_a_fwd(proj, nh, comm=None):
    s = proj.shape[0]
    nq = s // TQ
    hp = _heads_per_step(nh)
    ng = nh // hp

    def body(q_ref, k_ref, v_ref, o_ref, lt_ref, acc_ref, qb_ref, m_ref, car_ref, sum_ref,
             z_ref, lsig_ref, aft_ref, hi_ref, lo_ref, w_ref, kt_ref):
        i = pl.program_id(1)

        @pl.when(i == 0)
        def _():
            _transpose_tiles(k_ref, kt_ref, hp, nq)

        qb_ref[...] = q_ref[...].astype(BF16)
        m_ref[...] = _tri(lambda r, c: r > c).astype(BF16)
        acc_ref[...] = jnp.zeros_like(acc_ref)
        car_ref[...] = jnp.zeros_like(car_ref)

        def tile(j, diag):
            off = pl.multiple_of(j * TQ, TQ)

            def scores(hh):
                z_ref[hh] = _dot(qb_ref[:, _head(hh)], kt_ref[hh, j])

            def logs(hh):
                for r in range(TQ // ROWS):
                    sl = pl.ds(r * ROWS, ROWS)
                    ls, lsig = _sb_logs(z_ref[hh, sl, :], _chunk_causal(r) if diag else None)
                    lsig_ref[hh, sl, :] = lsig
                    hi = ls.astype(BF16)
                    hi_ref[hh, sl, :] = hi
                    lo_ref[hh, sl, :] = (ls - hi.astype(F32)).astype(BF16)
                    sum_ref[hh, sl, :] = jnp.sum(ls, axis=1, keepdims=True)

            def after(hh):
                aft_ref[hh] = _dot(hi_ref[hh], m_ref[...]) + _dot(lo_ref[hh], m_ref[...])

            def weights(hh):
                for r in range(TQ // ROWS):
                    sl = pl.ds(r * ROWS, ROWS)
                    w = jnp.exp(lsig_ref[hh, sl, :] + aft_ref[hh, sl, :] + car_ref[hh, sl, :])
                    if diag:
                        w = jnp.where(_chunk_causal(r), w, 0.0)
                    w_ref[hh, sl, :] = w.astype(BF16)
                    car_ref[hh, sl, :] += sum_ref[hh, sl, :]

            def values(hh):
                vb = v_ref[pl.ds(off, TQ), _head(hh)].astype(BF16)
                acc_ref[:, _head(hh)] += _dot(w_ref[hh], vb)

            _staggered([(scores,), (logs, after), (weights, values)], hp)

        tile(i, True)

        def step(t, carry):
            tile(i - 1 - t, False)
            return carry

        lax.fori_loop(0, i, step, 0)
        o_ref[...] = acc_ref[...]
        lt_ref[...] = car_ref[...]

    wd = hp * HEAD
    sq = lambda dt: pltpu.VMEM((hp, TQ, TQ), dt)
    return _call(
        body, name="attn_a_fwd", grid=(ng, nq),
        in_specs=[pl.BlockSpec((TQ, wd), lambda h, i: (i, h)),
                  pl.BlockSpec((s, wd), lambda h, i: (0, ng + h)),
                  pl.BlockSpec((s, wd), lambda h, i: (0, 2 * ng + h))],
        out_specs=[pl.BlockSpec((TQ, wd), lambda h, i: (i, h)),
                   pl.BlockSpec((hp, TQ, 1), lambda h, i: (h, i, 0))],
        out_shape=[jax.ShapeDtypeStruct((s, nh * HEAD), F32), jax.ShapeDtypeStruct((nh, s, 1), F32)],
        scratch=[pltpu.VMEM((TQ, wd), F32), pltpu.VMEM((TQ, wd), BF16), pltpu.VMEM((TQ, TQ), BF16),
                 pltpu.VMEM((hp, TQ, 1), F32), pltpu.VMEM((hp, TQ, 1), F32),
                 sq(F32), sq(F32), sq(F32), sq(BF16), sq(BF16), sq(BF16),
                 pltpu.VMEM((hp, nq, HEAD, TQ), BF16)],
        sem=("parallel", "arbitrary"), args=(proj, proj, proj), comm=comm)


def _band_valid(i):
    cl = lax.broadcasted_iota(jnp.int32, (TQ, WIN), 0) // CHUNK
    kl = lax.broadcasted_iota(jnp.int32, (TQ, WIN), 1) // CHUNK
    first = LEFT_CHUNKS - (TQ // CHUNK) * i
    return (kl >= cl) & (kl <= cl + LEFT_CHUNKS) & (kl >= first)


def _build_bias(e_ref, bias_ref):
    e8 = jnp.broadcast_to(e_ref[...], (8, EXT))
    row = lax.broadcasted_iota(jnp.int32, (8, EXT), 0)
    t8 = jnp.zeros((8, EXT), F32)
    for b in range(8):
        t8 = jnp.where(row == b, pltpu.roll(e8, b, 1) if b else e8, t8)
    for a in range(TQ // 8):
        sl = pltpu.roll(t8, 8 * a, 1) if a else t8
        bias_ref[pl.ds(8 * a, 8), :] = sl[:, :WIN]


def _reduce_bias_grad(db_ref):
    acc = jnp.zeros((8, EXT), F32)
    for a in range(TQ // 8):
        sl = db_ref[pl.ds(8 * a, 8), :]
        acc = acc + (pltpu.roll(sl, EXT - 8 * a, 1) if a else sl)
    row = lax.broadcasted_iota(jnp.int32, (8, EXT), 0)
    tot = jnp.zeros((8, EXT), F32)
    for b in range(8):
        tot = tot + jnp.where(row == b, pltpu.roll(acc, EXT - b, 1) if b else acc, 0.0)
    return jnp.sum(tot, axis=0, keepdims=True)


def _band_fill(k_ref, v_ref, kg_ref, kn_pad, v_pad, s):
    k = k_ref[...]
    rk = lax.rsqrt(jnp.mean(k * k, axis=1, keepdims=True) + NORM_EPS)
    kn_pad[pl.ds(0, PAD), :] = jnp.zeros((PAD, HEAD), BF16)
    kn_pad[pl.ds(PAD, s), :] = (k * rk * kg_ref[...]).astype(BF16)
    v_pad[pl.ds(0, PAD), :] = jnp.zeros((PAD, HEAD), BF16)
    v_pad[pl.ds(PAD, s), :] = v_ref[...].astype(BF16)


def _band_probs(q_ref, qg_ref, kn_pad, bias_ref, i):
    q = q_ref[...]
    rq = lax.rsqrt(jnp.mean(q * q, axis=1, keepdims=True) + NORM_EPS)
    qhat = q * rq
    qn = (qhat * qg_ref[...]).astype(BF16)
    off = pl.multiple_of(i * TQ, TQ)
    kw = kn_pad[pl.ds(off, WIN), :]
    sc = _dot_nt(qn, kw) * SCALE + bias_ref[...]
    sc = jnp.where(_band_valid(i), sc, NEG_BIG)
    p = jnp.exp(sc - jnp.max(sc, axis=1, keepdims=True))
    pn = p / jnp.sum(p, axis=1, keepdims=True)
    return rq, qhat, qn, kw, off, pn


def _attn_b_fwd(proj, qg, kg, ext, nh, comm=None):
    s = proj.shape[0]
    nq = s // TQ

    def body(q_ref, k_ref, v_ref, qg_ref, kg_ref, e_ref, o_ref, kn_pad, v_pad, bias_ref):
        i = pl.program_id(1)

        @pl.when(i == 0)
        def _():
            _band_fill(k_ref, v_ref, kg_ref, kn_pad, v_pad, s)
            _build_bias(e_ref, bias_ref)

        _, _, _, _, off, pn = _band_probs(q_ref, qg_ref, kn_pad, bias_ref, i)
        o_ref[...] = _dot(pn.astype(BF16), v_pad[pl.ds(off, WIN), :])

    vec = pl.BlockSpec((1, HEAD), lambda h, i: (0, 0))
    return _call(
        body, name="attn_b_fwd", grid=(nh, nq),
        in_specs=[pl.BlockSpec((TQ, HEAD), lambda h, i: (i, 4 * nh + h)),
                  pl.BlockSpec((s, HEAD), lambda h, i: (0, 5 * nh + h)),
                  pl.BlockSpec((s, HEAD), lambda h, i: (0, 6 * nh + h)),
                  vec, vec,
                  pl.BlockSpec((None, 1, EXT), lambda h, i: (h, 0, 0))],
        out_specs=[pl.BlockSpec((TQ, HEAD), lambda h, i: (i, h))],
        out_shape=[jax.ShapeDtypeStruct((s, nh * HEAD), F32)],
        scratch=[pltpu.VMEM((s + PAD, HEAD), BF16), pltpu.VMEM((s + PAD, HEAD), BF16), pltpu.VMEM((TQ, WIN), F32)],
        sem=("parallel", "arbitrary"), args=(proj, proj, proj, qg, kg, ext), comm=comm)


def _out_proj(x, ya, yb, proj, w):
    s, d = x.shape
    ds_ = ya.shape[1]
    tm = min(s, 256)

    def body(x_ref, ya_ref, yb_ref, ga_ref, gb_ref, w_ref, o_ref, mix_ref):
        ma = (ya_ref[...] * _silu_parts(ga_ref[...])[0]).astype(BF16)
        mb = (yb_ref[...] * _silu_parts(gb_ref[...])[0]).astype(BF16)
        mix_ref[:, :ds_] = ma
        mix_ref[:, ds_:] = mb
        o_ref[...] = x_ref[...] + _dot(ma, w_ref[pl.ds(0, ds_), :]) + _dot(mb, w_ref[pl.ds(ds_, ds_), :])

    row = lambda width: pl.BlockSpec((tm, width), lambda i: (i, 0))
    return pl.pallas_call(
        body, name="out_proj", grid=(s // tm,),
        in_specs=[row(d), row(ds_), row(ds_),
                  pl.BlockSpec((tm, ds_), lambda i: (i, 3)), pl.BlockSpec((tm, ds_), lambda i: (i, 7)),
                  pl.BlockSpec((2 * ds_, d), lambda i: (0, 0))],
        out_specs=[row(d), row(2 * ds_)],
        out_shape=[jax.ShapeDtypeStruct((s, d), F32), jax.ShapeDtypeStruct((s, 2 * ds_), BF16)],
        compiler_params=_params(("parallel",), 48))(x, ya, yb, proj, proj, w)


def _loss_head(y, tgt):
    s, d = y.shape
    tm = min(s, 256)

    def body(y_ref, t_ref, dy_ref, l_ref):
        @pl.when(pl.program_id(0) == 0)
        def _():
            l_ref[...] = jnp.zeros_like(l_ref)

        err = y_ref[...] - t_ref[...]
        dy_ref[...] = err * (1.0 / d)
        l_ref[...] += 0.5 * jnp.sum(jnp.mean(err * err, axis=1, keepdims=True), axis=0, keepdims=True)

    row = pl.BlockSpec((tm, d), lambda i: (i, 0))
    return pl.pallas_call(
        body, name="loss_head", grid=(s // tm,), in_specs=[row, row],
        out_specs=[row, pl.BlockSpec((8, 128), lambda i: (0, 0))],
        out_shape=[jax.ShapeDtypeStruct((s, d), F32), jax.ShapeDtypeStruct((8, 128), F32)],
        compiler_params=_params(("arbitrary",)))(y, tgt)


def _out_proj_bwd(dxo, ya, yb, proj, w):
    s, d = dxo.shape
    ds_ = ya.shape[1]
    tm = min(s, 256)

    def body(dx_ref, ya_ref, yb_ref, ga_ref, gb_ref, w_ref, dya_ref, dyb_ref, dga_ref, dgb_ref):
        dxb = dx_ref[...].astype(BF16)
        for y_ref, g_ref, lo, dy_ref, dg_ref in ((ya_ref, ga_ref, 0, dya_ref, dga_ref),
                                                 (yb_ref, gb_ref, ds_, dyb_ref, dgb_ref)):
            dmix = _dot_nt(dxb, w_ref[pl.ds(lo, ds_), :])
            act, dact = _silu_parts(g_ref[...])
            dy_ref[...] = dmix * act
            dg_ref[...] = (dmix * y_ref[...] * dact).astype(BF16)

    row = lambda width: pl.BlockSpec((tm, width), lambda i: (i, 0))
    return pl.pallas_call(
        body, name="out_proj_bwd", grid=(s // tm,),
        in_specs=[row(d), row(ds_), row(ds_),
                  pl.BlockSpec((tm, ds_), lambda i: (i, 3)), pl.BlockSpec((tm, ds_), lambda i: (i, 7)),
                  pl.BlockSpec((2 * ds_, d), lambda i: (0, 0))],
        out_specs=[row(ds_)] * 4,
        out_shape=[jax.ShapeDtypeStruct((s, ds_), F32)] * 2 + [jax.ShapeDtypeStruct((s, ds_), BF16)] * 2,
        compiler_params=_params(("parallel",), 48))(dxo, ya, yb, proj, proj, w)


def _wgrad(a, b, nblk, col_blocks, name):
    s, m = a.shape
    n = b.shape[1]
    if col_blocks:
        tr = min(m, 512)
        nb = n // nblk
        tn = min(nb, 1024)
        per = nb // tn
        out_shape = (nblk, m, nb)
        out_spec = pl.BlockSpec((None, tr, tn), lambda j, r: (j // per, r, j % per))
    else:
        tn = min(n, 1024)
        tr = m // nblk
        out_shape = (nblk, tr, n)
        out_spec = pl.BlockSpec((None, tr, tn), lambda j, r: (r, 0, j))

    def body(a_ref, b_ref, o_ref):
        o_ref[...] = _dot_tn(a_ref[...].astype(BF16), b_ref[...].astype(BF16)).astype(BF16)

    return pl.pallas_call(
        body, name=name, grid=(n // tn, m // tr),
        in_specs=[pl.BlockSpec((s, tr), lambda j, r: (0, r)), pl.BlockSpec((s, tn), lambda j, r: (0, j))],
        out_specs=out_spec, out_shape=jax.ShapeDtypeStruct(out_shape, BF16),
        compiler_params=_params(("parallel", "parallel"), 48))(a, b)


def _attn_a_bwd(proj, lt, dya, nh, comm=None):
    s = proj.shape[0]
    nq = s // TQ
    hp = _heads_per_step(nh)
    ng = nh // hp

    def body(q_ref, k_ref, v_ref, lt_ref, do_ref, dq_ref, dk_ref, dv_ref, dq_acc, dk_acc, dv_acc,
             qb_ref, dob_ref, upto_ref, before_ref, cls_ref, cg_ref, sls_ref, sg_ref,
             z_ref, dw_ref, lsig_ref, pre_ref, g_ref, hi_ref, lo_ref, wb_ref, kt_ref, vt_ref, qt_ref, dot_ref):
        i = pl.program_id(1)

        @pl.when(i == 0)
        def _():
            dk_acc[...] = jnp.zeros_like(dk_acc)
            dv_acc[...] = jnp.zeros_like(dv_acc)
            _transpose_tiles(k_ref, kt_ref, hp, nq)
            _transpose_tiles(v_ref, vt_ref, hp, nq)

        dq_acc[...] = jnp.zeros_like(dq_acc)
        qb_ref[...] = q_ref[...].astype(BF16)
        dob_ref[...] = do_ref[...].astype(BF16)
        for hh in range(hp):
            qt_ref[hh] = q_ref[:, _head(hh)].T.astype(BF16)
            dot_ref[hh] = do_ref[:, _head(hh)].T.astype(BF16)
        upto_ref[...] = _tri(lambda r, c: r <= c).astype(BF16)
        before_ref[...] = _tri(lambda r, c: r < c).astype(BF16)
        cls_ref[...] = jnp.zeros_like(cls_ref)
        cg_ref[...] = jnp.zeros_like(cg_ref)

        def tile(j, diag):
            off = pl.multiple_of(j * TQ, TQ)

            def scores(hh):
                z_ref[hh] = _dot(qb_ref[:, _head(hh)], kt_ref[hh, j])
                dw_ref[hh] = _dot(dob_ref[:, _head(hh)], vt_ref[hh, j])

            def logs(hh):
                for r in range(TQ // ROWS):
                    sl = pl.ds(r * ROWS, ROWS)
                    ls, lsig = _sb_logs(z_ref[hh, sl, :], _chunk_causal(r) if diag else None)
                    lsig_ref[hh, sl, :] = lsig
                    hi = ls.astype(BF16)
                    hi_ref[hh, sl, :] = hi
                    lo_ref[hh, sl, :] = (ls - hi.astype(F32)).astype(BF16)
                    sls_ref[hh, sl, :] = jnp.sum(ls, axis=1, keepdims=True)

            def upto(hh):
                pre_ref[hh] = _dot(hi_ref[hh], upto_ref[...]) + _dot(lo_ref[hh], upto_ref[...])

            def weights(hh):
                for r in range(TQ // ROWS):
                    sl = pl.ds(r * ROWS, ROWS)
                    w = jnp.exp(lsig_ref[hh, sl, :] + (lt_ref[hh, sl, :] - (cls_ref[hh, sl, :] + pre_ref[hh, sl, :])))
                    if diag:
                        w = jnp.where(_chunk_causal(r), w, 0.0)
                    wb_ref[hh, sl, :] = w.astype(BF16)
                    g = w * dw_ref[hh, sl, :]
                    g_ref[hh, sl, :] = g
                    hi_ref[hh, sl, :] = g.astype(BF16)
                    sg_ref[hh, sl, :] = jnp.sum(g, axis=1, keepdims=True)

            def earlier(hh):
                dw_ref[hh] = _dot(hi_ref[hh], before_ref[...])

            def logit_grads(hh):
                for r in range(TQ // ROWS):
                    sl = pl.ds(r * ROWS, ROWS)
                    z = z_ref[hh, sl, :] * SCALE
                    e = jnp.exp(-jnp.abs(z))
                    rinv = 1.0 / (1.0 + e)
                    beta = jnp.where(z >= 0.0, rinv, e * rinv)
                    dz = g_ref[hh, sl, :] * (1.0 - beta) - beta * (cg_ref[hh, sl, :] + dw_ref[hh, sl, :])
                    if diag:
                        dz = jnp.where(_chunk_causal(r), dz, 0.0)
                    lo_ref[hh, sl, :] = (dz * SCALE).astype(BF16)
                    cls_ref[hh, sl, :] += sls_ref[hh, sl, :]
                    cg_ref[hh, sl, :] += sg_ref[hh, sl, :]

            def grads(hh):
                dq_acc[:, _head(hh)] += _dot(lo_ref[hh], k_ref[pl.ds(off, TQ), _head(hh)].astype(BF16))
                dk_acc[hh, j] += _dot(qt_ref[hh], lo_ref[hh])
                dv_acc[hh, j] += _dot(dot_ref[hh], wb_ref[hh])

            _staggered([(scores,), (logs, upto), (weights, earlier), (logit_grads, grads)], hp)

        def step(j, carry):
            tile(j, False)
            return carry

        lax.fori_loop(0, i, step, 0)
        tile(i, True)
        dq_ref[...] = dq_acc[...].astype(BF16)

        @pl.when(i == nq - 1)
        def _():
            for hh in range(hp):
                for t in range(nq):
                    dk_ref[pl.ds(t * TQ, TQ), _head(hh)] = dk_acc[hh, t].T.astype(BF16)
                    dv_ref[pl.ds(t * TQ, TQ), _head(hh)] = dv_acc[hh, t].T.astype(BF16)

    wd = hp * HEAD
    sq = lambda dt: pltpu.VMEM((hp, TQ, TQ), dt)
    tiles = lambda dt: pltpu.VMEM((hp, nq, HEAD, TQ), dt)
    blk = pl.BlockSpec((TQ, wd), lambda h, i: (i, h))
    col = pl.BlockSpec((s, wd), lambda h, i: (0, h))
    shp = jax.ShapeDtypeStruct((s, nh * HEAD), BF16)
    return _call(
        body, name="attn_a_bwd", grid=(ng, nq),
        in_specs=[blk,
                  pl.BlockSpec((s, wd), lambda h, i: (0, ng + h)),
                  pl.BlockSpec((s, wd), lambda h, i: (0, 2 * ng + h)),
                  pl.BlockSpec((hp, TQ, 1), lambda h, i: (h, i, 0)), blk],
        out_specs=[blk, col, col], out_shape=[shp] * 3,
        scratch=[pltpu.VMEM((TQ, wd), F32), tiles(F32), tiles(F32),
                 pltpu.VMEM((TQ, wd), BF16), pltpu.VMEM((TQ, wd), BF16),
                 pltpu.VMEM((TQ, TQ), BF16), pltpu.VMEM((TQ, TQ), BF16)]
        + [pltpu.VMEM((hp, TQ, 1), F32)] * 4 + [sq(F32)] * 5 + [sq(BF16)] * 3
        + [tiles(BF16), tiles(BF16), pltpu.VMEM((hp, HEAD, TQ), BF16), pltpu.VMEM((hp, HEAD, TQ), BF16)],
        sem=("parallel", "arbitrary"), args=(proj, proj, proj, lt, dya), comm=comm)


def _attn_b_bwd(proj, dyb, qg, kg, ext, nh, comm=None):
    s = proj.shape[0]
    nq = s // TQ

    def body(q_ref, k_ref, v_ref, do_ref, qg_ref, kg_ref, e_ref,
             dq_ref, dk_ref, dv_ref, dqg_ref, dkg_ref, de_ref,
             kn_pad, v_pad, bias_ref, db_acc, dkn_acc, dv_acc):
        i = pl.program_id(1)

        @pl.when(i == 0)
        def _():
            _band_fill(k_ref, v_ref, kg_ref, kn_pad, v_pad, s)
            _build_bias(e_ref, bias_ref)
            db_acc[...] = jnp.zeros_like(db_acc)
            dkn_acc[...] = jnp.zeros_like(dkn_acc)
            dv_acc[...] = jnp.zeros_like(dv_acc)
            dqg_ref[...] = jnp.zeros_like(dqg_ref)

        rq, qhat, qn, kw, off, pn = _band_probs(q_ref, qg_ref, kn_pad, bias_ref, i)
        dob = do_ref[...].astype(BF16)
        dp = _dot_nt(dob, v_pad[pl.ds(off, WIN), :])
        dsc = pn * (dp - jnp.sum(pn * dp, axis=1, keepdims=True))
        db_acc[:, :WIN] += dsc
        dsb = (dsc * SCALE).astype(BF16)
        dqn = _dot(dsb, kw)
        dkn_acc[pl.ds(off, WIN), :] += _dot_tn(dsb, qn)
        dv_acc[pl.ds(off, WIN), :] += _dot_tn(pn.astype(BF16), dob)
        dqh = dqn * qg_ref[...]
        dq_ref[...] = (rq * (dqh - qhat * jnp.mean(dqh * qhat, axis=1, keepdims=True))).astype(BF16)
        dqg_ref[...] += jnp.sum(dqn * qhat, axis=0, keepdims=True)

        @pl.when(i == nq - 1)
        def _():
            k = k_ref[...]
            rk = lax.rsqrt(jnp.mean(k * k, axis=1, keepdims=True) + NORM_EPS)
            khat = k * rk
            dkn = dkn_acc[pl.ds(PAD, s), :]
            dkh = dkn * kg_ref[...]
            dk_ref[...] = (rk * (dkh - khat * jnp.mean(dkh * khat, axis=1, keepdims=True))).astype(BF16)
            dkg_ref[...] = jnp.sum(dkn * khat, axis=0, keepdims=True)
            dv_ref[...] = dv_acc[pl.ds(PAD, s), :].astype(BF16)
            de_ref[...] = _reduce_bias_grad(db_acc)

    blk = pl.BlockSpec((TQ, HEAD), lambda h, i: (i, h))
    col = pl.BlockSpec((s, HEAD), lambda h, i: (0, h))
    vec = pl.BlockSpec((1, HEAD), lambda h, i: (0, 0))
    hvec = pl.BlockSpec((None, 1, HEAD), lambda h, i: (h, 0, 0))
    hext = pl.BlockSpec((None, 1, EXT), lambda h, i: (h, 0, 0))
    shp = jax.ShapeDtypeStruct((s, nh * HEAD), BF16)
    return _call(
        body, name="attn_b_bwd", grid=(nh, nq),
        in_specs=[pl.BlockSpec((TQ, HEAD), lambda h, i: (i, 4 * nh + h)),
                  pl.BlockSpec((s, HEAD), lambda h, i: (0, 5 * nh + h)),
                  pl.BlockSpec((s, HEAD), lambda h, i: (0, 6 * nh + h)),
                  blk, vec, vec, hext],
        out_specs=[blk, col, col, hvec, hvec, hext],
        out_shape=[shp] * 3 + [jax.ShapeDtypeStruct((nh, 1, HEAD), F32)] * 2
        + [jax.ShapeDtypeStruct((nh, 1, EXT), F32)],
        scratch=[pltpu.VMEM((s + PAD, HEAD), BF16), pltpu.VMEM((s + PAD, HEAD), BF16),
                 pltpu.VMEM((TQ, WIN), F32), pltpu.VMEM((TQ, EXT), F32),
                 pltpu.VMEM((s + PAD, HEAD), F32), pltpu.VMEM((s + PAD, HEAD), F32)],
        sem=("parallel", "arbitrary"), args=(proj, proj, proj, dyb, qg, kg, ext), comm=comm)


def _in_proj_bwd(dproj, wt, x, dxo, g, comm=None):
    s, d = x.shape
    tm, tk = min(s, 512), min(wt.shape[0], 1024)
    nk = wt.shape[0] // tk

    def body(dp_ref, w_ref, x_ref, dxo_ref, g_ref, dx_ref, dg_ref, acc):
        m, k = pl.program_id(0), pl.program_id(1)

        @pl.when(k == 0)
        def _():
            acc[...] = jnp.zeros_like(acc)

        @pl.when((k == 0) & (m == 0))
        def _():
            dg_ref[...] = jnp.zeros_like(dg_ref)

        acc[...] += _dot(dp_ref[...], w_ref[...])

        @pl.when(k == nk - 1)
        def _():
            xv = x_ref[...]
            r = lax.rsqrt(jnp.mean(xv * xv, axis=1, keepdims=True) + NORM_EPS)
            xhat = xv * r
            dh = acc[...]
            dxh = dh * g_ref[...]
            dx_ref[...] = dxo_ref[...] + r * (dxh - xhat * jnp.mean(dxh * xhat, axis=1, keepdims=True))
            dg_ref[...] += jnp.sum(dh * xhat, axis=0, keepdims=True)

    row = pl.BlockSpec((tm, d), lambda m, k: (m, 0))
    return _call(
        body, name="in_proj_bwd", grid=(s // tm, nk),
        in_specs=[pl.BlockSpec((tm, tk), lambda m, k: (m, k)),
                  pl.BlockSpec((tk, d), lambda m, k: (k, 0)),
                  row, row, pl.BlockSpec((1, d), lambda m, k: (0, 0))],
        out_specs=[row, pl.BlockSpec((8, d), lambda m, k: (0, 0))],
        out_shape=[jax.ShapeDtypeStruct((s, d), F32), jax.ShapeDtypeStruct((8, d), F32)],
        scratch=[pltpu.VMEM((tm, d), F32)], sem=("arbitrary", "arbitrary"), vmem_mb=56,
        args=(dproj, wt, x, dxo, g), comm=comm)


def _place():
    x, y, c = lax.axis_index("x"), lax.axis_index("y"), lax.axis_index("c")
    chips = [(1 - x, y), (x, 1 - y), (1 - x, 1 - y)]
    return x, y, c, chips


def _comm_call(body, name, ins, out_shape, n_remote, n_local, aliases=None):
    return pl.pallas_call(
        body, name=name, in_specs=[ANY] * len(ins), out_specs=[ANY] * len(out_shape), out_shape=out_shape,
        input_output_aliases=aliases or {},
        scratch_shapes=[pltpu.SemaphoreType.DMA((n_remote,)), pltpu.SemaphoreType.DMA((n_remote,)),
                        pltpu.SemaphoreType.DMA((n_local,))])(*ins)


def _rcopy(src, dst, send_sems, recv_sems, k, dev):
    return pltpu.make_async_remote_copy(src_ref=src, dst_ref=dst, send_sem=send_sems.at[k], recv_sem=recv_sems.at[k],
                                        device_id=dev, device_id_type=MESH)


def _run_comm(comm, name):
    n_ci, n_co = len(comm.ins), len(comm.outs)

    def body(*refs):
        cins, couts = refs[:n_ci], refs[n_ci:n_ci + n_co]
        send_sems, recv_sems = refs[n_ci + n_co:]
        comm.start(cins, couts, send_sems, recv_sems)
        comm.finish(cins, couts, send_sems, recv_sems)

    return pl.pallas_call(
        body, name=name, in_specs=[ANY] * n_ci, out_specs=[ANY] * n_co, out_shape=list(comm.outs),
        input_output_aliases=dict(comm.aliases),
        scratch_shapes=[pltpu.SemaphoreType.DMA((comm.n_sems,)), pltpu.SemaphoreType.DMA((comm.n_sems,))])(*comm.ins)


def _gather_comm(fulls, rbp=None, stage="all"):
    n = len(fulls)
    n_ici = 3 * n
    base = n_ici if stage == "all" else 0

    def half(full, blk, core):
        hr = full.shape[1] // 2
        return full.at[blk].at[pl.ds(core * hr, hr)]

    def ici(couts, send_sems, recv_sems, x, y, c, chips):
        b = 2 * x + y
        return [_rcopy(half(full, b, c), half(full, b, c), send_sems, recv_sems, 3 * t + j, (*chip, c))
                for t, full in enumerate(couts[:n]) for j, chip in enumerate(chips)]

    def landed(couts, send_sems, recv_sems, x, y, c, chips, core, first):
        return [_rcopy(half(full, 2 * chip[0] + chip[1], core), half(full, 2 * chip[0] + chip[1], core),
                       send_sems, recv_sems, first + 3 * t + j, (x, y, 1 - c))
                for t, full in enumerate(couts[:n]) for j, chip in enumerate(chips)]

    def small(cins, couts, send_sems, recv_sems, x, y, c, chips):
        b = 2 * x + y
        return ([_rcopy(cins[n], couts[n].at[b], send_sems, recv_sems, 2 * n_ici + j, (*chip, c))
                 for j, chip in enumerate(chips)],
                pltpu.make_async_copy(cins[n], couts[n].at[b], send_sems.at[2 * n_ici + 3]))

    def start(cins, couts, send_sems, recv_sems):
        x, y, c, chips = _place()
        if stage == "sibling":
            for cp in landed(couts, send_sems, recv_sems, x, y, c, chips, c, base):
                cp.start()
            return
        for cp in ici(couts, send_sems, recv_sems, x, y, c, chips):
            cp.start()
        if rbp is not None:
            remote, local = small(cins, couts, send_sems, recv_sems, x, y, c, chips)
            for cp in remote:
                cp.start()
            local.start()

    def finish(cins, couts, send_sems, recv_sems):
        x, y, c, chips = _place()
        passed = landed(couts, send_sems, recv_sems, x, y, c, chips, c, base)
        if stage != "sibling":
            for k, cp in enumerate(landed(couts, send_sems, recv_sems, x, y, c, chips, c, 0)):
                cp.wait_recv()
                if stage == "all":
                    passed[k].start()
            for cp in ici(couts, send_sems, recv_sems, x, y, c, chips):
                cp.wait_send()
        if stage != "chips":
            for cp in landed(couts, send_sems, recv_sems, x, y, c, chips, 1 - c, base):
                cp.wait_recv()
            for cp in passed:
                cp.wait_send()
        if rbp is not None:
            remote, local = small(cins, couts, send_sems, recv_sems, x, y, c, chips)
            for j, chip in enumerate(chips):
                got = couts[n].at[2 * chip[0] + chip[1]]
                _rcopy(got, got, send_sems, recv_sems, 2 * n_ici + j, (x, y, c)).wait_recv()
            for cp in remote:
                cp.wait_send()
            local.wait()

    outs = [jax.ShapeDtypeStruct(f.shape, f.dtype) for f in fulls]
    ins = list(fulls)
    if rbp is not None:
        ins.append(rbp)
        outs.append(jax.ShapeDtypeStruct((4,) + rbp.shape, F32))
    return _Comm(tuple(ins), tuple(outs), {t: t for t in range(n)}, 2 * n_ici + 4, start, finish)


def _chips_comm(sums):
    n = len(sums)

    def copies(cins, couts, send_sems, recv_sems):
        x, y, c, chips = _place()
        return [_rcopy(cins[t].at[2 * chip[0] + chip[1]], couts[t].at[j], send_sems, recv_sems, 3 * t + j, (*chip, c))
                for t in range(n) for j, chip in enumerate(chips)]

    def start(*refs):
        for cp in copies(*refs):
            cp.start()

    def finish(*refs):
        for cp in copies(*refs):
            cp.wait()

    outs = tuple(jax.ShapeDtypeStruct((3,) + p.shape[1:], p.dtype) for p in sums)
    return _Comm(tuple(sums), outs, {}, 3 * n, start, finish)


def _pair_comm(ins, outs, aliases, copies):
    def start(*refs):
        for cp in copies(*refs):
            cp.start()

    def finish(*refs):
        for cp in copies(*refs):
            cp.wait()

    return _Comm(tuple(ins), tuple(outs), aliases, len(ins), start, finish)


def _sibling_comm(parts):
    def copies(cins, couts, send_sems, recv_sems):
        x, y, c, _ = _place()
        return [_rcopy(p.at[:, pl.ds((1 - c) * (p.shape[1] // 2), p.shape[1] // 2), :], couts[t],
                       send_sems, recv_sems, t, (x, y, 1 - c)) for t, p in enumerate(cins)]

    half = [jax.ShapeDtypeStruct((p.shape[0], p.shape[1] // 2, p.shape[2]), p.dtype) for p in parts]
    return _pair_comm(parts, half, {}, copies)


def _halves_comm(tots):
    def copies(cins, couts, send_sems, recv_sems):
        x, y, c, _ = _place()
        return [_rcopy(g.at[c], g.at[c], send_sems, recv_sems, t, (x, y, 1 - c)) for t, g in enumerate(couts)]

    return _pair_comm(tots, [jax.ShapeDtypeStruct(t.shape, t.dtype) for t in tots],
                      {t: t for t in range(len(tots))}, copies)


def _gather_small(packed):
    def body(p_ref, all_ref, send_sems, recv_sems, loc_sems):
        x, y, c, _ = _place()
        me = 4 * x + 2 * y + c
        local = pltpu.make_async_copy(p_ref, all_ref.at[me], loc_sems.at[0])
        local.start()
        sent = []
        for k in range(1, 8):
            px, py, pc = x ^ (k >> 2), y ^ ((k >> 1) & 1), c ^ (k & 1)
            cp = _rcopy(p_ref, all_ref.at[me], send_sems, recv_sems, k - 1, (px, py, pc))
            cp.start()
            sent.append(cp)
        for k in range(1, 8):
            px, py, pc = x ^ (k >> 2), y ^ ((k >> 1) & 1), c ^ (k & 1)
            got = all_ref.at[4 * px + 2 * py + pc]
            _rcopy(got, got, send_sems, recv_sems, k - 1, (x, y, c)).wait_recv()
        for cp in sent:
            cp.wait_send()
        local.wait()

    return _comm_call(body, "gather_small", [packed], [jax.ShapeDtypeStruct((8,) + packed.shape, F32)], 7, 1)[0]


def _sum_devices(allp):
    n, r, c = allp.shape

    def body(a_ref, o_ref):
        acc = a_ref[0]
        for k in range(1, n):
            acc = acc + a_ref[k]
        o_ref[...] = acc

    return pl.pallas_call(body, name="sum_devices", out_shape=jax.ShapeDtypeStruct((r, c), F32))(allp)


def _ext_index():
    u = np.arange(EXT)
    dist = np.where(u < WIN, PAD - u, PAD + EXT - u)
    return np.clip(dist, -(CHUNK - 1), REL_CLIP) + (CHUNK - 1)


def _pack(parts, rows):
    flat = jnp.concatenate([p.reshape(-1) for p in parts])
    return jnp.pad(flat, (0, rows * 128 - flat.shape[0])).reshape(rows, 128)


def _unpack(packed, shapes):
    flat, out, at = packed.reshape(-1), [], 0
    for shp in shapes:
        size = int(np.prod(shp))
        out.append(flat[at:at + size].reshape(shp))
        at += size
    return out


def kernel(x, norm_g, w_in, q_norm_g, k_norm_g, rel_bias, w_out, loss_target, m_norm_g, m_w_in, m_q_norm_g, m_k_norm_g, m_rel_bias, m_w_out, v_norm_g, v_w_in, v_q_norm_g, v_k_norm_g, v_rel_bias, v_w_out):
    nl, d, nb = w_in.shape
    s = x.shape[1]
    ds_ = d // 2
    nh = ds_ // HEAD
    rb = w_out.shape[1]
    nrel = rel_bias.shape[2]
    bx = lax.axis_index("x") * 2 + lax.axis_index("y")

    rb_rows = -(-(nl * nh * nrel) // 1024) * 8
    cx = lax.axis_index("c")
    wi_full = [_cast_block(w_in, l, bx, "cast_w_in") for l in range(nl)]
    wo_full = [_cast_block(w_out, l, bx, "cast_w_out") for l in range(nl)]
    wi_full[0], wo_full[0], rel_all = _run_comm(
        _gather_comm([wi_full[0], wo_full[0]], _pack([rel_bias], rb_rows)), "gather_first")
    rel_full = jnp.concatenate(
        [rel_all[j].reshape(-1)[:nl * nh * nrel].reshape(nl, nh, nrel) for j in range(4)], axis=2)
    ext_idx = _ext_index()
    onehot = jnp.asarray(ext_idx[:, None] == np.arange(N_REL)[None, :], F32)
    ext = jnp.einsum("lhr,ur->lhu", rel_full, onehot, precision=lax.Precision.HIGHEST).reshape(nl, nh, 1, EXT)

    xs, hs, projs, yas, lts, ybs, mixes, wi_t = [], [], [], [], [], [], [], []
    xc = x[0]
    for l in range(nl):
        h = _rmsnorm_fwd(xc, norm_g[l:l + 1])
        nxt = l + 1 < nl
        (proj, wt), got = _in_proj(h, wi_full[l], _gather_comm([wo_full[l + 1]], stage="chips") if nxt else None)
        wi_t.append(wt)
        if nxt:
            wo_full[l + 1] = got[0]
        (ya, lt), got = _attn_a_fwd(proj, nh, _gather_comm([wi_full[l + 1]], stage="chips") if nxt else None)
        if nxt:
            wi_full[l + 1] = got[0]
        (yb,), got = _attn_b_fwd(proj, q_norm_g[l:l + 1], k_norm_g[l:l + 1], ext[l], nh,
                                 _gather_comm([wi_full[l + 1], wo_full[l + 1]], stage="sibling") if nxt else None)
        if nxt:
            wi_full[l + 1], wo_full[l + 1] = got
        xs.append(xc)
        xc, mix = _out_proj(xc, ya, yb, proj, wo_full[l].reshape(4 * rb, d))
        hs.append(h); projs.append(proj); yas.append(ya); lts.append(lt); ybs.append(yb); mixes.append(mix)
    dx, loss_tile = _loss_head(xc, loss_target[0])

    small, g_wi, g_wo = [None] * nl, [None] * nl, [None] * nl
    pending = None

    def keep(lay, shared):
        g_wi[lay], g_wo[lay] = shared[0].reshape(d, nb), shared[1].reshape(rb, d)

    for l in reversed(range(nl)):
        wo = wo_full[l].reshape(4 * rb, d)
        p_wo = _wgrad(mixes[l], dx, 4, False, "wgrad_out")
        dya, dyb, dga, dgb = _out_proj_bwd(dx, yas[l], ybs[l], projs[l], wo)
        (dqa, dka, dva), got = _attn_a_bwd(projs[l], lts[l], dya, nh, _chips_comm(pending[1]) if pending else None)
        tots = [_add_chips(pending[1][t], got[t], bx, cx) for t in range(2)] if pending else None
        (dqb, dkb, dvb, dqg, dkg, dext), shared = _attn_b_bwd(
            projs[l], dyb, q_norm_g[l:l + 1], k_norm_g[l:l + 1], ext[l], nh, _halves_comm(tots) if pending else None)
        if pending:
            keep(pending[0], shared)
        dproj = jnp.concatenate([dqa, dka, dva, dga, dqb, dkb, dvb, dgb], axis=1)
        parts = [_wgrad(hs[l], dproj, 4, True, "wgrad_in"), p_wo]
        if l > 0:
            (dx, dng), theirs = _in_proj_bwd(dproj, wi_t[l], xs[l], dx, norm_g[l:l + 1], _sibling_comm(parts))
            pending = (l, [_add_sibling(parts[t], theirs[t], cx) for t in range(2)])
        else:
            theirs = _run_comm(_sibling_comm(parts), "reduce_sibling")
            sums = [_add_sibling(parts[t], theirs[t], cx) for t in range(2)]
            (dx, dng), got = _in_proj_bwd(dproj, wi_t[l], xs[l], dx, norm_g[l:l + 1], _chips_comm(sums))
            keep(0, _run_comm(_halves_comm([_add_chips(sums[t], got[t], bx, cx) for t in range(2)]), "share_halves"))
        small[l] = (dng[0], jnp.sum(dqg, axis=0).reshape(-1), jnp.sum(dkg, axis=0).reshape(-1), dext.reshape(nh, EXT))
    grad_x = dx[None]

    small_shapes = [(nl, d), (nl, HEAD), (nl, HEAD), (nl, nh, EXT), (1,)]
    small_parts = [jnp.stack([sm[i] for sm in small]) for i in range(4)] + [loss_tile[0, :1]]
    rows = -(-sum(int(np.prod(sh)) for sh in small_shapes) // 1024) * 8
    tot = _sum_devices(_gather_small(_pack(small_parts, rows)))
    g_ng, g_qg, g_kg, g_ext, loss = _unpack(tot, small_shapes)
    g_rel_full = jnp.einsum("lhu,ur->lhr", g_ext, onehot, precision=lax.Precision.HIGHEST)
    g_rel = lax.dynamic_slice_in_dim(g_rel_full, bx * nrel, nrel, axis=2)

    res_wi, res_wo = (), ()
    for l in range(nl):
        res_wi = _adamw_layer(l, w_in, g_wi[l], m_w_in, v_w_in, res_wi, "adamw_w_in")
        res_wo = _adamw_layer(l, w_out, g_wo[l], m_w_out, v_w_out, res_wo, "adamw_w_out")
    g_wi, d_wi, nm_wi, nv_wi = res_wi
    g_wo, d_wo, nm_wo, nv_wo = res_wo
    sm_shapes = [(nl, d), (nl, HEAD), (nl, HEAD), (nl, nh, nrel)]
    sm_rows = -(-sum(int(np.prod(sh)) for sh in sm_shapes) // 1024) * 8
    pw, pg, pm, pv = [_pack(group, sm_rows) for group in (
        (norm_g, q_norm_g, k_norm_g, rel_bias), (g_ng, g_qg, g_kg, g_rel),
        (m_norm_g, m_q_norm_g, m_k_norm_g, m_rel_bias), (v_norm_g, v_q_norm_g, v_k_norm_g, v_rel_bias))]
    d_sm, nm_sm, nv_sm = [_unpack(a[0], sm_shapes)
                          for a in _adamw_layer(0, pw[None], pg, pm[None], pv[None], (), "adamw_small")[1:]]

    return (loss[0], grad_x, g_ng, g_wi, g_qg, g_kg, g_rel, g_wo,
            d_sm[0], d_wi, d_sm[1], d_sm[2], d_sm[3], d_wo,
            nm_sm[0], nm_wi, nm_sm[1], nm_sm[2], nm_sm[3], nm_wo,
            nv_sm[0], nv_wi, nv_sm[1], nv_sm[2], nv_sm[3], nv_wo)
```

```python
from typing import Callable, NamedTuple

import jax
import jax.numpy as jnp
import numpy as np
from jax import lax
from jax.experimental import pallas as pl
from jax.experimental.pallas import tpu as pltpu

F32 = jnp.float32
BF16 = jnp.bfloat16

HEAD = 128
CHUNK = 64
LEFT_CHUNKS = 8
REL_CLIP = 256
N_REL = REL_CLIP + CHUNK
NORM_EPS = 1e-6
NEG_BIG = -1e30
TQ = 256
ROWS = 32
PAD = LEFT_CHUNKS * CHUNK
WIN = PAD + TQ
EXT = 1024
SCALE = HEAD ** -0.5

ADAM_LR = 0.001
ADAM_B1 = 0.9
ADAM_B2 = 0.999
ADAM_EPS = 1e-08
ADAM_WD = 0.01
ADAM_STEP = 10

ANY = pl.BlockSpec(memory_space=pl.ANY)
MESH = pl.DeviceIdType.MESH


def _params(sem=None, vmem_mb=None):
    kw = {}
    if sem is not None:
        kw["dimension_semantics"] = sem
    if vmem_mb is not None:
        kw["vmem_limit_bytes"] = vmem_mb << 20
    return pltpu.CompilerParams(**kw)


class _Comm(NamedTuple):
    ins: tuple
    outs: tuple
    aliases: dict
    n_sems: int
    start: Callable
    finish: Callable


def _call(body, *, name, grid, in_specs, out_specs, out_shape, args, scratch=(), sem=None, vmem_mb=None, comm=None):
    if comm is None:
        out = pl.pallas_call(body, name=name, grid=grid, in_specs=in_specs, out_specs=out_specs, out_shape=out_shape,
                             scratch_shapes=list(scratch), compiler_params=_params(sem, vmem_mb))(*args)
        return out, ()
    n_in, n_out, n_ci, n_co = len(in_specs), len(out_shape), len(comm.ins), len(comm.outs)

    def hosted(*refs):
        ins, cins = refs[:n_in], refs[n_in:n_in + n_ci]
        outs, couts = refs[n_in + n_ci:n_in + n_ci + n_out], refs[n_in + n_ci + n_out:n_in + n_ci + n_out + n_co]
        rest = refs[n_in + n_ci + n_out + n_co:]
        send_sems, recv_sems = rest[-2:]
        first, last = None, None
        for ax, size in enumerate(grid):
            at = pl.program_id(ax)
            first = (at == 0) if first is None else first & (at == 0)
            last = (at == size - 1) if last is None else last & (at == size - 1)

        @pl.when(first)
        def _():
            comm.start(cins, couts, send_sems, recv_sems)

        body(*ins, *outs, *rest[:-2])

        @pl.when(last)
        def _():
            comm.finish(cins, couts, send_sems, recv_sems)

    out = pl.pallas_call(
        hosted, name=name, grid=grid, in_specs=list(in_specs) + [ANY] * n_ci, out_specs=list(out_specs) + [ANY] * n_co,
        out_shape=list(out_shape) + list(comm.outs),
        input_output_aliases={n_in + k: n_out + v for k, v in comm.aliases.items()},
        scratch_shapes=list(scratch) + [pltpu.SemaphoreType.DMA((comm.n_sems,)), pltpu.SemaphoreType.DMA((comm.n_sems,))],
        compiler_params=_params(("arbitrary",) * len(grid), vmem_mb))(*args, *comm.ins)
    return out[:n_out], out[n_out:]


def _dot(a, b):
    return jnp.dot(a, b, preferred_element_type=F32)


def _dot_nt(a, b):
    return lax.dot_general(a, b, (((1,), (1,)), ((), ())), preferred_element_type=F32)


def _dot_tn(a, b):
    return lax.dot_general(a, b, (((0,), (0,)), ((), ())), preferred_element_type=F32)


def _split_dot(x, m):
    hi = x.astype(BF16)
    lo = (x - hi.astype(F32)).astype(BF16)
    return _dot(hi, m) + _dot(lo, m)


def _silu_parts(g):
    sg = 1.0 / (1.0 + jnp.exp(-g))
    return g * sg, sg * (1.0 + g * (1.0 - sg))


def _idx(*vals):
    return jnp.stack([jnp.asarray(v, jnp.int32) for v in vals])


def _cast_block(w, l, blk, name):
    _, r, c = w.shape
    tr = min(r, 512)

    def body(b_ref, w_ref, o_ref):
        o_ref[...] = w_ref[...].astype(BF16)

    spec = pltpu.PrefetchScalarGridSpec(
        num_scalar_prefetch=1, grid=(r // tr,),
        in_specs=[pl.BlockSpec((None, tr, c), lambda i, b: (l, i, 0))],
        out_specs=pl.BlockSpec((None, tr, c), lambda i, b: (b[0], i, 0)))
    return pl.pallas_call(body, name=name, grid_spec=spec, out_shape=jax.ShapeDtypeStruct((4, r, c), BF16),
                          compiler_params=_params(("parallel",)))(_idx(blk), w)


def _add_sibling(p, theirs, core):
    nblk, r, c = p.shape
    hr = r // 2
    tr = min(hr, 256)
    per = hr // tr

    def body(c_ref, p_ref, t_ref, o_ref):
        o_ref[...] = (p_ref[...].astype(F32) + t_ref[...].astype(F32)).astype(BF16)

    blk = pl.BlockSpec((None, tr, c), lambda j, i, cr: (j, i, 0))
    spec = pltpu.PrefetchScalarGridSpec(
        num_scalar_prefetch=1, grid=(nblk, per),
        in_specs=[pl.BlockSpec((None, tr, c), lambda j, i, cr: (j, cr[0] * per + i, 0)), blk], out_specs=blk)
    return pl.pallas_call(body, name="add_sibling", grid_spec=spec, out_shape=jax.ShapeDtypeStruct((nblk, hr, c), BF16),
                          compiler_params=_params(("parallel", "parallel")))(_idx(core), p, theirs)


def _add_chips(sums, got, blk, core):
    _, hr, c = sums.shape
    tr = min(hr, 256)

    def body(i_ref, s_ref, g0_ref, g1_ref, g2_ref, o_ref):
        o_ref[...] = ((s_ref[...].astype(F32) + g0_ref[...].astype(F32))
                      + g1_ref[...].astype(F32)) + g2_ref[...].astype(F32)

    at = lambda j: pl.BlockSpec((None, tr, c), lambda i, ir: (j, i, 0))
    spec = pltpu.PrefetchScalarGridSpec(
        num_scalar_prefetch=1, grid=(hr // tr,),
        in_specs=[pl.BlockSpec((None, tr, c), lambda i, ir: (ir[0], i, 0)), at(0), at(1), at(2)],
        out_specs=pl.BlockSpec((None, tr, c), lambda i, ir: (ir[1], i, 0)))
    return pl.pallas_call(body, name="add_chips", grid_spec=spec, out_shape=jax.ShapeDtypeStruct((2, hr, c), F32),
                          compiler_params=_params(("parallel",)))(_idx(blk, core), sums, got, got, got)


def _adamw_layer(l, w, g, m, v, prev, name):
    nl, r, c = w.shape
    tr = min(r, 256)
    c1 = 1.0 / (1.0 - ADAM_B1 ** ADAM_STEP)
    c2 = 1.0 / (1.0 - ADAM_B2 ** ADAM_STEP)

    def body(w_ref, g_ref, m_ref, v_ref, *rest):
        go_ref, d_ref, nm_ref, nv_ref = rest[-4:]
        gg = g_ref[...]
        nm = ADAM_B1 * m_ref[...] + (1.0 - ADAM_B1) * gg
        nv = ADAM_B2 * v_ref[...] + (1.0 - ADAM_B2) * (gg * gg)
        upd = (nm * c1) / (jnp.sqrt(nv * c2) + ADAM_EPS) + ADAM_WD * w_ref[...]
        go_ref[...] = gg
        d_ref[...] = -ADAM_LR * upd
        nm_ref[...] = nm
        nv_ref[...] = nv

    lay = pl.BlockSpec((None, tr, c), lambda i: (l, i, 0))
    shp = jax.ShapeDtypeStruct((nl, r, c), F32)
    return pl.pallas_call(
        body, name=name, grid=(r // tr,),
        in_specs=[lay, pl.BlockSpec((tr, c), lambda i: (i, 0)), lay, lay] + [ANY] * len(prev),
        out_specs=[lay] * 4, out_shape=[shp] * 4, input_output_aliases={4 + k: k for k in range(len(prev))},
        compiler_params=_params(("parallel",), 40))(w, g, m, v, *prev)


def _rmsnorm_fwd(x, g):
    s, d = x.shape
    tm = min(s, 256)

    def body(x_ref, g_ref, h_ref):
        xv = x_ref[...]
        r = lax.rsqrt(jnp.mean(xv * xv, axis=1, keepdims=True) + NORM_EPS)
        h_ref[...] = (xv * r * g_ref[...]).astype(BF16)

    return pl.pallas_call(
        body, name="rmsnorm_fwd", grid=(s // tm,),
        in_specs=[pl.BlockSpec((tm, d), lambda i: (i, 0)), pl.BlockSpec((1, d), lambda i: (0, 0))],
        out_specs=pl.BlockSpec((tm, d), lambda i: (i, 0)),
        out_shape=jax.ShapeDtypeStruct((s, d), BF16),
        compiler_params=_params(("parallel",)))(x, g)


def _in_proj(h, w, comm=None):
    s, d = h.shape
    nblk, _, nb = w.shape
    tm, tn = min(s, 512), min(nb, 1024)
    per = nb // tn

    def body(h_ref, w_ref, o_ref, wt_ref):
        o_ref[...] = _dot(h_ref[...], w_ref[...])

        @pl.when(pl.program_id(1) == 0)
        def _():
            for c in range(tn // HEAD):
                wt_ref[pl.ds(c * HEAD, HEAD), :] = w_ref[:, c * HEAD:(c + 1) * HEAD].astype(F32).T.astype(BF16)

    return _call(
        body, name="in_proj", grid=(nblk * per, s // tm),
        in_specs=[pl.BlockSpec((tm, d), lambda n, m: (m, 0)),
                  pl.BlockSpec((None, d, tn), lambda n, m: (n // per, 0, n % per))],
        out_specs=[pl.BlockSpec((tm, tn), lambda n, m: (m, n)), pl.BlockSpec((tn, d), lambda n, m: (n, 0))],
        out_shape=[jax.ShapeDtypeStruct((s, nblk * nb), F32), jax.ShapeDtypeStruct((nblk * nb, d), BF16)],
        sem=("parallel", "arbitrary"), vmem_mb=56, args=(h, w), comm=comm)


def _heads_per_step(nh):
    return 2 if nh % 2 == 0 else 1


def _head(hh):
    return slice(hh * HEAD, (hh + 1) * HEAD)


def _tri(op):
    r = lax.broadcasted_iota(jnp.int32, (TQ, TQ), 0)
    c = lax.broadcasted_iota(jnp.int32, (TQ, TQ), 1)
    return op(r, c)


def _staggered(groups, hp):
    for hh in range(hp):
        for fn in groups[0]:
            fn(hh)
    for group in groups[1:]:
        for hh in range(hp):
            for fn in group:
                fn(hh)


def _transpose_tiles(src_ref, dst_ref, hp, nq):
    for hh in range(hp):
        for t in range(nq):
            dst_ref[hh, t] = src_ref[pl.ds(t * TQ, TQ), _head(hh)].T.astype(BF16)


def _chunk_causal(r):
    row = lax.broadcasted_iota(jnp.int32, (ROWS, TQ), 0) + r * ROWS
    return row > lax.broadcasted_iota(jnp.int32, (ROWS, TQ), 1)


def _sb_logs(qk, causal):
    z = qk * SCALE
    l1p = jnp.log(1.0 + jnp.exp(-jnp.abs(z)))
    ls = jnp.minimum(-z, 0.0) - l1p
    if causal is not None:
        ls = jnp.where(causal, ls, 0.0)
    return ls, jnp.minimum(z, 0.0) - l1p


def _attn_a_fwd(proj, nh, comm=None):
    s = proj.shape[0]
    nq = s // TQ
    hp = _heads_per_step(nh)
    ng = nh // hp

    def body(q_ref, k_ref, v_ref, o_ref, lt_ref, acc_ref, qb_ref, m_ref, car_ref, sum_ref,
             z_ref, lsig_ref, aft_ref, hi_ref, lo_ref, w_ref, kt_ref):
        i = pl.program_id(1)

        @pl.when(i == 0)
        def _():
            _transpose_tiles(k_ref, kt_ref, hp, nq)

        qb_ref[...] = q_ref[...].astype(BF16)
        m_ref[...] = _tri(lambda r, c: r > c).astype(BF16)
        acc_ref[...] = jnp.zeros_like(acc_ref)
        car_ref[...] = jnp.zeros_like(car_ref)

        def tile(j, diag):
            off = pl.multiple_of(j * TQ, TQ)

            def scores(hh):
                z_ref[hh] = _dot(qb_ref[:, _head(hh)], kt_ref[hh, j])

            def logs(hh):
                for r in range(TQ // ROWS):
                    sl = pl.ds(r * ROWS, ROWS)
                    ls, lsig = _sb_logs(z_ref[hh, sl, :], _chunk_causal(r) if diag else None)
                    lsig_ref[hh, sl, :] = lsig
                    hi = ls.astype(BF16)
                    hi_ref[hh, sl, :] = hi
                    lo_ref[hh, sl, :] = (ls - hi.astype(F32)).astype(BF16)
                    sum_ref[hh, sl, :] = jnp.sum(ls, axis=1, keepdims=True)

            def after(hh):
                aft_ref[hh] = _dot(hi_ref[hh], m_ref[...]) + _dot(lo_ref[hh], m_ref[...])

            def weights(hh):
                for r in range(TQ // ROWS):
                    sl = pl.ds(r * ROWS, ROWS)
                    w = jnp.exp(lsig_ref[hh, sl, :] + aft_ref[hh, sl, :] + car_ref[hh, sl, :])
                    if diag:
                        w = jnp.where(_chunk_causal(r), w, 0.0)
                    w_ref[hh, sl, :] = w.astype(BF16)
                    car_ref[hh, sl, :] += sum_ref[hh, sl, :]

            def values(hh):
                vb = v_ref[pl.ds(off, TQ), _head(hh)].astype(BF16)
                acc_ref[:, _head(hh)] += _dot(w_ref[hh], vb)

            _staggered([(scores,), (logs, after), (weights, values)], hp)

        tile(i, True)

        def step(t, carry):
            tile(i - 1 - t, False)
            return carry

        lax.fori_loop(0, i, step, 0)
        o_ref[...] = acc_ref[...]
        lt_ref[...] = car_ref[...]

    wd = hp * HEAD
    sq = lambda dt: pltpu.VMEM((hp, TQ, TQ), dt)
    return _call(
        body, name="attn_a_fwd", grid=(ng, nq),
        in_specs=[pl.BlockSpec((TQ, wd), lambda h, i: (i, h)),
                  pl.BlockSpec((s, wd), lambda h, i: (0, ng + h)),
                  pl.BlockSpec((s, wd), lambda h, i: (0, 2 * ng + h))],
        out_specs=[pl.BlockSpec((TQ, wd), lambda h, i: (i, h)),
                   pl.BlockSpec((hp, TQ, 1), lambda h, i: (h, i, 0))],
        out_shape=[jax.ShapeDtypeStruct((s, nh * HEAD), F32), jax.ShapeDtypeStruct((nh, s, 1), F32)],
        scratch=[pltpu.VMEM((TQ, wd), F32), pltpu.VMEM((TQ, wd), BF16), pltpu.VMEM((TQ, TQ), BF16),
                 pltpu.VMEM((hp, TQ, 1), F32), pltpu.VMEM((hp, TQ, 1), F32),
                 sq(F32), sq(F32), sq(F32), sq(BF16), sq(BF16), sq(BF16),
                 pltpu.VMEM((hp, nq, HEAD, TQ), BF16)],
        sem=("parallel", "arbitrary"), args=(proj, proj, proj), comm=comm)


def _band_valid(i):
    cl = lax.broadcasted_iota(jnp.int32, (TQ, WIN), 0) // CHUNK
    kl = lax.broadcasted_iota(jnp.int32, (TQ, WIN), 1) // CHUNK
    first = LEFT_CHUNKS - (TQ // CHUNK) * i
    return (kl >= cl) & (kl <= cl + LEFT_CHUNKS) & (kl >= first)


def _build_bias(e_ref, bias_ref):
    e8 = jnp.broadcast_to(e_ref[...], (8, EXT))
    row = lax.broadcasted_iota(jnp.int32, (8, EXT), 0)
    t8 = jnp.zeros((8, EXT), F32)
    for b in range(8):
        t8 = jnp.where(row == b, pltpu.roll(e8, b, 1) if b else e8, t8)
    for a in range(TQ // 8):
        sl = pltpu.roll(t8, 8 * a, 1) if a else t8
        bias_ref[pl.ds(8 * a, 8), :] = sl[:, :WIN]


def _reduce_bias_grad(db_ref):
    acc = jnp.zeros((8, EXT), F32)
    for a in range(TQ // 8):
        sl = db_ref[pl.ds(8 * a, 8), :]
        acc = acc + (pltpu.roll(sl, EXT - 8 * a, 1) if a else sl)
    row = lax.broadcasted_iota(jnp.int32, (8, EXT), 0)
    tot = jnp.zeros((8, EXT), F32)
    for b in range(8):
        tot = tot + jnp.where(row == b, pltpu.roll(acc, EXT - b, 1) if b else acc, 0.0)
    return jnp.sum(tot, axis=0, keepdims=True)


def _band_fill(k_ref, v_ref, kg_ref, kn_pad, v_pad, s):
    k = k_ref[...]
    rk = lax.rsqrt(jnp.mean(k * k, axis=1, keepdims=True) + NORM_EPS)
    kn_pad[pl.ds(0, PAD), :] = jnp.zeros((PAD, HEAD), BF16)
    kn_pad[pl.ds(PAD, s), :] = (k * rk * kg_ref[...]).astype(BF16)
    v_pad[pl.ds(0, PAD), :] = jnp.zeros((PAD, HEAD), BF16)
    v_pad[pl.ds(PAD, s), :] = v_ref[...].astype(BF16)


def _band_probs(q_ref, qg_ref, kn_pad, bias_ref, i):
    q = q_ref[...]
    rq = lax.rsqrt(jnp.mean(q * q, axis=1, keepdims=True) + NORM_EPS)
    qhat = q * rq
    qn = (qhat * qg_ref[...]).astype(BF16)
    off = pl.multiple_of(i * TQ, TQ)
    kw = kn_pad[pl.ds(off, WIN), :]
    sc = _dot_nt(qn, kw) * SCALE + bias_ref[...]
    sc = jnp.where(_band_valid(i), sc, NEG_BIG)
    p = jnp.exp(sc - jnp.max(sc, axis=1, keepdims=True))
    pn = p / jnp.sum(p, axis=1, keepdims=True)
    return rq, qhat, qn, kw, off, pn


def _attn_b_fwd(proj, qg, kg, ext, nh, comm=None):
    s = proj.shape[0]
    nq = s // TQ

    def body(q_ref, k_ref, v_ref, qg_ref, kg_ref, e_ref, o_ref, kn_pad, v_pad, bias_ref):
        i = pl.program_id(1)

        @pl.when(i == 0)
        def _():
            _band_fill(k_ref, v_ref, kg_ref, kn_pad, v_pad, s)
            _build_bias(e_ref, bias_ref)

        _, _, _, _, off, pn = _band_probs(q_ref, qg_ref, kn_pad, bias_ref, i)
        o_ref[...] = _dot(pn.astype(BF16), v_pad[pl.ds(off, WIN), :])

    vec = pl.BlockSpec((1, HEAD), lambda h, i: (0, 0))
    return _call(
        body, name="attn_b_fwd", grid=(nh, nq),
        in_specs=[pl.BlockSpec((TQ, HEAD), lambda h, i: (i, 4 * nh + h)),
                  pl.BlockSpec((s, HEAD), lambda h, i: (0, 5 * nh + h)),
                  pl.BlockSpec((s, HEAD), lambda h, i: (0, 6 * nh + h)),
                  vec, vec,
                  pl.BlockSpec((None, 1, EXT), lambda h, i: (h, 0, 0))],
        out_specs=[pl.BlockSpec((TQ, HEAD), lambda h, i: (i, h))],
        out_shape=[jax.ShapeDtypeStruct((s, nh * HEAD), F32)],
        scratch=[pltpu.VMEM((s + PAD, HEAD), BF16), pltpu.VMEM((s + PAD, HEAD), BF16), pltpu.VMEM((TQ, WIN), F32)],
        sem=("parallel", "arbitrary"), args=(proj, proj, proj, qg, kg, ext), comm=comm)


def _out_proj(x, ya, yb, proj, w):
    s, d = x.shape
    ds_ = ya.shape[1]
    tm = min(s, 256)

    def body(x_ref, ya_ref, yb_ref, ga_ref, gb_ref, w_ref, o_ref, mix_ref):
        ma = (ya_ref[...] * _silu_parts(ga_ref[...])[0]).astype(BF16)
        mb = (yb_ref[...] * _silu_parts(gb_ref[...])[0]).astype(BF16)
        mix_ref[:, :ds_] = ma
        mix_ref[:, ds_:] = mb
        o_ref[...] = x_ref[...] + _dot(ma, w_ref[pl.ds(0, ds_), :]) + _dot(mb, w_ref[pl.ds(ds_, ds_), :])

    row = lambda width: pl.BlockSpec((tm, width), lambda i: (i, 0))
    return pl.pallas_call(
        body, name="out_proj", grid=(s // tm,),
        in_specs=[row(d), row(ds_), row(ds_),
                  pl.BlockSpec((tm, ds_), lambda i: (i, 3)), pl.BlockSpec((tm, ds_), lambda i: (i, 7)),
                  pl.BlockSpec((2 * ds_, d), lambda i: (0, 0))],
        out_specs=[row(d), row(2 * ds_)],
        out_shape=[jax.ShapeDtypeStruct((s, d), F32), jax.ShapeDtypeStruct((s, 2 * ds_), BF16)],
        compiler_params=_params(("parallel",), 48))(x, ya, yb, proj, proj, w)


def _loss_head(y, tgt):
    s, d = y.shape
    tm = min(s, 256)

    def body(y_ref, t_ref, dy_ref, l_ref):
        @pl.when(pl.program_id(0) == 0)
        def _():
            l_ref[...] = jnp.zeros_like(l_ref)

        err = y_ref[...] - t_ref[...]
        dy_ref[...] = err * (1.0 / d)
        l_ref[...] += 0.5 * jnp.sum(jnp.mean(err * err, axis=1, keepdims=True), axis=0, keepdims=True)

    row = pl.BlockSpec((tm, d), lambda i: (i, 0))
    return pl.pallas_call(
        body, name="loss_head", grid=(s // tm,), in_specs=[row, row],
        out_specs=[row, pl.BlockSpec((8, 128), lambda i: (0, 0))],
        out_shape=[jax.ShapeDtypeStruct((s, d), F32), jax.ShapeDtypeStruct((8, 128), F32)],
        compiler_params=_params(("arbitrary",)))(y, tgt)


def _out_proj_bwd(dxo, ya, yb, proj, w):
    s, d = dxo.shape
    ds_ = ya.shape[1]
    tm = min(s, 256)

    def body(dx_ref, ya_ref, yb_ref, ga_ref, gb_ref, w_ref, dya_ref, dyb_ref, dga_ref, dgb_ref):
        dxb = dx_ref[...].astype(BF16)
        for y_ref, g_ref, lo, dy_ref, dg_ref in ((ya_ref, ga_ref, 0, dya_ref, dga_ref),
                                                 (yb_ref, gb_ref, ds_, dyb_ref, dgb_ref)):
            dmix = _dot_nt(dxb, w_ref[pl.ds(lo, ds_), :])
            act, dact = _silu_parts(g_ref[...])
            dy_ref[...] = dmix * act
            dg_ref[...] = (dmix * y_ref[...] * dact).astype(BF16)

    row = lambda width: pl.BlockSpec((tm, width), lambda i: (i, 0))
    return pl.pallas_call(
        body, name="out_proj_bwd", grid=(s // tm,),
        in_specs=[row(d), row(ds_), row(ds_),
                  pl.BlockSpec((tm, ds_), lambda i: (i, 3)), pl.BlockSpec((tm, ds_), lambda i: (i, 7)),
                  pl.BlockSpec((2 * ds_, d), lambda i: (0, 0))],
        out_specs=[row(ds_)] * 4,
        out_shape=[jax.ShapeDtypeStruct((s, ds_), F32)] * 2 + [jax.ShapeDtypeStruct((s, ds_), BF16)] * 2,
        compiler_params=_params(("parallel",), 48))(dxo, ya, yb, proj, proj, w)


def _wgrad(a, b, nblk, col_blocks, name):
    s, m = a.shape
    n = b.shape[1]
    if col_blocks:
        tr = min(m, 512)
        nb = n // nblk
        tn = min(nb, 2048)
        per = nb // tn
        out_shape = (nblk, m, nb)
        out_spec = pl.BlockSpec((None, tr, tn), lambda j, r: (j // per, r, j % per))
    else:
        tn = min(n, 1024)
        tr = m // nblk
        out_shape = (nblk, tr, n)
        out_spec = pl.BlockSpec((None, tr, tn), lambda j, r: (r, 0, j))

    def body(a_ref, b_ref, o_ref):
        o_ref[...] = _dot_tn(a_ref[...].astype(BF16), b_ref[...].astype(BF16)).astype(BF16)

    return pl.pallas_call(
        body, name=name, grid=(n // tn, m // tr),
        in_specs=[pl.BlockSpec((s, tr), lambda j, r: (0, r)), pl.BlockSpec((s, tn), lambda j, r: (0, j))],
        out_specs=out_spec, out_shape=jax.ShapeDtypeStruct(out_shape, BF16),
        compiler_params=_params(("parallel", "parallel"), 48))(a, b)


def _attn_a_bwd(proj, lt, dya, nh, comm=None):
    s = proj.shape[0]
    nq = s // TQ
    hp = _heads_per_step(nh)
    ng = nh // hp

    def body(q_ref, k_ref, v_ref, lt_ref, do_ref, dq_ref, dk_ref, dv_ref, dq_acc, dk_acc, dv_acc,
             qb_ref, dob_ref, upto_ref, before_ref, cls_ref, cg_ref, sls_ref, sg_ref,
             z_ref, dw_ref, lsig_ref, pre_ref, g_ref, hi_ref, lo_ref, wb_ref, kt_ref, vt_ref, qt_ref, dot_ref):
        i = pl.program_id(1)

        @pl.when(i == 0)
        def _():
            dk_acc[...] = jnp.zeros_like(dk_acc)
            dv_acc[...] = jnp.zeros_like(dv_acc)
            _transpose_tiles(k_ref, kt_ref, hp, nq)
            _transpose_tiles(v_ref, vt_ref, hp, nq)

        dq_acc[...] = jnp.zeros_like(dq_acc)
        qb_ref[...] = q_ref[...].astype(BF16)
        dob_ref[...] = do_ref[...].astype(BF16)
        for hh in range(hp):
            qt_ref[hh] = q_ref[:, _head(hh)].T.astype(BF16)
            dot_ref[hh] = do_ref[:, _head(hh)].T.astype(BF16)
        upto_ref[...] = _tri(lambda r, c: r <= c).astype(BF16)
        before_ref[...] = _tri(lambda r, c: r < c).astype(BF16)
        cls_ref[...] = jnp.zeros_like(cls_ref)
        cg_ref[...] = jnp.zeros_like(cg_ref)

        def tile(j, diag):
            off = pl.multiple_of(j * TQ, TQ)

            def scores(hh):
                z_ref[hh] = _dot(qb_ref[:, _head(hh)], kt_ref[hh, j])
                dw_ref[hh] = _dot(dob_ref[:, _head(hh)], vt_ref[hh, j])

            def logs(hh):
                for r in range(TQ // ROWS):
                    sl = pl.ds(r * ROWS, ROWS)
                    ls, lsig = _sb_logs(z_ref[hh, sl, :], _chunk_causal(r) if diag else None)
                    lsig_ref[hh, sl, :] = lsig
                    hi = ls.astype(BF16)
                    hi_ref[hh, sl, :] = hi
                    lo_ref[hh, sl, :] = (ls - hi.astype(F32)).astype(BF16)
                    sls_ref[hh, sl, :] = jnp.sum(ls, axis=1, keepdims=True)

            def upto(hh):
                pre_ref[hh] = _dot(hi_ref[hh], upto_ref[...]) + _dot(lo_ref[hh], upto_ref[...])

            def weights(hh):
                for r in range(TQ // ROWS):
                    sl = pl.ds(r * ROWS, ROWS)
                    w = jnp.exp(lsig_ref[hh, sl, :] + (lt_ref[hh, sl, :] - (cls_ref[hh, sl, :] + pre_ref[hh, sl, :])))
                    if diag:
                        w = jnp.where(_chunk_causal(r), w, 0.0)
                    wb_ref[hh, sl, :] = w.astype(BF16)
                    g = w * dw_ref[hh, sl, :]
                    g_ref[hh, sl, :] = g
                    hi_ref[hh, sl, :] = g.astype(BF16)
                    sg_ref[hh, sl, :] = jnp.sum(g, axis=1, keepdims=True)

            def earlier(hh):
                dw_ref[hh] = _dot(hi_ref[hh], before_ref[...])

            def logit_grads(hh):
                for r in range(TQ // ROWS):
                    sl = pl.ds(r * ROWS, ROWS)
                    z = z_ref[hh, sl, :] * SCALE
                    e = jnp.exp(-jnp.abs(z))
                    rinv = 1.0 / (1.0 + e)
                    beta = jnp.where(z >= 0.0, rinv, e * rinv)
                    dz = g_ref[hh, sl, :] * (1.0 - beta) - beta * (cg_ref[hh, sl, :] + dw_ref[hh, sl, :])
                    if diag:
                        dz = jnp.where(_chunk_causal(r), dz, 0.0)
                    lo_ref[hh, sl, :] = (dz * SCALE).astype(BF16)
                    cls_ref[hh, sl, :] += sls_ref[hh, sl, :]
                    cg_ref[hh, sl, :] += sg_ref[hh, sl, :]

            def grads(hh):
                dq_acc[:, _head(hh)] += _dot(lo_ref[hh], k_ref[pl.ds(off, TQ), _head(hh)].astype(BF16))
                dk_acc[hh, j] += _dot(qt_ref[hh], lo_ref[hh])
                dv_acc[hh, j] += _dot(dot_ref[hh], wb_ref[hh])

            _staggered([(scores,), (logs, upto), (weights, earlier), (logit_grads, grads)], hp)

        def step(j, carry):
            tile(j, False)
            return carry

        lax.fori_loop(0, i, step, 0)
        tile(i, True)
        dq_ref[...] = dq_acc[...].astype(BF16)

        @pl.when(i == nq - 1)
        def _():
            for hh in range(hp):
                for t in range(nq):
                    dk_ref[pl.ds(t * TQ, TQ), _head(hh)] = dk_acc[hh, t].T.astype(BF16)
                    dv_ref[pl.ds(t * TQ, TQ), _head(hh)] = dv_acc[hh, t].T.astype(BF16)

    wd = hp * HEAD
    sq = lambda dt: pltpu.VMEM((hp, TQ, TQ), dt)
    tiles = lambda dt: pltpu.VMEM((hp, nq, HEAD, TQ), dt)
    blk = pl.BlockSpec((TQ, wd), lambda h, i: (i, h))
    col = pl.BlockSpec((s, wd), lambda h, i: (0, h))
    shp = jax.ShapeDtypeStruct((s, nh * HEAD), BF16)
    return _call(
        body, name="attn_a_bwd", grid=(ng, nq),
        in_specs=[blk,
                  pl.BlockSpec((s, wd), lambda h, i: (0, ng + h)),
                  pl.BlockSpec((s, wd), lambda h, i: (0, 2 * ng + h)),
                  pl.BlockSpec((hp, TQ, 1), lambda h, i: (h, i, 0)), blk],
        out_specs=[blk, col, col], out_shape=[shp] * 3,
        scratch=[pltpu.VMEM((TQ, wd), F32), tiles(F32), tiles(F32),
                 pltpu.VMEM((TQ, wd), BF16), pltpu.VMEM((TQ, wd), BF16),
                 pltpu.VMEM((TQ, TQ), BF16), pltpu.VMEM((TQ, TQ), BF16)]
        + [pltpu.VMEM((hp, TQ, 1), F32)] * 4 + [sq(F32)] * 5 + [sq(BF16)] * 3
        + [tiles(BF16), tiles(BF16), pltpu.VMEM((hp, HEAD, TQ), BF16), pltpu.VMEM((hp, HEAD, TQ), BF16)],
        sem=("parallel", "arbitrary"), args=(proj, proj, proj, lt, dya), comm=comm)


def _attn_b_bwd(proj, dyb, qg, kg, ext, nh, comm=None):
    s = proj.shape[0]
    nq = s // TQ

    def body(q_ref, k_ref, v_ref, do_ref, qg_ref, kg_ref, e_ref,
             dq_ref, dk_ref, dv_ref, dqg_ref, dkg_ref, de_ref,
             kn_pad, v_pad, bias_ref, db_acc, dkn_acc, dv_acc):
        i = pl.program_id(1)

        @pl.when(i == 0)
        def _():
            _band_fill(k_ref, v_ref, kg_ref, kn_pad, v_pad, s)
            _build_bias(e_ref, bias_ref)
            db_acc[...] = jnp.zeros_like(db_acc)
            dkn_acc[...] = jnp.zeros_like(dkn_acc)
            dv_acc[...] = jnp.zeros_like(dv_acc)
            dqg_ref[...] = jnp.zeros_like(dqg_ref)

        rq, qhat, qn, kw, off, pn = _band_probs(q_ref, qg_ref, kn_pad, bias_ref, i)
        dob = do_ref[...].astype(BF16)
        dp = _dot_nt(dob, v_pad[pl.ds(off, WIN), :])
        dsc = pn * (dp - jnp.sum(pn * dp, axis=1, keepdims=True))
        db_acc[:, :WIN] += dsc
        dsb = (dsc * SCALE).astype(BF16)
        dqn = _dot(dsb, kw)
        dkn_acc[pl.ds(off, WIN), :] += _dot_tn(dsb, qn)
        dv_acc[pl.ds(off, WIN), :] += _dot_tn(pn.astype(BF16), dob)
        dqh = dqn * qg_ref[...]
        dq_ref[...] = (rq * (dqh - qhat * jnp.mean(dqh * qhat, axis=1, keepdims=True))).astype(BF16)
        dqg_ref[...] += jnp.sum(dqn * qhat, axis=0, keepdims=True)

        @pl.when(i == nq - 1)
        def _():
            k = k_ref[...]
            rk = lax.rsqrt(jnp.mean(k * k, axis=1, keepdims=True) + NORM_EPS)
            khat = k * rk
            dkn = dkn_acc[pl.ds(PAD, s), :]
            dkh = dkn * kg_ref[...]
            dk_ref[...] = (rk * (dkh - khat * jnp.mean(dkh * khat, axis=1, keepdims=True))).astype(BF16)
            dkg_ref[...] = jnp.sum(dkn * khat, axis=0, keepdims=True)
            dv_ref[...] = dv_acc[pl.ds(PAD, s), :].astype(BF16)
            de_ref[...] = _reduce_bias_grad(db_acc)

    blk = pl.BlockSpec((TQ, HEAD), lambda h, i: (i, h))
    col = pl.BlockSpec((s, HEAD), lambda h, i: (0, h))
    vec = pl.BlockSpec((1, HEAD), lambda h, i: (0, 0))
    hvec = pl.BlockSpec((None, 1, HEAD), lambda h, i: (h, 0, 0))
    hext = pl.BlockSpec((None, 1, EXT), lambda h, i: (h, 0, 0))
    shp = jax.ShapeDtypeStruct((s, nh * HEAD), BF16)
    return _call(
        body, name="attn_b_bwd", grid=(nh, nq),
        in_specs=[pl.BlockSpec((TQ, HEAD), lambda h, i: (i, 4 * nh + h)),
                  pl.BlockSpec((s, HEAD), lambda h, i: (0, 5 * nh + h)),
                  pl.BlockSpec((s, HEAD), lambda h, i: (0, 6 * nh + h)),
                  blk, vec, vec, hext],
        out_specs=[blk, col, col, hvec, hvec, hext],
        out_shape=[shp] * 3 + [jax.ShapeDtypeStruct((nh, 1, HEAD), F32)] * 2
        + [jax.ShapeDtypeStruct((nh, 1, EXT), F32)],
        scratch=[pltpu.VMEM((s + PAD, HEAD), BF16), pltpu.VMEM((s + PAD, HEAD), BF16),
                 pltpu.VMEM((TQ, WIN), F32), pltpu.VMEM((TQ, EXT), F32),
                 pltpu.VMEM((s + PAD, HEAD), F32), pltpu.VMEM((s + PAD, HEAD), F32)],
        sem=("parallel", "arbitrary"), args=(proj, proj, proj, dyb, qg, kg, ext), comm=comm)


def _in_proj_bwd(dproj, wt, x, dxo, g, comm=None):
    s, d = x.shape
    tm, tk = min(s, 256), min(wt.shape[0], 1024)
    nk = wt.shape[0] // tk

    def body(dp_ref, w_ref, x_ref, dxo_ref, g_ref, dx_ref, dg_ref, acc):
        k, m = pl.program_id(0), pl.program_id(1)
        rows = pl.ds(pl.multiple_of(m * tm, tm), tm)
        part = _dot(dp_ref[...], w_ref[...])

        @pl.when(k == 0)
        def _():
            acc[rows, :] = part

        @pl.when(k > 0)
        def _():
            acc[rows, :] += part

        @pl.when((k == 0) & (m == 0))
        def _():
            dg_ref[...] = jnp.zeros_like(dg_ref)

        @pl.when(k == nk - 1)
        def _():
            xv = x_ref[...]
            r = lax.rsqrt(jnp.mean(xv * xv, axis=1, keepdims=True) + NORM_EPS)
            xhat = xv * r
            dh = acc[rows, :]
            dxh = dh * g_ref[...]
            dx_ref[...] = dxo_ref[...] + r * (dxh - xhat * jnp.mean(dxh * xhat, axis=1, keepdims=True))
            dg_ref[...] += jnp.sum(dh * xhat, axis=0, keepdims=True)

    row = pl.BlockSpec((tm, d), lambda k, m: (jnp.where(k == nk - 1, m, 0), 0))
    return _call(
        body, name="in_proj_bwd", grid=(nk, s // tm),
        in_specs=[pl.BlockSpec((tm, tk), lambda k, m: (m, k)),
                  pl.BlockSpec((tk, d), lambda k, m: (k, 0)),
                  row, row, pl.BlockSpec((1, d), lambda k, m: (0, 0))],
        out_specs=[row, pl.BlockSpec((8, d), lambda k, m: (0, 0))],
        out_shape=[jax.ShapeDtypeStruct((s, d), F32), jax.ShapeDtypeStruct((8, d), F32)],
        scratch=[pltpu.VMEM((s, d), F32)], sem=("arbitrary", "arbitrary"), vmem_mb=56,
        args=(dproj, wt, x, dxo, g), comm=comm)


def _place():
    x, y, c = lax.axis_index("x"), lax.axis_index("y"), lax.axis_index("c")
    chips = [(1 - x, y), (x, 1 - y), (1 - x, 1 - y)]
    return x, y, c, chips


def _comm_call(body, name, ins, out_shape, n_remote, n_local, aliases=None):
    return pl.pallas_call(
        body, name=name, in_specs=[ANY] * len(ins), out_specs=[ANY] * len(out_shape), out_shape=out_shape,
        input_output_aliases=aliases or {},
        scratch_shapes=[pltpu.SemaphoreType.DMA((n_remote,)), pltpu.SemaphoreType.DMA((n_remote,)),
                        pltpu.SemaphoreType.DMA((n_local,))])(*ins)


def _rcopy(src, dst, send_sems, recv_sems, k, dev):
    return pltpu.make_async_remote_copy(src_ref=src, dst_ref=dst, send_sem=send_sems.at[k], recv_sem=recv_sems.at[k],
                                        device_id=dev, device_id_type=MESH)


def _run_comm(comm, name):
    n_ci, n_co = len(comm.ins), len(comm.outs)

    def body(*refs):
        cins, couts = refs[:n_ci], refs[n_ci:n_ci + n_co]
        send_sems, recv_sems = refs[n_ci + n_co:]
        comm.start(cins, couts, send_sems, recv_sems)
        comm.finish(cins, couts, send_sems, recv_sems)

    return pl.pallas_call(
        body, name=name, in_specs=[ANY] * n_ci, out_specs=[ANY] * n_co, out_shape=list(comm.outs),
        input_output_aliases=dict(comm.aliases),
        scratch_shapes=[pltpu.SemaphoreType.DMA((comm.n_sems,)), pltpu.SemaphoreType.DMA((comm.n_sems,))])(*comm.ins)


def _gather_comm(fulls, rbp=None, stage="all", chips_at=(0, 1, 2)):
    n = len(fulls)
    n_ici = 3 * n
    base = n_ici if stage == "all" else 0
    per_tensor = chips_at if isinstance(chips_at, list) else [chips_at] * n

    def half(full, blk, core):
        hr = full.shape[1] // 2
        return full.at[blk].at[pl.ds(core * hr, hr)]

    def ici(couts, send_sems, recv_sems, x, y, c, chips):
        b = 2 * x + y
        return [_rcopy(half(full, b, c), half(full, b, c), send_sems, recv_sems, 3 * t + j, (*chip, c))
                for t, full in enumerate(couts[:n]) for j, chip in enumerate(chips) if j in per_tensor[t]]

    def landed(couts, send_sems, recv_sems, x, y, c, chips, core, first, at=None):
        return [_rcopy(half(full, 2 * chip[0] + chip[1], core), half(full, 2 * chip[0] + chip[1], core),
                       send_sems, recv_sems, first + 3 * t + j, (x, y, 1 - c))
                for t, full in enumerate(couts[:n]) for j, chip in enumerate(chips) if at is None or j in at[t]]

    def small(cins, couts, send_sems, recv_sems, x, y, c, chips):
        b = 2 * x + y
        return ([_rcopy(cins[n], couts[n].at[b], send_sems, recv_sems, 2 * n_ici + j, (*chip, c))
                 for j, chip in enumerate(chips)],
                pltpu.make_async_copy(cins[n], couts[n].at[b], send_sems.at[2 * n_ici + 3]))

    def start(cins, couts, send_sems, recv_sems):
        x, y, c, chips = _place()
        if stage == "sibling":
            for cp in landed(couts, send_sems, recv_sems, x, y, c, chips, c, base):
                cp.start()
            return
        for cp in ici(couts, send_sems, recv_sems, x, y, c, chips):
            cp.start()
        if rbp is not None:
            remote, local = small(cins, couts, send_sems, recv_sems, x, y, c, chips)
            for cp in remote:
                cp.start()
            local.start()

    def finish(cins, couts, send_sems, recv_sems):
        x, y, c, chips = _place()
        passed = landed(couts, send_sems, recv_sems, x, y, c, chips, c, base)
        if stage != "sibling":
            for k, cp in enumerate(landed(couts, send_sems, recv_sems, x, y, c, chips, c, 0, per_tensor)):
                cp.wait_recv()
                if stage == "all":
                    passed[k].start()
            for cp in ici(couts, send_sems, recv_sems, x, y, c, chips):
                cp.wait_send()
        if stage != "chips":
            for cp in landed(couts, send_sems, recv_sems, x, y, c, chips, 1 - c, base):
                cp.wait_recv()
            for cp in passed:
                cp.wait_send()
        if rbp is not None:
            remote, local = small(cins, couts, send_sems, recv_sems, x, y, c, chips)
            for j, chip in enumerate(chips):
                got = couts[n].at[2 * chip[0] + chip[1]]
                _rcopy(got, got, send_sems, recv_sems, 2 * n_ici + j, (x, y, c)).wait_recv()
            for cp in remote:
                cp.wait_send()
            local.wait()

    outs = [jax.ShapeDtypeStruct(f.shape, f.dtype) for f in fulls]
    ins = list(fulls)
    if rbp is not None:
        ins.append(rbp)
        outs.append(jax.ShapeDtypeStruct((4,) + rbp.shape, F32))
    return _Comm(tuple(ins), tuple(outs), {t: t for t in range(n)}, 2 * n_ici + 4, start, finish)


def _chips_comm(sums):
    n = len(sums)

    def copies(cins, couts, send_sems, recv_sems):
        x, y, c, chips = _place()
        return [_rcopy(cins[t].at[2 * chip[0] + chip[1]], couts[t].at[j], send_sems, recv_sems, 3 * t + j, (*chip, c))
                for t in range(n) for j, chip in enumerate(chips)]

    def start(*refs):
        for cp in copies(*refs):
            cp.start()

    def finish(*refs):
        for cp in copies(*refs):
            cp.wait()

    outs = tuple(jax.ShapeDtypeStruct((3,) + p.shape[1:], p.dtype) for p in sums)
    return _Comm(tuple(sums), outs, {}, 3 * n, start, finish)


def _pair_comm(ins, outs, aliases, copies):
    def start(*refs):
        for cp in copies(*refs):
            cp.start()

    def finish(*refs):
        for cp in copies(*refs):
            cp.wait()

    return _Comm(tuple(ins), tuple(outs), aliases, len(ins), start, finish)


def _sibling_comm(parts):
    def copies(cins, couts, send_sems, recv_sems):
        x, y, c, _ = _place()
        return [_rcopy(p.at[:, pl.ds((1 - c) * (p.shape[1] // 2), p.shape[1] // 2), :], couts[t],
                       send_sems, recv_sems, t, (x, y, 1 - c)) for t, p in enumerate(cins)]

    half = [jax.ShapeDtypeStruct((p.shape[0], p.shape[1] // 2, p.shape[2]), p.dtype) for p in parts]
    return _pair_comm(parts, half, {}, copies)


def _halves_comm(tots):
    def copies(cins, couts, send_sems, recv_sems):
        x, y, c, _ = _place()
        return [_rcopy(g.at[c], g.at[c], send_sems, recv_sems, t, (x, y, 1 - c)) for t, g in enumerate(couts)]

    return _pair_comm(tots, [jax.ShapeDtypeStruct(t.shape, t.dtype) for t in tots],
                      {t: t for t in range(len(tots))}, copies)


def _gather_small(packed):
    def body(p_ref, all_ref, send_sems, recv_sems, loc_sems):
        x, y, c, _ = _place()
        me = 4 * x + 2 * y + c
        local = pltpu.make_async_copy(p_ref, all_ref.at[me], loc_sems.at[0])
        local.start()
        sent = []
        for k in range(1, 8):
            px, py, pc = x ^ (k >> 2), y ^ ((k >> 1) & 1), c ^ (k & 1)
            cp = _rcopy(p_ref, all_ref.at[me], send_sems, recv_sems, k - 1, (px, py, pc))
            cp.start()
            sent.append(cp)
        for k in range(1, 8):
            px, py, pc = x ^ (k >> 2), y ^ ((k >> 1) & 1), c ^ (k & 1)
            got = all_ref.at[4 * px + 2 * py + pc]
            _rcopy(got, got, send_sems, recv_sems, k - 1, (x, y, c)).wait_recv()
        for cp in sent:
            cp.wait_send()
        local.wait()

    return _comm_call(body, "gather_small", [packed], [jax.ShapeDtypeStruct((8,) + packed.shape, F32)], 7, 1)[0]


def _sum_devices(allp):
    n, r, c = allp.shape

    def body(a_ref, o_ref):
        acc = a_ref[0]
        for k in range(1, n):
            acc = acc + a_ref[k]
        o_ref[...] = acc

    return pl.pallas_call(body, name="sum_devices", out_shape=jax.ShapeDtypeStruct((r, c), F32))(allp)


def _ext_index():
    u = np.arange(EXT)
    dist = np.where(u < WIN, PAD - u, PAD + EXT - u)
    return np.clip(dist, -(CHUNK - 1), REL_CLIP) + (CHUNK - 1)


def _pack(parts, rows):
    flat = jnp.concatenate([p.reshape(-1) for p in parts])
    return jnp.pad(flat, (0, rows * 128 - flat.shape[0])).reshape(rows, 128)


def _unpack(packed, shapes):
    flat, out, at = packed.reshape(-1), [], 0
    for shp in shapes:
        size = int(np.prod(shp))
        out.append(flat[at:at + size].reshape(shp))
        at += size
    return out


def kernel(x, norm_g, w_in, q_norm_g, k_norm_g, rel_bias, w_out, loss_target, m_norm_g, m_w_in, m_q_norm_g, m_k_norm_g, m_rel_bias, m_w_out, v_norm_g, v_w_in, v_q_norm_g, v_k_norm_g, v_rel_bias, v_w_out):
    nl, d, nb = w_in.shape
    s = x.shape[1]
    ds_ = d // 2
    nh = ds_ // HEAD
    rb = w_out.shape[1]
    nrel = rel_bias.shape[2]
    bx = lax.axis_index("x") * 2 + lax.axis_index("y")

    rb_rows = -(-(nl * nh * nrel) // 1024) * 8
    cx = lax.axis_index("c")
    wi_full = [_cast_block(w_in, l, bx, "cast_w_in") for l in range(nl)]
    wo_full = [_cast_block(w_out, l, bx, "cast_w_out") for l in range(nl)]
    wi_full[0], wo_full[0], rel_all = _run_comm(
        _gather_comm([wi_full[0], wo_full[0]], _pack([rel_bias], rb_rows)), "gather_first")
    rel_full = jnp.concatenate(
        [rel_all[j].reshape(-1)[:nl * nh * nrel].reshape(nl, nh, nrel) for j in range(4)], axis=2)
    ext_idx = _ext_index()
    onehot = jnp.asarray(ext_idx[:, None] == np.arange(N_REL)[None, :], F32)
    ext = jnp.einsum("lhr,ur->lhu", rel_full, onehot, precision=lax.Precision.HIGHEST).reshape(nl, nh, 1, EXT)

    xs, hs, projs, yas, lts, ybs, mixes, wi_t = [], [], [], [], [], [], [], []
    xc = x[0]
    for l in range(nl):
        h = _rmsnorm_fwd(xc, norm_g[l:l + 1])
        nxt = l + 1 < nl
        (proj, wt), got = _in_proj(h, wi_full[l], _gather_comm(
            [wo_full[l + 1], wi_full[l + 1]], stage="chips", chips_at=[(0, 1, 2), (2,)]) if nxt else None)
        wi_t.append(wt)
        if nxt:
            wo_full[l + 1], wi_full[l + 1] = got
        (ya, lt), got = _attn_a_fwd(proj, nh, _gather_comm(
            [wi_full[l + 1]], stage="chips", chips_at=(0, 1)) if nxt else None)
        if nxt:
            wi_full[l + 1] = got[0]
        (yb,), got = _attn_b_fwd(proj, q_norm_g[l:l + 1], k_norm_g[l:l + 1], ext[l], nh,
                                 _gather_comm([wi_full[l + 1], wo_full[l + 1]], stage="sibling") if nxt else None)
        if nxt:
            wi_full[l + 1], wo_full[l + 1] = got
        xs.append(xc)
        xc, mix = _out_proj(xc, ya, yb, proj, wo_full[l].reshape(4 * rb, d))
        hs.append(h); projs.append(proj); yas.append(ya); lts.append(lt); ybs.append(yb); mixes.append(mix)
    dx, loss_tile = _loss_head(xc, loss_target[0])

    small, g_wi, g_wo = [None] * nl, [None] * nl, [None] * nl
    pending = None

    def keep(lay, shared):
        g_wi[lay], g_wo[lay] = shared[0].reshape(d, nb), shared[1].reshape(rb, d)

    for l in reversed(range(nl)):
        wo = wo_full[l].reshape(4 * rb, d)
        p_wo = _wgrad(mixes[l], dx, 4, False, "wgrad_out")
        dya, dyb, dga, dgb = _out_proj_bwd(dx, yas[l], ybs[l], projs[l], wo)
        (dqa, dka, dva), got = _attn_a_bwd(projs[l], lts[l], dya, nh, _chips_comm(pending[1]) if pending else None)
        tots = [_add_chips(pending[1][t], got[t], bx, cx) for t in range(2)] if pending else None
        (dqb, dkb, dvb, dqg, dkg, dext), shared = _attn_b_bwd(
            projs[l], dyb, q_norm_g[l:l + 1], k_norm_g[l:l + 1], ext[l], nh, _halves_comm(tots) if pending else None)
        if pending:
            keep(pending[0], shared)
        dproj = jnp.concatenate([dqa, dka, dva, dga, dqb, dkb, dvb, dgb], axis=1)
        parts = [_wgrad(hs[l], dproj, 4, True, "wgrad_in"), p_wo]
        if l > 0:
            (dx, dng), theirs = _in_proj_bwd(dproj, wi_t[l],xs[l], dx, norm_g[l:l + 1], _sibling_comm(parts))
            pending = (l, [_add_sibling(parts[t], theirs[t], cx) for t in range(2)])
        else:
            theirs = _run_comm(_sibling_comm(parts), "reduce_sibling")
            sums = [_add_sibling(parts[t], theirs[t], cx) for t in range(2)]
            (dx, dng), got = _in_proj_bwd(dproj, wi_t[l],xs[l], dx, norm_g[l:l + 1], _chips_comm(sums))
            keep(0, _run_comm(_halves_comm([_add_chips(sums[t], got[t], bx, cx) for t in range(2)]), "share_halves"))
        small[l] = (dng[0], jnp.sum(dqg, axis=0).reshape(-1), jnp.sum(dkg, axis=0).reshape(-1), dext.reshape(nh, EXT))
    grad_x = dx[None]

    small_shapes = [(nl, d), (nl, HEAD), (nl, HEAD), (nl, nh, EXT), (1,)]
    small_parts = [jnp.stack([sm[i] for sm in small]) for i in range(4)] + [loss_tile[0, :1]]
    rows = -(-sum(int(np.prod(sh)) for sh in small_shapes) // 1024) * 8
    tot = _sum_devices(_gather_small(_pack(small_parts, rows)))
    g_ng, g_qg, g_kg, g_ext, loss = _unpack(tot, small_shapes)
    g_rel_full = jnp.einsum("lhu,ur->lhr", g_ext, onehot, precision=lax.Precision.HIGHEST)
    g_rel = lax.dynamic_slice_in_dim(g_rel_full, bx * nrel, nrel, axis=2)

    res_wi, res_wo = (), ()
    for l in range(nl):
        res_wi = _adamw_layer(l, w_in, g_wi[l], m_w_in, v_w_in, res_wi, "adamw_w_in")
        res_wo = _adamw_layer(l, w_out, g_wo[l], m_w_out, v_w_out, res_wo, "adamw_w_out")
    g_wi, d_wi, nm_wi, nv_wi = res_wi
    g_wo, d_wo, nm_wo, nv_wo = res_wo
    sm_shapes = [(nl, d), (nl, HEAD), (nl, HEAD), (nl, nh, nrel)]
    sm_rows = -(-sum(int(np.prod(sh)) for sh in sm_shapes) // 1024) * 8
    pw, pg, pm, pv = [_pack(group, sm_rows) for group in (
        (norm_g, q_norm_g, k_norm_g, rel_bias), (g_ng, g_qg, g_kg, g_rel),
        (m_norm_g, m_q_norm_g, m_k_norm_g, m_rel_bias), (v_norm_g, v_q_norm_g, v_k_norm_g, v_rel_bias))]
    d_sm, nm_sm, nv_sm = [_unpack(a[0], sm_shapes)
                          for a in _adamw_layer(0, pw[None], pg, pm[None], pv[None], (), "adamw_small")[1:]]

    return (loss[0], grad_x, g_ng, g_wi, g_qg, g_kg, g_rel, g_wo,
            d_sm[0], d_wi, d_sm[1], d_sm[2], d_sm[3], d_wo,
            nm_sm[0], nm_wi, nm_sm[1], nm_sm[2], nm_sm[3], nm_wo,
            nv_sm[0], nv_wi, nv_sm[1], nv_sm[2], nv_sm[3], nv_wo)
```

```python
from typing import Callable, NamedTuple

import jax
import jax.numpy as jnp
import numpy as np
from jax import lax
from jax.experimental import pallas as pl
from jax.experimental.pallas import tpu as pltpu

F32 = jnp.float32
BF16 = jnp.bfloat16

HEAD = 128
CHUNK = 64
LEFT_CHUNKS = 8
REL_CLIP = 256
N_REL = REL_CLIP + CHUNK
NORM_EPS = 1e-6
NEG_BIG = -1e30
TQ = 256
ROWS = 32
PAD = LEFT_CHUNKS * CHUNK
WIN = PAD + TQ
EXT = 1024
SCALE = HEAD ** -0.5

ADAM_LR = 0.001
ADAM_B1 = 0.9
ADAM_B2 = 0.999
ADAM_EPS = 1e-08
ADAM_WD = 0.01
ADAM_STEP = 10

ANY = pl.BlockSpec(memory_space=pl.ANY)
MESH = pl.DeviceIdType.MESH


def _params(sem=None, vmem_mb=None):
    kw = {}
    if sem is not None:
        kw["dimension_semantics"] = sem
    if vmem_mb is not None:
        kw["vmem_limit_bytes"] = vmem_mb << 20
    return pltpu.CompilerParams(**kw)


class _Comm(NamedTuple):
    ins: tuple
    outs: tuple
    aliases: dict
    n_sems: int
    start: Callable
    finish: Callable


def _call(body, *, name, grid, in_specs, out_specs, out_shape, args, scratch=(), sem=None, vmem_mb=None, comm=None):
    if comm is None:
        out = pl.pallas_call(body, name=name, grid=grid, in_specs=in_specs, out_specs=out_specs, out_shape=out_shape,
                             scratch_shapes=list(scratch), compiler_params=_params(sem, vmem_mb))(*args)
        return out, ()
    n_in, n_out, n_ci, n_co = len(in_specs), len(out_shape), len(comm.ins), len(comm.outs)

    def hosted(*refs):
        ins, cins = refs[:n_in], refs[n_in:n_in + n_ci]
        outs, couts = refs[n_in + n_ci:n_in + n_ci + n_out], refs[n_in + n_ci + n_out:n_in + n_ci + n_out + n_co]
        rest = refs[n_in + n_ci + n_out + n_co:]
        send_sems, recv_sems = rest[-2:]
        first, last = None, None
        for ax, size in enumerate(grid):
            at = pl.program_id(ax)
            first = (at == 0) if first is None else first & (at == 0)
            last = (at == size - 1) if last is None else last & (at == size - 1)

        @pl.when(first)
        def _():
            comm.start(cins, couts, send_sems, recv_sems)

        body(*ins, *outs, *rest[:-2])

        @pl.when(last)
        def _():
            comm.finish(cins, couts, send_sems, recv_sems)

    out = pl.pallas_call(
        hosted, name=name, grid=grid, in_specs=list(in_specs) + [ANY] * n_ci, out_specs=list(out_specs) + [ANY] * n_co,
        out_shape=list(out_shape) + list(comm.outs),
        input_output_aliases={n_in + k: n_out + v for k, v in comm.aliases.items()},
        scratch_shapes=list(scratch) + [pltpu.SemaphoreType.DMA((comm.n_sems,)), pltpu.SemaphoreType.DMA((comm.n_sems,))],
        compiler_params=_params(("arbitrary",) * len(grid), vmem_mb))(*args, *comm.ins)
    return out[:n_out], out[n_out:]


def _dot(a, b):
    return jnp.dot(a, b, preferred_element_type=F32)


def _dot_nt(a, b):
    return lax.dot_general(a, b, (((1,), (1,)), ((), ())), preferred_element_type=F32)


def _dot_tn(a, b):
    return lax.dot_general(a, b, (((0,), (0,)), ((), ())), preferred_element_type=F32)


def _split_dot(x, m):
    hi = x.astype(BF16)
    lo = (x - hi.astype(F32)).astype(BF16)
    return _dot(hi, m) + _dot(lo, m)


def _silu_parts(g):
    sg = 1.0 / (1.0 + jnp.exp(-g))
    return g * sg, sg * (1.0 + g * (1.0 - sg))


def _idx(*vals):
    return jnp.stack([jnp.asarray(v, jnp.int32) for v in vals])


def _cast_block(w, l, blk, name):
    _, r, c = w.shape
    tr = min(r, 512)

    def body(b_ref, w_ref, o_ref):
        o_ref[...] = w_ref[...].astype(BF16)

    spec = pltpu.PrefetchScalarGridSpec(
        num_scalar_prefetch=1, grid=(r // tr,),
        in_specs=[pl.BlockSpec((None, tr, c), lambda i, b: (l, i, 0))],
        out_specs=pl.BlockSpec((None, tr, c), lambda i, b: (b[0], i, 0)))
    return pl.pallas_call(body, name=name, grid_spec=spec, out_shape=jax.ShapeDtypeStruct((4, r, c), BF16),
                          compiler_params=_params(("parallel",)))(_idx(blk), w)


def _add_sibling(p, theirs, core):
    nblk, r, c = p.shape
    hr = r // 2
    tr = min(hr, 256)
    per = hr // tr

    def body(c_ref, p_ref, t_ref, o_ref):
        o_ref[...] = (p_ref[...].astype(F32) + t_ref[...].astype(F32)).astype(BF16)

    blk = pl.BlockSpec((None, tr, c), lambda j, i, cr: (j, i, 0))
    spec = pltpu.PrefetchScalarGridSpec(
        num_scalar_prefetch=1, grid=(nblk, per),
        in_specs=[pl.BlockSpec((None, tr, c), lambda j, i, cr: (j, cr[0] * per + i, 0)), blk], out_specs=blk)
    return pl.pallas_call(body, name="add_sibling", grid_spec=spec, out_shape=jax.ShapeDtypeStruct((nblk, hr, c), BF16),
                          compiler_params=_params(("parallel", "parallel")))(_idx(core), p, theirs)


def _add_chips(sums, got, blk, core):
    _, hr, c = sums.shape
    tr = min(hr, 256)

    def body(i_ref, s_ref, g0_ref, g1_ref, g2_ref, o_ref):
        o_ref[...] = ((s_ref[...].astype(F32) + g0_ref[...].astype(F32))
                      + g1_ref[...].astype(F32)) + g2_ref[...].astype(F32)

    at = lambda j: pl.BlockSpec((None, tr, c), lambda i, ir: (j, i, 0))
    spec = pltpu.PrefetchScalarGridSpec(
        num_scalar_prefetch=1, grid=(hr // tr,),
        in_specs=[pl.BlockSpec((None, tr, c), lambda i, ir: (ir[0], i, 0)), at(0), at(1), at(2)],
        out_specs=pl.BlockSpec((None, tr, c), lambda i, ir: (ir[1], i, 0)))
    return pl.pallas_call(body, name="add_chips", grid_spec=spec, out_shape=jax.ShapeDtypeStruct((2, hr, c), F32),
                          compiler_params=_params(("parallel",)))(_idx(blk, core), sums, got, got, got)


def _adamw_layer(l, w, g, m, v, prev, name):
    nl, r, c = w.shape
    tr = min(r, 256)
    c1 = 1.0 / (1.0 - ADAM_B1 ** ADAM_STEP)
    c2 = 1.0 / (1.0 - ADAM_B2 ** ADAM_STEP)

    def body(w_ref, g_ref, m_ref, v_ref, *rest):
        go_ref, d_ref, nm_ref, nv_ref = rest[-4:]
        gg = g_ref[...]
        nm = ADAM_B1 * m_ref[...] + (1.0 - ADAM_B1) * gg
        nv = ADAM_B2 * v_ref[...] + (1.0 - ADAM_B2) * (gg * gg)
        upd = (nm * c1) / (jnp.sqrt(nv * c2) + ADAM_EPS) + ADAM_WD * w_ref[...]
        go_ref[...] = gg
        d_ref[...] = -ADAM_LR * upd
        nm_ref[...] = nm
        nv_ref[...] = nv

    lay = pl.BlockSpec((None, tr, c), lambda i: (l, i, 0))
    shp = jax.ShapeDtypeStruct((nl, r, c), F32)
    return pl.pallas_call(
        body, name=name, grid=(r // tr,),
        in_specs=[lay, pl.BlockSpec((tr, c), lambda i: (i, 0)), lay, lay] + [ANY] * len(prev),
        out_specs=[lay] * 4, out_shape=[shp] * 4, input_output_aliases={4 + k: k for k in range(len(prev))},
        compiler_params=_params(("parallel",), 40))(w, g, m, v, *prev)


def _rmsnorm_fwd(x, g):
    s, d = x.shape
    tm = min(s, 256)

    def body(x_ref, g_ref, h_ref):
        xv = x_ref[...]
        r = lax.rsqrt(jnp.mean(xv * xv, axis=1, keepdims=True) + NORM_EPS)
        h_ref[...] = (xv * r * g_ref[...]).astype(BF16)

    return pl.pallas_call(
        body, name="rmsnorm_fwd", grid=(s // tm,),
        in_specs=[pl.BlockSpec((tm, d), lambda i: (i, 0)), pl.BlockSpec((1, d), lambda i: (0, 0))],
        out_specs=pl.BlockSpec((tm, d), lambda i: (i, 0)),
        out_shape=jax.ShapeDtypeStruct((s, d), BF16),
        compiler_params=_params(("parallel",)))(x, g)


def _in_proj(h, w, comm=None):
    s, d = h.shape
    nblk, _, nb = w.shape
    tm, tn = min(s, 1024), min(nb, 1024)
    per = nb // tn

    def body(h_ref, w_ref, o_ref, wt_ref):
        o_ref[...] = _dot(h_ref[...], w_ref[...])

        @pl.when(pl.program_id(1) == 0)
        def _():
            for c in range(tn // HEAD):
                wt_ref[pl.ds(c * HEAD, HEAD), :] = w_ref[:, c * HEAD:(c + 1) * HEAD].astype(F32).T.astype(BF16)

    return _call(
        body, name="in_proj", grid=(nblk * per, s // tm),
        in_specs=[pl.BlockSpec((tm, d), lambda n, m: (m, 0)),
                  pl.BlockSpec((None, d, tn), lambda n, m: (n // per, 0, n % per))],
        out_specs=[pl.BlockSpec((tm, tn), lambda n, m: (m, n)), pl.BlockSpec((tn, d), lambda n, m: (n, 0))],
        out_shape=[jax.ShapeDtypeStruct((s, nblk * nb), F32), jax.ShapeDtypeStruct((nblk * nb, d), BF16)],
        sem=("parallel", "arbitrary"), vmem_mb=56, args=(h, w), comm=comm)


def _heads_per_step(nh):
    return 2 if nh % 2 == 0 else 1


def _head(hh):
    return slice(hh * HEAD, (hh + 1) * HEAD)


def _tri(op):
    r = lax.broadcasted_iota(jnp.int32, (TQ, TQ), 0)
    c = lax.broadcasted_iota(jnp.int32, (TQ, TQ), 1)
    return op(r, c)


def _staggered(groups, hp):
    for hh in range(hp):
        for fn in groups[0]:
            fn(hh)
    for group in groups[1:]:
        for hh in range(hp):
            for fn in group:
                fn(hh)


def _transpose_tiles(src_ref, dst_ref, hp, nq):
    for hh in range(hp):
        for t in range(nq):
            dst_ref[hh, t] = src_ref[pl.ds(t * TQ, TQ), _head(hh)].T.astype(BF16)


def _chunk_causal(r):
    row = lax.broadcasted_iota(jnp.int32, (ROWS, TQ), 0) + r * ROWS
    return row > lax.broadcasted_iota(jnp.int32, (ROWS, TQ), 1)


def _sb_logs(qk, causal):
    z = qk * SCALE
    l1p = jnp.log(1.0 + jnp.exp(-jnp.abs(z)))
    ls = jnp.minimum(-z, 0.0) - l1p
    if causal is not None:
        ls = jnp.where(causal, ls, 0.0)
    return ls, jnp.minimum(z, 0.0) - l1p


def _attn_a_fwd(proj, nh, comm=None):
    s = proj.shape[0]
    nq = s // TQ
    hp = _heads_per_step(nh)
    ng = nh // hp

    def body(q_ref, k_ref, v_ref, o_ref, lt_ref, acc_ref, qb_ref, m_ref, car_ref, sum_ref,
             z_ref, lsig_ref, aft_ref, hi_ref, lo_ref, w_ref, kt_ref):
        i = pl.program_id(1)

        @pl.when(i == 0)
        def _():
            _transpose_tiles(k_ref, kt_ref, hp, nq)

        qb_ref[...] = q_ref[...].astype(BF16)
        m_ref[...] = _tri(lambda r, c: r > c).astype(BF16)
        acc_ref[...] = jnp.zeros_like(acc_ref)
        car_ref[...] = jnp.zeros_like(car_ref)

        def tile(j, diag):
            off = pl.multiple_of(j * TQ, TQ)

            def scores(hh):
                z_ref[hh] = _dot(qb_ref[:, _head(hh)], kt_ref[hh, j])

            def logs(hh):
                for r in range(TQ // ROWS):
                    sl = pl.ds(r * ROWS, ROWS)
                    ls, lsig = _sb_logs(z_ref[hh, sl, :], _chunk_causal(r) if diag else None)
                    lsig_ref[hh, sl, :] = lsig
                    hi = ls.astype(BF16)
                    hi_ref[hh, sl, :] = hi
                    lo_ref[hh, sl, :] = (ls - hi.astype(F32)).astype(BF16)
                    sum_ref[hh, sl, :] = jnp.sum(ls, axis=1, keepdims=True)

            def after(hh):
                aft_ref[hh] = _dot(hi_ref[hh], m_ref[...]) + _dot(lo_ref[hh], m_ref[...])

            def weights(hh):
                for r in range(TQ // ROWS):
                    sl = pl.ds(r * ROWS, ROWS)
                    w = jnp.exp(lsig_ref[hh, sl, :] + aft_ref[hh, sl, :] + car_ref[hh, sl, :])
                    if diag:
                        w = jnp.where(_chunk_causal(r), w, 0.0)
                    w_ref[hh, sl, :] = w.astype(BF16)
                    car_ref[hh, sl, :] += sum_ref[hh, sl, :]

            def values(hh):
                vb = v_ref[pl.ds(off, TQ), _head(hh)].astype(BF16)
                acc_ref[:, _head(hh)] += _dot(w_ref[hh], vb)

            _staggered([(scores,), (logs, after), (weights, values)], hp)

        tile(i, True)

        def step(t, carry):
            tile(i - 1 - t, False)
            return carry

        lax.fori_loop(0, i, step, 0)
        o_ref[...] = acc_ref[...]
        lt_ref[...] = car_ref[...]

    wd = hp * HEAD
    sq = lambda dt: pltpu.VMEM((hp, TQ, TQ), dt)
    return _call(
        body, name="attn_a_fwd", grid=(ng, nq),
        in_specs=[pl.BlockSpec((TQ, wd), lambda h, i: (i, h)),
                  pl.BlockSpec((s, wd), lambda h, i: (0, ng + h)),
                  pl.BlockSpec((s, wd), lambda h, i: (0, 2 * ng + h))],
        out_specs=[pl.BlockSpec((TQ, wd), lambda h, i: (i, h)),
                   pl.BlockSpec((hp, TQ, 1), lambda h, i: (h, i, 0))],
        out_shape=[jax.ShapeDtypeStruct((s, nh * HEAD), F32), jax.ShapeDtypeStruct((nh, s, 1), F32)],
        scratch=[pltpu.VMEM((TQ, wd), F32), pltpu.VMEM((TQ, wd), BF16), pltpu.VMEM((TQ, TQ), BF16),
                 pltpu.VMEM((hp, TQ, 1), F32), pltpu.VMEM((hp, TQ, 1), F32),
                 sq(F32), sq(F32), sq(F32), sq(BF16), sq(BF16), sq(BF16),
                 pltpu.VMEM((hp, nq, HEAD, TQ), BF16)],
        sem=("parallel", "arbitrary"), args=(proj, proj, proj), comm=comm)


def _band_valid(i):
    cl = lax.broadcasted_iota(jnp.int32, (TQ, WIN), 0) // CHUNK
    kl = lax.broadcasted_iota(jnp.int32, (TQ, WIN), 1) // CHUNK
    first = LEFT_CHUNKS - (TQ // CHUNK) * i
    return (kl >= cl) & (kl <= cl + LEFT_CHUNKS) & (kl >= first)


def _build_bias(e_ref, bias_ref):
    e8 = jnp.broadcast_to(e_ref[...], (8, EXT))
    row = lax.broadcasted_iota(jnp.int32, (8, EXT), 0)
    t8 = jnp.zeros((8, EXT), F32)
    for b in range(8):
        t8 = jnp.where(row == b, pltpu.roll(e8, b, 1) if b else e8, t8)
    for a in range(TQ // 8):
        sl = pltpu.roll(t8, 8 * a, 1) if a else t8
        bias_ref[pl.ds(8 * a, 8), :] = sl[:, :WIN]


def _reduce_bias_grad(db_ref):
    acc = jnp.zeros((8, EXT), F32)
    for a in range(TQ // 8):
        sl = db_ref[pl.ds(8 * a, 8), :]
        acc = acc + (pltpu.roll(sl, EXT - 8 * a, 1) if a else sl)
    row = lax.broadcasted_iota(jnp.int32, (8, EXT), 0)
    tot = jnp.zeros((8, EXT), F32)
    for b in range(8):
        tot = tot + jnp.where(row == b, pltpu.roll(acc, EXT - b, 1) if b else acc, 0.0)
    return jnp.sum(tot, axis=0, keepdims=True)


def _band_fill(k_ref, v_ref, kg_ref, kn_pad, v_pad, s):
    k = k_ref[...]
    rk = lax.rsqrt(jnp.mean(k * k, axis=1, keepdims=True) + NORM_EPS)
    kn_pad[pl.ds(0, PAD), :] = jnp.zeros((PAD, HEAD), BF16)
    kn_pad[pl.ds(PAD, s), :] = (k * rk * kg_ref[...]).astype(BF16)
    v_pad[pl.ds(0, PAD), :] = jnp.zeros((PAD, HEAD), BF16)
    v_pad[pl.ds(PAD, s), :] = v_ref[...].astype(BF16)


def _band_probs(q_ref, qg_ref, kn_pad, bias_ref, i):
    q = q_ref[...]
    rq = lax.rsqrt(jnp.mean(q * q, axis=1, keepdims=True) + NORM_EPS)
    qhat = q * rq
    qn = (qhat * qg_ref[...]).astype(BF16)
    off = pl.multiple_of(i * TQ, TQ)
    kw = kn_pad[pl.ds(off, WIN), :]
    sc = _dot_nt(qn, kw) * SCALE + bias_ref[...]
    sc = jnp.where(_band_valid(i), sc, NEG_BIG)
    p = jnp.exp(sc - jnp.max(sc, axis=1, keepdims=True))
    pn = p / jnp.sum(p, axis=1, keepdims=True)
    return rq, qhat, qn, kw, off, pn


def _attn_b_fwd(proj, qg, kg, ext, nh, comm=None):
    s = proj.shape[0]
    nq = s // TQ

    def body(q_ref, k_ref, v_ref, qg_ref, kg_ref, e_ref, o_ref, kn_pad, v_pad, bias_ref):
        i = pl.program_id(1)

        @pl.when(i == 0)
        def _():
            _band_fill(k_ref, v_ref, kg_ref, kn_pad, v_pad, s)
            _build_bias(e_ref, bias_ref)

        _, _, _, _, off, pn = _band_probs(q_ref, qg_ref, kn_pad, bias_ref, i)
        o_ref[...] = _dot(pn.astype(BF16), v_pad[pl.ds(off, WIN), :])

    vec = pl.BlockSpec((1, HEAD), lambda h, i: (0, 0))
    return _call(
        body, name="attn_b_fwd", grid=(nh, nq),
        in_specs=[pl.BlockSpec((TQ, HEAD), lambda h, i: (i, 4 * nh + h)),
                  pl.BlockSpec((s, HEAD), lambda h, i: (0, 5 * nh + h)),
                  pl.BlockSpec((s, HEAD), lambda h, i: (0, 6 * nh + h)),
                  vec, vec,
                  pl.BlockSpec((None, 1, EXT), lambda h, i: (h, 0, 0))],
        out_specs=[pl.BlockSpec((TQ, HEAD), lambda h, i: (i, h))],
        out_shape=[jax.ShapeDtypeStruct((s, nh * HEAD), F32)],
        scratch=[pltpu.VMEM((s + PAD, HEAD), BF16), pltpu.VMEM((s + PAD, HEAD), BF16), pltpu.VMEM((TQ, WIN), F32)],
        sem=("parallel", "arbitrary"), args=(proj, proj, proj, qg, kg, ext), comm=comm)


def _out_proj(x, ya, yb, proj, w, comm=None):
    s, d = x.shape
    ds_ = ya.shape[1]
    tm = min(s, 256)

    def body(x_ref, ya_ref, yb_ref, ga_ref, gb_ref, w_ref, o_ref, mix_ref):
        ma = (ya_ref[...] * _silu_parts(ga_ref[...])[0]).astype(BF16)
        mb = (yb_ref[...] * _silu_parts(gb_ref[...])[0]).astype(BF16)
        mix_ref[:, :ds_] = ma
        mix_ref[:, ds_:] = mb
        o_ref[...] = x_ref[...] + _dot(ma, w_ref[pl.ds(0, ds_), :]) + _dot(mb, w_ref[pl.ds(ds_, ds_), :])

    row = lambda width: pl.BlockSpec((tm, width), lambda i: (i, 0))
    return _call(
        body, name="out_proj", grid=(s // tm,),
        in_specs=[row(d), row(ds_), row(ds_),
                  pl.BlockSpec((tm, ds_), lambda i: (i, 3)), pl.BlockSpec((tm, ds_), lambda i: (i, 7)),
                  pl.BlockSpec((2 * ds_, d), lambda i: (0, 0))],
        out_specs=[row(d), row(2 * ds_)],
        out_shape=[jax.ShapeDtypeStruct((s, d), F32), jax.ShapeDtypeStruct((s, 2 * ds_), BF16)],
        sem=("parallel",), vmem_mb=48, args=(x, ya, yb, proj, proj, w), comm=comm)


def _loss_head(y, tgt):
    s, d = y.shape
    tm = min(s, 256)

    def body(y_ref, t_ref, dy_ref, l_ref):
        @pl.when(pl.program_id(0) == 0)
        def _():
            l_ref[...] = jnp.zeros_like(l_ref)

        err = y_ref[...] - t_ref[...]
        dy_ref[...] = err * (1.0 / d)
        l_ref[...] += 0.5 * jnp.sum(jnp.mean(err * err, axis=1, keepdims=True), axis=0, keepdims=True)

    row = pl.BlockSpec((tm, d), lambda i: (i, 0))
    return pl.pallas_call(
        body, name="loss_head", grid=(s // tm,), in_specs=[row, row],
        out_specs=[row, pl.BlockSpec((8, 128), lambda i: (0, 0))],
        out_shape=[jax.ShapeDtypeStruct((s, d), F32), jax.ShapeDtypeStruct((8, 128), F32)],
        compiler_params=_params(("arbitrary",)))(y, tgt)


def _out_proj_bwd(dxo, ya, yb, proj, w):
    s, d = dxo.shape
    ds_ = ya.shape[1]
    tm = min(s, 256)

    def body(dx_ref, ya_ref, yb_ref, ga_ref, gb_ref, w_ref, dya_ref, dyb_ref, dga_ref, dgb_ref):
        dxb = dx_ref[...].astype(BF16)
        for y_ref, g_ref, lo, dy_ref, dg_ref in ((ya_ref, ga_ref, 0, dya_ref, dga_ref),
                                                 (yb_ref, gb_ref, ds_, dyb_ref, dgb_ref)):
            dmix = _dot_nt(dxb, w_ref[pl.ds(lo, ds_), :])
            act, dact = _silu_parts(g_ref[...])
            dy_ref[...] = dmix * act
            dg_ref[...] = (dmix * y_ref[...] * dact).astype(BF16)

    row = lambda width: pl.BlockSpec((tm, width), lambda i: (i, 0))
    return pl.pallas_call(
        body, name="out_proj_bwd", grid=(s // tm,),
        in_specs=[row(d), row(ds_), row(ds_),
                  pl.BlockSpec((tm, ds_), lambda i: (i, 3)), pl.BlockSpec((tm, ds_), lambda i: (i, 7)),
                  pl.BlockSpec((2 * ds_, d), lambda i: (0, 0))],
        out_specs=[row(ds_)] * 4,
        out_shape=[jax.ShapeDtypeStruct((s, ds_), F32)] * 2 + [jax.ShapeDtypeStruct((s, ds_), BF16)] * 2,
        compiler_params=_params(("parallel",), 48))(dxo, ya, yb, proj, proj, w)


def _wgrad(a, b, nblk, col_blocks, name):
    s, m = a.shape
    n = b.shape[1]
    if col_blocks:
        tr = min(m, 1024)
        nb = n // nblk
        tn = min(nb, 2048)
        per = nb // tn
        out_shape = (nblk, m, nb)
        out_spec = pl.BlockSpec((None, tr, tn), lambda j, r: (j // per, r, j % per))
    else:
        tn = min(n, 1024)
        tr = m // nblk
        out_shape = (nblk, tr, n)
        out_spec = pl.BlockSpec((None, tr, tn), lambda j, r: (r, 0, j))

    def body(a_ref, b_ref, o_ref):
        o_ref[...] = _dot_tn(a_ref[...].astype(BF16), b_ref[...].astype(BF16)).astype(BF16)

    return pl.pallas_call(
        body, name=name, grid=(n // tn, m // tr),
        in_specs=[pl.BlockSpec((s, tr), lambda j, r: (0, r)), pl.BlockSpec((s, tn), lambda j, r: (0, j))],
        out_specs=out_spec, out_shape=jax.ShapeDtypeStruct(out_shape, BF16),
        compiler_params=_params(("parallel", "parallel"), 48))(a, b)


def _attn_a_bwd(proj, lt, dya, nh, comm=None):
    s = proj.shape[0]
    nq = s // TQ
    hp = _heads_per_step(nh)
    ng = nh // hp

    def body(q_ref, k_ref, v_ref, lt_ref, do_ref, dq_ref, dk_ref, dv_ref, dq_acc, dk_acc, dv_acc,
             qb_ref, dob_ref, upto_ref, before_ref, cls_ref, cg_ref, sls_ref, sg_ref,
             z_ref, dw_ref, lsig_ref, pre_ref, g_ref, hi_ref, lo_ref, wb_ref, kt_ref, vt_ref, qt_ref, dot_ref):
        i = pl.program_id(1)

        @pl.when(i == 0)
        def _():
            dk_acc[...] = jnp.zeros_like(dk_acc)
            dv_acc[...] = jnp.zeros_like(dv_acc)
            _transpose_tiles(k_ref, kt_ref, hp, nq)
            _transpose_tiles(v_ref, vt_ref, hp, nq)

        dq_acc[...] = jnp.zeros_like(dq_acc)
        qb_ref[...] = q_ref[...].astype(BF16)
        dob_ref[...] = do_ref[...].astype(BF16)
        for hh in range(hp):
            qt_ref[hh] = q_ref[:, _head(hh)].T.astype(BF16)
            dot_ref[hh] = do_ref[:, _head(hh)].T.astype(BF16)
        upto_ref[...] = _tri(lambda r, c: r <= c).astype(BF16)
        before_ref[...] = _tri(lambda r, c: r < c).astype(BF16)
        cls_ref[...] = jnp.zeros_like(cls_ref)
        cg_ref[...] = jnp.zeros_like(cg_ref)

        def tile(j, diag):
            off = pl.multiple_of(j * TQ, TQ)

            def scores(hh):
                z_ref[hh] = _dot(qb_ref[:, _head(hh)], kt_ref[hh, j])
                dw_ref[hh] = _dot(dob_ref[:, _head(hh)], vt_ref[hh, j])

            def logs(hh):
                for r in range(TQ // ROWS):
                    sl = pl.ds(r * ROWS, ROWS)
                    ls, lsig = _sb_logs(z_ref[hh, sl, :], _chunk_causal(r) if diag else None)
                    lsig_ref[hh, sl, :] = lsig
                    hi = ls.astype(BF16)
                    hi_ref[hh, sl, :] = hi
                    lo_ref[hh, sl, :] = (ls - hi.astype(F32)).astype(BF16)
                    sls_ref[hh, sl, :] = jnp.sum(ls, axis=1, keepdims=True)

            def upto(hh):
                pre_ref[hh] = _dot(hi_ref[hh], upto_ref[...]) + _dot(lo_ref[hh], upto_ref[...])

            def weights(hh):
                for r in range(TQ // ROWS):
                    sl = pl.ds(r * ROWS, ROWS)
                    w = jnp.exp(lsig_ref[hh, sl, :] + (lt_ref[hh, sl, :] - (cls_ref[hh, sl, :] + pre_ref[hh, sl, :])))
                    if diag:
                        w = jnp.where(_chunk_causal(r), w, 0.0)
                    wb_ref[hh, sl, :] = w.astype(BF16)
                    g = w * dw_ref[hh, sl, :]
                    g_ref[hh, sl, :] = g
                    hi_ref[hh, sl, :] = g.astype(BF16)
                    sg_ref[hh, sl, :] = jnp.sum(g, axis=1, keepdims=True)

            def earlier(hh):
                dw_ref[hh] = _dot(hi_ref[hh], before_ref[...])

            def logit_grads(hh):
                for r in range(TQ // ROWS):
                    sl = pl.ds(r * ROWS, ROWS)
                    z = z_ref[hh, sl, :] * SCALE
                    e = jnp.exp(-jnp.abs(z))
                    rinv = 1.0 / (1.0 + e)
                    beta = jnp.where(z >= 0.0, rinv, e * rinv)
                    dz = g_ref[hh, sl, :] * (1.0 - beta) - beta * (cg_ref[hh, sl, :] + dw_ref[hh, sl, :])
                    if diag:
                        dz = jnp.where(_chunk_causal(r), dz, 0.0)
                    lo_ref[hh, sl, :] = (dz * SCALE).astype(BF16)
                    cls_ref[hh, sl, :] += sls_ref[hh, sl, :]
                    cg_ref[hh, sl, :] += sg_ref[hh, sl, :]

            def grads(hh):
                dq_acc[:, _head(hh)] += _dot(lo_ref[hh], k_ref[pl.ds(off, TQ), _head(hh)].astype(BF16))
                dk_acc[hh, j] += _dot(qt_ref[hh], lo_ref[hh])
                dv_acc[hh, j] += _dot(dot_ref[hh], wb_ref[hh])

            _staggered([(scores,), (logs, upto), (weights, earlier), (logit_grads, grads)], hp)

        def step(j, carry):
            tile(j, False)
            return carry

        lax.fori_loop(0, i, step, 0)
        tile(i, True)
        dq_ref[...] = dq_acc[...].astype(BF16)

        @pl.when(i == nq - 1)
        def _():
            for hh in range(hp):
                for t in range(nq):
                    dk_ref[pl.ds(t * TQ, TQ), _head(hh)] = dk_acc[hh, t].T.astype(BF16)
                    dv_ref[pl.ds(t * TQ, TQ), _head(hh)] = dv_acc[hh, t].T.astype(BF16)

    wd = hp * HEAD
    sq = lambda dt: pltpu.VMEM((hp, TQ, TQ), dt)
    tiles = lambda dt: pltpu.VMEM((hp, nq, HEAD, TQ), dt)
    blk = pl.BlockSpec((TQ, wd), lambda h, i: (i, h))
    col = pl.BlockSpec((s, wd), lambda h, i: (0, h))
    shp = jax.ShapeDtypeStruct((s, nh * HEAD), BF16)
    return _call(
        body, name="attn_a_bwd", grid=(ng, nq),
        in_specs=[blk,
                  pl.BlockSpec((s, wd), lambda h, i: (0, ng + h)),
                  pl.BlockSpec((s, wd), lambda h, i: (0, 2 * ng + h)),
                  pl.BlockSpec((hp, TQ, 1), lambda h, i: (h, i, 0)), blk],
        out_specs=[blk, col, col], out_shape=[shp] * 3,
        scratch=[pltpu.VMEM((TQ, wd), F32), tiles(F32), tiles(F32),
                 pltpu.VMEM((TQ, wd), BF16), pltpu.VMEM((TQ, wd), BF16),
                 pltpu.VMEM((TQ, TQ), BF16), pltpu.VMEM((TQ, TQ), BF16)]
        + [pltpu.VMEM((hp, TQ, 1), F32)] * 4 + [sq(F32)] * 5 + [sq(BF16)] * 3
        + [tiles(BF16), tiles(BF16), pltpu.VMEM((hp, HEAD, TQ), BF16), pltpu.VMEM((hp, HEAD, TQ), BF16)],
        sem=("parallel", "arbitrary"), args=(proj, proj, proj, lt, dya), comm=comm)


def _attn_b_bwd(proj, dyb, qg, kg, ext, nh, comm=None):
    s = proj.shape[0]
    nq = s // TQ

    def body(q_ref, k_ref, v_ref, do_ref, qg_ref, kg_ref, e_ref,
             dq_ref, dk_ref, dv_ref, dqg_ref, dkg_ref, de_ref,
             kn_pad, v_pad, bias_ref, db_acc, dkn_acc, dv_acc):
        i = pl.program_id(1)

        @pl.when(i == 0)
        def _():
            _band_fill(k_ref, v_ref, kg_ref, kn_pad, v_pad, s)
            _build_bias(e_ref, bias_ref)
            db_acc[...] = jnp.zeros_like(db_acc)
            dkn_acc[...] = jnp.zeros_like(dkn_acc)
            dv_acc[...] = jnp.zeros_like(dv_acc)
            dqg_ref[...] = jnp.zeros_like(dqg_ref)

        rq, qhat, qn, kw, off, pn = _band_probs(q_ref, qg_ref, kn_pad, bias_ref, i)
        dob = do_ref[...].astype(BF16)
        dp = _dot_nt(dob, v_pad[pl.ds(off, WIN), :])
        dsc = pn * (dp - jnp.sum(pn * dp, axis=1, keepdims=True))
        db_acc[:, :WIN] += dsc
        dsb = (dsc * SCALE).astype(BF16)
        dqn = _dot(dsb, kw)
        dkn_acc[pl.ds(off, WIN), :] += _dot_tn(dsb, qn)
        dv_acc[pl.ds(off, WIN), :] += _dot_tn(pn.astype(BF16), dob)
        dqh = dqn * qg_ref[...]
        dq_ref[...] = (rq * (dqh - qhat * jnp.mean(dqh * qhat, axis=1, keepdims=True))).astype(BF16)
        dqg_ref[...] += jnp.sum(dqn * qhat, axis=0, keepdims=True)

        @pl.when(i == nq - 1)
        def _():
            k = k_ref[...]
            rk = lax.rsqrt(jnp.mean(k * k, axis=1, keepdims=True) + NORM_EPS)
            khat = k * rk
            dkn = dkn_acc[pl.ds(PAD, s), :]
            dkh = dkn * kg_ref[...]
            dk_ref[...] = (rk * (dkh - khat * jnp.mean(dkh * khat, axis=1, keepdims=True))).astype(BF16)
            dkg_ref[...] = jnp.sum(dkn * khat, axis=0, keepdims=True)
            dv_ref[...] = dv_acc[pl.ds(PAD, s), :].astype(BF16)
            de_ref[...] = _reduce_bias_grad(db_acc)

    blk = pl.BlockSpec((TQ, HEAD), lambda h, i: (i, h))
    col = pl.BlockSpec((s, HEAD), lambda h, i: (0, h))
    vec = pl.BlockSpec((1, HEAD), lambda h, i: (0, 0))
    hvec = pl.BlockSpec((None, 1, HEAD), lambda h, i: (h, 0, 0))
    hext = pl.BlockSpec((None, 1, EXT), lambda h, i: (h, 0, 0))
    shp = jax.ShapeDtypeStruct((s, nh * HEAD), BF16)
    return _call(
        body, name="attn_b_bwd", grid=(nh, nq),
        in_specs=[pl.BlockSpec((TQ, HEAD), lambda h, i: (i, 4 * nh + h)),
                  pl.BlockSpec((s, HEAD), lambda h, i: (0, 5 * nh + h)),
                  pl.BlockSpec((s, HEAD), lambda h, i: (0, 6 * nh + h)),
                  blk, vec, vec, hext],
        out_specs=[blk, col, col, hvec, hvec, hext],
        out_shape=[shp] * 3 + [jax.ShapeDtypeStruct((nh, 1, HEAD), F32)] * 2
        + [jax.ShapeDtypeStruct((nh, 1, EXT), F32)],
        scratch=[pltpu.VMEM((s + PAD, HEAD), BF16), pltpu.VMEM((s + PAD, HEAD), BF16),
                 pltpu.VMEM((TQ, WIN), F32), pltpu.VMEM((TQ, EXT), F32),
                 pltpu.VMEM((s + PAD, HEAD), F32), pltpu.VMEM((s + PAD, HEAD), F32)],
        sem=("parallel", "arbitrary"), args=(proj, proj, proj, dyb, qg, kg, ext), comm=comm)


def _in_proj_bwd(dproj, wt, x, dxo, g, comm=None):
    s, d = x.shape
    tm, tk = min(s, 512), min(wt.shape[0], 1024)
    nk = wt.shape[0] // tk

    def body(dp_ref, w_ref, x_ref, dxo_ref, g_ref, dx_ref, dg_ref, acc):
        m, k = pl.program_id(0), pl.program_id(1)

        @pl.when(k == 0)
        def _():
            acc[...] = jnp.zeros_like(acc)

        @pl.when((k == 0) & (m == 0))
        def _():
            dg_ref[...] = jnp.zeros_like(dg_ref)

        acc[...] += _dot(dp_ref[...], w_ref[...])

        @pl.when(k == nk - 1)
        def _():
            xv = x_ref[...]
            r = lax.rsqrt(jnp.mean(xv * xv, axis=1, keepdims=True) + NORM_EPS)
            xhat = xv * r
            dh = acc[...]
            dxh = dh * g_ref[...]
            dx_ref[...] = dxo_ref[...] + r * (dxh - xhat * jnp.mean(dxh * xhat, axis=1, keepdims=True))
            dg_ref[...] += jnp.sum(dh * xhat, axis=0, keepdims=True)

    row = pl.BlockSpec((tm, d), lambda m, k: (m, 0))
    return _call(
        body, name="in_proj_bwd", grid=(s // tm, nk),
        in_specs=[pl.BlockSpec((tm, tk), lambda m, k: (m, k)),
                  pl.BlockSpec((tk, d), lambda m, k: (k, 0)),
                  row, row, pl.BlockSpec((1, d), lambda m, k: (0, 0))],
        out_specs=[row, pl.BlockSpec((8, d), lambda m, k: (0, 0))],
        out_shape=[jax.ShapeDtypeStruct((s, d), F32), jax.ShapeDtypeStruct((8, d), F32)],
        scratch=[pltpu.VMEM((tm, d), F32)], sem=("arbitrary", "arbitrary"), vmem_mb=56,
        args=(dproj, wt, x, dxo, g), comm=comm)


def _place():
    x, y, c = lax.axis_index("x"), lax.axis_index("y"), lax.axis_index("c")
    chips = [(1 - x, y), (x, 1 - y), (1 - x, 1 - y)]
    return x, y, c, chips


def _comm_call(body, name, ins, out_shape, n_remote, n_local, aliases=None):
    return pl.pallas_call(
        body, name=name, in_specs=[ANY] * len(ins), out_specs=[ANY] * len(out_shape), out_shape=out_shape,
        input_output_aliases=aliases or {},
        scratch_shapes=[pltpu.SemaphoreType.DMA((n_remote,)), pltpu.SemaphoreType.DMA((n_remote,)),
                        pltpu.SemaphoreType.DMA((n_local,))])(*ins)


def _rcopy(src, dst, send_sems, recv_sems, k, dev):
    return pltpu.make_async_remote_copy(src_ref=src, dst_ref=dst, send_sem=send_sems.at[k], recv_sem=recv_sems.at[k],
                                        device_id=dev, device_id_type=MESH)


def _run_comm(comm, name):
    n_ci, n_co = len(comm.ins), len(comm.outs)

    def body(*refs):
        cins, couts = refs[:n_ci], refs[n_ci:n_ci + n_co]
        send_sems, recv_sems = refs[n_ci + n_co:]
        comm.start(cins, couts, send_sems, recv_sems)
        comm.finish(cins, couts, send_sems, recv_sems)

    return pl.pallas_call(
        body, name=name, in_specs=[ANY] * n_ci, out_specs=[ANY] * n_co, out_shape=list(comm.outs),
        input_output_aliases=dict(comm.aliases),
        scratch_shapes=[pltpu.SemaphoreType.DMA((comm.n_sems,)), pltpu.SemaphoreType.DMA((comm.n_sems,))])(*comm.ins)


def _gather_comm(fulls, rbp=None, stage="all", chips_at=(0, 1, 2)):
    n = len(fulls)
    n_ici = 3 * n
    base = n_ici if stage == "all" else 0
    per_tensor = chips_at if isinstance(chips_at, list) else [chips_at] * n

    def half(full, blk, core):
        hr = full.shape[1] // 2
        return full.at[blk].at[pl.ds(core * hr, hr)]

    def ici(couts, send_sems, recv_sems, x, y, c, chips):
        b = 2 * x + y
        return [_rcopy(half(full, b, c), half(full, b, c), send_sems, recv_sems, 3 * t + j, (*chip, c))
                for t, full in enumerate(couts[:n]) for j, chip in enumerate(chips) if j in per_tensor[t]]

    def landed(couts, send_sems, recv_sems, x, y, c, chips, core, first, at=None):
        return [_rcopy(half(full, 2 * chip[0] + chip[1], core), half(full, 2 * chip[0] + chip[1], core),
                       send_sems, recv_sems, first + 3 * t + j, (x, y, 1 - c))
                for t, full in enumerate(couts[:n]) for j, chip in enumerate(chips) if at is None or j in at[t]]

    def small(cins, couts, send_sems, recv_sems, x, y, c, chips):
        b = 2 * x + y
        return ([_rcopy(cins[n], couts[n].at[b], send_sems, recv_sems, 2 * n_ici + j, (*chip, c))
                 for j, chip in enumerate(chips)],
                pltpu.make_async_copy(cins[n], couts[n].at[b], send_sems.at[2 * n_ici + 3]))

    def start(cins, couts, send_sems, recv_sems):
        x, y, c, chips = _place()
        if stage == "sibling":
            for cp in landed(couts, send_sems, recv_sems, x, y, c, chips, c, base):
                cp.start()
            return
        for cp in ici(couts, send_sems, recv_sems, x, y, c, chips):
            cp.start()
        if rbp is not None:
            remote, local = small(cins, couts, send_sems, recv_sems, x, y, c, chips)
            for cp in remote:
                cp.start()
            local.start()

    def finish(cins, couts, send_sems, recv_sems):
        x, y, c, chips = _place()
        passed = landed(couts, send_sems, recv_sems, x, y, c, chips, c, base)
        if stage != "sibling":
            for k, cp in enumerate(landed(couts, send_sems, recv_sems, x, y, c, chips, c, 0, per_tensor)):
                cp.wait_recv()
                if stage == "all":
                    passed[k].start()
            for cp in ici(couts, send_sems, recv_sems, x, y, c, chips):
                cp.wait_send()
        if stage != "chips":
            for cp in landed(couts, send_sems, recv_sems, x, y, c, chips, 1 - c, base):
                cp.wait_recv()
            for cp in passed:
                cp.wait_send()
        if rbp is not None:
            remote, local = small(cins, couts, send_sems, recv_sems, x, y, c, chips)
            for j, chip in enumerate(chips):
                got = couts[n].at[2 * chip[0] + chip[1]]
                _rcopy(got, got, send_sems, recv_sems, 2 * n_ici + j, (x, y, c)).wait_recv()
            for cp in remote:
                cp.wait_send()
            local.wait()

    outs = [jax.ShapeDtypeStruct(f.shape, f.dtype) for f in fulls]
    ins = list(fulls)
    if rbp is not None:
        ins.append(rbp)
        outs.append(jax.ShapeDtypeStruct((4,) + rbp.shape, F32))
    return _Comm(tuple(ins), tuple(outs), {t: t for t in range(n)}, 2 * n_ici + 4, start, finish)


def _chips_comm(sums):
    n = len(sums)

    def copies(cins, couts, send_sems, recv_sems):
        x, y, c, chips = _place()
        return [_rcopy(cins[t].at[2 * chip[0] + chip[1]], couts[t].at[j], send_sems, recv_sems, 3 * t + j, (*chip, c))
                for t in range(n) for j, chip in enumerate(chips)]

    def start(*refs):
        for cp in copies(*refs):
            cp.start()

    def finish(*refs):
        for cp in copies(*refs):
            cp.wait()

    outs = tuple(jax.ShapeDtypeStruct((3,) + p.shape[1:], p.dtype) for p in sums)
    return _Comm(tuple(sums), outs, {}, 3 * n, start, finish)


def _pair_comm(ins, outs, aliases, copies):
    def start(*refs):
        for cp in copies(*refs):
            cp.start()

    def finish(*refs):
        for cp in copies(*refs):
            cp.wait()

    return _Comm(tuple(ins), tuple(outs), aliases, len(ins), start, finish)


def _sibling_comm(parts):
    def copies(cins, couts, send_sems, recv_sems):
        x, y, c, _ = _place()
        return [_rcopy(p.at[:, pl.ds((1 - c) * (p.shape[1] // 2), p.shape[1] // 2), :], couts[t],
                       send_sems, recv_sems, t, (x, y, 1 - c)) for t, p in enumerate(cins)]

    half = [jax.ShapeDtypeStruct((p.shape[0], p.shape[1] // 2, p.shape[2]), p.dtype) for p in parts]
    return _pair_comm(parts, half, {}, copies)


def _halves_comm(tots):
    def copies(cins, couts, send_sems, recv_sems):
        x, y, c, _ = _place()
        return [_rcopy(g.at[c], g.at[c], send_sems, recv_sems, t, (x, y, 1 - c)) for t, g in enumerate(couts)]

    return _pair_comm(tots, [jax.ShapeDtypeStruct(t.shape, t.dtype) for t in tots],
                      {t: t for t in range(len(tots))}, copies)


def _gather_small(packed):
    def body(p_ref, all_ref, send_sems, recv_sems, loc_sems):
        x, y, c, _ = _place()
        me = 4 * x + 2 * y + c
        local = pltpu.make_async_copy(p_ref, all_ref.at[me], loc_sems.at[0])
        local.start()
        sent = []
        for k in range(1, 8):
            px, py, pc = x ^ (k >> 2), y ^ ((k >> 1) & 1), c ^ (k & 1)
            cp = _rcopy(p_ref, all_ref.at[me], send_sems, recv_sems, k - 1, (px, py, pc))
            cp.start()
            sent.append(cp)
        for k in range(1, 8):
            px, py, pc = x ^ (k >> 2), y ^ ((k >> 1) & 1), c ^ (k & 1)
            got = all_ref.at[4 * px + 2 * py + pc]
            _rcopy(got, got, send_sems, recv_sems, k - 1, (x, y, c)).wait_recv()
        for cp in sent:
            cp.wait_send()
        local.wait()

    return _comm_call(body, "gather_small", [packed], [jax.ShapeDtypeStruct((8,) + packed.shape, F32)], 7, 1)[0]


def _sum_devices(allp):
    n, r, c = allp.shape

    def body(a_ref, o_ref):
        acc = a_ref[0]
        for k in range(1, n):
            acc = acc + a_ref[k]
        o_ref[...] = acc

    return pl.pallas_call(body, name="sum_devices", out_shape=jax.ShapeDtypeStruct((r, c), F32))(allp)


def _ext_index():
    u = np.arange(EXT)
    dist = np.where(u < WIN, PAD - u, PAD + EXT - u)
    return np.clip(dist, -(CHUNK - 1), REL_CLIP) + (CHUNK - 1)


def _pack(parts, rows):
    flat = jnp.concatenate([p.reshape(-1) for p in parts])
    return jnp.pad(flat, (0, rows * 128 - flat.shape[0])).reshape(rows, 128)


def _unpack(packed, shapes):
    flat, out, at = packed.reshape(-1), [], 0
    for shp in shapes:
        size = int(np.prod(shp))
        out.append(flat[at:at + size].reshape(shp))
        at += size
    return out


def kernel(x, norm_g, w_in, q_norm_g, k_norm_g, rel_bias, w_out, loss_target, m_norm_g, m_w_in, m_q_norm_g, m_k_norm_g, m_rel_bias, m_w_out, v_norm_g, v_w_in, v_q_norm_g, v_k_norm_g, v_rel_bias, v_w_out):
    nl, d, nb = w_in.shape
    s = x.shape[1]
    ds_ = d // 2
    nh = ds_ // HEAD
    rb = w_out.shape[1]
    nrel = rel_bias.shape[2]
    bx = lax.axis_index("x") * 2 + lax.axis_index("y")

    rb_rows = -(-(nl * nh * nrel) // 1024) * 8
    cx = lax.axis_index("c")
    wi_full = [_cast_block(w_in, l, bx, "cast_w_in") for l in range(nl)]
    wo_full = [_cast_block(w_out, l, bx, "cast_w_out") for l in range(nl)]
    wi_full[0], wo_full[0], rel_all = _run_comm(
        _gather_comm([wi_full[0], wo_full[0]], _pack([rel_bias], rb_rows)), "gather_first")
    rel_full = jnp.concatenate(
        [rel_all[j].reshape(-1)[:nl * nh * nrel].reshape(nl, nh, nrel) for j in range(4)], axis=2)
    ext_idx = _ext_index()
    onehot = jnp.asarray(ext_idx[:, None] == np.arange(N_REL)[None, :], F32)
    ext = jnp.einsum("lhr,ur->lhu", rel_full, onehot, precision=lax.Precision.HIGHEST).reshape(nl, nh, 1, EXT)

    xs, hs, projs, yas, lts, ybs, mixes, wi_t = [], [], [], [], [], [], [], []
    xc = x[0]
    for l in range(nl):
        h = _rmsnorm_fwd(xc, norm_g[l:l + 1])
        nxt = l + 1 < nl
        (proj, wt), got = _in_proj(h, wi_full[l], _gather_comm([wo_full[l + 1]], stage="chips") if nxt else None)
        wi_t.append(wt)
        if nxt:
            wo_full[l + 1] = got[0]
        (ya, lt), got = _attn_a_fwd(proj, nh, _gather_comm(
            [wi_full[l + 1]], stage="chips", chips_at=(0, 1)) if nxt else None)
        if nxt:
            wi_full[l + 1] = got[0]
        (yb,), got = _attn_b_fwd(proj, q_norm_g[l:l + 1], k_norm_g[l:l + 1], ext[l], nh,
                                 _gather_comm([wi_full[l + 1]], stage="chips", chips_at=(2,)) if nxt else None)
        if nxt:
            wi_full[l + 1] = got[0]
        xs.append(xc)
        wo_now = wo_full[l].reshape(4 * rb, d)
        (xc, mix), got = _out_proj(xc, ya, yb, proj, wo_now, _gather_comm(
            [wi_full[l + 1], wo_full[l + 1]], stage="sibling") if nxt else None)
        if nxt:
            wi_full[l + 1], wo_full[l + 1] = got
        hs.append(h); projs.append(proj); yas.append(ya); lts.append(lt); ybs.append(yb); mixes.append(mix)
    dx, loss_tile = _loss_head(xc, loss_target[0])

    small, g_wi, g_wo = [None] * nl, [None] * nl, [None] * nl
    pending = None

    def keep(lay, shared):
        g_wi[lay], g_wo[lay] = shared[0].reshape(d, nb), shared[1].reshape(rb, d)

    for l in reversed(range(nl)):
        wo = wo_full[l].reshape(4 * rb, d)
        p_wo = _wgrad(mixes[l], dx, 4, False, "wgrad_out")
        dya, dyb, dga, dgb = _out_proj_bwd(dx, yas[l], ybs[l], projs[l], wo)
        (dqa, dka, dva), got = _attn_a_bwd(projs[l], lts[l], dya, nh, _chips_comm(pending[1]) if pending else None)
        tots = [_add_chips(pending[1][t], got[t], bx, cx) for t in range(2)] if pending else None
        (dqb, dkb, dvb, dqg, dkg, dext), shared = _attn_b_bwd(
            projs[l], dyb, q_norm_g[l:l + 1], k_norm_g[l:l + 1], ext[l], nh, _halves_comm(tots) if pending else None)
        if pending:
            keep(pending[0], shared)
        dproj = jnp.concatenate([dqa, dka, dva, dga, dqb, dkb, dvb, dgb], axis=1)
        parts = [_wgrad(hs[l], dproj, 4, True, "wgrad_in"), p_wo]
        if l > 0:
            (dx, dng), theirs = _in_proj_bwd(dproj, wi_t[l],xs[l], dx, norm_g[l:l + 1], _sibling_comm(parts))
            pending = (l, [_add_sibling(parts[t], theirs[t], cx) for t in range(2)])
        else:
            theirs = _run_comm(_sibling_comm(parts), "reduce_sibling")
            sums = [_add_sibling(parts[t], theirs[t], cx) for t in range(2)]
            (dx, dng), got = _in_proj_bwd(dproj, wi_t[l],xs[l], dx, norm_g[l:l + 1], _chips_comm(sums))
            keep(0, _run_comm(_halves_comm([_add_chips(sums[t], got[t], bx, cx) for t in range(2)]), "share_halves"))
        small[l] = (dng[0], jnp.sum(dqg, axis=0).reshape(-1), jnp.sum(dkg, axis=0).reshape(-1), dext.reshape(nh, EXT))
    grad_x = dx[None]

    small_shapes = [(nl, d), (nl, HEAD), (nl, HEAD), (nl, nh, EXT), (1,)]
    small_parts = [jnp.stack([sm[i] for sm in small]) for i in range(4)] + [loss_tile[0, :1]]
    rows = -(-sum(int(np.prod(sh)) for sh in small_shapes) // 1024) * 8
    tot = _sum_devices(_gather_small(_pack(small_parts, rows)))
    g_ng, g_qg, g_kg, g_ext, loss = _unpack(tot, small_shapes)
    g_rel_full = jnp.einsum("lhu,ur->lhr", g_ext, onehot, precision=lax.Precision.HIGHEST)
    g_rel = lax.dynamic_slice_in_dim(g_rel_full, bx * nrel, nrel, axis=2)

    res_wi, res_wo = (), ()
    for l in range(nl):
        res_wi = _adamw_layer(l, w_in, g_wi[l], m_w_in, v_w_in, res_wi, "adamw_w_in")
        res_wo = _adamw_layer(l, w_out, g_wo[l], m_w_out, v_w_out, res_wo, "adamw_w_out")
    g_wi, d_wi, nm_wi, nv_wi = res_wi
    g_wo, d_wo, nm_wo, nv_wo = res_wo
    sm_shapes = [(nl, d), (nl, HEAD), (nl, HEAD), (nl, nh, nrel)]
    sm_rows = -(-sum(int(np.prod(sh)) for sh in sm_shapes) // 1024) * 8
    pw, pg, pm, pv = [_pack(group, sm_rows) for group in (
        (norm_g, q_norm_g, k_norm_g, rel_bias), (g_ng, g_qg, g_kg, g_rel),
        (m_norm_g, m_q_norm_g, m_k_norm_g, m_rel_bias), (v_norm_g, v_q_norm_g, v_k_norm_g, v_rel_bias))]
    d_sm, nm_sm, nv_sm = [_unpack(a[0], sm_shapes)
                          for a in _adamw_layer(0, pw[None], pg, pm[None], pv[None], (), "adamw_small")[1:]]

    return (loss[0], grad_x, g_ng, g_wi, g_qg, g_kg, g_rel, g_wo,
            d_sm[0], d_wi, d_sm[1], d_sm[2], d_sm[3], d_wo,
            nm_sm[0], nm_wi, nm_sm[1], nm_sm[2], nm_sm[3], nm_wo,
            nv_sm[0], nv_wi, nv_sm[1], nv_sm[2], nv_sm[3], nv_wo)
```

```python
from typing import Callable, NamedTuple

import jax
import jax.numpy as jnp
import numpy as np
from jax import lax
from jax.experimental import pallas as pl
from jax.experimental.pallas import tpu as pltpu

F32 = jnp.float32
BF16 = jnp.bfloat16

HEAD = 128
CHUNK = 64
LEFT_CHUNKS = 8
REL_CLIP = 256
N_REL = REL_CLIP + CHUNK
NORM_EPS = 1e-6
NEG_BIG = -1e30
TQ = 256
TK = TQ
QB = 512
ROWS = 32
PAD = LEFT_CHUNKS * CHUNK
WIN = PAD + TQ
EXT = 1024
SCALE = HEAD ** -0.5

ADAM_LR = 0.001
ADAM_B1 = 0.9
ADAM_B2 = 0.999
ADAM_EPS = 1e-08
ADAM_WD = 0.01
ADAM_STEP = 10

ANY = pl.BlockSpec(memory_space=pl.ANY)
MESH = pl.DeviceIdType.MESH


def _params(sem=None, vmem_mb=None):
    kw = {}
    if sem is not None:
        kw["dimension_semantics"] = sem
    if vmem_mb is not None:
        kw["vmem_limit_bytes"] = vmem_mb << 20
    return pltpu.CompilerParams(**kw)


class _Comm(NamedTuple):
    ins: tuple
    outs: tuple
    aliases: dict
    n_sems: int
    start: Callable
    finish: Callable


def _call(body, *, name, grid, in_specs, out_specs, out_shape, args, scratch=(), sem=None, vmem_mb=None, comm=None):
    if comm is None:
        out = pl.pallas_call(body, name=name, grid=grid, in_specs=in_specs, out_specs=out_specs, out_shape=out_shape,
                             scratch_shapes=list(scratch), compiler_params=_params(sem, vmem_mb))(*args)
        return out, ()
    n_in, n_out, n_ci, n_co = len(in_specs), len(out_shape), len(comm.ins), len(comm.outs)

    def hosted(*refs):
        ins, cins = refs[:n_in], refs[n_in:n_in + n_ci]
        outs, couts = refs[n_in + n_ci:n_in + n_ci + n_out], refs[n_in + n_ci + n_out:n_in + n_ci + n_out + n_co]
        rest = refs[n_in + n_ci + n_out + n_co:]
        send_sems, recv_sems = rest[-2:]
        first, last = None, None
        for ax, size in enumerate(grid):
            at = pl.program_id(ax)
            first = (at == 0) if first is None else first & (at == 0)
            last = (at == size - 1) if last is None else last & (at == size - 1)

        @pl.when(first)
        def _():
            comm.start(cins, couts, send_sems, recv_sems)

        body(*ins, *outs, *rest[:-2])

        @pl.when(last)
        def _():
            comm.finish(cins, couts, send_sems, recv_sems)

    out = pl.pallas_call(
        hosted, name=name, grid=grid, in_specs=list(in_specs) + [ANY] * n_ci, out_specs=list(out_specs) + [ANY] * n_co,
        out_shape=list(out_shape) + list(comm.outs),
        input_output_aliases={n_in + k: n_out + v for k, v in comm.aliases.items()},
        scratch_shapes=list(scratch) + [pltpu.SemaphoreType.DMA((comm.n_sems,)), pltpu.SemaphoreType.DMA((comm.n_sems,))],
        compiler_params=_params(("arbitrary",) * len(grid), vmem_mb))(*args, *comm.ins)
    return out[:n_out], out[n_out:]


def _dot(a, b):
    return jnp.dot(a, b, preferred_element_type=F32)


def _dot_nt(a, b):
    return lax.dot_general(a, b, (((1,), (1,)), ((), ())), preferred_element_type=F32)


def _dot_tn(a, b):
    return lax.dot_general(a, b, (((0,), (0,)), ((), ())), preferred_element_type=F32)


def _split_dot(x, m):
    hi = x.astype(BF16)
    lo = (x - hi.astype(F32)).astype(BF16)
    return _dot(hi, m) + _dot(lo, m)


def _silu_parts(g):
    sg = 1.0 / (1.0 + jnp.exp(-g))
    return g * sg, sg * (1.0 + g * (1.0 - sg))


def _idx(*vals):
    return jnp.stack([jnp.asarray(v, jnp.int32) for v in vals])


def _cast_block(w, l, blk, name):
    _, r, c = w.shape
    tr = min(r, 512)

    def body(b_ref, w_ref, o_ref):
        o_ref[...] = w_ref[...].astype(BF16)

    spec = pltpu.PrefetchScalarGridSpec(
        num_scalar_prefetch=1, grid=(r // tr,),
        in_specs=[pl.BlockSpec((None, tr, c), lambda i, b: (l, i, 0))],
        out_specs=pl.BlockSpec((None, tr, c), lambda i, b: (b[0], i, 0)))
    return pl.pallas_call(body, name=name, grid_spec=spec, out_shape=jax.ShapeDtypeStruct((4, r, c), BF16),
                          compiler_params=_params(("parallel",)))(_idx(blk), w)


def _add_sibling(p, theirs, core):
    nblk, r, c = p.shape
    hr = r // 2
    tr = min(hr, 256)
    per = hr // tr

    def body(c_ref, p_ref, t_ref, o_ref):
        o_ref[...] = (p_ref[...].astype(F32) + t_ref[...].astype(F32)).astype(BF16)

    blk = pl.BlockSpec((None, tr, c), lambda j, i, cr: (j, i, 0))
    spec = pltpu.PrefetchScalarGridSpec(
        num_scalar_prefetch=1, grid=(nblk, per),
        in_specs=[pl.BlockSpec((None, tr, c), lambda j, i, cr: (j, cr[0] * per + i, 0)), blk], out_specs=blk)
    return pl.pallas_call(body, name="add_sibling", grid_spec=spec, out_shape=jax.ShapeDtypeStruct((nblk, hr, c), BF16),
                          compiler_params=_params(("parallel", "parallel")))(_idx(core), p, theirs)


def _add_chips(sums, got, blk, core):
    _, hr, c = sums.shape
    tr = min(hr, 256)

    def body(i_ref, s_ref, g0_ref, g1_ref, g2_ref, o_ref):
        o_ref[...] = ((s_ref[...].astype(F32) + g0_ref[...].astype(F32))
                      + g1_ref[...].astype(F32)) + g2_ref[...].astype(F32)

    at = lambda j: pl.BlockSpec((None, tr, c), lambda i, ir: (j, i, 0))
    spec = pltpu.PrefetchScalarGridSpec(
        num_scalar_prefetch=1, grid=(hr // tr,),
        in_specs=[pl.BlockSpec((None, tr, c), lambda i, ir: (ir[0], i, 0)), at(0), at(1), at(2)],
        out_specs=pl.BlockSpec((None, tr, c), lambda i, ir: (ir[1], i, 0)))
    return pl.pallas_call(body, name="add_chips", grid_spec=spec, out_shape=jax.ShapeDtypeStruct((2, hr, c), F32),
                          compiler_params=_params(("parallel",)))(_idx(blk, core), sums, got, got, got)


def _adamw_layer(l, w, g, m, v, prev, name):
    nl, r, c = w.shape
    tr = min(r, 256)
    c1 = 1.0 / (1.0 - ADAM_B1 ** ADAM_STEP)
    c2 = 1.0 / (1.0 - ADAM_B2 ** ADAM_STEP)

    def body(w_ref, g_ref, m_ref, v_ref, *rest):
        go_ref, d_ref, nm_ref, nv_ref = rest[-4:]
        gg = g_ref[...]
        nm = ADAM_B1 * m_ref[...] + (1.0 - ADAM_B1) * gg
        nv = ADAM_B2 * v_ref[...] + (1.0 - ADAM_B2) * (gg * gg)
        upd = (nm * c1) / (jnp.sqrt(nv * c2) + ADAM_EPS) + ADAM_WD * w_ref[...]
        go_ref[...] = gg
        d_ref[...] = -ADAM_LR * upd
        nm_ref[...] = nm
        nv_ref[...] = nv

    lay = pl.BlockSpec((None, tr, c), lambda i: (l, i, 0))
    shp = jax.ShapeDtypeStruct((nl, r, c), F32)
    return pl.pallas_call(
        body, name=name, grid=(r // tr,),
        in_specs=[lay, pl.BlockSpec((tr, c), lambda i: (i, 0)), lay, lay] + [ANY] * len(prev),
        out_specs=[lay] * 4, out_shape=[shp] * 4, input_output_aliases={4 + k: k for k in range(len(prev))},
        compiler_params=_params(("parallel",), 40))(w, g, m, v, *prev)


def _rmsnorm_fwd(x, g):
    s, d = x.shape
    tm = min(s, 256)

    def body(x_ref, g_ref, h_ref):
        xv = x_ref[...]
        r = lax.rsqrt(jnp.mean(xv * xv, axis=1, keepdims=True) + NORM_EPS)
        h_ref[...] = (xv * r * g_ref[...]).astype(BF16)

    return pl.pallas_call(
        body, name="rmsnorm_fwd", grid=(s // tm,),
        in_specs=[pl.BlockSpec((tm, d), lambda i: (i, 0)), pl.BlockSpec((1, d), lambda i: (0, 0))],
        out_specs=pl.BlockSpec((tm, d), lambda i: (i, 0)),
        out_shape=jax.ShapeDtypeStruct((s, d), BF16),
        compiler_params=_params(("parallel",)))(x, g)


def _in_proj(h, w, comm=None):
    s, d = h.shape
    nblk, _, nb = w.shape
    tm, tn = min(s, 1024), min(nb, 1024)
    per = nb // tn

    def body(h_ref, w_ref, o_ref, wt_ref):
        o_ref[...] = _dot(h_ref[...], w_ref[...])

        @pl.when(pl.program_id(1) == 0)
        def _():
            for c in range(tn // HEAD):
                wt_ref[pl.ds(c * HEAD, HEAD), :] = w_ref[:, c * HEAD:(c + 1) * HEAD].astype(F32).T.astype(BF16)

    return _call(
        body, name="in_proj", grid=(nblk * per, s // tm),
        in_specs=[pl.BlockSpec((tm, d), lambda n, m: (m, 0)),
                  pl.BlockSpec((None, d, tn), lambda n, m: (n // per, 0, n % per))],
        out_specs=[pl.BlockSpec((tm, tn), lambda n, m: (m, n)), pl.BlockSpec((tn, d), lambda n, m: (n, 0))],
        out_shape=[jax.ShapeDtypeStruct((s, nblk * nb), F32), jax.ShapeDtypeStruct((nblk * nb, d), BF16)],
        sem=("parallel", "arbitrary"), vmem_mb=56, args=(h, w), comm=comm)


def _heads_per_step(nh):
    return 2 if nh % 2 == 0 else 1


def _head(hh):
    return slice(hh * HEAD, (hh + 1) * HEAD)


def _tri(op):
    r = lax.broadcasted_iota(jnp.int32, (TQ, TQ), 0)
    c = lax.broadcasted_iota(jnp.int32, (TQ, TQ), 1)
    return op(r, c)


def _staggered(groups, hp):
    for hh in range(hp):
        for fn in groups[0]:
            fn(hh)
    for group in groups[1:]:
        for hh in range(hp):
            for fn in group:
                fn(hh)


def _transpose_tiles(src_ref, dst_ref, hp, nt):
    for hh in range(hp):
        for t in range(nt):
            dst_ref[hh, t] = src_ref[pl.ds(t * TK, TK), _head(hh)].T.astype(BF16)


def _chunk_mask(r, a):
    if a is None or r * ROWS >= (a + 1) * TK:
        return None
    if (r + 1) * ROWS <= a * TK:
        return False
    row = lax.broadcasted_iota(jnp.int32, (ROWS, TK), 0) + r * ROWS
    return row > lax.broadcasted_iota(jnp.int32, (ROWS, TK), 1) + a * TK


def _sb_logs(qk, causal):
    z = qk * SCALE
    l1p = jnp.log(1.0 + jnp.exp(-jnp.abs(z)))
    ls = jnp.minimum(-z, 0.0) - l1p
    if causal is not None:
        ls = jnp.where(causal, ls, 0.0)
    return ls, jnp.minimum(z, 0.0) - l1p


def _attn_a_fwd(proj, nh, comm=None):
    s = proj.shape[0]
    qb_rows = min(QB, s)
    nq, nt, per = s // qb_rows, s // TK, qb_rows // TK
    hp = _heads_per_step(nh)
    ng = nh // hp
    chunks = qb_rows // ROWS

    def body(q_ref, k_ref, v_ref, o_ref, lt_ref, acc_ref, qb_ref, m_ref, car_ref, sum_ref,
             z_ref, lsig_ref, aft_ref, hi_ref, lo_ref, w_ref, kt_ref):
        i = pl.program_id(1)

        @pl.when(i == 0)
        def _():
            _transpose_tiles(k_ref, kt_ref, hp, nt)

        qb_ref[...] = q_ref[...].astype(BF16)
        m_ref[...] = _tri(lambda r, c: r > c).astype(BF16)
        acc_ref[...] = jnp.zeros_like(acc_ref)
        car_ref[...] = jnp.zeros_like(car_ref)

        def tile(j, a):
            off = pl.multiple_of(j * TK, TK)

            def scores(hh):
                z_ref[hh] = _dot(qb_ref[:, _head(hh)], kt_ref[hh, j])

            def logs(hh):
                for r in range(chunks):
                    sl, mask = pl.ds(r * ROWS, ROWS), _chunk_mask(r, a)
                    if mask is False:
                        hi_ref[hh, sl, :] = jnp.zeros((ROWS, TK), BF16)
                        lo_ref[hh, sl, :] = jnp.zeros((ROWS, TK), BF16)
                        continue
                    ls, lsig = _sb_logs(z_ref[hh, sl, :], mask)
                    lsig_ref[hh, sl, :] = lsig
                    hi = ls.astype(BF16)
                    hi_ref[hh, sl, :] = hi
                    lo_ref[hh, sl, :] = (ls - hi.astype(F32)).astype(BF16)
                    sum_ref[hh, sl, :] = jnp.sum(ls, axis=1, keepdims=True)

            def after(hh):
                aft_ref[hh] = _dot(hi_ref[hh], m_ref[...]) + _dot(lo_ref[hh], m_ref[...])

            def weights(hh):
                for r in range(chunks):
                    sl, mask = pl.ds(r * ROWS, ROWS), _chunk_mask(r, a)
                    if mask is False:
                        w_ref[hh, sl, :] = jnp.zeros((ROWS, TK), BF16)
                        continue
                    w = jnp.exp(lsig_ref[hh, sl, :] + aft_ref[hh, sl, :] + car_ref[hh, sl, :])
                    if mask is not None:
                        w = jnp.where(mask, w, 0.0)
                    w_ref[hh, sl, :] = w.astype(BF16)
                    car_ref[hh, sl, :] += sum_ref[hh, sl, :]

            def values(hh):
                vb = v_ref[pl.ds(off, TK), _head(hh)].astype(BF16)
                acc_ref[:, _head(hh)] += _dot(w_ref[hh], vb)

            _staggered([(scores,), (logs, after), (weights, values)], hp)

        for a in reversed(range(per)):
            tile(i * per + a, a)

        def step(t, carry):
            tile(i * per - 1 - t, None)
            return carry

        lax.fori_loop(0, i * per, step, 0)
        o_ref[...] = acc_ref[...]
        lt_ref[...] = car_ref[...]

    wd = hp * HEAD
    sq = lambda dt: pltpu.VMEM((hp, qb_rows, TK), dt)
    return _call(
        body, name="attn_a_fwd", grid=(ng, nq),
        in_specs=[pl.BlockSpec((qb_rows, wd), lambda h, i: (i, h)),
                  pl.BlockSpec((s, wd), lambda h, i: (0, ng + h)),
                  pl.BlockSpec((s, wd), lambda h, i: (0, 2 * ng + h))],
        out_specs=[pl.BlockSpec((qb_rows, wd), lambda h, i: (i, h)),
                   pl.BlockSpec((hp, qb_rows, 1), lambda h, i: (h, i, 0))],
        out_shape=[jax.ShapeDtypeStruct((s, nh * HEAD), F32), jax.ShapeDtypeStruct((nh, s, 1), F32)],
        scratch=[pltpu.VMEM((qb_rows, wd), F32), pltpu.VMEM((qb_rows, wd), BF16), pltpu.VMEM((TK, TK), BF16),
                 pltpu.VMEM((hp, qb_rows, 1), F32), pltpu.VMEM((hp, qb_rows, 1), F32),
                 sq(F32), sq(F32), sq(F32), sq(BF16), sq(BF16), sq(BF16),
                 pltpu.VMEM((hp, nt, HEAD, TK), BF16)],
        sem=("parallel", "arbitrary"), args=(proj, proj, proj), comm=comm)


def _band_valid(i):
    cl = lax.broadcasted_iota(jnp.int32, (TQ, WIN), 0) // CHUNK
    kl = lax.broadcasted_iota(jnp.int32, (TQ, WIN), 1) // CHUNK
    first = LEFT_CHUNKS - (TQ // CHUNK) * i
    return (kl >= cl) & (kl <= cl + LEFT_CHUNKS) & (kl >= first)


def _build_bias(e_ref, bias_ref):
    e8 = jnp.broadcast_to(e_ref[...], (8, EXT))
    row = lax.broadcasted_iota(jnp.int32, (8, EXT), 0)
    t8 = jnp.zeros((8, EXT), F32)
    for b in range(8):
        t8 = jnp.where(row == b, pltpu.roll(e8, b, 1) if b else e8, t8)
    for a in range(TQ // 8):
        sl = pltpu.roll(t8, 8 * a, 1) if a else t8
        bias_ref[pl.ds(8 * a, 8), :] = sl[:, :WIN]


def _reduce_bias_grad(db_ref):
    acc = jnp.zeros((8, EXT), F32)
    for a in range(TQ // 8):
        sl = db_ref[pl.ds(8 * a, 8), :]
        acc = acc + (pltpu.roll(sl, EXT - 8 * a, 1) if a else sl)
    row = lax.broadcasted_iota(jnp.int32, (8, EXT), 0)
    tot = jnp.zeros((8, EXT), F32)
    for b in range(8):
        tot = tot + jnp.where(row == b, pltpu.roll(acc, EXT - b, 1) if b else acc, 0.0)
    return jnp.sum(tot, axis=0, keepdims=True)


def _band_fill(k_ref, v_ref, kg_ref, kn_pad, v_pad, s):
    k = k_ref[...]
    rk = lax.rsqrt(jnp.mean(k * k, axis=1, keepdims=True) + NORM_EPS)
    kn_pad[pl.ds(0, PAD), :] = jnp.zeros((PAD, HEAD), BF16)
    kn_pad[pl.ds(PAD, s), :] = (k * rk * kg_ref[...]).astype(BF16)
    v_pad[pl.ds(0, PAD), :] = jnp.zeros((PAD, HEAD), BF16)
    v_pad[pl.ds(PAD, s), :] = v_ref[...].astype(BF16)


def _band_probs(q_ref, qg_ref, kn_pad, bias_ref, i):
    q = q_ref[...]
    rq = lax.rsqrt(jnp.mean(q * q, axis=1, keepdims=True) + NORM_EPS)
    qhat = q * rq
    qn = (qhat * qg_ref[...]).astype(BF16)
    off = pl.multiple_of(i * TQ, TQ)
    kw = kn_pad[pl.ds(off, WIN), :]
    sc = _dot_nt(qn, kw) * SCALE + bias_ref[...]
    sc = jnp.where(_band_valid(i), sc, NEG_BIG)
    p = jnp.exp(sc - jnp.max(sc, axis=1, keepdims=True))
    pn = p / jnp.sum(p, axis=1, keepdims=True)
    return rq, qhat, qn, kw, off, pn


def _attn_b_fwd(proj, qg, kg, ext, nh, comm=None):
    s = proj.shape[0]
    nq = s // TQ

    def body(q_ref, k_ref, v_ref, qg_ref, kg_ref, e_ref, o_ref, kn_pad, v_pad, bias_ref):
        i = pl.program_id(1)

        @pl.when(i == 0)
        def _():
            _band_fill(k_ref, v_ref, kg_ref, kn_pad, v_pad, s)
            _build_bias(e_ref, bias_ref)

        _, _, _, _, off, pn = _band_probs(q_ref, qg_ref, kn_pad, bias_ref, i)
        o_ref[...] = _dot(pn.astype(BF16), v_pad[pl.ds(off, WIN), :])

    vec = pl.BlockSpec((1, HEAD), lambda h, i: (0, 0))
    return _call(
        body, name="attn_b_fwd", grid=(nh, nq),
        in_specs=[pl.BlockSpec((TQ, HEAD), lambda h, i: (i, 4 * nh + h)),
                  pl.BlockSpec((s, HEAD), lambda h, i: (0, 5 * nh + h)),
                  pl.BlockSpec((s, HEAD), lambda h, i: (0, 6 * nh + h)),
                  vec, vec,
                  pl.BlockSpec((None, 1, EXT), lambda h, i: (h, 0, 0))],
        out_specs=[pl.BlockSpec((TQ, HEAD), lambda h, i: (i, h))],
        out_shape=[jax.ShapeDtypeStruct((s, nh * HEAD), F32)],
        scratch=[pltpu.VMEM((s + PAD, HEAD), BF16), pltpu.VMEM((s + PAD, HEAD), BF16), pltpu.VMEM((TQ, WIN), F32)],
        sem=("parallel", "arbitrary"), args=(proj, proj, proj, qg, kg, ext), comm=comm)


def _out_proj(x, ya, yb, proj, w, comm=None):
    s, d = x.shape
    ds_ = ya.shape[1]
    tm = min(s, 256)

    def body(x_ref, ya_ref, yb_ref, ga_ref, gb_ref, w_ref, o_ref, mix_ref):
        ma = (ya_ref[...] * _silu_parts(ga_ref[...])[0]).astype(BF16)
        mb = (yb_ref[...] * _silu_parts(gb_ref[...])[0]).astype(BF16)
        mix_ref[:, :ds_] = ma
        mix_ref[:, ds_:] = mb
        o_ref[...] = x_ref[...] + _dot(ma, w_ref[pl.ds(0, ds_), :]) + _dot(mb, w_ref[pl.ds(ds_, ds_), :])

    row = lambda width: pl.BlockSpec((tm, width), lambda i: (i, 0))
    return _call(
        body, name="out_proj", grid=(s // tm,),
        in_specs=[row(d), row(ds_), row(ds_),
                  pl.BlockSpec((tm, ds_), lambda i: (i, 3)), pl.BlockSpec((tm, ds_), lambda i: (i, 7)),
                  pl.BlockSpec((2 * ds_, d), lambda i: (0, 0))],
        out_specs=[row(d), row(2 * ds_)],
        out_shape=[jax.ShapeDtypeStruct((s, d), F32), jax.ShapeDtypeStruct((s, 2 * ds_), BF16)],
        sem=("parallel",), vmem_mb=48, args=(x, ya, yb, proj, proj, w), comm=comm)


def _loss_head(y, tgt):
    s, d = y.shape
    tm = min(s, 256)

    def body(y_ref, t_ref, dy_ref, l_ref):
        @pl.when(pl.program_id(0) == 0)
        def _():
            l_ref[...] = jnp.zeros_like(l_ref)

        err = y_ref[...] - t_ref[...]
        dy_ref[...] = err * (1.0 / d)
        l_ref[...] += 0.5 * jnp.sum(jnp.mean(err * err, axis=1, keepdims=True), axis=0, keepdims=True)

    row = pl.BlockSpec((tm, d), lambda i: (i, 0))
    return pl.pallas_call(
        body, name="loss_head", grid=(s // tm,), in_specs=[row, row],
        out_specs=[row, pl.BlockSpec((8, 128), lambda i: (0, 0))],
        out_shape=[jax.ShapeDtypeStruct((s, d), F32), jax.ShapeDtypeStruct((8, 128), F32)],
        compiler_params=_params(("arbitrary",)))(y, tgt)


def _out_proj_bwd(dxo, ya, yb, proj, w):
    s, d = dxo.shape
    ds_ = ya.shape[1]
    tm = min(s, 256)

    def body(dx_ref, ya_ref, yb_ref, ga_ref, gb_ref, w_ref, dya_ref, dyb_ref, dga_ref, dgb_ref):
        dxb = dx_ref[...].astype(BF16)
        for y_ref, g_ref, lo, dy_ref, dg_ref in ((ya_ref, ga_ref, 0, dya_ref, dga_ref),
                                                 (yb_ref, gb_ref, ds_, dyb_ref, dgb_ref)):
            dmix = _dot_nt(dxb, w_ref[pl.ds(lo, ds_), :])
            act, dact = _silu_parts(g_ref[...])
            dy_ref[...] = dmix * act
            dg_ref[...] = (dmix * y_ref[...] * dact).astype(BF16)

    row = lambda width: pl.BlockSpec((tm, width), lambda i: (i, 0))
    return pl.pallas_call(
        body, name="out_proj_bwd", grid=(s // tm,),
        in_specs=[row(d), row(ds_), row(ds_),
                  pl.BlockSpec((tm, ds_), lambda i: (i, 3)), pl.BlockSpec((tm, ds_), lambda i: (i, 7)),
                  pl.BlockSpec((2 * ds_, d), lambda i: (0, 0))],
        out_specs=[row(ds_)] * 4,
        out_shape=[jax.ShapeDtypeStruct((s, ds_), F32)] * 2 + [jax.ShapeDtypeStruct((s, ds_), BF16)] * 2,
        compiler_params=_params(("parallel",), 48))(dxo, ya, yb, proj, proj, w)


def _wgrad(a, b, nblk, col_blocks, name):
    s, m = a.shape
    n = b.shape[1]
    if col_blocks:
        tr = min(m, 1024)
        nb = n // nblk
        tn = min(nb, 2048)
        per = nb // tn
        out_shape = (nblk, m, nb)
        out_spec = pl.BlockSpec((None, tr, tn), lambda j, r: (j // per, r, j % per))
    else:
        tn = min(n, 1024)
        tr = m // nblk
        out_shape = (nblk, tr, n)
        out_spec = pl.BlockSpec((None, tr, tn), lambda j, r: (r, 0, j))

    def body(a_ref, b_ref, o_ref):
        o_ref[...] = _dot_tn(a_ref[...].astype(BF16), b_ref[...].astype(BF16)).astype(BF16)

    return pl.pallas_call(
        body, name=name, grid=(n // tn, m // tr),
        in_specs=[pl.BlockSpec((s, tr), lambda j, r: (0, r)), pl.BlockSpec((s, tn), lambda j, r: (0, j))],
        out_specs=out_spec, out_shape=jax.ShapeDtypeStruct(out_shape, BF16),
        compiler_params=_params(("parallel", "parallel"), 48))(a, b)


def _attn_a_bwd(proj, lt, dya, nh, comm=None):
    s = proj.shape[0]
    qb_rows = min(QB, s)
    nq, nt, per = s // qb_rows, s // TK, qb_rows // TK
    hp = _heads_per_step(nh)
    ng = nh // hp
    chunks = qb_rows // ROWS

    def body(q_ref, k_ref, v_ref, lt_ref, do_ref, dq_ref, dk_ref, dv_ref, dq_acc, dk_acc, dv_acc,
             qb_ref, dob_ref, upto_ref, before_ref, cls_ref, cg_ref, sls_ref, sg_ref,
             z_ref, dw_ref, lsig_ref, pre_ref, g_ref, hi_ref, lo_ref, wb_ref, kt_ref, vt_ref, qt_ref, dot_ref):
        i = pl.program_id(1)

        @pl.when(i == 0)
        def _():
            dk_acc[...] = jnp.zeros_like(dk_acc)
            dv_acc[...] = jnp.zeros_like(dv_acc)
            _transpose_tiles(k_ref, kt_ref, hp, nt)
            _transpose_tiles(v_ref, vt_ref, hp, nt)

        dq_acc[...] = jnp.zeros_like(dq_acc)
        qb_ref[...] = q_ref[...].astype(BF16)
        dob_ref[...] = do_ref[...].astype(BF16)
        for hh in range(hp):
            qt_ref[hh] = q_ref[:, _head(hh)].T.astype(BF16)
            dot_ref[hh] = do_ref[:, _head(hh)].T.astype(BF16)
        upto_ref[...] = _tri(lambda r, c: r <= c).astype(BF16)
        before_ref[...] = _tri(lambda r, c: r < c).astype(BF16)
        cls_ref[...] = jnp.zeros_like(cls_ref)
        cg_ref[...] = jnp.zeros_like(cg_ref)

        def tile(j, a):
            off = pl.multiple_of(j * TK, TK)
            zeros = jnp.zeros((ROWS, TK), BF16)

            def scores(hh):
                z_ref[hh] = _dot(qb_ref[:, _head(hh)], kt_ref[hh, j])
                dw_ref[hh] = _dot(dob_ref[:, _head(hh)], vt_ref[hh, j])

            def logs(hh):
                for r in range(chunks):
                    sl, mask = pl.ds(r * ROWS, ROWS), _chunk_mask(r, a)
                    if mask is False:
                        hi_ref[hh, sl, :] = zeros
                        lo_ref[hh, sl, :] = zeros
                        continue
                    ls, lsig = _sb_logs(z_ref[hh, sl, :], mask)
                    lsig_ref[hh, sl, :] = lsig
                    hi = ls.astype(BF16)
                    hi_ref[hh, sl, :] = hi
                    lo_ref[hh, sl, :] = (ls - hi.astype(F32)).astype(BF16)
                    sls_ref[hh, sl, :] = jnp.sum(ls, axis=1, keepdims=True)

            def upto(hh):
                pre_ref[hh] = _dot(hi_ref[hh], upto_ref[...]) + _dot(lo_ref[hh], upto_ref[...])

            def weights(hh):
                for r in range(chunks):
                    sl, mask = pl.ds(r * ROWS, ROWS), _chunk_mask(r, a)
                    if mask is False:
                        wb_ref[hh, sl, :] = zeros
                        hi_ref[hh, sl, :] = zeros
                        continue
                    w = jnp.exp(lsig_ref[hh, sl, :] + (lt_ref[hh, sl, :] - (cls_ref[hh, sl, :] + pre_ref[hh, sl, :])))
                    if mask is not None:
                        w = jnp.where(mask, w, 0.0)
                    wb_ref[hh, sl, :] = w.astype(BF16)
                    g = w * dw_ref[hh, sl, :]
                    g_ref[hh, sl, :] = g
                    hi_ref[hh, sl, :] = g.astype(BF16)
                    sg_ref[hh, sl, :] = jnp.sum(g, axis=1, keepdims=True)

            def earlier(hh):
                dw_ref[hh] = _dot(hi_ref[hh], before_ref[...])

            def logit_grads(hh):
                for r in range(chunks):
                    sl, mask = pl.ds(r * ROWS, ROWS), _chunk_mask(r, a)
                    if mask is False:
                        lo_ref[hh, sl, :] = zeros
                        continue
                    z = z_ref[hh, sl, :] * SCALE
                    e = jnp.exp(-jnp.abs(z))
                    rinv = 1.0 / (1.0 + e)
                    beta = jnp.where(z >= 0.0, rinv, e * rinv)
                    dz = g_ref[hh, sl, :] * (1.0 - beta) - beta * (cg_ref[hh, sl, :] + dw_ref[hh, sl, :])
                    if mask is not None:
                        dz = jnp.where(mask, dz, 0.0)
                    lo_ref[hh, sl, :] = (dz * SCALE).astype(BF16)
                    cls_ref[hh, sl, :] += sls_ref[hh, sl, :]
                    cg_ref[hh, sl, :] += sg_ref[hh, sl, :]

            def grads(hh):
                dq_acc[:, _head(hh)] += _dot(lo_ref[hh], k_ref[pl.ds(off, TK), _head(hh)].astype(BF16))
                dk_acc[hh, j] += _dot(qt_ref[hh], lo_ref[hh])
                dv_acc[hh, j] += _dot(dot_ref[hh], wb_ref[hh])

            _staggered([(scores,), (logs, upto), (weights, earlier), (logit_grads, grads)], hp)

        def step(j, carry):
            tile(j, None)
            return carry

        lax.fori_loop(0, i * per, step, 0)
        for a in range(per):
            tile(i * per + a, a)
        dq_ref[...] = dq_acc[...].astype(BF16)

        @pl.when(i == nq - 1)
        def _():
            for hh in range(hp):
                for t in range(nt):
                    dk_ref[pl.ds(t * TK, TK), _head(hh)] = dk_acc[hh, t].T.astype(BF16)
                    dv_ref[pl.ds(t * TK, TK), _head(hh)] = dv_acc[hh, t].T.astype(BF16)

    wd = hp * HEAD
    sq = lambda dt: pltpu.VMEM((hp, qb_rows, TK), dt)
    tiles = lambda dt: pltpu.VMEM((hp, nt, HEAD, TK), dt)
    blk = pl.BlockSpec((qb_rows, wd), lambda h, i: (i, h))
    col = pl.BlockSpec((s, wd), lambda h, i: (0, h))
    shp = jax.ShapeDtypeStruct((s, nh * HEAD), BF16)
    return _call(
        body, name="attn_a_bwd", grid=(ng, nq),
        in_specs=[blk,
                  pl.BlockSpec((s, wd), lambda h, i: (0, ng + h)),
                  pl.BlockSpec((s, wd), lambda h, i: (0, 2 * ng + h)),
                  pl.BlockSpec((hp, qb_rows, 1), lambda h, i: (h, i, 0)), blk],
        out_specs=[blk, col, col], out_shape=[shp] * 3,
        scratch=[pltpu.VMEM((qb_rows, wd), F32), tiles(F32), tiles(F32),
                 pltpu.VMEM((qb_rows, wd), BF16), pltpu.VMEM((qb_rows, wd), BF16),
                 pltpu.VMEM((TK, TK), BF16), pltpu.VMEM((TK, TK), BF16)]
        + [pltpu.VMEM((hp, qb_rows, 1), F32)] * 4 + [sq(F32)] * 5 + [sq(BF16)] * 3
        + [tiles(BF16), tiles(BF16), pltpu.VMEM((hp, HEAD, qb_rows), BF16), pltpu.VMEM((hp, HEAD, qb_rows), BF16)],
        sem=("parallel", "arbitrary"), args=(proj, proj, proj, lt, dya), comm=comm)


def _attn_b_bwd(proj, dyb, qg, kg, ext, nh, comm=None):
    s = proj.shape[0]
    nq = s // TQ

    def body(q_ref, k_ref, v_ref, do_ref, qg_ref, kg_ref, e_ref,
             dq_ref, dk_ref, dv_ref, dqg_ref, dkg_ref, de_ref,
             kn_pad, v_pad, bias_ref, db_acc, dkn_acc, dv_acc):
        i = pl.program_id(1)

        @pl.when(i == 0)
        def _():
            _band_fill(k_ref, v_ref, kg_ref, kn_pad, v_pad, s)
            _build_bias(e_ref, bias_ref)
            db_acc[...] = jnp.zeros_like(db_acc)
            dkn_acc[...] = jnp.zeros_like(dkn_acc)
            dv_acc[...] = jnp.zeros_like(dv_acc)
            dqg_ref[...] = jnp.zeros_like(dqg_ref)

        rq, qhat, qn, kw, off, pn = _band_probs(q_ref, qg_ref, kn_pad, bias_ref, i)
        dob = do_ref[...].astype(BF16)
        dp = _dot_nt(dob, v_pad[pl.ds(off, WIN), :])
        dsc = pn * (dp - jnp.sum(pn * dp, axis=1, keepdims=True))
        db_acc[:, :WIN] += dsc
        dsb = (dsc * SCALE).astype(BF16)
        dqn = _dot(dsb, kw)
        dkn_acc[pl.ds(off, WIN), :] += _dot_tn(dsb, qn)
        dv_acc[pl.ds(off, WIN), :] += _dot_tn(pn.astype(BF16), dob)
        dqh = dqn * qg_ref[...]
        dq_ref[...] = (rq * (dqh - qhat * jnp.mean(dqh * qhat, axis=1, keepdims=True))).astype(BF16)
        dqg_ref[...] += jnp.sum(dqn * qhat, axis=0, keepdims=True)

        @pl.when(i == nq - 1)
        def _():
            k = k_ref[...]
            rk = lax.rsqrt(jnp.mean(k * k, axis=1, keepdims=True) + NORM_EPS)
            khat = k * rk
            dkn = dkn_acc[pl.ds(PAD, s), :]
            dkh = dkn * kg_ref[...]
            dk_ref[...] = (rk * (dkh - khat * jnp.mean(dkh * khat, axis=1, keepdims=True))).astype(BF16)
            dkg_ref[...] = jnp.sum(dkn * khat, axis=0, keepdims=True)
            dv_ref[...] = dv_acc[pl.ds(PAD, s), :].astype(BF16)
            de_ref[...] = _reduce_bias_grad(db_acc)

    blk = pl.BlockSpec((TQ, HEAD), lambda h, i: (i, h))
    col = pl.BlockSpec((s, HEAD), lambda h, i: (0, h))
    vec = pl.BlockSpec((1, HEAD), lambda h, i: (0, 0))
    hvec = pl.BlockSpec((None, 1, HEAD), lambda h, i: (h, 0, 0))
    hext = pl.BlockSpec((None, 1, EXT), lambda h, i: (h, 0, 0))
    shp = jax.ShapeDtypeStruct((s, nh * HEAD), BF16)
    return _call(
        body, name="attn_b_bwd", grid=(nh, nq),
        in_specs=[pl.BlockSpec((TQ, HEAD), lambda h, i: (i, 4 * nh + h)),
                  pl.BlockSpec((s, HEAD), lambda h, i: (0, 5 * nh + h)),
                  pl.BlockSpec((s, HEAD), lambda h, i: (0, 6 * nh + h)),
                  blk, vec, vec, hext],
        out_specs=[blk, col, col, hvec, hvec, hext],
        out_shape=[shp] * 3 + [jax.ShapeDtypeStruct((nh, 1, HEAD), F32)] * 2
        + [jax.ShapeDtypeStruct((nh, 1, EXT), F32)],
        scratch=[pltpu.VMEM((s + PAD, HEAD), BF16), pltpu.VMEM((s + PAD, HEAD), BF16),
                 pltpu.VMEM((TQ, WIN), F32), pltpu.VMEM((TQ, EXT), F32),
                 pltpu.VMEM((s + PAD, HEAD), F32), pltpu.VMEM((s + PAD, HEAD), F32)],
        sem=("parallel", "arbitrary"), args=(proj, proj, proj, dyb, qg, kg, ext), comm=comm)


def _in_proj_bwd(dproj, wt, x, dxo, g, comm=None):
    s, d = x.shape
    tm, tk = min(s, 512), min(wt.shape[0], 1024)
    nk = wt.shape[0] // tk

    def body(dp_ref, w_ref, x_ref, dxo_ref, g_ref, dx_ref, dg_ref, acc):
        m, k = pl.program_id(0), pl.program_id(1)

        @pl.when(k == 0)
        def _():
            acc[...] = jnp.zeros_like(acc)

        @pl.when((k == 0) & (m == 0))
        def _():
            dg_ref[...] = jnp.zeros_like(dg_ref)

        acc[...] += _dot(dp_ref[...], w_ref[...])

        @pl.when(k == nk - 1)
        def _():
            xv = x_ref[...]
            r = lax.rsqrt(jnp.mean(xv * xv, axis=1, keepdims=True) + NORM_EPS)
            xhat = xv * r
            dh = acc[...]
            dxh = dh * g_ref[...]
            dx_ref[...] = dxo_ref[...] + r * (dxh - xhat * jnp.mean(dxh * xhat, axis=1, keepdims=True))
            dg_ref[...] += jnp.sum(dh * xhat, axis=0, keepdims=True)

    row = pl.BlockSpec((tm, d), lambda m, k: (m, 0))
    return _call(
        body, name="in_proj_bwd", grid=(s // tm, nk),
        in_specs=[pl.BlockSpec((tm, tk), lambda m, k: (m, k)),
                  pl.BlockSpec((tk, d), lambda m, k: (k, 0)),
                  row, row, pl.BlockSpec((1, d), lambda m, k: (0, 0))],
        out_specs=[row, pl.BlockSpec((8, d), lambda m, k: (0, 0))],
        out_shape=[jax.ShapeDtypeStruct((s, d), F32), jax.ShapeDtypeStruct((8, d), F32)],
        scratch=[pltpu.VMEM((tm, d), F32)], sem=("arbitrary", "arbitrary"), vmem_mb=56,
        args=(dproj, wt, x, dxo, g), comm=comm)


def _place():
    x, y, c = lax.axis_index("x"), lax.axis_index("y"), lax.axis_index("c")
    chips = [(1 - x, y), (x, 1 - y), (1 - x, 1 - y)]
    return x, y, c, chips


def _comm_call(body, name, ins, out_shape, n_remote, n_local, aliases=None):
    return pl.pallas_call(
        body, name=name, in_specs=[ANY] * len(ins), out_specs=[ANY] * len(out_shape), out_shape=out_shape,
        input_output_aliases=aliases or {},
        scratch_shapes=[pltpu.SemaphoreType.DMA((n_remote,)), pltpu.SemaphoreType.DMA((n_remote,)),
                        pltpu.SemaphoreType.DMA((n_local,))])(*ins)


def _rcopy(src, dst, send_sems, recv_sems, k, dev):
    return pltpu.make_async_remote_copy(src_ref=src, dst_ref=dst, send_sem=send_sems.at[k], recv_sem=recv_sems.at[k],
                                        device_id=dev, device_id_type=MESH)


def _run_comm(comm, name):
    n_ci, n_co = len(comm.ins), len(comm.outs)

    def body(*refs):
        cins, couts = refs[:n_ci], refs[n_ci:n_ci + n_co]
        send_sems, recv_sems = refs[n_ci + n_co:]
        comm.start(cins, couts, send_sems, recv_sems)
        comm.finish(cins, couts, send_sems, recv_sems)

    return pl.pallas_call(
        body, name=name, in_specs=[ANY] * n_ci, out_specs=[ANY] * n_co, out_shape=list(comm.outs),
        input_output_aliases=dict(comm.aliases),
        scratch_shapes=[pltpu.SemaphoreType.DMA((comm.n_sems,)), pltpu.SemaphoreType.DMA((comm.n_sems,))])(*comm.ins)


PIECES = ((0, 0, 2), (1, 0, 2), (2, 0, 1), (2, 1, 2))


def _gather_comm(fulls, rbp=None, stage="all", chips_at=(0, 1, 2, 3)):
    n = len(fulls)
    n_ici = len(PIECES) * n
    base = n_ici if stage == "all" else 0
    per_tensor = chips_at if isinstance(chips_at, list) else [chips_at] * n

    def piece(full, blk, core, p):
        quarter = full.shape[1] // 4
        return full.at[blk].at[pl.ds(core * 2 * quarter + PIECES[p][1] * quarter,
                                     (PIECES[p][2] - PIECES[p][1]) * quarter)]

    def ici(couts, send_sems, recv_sems, x, y, c, chips):
        b = 2 * x + y
        return [_rcopy(piece(full, b, c, p), piece(full, b, c, p), send_sems, recv_sems, len(PIECES) * t + p,
                       (*chips[PIECES[p][0]], c))
                for t, full in enumerate(couts[:n]) for p in range(len(PIECES)) if p in per_tensor[t]]

    def landed(couts, send_sems, recv_sems, x, y, c, chips, core, first, at=None):
        out = []
        for t, full in enumerate(couts[:n]):
            for p in range(len(PIECES)):
                if at is None or p in at[t]:
                    chip = chips[PIECES[p][0]]
                    got = piece(full, 2 * chip[0] + chip[1], core, p)
                    out.append(_rcopy(got, got, send_sems, recv_sems, first + len(PIECES) * t + p, (x, y, 1 - c)))
        return out

    def small(cins, couts, send_sems, recv_sems, x, y, c, chips):
        b = 2 * x + y
        return ([_rcopy(cins[n], couts[n].at[b], send_sems, recv_sems, 2 * n_ici + j, (*chip, c))
                 for j, chip in enumerate(chips)],
                pltpu.make_async_copy(cins[n], couts[n].at[b], send_sems.at[2 * n_ici + 3]))

    def start(cins, couts, send_sems, recv_sems):
        x, y, c, chips = _place()
        if stage == "sibling":
            for cp in landed(couts, send_sems, recv_sems, x, y, c, chips, c, base):
                cp.start()
            return
        for cp in ici(couts, send_sems, recv_sems, x, y, c, chips):
            cp.start()
        if rbp is not None:
            remote, local = small(cins, couts, send_sems, recv_sems, x, y, c, chips)
            for cp in remote:
                cp.start()
            local.start()

    def finish(cins, couts, send_sems, recv_sems):
        x, y, c, chips = _place()
        passed = landed(couts, send_sems, recv_sems, x, y, c, chips, c, base)
        if stage != "sibling":
            for k, cp in enumerate(landed(couts, send_sems, recv_sems, x, y, c, chips, c, 0, per_tensor)):
                cp.wait_recv()
                if stage == "all":
                    passed[k].start()
            for cp in ici(couts, send_sems, recv_sems, x, y, c, chips):
                cp.wait_send()
        if stage != "chips":
            for cp in landed(couts, send_sems, recv_sems, x, y, c, chips, 1 - c, base):
                cp.wait_recv()
            for cp in passed:
                cp.wait_send()
        if rbp is not None:
            remote, local = small(cins, couts, send_sems, recv_sems, x, y, c, chips)
            for j, chip in enumerate(chips):
                got = couts[n].at[2 * chip[0] + chip[1]]
                _rcopy(got, got, send_sems, recv_sems, 2 * n_ici + j, (x, y, c)).wait_recv()
            for cp in remote:
                cp.wait_send()
            local.wait()

    outs = [jax.ShapeDtypeStruct(f.shape, f.dtype) for f in fulls]
    ins = list(fulls)
    if rbp is not None:
        ins.append(rbp)
        outs.append(jax.ShapeDtypeStruct((4,) + rbp.shape, F32))
    return _Comm(tuple(ins), tuple(outs), {t: t for t in range(n)}, 2 * n_ici + 4, start, finish)


def _chips_comm(sums):
    n = len(sums)

    def copies(cins, couts, send_sems, recv_sems):
        x, y, c, chips = _place()
        return [_rcopy(cins[t].at[2 * chip[0] + chip[1]], couts[t].at[j], send_sems, recv_sems, 3 * t + j, (*chip, c))
                for t in range(n) for j, chip in enumerate(chips)]

    def start(*refs):
        for cp in copies(*refs):
            cp.start()

    def finish(*refs):
        for cp in copies(*refs):
            cp.wait()

    outs = tuple(jax.ShapeDtypeStruct((3,) + p.shape[1:], p.dtype) for p in sums)
    return _Comm(tuple(sums), outs, {}, 3 * n, start, finish)


def _pair_comm(ins, outs, aliases, copies):
    def start(*refs):
        for cp in copies(*refs):
            cp.start()

    def finish(*refs):
        for cp in copies(*refs):
            cp.wait()

    return _Comm(tuple(ins), tuple(outs), aliases, len(ins), start, finish)


def _sibling_comm(parts):
    def copies(cins, couts, send_sems, recv_sems):
        x, y, c, _ = _place()
        return [_rcopy(p.at[:, pl.ds((1 - c) * (p.shape[1] // 2), p.shape[1] // 2), :], couts[t],
                       send_sems, recv_sems, t, (x, y, 1 - c)) for t, p in enumerate(cins)]

    half = [jax.ShapeDtypeStruct((p.shape[0], p.shape[1] // 2, p.shape[2]), p.dtype) for p in parts]
    return _pair_comm(parts, half, {}, copies)


def _halves_comm(tots):
    def copies(cins, couts, send_sems, recv_sems):
        x, y, c, _ = _place()
        return [_rcopy(g.at[c], g.at[c], send_sems, recv_sems, t, (x, y, 1 - c)) for t, g in enumerate(couts)]

    return _pair_comm(tots, [jax.ShapeDtypeStruct(t.shape, t.dtype) for t in tots],
                      {t: t for t in range(len(tots))}, copies)


def _gather_small(packed):
    def body(p_ref, all_ref, send_sems, recv_sems, loc_sems):
        x, y, c, _ = _place()
        me = 4 * x + 2 * y + c
        local = pltpu.make_async_copy(p_ref, all_ref.at[me], loc_sems.at[0])
        local.start()
        sent = []
        for k in range(1, 8):
            px, py, pc = x ^ (k >> 2), y ^ ((k >> 1) & 1), c ^ (k & 1)
            cp = _rcopy(p_ref, all_ref.at[me], send_sems, recv_sems, k - 1, (px, py, pc))
            cp.start()
            sent.append(cp)
        for k in range(1, 8):
            px, py, pc = x ^ (k >> 2), y ^ ((k >> 1) & 1), c ^ (k & 1)
            got = all_ref.at[4 * px + 2 * py + pc]
            _rcopy(got, got, send_sems, recv_sems, k - 1, (x, y, c)).wait_recv()
        for cp in sent:
            cp.wait_send()
        local.wait()

    return _comm_call(body, "gather_small", [packed], [jax.ShapeDtypeStruct((8,) + packed.shape, F32)], 7, 1)[0]


def _sum_devices(allp):
    n, r, c = allp.shape

    def body(a_ref, o_ref):
        acc = a_ref[0]
        for k in range(1, n):
            acc = acc + a_ref[k]
        o_ref[...] = acc

    return pl.pallas_call(body, name="sum_devices", out_shape=jax.ShapeDtypeStruct((r, c), F32))(allp)


def _ext_index():
    u = np.arange(EXT)
    dist = np.where(u < WIN, PAD - u, PAD + EXT - u)
    return np.clip(dist, -(CHUNK - 1), REL_CLIP) + (CHUNK - 1)


def _pack(parts, rows):
    flat = jnp.concatenate([p.reshape(-1) for p in parts])
    return jnp.pad(flat, (0, rows * 128 - flat.shape[0])).reshape(rows, 128)


def _unpack(packed, shapes):
    flat, out, at = packed.reshape(-1), [], 0
    for shp in shapes:
        size = int(np.prod(shp))
        out.append(flat[at:at + size].reshape(shp))
        at += size
    return out


def kernel(x, norm_g, w_in, q_norm_g, k_norm_g, rel_bias, w_out, loss_target, m_norm_g, m_w_in, m_q_norm_g, m_k_norm_g, m_rel_bias, m_w_out, v_norm_g, v_w_in, v_q_norm_g, v_k_norm_g, v_rel_bias, v_w_out):
    nl, d, nb = w_in.shape
    s = x.shape[1]
    ds_ = d // 2
    nh = ds_ // HEAD
    rb = w_out.shape[1]
    nrel = rel_bias.shape[2]
    bx = lax.axis_index("x") * 2 + lax.axis_index("y")

    rb_rows = -(-(nl * nh * nrel) // 1024) * 8
    cx = lax.axis_index("c")
    wi_full = [_cast_block(w_in, l, bx, "cast_w_in") for l in range(nl)]
    wo_full = [_cast_block(w_out, l, bx, "cast_w_out") for l in range(nl)]
    wi_full[0], wo_full[0], rel_all = _run_comm(
        _gather_comm([wi_full[0], wo_full[0]], _pack([rel_bias], rb_rows)), "gather_first")
    rel_full = jnp.concatenate(
        [rel_all[j].reshape(-1)[:nl * nh * nrel].reshape(nl, nh, nrel) for j in range(4)], axis=2)
    ext_idx = _ext_index()
    onehot = jnp.asarray(ext_idx[:, None] == np.arange(N_REL)[None, :], F32)
    ext = jnp.einsum("lhr,ur->lhu", rel_full, onehot, precision=lax.Precision.HIGHEST).reshape(nl, nh, 1, EXT)

    xs, hs, projs, yas, lts, ybs, mixes, wi_t = [], [], [], [], [], [], [], []
    xc = x[0]
    for l in range(nl):
        h = _rmsnorm_fwd(xc, norm_g[l:l + 1])
        nxt = l + 1 < nl
        (proj, wt), got = _in_proj(h, wi_full[l], _gather_comm(
            [wo_full[l + 1], wi_full[l + 1]], stage="chips", chips_at=[(0, 1, 2, 3), (2,)]) if nxt else None)
        wi_t.append(wt)
        if nxt:
            wo_full[l + 1], wi_full[l + 1] = got
        (ya, lt), got = _attn_a_fwd(proj, nh, _gather_comm(
            [wi_full[l + 1]], stage="chips", chips_at=(0, 1)) if nxt else None)
        if nxt:
            wi_full[l + 1] = got[0]
        (yb,), got = _attn_b_fwd(proj, q_norm_g[l:l + 1], k_norm_g[l:l + 1], ext[l], nh,
                                 _gather_comm([wi_full[l + 1]], stage="chips", chips_at=(3,)) if nxt else None)
        if nxt:
            wi_full[l + 1] = got[0]
        xs.append(xc)
        wo_now = wo_full[l].reshape(4 * rb, d)
        (xc, mix), got = _out_proj(xc, ya, yb, proj, wo_now, _gather_comm(
            [wi_full[l + 1], wo_full[l + 1]], stage="sibling") if nxt else None)
        if nxt:
            wi_full[l + 1], wo_full[l + 1] = got
        hs.append(h); projs.append(proj); yas.append(ya); lts.append(lt); ybs.append(yb); mixes.append(mix)
    dx, loss_tile = _loss_head(xc, loss_target[0])

    small, g_wi, g_wo = [None] * nl, [None] * nl, [None] * nl
    pending = None

    def keep(lay, shared):
        g_wi[lay], g_wo[lay] = shared[0].reshape(d, nb), shared[1].reshape(rb, d)

    for l in reversed(range(nl)):
        wo = wo_full[l].reshape(4 * rb, d)
        p_wo = _wgrad(mixes[l], dx, 4, False, "wgrad_out")
        dya, dyb, dga, dgb = _out_proj_bwd(dx, yas[l], ybs[l], projs[l], wo)
        (dqa, dka, dva), got = _attn_a_bwd(projs[l], lts[l], dya, nh, _chips_comm(pending[1]) if pending else None)
        tots = [_add_chips(pending[1][t], got[t], bx, cx) for t in range(2)] if pending else None
        (dqb, dkb, dvb, dqg, dkg, dext), shared = _attn_b_bwd(
            projs[l], dyb, q_norm_g[l:l + 1], k_norm_g[l:l + 1], ext[l], nh, _halves_comm(tots) if pending else None)
        if pending:
            keep(pending[0], shared)
        dproj = jnp.concatenate([dqa, dka, dva, dga, dqb, dkb, dvb, dgb], axis=1)
        parts = [_wgrad(hs[l], dproj, 4, True, "wgrad_in"), p_wo]
        if l > 0:
            (dx, dng), theirs = _in_proj_bwd(dproj, wi_t[l],xs[l], dx, norm_g[l:l + 1], _sibling_comm(parts))
            pending = (l, [_add_sibling(parts[t], theirs[t], cx) for t in range(2)])
        else:
            theirs = _run_comm(_sibling_comm(parts), "reduce_sibling")
            sums = [_add_sibling(parts[t], theirs[t], cx) for t in range(2)]
            (dx, dng), got = _in_proj_bwd(dproj, wi_t[l],xs[l], dx, norm_g[l:l + 1], _chips_comm(sums))
            keep(0, _run_comm(_halves_comm([_add_chips(sums[t], got[t], bx, cx) for t in range(2)]), "share_halves"))
        small[l] = (dng[0], jnp.sum(dqg, axis=0).reshape(-1), jnp.sum(dkg, axis=0).reshape(-1), dext.reshape(nh, EXT))
    grad_x = dx[None]

    small_shapes = [(nl, d), (nl, HEAD), (nl, HEAD), (nl, nh, EXT), (1,)]
    small_parts = [jnp.stack([sm[i] for sm in small]) for i in range(4)] + [loss_tile[0, :1]]
    rows = -(-sum(int(np.prod(sh)) for sh in small_shapes) // 1024) * 8
    tot = _sum_devices(_gather_small(_pack(small_parts, rows)))
    g_ng, g_qg, g_kg, g_ext, loss = _unpack(tot, small_shapes)
    g_rel_full = jnp.einsum("lhu,ur->lhr", g_ext, onehot, precision=lax.Precision.HIGHEST)
    g_rel = lax.dynamic_slice_in_dim(g_rel_full, bx * nrel, nrel, axis=2)

    res_wi, res_wo = (), ()
    for l in range(nl):
        res_wi = _adamw_layer(l, w_in, g_wi[l], m_w_in, v_w_in, res_wi, "adamw_w_in")
        res_wo = _adamw_layer(l, w_out, g_wo[l], m_w_out, v_w_out, res_wo, "adamw_w_out")
    g_wi, d_wi, nm_wi, nv_wi = res_wi
    g_wo, d_wo, nm_wo, nv_wo = res_wo
    sm_shapes = [(nl, d), (nl, HEAD), (nl, HEAD), (nl, nh, nrel)]
    sm_rows = -(-sum(int(np.prod(sh)) for sh in sm_shapes) // 1024) * 8
    pw, pg, pm, pv = [_pack(group, sm_rows) for group in (
        (norm_g, q_norm_g, k_norm_g, rel_bias), (g_ng, g_qg, g_kg, g_rel),
        (m_norm_g, m_q_norm_g, m_k_norm_g, m_rel_bias), (v_norm_g, v_q_norm_g, v_k_norm_g, v_rel_bias))]
    d_sm, nm_sm, nv_sm = [_unpack(a[0], sm_shapes)
                          for a in _adamw_layer(0, pw[None], pg, pm[None], pv[None], (), "adamw_small")[1:]]

    return (loss[0], grad_x, g_ng, g_wi, g_qg, g_kg, g_rel, g_wo,
            d_sm[0], d_wi, d_sm[1], d_sm[2], d_sm[3], d_wo,
            nm_sm[0], nm_wi, nm_sm[1], nm_sm[2], nm_sm[3], nm_wo,
            nv_sm[0], nv_wi, nv_sm[1], nv_sm[2], nv_sm[3], nv_wo)
```

```python
from typing import Callable, NamedTuple

import jax
import jax.numpy as jnp
import numpy as np
from jax import lax
from jax.experimental import pallas as pl
from jax.experimental.pallas import tpu as pltpu

F32 = jnp.float32
BF16 = jnp.bfloat16

HEAD = 128
CHUNK = 64
LEFT_CHUNKS = 8
REL_CLIP = 256
N_REL = REL_CLIP + CHUNK
NORM_EPS = 1e-6
NEG_BIG = -1e30
TQ = 256
TK = TQ
QB = 512
ROWS = 32
PAD = LEFT_CHUNKS * CHUNK
WIN = PAD + TQ
EXT = 1024
SCALE = HEAD ** -0.5

ADAM_LR = 0.001
ADAM_B1 = 0.9
ADAM_B2 = 0.999
ADAM_EPS = 1e-08
ADAM_WD = 0.01
ADAM_STEP = 10

ANY = pl.BlockSpec(memory_space=pl.ANY)
MESH = pl.DeviceIdType.MESH


def _params(sem=None, vmem_mb=None):
    kw = {}
    if sem is not None:
        kw["dimension_semantics"] = sem
    if vmem_mb is not None:
        kw["vmem_limit_bytes"] = vmem_mb << 20
    return pltpu.CompilerParams(**kw)


class _Comm(NamedTuple):
    ins: tuple
    outs: tuple
    aliases: dict
    n_sems: int
    start: Callable
    finish: Callable


def _call(body, *, name, grid, in_specs, out_specs, out_shape, args, scratch=(), sem=None, vmem_mb=None, comm=None):
    if comm is None:
        out = pl.pallas_call(body, name=name, grid=grid, in_specs=in_specs, out_specs=out_specs, out_shape=out_shape,
                             scratch_shapes=list(scratch), compiler_params=_params(sem, vmem_mb))(*args)
        return out, ()
    n_in, n_out, n_ci, n_co = len(in_specs), len(out_shape), len(comm.ins), len(comm.outs)

    def hosted(*refs):
        ins, cins = refs[:n_in], refs[n_in:n_in + n_ci]
        outs, couts = refs[n_in + n_ci:n_in + n_ci + n_out], refs[n_in + n_ci + n_out:n_in + n_ci + n_out + n_co]
        rest = refs[n_in + n_ci + n_out + n_co:]
        send_sems, recv_sems = rest[-2:]
        first, last = None, None
        for ax, size in enumerate(grid):
            at = pl.program_id(ax)
            first = (at == 0) if first is None else first & (at == 0)
            last = (at == size - 1) if last is None else last & (at == size - 1)

        @pl.when(first)
        def _():
            comm.start(cins, couts, send_sems, recv_sems)

        body(*ins, *outs, *rest[:-2])

        @pl.when(last)
        def _():
            comm.finish(cins, couts, send_sems, recv_sems)

    out = pl.pallas_call(
        hosted, name=name, grid=grid, in_specs=list(in_specs) + [ANY] * n_ci, out_specs=list(out_specs) + [ANY] * n_co,
        out_shape=list(out_shape) + list(comm.outs),
        input_output_aliases={n_in + k: n_out + v for k, v in comm.aliases.items()},
        scratch_shapes=list(scratch) + [pltpu.SemaphoreType.DMA((comm.n_sems,)), pltpu.SemaphoreType.DMA((comm.n_sems,))],
        compiler_params=_params(("arbitrary",) * len(grid), vmem_mb))(*args, *comm.ins)
    return out[:n_out], out[n_out:]


def _dot(a, b):
    return jnp.dot(a, b, preferred_element_type=F32)


def _dot_nt(a, b):
    return lax.dot_general(a, b, (((1,), (1,)), ((), ())), preferred_element_type=F32)


def _dot_tn(a, b):
    return lax.dot_general(a, b, (((0,), (0,)), ((), ())), preferred_element_type=F32)


def _split_dot(x, m):
    hi = x.astype(BF16)
    lo = (x - hi.astype(F32)).astype(BF16)
    return _dot(hi, m) + _dot(lo, m)


def _silu_parts(g):
    sg = 1.0 / (1.0 + jnp.exp(-g))
    return g * sg, sg * (1.0 + g * (1.0 - sg))


def _idx(*vals):
    return jnp.stack([jnp.asarray(v, jnp.int32) for v in vals])


def _cast_block(w, l, blk, name):
    _, r, c = w.shape
    tr = min(r, 512)

    def body(b_ref, w_ref, o_ref):
        o_ref[...] = w_ref[...].astype(BF16)

    spec = pltpu.PrefetchScalarGridSpec(
        num_scalar_prefetch=1, grid=(r // tr,),
        in_specs=[pl.BlockSpec((None, tr, c), lambda i, b: (l, i, 0))],
        out_specs=pl.BlockSpec((None, tr, c), lambda i, b: (b[0], i, 0)))
    return pl.pallas_call(body, name=name, grid_spec=spec, out_shape=jax.ShapeDtypeStruct((4, r, c), BF16),
                          compiler_params=_params(("parallel",)))(_idx(blk), w)


def _add_sibling(p, theirs, core):
    nblk, r, c = p.shape
    hr = r // 2
    tr = min(hr, 256)
    per = hr // tr

    def body(c_ref, p_ref, t_ref, o_ref):
        o_ref[...] = (p_ref[...].astype(F32) + t_ref[...].astype(F32)).astype(BF16)

    blk = pl.BlockSpec((None, tr, c), lambda j, i, cr: (j, i, 0))
    spec = pltpu.PrefetchScalarGridSpec(
        num_scalar_prefetch=1, grid=(nblk, per),
        in_specs=[pl.BlockSpec((None, tr, c), lambda j, i, cr: (j, cr[0] * per + i, 0)), blk], out_specs=blk)
    return pl.pallas_call(body, name="add_sibling", grid_spec=spec, out_shape=jax.ShapeDtypeStruct((nblk, hr, c), BF16),
                          compiler_params=_params(("parallel", "parallel")))(_idx(core), p, theirs)


def _add_chips(sums, got, blk, core):
    _, hr, c = sums.shape
    tr = min(hr, 256)

    def body(i_ref, s_ref, g0_ref, g1_ref, g2_ref, o_ref):
        o_ref[...] = ((s_ref[...].astype(F32) + g0_ref[...].astype(F32))
                      + g1_ref[...].astype(F32)) + g2_ref[...].astype(F32)

    at = lambda j: pl.BlockSpec((None, tr, c), lambda i, ir: (j, i, 0))
    spec = pltpu.PrefetchScalarGridSpec(
        num_scalar_prefetch=1, grid=(hr // tr,),
        in_specs=[pl.BlockSpec((None, tr, c), lambda i, ir: (ir[0], i, 0)), at(0), at(1), at(2)],
        out_specs=pl.BlockSpec((None, tr, c), lambda i, ir: (ir[1], i, 0)))
    return pl.pallas_call(body, name="add_chips", grid_spec=spec, out_shape=jax.ShapeDtypeStruct((2, hr, c), F32),
                          compiler_params=_params(("parallel",)))(_idx(blk, core), sums, got, got, got)


def _adamw_layer(l, w, g, m, v, prev, name):
    nl, r, c = w.shape
    tr = min(r, 256)
    c1 = 1.0 / (1.0 - ADAM_B1 ** ADAM_STEP)
    c2 = 1.0 / (1.0 - ADAM_B2 ** ADAM_STEP)

    def body(w_ref, g_ref, m_ref, v_ref, *rest):
        go_ref, d_ref, nm_ref, nv_ref = rest[-4:]
        gg = g_ref[...]
        nm = ADAM_B1 * m_ref[...] + (1.0 - ADAM_B1) * gg
        nv = ADAM_B2 * v_ref[...] + (1.0 - ADAM_B2) * (gg * gg)
        upd = (nm * c1) / (jnp.sqrt(nv * c2) + ADAM_EPS) + ADAM_WD * w_ref[...]
        go_ref[...] = gg
        d_ref[...] = -ADAM_LR * upd
        nm_ref[...] = nm
        nv_ref[...] = nv

    lay = pl.BlockSpec((None, tr, c), lambda i: (l, i, 0))
    shp = jax.ShapeDtypeStruct((nl, r, c), F32)
    return pl.pallas_call(
        body, name=name, grid=(r // tr,),
        in_specs=[lay, pl.BlockSpec((tr, c), lambda i: (i, 0)), lay, lay] + [ANY] * len(prev),
        out_specs=[lay] * 4, out_shape=[shp] * 4, input_output_aliases={4 + k: k for k in range(len(prev))},
        compiler_params=_params(("parallel",), 40))(w, g, m, v, *prev)


def _rmsnorm_fwd(x, g):
    s, d = x.shape
    tm = min(s, 256)

    def body(x_ref, g_ref, h_ref):
        xv = x_ref[...]
        r = lax.rsqrt(jnp.mean(xv * xv, axis=1, keepdims=True) + NORM_EPS)
        h_ref[...] = (xv * r * g_ref[...]).astype(BF16)

    return pl.pallas_call(
        body, name="rmsnorm_fwd", grid=(s // tm,),
        in_specs=[pl.BlockSpec((tm, d), lambda i: (i, 0)), pl.BlockSpec((1, d), lambda i: (0, 0))],
        out_specs=pl.BlockSpec((tm, d), lambda i: (i, 0)),
        out_shape=jax.ShapeDtypeStruct((s, d), BF16),
        compiler_params=_params(("parallel",)))(x, g)


def _in_proj(h, w, comm=None):
    s, d = h.shape
    nblk, _, nb = w.shape
    tm, tn = min(s, 1024), min(nb, 1024)
    per = nb // tn

    def body(h_ref, w_ref, o_ref, wt_ref):
        o_ref[...] = _dot(h_ref[...], w_ref[...])

        @pl.when(pl.program_id(1) == 0)
        def _():
            for c in range(tn // HEAD):
                wt_ref[pl.ds(c * HEAD, HEAD), :] = w_ref[:, c * HEAD:(c + 1) * HEAD].astype(F32).T.astype(BF16)

    return _call(
        body, name="in_proj", grid=(nblk * per, s // tm),
        in_specs=[pl.BlockSpec((tm, d), lambda n, m: (m, 0)),
                  pl.BlockSpec((None, d, tn), lambda n, m: (n // per, 0, n % per))],
        out_specs=[pl.BlockSpec((tm, tn), lambda n, m: (m, n)), pl.BlockSpec((tn, d), lambda n, m: (n, 0))],
        out_shape=[jax.ShapeDtypeStruct((s, nblk * nb), F32), jax.ShapeDtypeStruct((nblk * nb, d), BF16)],
        sem=("parallel", "arbitrary"), vmem_mb=56, args=(h, w), comm=comm)


def _heads_per_step(nh):
    return 2 if nh % 2 == 0 else 1


def _head(hh):
    return slice(hh * HEAD, (hh + 1) * HEAD)


def _tri(op):
    r = lax.broadcasted_iota(jnp.int32, (TQ, TQ), 0)
    c = lax.broadcasted_iota(jnp.int32, (TQ, TQ), 1)
    return op(r, c)


def _staggered(groups, hp):
    for hh in range(hp):
        for fn in groups[0]:
            fn(hh)
    for group in groups[1:]:
        for hh in range(hp):
            for fn in group:
                fn(hh)


def _transpose_tiles(src_ref, dst_ref, hp, nt):
    for hh in range(hp):
        for t in range(nt):
            dst_ref[hh, t] = src_ref[pl.ds(t * TK, TK), _head(hh)].T.astype(BF16)


def _chunk_mask(r, a):
    if a is None or r * ROWS >= (a + 1) * TK:
        return None
    if (r + 1) * ROWS <= a * TK:
        return False
    row = lax.broadcasted_iota(jnp.int32, (ROWS, TK), 0) + r * ROWS
    return row > lax.broadcasted_iota(jnp.int32, (ROWS, TK), 1) + a * TK


def _sb_logs(qk, causal):
    z = qk * SCALE
    l1p = jnp.log(1.0 + jnp.exp(-jnp.abs(z)))
    ls = jnp.minimum(-z, 0.0) - l1p
    if causal is not None:
        ls = jnp.where(causal, ls, 0.0)
    return ls, jnp.minimum(z, 0.0) - l1p


def _attn_a_fwd(proj, nh, comm=None):
    s = proj.shape[0]
    qb_rows = min(QB, s)
    nq, nt, per = s // qb_rows, s // TK, qb_rows // TK
    hp = _heads_per_step(nh)
    ng = nh // hp
    chunks = qb_rows // ROWS

    def body(q_ref, k_ref, v_ref, o_ref, lt_ref, acc_ref, qb_ref, m_ref, car_ref, sum_ref,
             z_ref, lsig_ref, aft_ref, hi_ref, lo_ref, w_ref, kt_ref):
        i = pl.program_id(1)

        @pl.when(i == 0)
        def _():
            _transpose_tiles(k_ref, kt_ref, hp, nt)

        qb_ref[...] = q_ref[...].astype(BF16)
        m_ref[...] = _tri(lambda r, c: r > c).astype(BF16)
        acc_ref[...] = jnp.zeros_like(acc_ref)
        car_ref[...] = jnp.zeros_like(car_ref)

        def tile(j, a):
            off = pl.multiple_of(j * TK, TK)

            def scores(hh):
                z_ref[hh] = _dot(qb_ref[:, _head(hh)], kt_ref[hh, j])

            def logs(hh):
                for r in range(chunks):
                    sl, mask = pl.ds(r * ROWS, ROWS), _chunk_mask(r, a)
                    if mask is False:
                        hi_ref[hh, sl, :] = jnp.zeros((ROWS, TK), BF16)
                        lo_ref[hh, sl, :] = jnp.zeros((ROWS, TK), BF16)
                        continue
                    ls, lsig = _sb_logs(z_ref[hh, sl, :], mask)
                    lsig_ref[hh, sl, :] = lsig
                    hi = ls.astype(BF16)
                    hi_ref[hh, sl, :] = hi
                    lo_ref[hh, sl, :] = (ls - hi.astype(F32)).astype(BF16)
                    sum_ref[hh, sl, :] = jnp.sum(ls, axis=1, keepdims=True)

            def after(hh):
                aft_ref[hh] = _dot(hi_ref[hh], m_ref[...]) + _dot(lo_ref[hh], m_ref[...])

            def weights(hh):
                for r in range(chunks):
                    sl, mask = pl.ds(r * ROWS, ROWS), _chunk_mask(r, a)
                    if mask is False:
                        w_ref[hh, sl, :] = jnp.zeros((ROWS, TK), BF16)
                        continue
                    w = jnp.exp(lsig_ref[hh, sl, :] + aft_ref[hh, sl, :] + car_ref[hh, sl, :])
                    if mask is not None:
                        w = jnp.where(mask, w, 0.0)
                    w_ref[hh, sl, :] = w.astype(BF16)
                    car_ref[hh, sl, :] += sum_ref[hh, sl, :]

            def values(hh):
                vb = v_ref[pl.ds(off, TK), _head(hh)].astype(BF16)
                acc_ref[:, _head(hh)] += _dot(w_ref[hh], vb)

            _staggered([(scores,), (logs, after), (weights, values)], hp)

        for a in reversed(range(per)):
            tile(i * per + a, a)

        def step(t, carry):
            tile(i * per - 1 - t, None)
            return carry

        lax.fori_loop(0, i * per, step, 0)
        o_ref[...] = acc_ref[...]
        lt_ref[...] = car_ref[...]

    wd = hp * HEAD
    sq = lambda dt: pltpu.VMEM((hp, qb_rows, TK), dt)
    return _call(
        body, name="attn_a_fwd", grid=(ng, nq),
        in_specs=[pl.BlockSpec((qb_rows, wd), lambda h, i: (i, h)),
                  pl.BlockSpec((s, wd), lambda h, i: (0, ng + h)),
                  pl.BlockSpec((s, wd), lambda h, i: (0, 2 * ng + h))],
        out_specs=[pl.BlockSpec((qb_rows, wd), lambda h, i: (i, h)),
                   pl.BlockSpec((hp, qb_rows, 1), lambda h, i: (h, i, 0))],
        out_shape=[jax.ShapeDtypeStruct((s, nh * HEAD), F32), jax.ShapeDtypeStruct((nh, s, 1), F32)],
        scratch=[pltpu.VMEM((qb_rows, wd), F32), pltpu.VMEM((qb_rows, wd), BF16), pltpu.VMEM((TK, TK), BF16),
                 pltpu.VMEM((hp, qb_rows, 1), F32), pltpu.VMEM((hp, qb_rows, 1), F32),
                 sq(F32), sq(F32), sq(F32), sq(BF16), sq(BF16), sq(BF16),
                 pltpu.VMEM((hp, nt, HEAD, TK), BF16)],
        sem=("parallel", "arbitrary"), args=(proj, proj, proj), comm=comm)


def _band_valid(i):
    cl = lax.broadcasted_iota(jnp.int32, (TQ, WIN), 0) // CHUNK
    kl = lax.broadcasted_iota(jnp.int32, (TQ, WIN), 1) // CHUNK
    first = LEFT_CHUNKS - (TQ // CHUNK) * i
    return (kl >= cl) & (kl <= cl + LEFT_CHUNKS) & (kl >= first)


def _build_bias(e_ref, bias_ref):
    e8 = jnp.broadcast_to(e_ref[...], (8, EXT))
    row = lax.broadcasted_iota(jnp.int32, (8, EXT), 0)
    t8 = jnp.zeros((8, EXT), F32)
    for b in range(8):
        t8 = jnp.where(row == b, pltpu.roll(e8, b, 1) if b else e8, t8)
    for a in range(TQ // 8):
        sl = pltpu.roll(t8, 8 * a, 1) if a else t8
        bias_ref[pl.ds(8 * a, 8), :] = sl[:, :WIN]


def _reduce_bias_grad(db_ref):
    acc = jnp.zeros((8, EXT), F32)
    for a in range(TQ // 8):
        sl = db_ref[pl.ds(8 * a, 8), :]
        acc = acc + (pltpu.roll(sl, EXT - 8 * a, 1) if a else sl)
    row = lax.broadcasted_iota(jnp.int32, (8, EXT), 0)
    tot = jnp.zeros((8, EXT), F32)
    for b in range(8):
        tot = tot + jnp.where(row == b, pltpu.roll(acc, EXT - b, 1) if b else acc, 0.0)
    return jnp.sum(tot, axis=0, keepdims=True)


def _band_fill(k_ref, v_ref, kg_ref, kn_pad, v_pad, s):
    k = k_ref[...]
    rk = lax.rsqrt(jnp.mean(k * k, axis=1, keepdims=True) + NORM_EPS)
    kn_pad[pl.ds(0, PAD), :] = jnp.zeros((PAD, HEAD), BF16)
    kn_pad[pl.ds(PAD, s), :] = (k * rk * kg_ref[...]).astype(BF16)
    v_pad[pl.ds(0, PAD), :] = jnp.zeros((PAD, HEAD), BF16)
    v_pad[pl.ds(PAD, s), :] = v_ref[...].astype(BF16)


def _band_probs(q_ref, qg_ref, kn_pad, bias_ref, i):
    q = q_ref[...]
    rq = lax.rsqrt(jnp.mean(q * q, axis=1, keepdims=True) + NORM_EPS)
    qhat = q * rq
    qn = (qhat * qg_ref[...]).astype(BF16)
    off = pl.multiple_of(i * TQ, TQ)
    kw = kn_pad[pl.ds(off, WIN), :]
    sc = _dot_nt(qn, kw) * SCALE + bias_ref[...]
    sc = jnp.where(_band_valid(i), sc, NEG_BIG)
    p = jnp.exp(sc - jnp.max(sc, axis=1, keepdims=True))
    pn = p / jnp.sum(p, axis=1, keepdims=True)
    return rq, qhat, qn, kw, off, pn


def _attn_b_fwd(proj, qg, kg, ext, nh, comm=None):
    s = proj.shape[0]
    nq = s // TQ

    def body(q_ref, k_ref, v_ref, qg_ref, kg_ref, e_ref, o_ref, kn_pad, v_pad, bias_ref):
        i = pl.program_id(1)

        @pl.when(i == 0)
        def _():
            _band_fill(k_ref, v_ref, kg_ref, kn_pad, v_pad, s)
            _build_bias(e_ref, bias_ref)

        _, _, _, _, off, pn = _band_probs(q_ref, qg_ref, kn_pad, bias_ref, i)
        o_ref[...] = _dot(pn.astype(BF16), v_pad[pl.ds(off, WIN), :])

    vec = pl.BlockSpec((1, HEAD), lambda h, i: (0, 0))
    return _call(
        body, name="attn_b_fwd", grid=(nh, nq),
        in_specs=[pl.BlockSpec((TQ, HEAD), lambda h, i: (i, 4 * nh + h)),
                  pl.BlockSpec((s, HEAD), lambda h, i: (0, 5 * nh + h)),
                  pl.BlockSpec((s, HEAD), lambda h, i: (0, 6 * nh + h)),
                  vec, vec,
                  pl.BlockSpec((None, 1, EXT), lambda h, i: (h, 0, 0))],
        out_specs=[pl.BlockSpec((TQ, HEAD), lambda h, i: (i, h))],
        out_shape=[jax.ShapeDtypeStruct((s, nh * HEAD), F32)],
        scratch=[pltpu.VMEM((s + PAD, HEAD), BF16), pltpu.VMEM((s + PAD, HEAD), BF16), pltpu.VMEM((TQ, WIN), F32)],
        sem=("parallel", "arbitrary"), args=(proj, proj, proj, qg, kg, ext), comm=comm)


def _out_proj(x, ya, yb, proj, w, comm=None):
    s, d = x.shape
    ds_ = ya.shape[1]
    tm = min(s, 256)

    def body(x_ref, ya_ref, yb_ref, ga_ref, gb_ref, w_ref, o_ref, mix_ref):
        ma = (ya_ref[...] * _silu_parts(ga_ref[...])[0]).astype(BF16)
        mb = (yb_ref[...] * _silu_parts(gb_ref[...])[0]).astype(BF16)
        mix_ref[:, :ds_] = ma
        mix_ref[:, ds_:] = mb
        o_ref[...] = x_ref[...] + _dot(ma, w_ref[pl.ds(0, ds_), :]) + _dot(mb, w_ref[pl.ds(ds_, ds_), :])

    row = lambda width: pl.BlockSpec((tm, width), lambda i: (i, 0))
    return _call(
        body, name="out_proj", grid=(s // tm,),
        in_specs=[row(d), row(ds_), row(ds_),
                  pl.BlockSpec((tm, ds_), lambda i: (i, 3)), pl.BlockSpec((tm, ds_), lambda i: (i, 7)),
                  pl.BlockSpec((2 * ds_, d), lambda i: (0, 0))],
        out_specs=[row(d), row(2 * ds_)],
        out_shape=[jax.ShapeDtypeStruct((s, d), F32), jax.ShapeDtypeStruct((s, 2 * ds_), BF16)],
        sem=("parallel",), vmem_mb=48, args=(x, ya, yb, proj, proj, w), comm=comm)


def _loss_head(y, tgt):
    s, d = y.shape
    tm = min(s, 256)

    def body(y_ref, t_ref, dy_ref, l_ref):
        @pl.when(pl.program_id(0) == 0)
        def _():
            l_ref[...] = jnp.zeros_like(l_ref)

        err = y_ref[...] - t_ref[...]
        dy_ref[...] = err * (1.0 / d)
        l_ref[...] += 0.5 * jnp.sum(jnp.mean(err * err, axis=1, keepdims=True), axis=0, keepdims=True)

    row = pl.BlockSpec((tm, d), lambda i: (i, 0))
    return pl.pallas_call(
        body, name="loss_head", grid=(s // tm,), in_specs=[row, row],
        out_specs=[row, pl.BlockSpec((8, 128), lambda i: (0, 0))],
        out_shape=[jax.ShapeDtypeStruct((s, d), F32), jax.ShapeDtypeStruct((8, 128), F32)],
        compiler_params=_params(("arbitrary",)))(y, tgt)


def _out_proj_bwd(dxo, ya, yb, proj, w, comm=None):
    s, d = dxo.shape
    ds_ = ya.shape[1]
    tm = min(s, 256)

    def body(dx_ref, ya_ref, yb_ref, ga_ref, gb_ref, w_ref, dya_ref, dyb_ref, dga_ref, dgb_ref):
        dxb = dx_ref[...].astype(BF16)
        for y_ref, g_ref, lo, dy_ref, dg_ref in ((ya_ref, ga_ref, 0, dya_ref, dga_ref),
                                                 (yb_ref, gb_ref, ds_, dyb_ref, dgb_ref)):
            dmix = _dot_nt(dxb, w_ref[pl.ds(lo, ds_), :])
            act, dact = _silu_parts(g_ref[...])
            dy_ref[...] = dmix * act
            dg_ref[...] = (dmix * y_ref[...] * dact).astype(BF16)

    row = lambda width: pl.BlockSpec((tm, width), lambda i: (i, 0))
    return _call(
        body, name="out_proj_bwd", grid=(s // tm,),
        in_specs=[row(d), row(ds_), row(ds_),
                  pl.BlockSpec((tm, ds_), lambda i: (i, 3)), pl.BlockSpec((tm, ds_), lambda i: (i, 7)),
                  pl.BlockSpec((2 * ds_, d), lambda i: (0, 0))],
        out_specs=[row(ds_)] * 4,
        out_shape=[jax.ShapeDtypeStruct((s, ds_), F32)] * 2 + [jax.ShapeDtypeStruct((s, ds_), BF16)] * 2,
        sem=("parallel",), vmem_mb=48, args=(dxo, ya, yb, proj, proj, w), comm=comm)


def _wgrad(a, b, nblk, col_blocks, name):
    s, m = a.shape
    n = b.shape[1]
    if col_blocks:
        tr = min(m, 1024)
        nb = n // nblk
        tn = min(nb, 2048)
        per = nb // tn
        out_shape = (nblk, m, nb)
        out_spec = pl.BlockSpec((None, tr, tn), lambda j, r: (j // per, r, j % per))
    else:
        tn = min(n, 1024)
        tr = m // nblk
        out_shape = (nblk, tr, n)
        out_spec = pl.BlockSpec((None, tr, tn), lambda j, r: (r, 0, j))

    def body(a_ref, b_ref, o_ref):
        o_ref[...] = _dot_tn(a_ref[...].astype(BF16), b_ref[...].astype(BF16)).astype(BF16)

    return pl.pallas_call(
        body, name=name, grid=(n // tn, m // tr),
        in_specs=[pl.BlockSpec((s, tr), lambda j, r: (0, r)), pl.BlockSpec((s, tn), lambda j, r: (0, j))],
        out_specs=out_spec, out_shape=jax.ShapeDtypeStruct(out_shape, BF16),
        compiler_params=_params(("parallel", "parallel"), 48))(a, b)


def _attn_a_bwd(proj, lt, dya, nh, comm=None):
    s = proj.shape[0]
    qb_rows = min(TK, s)
    nq, nt, per = s // qb_rows, s // TK, qb_rows // TK
    hp = _heads_per_step(nh)
    ng = nh // hp
    chunks = qb_rows // ROWS

    def body(q_ref, k_ref, v_ref, lt_ref, do_ref, dq_ref, dk_ref, dv_ref, dq_acc, dk_acc, dv_acc,
             qb_ref, dob_ref, upto_ref, before_ref, cls_ref, cg_ref, sls_ref, sg_ref,
             z_ref, dw_ref, lsig_ref, pre_ref, g_ref, hi_ref, lo_ref, wb_ref, kt_ref, vt_ref, qt_ref, dot_ref):
        i = pl.program_id(1)

        @pl.when(i == 0)
        def _():
            dk_acc[...] = jnp.zeros_like(dk_acc)
            dv_acc[...] = jnp.zeros_like(dv_acc)
            _transpose_tiles(k_ref, kt_ref, hp, nt)
            _transpose_tiles(v_ref, vt_ref, hp, nt)

        dq_acc[...] = jnp.zeros_like(dq_acc)
        qb_ref[...] = q_ref[...].astype(BF16)
        dob_ref[...] = do_ref[...].astype(BF16)
        for hh in range(hp):
            qt_ref[hh] = q_ref[:, _head(hh)].T.astype(BF16)
            dot_ref[hh] = do_ref[:, _head(hh)].T.astype(BF16)
        upto_ref[...] = _tri(lambda r, c: r <= c).astype(BF16)
        before_ref[...] = _tri(lambda r, c: r < c).astype(BF16)
        cls_ref[...] = jnp.zeros_like(cls_ref)
        cg_ref[...] = jnp.zeros_like(cg_ref)

        def tile(j, a):
            off = pl.multiple_of(j * TK, TK)
            zeros = jnp.zeros((ROWS, TK), BF16)

            def scores(hh):
                z_ref[hh] = _dot(qb_ref[:, _head(hh)], kt_ref[hh, j])
                dw_ref[hh] = _dot(dob_ref[:, _head(hh)], vt_ref[hh, j])

            def logs(hh):
                for r in range(chunks):
                    sl, mask = pl.ds(r * ROWS, ROWS), _chunk_mask(r, a)
                    if mask is False:
                        hi_ref[hh, sl, :] = zeros
                        lo_ref[hh, sl, :] = zeros
                        continue
                    ls, lsig = _sb_logs(z_ref[hh, sl, :], mask)
                    lsig_ref[hh, sl, :] = lsig
                    hi = ls.astype(BF16)
                    hi_ref[hh, sl, :] = hi
                    lo_ref[hh, sl, :] = (ls - hi.astype(F32)).astype(BF16)
                    sls_ref[hh, sl, :] = jnp.sum(ls, axis=1, keepdims=True)

            def upto(hh):
                pre_ref[hh] = _dot(hi_ref[hh], upto_ref[...]) + _dot(lo_ref[hh], upto_ref[...])

            def weights(hh):
                for r in range(chunks):
                    sl, mask = pl.ds(r * ROWS, ROWS), _chunk_mask(r, a)
                    if mask is False:
                        wb_ref[hh, sl, :] = zeros
                        hi_ref[hh, sl, :] = zeros
                        continue
                    w = jnp.exp(lsig_ref[hh, sl, :] + (lt_ref[hh, sl, :] - (cls_ref[hh, sl, :] + pre_ref[hh, sl, :])))
                    if mask is not None:
                        w = jnp.where(mask, w, 0.0)
                    wb_ref[hh, sl, :] = w.astype(BF16)
                    g = w * dw_ref[hh, sl, :]
                    g_ref[hh, sl, :] = g
                    hi_ref[hh, sl, :] = g.astype(BF16)
                    sg_ref[hh, sl, :] = jnp.sum(g, axis=1, keepdims=True)

            def earlier(hh):
                dw_ref[hh] = _dot(hi_ref[hh], before_ref[...])

            def logit_grads(hh):
                for r in range(chunks):
                    sl, mask = pl.ds(r * ROWS, ROWS), _chunk_mask(r, a)
                    if mask is False:
                        lo_ref[hh, sl, :] = zeros
                        continue
                    z = z_ref[hh, sl, :] * SCALE
                    e = jnp.exp(-jnp.abs(z))
                    rinv = 1.0 / (1.0 + e)
                    beta = jnp.where(z >= 0.0, rinv, e * rinv)
                    dz = g_ref[hh, sl, :] * (1.0 - beta) - beta * (cg_ref[hh, sl, :] + dw_ref[hh, sl, :])
                    if mask is not None:
                        dz = jnp.where(mask, dz, 0.0)
                    lo_ref[hh, sl, :] = (dz * SCALE).astype(BF16)
                    cls_ref[hh, sl, :] += sls_ref[hh, sl, :]
                    cg_ref[hh, sl, :] += sg_ref[hh, sl, :]

            def grads(hh):
                dq_acc[:, _head(hh)] += _dot(lo_ref[hh], k_ref[pl.ds(off, TK), _head(hh)].astype(BF16))
                dk_acc[hh, j] += _dot(qt_ref[hh], lo_ref[hh])
                dv_acc[hh, j] += _dot(dot_ref[hh], wb_ref[hh])

            _staggered([(scores,), (logs, upto), (weights, earlier), (logit_grads, grads)], hp)

        def step(j, carry):
            tile(j, None)
            return carry

        lax.fori_loop(0, i * per, step, 0)
        for a in range(per):
            tile(i * per + a, a)
        dq_ref[...] = dq_acc[...].astype(BF16)

        @pl.when(i == nq - 1)
        def _():
            for hh in range(hp):
                for t in range(nt):
                    dk_ref[pl.ds(t * TK, TK), _head(hh)] = dk_acc[hh, t].T.astype(BF16)
                    dv_ref[pl.ds(t * TK, TK), _head(hh)] = dv_acc[hh, t].T.astype(BF16)

    wd = hp * HEAD
    sq = lambda dt: pltpu.VMEM((hp, qb_rows, TK), dt)
    tiles = lambda dt: pltpu.VMEM((hp, nt, HEAD, TK), dt)
    blk = pl.BlockSpec((qb_rows, wd), lambda h, i: (i, h))
    col = pl.BlockSpec((s, wd), lambda h, i: (0, h))
    shp = jax.ShapeDtypeStruct((s, nh * HEAD), BF16)
    return _call(
        body, name="attn_a_bwd", grid=(ng, nq),
        in_specs=[blk,
                  pl.BlockSpec((s, wd), lambda h, i: (0, ng + h)),
                  pl.BlockSpec((s, wd), lambda h, i: (0, 2 * ng + h)),
                  pl.BlockSpec((hp, qb_rows, 1), lambda h, i: (h, i, 0)), blk],
        out_specs=[blk, col, col], out_shape=[shp] * 3,
        scratch=[pltpu.VMEM((qb_rows, wd), F32), tiles(F32), tiles(F32),
                 pltpu.VMEM((qb_rows, wd), BF16), pltpu.VMEM((qb_rows, wd), BF16),
                 pltpu.VMEM((TK, TK), BF16), pltpu.VMEM((TK, TK), BF16)]
        + [pltpu.VMEM((hp, qb_rows, 1), F32)] * 4 + [sq(F32)] * 5 + [sq(BF16)] * 3
        + [tiles(BF16), tiles(BF16), pltpu.VMEM((hp, HEAD, qb_rows), BF16), pltpu.VMEM((hp, HEAD, qb_rows), BF16)],
        sem=("parallel", "arbitrary"), args=(proj, proj, proj, lt, dya), comm=comm)


def _attn_b_bwd(proj, dyb, qg, kg, ext, nh, comm=None):
    s = proj.shape[0]
    nq = s // TQ

    def body(q_ref, k_ref, v_ref, do_ref, qg_ref, kg_ref, e_ref,
             dq_ref, dk_ref, dv_ref, dqg_ref, dkg_ref, de_ref,
             kn_pad, v_pad, bias_ref, db_acc, dkn_acc, dv_acc):
        i = pl.program_id(1)

        @pl.when(i == 0)
        def _():
            _band_fill(k_ref, v_ref, kg_ref, kn_pad, v_pad, s)
            _build_bias(e_ref, bias_ref)
            db_acc[...] = jnp.zeros_like(db_acc)
            dkn_acc[...] = jnp.zeros_like(dkn_acc)
            dv_acc[...] = jnp.zeros_like(dv_acc)
            dqg_ref[...] = jnp.zeros_like(dqg_ref)

        rq, qhat, qn, kw, off, pn = _band_probs(q_ref, qg_ref, kn_pad, bias_ref, i)
        dob = do_ref[...].astype(BF16)
        dp = _dot_nt(dob, v_pad[pl.ds(off, WIN), :])
        dsc = pn * (dp - jnp.sum(pn * dp, axis=1, keepdims=True))
        db_acc[:, :WIN] += dsc
        dsb = (dsc * SCALE).astype(BF16)
        dqn = _dot(dsb, kw)
        dkn_acc[pl.ds(off, WIN), :] += _dot_tn(dsb, qn)
        dv_acc[pl.ds(off, WIN), :] += _dot_tn(pn.astype(BF16), dob)
        dqh = dqn * qg_ref[...]
        dq_ref[...] = (rq * (dqh - qhat * jnp.mean(dqh * qhat, axis=1, keepdims=True))).astype(BF16)
        dqg_ref[...] += jnp.sum(dqn * qhat, axis=0, keepdims=True)

        @pl.when(i == nq - 1)
        def _():
            k = k_ref[...]
            rk = lax.rsqrt(jnp.mean(k * k, axis=1, keepdims=True) + NORM_EPS)
            khat = k * rk
            dkn = dkn_acc[pl.ds(PAD, s), :]
            dkh = dkn * kg_ref[...]
            dk_ref[...] = (rk * (dkh - khat * jnp.mean(dkh * khat, axis=1, keepdims=True))).astype(BF16)
            dkg_ref[...] = jnp.sum(dkn * khat, axis=0, keepdims=True)
            dv_ref[...] = dv_acc[pl.ds(PAD, s), :].astype(BF16)
            de_ref[...] = _reduce_bias_grad(db_acc)

    blk = pl.BlockSpec((TQ, HEAD), lambda h, i: (i, h))
    col = pl.BlockSpec((s, HEAD), lambda h, i: (0, h))
    vec = pl.BlockSpec((1, HEAD), lambda h, i: (0, 0))
    hvec = pl.BlockSpec((None, 1, HEAD), lambda h, i: (h, 0, 0))
    hext = pl.BlockSpec((None, 1, EXT), lambda h, i: (h, 0, 0))
    shp = jax.ShapeDtypeStruct((s, nh * HEAD), BF16)
    return _call(
        body, name="attn_b_bwd", grid=(nh, nq),
        in_specs=[pl.BlockSpec((TQ, HEAD), lambda h, i: (i, 4 * nh + h)),
                  pl.BlockSpec((s, HEAD), lambda h, i: (0, 5 * nh + h)),
                  pl.BlockSpec((s, HEAD), lambda h, i: (0, 6 * nh + h)),
                  blk, vec, vec, hext],
        out_specs=[blk, col, col, hvec, hvec, hext],
        out_shape=[shp] * 3 + [jax.ShapeDtypeStruct((nh, 1, HEAD), F32)] * 2
        + [jax.ShapeDtypeStruct((nh, 1, EXT), F32)],
        scratch=[pltpu.VMEM((s + PAD, HEAD), BF16), pltpu.VMEM((s + PAD, HEAD), BF16),
                 pltpu.VMEM((TQ, WIN), F32), pltpu.VMEM((TQ, EXT), F32),
                 pltpu.VMEM((s + PAD, HEAD), F32), pltpu.VMEM((s + PAD, HEAD), F32)],
        sem=("parallel", "arbitrary"), args=(proj, proj, proj, dyb, qg, kg, ext), comm=comm)


def _in_proj_bwd(dproj, wt, x, dxo, g, comm=None):
    s, d = x.shape
    tm, tk = min(s, 512), min(wt.shape[0], 1024)
    nk = wt.shape[0] // tk

    def body(dp_ref, w_ref, x_ref, dxo_ref, g_ref, dx_ref, dg_ref, acc):
        m, k = pl.program_id(0), pl.program_id(1)

        @pl.when(k == 0)
        def _():
            acc[...] = jnp.zeros_like(acc)

        @pl.when((k == 0) & (m == 0))
        def _():
            dg_ref[...] = jnp.zeros_like(dg_ref)

        acc[...] += _dot(dp_ref[...], w_ref[...])

        @pl.when(k == nk - 1)
        def _():
            xv = x_ref[...]
            r = lax.rsqrt(jnp.mean(xv * xv, axis=1, keepdims=True) + NORM_EPS)
            xhat = xv * r
            dh = acc[...]
            dxh = dh * g_ref[...]
            dx_ref[...] = dxo_ref[...] + r * (dxh - xhat * jnp.mean(dxh * xhat, axis=1, keepdims=True))
            dg_ref[...] += jnp.sum(dh * xhat, axis=0, keepdims=True)

    row = pl.BlockSpec((tm, d), lambda m, k: (m, 0))
    return _call(
        body, name="in_proj_bwd", grid=(s // tm, nk),
        in_specs=[pl.BlockSpec((tm, tk), lambda m, k: (m, k)),
                  pl.BlockSpec((tk, d), lambda m, k: (k, 0)),
                  row, row, pl.BlockSpec((1, d), lambda m, k: (0, 0))],
        out_specs=[row, pl.BlockSpec((8, d), lambda m, k: (0, 0))],
        out_shape=[jax.ShapeDtypeStruct((s, d), F32), jax.ShapeDtypeStruct((8, d), F32)],
        scratch=[pltpu.VMEM((tm, d), F32)], sem=("arbitrary", "arbitrary"), vmem_mb=56,
        args=(dproj, wt, x, dxo, g), comm=comm)


def _place():
    x, y, c = lax.axis_index("x"), lax.axis_index("y"), lax.axis_index("c")
    chips = [(1 - x, y), (x, 1 - y), (1 - x, 1 - y)]
    return x, y, c, chips


def _comm_call(body, name, ins, out_shape, n_remote, n_local, aliases=None):
    return pl.pallas_call(
        body, name=name, in_specs=[ANY] * len(ins), out_specs=[ANY] * len(out_shape), out_shape=out_shape,
        input_output_aliases=aliases or {},
        scratch_shapes=[pltpu.SemaphoreType.DMA((n_remote,)), pltpu.SemaphoreType.DMA((n_remote,)),
                        pltpu.SemaphoreType.DMA((n_local,))])(*ins)


def _rcopy(src, dst, send_sems, recv_sems, k, dev):
    return pltpu.make_async_remote_copy(src_ref=src, dst_ref=dst, send_sem=send_sems.at[k], recv_sem=recv_sems.at[k],
                                        device_id=dev, device_id_type=MESH)


def _run_comm(comm, name):
    n_ci, n_co = len(comm.ins), len(comm.outs)

    def body(*refs):
        cins, couts = refs[:n_ci], refs[n_ci:n_ci + n_co]
        send_sems, recv_sems = refs[n_ci + n_co:]
        comm.start(cins, couts, send_sems, recv_sems)
        comm.finish(cins, couts, send_sems, recv_sems)

    return pl.pallas_call(
        body, name=name, in_specs=[ANY] * n_ci, out_specs=[ANY] * n_co, out_shape=list(comm.outs),
        input_output_aliases=dict(comm.aliases),
        scratch_shapes=[pltpu.SemaphoreType.DMA((comm.n_sems,)), pltpu.SemaphoreType.DMA((comm.n_sems,))])(*comm.ins)


PIECES = ((0, 0, 2), (1, 0, 2), (2, 0, 1), (2, 1, 2))


def _gather_comm(fulls, rbp=None, stage="all", chips_at=(0, 1, 2, 3)):
    n = len(fulls)
    n_ici = len(PIECES) * n
    base = n_ici if stage == "all" else 0
    per_tensor = chips_at if isinstance(chips_at, list) else [chips_at] * n

    def piece(full, blk, core, p):
        quarter = full.shape[1] // 4
        return full.at[blk].at[pl.ds(core * 2 * quarter + PIECES[p][1] * quarter,
                                     (PIECES[p][2] - PIECES[p][1]) * quarter)]

    def ici(couts, send_sems, recv_sems, x, y, c, chips):
        b = 2 * x + y
        return [_rcopy(piece(full, b, c, p), piece(full, b, c, p), send_sems, recv_sems, len(PIECES) * t + p,
                       (*chips[PIECES[p][0]], c))
                for t, full in enumerate(couts[:n]) for p in range(len(PIECES)) if p in per_tensor[t]]

    def landed(couts, send_sems, recv_sems, x, y, c, chips, core, first, at=None):
        out = []
        for t, full in enumerate(couts[:n]):
            for p in range(len(PIECES)):
                if at is None or p in at[t]:
                    chip = chips[PIECES[p][0]]
                    got = piece(full, 2 * chip[0] + chip[1], core, p)
                    out.append(_rcopy(got, got, send_sems, recv_sems, first + len(PIECES) * t + p, (x, y, 1 - c)))
        return out

    def small(cins, couts, send_sems, recv_sems, x, y, c, chips):
        b = 2 * x + y
        return ([_rcopy(cins[n], couts[n].at[b], send_sems, recv_sems, 2 * n_ici + j, (*chip, c))
                 for j, chip in enumerate(chips)],
                pltpu.make_async_copy(cins[n], couts[n].at[b], send_sems.at[2 * n_ici + 3]))

    def start(cins, couts, send_sems, recv_sems):
        x, y, c, chips = _place()
        if stage == "sibling":
            for cp in landed(couts, send_sems, recv_sems, x, y, c, chips, c, base):
                cp.start()
            return
        for cp in ici(couts, send_sems, recv_sems, x, y, c, chips):
            cp.start()
        if rbp is not None:
            remote, local = small(cins, couts, send_sems, recv_sems, x, y, c, chips)
            for cp in remote:
                cp.start()
            local.start()

    def finish(cins, couts, send_sems, recv_sems):
        x, y, c, chips = _place()
        passed = landed(couts, send_sems, recv_sems, x, y, c, chips, c, base)
        if stage != "sibling":
            for k, cp in enumerate(landed(couts, send_sems, recv_sems, x, y, c, chips, c, 0, per_tensor)):
                cp.wait_recv()
                if stage == "all":
                    passed[k].start()
            for cp in ici(couts, send_sems, recv_sems, x, y, c, chips):
                cp.wait_send()
        if stage != "chips":
            for cp in landed(couts, send_sems, recv_sems, x, y, c, chips, 1 - c, base):
                cp.wait_recv()
            for cp in passed:
                cp.wait_send()
        if rbp is not None:
            remote, local = small(cins, couts, send_sems, recv_sems, x, y, c, chips)
            for j, chip in enumerate(chips):
                got = couts[n].at[2 * chip[0] + chip[1]]
                _rcopy(got, got, send_sems, recv_sems, 2 * n_ici + j, (x, y, c)).wait_recv()
            for cp in remote:
                cp.wait_send()
            local.wait()

    outs = [jax.ShapeDtypeStruct(f.shape, f.dtype) for f in fulls]
    ins = list(fulls)
    if rbp is not None:
        ins.append(rbp)
        outs.append(jax.ShapeDtypeStruct((4,) + rbp.shape, F32))
    return _Comm(tuple(ins), tuple(outs), {t: t for t in range(n)}, 2 * n_ici + 4, start, finish)


def _chips_comm(sums):
    n = len(sums)

    def copies(cins, couts, send_sems, recv_sems):
        x, y, c, chips = _place()
        return [_rcopy(cins[t].at[2 * chip[0] + chip[1]], couts[t].at[j], send_sems, recv_sems, 3 * t + j, (*chip, c))
                for t in range(n) for j, chip in enumerate(chips)]

    def start(*refs):
        for cp in copies(*refs):
            cp.start()

    def finish(*refs):
        for cp in copies(*refs):
            cp.wait()

    outs = tuple(jax.ShapeDtypeStruct((3,) + p.shape[1:], p.dtype) for p in sums)
    return _Comm(tuple(sums), outs, {}, 3 * n, start, finish)


def _pair_comm(ins, outs, aliases, copies):
    def start(*refs):
        for cp in copies(*refs):
            cp.start()

    def finish(*refs):
        for cp in copies(*refs):
            cp.wait()

    return _Comm(tuple(ins), tuple(outs), aliases, len(ins), start, finish)


def _sibling_comm(parts):
    def copies(cins, couts, send_sems, recv_sems):
        x, y, c, _ = _place()
        return [_rcopy(p.at[:, pl.ds((1 - c) * (p.shape[1] // 2), p.shape[1] // 2), :], couts[t],
                       send_sems, recv_sems, t, (x, y, 1 - c)) for t, p in enumerate(cins)]

    half = [jax.ShapeDtypeStruct((p.shape[0], p.shape[1] // 2, p.shape[2]), p.dtype) for p in parts]
    return _pair_comm(parts, half, {}, copies)


def _halves_comm(tots):
    def copies(cins, couts, send_sems, recv_sems):
        x, y, c, _ = _place()
        return [_rcopy(g.at[c], g.at[c], send_sems, recv_sems, t, (x, y, 1 - c)) for t, g in enumerate(couts)]

    return _pair_comm(tots, [jax.ShapeDtypeStruct(t.shape, t.dtype) for t in tots],
                      {t: t for t in range(len(tots))}, copies)


def _gather_small(packed):
    def body(p_ref, all_ref, send_sems, recv_sems, loc_sems):
        x, y, c, _ = _place()
        me = 4 * x + 2 * y + c
        local = pltpu.make_async_copy(p_ref, all_ref.at[me], loc_sems.at[0])
        local.start()
        sent = []
        for k in range(1, 8):
            px, py, pc = x ^ (k >> 2), y ^ ((k >> 1) & 1), c ^ (k & 1)
            cp = _rcopy(p_ref, all_ref.at[me], send_sems, recv_sems, k - 1, (px, py, pc))
            cp.start()
            sent.append(cp)
        for k in range(1, 8):
            px, py, pc = x ^ (k >> 2), y ^ ((k >> 1) & 1), c ^ (k & 1)
            got = all_ref.at[4 * px + 2 * py + pc]
            _rcopy(got, got, send_sems, recv_sems, k - 1, (x, y, c)).wait_recv()
        for cp in sent:
            cp.wait_send()
        local.wait()

    return _comm_call(body, "gather_small", [packed], [jax.ShapeDtypeStruct((8,) + packed.shape, F32)], 7, 1)[0]


def _sum_devices(allp):
    n, r, c = allp.shape

    def body(a_ref, o_ref):
        acc = a_ref[0]
        for k in range(1, n):
            acc = acc + a_ref[k]
        o_ref[...] = acc

    return pl.pallas_call(body, name="sum_devices", out_shape=jax.ShapeDtypeStruct((r, c), F32))(allp)


def _ext_index():
    u = np.arange(EXT)
    dist = np.where(u < WIN, PAD - u, PAD + EXT - u)
    return np.clip(dist, -(CHUNK - 1), REL_CLIP) + (CHUNK - 1)


def _pack(parts, rows):
    flat = jnp.concatenate([p.reshape(-1) for p in parts])
    return jnp.pad(flat, (0, rows * 128 - flat.shape[0])).reshape(rows, 128)


def _unpack(packed, shapes):
    flat, out, at = packed.reshape(-1), [], 0
    for shp in shapes:
        size = int(np.prod(shp))
        out.append(flat[at:at + size].reshape(shp))
        at += size
    return out


def kernel(x, norm_g, w_in, q_norm_g, k_norm_g, rel_bias, w_out, loss_target, m_norm_g, m_w_in, m_q_norm_g, m_k_norm_g, m_rel_bias, m_w_out, v_norm_g, v_w_in, v_q_norm_g, v_k_norm_g, v_rel_bias, v_w_out):
    nl, d, nb = w_in.shape
    s = x.shape[1]
    ds_ = d // 2
    nh = ds_ // HEAD
    rb = w_out.shape[1]
    nrel = rel_bias.shape[2]
    bx = lax.axis_index("x") * 2 + lax.axis_index("y")

    rb_rows = -(-(nl * nh * nrel) // 1024) * 8
    cx = lax.axis_index("c")
    wi_full = [_cast_block(w_in, l, bx, "cast_w_in") for l in range(nl)]
    wo_full = [_cast_block(w_out, l, bx, "cast_w_out") for l in range(nl)]
    wi_full[0], wo_full[0], rel_all = _run_comm(
        _gather_comm([wi_full[0], wo_full[0]], _pack([rel_bias], rb_rows)), "gather_first")
    rel_full = jnp.concatenate(
        [rel_all[j].reshape(-1)[:nl * nh * nrel].reshape(nl, nh, nrel) for j in range(4)], axis=2)
    ext_idx = _ext_index()
    onehot = jnp.asarray(ext_idx[:, None] == np.arange(N_REL)[None, :], F32)
    ext = jnp.einsum("lhr,ur->lhu", rel_full, onehot, precision=lax.Precision.HIGHEST).reshape(nl, nh, 1, EXT)

    xs, hs, projs, yas, lts, ybs, mixes, wi_t = [], [], [], [], [], [], [], []
    xc = x[0]
    for l in range(nl):
        h = _rmsnorm_fwd(xc, norm_g[l:l + 1])
        nxt = l + 1 < nl
        (proj, wt), got = _in_proj(h, wi_full[l], _gather_comm(
            [wo_full[l + 1], wi_full[l + 1]], stage="chips", chips_at=[(0, 1, 2, 3), (2,)]) if nxt else None)
        wi_t.append(wt)
        if nxt:
            wo_full[l + 1], wi_full[l + 1] = got
        (ya, lt), got = _attn_a_fwd(proj, nh, _gather_comm(
            [wi_full[l + 1]], stage="chips", chips_at=(0, 1)) if nxt else None)
        if nxt:
            wi_full[l + 1] = got[0]
        (yb,), got = _attn_b_fwd(proj, q_norm_g[l:l + 1], k_norm_g[l:l + 1], ext[l], nh,
                                 _gather_comm([wi_full[l + 1]], stage="chips", chips_at=(3,)) if nxt else None)
        if nxt:
            wi_full[l + 1] = got[0]
        xs.append(xc)
        wo_now = wo_full[l].reshape(4 * rb, d)
        (xc, mix), got = _out_proj(xc, ya, yb, proj, wo_now, _gather_comm(
            [wi_full[l + 1], wo_full[l + 1]], stage="sibling") if nxt else None)
        if nxt:
            wi_full[l + 1], wo_full[l + 1] = got
        hs.append(h); projs.append(proj); yas.append(ya); lts.append(lt); ybs.append(yb); mixes.append(mix)
    dx, loss_tile = _loss_head(xc, loss_target[0])

    small, g_wi, g_wo = [None] * nl, [None] * nl, [None] * nl
    pending = None

    def keep(lay, shared):
        g_wi[lay], g_wo[lay] = shared[0].reshape(d, nb), shared[1].reshape(rb, d)

    for l in reversed(range(nl)):
        wo = wo_full[l].reshape(4 * rb, d)
        p_wo = _wgrad(mixes[l], dx, 4, False, "wgrad_out")
        last = l == 0 and pending is not None
        (dya, dyb, dga, dgb), theirs_wo = _out_proj_bwd(dx, yas[l], ybs[l], projs[l], wo,
                                                        _sibling_comm([p_wo]) if last else None)
        travelling = list(pending[1]) if pending else []
        if last:
            sums_wo = _add_sibling(p_wo, theirs_wo[0], cx)
            travelling.append(sums_wo)
        (dqa, dka, dva), got = _attn_a_bwd(projs[l], lts[l], dya, nh, _chips_comm(travelling) if pending else None)
        tots = [_add_chips(pending[1][t], got[t], bx, cx) for t in range(2)] if pending else None
        (dqb, dkb, dvb, dqg, dkg, dext), shared = _attn_b_bwd(
            projs[l], dyb, q_norm_g[l:l + 1], k_norm_g[l:l + 1], ext[l], nh, _halves_comm(tots) if pending else None)
        if pending:
            keep(pending[0], shared)
        dproj = jnp.concatenate([dqa, dka, dva, dga, dqb, dkb, dvb, dgb], axis=1)
        parts = [_wgrad(hs[l], dproj, 4, True, "wgrad_in"), p_wo]
        if l > 0:
            (dx, dng), theirs = _in_proj_bwd(dproj, wi_t[l],xs[l], dx, norm_g[l:l + 1], _sibling_comm(parts))
            pending = (l, [_add_sibling(parts[t], theirs[t], cx) for t in range(2)])
        elif last:
            theirs = _run_comm(_sibling_comm(parts[:1]), "reduce_sibling")
            sums_wi = _add_sibling(parts[0], theirs[0], cx)
            (dx, dng), got_wi = _in_proj_bwd(dproj, wi_t[l],xs[l], dx, norm_g[l:l + 1], _chips_comm([sums_wi]))
            keep(0, _run_comm(_halves_comm([_add_chips(sums_wi, got_wi[0], bx, cx),
                                            _add_chips(sums_wo, got[2], bx, cx)]), "share_halves"))
        else:
            theirs = _run_comm(_sibling_comm(parts), "reduce_sibling")
            sums = [_add_sibling(parts[t], theirs[t], cx) for t in range(2)]
            (dx, dng), got = _in_proj_bwd(dproj, wi_t[l],xs[l], dx, norm_g[l:l + 1], _chips_comm(sums))
            keep(0, _run_comm(_halves_comm([_add_chips(sums[t], got[t], bx, cx) for t in range(2)]), "share_halves"))
        small[l] = (dng[0], jnp.sum(dqg, axis=0).reshape(-1), jnp.sum(dkg, axis=0).reshape(-1), dext.reshape(nh, EXT))
    grad_x = dx[None]

    small_shapes = [(nl, d), (nl, HEAD), (nl, HEAD), (nl, nh, EXT), (1,)]
    small_parts = [jnp.stack([sm[i] for sm in small]) for i in range(4)] + [loss_tile[0, :1]]
    rows = -(-sum(int(np.prod(sh)) for sh in small_shapes) // 1024) * 8
    tot = _sum_devices(_gather_small(_pack(small_parts, rows)))
    g_ng, g_qg, g_kg, g_ext, loss = _unpack(tot, small_shapes)
    g_rel_full = jnp.einsum("lhu,ur->lhr", g_ext, onehot, precision=lax.Precision.HIGHEST)
    g_rel = lax.dynamic_slice_in_dim(g_rel_full, bx * nrel, nrel, axis=2)

    res_wi, res_wo = (), ()
    for l in range(nl):
        res_wi = _adamw_layer(l, w_in, g_wi[l], m_w_in, v_w_in, res_wi, "adamw_w_in")
        res_wo = _adamw_layer(l, w_out, g_wo[l], m_w_out, v_w_out, res_wo, "adamw_w_out")
    g_wi, d_wi, nm_wi, nv_wi = res_wi
    g_wo, d_wo, nm_wo, nv_wo = res_wo
    sm_shapes = [(nl, d), (nl, HEAD), (nl, HEAD), (nl, nh, nrel)]
    sm_rows = -(-sum(int(np.prod(sh)) for sh in sm_shapes) // 1024) * 8
    pw, pg, pm, pv = [_pack(group, sm_rows) for group in (
        (norm_g, q_norm_g, k_norm_g, rel_bias), (g_ng, g_qg, g_kg, g_rel),
        (m_norm_g, m_q_norm_g, m_k_norm_g, m_rel_bias), (v_norm_g, v_q_norm_g, v_k_norm_g, v_rel_bias))]
    d_sm, nm_sm, nv_sm = [_unpack(a[0], sm_shapes)
                          for a in _adamw_layer(0, pw[None], pg, pm[None], pv[None], (), "adamw_small")[1:]]

    return (loss[0], grad_x, g_ng, g_wi, g_qg, g_kg, g_rel, g_wo,
            d_sm[0], d_wi, d_sm[1], d_sm[2], d_sm[3], d_wo,
            nm_sm[0], nm_wi, nm_sm[1], nm_sm[2], nm_sm[3], nm_wo,
            nv_sm[0], nv_wi, nv_sm[1], nv_sm[2], nv_sm[3], nv_wo)
```

```python
from typing import Callable, NamedTuple

import jax
import jax.numpy as jnp
import numpy as np
from jax import lax
from jax.experimental import pallas as pl
from jax.experimental.pallas import tpu as pltpu

F32 = jnp.float32
BF16 = jnp.bfloat16

HEAD = 128
CHUNK = 64
LEFT_CHUNKS = 8
REL_CLIP = 256
N_REL = REL_CLIP + CHUNK
NORM_EPS = 1e-6
NEG_BIG = -1e30
TQ = 256
TK = TQ
QB = 512
ROWS = 32
PAD = LEFT_CHUNKS * CHUNK
WIN = PAD + TQ
EXT = 1024
SCALE = HEAD ** -0.5

ADAM_LR = 0.001
ADAM_B1 = 0.9
ADAM_B2 = 0.999
ADAM_EPS = 1e-08
ADAM_WD = 0.01
ADAM_STEP = 10

ANY = pl.BlockSpec(memory_space=pl.ANY)
MESH = pl.DeviceIdType.MESH


def _params(sem=None, vmem_mb=None):
    kw = {}
    if sem is not None:
        kw["dimension_semantics"] = sem
    if vmem_mb is not None:
        kw["vmem_limit_bytes"] = vmem_mb << 20
    return pltpu.CompilerParams(**kw)


class _Comm(NamedTuple):
    ins: tuple
    outs: tuple
    aliases: dict
    n_sems: int
    start: Callable
    finish: Callable


def _call(body, *, name, grid, in_specs, out_specs, out_shape, args, scratch=(), sem=None, vmem_mb=None, comm=None):
    if comm is None:
        out = pl.pallas_call(body, name=name, grid=grid, in_specs=in_specs, out_specs=out_specs, out_shape=out_shape,
                             scratch_shapes=list(scratch), compiler_params=_params(sem, vmem_mb))(*args)
        return out, ()
    n_in, n_out, n_ci, n_co = len(in_specs), len(out_shape), len(comm.ins), len(comm.outs)

    def hosted(*refs):
        ins, cins = refs[:n_in], refs[n_in:n_in + n_ci]
        outs, couts = refs[n_in + n_ci:n_in + n_ci + n_out], refs[n_in + n_ci + n_out:n_in + n_ci + n_out + n_co]
        rest = refs[n_in + n_ci + n_out + n_co:]
        send_sems, recv_sems = rest[-2:]
        first, last = None, None
        for ax, size in enumerate(grid):
            at = pl.program_id(ax)
            first = (at == 0) if first is None else first & (at == 0)
            last = (at == size - 1) if last is None else last & (at == size - 1)

        @pl.when(first)
        def _():
            comm.start(cins, couts, send_sems, recv_sems)

        body(*ins, *outs, *rest[:-2])

        @pl.when(last)
        def _():
            comm.finish(cins, couts, send_sems, recv_sems)

    out = pl.pallas_call(
        hosted, name=name, grid=grid, in_specs=list(in_specs) + [ANY] * n_ci, out_specs=list(out_specs) + [ANY] * n_co,
        out_shape=list(out_shape) + list(comm.outs),
        input_output_aliases={n_in + k: n_out + v for k, v in comm.aliases.items()},
        scratch_shapes=list(scratch) + [pltpu.SemaphoreType.DMA((comm.n_sems,)), pltpu.SemaphoreType.DMA((comm.n_sems,))],
        compiler_params=_params(("arbitrary",) * len(grid), vmem_mb))(*args, *comm.ins)
    return out[:n_out], out[n_out:]


def _dot(a, b):
    return jnp.dot(a, b, preferred_element_type=F32)


def _dot_nt(a, b):
    return lax.dot_general(a, b, (((1,), (1,)), ((), ())), preferred_element_type=F32)


def _dot_tn(a, b):
    return lax.dot_general(a, b, (((0,), (0,)), ((), ())), preferred_element_type=F32)


def _split_dot(x, m):
    hi = x.astype(BF16)
    lo = (x - hi.astype(F32)).astype(BF16)
    return _dot(hi, m) + _dot(lo, m)


def _silu_parts(g):
    sg = 1.0 / (1.0 + jnp.exp(-g))
    return g * sg, sg * (1.0 + g * (1.0 - sg))


def _idx(*vals):
    return jnp.stack([jnp.asarray(v, jnp.int32) for v in vals])


def _cast_block(w, l, blk, name):
    _, r, c = w.shape
    tr = min(r, 512)

    def body(b_ref, w_ref, o_ref):
        o_ref[...] = w_ref[...].astype(BF16)

    spec = pltpu.PrefetchScalarGridSpec(
        num_scalar_prefetch=1, grid=(r // tr,),
        in_specs=[pl.BlockSpec((None, tr, c), lambda i, b: (l, i, 0))],
        out_specs=pl.BlockSpec((None, tr, c), lambda i, b: (b[0], i, 0)))
    return pl.pallas_call(body, name=name, grid_spec=spec, out_shape=jax.ShapeDtypeStruct((4, r, c), BF16),
                          compiler_params=_params(("parallel",)))(_idx(blk), w)


def _add_sibling(p, theirs, core):
    nblk, r, c = p.shape
    hr = r // 2
    tr = min(hr, 256)
    per = hr // tr

    def body(c_ref, p_ref, t_ref, o_ref):
        o_ref[...] = (p_ref[...].astype(F32) + t_ref[...].astype(F32)).astype(BF16)

    blk = pl.BlockSpec((None, tr, c), lambda j, i, cr: (j, i, 0))
    spec = pltpu.PrefetchScalarGridSpec(
        num_scalar_prefetch=1, grid=(nblk, per),
        in_specs=[pl.BlockSpec((None, tr, c), lambda j, i, cr: (j, cr[0] * per + i, 0)), blk], out_specs=blk)
    return pl.pallas_call(body, name="add_sibling", grid_spec=spec, out_shape=jax.ShapeDtypeStruct((nblk, hr, c), BF16),
                          compiler_params=_params(("parallel", "parallel")))(_idx(core), p, theirs)


def _add_chips(sums, got, blk, core):
    _, hr, c = sums.shape
    tr = min(hr, 256)

    def body(i_ref, s_ref, g0_ref, g1_ref, g2_ref, o_ref):
        o_ref[...] = ((s_ref[...].astype(F32) + g0_ref[...].astype(F32))
                      + g1_ref[...].astype(F32)) + g2_ref[...].astype(F32)

    at = lambda j: pl.BlockSpec((None, tr, c), lambda i, ir: (j, i, 0))
    spec = pltpu.PrefetchScalarGridSpec(
        num_scalar_prefetch=1, grid=(hr // tr,),
        in_specs=[pl.BlockSpec((None, tr, c), lambda i, ir: (ir[0], i, 0)), at(0), at(1), at(2)],
        out_specs=pl.BlockSpec((None, tr, c), lambda i, ir: (ir[1], i, 0)))
    return pl.pallas_call(body, name="add_chips", grid_spec=spec, out_shape=jax.ShapeDtypeStruct((2, hr, c), F32),
                          compiler_params=_params(("parallel",)))(_idx(blk, core), sums, got, got, got)


def _adamw_layer(l, w, g, m, v, prev, name):
    nl, r, c = w.shape
    tr = min(r, 256)
    c1 = 1.0 / (1.0 - ADAM_B1 ** ADAM_STEP)
    c2 = 1.0 / (1.0 - ADAM_B2 ** ADAM_STEP)

    def body(w_ref, g_ref, m_ref, v_ref, *rest):
        go_ref, d_ref, nm_ref, nv_ref = rest[-4:]
        gg = g_ref[...]
        nm = ADAM_B1 * m_ref[...] + (1.0 - ADAM_B1) * gg
        nv = ADAM_B2 * v_ref[...] + (1.0 - ADAM_B2) * (gg * gg)
        upd = (nm * c1) / (jnp.sqrt(nv * c2) + ADAM_EPS) + ADAM_WD * w_ref[...]
        go_ref[...] = gg
        d_ref[...] = -ADAM_LR * upd
        nm_ref[...] = nm
        nv_ref[...] = nv

    lay = pl.BlockSpec((None, tr, c), lambda i: (l, i, 0))
    shp = jax.ShapeDtypeStruct((nl, r, c), F32)
    return pl.pallas_call(
        body, name=name, grid=(r // tr,),
        in_specs=[lay, pl.BlockSpec((tr, c), lambda i: (i, 0)), lay, lay] + [ANY] * len(prev),
        out_specs=[lay] * 4, out_shape=[shp] * 4, input_output_aliases={4 + k: k for k in range(len(prev))},
        compiler_params=_params(("parallel",), 40))(w, g, m, v, *prev)


def _rmsnorm_fwd(x, g):
    s, d = x.shape
    tm = min(s, 256)

    def body(x_ref, g_ref, h_ref):
        xv = x_ref[...]
        r = lax.rsqrt(jnp.mean(xv * xv, axis=1, keepdims=True) + NORM_EPS)
        h_ref[...] = (xv * r * g_ref[...]).astype(BF16)

    return pl.pallas_call(
        body, name="rmsnorm_fwd", grid=(s // tm,),
        in_specs=[pl.BlockSpec((tm, d), lambda i: (i, 0)), pl.BlockSpec((1, d), lambda i: (0, 0))],
        out_specs=pl.BlockSpec((tm, d), lambda i: (i, 0)),
        out_shape=jax.ShapeDtypeStruct((s, d), BF16),
        compiler_params=_params(("parallel",)))(x, g)


def _in_proj(h, w, comm=None):
    s, d = h.shape
    nblk, _, nb = w.shape
    tm, tn = min(s, 1024), min(nb, 1024)
    per = nb // tn

    def body(h_ref, w_ref, o_ref, wt_ref):
        o_ref[...] = _dot(h_ref[...], w_ref[...])

        @pl.when(pl.program_id(1) == 0)
        def _():
            for c in range(tn // HEAD):
                wt_ref[pl.ds(c * HEAD, HEAD), :] = w_ref[:, c * HEAD:(c + 1) * HEAD].astype(F32).T.astype(BF16)

    return _call(
        body, name="in_proj", grid=(nblk * per, s // tm),
        in_specs=[pl.BlockSpec((tm, d), lambda n, m: (m, 0)),
                  pl.BlockSpec((None, d, tn), lambda n, m: (n // per, 0, n % per))],
        out_specs=[pl.BlockSpec((tm, tn), lambda n, m: (m, n)), pl.BlockSpec((tn, d), lambda n, m: (n, 0))],
        out_shape=[jax.ShapeDtypeStruct((s, nblk * nb), F32), jax.ShapeDtypeStruct((nblk * nb, d), BF16)],
        sem=("parallel", "arbitrary"), vmem_mb=56, args=(h, w), comm=comm)


def _heads_per_step(nh):
    return 2 if nh % 2 == 0 else 1


def _head(hh):
    return slice(hh * HEAD, (hh + 1) * HEAD)


def _tri(op):
    r = lax.broadcasted_iota(jnp.int32, (TQ, TQ), 0)
    c = lax.broadcasted_iota(jnp.int32, (TQ, TQ), 1)
    return op(r, c)


def _staggered(groups, hp):
    for hh in range(hp):
        for fn in groups[0]:
            fn(hh)
    for group in groups[1:]:
        for hh in range(hp):
            for fn in group:
                fn(hh)


def _transpose_tiles(src_ref, dst_ref, hp, nt):
    for hh in range(hp):
        for t in range(nt):
            dst_ref[hh, t] = src_ref[pl.ds(t * TK, TK), _head(hh)].T.astype(BF16)


def _chunk_mask(r, a):
    if a is None or r * ROWS >= (a + 1) * TK:
        return None
    if (r + 1) * ROWS <= a * TK:
        return False
    row = lax.broadcasted_iota(jnp.int32, (ROWS, TK), 0) + r * ROWS
    return row > lax.broadcasted_iota(jnp.int32, (ROWS, TK), 1) + a * TK


def _sb_logs(qk, causal):
    z = qk * SCALE
    l1p = jnp.log(1.0 + jnp.exp(-jnp.abs(z)))
    ls = jnp.minimum(-z, 0.0) - l1p
    if causal is not None:
        ls = jnp.where(causal, ls, 0.0)
    return ls, jnp.minimum(z, 0.0) - l1p


def _attn_a_fwd(proj, nh, comm=None):
    s = proj.shape[0]
    qb_rows = min(QB, s)
    nq, nt, per = s // qb_rows, s // TK, qb_rows // TK
    hp = _heads_per_step(nh)
    ng = nh // hp
    chunks = qb_rows // ROWS

    def body(q_ref, k_ref, v_ref, o_ref, lt_ref, acc_ref, qb_ref, m_ref, car_ref, sum_ref,
             z_ref, lsig_ref, aft_ref, hi_ref, lo_ref, w_ref, kt_ref):
        i = pl.program_id(1)

        @pl.when(i == 0)
        def _():
            _transpose_tiles(k_ref, kt_ref, hp, nt)

        qb_ref[...] = q_ref[...].astype(BF16)
        m_ref[...] = _tri(lambda r, c: r > c).astype(BF16)
        acc_ref[...] = jnp.zeros_like(acc_ref)
        car_ref[...] = jnp.zeros_like(car_ref)

        def tile(j, a):
            off = pl.multiple_of(j * TK, TK)

            def scores(hh):
                z_ref[hh] = _dot(qb_ref[:, _head(hh)], kt_ref[hh, j])

            def logs(hh):
                for r in range(chunks):
                    sl, mask = pl.ds(r * ROWS, ROWS), _chunk_mask(r, a)
                    if mask is False:
                        hi_ref[hh, sl, :] = jnp.zeros((ROWS, TK), BF16)
                        lo_ref[hh, sl, :] = jnp.zeros((ROWS, TK), BF16)
                        continue
                    ls, lsig = _sb_logs(z_ref[hh, sl, :], mask)
                    lsig_ref[hh, sl, :] = lsig
                    hi = ls.astype(BF16)
                    hi_ref[hh, sl, :] = hi
                    lo_ref[hh, sl, :] = (ls - hi.astype(F32)).astype(BF16)
                    sum_ref[hh, sl, :] = jnp.sum(ls, axis=1, keepdims=True)

            def after(hh):
                aft_ref[hh] = _dot(hi_ref[hh], m_ref[...]) + _dot(lo_ref[hh], m_ref[...])

            def weights(hh):
                for r in range(chunks):
                    sl, mask = pl.ds(r * ROWS, ROWS), _chunk_mask(r, a)
                    if mask is False:
                        w_ref[hh, sl, :] = jnp.zeros((ROWS, TK), BF16)
                        continue
                    w = jnp.exp(lsig_ref[hh, sl, :] + aft_ref[hh, sl, :] + car_ref[hh, sl, :])
                    if mask is not None:
                        w = jnp.where(mask, w, 0.0)
                    w_ref[hh, sl, :] = w.astype(BF16)
                    car_ref[hh, sl, :] += sum_ref[hh, sl, :]

            def values(hh):
                vb = v_ref[pl.ds(off, TK), _head(hh)].astype(BF16)
                acc_ref[:, _head(hh)] += _dot(w_ref[hh], vb)

            _staggered([(scores,), (logs, after), (weights, values)], hp)

        for a in reversed(range(per)):
            tile(i * per + a, a)

        def step(t, carry):
            tile(i * per - 1 - t, None)
            return carry

        lax.fori_loop(0, i * per, step, 0)
        o_ref[...] = acc_ref[...]
        lt_ref[...] = car_ref[...]

    wd = hp * HEAD
    sq = lambda dt: pltpu.VMEM((hp, qb_rows, TK), dt)
    return _call(
        body, name="attn_a_fwd", grid=(ng, nq),
        in_specs=[pl.BlockSpec((qb_rows, wd), lambda h, i: (i, h)),
                  pl.BlockSpec((s, wd), lambda h, i: (0, ng + h)),
                  pl.BlockSpec((s, wd), lambda h, i: (0, 2 * ng + h))],
        out_specs=[pl.BlockSpec((qb_rows, wd), lambda h, i: (i, h)),
                   pl.BlockSpec((hp, qb_rows, 1), lambda h, i: (h, i, 0))],
        out_shape=[jax.ShapeDtypeStruct((s, nh * HEAD), F32), jax.ShapeDtypeStruct((nh, s, 1), F32)],
        scratch=[pltpu.VMEM((qb_rows, wd), F32), pltpu.VMEM((qb_rows, wd), BF16), pltpu.VMEM((TK, TK), BF16),
                 pltpu.VMEM((hp, qb_rows, 1), F32), pltpu.VMEM((hp, qb_rows, 1), F32),
                 sq(F32), sq(F32), sq(F32), sq(BF16), sq(BF16), sq(BF16),
                 pltpu.VMEM((hp, nt, HEAD, TK), BF16)],
        sem=("parallel", "arbitrary"), args=(proj, proj, proj), comm=comm)


def _band_valid(i):
    cl = lax.broadcasted_iota(jnp.int32, (TQ, WIN), 0) // CHUNK
    kl = lax.broadcasted_iota(jnp.int32, (TQ, WIN), 1) // CHUNK
    first = LEFT_CHUNKS - (TQ // CHUNK) * i
    return (kl >= cl) & (kl <= cl + LEFT_CHUNKS) & (kl >= first)


def _build_bias(e_ref, bias_ref):
    e8 = jnp.broadcast_to(e_ref[...], (8, EXT))
    row = lax.broadcasted_iota(jnp.int32, (8, EXT), 0)
    t8 = jnp.zeros((8, EXT), F32)
    for b in range(8):
        t8 = jnp.where(row == b, pltpu.roll(e8, b, 1) if b else e8, t8)
    for a in range(TQ // 8):
        sl = pltpu.roll(t8, 8 * a, 1) if a else t8
        bias_ref[pl.ds(8 * a, 8), :] = sl[:, :WIN]


def _reduce_bias_grad(db_ref):
    acc = jnp.zeros((8, EXT), F32)
    for a in range(TQ // 8):
        sl = db_ref[pl.ds(8 * a, 8), :]
        acc = acc + (pltpu.roll(sl, EXT - 8 * a, 1) if a else sl)
    row = lax.broadcasted_iota(jnp.int32, (8, EXT), 0)
    tot = jnp.zeros((8, EXT), F32)
    for b in range(8):
        tot = tot + jnp.where(row == b, pltpu.roll(acc, EXT - b, 1) if b else acc, 0.0)
    return jnp.sum(tot, axis=0, keepdims=True)


def _band_fill(k_ref, v_ref, kg_ref, kn_pad, v_pad, s):
    k = k_ref[...]
    rk = lax.rsqrt(jnp.mean(k * k, axis=1, keepdims=True) + NORM_EPS)
    kn_pad[pl.ds(0, PAD), :] = jnp.zeros((PAD, HEAD), BF16)
    kn_pad[pl.ds(PAD, s), :] = (k * rk * kg_ref[...]).astype(BF16)
    v_pad[pl.ds(0, PAD), :] = jnp.zeros((PAD, HEAD), BF16)
    v_pad[pl.ds(PAD, s), :] = v_ref[...].astype(BF16)


def _band_probs(q_ref, qg_ref, kn_pad, bias_ref, i):
    q = q_ref[...]
    rq = lax.rsqrt(jnp.mean(q * q, axis=1, keepdims=True) + NORM_EPS)
    qhat = q * rq
    qn = (qhat * qg_ref[...]).astype(BF16)
    off = pl.multiple_of(i * TQ, TQ)
    kw = kn_pad[pl.ds(off, WIN), :]
    sc = _dot_nt(qn, kw) * SCALE + bias_ref[...]
    sc = jnp.where(_band_valid(i), sc, NEG_BIG)
    p = jnp.exp(sc - jnp.max(sc, axis=1, keepdims=True))
    pn = p / jnp.sum(p, axis=1, keepdims=True)
    return rq, qhat, qn, kw, off, pn


def _attn_b_fwd(proj, qg, kg, ext, nh, comm=None):
    s = proj.shape[0]
    nq = s // TQ

    def body(q_ref, k_ref, v_ref, qg_ref, kg_ref, e_ref, o_ref, kn_pad, v_pad, bias_ref):
        i = pl.program_id(1)

        @pl.when(i == 0)
        def _():
            _band_fill(k_ref, v_ref, kg_ref, kn_pad, v_pad, s)
            _build_bias(e_ref, bias_ref)

        _, _, _, _, off, pn = _band_probs(q_ref, qg_ref, kn_pad, bias_ref, i)
        o_ref[...] = _dot(pn.astype(BF16), v_pad[pl.ds(off, WIN), :])

    vec = pl.BlockSpec((1, HEAD), lambda h, i: (0, 0))
    return _call(
        body, name="attn_b_fwd", grid=(nh, nq),
        in_specs=[pl.BlockSpec((TQ, HEAD), lambda h, i: (i, 4 * nh + h)),
                  pl.BlockSpec((s, HEAD), lambda h, i: (0, 5 * nh + h)),
                  pl.BlockSpec((s, HEAD), lambda h, i: (0, 6 * nh + h)),
                  vec, vec,
                  pl.BlockSpec((None, 1, EXT), lambda h, i: (h, 0, 0))],
        out_specs=[pl.BlockSpec((TQ, HEAD), lambda h, i: (i, h))],
        out_shape=[jax.ShapeDtypeStruct((s, nh * HEAD), F32)],
        scratch=[pltpu.VMEM((s + PAD, HEAD), BF16), pltpu.VMEM((s + PAD, HEAD), BF16), pltpu.VMEM((TQ, WIN), F32)],
        sem=("parallel", "arbitrary"), args=(proj, proj, proj, qg, kg, ext), comm=comm)


def _out_proj(x, ya, yb, proj, w, comm=None):
    s, d = x.shape
    ds_ = ya.shape[1]
    tm = min(s, 256)

    def body(x_ref, ya_ref, yb_ref, ga_ref, gb_ref, w_ref, o_ref, mix_ref):
        ma = (ya_ref[...] * _silu_parts(ga_ref[...])[0]).astype(BF16)
        mb = (yb_ref[...] * _silu_parts(gb_ref[...])[0]).astype(BF16)
        mix_ref[:, :ds_] = ma
        mix_ref[:, ds_:] = mb
        o_ref[...] = x_ref[...] + _dot(ma, w_ref[pl.ds(0, ds_), :]) + _dot(mb, w_ref[pl.ds(ds_, ds_), :])

    row = lambda width: pl.BlockSpec((tm, width), lambda i: (i, 0))
    return _call(
        body, name="out_proj", grid=(s // tm,),
        in_specs=[row(d), row(ds_), row(ds_),
                  pl.BlockSpec((tm, ds_), lambda i: (i, 3)), pl.BlockSpec((tm, ds_), lambda i: (i, 7)),
                  pl.BlockSpec((2 * ds_, d), lambda i: (0, 0))],
        out_specs=[row(d), row(2 * ds_)],
        out_shape=[jax.ShapeDtypeStruct((s, d), F32), jax.ShapeDtypeStruct((s, 2 * ds_), BF16)],
        sem=("parallel",), vmem_mb=48, args=(x, ya, yb, proj, proj, w), comm=comm)


def _loss_head(y, tgt):
    s, d = y.shape
    tm = min(s, 256)

    def body(y_ref, t_ref, dy_ref, l_ref):
        @pl.when(pl.program_id(0) == 0)
        def _():
            l_ref[...] = jnp.zeros_like(l_ref)

        err = y_ref[...] - t_ref[...]
        dy_ref[...] = err * (1.0 / d)
        l_ref[...] += 0.5 * jnp.sum(jnp.mean(err * err, axis=1, keepdims=True), axis=0, keepdims=True)

    row = pl.BlockSpec((tm, d), lambda i: (i, 0))
    return pl.pallas_call(
        body, name="loss_head", grid=(s // tm,), in_specs=[row, row],
        out_specs=[row, pl.BlockSpec((8, 128), lambda i: (0, 0))],
        out_shape=[jax.ShapeDtypeStruct((s, d), F32), jax.ShapeDtypeStruct((8, 128), F32)],
        compiler_params=_params(("arbitrary",)))(y, tgt)


def _out_proj_bwd(dxo, ya, yb, proj, w, comm=None):
    s, d = dxo.shape
    ds_ = ya.shape[1]
    tm = min(s, 256)

    def body(dx_ref, ya_ref, yb_ref, ga_ref, gb_ref, w_ref, dya_ref, dyb_ref, dga_ref, dgb_ref):
        dxb = dx_ref[...].astype(BF16)
        for y_ref, g_ref, lo, dy_ref, dg_ref in ((ya_ref, ga_ref, 0, dya_ref, dga_ref),
                                                 (yb_ref, gb_ref, ds_, dyb_ref, dgb_ref)):
            dmix = _dot_nt(dxb, w_ref[pl.ds(lo, ds_), :])
            act, dact = _silu_parts(g_ref[...])
            dy_ref[...] = dmix * act
            dg_ref[...] = (dmix * y_ref[...] * dact).astype(BF16)

    row = lambda width: pl.BlockSpec((tm, width), lambda i: (i, 0))
    return _call(
        body, name="out_proj_bwd", grid=(s // tm,),
        in_specs=[row(d), row(ds_), row(ds_),
                  pl.BlockSpec((tm, ds_), lambda i: (i, 3)), pl.BlockSpec((tm, ds_), lambda i: (i, 7)),
                  pl.BlockSpec((2 * ds_, d), lambda i: (0, 0))],
        out_specs=[row(ds_)] * 4,
        out_shape=[jax.ShapeDtypeStruct((s, ds_), F32)] * 2 + [jax.ShapeDtypeStruct((s, ds_), BF16)] * 2,
        sem=("parallel",), vmem_mb=48, args=(dxo, ya, yb, proj, proj, w), comm=comm)


def _wgrad(a, b, nblk, col_blocks, name):
    s, m = a.shape
    n = b.shape[1]
    if col_blocks:
        tr = min(m, 1024)
        nb = n // nblk
        tn = min(nb, 2048)
        per = nb // tn
        out_shape = (nblk, m, nb)
        out_spec = pl.BlockSpec((None, tr, tn), lambda j, r: (j // per, r, j % per))
    else:
        tn = min(n, 1024)
        tr = m // nblk
        out_shape = (nblk, tr, n)
        out_spec = pl.BlockSpec((None, tr, tn), lambda j, r: (r, 0, j))

    def body(a_ref, b_ref, o_ref):
        o_ref[...] = _dot_tn(a_ref[...].astype(BF16), b_ref[...].astype(BF16)).astype(BF16)

    return pl.pallas_call(
        body, name=name, grid=(n // tn, m // tr),
        in_specs=[pl.BlockSpec((s, tr), lambda j, r: (0, r)), pl.BlockSpec((s, tn), lambda j, r: (0, j))],
        out_specs=out_spec, out_shape=jax.ShapeDtypeStruct(out_shape, BF16),
        compiler_params=_params(("parallel", "parallel"), 48))(a, b)


def _attn_a_bwd(proj, lt, dya, nh, comm=None):
    s = proj.shape[0]
    qb_rows = min(TK, s)
    nq, nt, per = s // qb_rows, s // TK, qb_rows // TK
    hp = _heads_per_step(nh)
    ng = nh // hp
    chunks = qb_rows // ROWS

    def body(q_ref, k_ref, v_ref, lt_ref, do_ref, dq_ref, dk_ref, dv_ref, dq_acc, dk_acc, dv_acc,
             qb_ref, dob_ref, upto_ref, before_ref, cls_ref, cg_ref, sls_ref, sg_ref,
             z_ref, dw_ref, lsig_ref, pre_ref, g_ref, hi_ref, lo_ref, wb_ref, kt_ref, vt_ref, qt_ref, dot_ref):
        i = pl.program_id(1)

        @pl.when(i == 0)
        def _():
            dk_acc[...] = jnp.zeros_like(dk_acc)
            dv_acc[...] = jnp.zeros_like(dv_acc)
            _transpose_tiles(k_ref, kt_ref, hp, nt)
            _transpose_tiles(v_ref, vt_ref, hp, nt)

        dq_acc[...] = jnp.zeros_like(dq_acc)
        qb_ref[...] = q_ref[...].astype(BF16)
        dob_ref[...] = do_ref[...].astype(BF16)
        for hh in range(hp):
            qt_ref[hh] = q_ref[:, _head(hh)].T.astype(BF16)
            dot_ref[hh] = do_ref[:, _head(hh)].T.astype(BF16)
        upto_ref[...] = _tri(lambda r, c: r <= c).astype(BF16)
        before_ref[...] = _tri(lambda r, c: r < c).astype(BF16)
        cls_ref[...] = jnp.zeros_like(cls_ref)
        cg_ref[...] = jnp.zeros_like(cg_ref)

        def tile(j, a):
            off = pl.multiple_of(j * TK, TK)
            zeros = jnp.zeros((ROWS, TK), BF16)

            def scores(hh):
                z_ref[hh] = _dot(qb_ref[:, _head(hh)], kt_ref[hh, j])
                dw_ref[hh] = _dot(dob_ref[:, _head(hh)], vt_ref[hh, j])

            def logs(hh):
                for r in range(chunks):
                    sl, mask = pl.ds(r * ROWS, ROWS), _chunk_mask(r, a)
                    if mask is False:
                        hi_ref[hh, sl, :] = zeros
                        lo_ref[hh, sl, :] = zeros
                        continue
                    ls, lsig = _sb_logs(z_ref[hh, sl, :], mask)
                    lsig_ref[hh, sl, :] = lsig
                    hi = ls.astype(BF16)
                    hi_ref[hh, sl, :] = hi
                    lo_ref[hh, sl, :] = (ls - hi.astype(F32)).astype(BF16)
                    sls_ref[hh, sl, :] = jnp.sum(ls, axis=1, keepdims=True)

            def upto(hh):
                pre_ref[hh] = _dot(hi_ref[hh], upto_ref[...]) + _dot(lo_ref[hh], upto_ref[...])

            def weights(hh):
                for r in range(chunks):
                    sl, mask = pl.ds(r * ROWS, ROWS), _chunk_mask(r, a)
                    if mask is False:
                        wb_ref[hh, sl, :] = zeros
                        hi_ref[hh, sl, :] = zeros
                        continue
                    w = jnp.exp(lsig_ref[hh, sl, :] + (lt_ref[hh, sl, :] - (cls_ref[hh, sl, :] + pre_ref[hh, sl, :])))
                    if mask is not None:
                        w = jnp.where(mask, w, 0.0)
                    wb_ref[hh, sl, :] = w.astype(BF16)
                    g = w * dw_ref[hh, sl, :]
                    g_ref[hh, sl, :] = g
                    hi_ref[hh, sl, :] = g.astype(BF16)
                    sg_ref[hh, sl, :] = jnp.sum(g, axis=1, keepdims=True)

            def earlier(hh):
                dw_ref[hh] = _dot(hi_ref[hh], before_ref[...])

            def logit_grads(hh):
                for r in range(chunks):
                    sl, mask = pl.ds(r * ROWS, ROWS), _chunk_mask(r, a)
                    if mask is False:
                        lo_ref[hh, sl, :] = zeros
                        continue
                    z = z_ref[hh, sl, :] * SCALE
                    e = jnp.exp(-jnp.abs(z))
                    rinv = 1.0 / (1.0 + e)
                    beta = jnp.where(z >= 0.0, rinv, e * rinv)
                    dz = g_ref[hh, sl, :] * (1.0 - beta) - beta * (cg_ref[hh, sl, :] + dw_ref[hh, sl, :])
                    if mask is not None:
                        dz = jnp.where(mask, dz, 0.0)
                    lo_ref[hh, sl, :] = (dz * SCALE).astype(BF16)
                    cls_ref[hh, sl, :] += sls_ref[hh, sl, :]
                    cg_ref[hh, sl, :] += sg_ref[hh, sl, :]

            def grads(hh):
                dq_acc[:, _head(hh)] += _dot(lo_ref[hh], k_ref[pl.ds(off, TK), _head(hh)].astype(BF16))
                dk_acc[hh, j] += _dot(qt_ref[hh], lo_ref[hh])
                dv_acc[hh, j] += _dot(dot_ref[hh], wb_ref[hh])

            _staggered([(scores,), (logs, upto), (weights, earlier), (logit_grads, grads)], hp)

        def step(j, carry):
            tile(j, None)
            return carry

        lax.fori_loop(0, i * per, step, 0)
        for a in range(per):
            tile(i * per + a, a)
        dq_ref[...] = dq_acc[...].astype(BF16)

        @pl.when(i == nq - 1)
        def _():
            for hh in range(hp):
                for t in range(nt):
                    dk_ref[pl.ds(t * TK, TK), _head(hh)] = dk_acc[hh, t].T.astype(BF16)
                    dv_ref[pl.ds(t * TK, TK), _head(hh)] = dv_acc[hh, t].T.astype(BF16)

    wd = hp * HEAD
    sq = lambda dt: pltpu.VMEM((hp, qb_rows, TK), dt)
    tiles = lambda dt: pltpu.VMEM((hp, nt, HEAD, TK), dt)
    blk = pl.BlockSpec((qb_rows, wd), lambda h, i: (i, h))
    col = pl.BlockSpec((s, wd), lambda h, i: (0, h))
    shp = jax.ShapeDtypeStruct((s, nh * HEAD), BF16)
    return _call(
        body, name="attn_a_bwd", grid=(ng, nq),
        in_specs=[blk,
                  pl.BlockSpec((s, wd), lambda h, i: (0, ng + h)),
                  pl.BlockSpec((s, wd), lambda h, i: (0, 2 * ng + h)),
                  pl.BlockSpec((hp, qb_rows, 1), lambda h, i: (h, i, 0)), blk],
        out_specs=[blk, col, col], out_shape=[shp] * 3,
        scratch=[pltpu.VMEM((qb_rows, wd), F32), tiles(F32), tiles(F32),
                 pltpu.VMEM((qb_rows, wd), BF16), pltpu.VMEM((qb_rows, wd), BF16),
                 pltpu.VMEM((TK, TK), BF16), pltpu.VMEM((TK, TK), BF16)]
        + [pltpu.VMEM((hp, qb_rows, 1), F32)] * 4 + [sq(F32)] * 5 + [sq(BF16)] * 3
        + [tiles(BF16), tiles(BF16), pltpu.VMEM((hp, HEAD, qb_rows), BF16), pltpu.VMEM((hp, HEAD, qb_rows), BF16)],
        sem=("parallel", "arbitrary"), args=(proj, proj, proj, lt, dya), comm=comm)


def _attn_b_bwd(proj, dyb, qg, kg, ext, nh, comm=None):
    s = proj.shape[0]
    nq = s // TQ

    def body(q_ref, k_ref, v_ref, do_ref, qg_ref, kg_ref, e_ref,
             dq_ref, dk_ref, dv_ref, dqg_ref, dkg_ref, de_ref,
             kn_pad, v_pad, bias_ref, db_acc, dkn_acc, dv_acc):
        i = pl.program_id(1)

        @pl.when(i == 0)
        def _():
            _band_fill(k_ref, v_ref, kg_ref, kn_pad, v_pad, s)
            _build_bias(e_ref, bias_ref)
            db_acc[...] = jnp.zeros_like(db_acc)
            dkn_acc[...] = jnp.zeros_like(dkn_acc)
            dv_acc[...] = jnp.zeros_like(dv_acc)
            dqg_ref[...] = jnp.zeros_like(dqg_ref)

        rq, qhat, qn, kw, off, pn = _band_probs(q_ref, qg_ref, kn_pad, bias_ref, i)
        dob = do_ref[...].astype(BF16)
        dp = _dot_nt(dob, v_pad[pl.ds(off, WIN), :])
        dsc = pn * (dp - jnp.sum(pn * dp, axis=1, keepdims=True))
        db_acc[:, :WIN] += dsc
        dsb = (dsc * SCALE).astype(BF16)
        dqn = _dot(dsb, kw)
        dkn_acc[pl.ds(off, WIN), :] += _dot_tn(dsb, qn)
        dv_acc[pl.ds(off, WIN), :] += _dot_tn(pn.astype(BF16), dob)
        dqh = dqn * qg_ref[...]
        dq_ref[...] = (rq * (dqh - qhat * jnp.mean(dqh * qhat, axis=1, keepdims=True))).astype(BF16)
        dqg_ref[...] += jnp.sum(dqn * qhat, axis=0, keepdims=True)

        @pl.when(i == nq - 1)
        def _():
            k = k_ref[...]
            rk = lax.rsqrt(jnp.mean(k * k, axis=1, keepdims=True) + NORM_EPS)
            khat = k * rk
            dkn = dkn_acc[pl.ds(PAD, s), :]
            dkh = dkn * kg_ref[...]
            dk_ref[...] = (rk * (dkh - khat * jnp.mean(dkh * khat, axis=1, keepdims=True))).astype(BF16)
            dkg_ref[...] = jnp.sum(dkn * khat, axis=0, keepdims=True)
            dv_ref[...] = dv_acc[pl.ds(PAD, s), :].astype(BF16)
            de_ref[...] = _reduce_bias_grad(db_acc)

    blk = pl.BlockSpec((TQ, HEAD), lambda h, i: (i, h))
    col = pl.BlockSpec((s, HEAD), lambda h, i: (0, h))
    vec = pl.BlockSpec((1, HEAD), lambda h, i: (0, 0))
    hvec = pl.BlockSpec((None, 1, HEAD), lambda h, i: (h, 0, 0))
    hext = pl.BlockSpec((None, 1, EXT), lambda h, i: (h, 0, 0))
    shp = jax.ShapeDtypeStruct((s, nh * HEAD), BF16)
    return _call(
        body, name="attn_b_bwd", grid=(nh, nq),
        in_specs=[pl.BlockSpec((TQ, HEAD), lambda h, i: (i, 4 * nh + h)),
                  pl.BlockSpec((s, HEAD), lambda h, i: (0, 5 * nh + h)),
                  pl.BlockSpec((s, HEAD), lambda h, i: (0, 6 * nh + h)),
                  blk, vec, vec, hext],
        out_specs=[blk, col, col, hvec, hvec, hext],
        out_shape=[shp] * 3 + [jax.ShapeDtypeStruct((nh, 1, HEAD), F32)] * 2
        + [jax.ShapeDtypeStruct((nh, 1, EXT), F32)],
        scratch=[pltpu.VMEM((s + PAD, HEAD), BF16), pltpu.VMEM((s + PAD, HEAD), BF16),
                 pltpu.VMEM((TQ, WIN), F32), pltpu.VMEM((TQ, EXT), F32),
                 pltpu.VMEM((s + PAD, HEAD), F32), pltpu.VMEM((s + PAD, HEAD), F32)],
        sem=("parallel", "arbitrary"), args=(proj, proj, proj, dyb, qg, kg, ext), comm=comm)


def _in_proj_bwd(dproj, wt, x, dxo, g, comm=None):
    s, d = x.shape
    tm, tk = min(s, 512), min(wt.shape[0], 1024)
    nk = wt.shape[0] // tk

    def body(dp_ref, w_ref, x_ref, dxo_ref, g_ref, dx_ref, dg_ref, acc):
        m, k = pl.program_id(0), pl.program_id(1)

        @pl.when(k == 0)
        def _():
            acc[...] = jnp.zeros_like(acc)

        @pl.when((k == 0) & (m == 0))
        def _():
            dg_ref[...] = jnp.zeros_like(dg_ref)

        acc[...] += _dot(dp_ref[...], w_ref[...])

        @pl.when(k == nk - 1)
        def _():
            xv = x_ref[...]
            r = lax.rsqrt(jnp.mean(xv * xv, axis=1, keepdims=True) + NORM_EPS)
            xhat = xv * r
            dh = acc[...]
            dxh = dh * g_ref[...]
            dx_ref[...] = dxo_ref[...] + r * (dxh - xhat * jnp.mean(dxh * xhat, axis=1, keepdims=True))
            dg_ref[...] += jnp.sum(dh * xhat, axis=0, keepdims=True)

    row = pl.BlockSpec((tm, d), lambda m, k: (m, 0))
    return _call(
        body, name="in_proj_bwd", grid=(s // tm, nk),
        in_specs=[pl.BlockSpec((tm, tk), lambda m, k: (m, k)),
                  pl.BlockSpec((tk, d), lambda m, k: (k, 0)),
                  row, row, pl.BlockSpec((1, d), lambda m, k: (0, 0))],
        out_specs=[row, pl.BlockSpec((8, d), lambda m, k: (0, 0))],
        out_shape=[jax.ShapeDtypeStruct((s, d), F32), jax.ShapeDtypeStruct((8, d), F32)],
        scratch=[pltpu.VMEM((tm, d), F32)], sem=("arbitrary", "arbitrary"), vmem_mb=56,
        args=(dproj, wt, x, dxo, g), comm=comm)


def _place():
    x, y, c = lax.axis_index("x"), lax.axis_index("y"), lax.axis_index("c")
    chips = [(1 - x, y), (x, 1 - y), (1 - x, 1 - y)]
    return x, y, c, chips


def _comm_call(body, name, ins, out_shape, n_remote, n_local, aliases=None):
    return pl.pallas_call(
        body, name=name, in_specs=[ANY] * len(ins), out_specs=[ANY] * len(out_shape), out_shape=out_shape,
        input_output_aliases=aliases or {},
        scratch_shapes=[pltpu.SemaphoreType.DMA((n_remote,)), pltpu.SemaphoreType.DMA((n_remote,)),
                        pltpu.SemaphoreType.DMA((n_local,))])(*ins)


def _rcopy(src, dst, send_sems, recv_sems, k, dev):
    return pltpu.make_async_remote_copy(src_ref=src, dst_ref=dst, send_sem=send_sems.at[k], recv_sem=recv_sems.at[k],
                                        device_id=dev, device_id_type=MESH)


def _run_comm(comm, name):
    n_ci, n_co = len(comm.ins), len(comm.outs)

    def body(*refs):
        cins, couts = refs[:n_ci], refs[n_ci:n_ci + n_co]
        send_sems, recv_sems = refs[n_ci + n_co:]
        comm.start(cins, couts, send_sems, recv_sems)
        comm.finish(cins, couts, send_sems, recv_sems)

    return pl.pallas_call(
        body, name=name, in_specs=[ANY] * n_ci, out_specs=[ANY] * n_co, out_shape=list(comm.outs),
        input_output_aliases=dict(comm.aliases),
        scratch_shapes=[pltpu.SemaphoreType.DMA((comm.n_sems,)), pltpu.SemaphoreType.DMA((comm.n_sems,))])(*comm.ins)


MOVES = ((None, 0, 2, 0), (None, 0, 2, 1), (0, 0, 1, 1), (1, 1, 2, 0))
ARRIVALS = ((0, 0, 2), (1, 0, 2), (2, 0, 1), (2, 1, 2))


def _gather_comm(fulls, rbp=None, stage="all", chips_at=(0, 1, 2, 3)):
    n = len(fulls)
    n_ici = len(MOVES) * n
    base = n_ici if stage == "all" else 0
    per_tensor = chips_at if isinstance(chips_at, list) else [chips_at] * n

    def region(full, blk, core, lo, hi):
        quarter = full.shape[1] // 4
        return full.at[blk].at[pl.ds(core * 2 * quarter + lo * quarter, (hi - lo) * quarter)]

    def block_of(chip):
        return 2 * chip[0] + chip[1]

    def ici(couts, send_sems, recv_sems, x, y, c, chips, at):
        out = []
        for t, full in enumerate(couts[:n]):
            for p, (whose, lo, hi, to) in enumerate(MOVES):
                if p in at[t]:
                    src = region(full, 2 * x + y if whose is None else block_of(chips[whose]), c, lo, hi)
                    out.append(_rcopy(src, src, send_sems, recv_sems, len(MOVES) * t + p, (*chips[to], c)))
        return out

    def landed(couts, send_sems, recv_sems, x, y, c, chips, core, first, at=None):
        out = []
        for t, full in enumerate(couts[:n]):
            for p, (whose, lo, hi) in enumerate(ARRIVALS):
                if at is None or p in at[t]:
                    got = region(full, block_of(chips[whose]), core, lo, hi)
                    out.append(_rcopy(got, got, send_sems, recv_sems, first + len(MOVES) * t + p, (x, y, 1 - c)))
        return out

    def small(cins, couts, send_sems, recv_sems, x, y, c, chips):
        b = 2 * x + y
        return ([_rcopy(cins[n], couts[n].at[b], send_sems, recv_sems, 2 * n_ici + j, (*chip, c))
                 for j, chip in enumerate(chips)],
                pltpu.make_async_copy(cins[n], couts[n].at[b], send_sems.at[2 * n_ici + 3]))

    first_hop, second_hop = [(0, 1)] * n, [(2, 3)] * n

    def start(cins, couts, send_sems, recv_sems):
        x, y, c, chips = _place()
        if stage == "sibling":
            for cp in landed(couts, send_sems, recv_sems, x, y, c, chips, c, base):
                cp.start()
            return
        for cp in ici(couts, send_sems, recv_sems, x, y, c, chips, first_hop if stage == "all" else per_tensor):
            cp.start()
        if rbp is not None:
            remote, local = small(cins, couts, send_sems, recv_sems, x, y, c, chips)
            for cp in remote:
                cp.start()
            local.start()

    def finish(cins, couts, send_sems, recv_sems):
        x, y, c, chips = _place()
        place = (couts, send_sems, recv_sems, x, y, c, chips)
        passed = landed(*place, c, base)
        if stage == "chips":
            for cp in landed(*place, c, 0, per_tensor):
                cp.wait_recv()
            for cp in ici(*place, per_tensor):
                cp.wait_send()
        if stage == "all":
            for cp in landed(*place, c, 0, first_hop):
                cp.wait_recv()
            for cp in ici(*place, second_hop):
                cp.start()
            for cp in landed(*place, c, 0, second_hop):
                cp.wait_recv()
            for cp in passed:
                cp.start()
            for cp in ici(*place, first_hop) + ici(*place, second_hop):
                cp.wait_send()
        if stage != "chips":
            for cp in landed(*place, 1 - c, base):
                cp.wait_recv()
            for cp in passed:
                cp.wait_send()
        if rbp is not None:
            remote, local = small(cins, couts, send_sems, recv_sems, x, y, c, chips)
            for j, chip in enumerate(chips):
                got = couts[n].at[2 * chip[0] + chip[1]]
                _rcopy(got, got, send_sems, recv_sems, 2 * n_ici + j, (x, y, c)).wait_recv()
            for cp in remote:
                cp.wait_send()
            local.wait()

    outs = [jax.ShapeDtypeStruct(f.shape, f.dtype) for f in fulls]
    ins = list(fulls)
    if rbp is not None:
        ins.append(rbp)
        outs.append(jax.ShapeDtypeStruct((4,) + rbp.shape, F32))
    return _Comm(tuple(ins), tuple(outs), {t: t for t in range(n)}, 2 * n_ici + 4, start, finish)


def _chips_comm(sums):
    n = len(sums)

    def copies(cins, couts, send_sems, recv_sems):
        x, y, c, chips = _place()
        return [_rcopy(cins[t].at[2 * chip[0] + chip[1]], couts[t].at[j], send_sems, recv_sems, 3 * t + j, (*chip, c))
                for t in range(n) for j, chip in enumerate(chips)]

    def start(*refs):
        for cp in copies(*refs):
            cp.start()

    def finish(*refs):
        for cp in copies(*refs):
            cp.wait()

    outs = tuple(jax.ShapeDtypeStruct((3,) + p.shape[1:], p.dtype) for p in sums)
    return _Comm(tuple(sums), outs, {}, 3 * n, start, finish)


def _pair_comm(ins, outs, aliases, copies):
    def start(*refs):
        for cp in copies(*refs):
            cp.start()

    def finish(*refs):
        for cp in copies(*refs):
            cp.wait()

    return _Comm(tuple(ins), tuple(outs), aliases, len(ins), start, finish)


def _sibling_comm(parts):
    def copies(cins, couts, send_sems, recv_sems):
        x, y, c, _ = _place()
        return [_rcopy(p.at[:, pl.ds((1 - c) * (p.shape[1] // 2), p.shape[1] // 2), :], couts[t],
                       send_sems, recv_sems, t, (x, y, 1 - c)) for t, p in enumerate(cins)]

    half = [jax.ShapeDtypeStruct((p.shape[0], p.shape[1] // 2, p.shape[2]), p.dtype) for p in parts]
    return _pair_comm(parts, half, {}, copies)


def _halves_comm(tots):
    def copies(cins, couts, send_sems, recv_sems):
        x, y, c, _ = _place()
        return [_rcopy(g.at[c], g.at[c], send_sems, recv_sems, t, (x, y, 1 - c)) for t, g in enumerate(couts)]

    return _pair_comm(tots, [jax.ShapeDtypeStruct(t.shape, t.dtype) for t in tots],
                      {t: t for t in range(len(tots))}, copies)


def _gather_small(packed):
    def body(p_ref, all_ref, send_sems, recv_sems, loc_sems):
        x, y, c, _ = _place()
        me = 4 * x + 2 * y + c
        local = pltpu.make_async_copy(p_ref, all_ref.at[me], loc_sems.at[0])
        local.start()
        sent = []
        for k in range(1, 8):
            px, py, pc = x ^ (k >> 2), y ^ ((k >> 1) & 1), c ^ (k & 1)
            cp = _rcopy(p_ref, all_ref.at[me], send_sems, recv_sems, k - 1, (px, py, pc))
            cp.start()
            sent.append(cp)
        for k in range(1, 8):
            px, py, pc = x ^ (k >> 2), y ^ ((k >> 1) & 1), c ^ (k & 1)
            got = all_ref.at[4 * px + 2 * py + pc]
            _rcopy(got, got, send_sems, recv_sems, k - 1, (x, y, c)).wait_recv()
        for cp in sent:
            cp.wait_send()
        local.wait()

    return _comm_call(body, "gather_small", [packed], [jax.ShapeDtypeStruct((8,) + packed.shape, F32)], 7, 1)[0]


def _sum_devices(allp):
    n, r, c = allp.shape

    def body(a_ref, o_ref):
        acc = a_ref[0]
        for k in range(1, n):
            acc = acc + a_ref[k]
        o_ref[...] = acc

    return pl.pallas_call(body, name="sum_devices", out_shape=jax.ShapeDtypeStruct((r, c), F32))(allp)


def _ext_index():
    u = np.arange(EXT)
    dist = np.where(u < WIN, PAD - u, PAD + EXT - u)
    return np.clip(dist, -(CHUNK - 1), REL_CLIP) + (CHUNK - 1)


def _pack(parts, rows):
    flat = jnp.concatenate([p.reshape(-1) for p in parts])
    return jnp.pad(flat, (0, rows * 128 - flat.shape[0])).reshape(rows, 128)


def _unpack(packed, shapes):
    flat, out, at = packed.reshape(-1), [], 0
    for shp in shapes:
        size = int(np.prod(shp))
        out.append(flat[at:at + size].reshape(shp))
        at += size
    return out


def kernel(x, norm_g, w_in, q_norm_g, k_norm_g, rel_bias, w_out, loss_target, m_norm_g, m_w_in, m_q_norm_g, m_k_norm_g, m_rel_bias, m_w_out, v_norm_g, v_w_in, v_q_norm_g, v_k_norm_g, v_rel_bias, v_w_out):
    nl, d, nb = w_in.shape
    s = x.shape[1]
    ds_ = d // 2
    nh = ds_ // HEAD
    rb = w_out.shape[1]
    nrel = rel_bias.shape[2]
    bx = lax.axis_index("x") * 2 + lax.axis_index("y")

    rb_rows = -(-(nl * nh * nrel) // 1024) * 8
    cx = lax.axis_index("c")
    wi_full = [_cast_block(w_in, l, bx, "cast_w_in") for l in range(nl)]
    wo_full = [_cast_block(w_out, l, bx, "cast_w_out") for l in range(nl)]
    wi_full[0], wo_full[0], rel_all = _run_comm(
        _gather_comm([wi_full[0], wo_full[0]], _pack([rel_bias], rb_rows)), "gather_first")
    rel_full = jnp.concatenate(
        [rel_all[j].reshape(-1)[:nl * nh * nrel].reshape(nl, nh, nrel) for j in range(4)], axis=2)
    ext_idx = _ext_index()
    onehot = jnp.asarray(ext_idx[:, None] == np.arange(N_REL)[None, :], F32)
    ext = jnp.einsum("lhr,ur->lhu", rel_full, onehot, precision=lax.Precision.HIGHEST).reshape(nl, nh, 1, EXT)

    xs, hs, projs, yas, lts, ybs, mixes, wi_t = [], [], [], [], [], [], [], []
    xc = x[0]
    for l in range(nl):
        h = _rmsnorm_fwd(xc, norm_g[l:l + 1])
        nxt = l + 1 < nl
        (proj, wt), got = _in_proj(h, wi_full[l], _gather_comm(
            [wo_full[l + 1]], stage="chips", chips_at=(0, 1)) if nxt else None)
        wi_t.append(wt)
        if nxt:
            wo_full[l + 1] = got[0]
        (ya, lt), got = _attn_a_fwd(proj, nh, _gather_comm(
            [wi_full[l + 1]], stage="chips", chips_at=(0, 1)) if nxt else None)
        if nxt:
            wi_full[l + 1] = got[0]
        (yb,), got = _attn_b_fwd(proj, q_norm_g[l:l + 1], k_norm_g[l:l + 1], ext[l], nh, _gather_comm(
            [wi_full[l + 1], wo_full[l + 1]], stage="chips", chips_at=(2, 3)) if nxt else None)
        if nxt:
            wi_full[l + 1], wo_full[l + 1] = got
        xs.append(xc)
        wo_now = wo_full[l].reshape(4 * rb, d)
        (xc, mix), got = _out_proj(xc, ya, yb, proj, wo_now, _gather_comm(
            [wi_full[l + 1], wo_full[l + 1]], stage="sibling") if nxt else None)
        if nxt:
            wi_full[l + 1], wo_full[l + 1] = got
        hs.append(h); projs.append(proj); yas.append(ya); lts.append(lt); ybs.append(yb); mixes.append(mix)
    dx, loss_tile = _loss_head(xc, loss_target[0])

    small, g_wi, g_wo = [None] * nl, [None] * nl, [None] * nl
    pending = None

    def keep(lay, shared):
        g_wi[lay], g_wo[lay] = shared[0].reshape(d, nb), shared[1].reshape(rb, d)

    for l in reversed(range(nl)):
        wo = wo_full[l].reshape(4 * rb, d)
        p_wo = _wgrad(mixes[l], dx, 4, False, "wgrad_out")
        last = l == 0 and pending is not None
        (dya, dyb, dga, dgb), theirs_wo = _out_proj_bwd(dx, yas[l], ybs[l], projs[l], wo,
                                                        _sibling_comm([p_wo]) if last else None)
        travelling = list(pending[1]) if pending else []
        if last:
            sums_wo = _add_sibling(p_wo, theirs_wo[0], cx)
            travelling.append(sums_wo)
        (dqa, dka, dva), got = _attn_a_bwd(projs[l], lts[l], dya, nh, _chips_comm(travelling) if pending else None)
        tots = [_add_chips(pending[1][t], got[t], bx, cx) for t in range(2)] if pending else None
        (dqb, dkb, dvb, dqg, dkg, dext), shared = _attn_b_bwd(
            projs[l], dyb, q_norm_g[l:l + 1], k_norm_g[l:l + 1], ext[l], nh, _halves_comm(tots) if pending else None)
        if pending:
            keep(pending[0], shared)
        dproj = jnp.concatenate([dqa, dka, dva, dga, dqb, dkb, dvb, dgb], axis=1)
        parts = [_wgrad(hs[l], dproj, 4, True, "wgrad_in"), p_wo]
        if l > 0:
            (dx, dng), theirs = _in_proj_bwd(dproj, wi_t[l],xs[l], dx, norm_g[l:l + 1], _sibling_comm(parts))
            pending = (l, [_add_sibling(parts[t], theirs[t], cx) for t in range(2)])
        elif last:
            theirs = _run_comm(_sibling_comm(parts[:1]), "reduce_sibling")
            sums_wi = _add_sibling(parts[0], theirs[0], cx)
            (dx, dng), got_wi = _in_proj_bwd(dproj, wi_t[l],xs[l], dx, norm_g[l:l + 1], _chips_comm([sums_wi]))
            keep(0, _run_comm(_halves_comm([_add_chips(sums_wi, got_wi[0], bx, cx),
                                            _add_chips(sums_wo, got[2], bx, cx)]), "share_halves"))
        else:
            theirs = _run_comm(_sibling_comm(parts), "reduce_sibling")
            sums = [_add_sibling(parts[t], theirs[t], cx) for t in range(2)]
            (dx, dng), got = _in_proj_bwd(dproj, wi_t[l],xs[l], dx, norm_g[l:l + 1], _chips_comm(sums))
            keep(0, _run_comm(_halves_comm([_add_chips(sums[t], got[t], bx, cx) for t in range(2)]), "share_halves"))
        small[l] = (dng[0], jnp.sum(dqg, axis=0).reshape(-1), jnp.sum(dkg, axis=0).reshape(-1), dext.reshape(nh, EXT))
    grad_x = dx[None]

    small_shapes = [(nl, d), (nl, HEAD), (nl, HEAD), (nl, nh, EXT), (1,)]
    small_parts = [jnp.stack([sm[i] for sm in small]) for i in range(4)] + [loss_tile[0, :1]]
    rows = -(-sum(int(np.prod(sh)) for sh in small_shapes) // 1024) * 8
    tot = _sum_devices(_gather_small(_pack(small_parts, rows)))
    g_ng, g_qg, g_kg, g_ext, loss = _unpack(tot, small_shapes)
    g_rel_full = jnp.einsum("lhu,ur->lhr", g_ext, onehot, precision=lax.Precision.HIGHEST)
    g_rel = lax.dynamic_slice_in_dim(g_rel_full, bx * nrel, nrel, axis=2)

    res_wi, res_wo = (), ()
    for l in range(nl):
        res_wi = _adamw_layer(l, w_in, g_wi[l], m_w_in, v_w_in, res_wi, "adamw_w_in")
        res_wo = _adamw_layer(l, w_out, g_wo[l], m_w_out, v_w_out, res_wo, "adamw_w_out")
    g_wi, d_wi, nm_wi, nv_wi = res_wi
    g_wo, d_wo, nm_wo, nv_wo = res_wo
    sm_shapes = [(nl, d), (nl, HEAD), (nl, HEAD), (nl, nh, nrel)]
    sm_rows = -(-sum(int(np.prod(sh)) for sh in sm_shapes) // 1024) * 8
    pw, pg, pm, pv = [_pack(group, sm_rows) for group in (
        (norm_g, q_norm_g, k_norm_g, rel_bias), (g_ng, g_qg, g_kg, g_rel),
        (m_norm_g, m_q_norm_g, m_k_norm_g, m_rel_bias), (v_norm_g, v_q_norm_g, v_k_norm_g, v_rel_bias))]
    d_sm, nm_sm, nv_sm = [_unpack(a[0], sm_shapes)
                          for a in _adamw_layer(0, pw[None], pg, pm[None], pv[None], (), "adamw_small")[1:]]

    return (loss[0], grad_x, g_ng, g_wi, g_qg, g_kg, g_rel, g_wo,
            d_sm[0], d_wi, d_sm[1], d_sm[2], d_sm[3], d_wo,
            nm_sm[0], nm_wi, nm_sm[1], nm_sm[2], nm_sm[3], nm_wo,
            nv_sm[0], nv_wi, nv_sm[1], nv_sm[2], nv_sm[3], nv_wo)
```

```python
from typing import Callable, NamedTuple

import jax
import jax.numpy as jnp
import numpy as np
from jax import lax
from jax.experimental import pallas as pl
from jax.experimental.pallas import tpu as pltpu

F32 = jnp.float32
BF16 = jnp.bfloat16

HEAD = 128
CHUNK = 64
LEFT_CHUNKS = 8
REL_CLIP = 256
N_REL = REL_CLIP + CHUNK
NORM_EPS = 1e-6
NEG_BIG = -1e30
TQ = 256
TK = TQ
QB = 512
ROWS = 32
PAD = LEFT_CHUNKS * CHUNK
WIN = PAD + TQ
EXT = 1024
SCALE = HEAD ** -0.5

ADAM_LR = 0.001
ADAM_B1 = 0.9
ADAM_B2 = 0.999
ADAM_EPS = 1e-08
ADAM_WD = 0.01
ADAM_STEP = 10

ANY = pl.BlockSpec(memory_space=pl.ANY)
MESH = pl.DeviceIdType.MESH


def _params(sem=None, vmem_mb=None):
    kw = {}
    if sem is not None:
        kw["dimension_semantics"] = sem
    if vmem_mb is not None:
        kw["vmem_limit_bytes"] = vmem_mb << 20
    return pltpu.CompilerParams(**kw)


class _Comm(NamedTuple):
    ins: tuple
    outs: tuple
    aliases: dict
    n_sems: int
    start: Callable
    finish: Callable


def _call(body, *, name, grid, in_specs, out_specs, out_shape, args, scratch=(), sem=None, vmem_mb=None, comm=None):
    if comm is None:
        out = pl.pallas_call(body, name=name, grid=grid, in_specs=in_specs, out_specs=out_specs, out_shape=out_shape,
                             scratch_shapes=list(scratch), compiler_params=_params(sem, vmem_mb))(*args)
        return out, ()
    n_in, n_out, n_ci, n_co = len(in_specs), len(out_shape), len(comm.ins), len(comm.outs)

    def hosted(*refs):
        ins, cins = refs[:n_in], refs[n_in:n_in + n_ci]
        outs, couts = refs[n_in + n_ci:n_in + n_ci + n_out], refs[n_in + n_ci + n_out:n_in + n_ci + n_out + n_co]
        rest = refs[n_in + n_ci + n_out + n_co:]
        send_sems, recv_sems = rest[-2:]
        first, last = None, None
        for ax, size in enumerate(grid):
            at = pl.program_id(ax)
            first = (at == 0) if first is None else first & (at == 0)
            last = (at == size - 1) if last is None else last & (at == size - 1)

        @pl.when(first)
        def _():
            comm.start(cins, couts, send_sems, recv_sems)

        body(*ins, *outs, *rest[:-2])

        @pl.when(last)
        def _():
            comm.finish(cins, couts, send_sems, recv_sems)

    out = pl.pallas_call(
        hosted, name=name, grid=grid, in_specs=list(in_specs) + [ANY] * n_ci, out_specs=list(out_specs) + [ANY] * n_co,
        out_shape=list(out_shape) + list(comm.outs),
        input_output_aliases={n_in + k: n_out + v for k, v in comm.aliases.items()},
        scratch_shapes=list(scratch) + [pltpu.SemaphoreType.DMA((comm.n_sems,)), pltpu.SemaphoreType.DMA((comm.n_sems,))],
        compiler_params=_params(("arbitrary",) * len(grid), vmem_mb))(*args, *comm.ins)
    return out[:n_out], out[n_out:]


def _dot(a, b):
    return jnp.dot(a, b, preferred_element_type=F32)


def _dot_nt(a, b):
    return lax.dot_general(a, b, (((1,), (1,)), ((), ())), preferred_element_type=F32)


def _dot_tn(a, b):
    return lax.dot_general(a, b, (((0,), (0,)), ((), ())), preferred_element_type=F32)


def _split_dot(x, m):
    hi = x.astype(BF16)
    lo = (x - hi.astype(F32)).astype(BF16)
    return _dot(hi, m) + _dot(lo, m)


def _silu_parts(g):
    sg = 1.0 / (1.0 + jnp.exp(-g))
    return g * sg, sg * (1.0 + g * (1.0 - sg))


def _idx(*vals):
    return jnp.stack([jnp.asarray(v, jnp.int32) for v in vals])


def _cast_block(w, l, blk, name):
    _, r, c = w.shape
    tr = min(r, 512)

    def body(b_ref, w_ref, o_ref):
        o_ref[...] = w_ref[...].astype(BF16)

    spec = pltpu.PrefetchScalarGridSpec(
        num_scalar_prefetch=1, grid=(r // tr,),
        in_specs=[pl.BlockSpec((None, tr, c), lambda i, b: (l, i, 0))],
        out_specs=pl.BlockSpec((None, tr, c), lambda i, b: (b[0], i, 0)))
    return pl.pallas_call(body, name=name, grid_spec=spec, out_shape=jax.ShapeDtypeStruct((4, r, c), BF16),
                          compiler_params=_params(("parallel",)))(_idx(blk), w)


def _add_sibling(p, theirs, core):
    nblk, r, c = p.shape
    hr = r // 2
    tr = min(hr, 256)
    per = hr // tr

    def body(c_ref, p_ref, t_ref, o_ref):
        o_ref[...] = (p_ref[...].astype(F32) + t_ref[...].astype(F32)).astype(BF16)

    blk = pl.BlockSpec((None, tr, c), lambda j, i, cr: (j, i, 0))
    spec = pltpu.PrefetchScalarGridSpec(
        num_scalar_prefetch=1, grid=(nblk, per),
        in_specs=[pl.BlockSpec((None, tr, c), lambda j, i, cr: (j, cr[0] * per + i, 0)), blk], out_specs=blk)
    return pl.pallas_call(body, name="add_sibling", grid_spec=spec, out_shape=jax.ShapeDtypeStruct((nblk, hr, c), BF16),
                          compiler_params=_params(("parallel", "parallel")))(_idx(core), p, theirs)


def _add_chips(sums, got, blk, core):
    _, hr, c = sums.shape
    tr = min(hr, 256)

    def body(i_ref, s_ref, g0_ref, g1_ref, g2_ref, o_ref):
        o_ref[...] = ((s_ref[...].astype(F32) + g0_ref[...].astype(F32))
                      + g1_ref[...].astype(F32)) + g2_ref[...].astype(F32)

    at = lambda j: pl.BlockSpec((None, tr, c), lambda i, ir: (j, i, 0))
    spec = pltpu.PrefetchScalarGridSpec(
        num_scalar_prefetch=1, grid=(hr // tr,),
        in_specs=[pl.BlockSpec((None, tr, c), lambda i, ir: (ir[0], i, 0)), at(0), at(1), at(2)],
        out_specs=pl.BlockSpec((None, tr, c), lambda i, ir: (ir[1], i, 0)))
    return pl.pallas_call(body, name="add_chips", grid_spec=spec, out_shape=jax.ShapeDtypeStruct((2, hr, c), F32),
                          compiler_params=_params(("parallel",)))(_idx(blk, core), sums, got, got, got)


def _adamw_layer(l, w, g, m, v, prev, name):
    nl, r, c = w.shape
    tr = min(r, 256)
    c1 = 1.0 / (1.0 - ADAM_B1 ** ADAM_STEP)
    c2 = 1.0 / (1.0 - ADAM_B2 ** ADAM_STEP)

    def body(w_ref, g_ref, m_ref, v_ref, *rest):
        go_ref, d_ref, nm_ref, nv_ref = rest[-4:]
        gg = g_ref[...]
        nm = ADAM_B1 * m_ref[...] + (1.0 - ADAM_B1) * gg
        nv = ADAM_B2 * v_ref[...] + (1.0 - ADAM_B2) * (gg * gg)
        upd = (nm * c1) / (jnp.sqrt(nv * c2) + ADAM_EPS) + ADAM_WD * w_ref[...]
        go_ref[...] = gg
        d_ref[...] = -ADAM_LR * upd
        nm_ref[...] = nm
        nv_ref[...] = nv

    lay = pl.BlockSpec((None, tr, c), lambda i: (l, i, 0))
    shp = jax.ShapeDtypeStruct((nl, r, c), F32)
    return pl.pallas_call(
        body, name=name, grid=(r // tr,),
        in_specs=[lay, pl.BlockSpec((tr, c), lambda i: (i, 0)), lay, lay] + [ANY] * len(prev),
        out_specs=[lay] * 4, out_shape=[shp] * 4, input_output_aliases={4 + k: k for k in range(len(prev))},
        compiler_params=_params(("parallel",), 40))(w, g, m, v, *prev)


def _rmsnorm_fwd(x, g):
    s, d = x.shape
    tm = min(s, 256)

    def body(x_ref, g_ref, h_ref):
        xv = x_ref[...]
        r = lax.rsqrt(jnp.mean(xv * xv, axis=1, keepdims=True) + NORM_EPS)
        h_ref[...] = (xv * r * g_ref[...]).astype(BF16)

    return pl.pallas_call(
        body, name="rmsnorm_fwd", grid=(s // tm,),
        in_specs=[pl.BlockSpec((tm, d), lambda i: (i, 0)), pl.BlockSpec((1, d), lambda i: (0, 0))],
        out_specs=pl.BlockSpec((tm, d), lambda i: (i, 0)),
        out_shape=jax.ShapeDtypeStruct((s, d), BF16),
        compiler_params=_params(("parallel",)))(x, g)


def _in_proj(h, w, comm=None):
    s, d = h.shape
    nblk, _, nb = w.shape
    tm, tn = min(s, 1024), min(nb, 1024)
    per = nb // tn

    def body(h_ref, w_ref, o_ref, wt_ref):
        o_ref[...] = _dot(h_ref[...], w_ref[...])

        @pl.when(pl.program_id(1) == 0)
        def _():
            for c in range(tn // HEAD):
                wt_ref[pl.ds(c * HEAD, HEAD), :] = w_ref[:, c * HEAD:(c + 1) * HEAD].astype(F32).T.astype(BF16)

    return _call(
        body, name="in_proj", grid=(nblk * per, s // tm),
        in_specs=[pl.BlockSpec((tm, d), lambda n, m: (m, 0)),
                  pl.BlockSpec((None, d, tn), lambda n, m: (n // per, 0, n % per))],
        out_specs=[pl.BlockSpec((tm, tn), lambda n, m: (m, n)), pl.BlockSpec((tn, d), lambda n, m: (n, 0))],
        out_shape=[jax.ShapeDtypeStruct((s, nblk * nb), F32), jax.ShapeDtypeStruct((nblk * nb, d), BF16)],
        sem=("parallel", "arbitrary"), vmem_mb=56, args=(h, w), comm=comm)


def _heads_per_step(nh):
    return 2 if nh % 2 == 0 else 1


def _head(hh):
    return slice(hh * HEAD, (hh + 1) * HEAD)


def _tri(op):
    r = lax.broadcasted_iota(jnp.int32, (TQ, TQ), 0)
    c = lax.broadcasted_iota(jnp.int32, (TQ, TQ), 1)
    return op(r, c)


def _staggered(groups, hp):
    for hh in range(hp):
        for fn in groups[0]:
            fn(hh)
    for group in groups[1:]:
        for hh in range(hp):
            for fn in group:
                fn(hh)


def _transpose_tiles(src_ref, dst_ref, hp, nt):
    for hh in range(hp):
        for t in range(nt):
            dst_ref[hh, t] = src_ref[pl.ds(t * TK, TK), _head(hh)].T.astype(BF16)


def _chunk_mask(r, a):
    if a is None or r * ROWS >= (a + 1) * TK:
        return None
    if (r + 1) * ROWS <= a * TK:
        return False
    row = lax.broadcasted_iota(jnp.int32, (ROWS, TK), 0) + r * ROWS
    return row > lax.broadcasted_iota(jnp.int32, (ROWS, TK), 1) + a * TK


def _sb_logs(qk, causal):
    z = qk * SCALE
    l1p = jnp.log(1.0 + jnp.exp(-jnp.abs(z)))
    ls = jnp.minimum(-z, 0.0) - l1p
    if causal is not None:
        ls = jnp.where(causal, ls, 0.0)
    return ls, jnp.minimum(z, 0.0) - l1p


def _attn_a_fwd(proj, nh, comm=None):
    s = proj.shape[0]
    qb_rows = min(QB, s)
    nq, nt, per = s // qb_rows, s // TK, qb_rows // TK
    hp = _heads_per_step(nh)
    ng = nh // hp
    chunks = qb_rows // ROWS

    def body(q_ref, k_ref, v_ref, o_ref, lt_ref, acc_ref, qb_ref, m_ref, car_ref, sum_ref,
             z_ref, lsig_ref, aft_ref, hi_ref, lo_ref, w_ref, kt_ref):
        i = pl.program_id(1)

        @pl.when(i == 0)
        def _():
            _transpose_tiles(k_ref, kt_ref, hp, nt)

        qb_ref[...] = q_ref[...].astype(BF16)
        m_ref[...] = _tri(lambda r, c: r > c).astype(BF16)
        acc_ref[...] = jnp.zeros_like(acc_ref)
        car_ref[...] = jnp.zeros_like(car_ref)

        def tile(j, a):
            off = pl.multiple_of(j * TK, TK)

            def scores(hh):
                z_ref[hh] = _dot(qb_ref[:, _head(hh)], kt_ref[hh, j])

            def logs(hh):
                for r in range(chunks):
                    sl, mask = pl.ds(r * ROWS, ROWS), _chunk_mask(r, a)
                    if mask is False:
                        hi_ref[hh, sl, :] = jnp.zeros((ROWS, TK), BF16)
                        lo_ref[hh, sl, :] = jnp.zeros((ROWS, TK), BF16)
                        continue
                    ls, lsig = _sb_logs(z_ref[hh, sl, :], mask)
                    lsig_ref[hh, sl, :] = lsig
                    hi = ls.astype(BF16)
                    hi_ref[hh, sl, :] = hi
                    lo_ref[hh, sl, :] = (ls - hi.astype(F32)).astype(BF16)
                    sum_ref[hh, sl, :] = jnp.sum(ls, axis=1, keepdims=True)

            def after(hh):
                aft_ref[hh] = _dot(hi_ref[hh], m_ref[...]) + _dot(lo_ref[hh], m_ref[...])

            def weights(hh):
                for r in range(chunks):
                    sl, mask = pl.ds(r * ROWS, ROWS), _chunk_mask(r, a)
                    if mask is False:
                        w_ref[hh, sl, :] = jnp.zeros((ROWS, TK), BF16)
                        continue
                    w = jnp.exp(lsig_ref[hh, sl, :] + aft_ref[hh, sl, :] + car_ref[hh, sl, :])
                    if mask is not None:
                        w = jnp.where(mask, w, 0.0)
                    w_ref[hh, sl, :] = w.astype(BF16)
                    car_ref[hh, sl, :] += sum_ref[hh, sl, :]

            def values(hh):
                vb = v_ref[pl.ds(off, TK), _head(hh)].astype(BF16)
                acc_ref[:, _head(hh)] += _dot(w_ref[hh], vb)

            _staggered([(scores,), (logs, after), (weights, values)], hp)

        for a in reversed(range(per)):
            tile(i * per + a, a)

        def step(t, carry):
            tile(i * per - 1 - t, None)
            return carry

        lax.fori_loop(0, i * per, step, 0)
        o_ref[...] = acc_ref[...]
        lt_ref[...] = car_ref[...]

    wd = hp * HEAD
    sq = lambda dt: pltpu.VMEM((hp, qb_rows, TK), dt)
    return _call(
        body, name="attn_a_fwd", grid=(ng, nq),
        in_specs=[pl.BlockSpec((qb_rows, wd), lambda h, i: (i, h)),
                  pl.BlockSpec((s, wd), lambda h, i: (0, ng + h)),
                  pl.BlockSpec((s, wd), lambda h, i: (0, 2 * ng + h))],
        out_specs=[pl.BlockSpec((qb_rows, wd), lambda h, i: (i, h)),
                   pl.BlockSpec((hp, qb_rows, 1), lambda h, i: (h, i, 0))],
        out_shape=[jax.ShapeDtypeStruct((s, nh * HEAD), F32), jax.ShapeDtypeStruct((nh, s, 1), F32)],
        scratch=[pltpu.VMEM((qb_rows, wd), F32), pltpu.VMEM((qb_rows, wd), BF16), pltpu.VMEM((TK, TK), BF16),
                 pltpu.VMEM((hp, qb_rows, 1), F32), pltpu.VMEM((hp, qb_rows, 1), F32),
                 sq(F32), sq(F32), sq(F32), sq(BF16), sq(BF16), sq(BF16),
                 pltpu.VMEM((hp, nt, HEAD, TK), BF16)],
        sem=("parallel", "arbitrary"), args=(proj, proj, proj), comm=comm)


def _band_valid(i):
    cl = lax.broadcasted_iota(jnp.int32, (TQ, WIN), 0) // CHUNK
    kl = lax.broadcasted_iota(jnp.int32, (TQ, WIN), 1) // CHUNK
    first = LEFT_CHUNKS - (TQ // CHUNK) * i
    return (kl >= cl) & (kl <= cl + LEFT_CHUNKS) & (kl >= first)


def _build_bias(e_ref, bias_ref):
    e8 = jnp.broadcast_to(e_ref[...], (8, EXT))
    row = lax.broadcasted_iota(jnp.int32, (8, EXT), 0)
    t8 = jnp.zeros((8, EXT), F32)
    for b in range(8):
        t8 = jnp.where(row == b, pltpu.roll(e8, b, 1) if b else e8, t8)
    for a in range(TQ // 8):
        sl = pltpu.roll(t8, 8 * a, 1) if a else t8
        bias_ref[pl.ds(8 * a, 8), :] = sl[:, :WIN]


def _reduce_bias_grad(db_ref):
    acc = jnp.zeros((8, EXT), F32)
    for a in range(TQ // 8):
        sl = db_ref[pl.ds(8 * a, 8), :]
        acc = acc + (pltpu.roll(sl, EXT - 8 * a, 1) if a else sl)
    row = lax.broadcasted_iota(jnp.int32, (8, EXT), 0)
    tot = jnp.zeros((8, EXT), F32)
    for b in range(8):
        tot = tot + jnp.where(row == b, pltpu.roll(acc, EXT - b, 1) if b else acc, 0.0)
    return jnp.sum(tot, axis=0, keepdims=True)


def _band_fill(k_ref, v_ref, kg_ref, kn_pad, v_pad, s):
    k = k_ref[...]
    rk = lax.rsqrt(jnp.mean(k * k, axis=1, keepdims=True) + NORM_EPS)
    kn_pad[pl.ds(0, PAD), :] = jnp.zeros((PAD, HEAD), BF16)
    kn_pad[pl.ds(PAD, s), :] = (k * rk * kg_ref[...]).astype(BF16)
    v_pad[pl.ds(0, PAD), :] = jnp.zeros((PAD, HEAD), BF16)
    v_pad[pl.ds(PAD, s), :] = v_ref[...].astype(BF16)


def _band_probs(q_ref, qg_ref, kn_pad, bias_ref, i):
    q = q_ref[...]
    rq = lax.rsqrt(jnp.mean(q * q, axis=1, keepdims=True) + NORM_EPS)
    qhat = q * rq
    qn = (qhat * qg_ref[...]).astype(BF16)
    off = pl.multiple_of(i * TQ, TQ)
    kw = kn_pad[pl.ds(off, WIN), :]
    sc = _dot_nt(qn, kw) * SCALE + bias_ref[...]
    sc = jnp.where(_band_valid(i), sc, NEG_BIG)
    p = jnp.exp(sc - jnp.max(sc, axis=1, keepdims=True))
    pn = p / jnp.sum(p, axis=1, keepdims=True)
    return rq, qhat, qn, kw, off, pn


def _attn_b_fwd(proj, qg, kg, ext, nh, comm=None):
    s = proj.shape[0]
    nq = s // TQ

    def body(q_ref, k_ref, v_ref, qg_ref, kg_ref, e_ref, o_ref, kn_pad, v_pad, bias_ref):
        i = pl.program_id(1)

        @pl.when(i == 0)
        def _():
            _band_fill(k_ref, v_ref, kg_ref, kn_pad, v_pad, s)
            _build_bias(e_ref, bias_ref)

        _, _, _, _, off, pn = _band_probs(q_ref, qg_ref, kn_pad, bias_ref, i)
        o_ref[...] = _dot(pn.astype(BF16), v_pad[pl.ds(off, WIN), :])

    vec = pl.BlockSpec((1, HEAD), lambda h, i: (0, 0))
    return _call(
        body, name="attn_b_fwd", grid=(nh, nq),
        in_specs=[pl.BlockSpec((TQ, HEAD), lambda h, i: (i, 4 * nh + h)),
                  pl.BlockSpec((s, HEAD), lambda h, i: (0, 5 * nh + h)),
                  pl.BlockSpec((s, HEAD), lambda h, i: (0, 6 * nh + h)),
                  vec, vec,
                  pl.BlockSpec((None, 1, EXT), lambda h, i: (h, 0, 0))],
        out_specs=[pl.BlockSpec((TQ, HEAD), lambda h, i: (i, h))],
        out_shape=[jax.ShapeDtypeStruct((s, nh * HEAD), F32)],
        scratch=[pltpu.VMEM((s + PAD, HEAD), BF16), pltpu.VMEM((s + PAD, HEAD), BF16), pltpu.VMEM((TQ, WIN), F32)],
        sem=("parallel", "arbitrary"), args=(proj, proj, proj, qg, kg, ext), comm=comm)


def _out_proj(x, ya, yb, proj, w, comm=None):
    s, d = x.shape
    ds_ = ya.shape[1]
    tm = min(s, 256)

    def body(x_ref, ya_ref, yb_ref, ga_ref, gb_ref, w_ref, o_ref, mix_ref):
        ma = (ya_ref[...] * _silu_parts(ga_ref[...])[0]).astype(BF16)
        mb = (yb_ref[...] * _silu_parts(gb_ref[...])[0]).astype(BF16)
        mix_ref[:, :ds_] = ma
        mix_ref[:, ds_:] = mb
        o_ref[...] = x_ref[...] + _dot(ma, w_ref[pl.ds(0, ds_), :]) + _dot(mb, w_ref[pl.ds(ds_, ds_), :])

    row = lambda width: pl.BlockSpec((tm, width), lambda i: (i, 0))
    return _call(
        body, name="out_proj", grid=(s // tm,),
        in_specs=[row(d), row(ds_), row(ds_),
                  pl.BlockSpec((tm, ds_), lambda i: (i, 3)), pl.BlockSpec((tm, ds_), lambda i: (i, 7)),
                  pl.BlockSpec((2 * ds_, d), lambda i: (0, 0))],
        out_specs=[row(d), row(2 * ds_)],
        out_shape=[jax.ShapeDtypeStruct((s, d), F32), jax.ShapeDtypeStruct((s, 2 * ds_), BF16)],
        sem=("parallel",), vmem_mb=48, args=(x, ya, yb, proj, proj, w), comm=comm)


def _loss_head(y, tgt):
    s, d = y.shape
    tm = min(s, 256)

    def body(y_ref, t_ref, dy_ref, l_ref):
        @pl.when(pl.program_id(0) == 0)
        def _():
            l_ref[...] = jnp.zeros_like(l_ref)

        err = y_ref[...] - t_ref[...]
        dy_ref[...] = err * (1.0 / d)
        l_ref[...] += 0.5 * jnp.sum(jnp.mean(err * err, axis=1, keepdims=True), axis=0, keepdims=True)

    row = pl.BlockSpec((tm, d), lambda i: (i, 0))
    return pl.pallas_call(
        body, name="loss_head", grid=(s // tm,), in_specs=[row, row],
        out_specs=[row, pl.BlockSpec((8, 128), lambda i: (0, 0))],
        out_shape=[jax.ShapeDtypeStruct((s, d), F32), jax.ShapeDtypeStruct((8, 128), F32)],
        compiler_params=_params(("arbitrary",)))(y, tgt)


def _out_proj_bwd(dxo, ya, yb, proj, w, comm=None):
    s, d = dxo.shape
    ds_ = ya.shape[1]
    tm = min(s, 256)

    def body(dx_ref, ya_ref, yb_ref, ga_ref, gb_ref, w_ref, dya_ref, dyb_ref, dga_ref, dgb_ref):
        dxb = dx_ref[...].astype(BF16)
        for y_ref, g_ref, lo, dy_ref, dg_ref in ((ya_ref, ga_ref, 0, dya_ref, dga_ref),
                                                 (yb_ref, gb_ref, ds_, dyb_ref, dgb_ref)):
            dmix = _dot_nt(dxb, w_ref[pl.ds(lo, ds_), :])
            act, dact = _silu_parts(g_ref[...])
            dy_ref[...] = dmix * act
            dg_ref[...] = (dmix * y_ref[...] * dact).astype(BF16)

    row = lambda width: pl.BlockSpec((tm, width), lambda i: (i, 0))
    return _call(
        body, name="out_proj_bwd", grid=(s // tm,),
        in_specs=[row(d), row(ds_), row(ds_),
                  pl.BlockSpec((tm, ds_), lambda i: (i, 3)), pl.BlockSpec((tm, ds_), lambda i: (i, 7)),
                  pl.BlockSpec((2 * ds_, d), lambda i: (0, 0))],
        out_specs=[row(ds_)] * 4,
        out_shape=[jax.ShapeDtypeStruct((s, ds_), F32)] * 2 + [jax.ShapeDtypeStruct((s, ds_), BF16)] * 2,
        sem=("parallel",), vmem_mb=48, args=(dxo, ya, yb, proj, proj, w), comm=comm)


def _wgrad(a, b, nblk, col_blocks, name):
    s, m = a.shape
    n = b.shape[1]
    if col_blocks:
        tr = min(m, 1024)
        nb = n // nblk
        tn = min(nb, 2048)
        per = nb // tn
        out_shape = (nblk, m, nb)
        out_spec = pl.BlockSpec((None, tr, tn), lambda j, r: (j // per, r, j % per))
    else:
        tn = min(n, 1024)
        tr = m // nblk
        out_shape = (nblk, tr, n)
        out_spec = pl.BlockSpec((None, tr, tn), lambda j, r: (r, 0, j))

    def body(a_ref, b_ref, o_ref):
        o_ref[...] = _dot_tn(a_ref[...].astype(BF16), b_ref[...].astype(BF16)).astype(BF16)

    return pl.pallas_call(
        body, name=name, grid=(n // tn, m // tr),
        in_specs=[pl.BlockSpec((s, tr), lambda j, r: (0, r)), pl.BlockSpec((s, tn), lambda j, r: (0, j))],
        out_specs=out_spec, out_shape=jax.ShapeDtypeStruct(out_shape, BF16),
        compiler_params=_params(("parallel", "parallel"), 48))(a, b)


def _attn_a_bwd(proj, lt, dya, nh, comm=None):
    s = proj.shape[0]
    qb_rows = min(TK, s)
    nq, nt, per = s // qb_rows, s // TK, qb_rows // TK
    hp = _heads_per_step(nh)
    ng = nh // hp
    chunks = qb_rows // ROWS

    def body(q_ref, k_ref, v_ref, lt_ref, do_ref, dq_ref, dk_ref, dv_ref, dq_acc, dk_acc, dv_acc,
             qb_ref, dob_ref, upto_ref, before_ref, cls_ref, cg_ref, sls_ref, sg_ref,
             z_ref, dw_ref, lsig_ref, pre_ref, g_ref, hi_ref, lo_ref, wb_ref, kt_ref, vt_ref, qt_ref, dot_ref):
        i = pl.program_id(1)

        @pl.when(i == 0)
        def _():
            dk_acc[...] = jnp.zeros_like(dk_acc)
            dv_acc[...] = jnp.zeros_like(dv_acc)
            _transpose_tiles(k_ref, kt_ref, hp, nt)
            _transpose_tiles(v_ref, vt_ref, hp, nt)

        dq_acc[...] = jnp.zeros_like(dq_acc)
        qb_ref[...] = q_ref[...].astype(BF16)
        dob_ref[...] = do_ref[...].astype(BF16)
        for hh in range(hp):
            qt_ref[hh] = q_ref[:, _head(hh)].T.astype(BF16)
            dot_ref[hh] = do_ref[:, _head(hh)].T.astype(BF16)
        upto_ref[...] = _tri(lambda r, c: r <= c).astype(BF16)
        before_ref[...] = _tri(lambda r, c: r < c).astype(BF16)
        cls_ref[...] = jnp.zeros_like(cls_ref)
        cg_ref[...] = jnp.zeros_like(cg_ref)

        def tile(j, a):
            off = pl.multiple_of(j * TK, TK)
            zeros = jnp.zeros((ROWS, TK), BF16)

            def scores(hh):
                z_ref[hh] = _dot(qb_ref[:, _head(hh)], kt_ref[hh, j])
                dw_ref[hh] = _dot(dob_ref[:, _head(hh)], vt_ref[hh, j])

            def logs(hh):
                for r in range(chunks):
                    sl, mask = pl.ds(r * ROWS, ROWS), _chunk_mask(r, a)
                    if mask is False:
                        hi_ref[hh, sl, :] = zeros
                        lo_ref[hh, sl, :] = zeros
                        continue
                    ls, lsig = _sb_logs(z_ref[hh, sl, :], mask)
                    lsig_ref[hh, sl, :] = lsig
                    hi = ls.astype(BF16)
                    hi_ref[hh, sl, :] = hi
                    lo_ref[hh, sl, :] = (ls - hi.astype(F32)).astype(BF16)
                    sls_ref[hh, sl, :] = jnp.sum(ls, axis=1, keepdims=True)

            def upto(hh):
                pre_ref[hh] = _dot(hi_ref[hh], upto_ref[...]) + _dot(lo_ref[hh], upto_ref[...])

            def weights(hh):
                for r in range(chunks):
                    sl, mask = pl.ds(r * ROWS, ROWS), _chunk_mask(r, a)
                    if mask is False:
                        wb_ref[hh, sl, :] = zeros
                        hi_ref[hh, sl, :] = zeros
                        continue
                    w = jnp.exp(lsig_ref[hh, sl, :] + (lt_ref[hh, sl, :] - (cls_ref[hh, sl, :] + pre_ref[hh, sl, :])))
                    if mask is not None:
                        w = jnp.where(mask, w, 0.0)
                    wb_ref[hh, sl, :] = w.astype(BF16)
                    g = w * dw_ref[hh, sl, :]
                    g_ref[hh, sl, :] = g
                    hi_ref[hh, sl, :] = g.astype(BF16)
                    sg_ref[hh, sl, :] = jnp.sum(g, axis=1, keepdims=True)

            def earlier(hh):
                dw_ref[hh] = _dot(hi_ref[hh], before_ref[...])

            def logit_grads(hh):
                for r in range(chunks):
                    sl, mask = pl.ds(r * ROWS, ROWS), _chunk_mask(r, a)
                    if mask is False:
                        lo_ref[hh, sl, :] = zeros
                        continue
                    z = z_ref[hh, sl, :] * SCALE
                    e = jnp.exp(-jnp.abs(z))
                    rinv = 1.0 / (1.0 + e)
                    beta = jnp.where(z >= 0.0, rinv, e * rinv)
                    dz = g_ref[hh, sl, :] * (1.0 - beta) - beta * (cg_ref[hh, sl, :] + dw_ref[hh, sl, :])
                    if mask is not None:
                        dz = jnp.where(mask, dz, 0.0)
                    lo_ref[hh, sl, :] = (dz * SCALE).astype(BF16)
                    cls_ref[hh, sl, :] += sls_ref[hh, sl, :]
                    cg_ref[hh, sl, :] += sg_ref[hh, sl, :]

            def grads(hh):
                dq_acc[:, _head(hh)] += _dot(lo_ref[hh], k_ref[pl.ds(off, TK), _head(hh)].astype(BF16))
                dk_acc[hh, j] += _dot(qt_ref[hh], lo_ref[hh])
                dv_acc[hh, j] += _dot(dot_ref[hh], wb_ref[hh])

            _staggered([(scores,), (logs, upto), (weights, earlier), (logit_grads, grads)], hp)

        def step(j, carry):
            tile(j, None)
            return carry

        lax.fori_loop(0, i * per, step, 0)
        for a in range(per):
            tile(i * per + a, a)
        dq_ref[...] = dq_acc[...].astype(BF16)

        @pl.when(i == nq - 1)
        def _():
            for hh in range(hp):
                for t in range(nt):
                    dk_ref[pl.ds(t * TK, TK), _head(hh)] = dk_acc[hh, t].T.astype(BF16)
                    dv_ref[pl.ds(t * TK, TK), _head(hh)] = dv_acc[hh, t].T.astype(BF16)

    wd = hp * HEAD
    sq = lambda dt: pltpu.VMEM((hp, qb_rows, TK), dt)
    tiles = lambda dt: pltpu.VMEM((hp, nt, HEAD, TK), dt)
    blk = pl.BlockSpec((qb_rows, wd), lambda h, i: (i, h))
    col = pl.BlockSpec((s, wd), lambda h, i: (0, h))
    shp = jax.ShapeDtypeStruct((s, nh * HEAD), BF16)
    return _call(
        body, name="attn_a_bwd", grid=(ng, nq),
        in_specs=[blk,
                  pl.BlockSpec((s, wd), lambda h, i: (0, ng + h)),
                  pl.BlockSpec((s, wd), lambda h, i: (0, 2 * ng + h)),
                  pl.BlockSpec((hp, qb_rows, 1), lambda h, i: (h, i, 0)), blk],
        out_specs=[blk, col, col], out_shape=[shp] * 3,
        scratch=[pltpu.VMEM((qb_rows, wd), F32), tiles(F32), tiles(F32),
                 pltpu.VMEM((qb_rows, wd), BF16), pltpu.VMEM((qb_rows, wd), BF16),
                 pltpu.VMEM((TK, TK), BF16), pltpu.VMEM((TK, TK), BF16)]
        + [pltpu.VMEM((hp, qb_rows, 1), F32)] * 4 + [sq(F32)] * 5 + [sq(BF16)] * 3
        + [tiles(BF16), tiles(BF16), pltpu.VMEM((hp, HEAD, qb_rows), BF16), pltpu.VMEM((hp, HEAD, qb_rows), BF16)],
        sem=("parallel", "arbitrary"), args=(proj, proj, proj, lt, dya), comm=comm)


def _attn_b_bwd(proj, dyb, qg, kg, ext, nh, comm=None):
    s = proj.shape[0]
    nq = s // TQ

    def body(q_ref, k_ref, v_ref, do_ref, qg_ref, kg_ref, e_ref,
             dq_ref, dk_ref, dv_ref, dqg_ref, dkg_ref, de_ref,
             kn_pad, v_pad, bias_ref, db_acc, dkn_acc, dv_acc):
        i = pl.program_id(1)

        @pl.when(i == 0)
        def _():
            _band_fill(k_ref, v_ref, kg_ref, kn_pad, v_pad, s)
            _build_bias(e_ref, bias_ref)
            db_acc[...] = jnp.zeros_like(db_acc)
            dkn_acc[...] = jnp.zeros_like(dkn_acc)
            dv_acc[...] = jnp.zeros_like(dv_acc)
            dqg_ref[...] = jnp.zeros_like(dqg_ref)

        rq, qhat, qn, kw, off, pn = _band_probs(q_ref, qg_ref, kn_pad, bias_ref, i)
        dob = do_ref[...].astype(BF16)
        dp = _dot_nt(dob, v_pad[pl.ds(off, WIN), :])
        dsc = pn * (dp - jnp.sum(pn * dp, axis=1, keepdims=True))
        db_acc[:, :WIN] += dsc
        dsb = (dsc * SCALE).astype(BF16)
        dqn = _dot(dsb, kw)
        dkn_acc[pl.ds(off, WIN), :] += _dot_tn(dsb, qn)
        dv_acc[pl.ds(off, WIN), :] += _dot_tn(pn.astype(BF16), dob)
        dqh = dqn * qg_ref[...]
        dq_ref[...] = (rq * (dqh - qhat * jnp.mean(dqh * qhat, axis=1, keepdims=True))).astype(BF16)
        dqg_ref[...] += jnp.sum(dqn * qhat, axis=0, keepdims=True)

        @pl.when(i == nq - 1)
        def _():
            k = k_ref[...]
            rk = lax.rsqrt(jnp.mean(k * k, axis=1, keepdims=True) + NORM_EPS)
            khat = k * rk
            dkn = dkn_acc[pl.ds(PAD, s), :]
            dkh = dkn * kg_ref[...]
            dk_ref[...] = (rk * (dkh - khat * jnp.mean(dkh * khat, axis=1, keepdims=True))).astype(BF16)
            dkg_ref[...] = jnp.sum(dkn * khat, axis=0, keepdims=True)
            dv_ref[...] = dv_acc[pl.ds(PAD, s), :].astype(BF16)
            de_ref[...] = _reduce_bias_grad(db_acc)

    blk = pl.BlockSpec((TQ, HEAD), lambda h, i: (i, h))
    col = pl.BlockSpec((s, HEAD), lambda h, i: (0, h))
    vec = pl.BlockSpec((1, HEAD), lambda h, i: (0, 0))
    hvec = pl.BlockSpec((None, 1, HEAD), lambda h, i: (h, 0, 0))
    hext = pl.BlockSpec((None, 1, EXT), lambda h, i: (h, 0, 0))
    shp = jax.ShapeDtypeStruct((s, nh * HEAD), BF16)
    return _call(
        body, name="attn_b_bwd", grid=(nh, nq),
        in_specs=[pl.BlockSpec((TQ, HEAD), lambda h, i: (i, 4 * nh + h)),
                  pl.BlockSpec((s, HEAD), lambda h, i: (0, 5 * nh + h)),
                  pl.BlockSpec((s, HEAD), lambda h, i: (0, 6 * nh + h)),
                  blk, vec, vec, hext],
        out_specs=[blk, col, col, hvec, hvec, hext],
        out_shape=[shp] * 3 + [jax.ShapeDtypeStruct((nh, 1, HEAD), F32)] * 2
        + [jax.ShapeDtypeStruct((nh, 1, EXT), F32)],
        scratch=[pltpu.VMEM((s + PAD, HEAD), BF16), pltpu.VMEM((s + PAD, HEAD), BF16),
                 pltpu.VMEM((TQ, WIN), F32), pltpu.VMEM((TQ, EXT), F32),
                 pltpu.VMEM((s + PAD, HEAD), F32), pltpu.VMEM((s + PAD, HEAD), F32)],
        sem=("parallel", "arbitrary"), args=(proj, proj, proj, dyb, qg, kg, ext), comm=comm)


def _in_proj_bwd(dproj, wt, x, dxo, g, comm=None):
    s, d = x.shape
    tm, tk = min(s, 512), min(wt.shape[0], 1024)
    nk = wt.shape[0] // tk

    def body(dp_ref, w_ref, x_ref, dxo_ref, g_ref, dx_ref, dg_ref, acc):
        m, k = pl.program_id(0), pl.program_id(1)

        @pl.when(k == 0)
        def _():
            acc[...] = jnp.zeros_like(acc)

        @pl.when((k == 0) & (m == 0))
        def _():
            dg_ref[...] = jnp.zeros_like(dg_ref)

        acc[...] += _dot(dp_ref[...], w_ref[...])

        @pl.when(k == nk - 1)
        def _():
            xv = x_ref[...]
            r = lax.rsqrt(jnp.mean(xv * xv, axis=1, keepdims=True) + NORM_EPS)
            xhat = xv * r
            dh = acc[...]
            dxh = dh * g_ref[...]
            dx_ref[...] = dxo_ref[...] + r * (dxh - xhat * jnp.mean(dxh * xhat, axis=1, keepdims=True))
            dg_ref[...] += jnp.sum(dh * xhat, axis=0, keepdims=True)

    row = pl.BlockSpec((tm, d), lambda m, k: (m, 0))
    return _call(
        body, name="in_proj_bwd", grid=(s // tm, nk),
        in_specs=[pl.BlockSpec((tm, tk), lambda m, k: (m, k)),
                  pl.BlockSpec((tk, d), lambda m, k: (k, 0)),
                  row, row, pl.BlockSpec((1, d), lambda m, k: (0, 0))],
        out_specs=[row, pl.BlockSpec((8, d), lambda m, k: (0, 0))],
        out_shape=[jax.ShapeDtypeStruct((s, d), F32), jax.ShapeDtypeStruct((8, d), F32)],
        scratch=[pltpu.VMEM((tm, d), F32)], sem=("arbitrary", "arbitrary"), vmem_mb=56,
        args=(dproj, wt, x, dxo, g), comm=comm)


def _place():
    x, y, c = lax.axis_index("x"), lax.axis_index("y"), lax.axis_index("c")
    chips = [(1 - x, y), (x, 1 - y), (1 - x, 1 - y)]
    return x, y, c, chips


def _comm_call(body, name, ins, out_shape, n_remote, n_local, aliases=None):
    return pl.pallas_call(
        body, name=name, in_specs=[ANY] * len(ins), out_specs=[ANY] * len(out_shape), out_shape=out_shape,
        input_output_aliases=aliases or {},
        scratch_shapes=[pltpu.SemaphoreType.DMA((n_remote,)), pltpu.SemaphoreType.DMA((n_remote,)),
                        pltpu.SemaphoreType.DMA((n_local,))])(*ins)


def _rcopy(src, dst, send_sems, recv_sems, k, dev):
    return pltpu.make_async_remote_copy(src_ref=src, dst_ref=dst, send_sem=send_sems.at[k], recv_sem=recv_sems.at[k],
                                        device_id=dev, device_id_type=MESH)


def _run_comm(comm, name):
    n_ci, n_co = len(comm.ins), len(comm.outs)

    def body(*refs):
        cins, couts = refs[:n_ci], refs[n_ci:n_ci + n_co]
        send_sems, recv_sems = refs[n_ci + n_co:]
        comm.start(cins, couts, send_sems, recv_sems)
        comm.finish(cins, couts, send_sems, recv_sems)

    return pl.pallas_call(
        body, name=name, in_specs=[ANY] * n_ci, out_specs=[ANY] * n_co, out_shape=list(comm.outs),
        input_output_aliases=dict(comm.aliases),
        scratch_shapes=[pltpu.SemaphoreType.DMA((comm.n_sems,)), pltpu.SemaphoreType.DMA((comm.n_sems,))])(*comm.ins)


MOVES = ((None, 0, 2, 0), (None, 0, 2, 1), (0, 0, 1, 1), (1, 1, 2, 0))
ARRIVALS = ((0, 0, 2), (1, 0, 2), (2, 0, 1), (2, 1, 2))


def _gather_comm(fulls, rbp=None, stage="all", chips_at=(0, 1, 2, 3), pass_on=None):
    n = len(fulls)
    n_ici = len(MOVES) * n
    per_tensor = chips_at if isinstance(chips_at, list) else [chips_at] * n

    def region(full, blk, core, lo, hi):
        quarter = full.shape[1] // 4
        return full.at[blk].at[pl.ds(core * 2 * quarter + lo * quarter, (hi - lo) * quarter)]

    def block_of(chip):
        return 2 * chip[0] + chip[1]

    def ici(couts, send_sems, recv_sems, x, y, c, chips, at):
        out = []
        for t, full in enumerate(couts[:n]):
            for p, (whose, lo, hi, to) in enumerate(MOVES):
                if p in at[t]:
                    src = region(full, 2 * x + y if whose is None else block_of(chips[whose]), c, lo, hi)
                    out.append(_rcopy(src, src, send_sems, recv_sems, len(MOVES) * t + p, (*chips[to], c)))
        return out

    def landed(couts, send_sems, recv_sems, x, y, c, chips, core, first, at=None):
        out = []
        for t, full in enumerate(couts[:n]):
            for p, (whose, lo, hi) in enumerate(ARRIVALS):
                if at is None or p in at[t]:
                    got = region(full, block_of(chips[whose]), core, lo, hi)
                    out.append(_rcopy(got, got, send_sems, recv_sems, first + len(MOVES) * t + p, (x, y, 1 - c)))
        return out

    def small(cins, couts, send_sems, recv_sems, x, y, c, chips):
        b = 2 * x + y
        return ([_rcopy(cins[n], couts[n].at[b], send_sems, recv_sems, 2 * n_ici + j, (*chip, c))
                 for j, chip in enumerate(chips)],
                pltpu.make_async_copy(cins[n], couts[n].at[b], send_sems.at[2 * n_ici + 3]))

    first_hop, second_hop = [(0, 1)] * n, [(2, 3)] * n
    to_sibling = [pass_on] * n if pass_on is not None else (None if stage != "chips" else [()] * n)

    def start(cins, couts, send_sems, recv_sems):
        x, y, c, chips = _place()
        if stage != "all":
            for cp in landed(couts, send_sems, recv_sems, x, y, c, chips, c, n_ici, to_sibling):
                cp.start()
        if stage == "sibling":
            return
        for cp in ici(couts, send_sems, recv_sems, x, y, c, chips, first_hop if stage == "all" else per_tensor):
            cp.start()
        if rbp is not None:
            remote, local = small(cins, couts, send_sems, recv_sems, x, y, c, chips)
            for cp in remote:
                cp.start()
            local.start()

    def finish(cins, couts, send_sems, recv_sems):
        x, y, c, chips = _place()
        place = (couts, send_sems, recv_sems, x, y, c, chips)
        passed = landed(*place, c, n_ici, to_sibling)
        if stage == "chips":
            for cp in landed(*place, c, 0, per_tensor):
                cp.wait_recv()
            for cp in ici(*place, per_tensor):
                cp.wait_send()
        if stage == "all":
            for cp in landed(*place, c, 0, first_hop):
                cp.wait_recv()
            for cp in ici(*place, second_hop):
                cp.start()
            for cp in landed(*place, c, 0, second_hop):
                cp.wait_recv()
            for cp in passed:
                cp.start()
            for cp in ici(*place, first_hop) + ici(*place, second_hop):
                cp.wait_send()
        for cp in landed(*place, 1 - c, n_ici, to_sibling):
            cp.wait_recv()
        for cp in passed:
            cp.wait_send()
        if rbp is not None:
            remote, local = small(cins, couts, send_sems, recv_sems, x, y, c, chips)
            for j, chip in enumerate(chips):
                got = couts[n].at[2 * chip[0] + chip[1]]
                _rcopy(got, got, send_sems, recv_sems, 2 * n_ici + j, (x, y, c)).wait_recv()
            for cp in remote:
                cp.wait_send()
            local.wait()

    outs = [jax.ShapeDtypeStruct(f.shape, f.dtype) for f in fulls]
    ins = list(fulls)
    if rbp is not None:
        ins.append(rbp)
        outs.append(jax.ShapeDtypeStruct((4,) + rbp.shape, F32))
    return _Comm(tuple(ins), tuple(outs), {t: t for t in range(n)}, 2 * n_ici + 4, start, finish)


def _chips_comm(sums):
    n = len(sums)

    def copies(cins, couts, send_sems, recv_sems):
        x, y, c, chips = _place()
        return [_rcopy(cins[t].at[2 * chip[0] + chip[1]], couts[t].at[j], send_sems, recv_sems, 3 * t + j, (*chip, c))
                for t in range(n) for j, chip in enumerate(chips)]

    def start(*refs):
        for cp in copies(*refs):
            cp.start()

    def finish(*refs):
        for cp in copies(*refs):
            cp.wait()

    outs = tuple(jax.ShapeDtypeStruct((3,) + p.shape[1:], p.dtype) for p in sums)
    return _Comm(tuple(sums), outs, {}, 3 * n, start, finish)


def _pair_comm(ins, outs, aliases, copies):
    def start(*refs):
        for cp in copies(*refs):
            cp.start()

    def finish(*refs):
        for cp in copies(*refs):
            cp.wait()

    return _Comm(tuple(ins), tuple(outs), aliases, len(ins), start, finish)


def _sibling_comm(parts):
    def copies(cins, couts, send_sems, recv_sems):
        x, y, c, _ = _place()
        return [_rcopy(p.at[:, pl.ds((1 - c) * (p.shape[1] // 2), p.shape[1] // 2), :], couts[t],
                       send_sems, recv_sems, t, (x, y, 1 - c)) for t, p in enumerate(cins)]

    half = [jax.ShapeDtypeStruct((p.shape[0], p.shape[1] // 2, p.shape[2]), p.dtype) for p in parts]
    return _pair_comm(parts, half, {}, copies)


def _halves_comm(tots):
    def copies(cins, couts, send_sems, recv_sems):
        x, y, c, _ = _place()
        return [_rcopy(g.at[c], g.at[c], send_sems, recv_sems, t, (x, y, 1 - c)) for t, g in enumerate(couts)]

    return _pair_comm(tots, [jax.ShapeDtypeStruct(t.shape, t.dtype) for t in tots],
                      {t: t for t in range(len(tots))}, copies)


def _gather_small(packed):
    def body(p_ref, all_ref, send_sems, recv_sems, loc_sems):
        x, y, c, _ = _place()
        me = 4 * x + 2 * y + c
        local = pltpu.make_async_copy(p_ref, all_ref.at[me], loc_sems.at[0])
        local.start()
        sent = []
        for k in range(1, 8):
            px, py, pc = x ^ (k >> 2), y ^ ((k >> 1) & 1), c ^ (k & 1)
            cp = _rcopy(p_ref, all_ref.at[me], send_sems, recv_sems, k - 1, (px, py, pc))
            cp.start()
            sent.append(cp)
        for k in range(1, 8):
            px, py, pc = x ^ (k >> 2), y ^ ((k >> 1) & 1), c ^ (k & 1)
            got = all_ref.at[4 * px + 2 * py + pc]
            _rcopy(got, got, send_sems, recv_sems, k - 1, (x, y, c)).wait_recv()
        for cp in sent:
            cp.wait_send()
        local.wait()

    return _comm_call(body, "gather_small", [packed], [jax.ShapeDtypeStruct((8,) + packed.shape, F32)], 7, 1)[0]


def _sum_devices(allp):
    n, r, c = allp.shape

    def body(a_ref, o_ref):
        acc = a_ref[0]
        for k in range(1, n):
            acc = acc + a_ref[k]
        o_ref[...] = acc

    return pl.pallas_call(body, name="sum_devices", out_shape=jax.ShapeDtypeStruct((r, c), F32))(allp)


def _ext_index():
    u = np.arange(EXT)
    dist = np.where(u < WIN, PAD - u, PAD + EXT - u)
    return np.clip(dist, -(CHUNK - 1), REL_CLIP) + (CHUNK - 1)


def _pack(parts, rows):
    flat = jnp.concatenate([p.reshape(-1) for p in parts])
    return jnp.pad(flat, (0, rows * 128 - flat.shape[0])).reshape(rows, 128)


def _unpack(packed, shapes):
    flat, out, at = packed.reshape(-1), [], 0
    for shp in shapes:
        size = int(np.prod(shp))
        out.append(flat[at:at + size].reshape(shp))
        at += size
    return out


def kernel(x, norm_g, w_in, q_norm_g, k_norm_g, rel_bias, w_out, loss_target, m_norm_g, m_w_in, m_q_norm_g, m_k_norm_g, m_rel_bias, m_w_out, v_norm_g, v_w_in, v_q_norm_g, v_k_norm_g, v_rel_bias, v_w_out):
    nl, d, nb = w_in.shape
    s = x.shape[1]
    ds_ = d // 2
    nh = ds_ // HEAD
    rb = w_out.shape[1]
    nrel = rel_bias.shape[2]
    bx = lax.axis_index("x") * 2 + lax.axis_index("y")

    rb_rows = -(-(nl * nh * nrel) // 1024) * 8
    cx = lax.axis_index("c")
    wi_full = [_cast_block(w_in, l, bx, "cast_w_in") for l in range(nl)]
    wo_full = [_cast_block(w_out, l, bx, "cast_w_out") for l in range(nl)]
    wi_full[0], wo_full[0], rel_all = _run_comm(
        _gather_comm([wi_full[0], wo_full[0]], _pack([rel_bias], rb_rows)), "gather_first")
    rel_full = jnp.concatenate(
        [rel_all[j].reshape(-1)[:nl * nh * nrel].reshape(nl, nh, nrel) for j in range(4)], axis=2)
    ext_idx = _ext_index()
    onehot = jnp.asarray(ext_idx[:, None] == np.arange(N_REL)[None, :], F32)
    ext = jnp.einsum("lhr,ur->lhu", rel_full, onehot, precision=lax.Precision.HIGHEST).reshape(nl, nh, 1, EXT)

    xs, hs, projs, yas, lts, ybs, mixes, wi_t = [], [], [], [], [], [], [], []
    xc = x[0]
    for l in range(nl):
        h = _rmsnorm_fwd(xc, norm_g[l:l + 1])
        nxt = l + 1 < nl
        (proj, wt), got = _in_proj(h, wi_full[l], _gather_comm(
            [wo_full[l + 1]], stage="chips", chips_at=(0, 1)) if nxt else None)
        wi_t.append(wt)
        if nxt:
            wo_full[l + 1] = got[0]
        (ya, lt), got = _attn_a_fwd(proj, nh, _gather_comm(
            [wi_full[l + 1]], stage="chips", chips_at=(0, 1)) if nxt else None)
        if nxt:
            wi_full[l + 1] = got[0]
        (yb,), got = _attn_b_fwd(proj, q_norm_g[l:l + 1], k_norm_g[l:l + 1], ext[l], nh, _gather_comm(
            [wi_full[l + 1], wo_full[l + 1]], stage="chips", chips_at=(2, 3), pass_on=(0, 1)) if nxt else None)
        if nxt:
            wi_full[l + 1], wo_full[l + 1] = got
        xs.append(xc)
        wo_now = wo_full[l].reshape(4 * rb, d)
        (xc, mix), got = _out_proj(xc, ya, yb, proj, wo_now, _gather_comm(
            [wi_full[l + 1], wo_full[l + 1]], stage="sibling", pass_on=(2, 3)) if nxt else None)
        if nxt:
            wi_full[l + 1], wo_full[l + 1] = got
        hs.append(h); projs.append(proj); yas.append(ya); lts.append(lt); ybs.append(yb); mixes.append(mix)
    dx, loss_tile = _loss_head(xc, loss_target[0])

    small, g_wi, g_wo = [None] * nl, [None] * nl, [None] * nl
    pending = None

    def keep(lay, shared):
        g_wi[lay], g_wo[lay] = shared[0].reshape(d, nb), shared[1].reshape(rb, d)

    for l in reversed(range(nl)):
        wo = wo_full[l].reshape(4 * rb, d)
        p_wo = _wgrad(mixes[l], dx, 4, False, "wgrad_out")
        last = l == 0 and pending is not None
        (dya, dyb, dga, dgb), theirs_wo = _out_proj_bwd(dx, yas[l], ybs[l], projs[l], wo,
                                                        _sibling_comm([p_wo]) if last else None)
        travelling = list(pending[1]) if pending else []
        if last:
            sums_wo = _add_sibling(p_wo, theirs_wo[0], cx)
            travelling.append(sums_wo)
        (dqa, dka, dva), got = _attn_a_bwd(projs[l], lts[l], dya, nh, _chips_comm(travelling) if pending else None)
        tots = [_add_chips(pending[1][t], got[t], bx, cx) for t in range(2)] if pending else None
        (dqb, dkb, dvb, dqg, dkg, dext), shared = _attn_b_bwd(
            projs[l], dyb, q_norm_g[l:l + 1], k_norm_g[l:l + 1], ext[l], nh, _halves_comm(tots) if pending else None)
        if pending:
            keep(pending[0], shared)
        dproj = jnp.concatenate([dqa, dka, dva, dga, dqb, dkb, dvb, dgb], axis=1)
        parts = [_wgrad(hs[l], dproj, 4, True, "wgrad_in"), p_wo]
        if l > 0:
            (dx, dng), theirs = _in_proj_bwd(dproj, wi_t[l],xs[l], dx, norm_g[l:l + 1], _sibling_comm(parts))
            pending = (l, [_add_sibling(parts[t], theirs[t], cx) for t in range(2)])
        elif last:
            theirs = _run_comm(_sibling_comm(parts[:1]), "reduce_sibling")
            sums_wi = _add_sibling(parts[0], theirs[0], cx)
            (dx, dng), got_wi = _in_proj_bwd(dproj, wi_t[l],xs[l], dx, norm_g[l:l + 1], _chips_comm([sums_wi]))
            keep(0, _run_comm(_halves_comm([_add_chips(sums_wi, got_wi[0], bx, cx),
                                            _add_chips(sums_wo, got[2], bx, cx)]), "share_halves"))
        else:
            theirs = _run_comm(_sibling_comm(parts), "reduce_sibling")
            sums = [_add_sibling(parts[t], theirs[t], cx) for t in range(2)]
            (dx, dng), got = _in_proj_bwd(dproj, wi_t[l],xs[l], dx, norm_g[l:l + 1], _chips_comm(sums))
            keep(0, _run_comm(_halves_comm([_add_chips(sums[t], got[t], bx, cx) for t in range(2)]), "share_halves"))
        small[l] = (dng[0], jnp.sum(dqg, axis=0).reshape(-1), jnp.sum(dkg, axis=0).reshape(-1), dext.reshape(nh, EXT))
    grad_x = dx[None]

    small_shapes = [(nl, d), (nl, HEAD), (nl, HEAD), (nl, nh, EXT), (1,)]
    small_parts = [jnp.stack([sm[i] for sm in small]) for i in range(4)] + [loss_tile[0, :1]]
    rows = -(-sum(int(np.prod(sh)) for sh in small_shapes) // 1024) * 8
    tot = _sum_devices(_gather_small(_pack(small_parts, rows)))
    g_ng, g_qg, g_kg, g_ext, loss = _unpack(tot, small_shapes)
    g_rel_full = jnp.einsum("lhu,ur->lhr", g_ext, onehot, precision=lax.Precision.HIGHEST)
    g_rel = lax.dynamic_slice_in_dim(g_rel_full, bx * nrel, nrel, axis=2)

    res_wi, res_wo = (), ()
    for l in range(nl):
        res_wi = _adamw_layer(l, w_in, g_wi[l], m_w_in, v_w_in, res_wi, "adamw_w_in")
        res_wo = _adamw_layer(l, w_out, g_wo[l], m_w_out, v_w_out, res_wo, "adamw_w_out")
    g_wi, d_wi, nm_wi, nv_wi = res_wi
    g_wo, d_wo, nm_wo, nv_wo = res_wo
    sm_shapes = [(nl, d), (nl, HEAD), (nl, HEAD), (nl, nh, nrel)]
    sm_rows = -(-sum(int(np.prod(sh)) for sh in sm_shapes) // 1024) * 8
    pw, pg, pm, pv = [_pack(group, sm_rows) for group in (
        (norm_g, q_norm_g, k_norm_g, rel_bias), (g_ng, g_qg, g_kg, g_rel),
        (m_norm_g, m_q_norm_g, m_k_norm_g, m_rel_bias), (v_norm_g, v_q_norm_g, v_k_norm_g, v_rel_bias))]
    d_sm, nm_sm, nv_sm = [_unpack(a[0], sm_shapes)
                          for a in _adamw_layer(0, pw[None], pg, pm[None], pv[None], (), "adamw_small")[1:]]

    return (loss[0], grad_x, g_ng, g_wi, g_qg, g_kg, g_rel, g_wo,
            d_sm[0], d_wi, d_sm[1], d_sm[2], d_sm[3], d_wo,
            nm_sm[0], nm_wi, nm_sm[1], nm_sm[2], nm_sm[3], nm_wo,
            nv_sm[0], nv_wi, nv_sm[1], nv_sm[2], nv_sm[3], nv_wo)
```

```python
from typing import Callable, NamedTuple

import jax
import jax.numpy as jnp
import numpy as np
from jax import lax
from jax.experimental import pallas as pl
from jax.experimental.pallas import tpu as pltpu

F32 = jnp.float32
BF16 = jnp.bfloat16

HEAD = 128
CHUNK = 64
LEFT_CHUNKS = 8
REL_CLIP = 256
N_REL = REL_CLIP + CHUNK
NORM_EPS = 1e-6
NEG_BIG = -1e30
TQ = 256
TK = TQ
QB = 512
ROWS = 32
PAD = LEFT_CHUNKS * CHUNK
WIN = PAD + TQ
EXT = 1024
SCALE = HEAD ** -0.5

ADAM_LR = 0.001
ADAM_B1 = 0.9
ADAM_B2 = 0.999
ADAM_EPS = 1e-08
ADAM_WD = 0.01
ADAM_STEP = 10

ANY = pl.BlockSpec(memory_space=pl.ANY)
MESH = pl.DeviceIdType.MESH


def _params(sem=None, vmem_mb=None):
    kw = {}
    if sem is not None:
        kw["dimension_semantics"] = sem
    if vmem_mb is not None:
        kw["vmem_limit_bytes"] = vmem_mb << 20
    return pltpu.CompilerParams(**kw)


class _Comm(NamedTuple):
    ins: tuple
    outs: tuple
    aliases: dict
    n_sems: int
    start: Callable
    finish: Callable


def _call(body, *, name, grid, in_specs, out_specs, out_shape, args, scratch=(), sem=None, vmem_mb=None, comm=None,
          aliases=None):
    aliases = dict(aliases or {})
    if comm is None:
        out = pl.pallas_call(body, name=name, grid=grid, in_specs=in_specs, out_specs=out_specs, out_shape=out_shape,
                             input_output_aliases=aliases, scratch_shapes=list(scratch),
                             compiler_params=_params(sem, vmem_mb))(*args)
        return out, ()
    n_in, n_out, n_ci, n_co = len(in_specs), len(out_shape), len(comm.ins), len(comm.outs)

    def hosted(*refs):
        ins, cins = refs[:n_in], refs[n_in:n_in + n_ci]
        outs, couts = refs[n_in + n_ci:n_in + n_ci + n_out], refs[n_in + n_ci + n_out:n_in + n_ci + n_out + n_co]
        rest = refs[n_in + n_ci + n_out + n_co:]
        send_sems, recv_sems = rest[-2:]
        first, last = None, None
        for ax, size in enumerate(grid):
            at = pl.program_id(ax)
            first = (at == 0) if first is None else first & (at == 0)
            last = (at == size - 1) if last is None else last & (at == size - 1)

        @pl.when(first)
        def _():
            comm.start(cins, couts, send_sems, recv_sems)

        body(*ins, *outs, *rest[:-2])

        @pl.when(last)
        def _():
            comm.finish(cins, couts, send_sems, recv_sems)

    out = pl.pallas_call(
        hosted, name=name, grid=grid, in_specs=list(in_specs) + [ANY] * n_ci, out_specs=list(out_specs) + [ANY] * n_co,
        out_shape=list(out_shape) + list(comm.outs),
        input_output_aliases={**aliases, **{n_in + k: n_out + v for k, v in comm.aliases.items()}},
        scratch_shapes=list(scratch) + [pltpu.SemaphoreType.DMA((comm.n_sems,)), pltpu.SemaphoreType.DMA((comm.n_sems,))],
        compiler_params=_params(("arbitrary",) * len(grid), vmem_mb))(*args, *comm.ins)
    return out[:n_out], out[n_out:]


def _dot(a, b):
    return jnp.dot(a, b, preferred_element_type=F32)


def _dot_nt(a, b):
    return lax.dot_general(a, b, (((1,), (1,)), ((), ())), preferred_element_type=F32)


def _dot_tn(a, b):
    return lax.dot_general(a, b, (((0,), (0,)), ((), ())), preferred_element_type=F32)


def _split_dot(x, m):
    hi = x.astype(BF16)
    lo = (x - hi.astype(F32)).astype(BF16)
    return _dot(hi, m) + _dot(lo, m)


def _silu_parts(g):
    sg = 1.0 / (1.0 + jnp.exp(-g))
    return g * sg, sg * (1.0 + g * (1.0 - sg))


def _idx(*vals):
    return jnp.stack([jnp.asarray(v, jnp.int32) for v in vals])


def _cast_block(w, l, blk, name):
    _, r, c = w.shape
    tr = min(r, 512)

    def body(b_ref, w_ref, o_ref):
        o_ref[...] = w_ref[...].astype(BF16)

    spec = pltpu.PrefetchScalarGridSpec(
        num_scalar_prefetch=1, grid=(r // tr,),
        in_specs=[pl.BlockSpec((None, tr, c), lambda i, b: (l, i, 0))],
        out_specs=pl.BlockSpec((None, tr, c), lambda i, b: (b[0], i, 0)))
    return pl.pallas_call(body, name=name, grid_spec=spec, out_shape=jax.ShapeDtypeStruct((4, r, c), BF16),
                          compiler_params=_params(("parallel",)))(_idx(blk), w)


def _add_sibling(p, theirs, core):
    nblk, r, c = p.shape
    hr = r // 2
    tr = min(hr, 256)
    per = hr // tr

    def body(c_ref, p_ref, t_ref, o_ref):
        o_ref[...] = (p_ref[...].astype(F32) + t_ref[...].astype(F32)).astype(BF16)

    blk = pl.BlockSpec((None, tr, c), lambda j, i, cr: (j, i, 0))
    spec = pltpu.PrefetchScalarGridSpec(
        num_scalar_prefetch=1, grid=(nblk, per),
        in_specs=[pl.BlockSpec((None, tr, c), lambda j, i, cr: (j, cr[0] * per + i, 0)), blk], out_specs=blk)
    return pl.pallas_call(body, name="add_sibling", grid_spec=spec, out_shape=jax.ShapeDtypeStruct((nblk, hr, c), BF16),
                          compiler_params=_params(("parallel", "parallel")))(_idx(core), p, theirs)


def _add_chips(sums, got, blk, core):
    _, hr, c = sums.shape
    tr = min(hr, 256)

    def body(i_ref, s_ref, g0_ref, g1_ref, g2_ref, o_ref):
        o_ref[...] = ((s_ref[...].astype(F32) + g0_ref[...].astype(F32))
                      + g1_ref[...].astype(F32)) + g2_ref[...].astype(F32)

    at = lambda j: pl.BlockSpec((None, tr, c), lambda i, ir: (j, i, 0))
    spec = pltpu.PrefetchScalarGridSpec(
        num_scalar_prefetch=1, grid=(hr // tr,),
        in_specs=[pl.BlockSpec((None, tr, c), lambda i, ir: (ir[0], i, 0)), at(0), at(1), at(2)],
        out_specs=pl.BlockSpec((None, tr, c), lambda i, ir: (ir[1], i, 0)))
    return pl.pallas_call(body, name="add_chips", grid_spec=spec, out_shape=jax.ShapeDtypeStruct((2, hr, c), F32),
                          compiler_params=_params(("parallel",)))(_idx(blk, core), sums, got, got, got)


def _adamw_layer(l, w, g, m, v, prev, name, comm=None):
    nl, r, c = w.shape
    tr = min(r, 256)
    c1 = 1.0 / (1.0 - ADAM_B1 ** ADAM_STEP)
    c2 = 1.0 / (1.0 - ADAM_B2 ** ADAM_STEP)

    def body(w_ref, g_ref, m_ref, v_ref, *rest):
        go_ref, d_ref, nm_ref, nv_ref = rest[-4:]
        gg = g_ref[...]
        nm = ADAM_B1 * m_ref[...] + (1.0 - ADAM_B1) * gg
        nv = ADAM_B2 * v_ref[...] + (1.0 - ADAM_B2) * (gg * gg)
        upd = (nm * c1) / (jnp.sqrt(nv * c2) + ADAM_EPS) + ADAM_WD * w_ref[...]
        go_ref[...] = gg
        d_ref[...] = -ADAM_LR * upd
        nm_ref[...] = nm
        nv_ref[...] = nv

    lay = pl.BlockSpec((None, tr, c), lambda i: (l, i, 0))
    shp = jax.ShapeDtypeStruct((nl, r, c), F32)
    return _call(
        body, name=name, grid=(r // tr,),
        in_specs=[lay, pl.BlockSpec((tr, c), lambda i: (i, 0)), lay, lay] + [ANY] * len(prev),
        out_specs=[lay] * 4, out_shape=[shp] * 4, aliases={4 + k: k for k in range(len(prev))},
        sem=("parallel",), vmem_mb=40, args=(w, g, m, v, *prev), comm=comm)


def _rmsnorm_fwd(x, g):
    s, d = x.shape
    tm = min(s, 256)

    def body(x_ref, g_ref, h_ref):
        xv = x_ref[...]
        r = lax.rsqrt(jnp.mean(xv * xv, axis=1, keepdims=True) + NORM_EPS)
        h_ref[...] = (xv * r * g_ref[...]).astype(BF16)

    return pl.pallas_call(
        body, name="rmsnorm_fwd", grid=(s // tm,),
        in_specs=[pl.BlockSpec((tm, d), lambda i: (i, 0)), pl.BlockSpec((1, d), lambda i: (0, 0))],
        out_specs=pl.BlockSpec((tm, d), lambda i: (i, 0)),
        out_shape=jax.ShapeDtypeStruct((s, d), BF16),
        compiler_params=_params(("parallel",)))(x, g)


def _in_proj(h, w, comm=None):
    s, d = h.shape
    nblk, _, nb = w.shape
    tm, tn = min(s, 1024), min(nb, 1024)
    per = nb // tn

    def body(h_ref, w_ref, o_ref, wt_ref):
        o_ref[...] = _dot(h_ref[...], w_ref[...])

        @pl.when(pl.program_id(1) == 0)
        def _():
            for c in range(tn // HEAD):
                wt_ref[pl.ds(c * HEAD, HEAD), :] = w_ref[:, c * HEAD:(c + 1) * HEAD].astype(F32).T.astype(BF16)

    return _call(
        body, name="in_proj", grid=(nblk * per, s // tm),
        in_specs=[pl.BlockSpec((tm, d), lambda n, m: (m, 0)),
                  pl.BlockSpec((None, d, tn), lambda n, m: (n // per, 0, n % per))],
        out_specs=[pl.BlockSpec((tm, tn), lambda n, m: (m, n)), pl.BlockSpec((tn, d), lambda n, m: (n, 0))],
        out_shape=[jax.ShapeDtypeStruct((s, nblk * nb), F32), jax.ShapeDtypeStruct((nblk * nb, d), BF16)],
        sem=("parallel", "arbitrary"), vmem_mb=56, args=(h, w), comm=comm)


def _heads_per_step(nh):
    return 2 if nh % 2 == 0 else 1


def _head(hh):
    return slice(hh * HEAD, (hh + 1) * HEAD)


def _tri(op):
    r = lax.broadcasted_iota(jnp.int32, (TQ, TQ), 0)
    c = lax.broadcasted_iota(jnp.int32, (TQ, TQ), 1)
    return op(r, c)


def _staggered(groups, hp):
    for hh in range(hp):
        for fn in groups[0]:
            fn(hh)
    for group in groups[1:]:
        for hh in range(hp):
            for fn in group:
                fn(hh)


def _transpose_tiles(src_ref, dst_ref, hp, nt):
    for hh in range(hp):
        for t in range(nt):
            dst_ref[hh, t] = src_ref[pl.ds(t * TK, TK), _head(hh)].T.astype(BF16)


def _chunk_mask(r, a):
    if a is None or r * ROWS >= (a + 1) * TK:
        return None
    if (r + 1) * ROWS <= a * TK:
        return False
    row = lax.broadcasted_iota(jnp.int32, (ROWS, TK), 0) + r * ROWS
    return row > lax.broadcasted_iota(jnp.int32, (ROWS, TK), 1) + a * TK


def _sb_logs(qk, causal):
    z = qk * SCALE
    l1p = jnp.log(1.0 + jnp.exp(-jnp.abs(z)))
    ls = jnp.minimum(-z, 0.0) - l1p
    if causal is not None:
        ls = jnp.where(causal, ls, 0.0)
    return ls, jnp.minimum(z, 0.0) - l1p


def _attn_a_fwd(proj, nh, comm=None):
    s = proj.shape[0]
    qb_rows = min(QB, s)
    nq, nt, per = s // qb_rows, s // TK, qb_rows // TK
    hp = _heads_per_step(nh)
    ng = nh // hp
    chunks = qb_rows // ROWS

    def body(q_ref, k_ref, v_ref, o_ref, lt_ref, acc_ref, qb_ref, m_ref, car_ref, sum_ref,
             z_ref, lsig_ref, aft_ref, hi_ref, lo_ref, w_ref, kt_ref):
        i = pl.program_id(1)

        @pl.when(i == 0)
        def _():
            _transpose_tiles(k_ref, kt_ref, hp, nt)

        qb_ref[...] = q_ref[...].astype(BF16)
        m_ref[...] = _tri(lambda r, c: r > c).astype(BF16)
        acc_ref[...] = jnp.zeros_like(acc_ref)
        car_ref[...] = jnp.zeros_like(car_ref)

        def tile(j, a):
            off = pl.multiple_of(j * TK, TK)

            def scores(hh):
                z_ref[hh] = _dot(qb_ref[:, _head(hh)], kt_ref[hh, j])

            def logs(hh):
                for r in range(chunks):
                    sl, mask = pl.ds(r * ROWS, ROWS), _chunk_mask(r, a)
                    if mask is False:
                        hi_ref[hh, sl, :] = jnp.zeros((ROWS, TK), BF16)
                        lo_ref[hh, sl, :] = jnp.zeros((ROWS, TK), BF16)
                        continue
                    ls, lsig = _sb_logs(z_ref[hh, sl, :], mask)
                    lsig_ref[hh, sl, :] = lsig
                    hi = ls.astype(BF16)
                    hi_ref[hh, sl, :] = hi
                    lo_ref[hh, sl, :] = (ls - hi.astype(F32)).astype(BF16)
                    sum_ref[hh, sl, :] = jnp.sum(ls, axis=1, keepdims=True)

            def after(hh):
                aft_ref[hh] = _dot(hi_ref[hh], m_ref[...]) + _dot(lo_ref[hh], m_ref[...])

            def weights(hh):
                for r in range(chunks):
                    sl, mask = pl.ds(r * ROWS, ROWS), _chunk_mask(r, a)
                    if mask is False:
                        w_ref[hh, sl, :] = jnp.zeros((ROWS, TK), BF16)
                        continue
                    w = jnp.exp(lsig_ref[hh, sl, :] + aft_ref[hh, sl, :] + car_ref[hh, sl, :])
                    if mask is not None:
                        w = jnp.where(mask, w, 0.0)
                    w_ref[hh, sl, :] = w.astype(BF16)
                    car_ref[hh, sl, :] += sum_ref[hh, sl, :]

            def values(hh):
                vb = v_ref[pl.ds(off, TK), _head(hh)].astype(BF16)
                acc_ref[:, _head(hh)] += _dot(w_ref[hh], vb)

            _staggered([(scores,), (logs, after), (weights, values)], hp)

        for a in reversed(range(per)):
            tile(i * per + a, a)

        def step(t, carry):
            tile(i * per - 1 - t, None)
            return carry

        lax.fori_loop(0, i * per, step, 0)
        o_ref[...] = acc_ref[...]
        lt_ref[...] = car_ref[...]

    wd = hp * HEAD
    sq = lambda dt: pltpu.VMEM((hp, qb_rows, TK), dt)
    return _call(
        body, name="attn_a_fwd", grid=(ng, nq),
        in_specs=[pl.BlockSpec((qb_rows, wd), lambda h, i: (i, h)),
                  pl.BlockSpec((s, wd), lambda h, i: (0, ng + h)),
                  pl.BlockSpec((s, wd), lambda h, i: (0, 2 * ng + h))],
        out_specs=[pl.BlockSpec((qb_rows, wd), lambda h, i: (i, h)),
                   pl.BlockSpec((hp, qb_rows, 1), lambda h, i: (h, i, 0))],
        out_shape=[jax.ShapeDtypeStruct((s, nh * HEAD), F32), jax.ShapeDtypeStruct((nh, s, 1), F32)],
        scratch=[pltpu.VMEM((qb_rows, wd), F32), pltpu.VMEM((qb_rows, wd), BF16), pltpu.VMEM((TK, TK), BF16),
                 pltpu.VMEM((hp, qb_rows, 1), F32), pltpu.VMEM((hp, qb_rows, 1), F32),
                 sq(F32), sq(F32), sq(F32), sq(BF16), sq(BF16), sq(BF16),
                 pltpu.VMEM((hp, nt, HEAD, TK), BF16)],
        sem=("parallel", "arbitrary"), args=(proj, proj, proj), comm=comm)


def _band_valid(i):
    cl = lax.broadcasted_iota(jnp.int32, (TQ, WIN), 0) // CHUNK
    kl = lax.broadcasted_iota(jnp.int32, (TQ, WIN), 1) // CHUNK
    first = LEFT_CHUNKS - (TQ // CHUNK) * i
    return (kl >= cl) & (kl <= cl + LEFT_CHUNKS) & (kl >= first)


def _build_bias(e_ref, bias_ref):
    e8 = jnp.broadcast_to(e_ref[...], (8, EXT))
    row = lax.broadcasted_iota(jnp.int32, (8, EXT), 0)
    t8 = jnp.zeros((8, EXT), F32)
    for b in range(8):
        t8 = jnp.where(row == b, pltpu.roll(e8, b, 1) if b else e8, t8)
    for a in range(TQ // 8):
        sl = pltpu.roll(t8, 8 * a, 1) if a else t8
        bias_ref[pl.ds(8 * a, 8), :] = sl[:, :WIN]


def _reduce_bias_grad(db_ref):
    acc = jnp.zeros((8, EXT), F32)
    for a in range(TQ // 8):
        sl = db_ref[pl.ds(8 * a, 8), :]
        acc = acc + (pltpu.roll(sl, EXT - 8 * a, 1) if a else sl)
    row = lax.broadcasted_iota(jnp.int32, (8, EXT), 0)
    tot = jnp.zeros((8, EXT), F32)
    for b in range(8):
        tot = tot + jnp.where(row == b, pltpu.roll(acc, EXT - b, 1) if b else acc, 0.0)
    return jnp.sum(tot, axis=0, keepdims=True)


def _band_fill(k_ref, v_ref, kg_ref, kn_pad, v_pad, s):
    k = k_ref[...]
    rk = lax.rsqrt(jnp.mean(k * k, axis=1, keepdims=True) + NORM_EPS)
    kn_pad[pl.ds(0, PAD), :] = jnp.zeros((PAD, HEAD), BF16)
    kn_pad[pl.ds(PAD, s), :] = (k * rk * kg_ref[...]).astype(BF16)
    v_pad[pl.ds(0, PAD), :] = jnp.zeros((PAD, HEAD), BF16)
    v_pad[pl.ds(PAD, s), :] = v_ref[...].astype(BF16)


def _band_probs(q_ref, qg_ref, kn_pad, bias_ref, i):
    q = q_ref[...]
    rq = lax.rsqrt(jnp.mean(q * q, axis=1, keepdims=True) + NORM_EPS)
    qhat = q * rq
    qn = (qhat * qg_ref[...]).astype(BF16)
    off = pl.multiple_of(i * TQ, TQ)
    kw = kn_pad[pl.ds(off, WIN), :]
    sc = _dot_nt(qn, kw) * SCALE + bias_ref[...]
    sc = jnp.where(_band_valid(i), sc, NEG_BIG)
    p = jnp.exp(sc - jnp.max(sc, axis=1, keepdims=True))
    pn = p / jnp.sum(p, axis=1, keepdims=True)
    return rq, qhat, qn, kw, off, pn


def _attn_b_fwd(proj, qg, kg, ext, nh, comm=None):
    s = proj.shape[0]
    nq = s // TQ

    def body(q_ref, k_ref, v_ref, qg_ref, kg_ref, e_ref, o_ref, kn_pad, v_pad, bias_ref):
        i = pl.program_id(1)

        @pl.when(i == 0)
        def _():
            _band_fill(k_ref, v_ref, kg_ref, kn_pad, v_pad, s)
            _build_bias(e_ref, bias_ref)

        _, _, _, _, off, pn = _band_probs(q_ref, qg_ref, kn_pad, bias_ref, i)
        o_ref[...] = _dot(pn.astype(BF16), v_pad[pl.ds(off, WIN), :])

    vec = pl.BlockSpec((1, HEAD), lambda h, i: (0, 0))
    return _call(
        body, name="attn_b_fwd", grid=(nh, nq),
        in_specs=[pl.BlockSpec((TQ, HEAD), lambda h, i: (i, 4 * nh + h)),
                  pl.BlockSpec((s, HEAD), lambda h, i: (0, 5 * nh + h)),
                  pl.BlockSpec((s, HEAD), lambda h, i: (0, 6 * nh + h)),
                  vec, vec,
                  pl.BlockSpec((None, 1, EXT), lambda h, i: (h, 0, 0))],
        out_specs=[pl.BlockSpec((TQ, HEAD), lambda h, i: (i, h))],
        out_shape=[jax.ShapeDtypeStruct((s, nh * HEAD), F32)],
        scratch=[pltpu.VMEM((s + PAD, HEAD), BF16), pltpu.VMEM((s + PAD, HEAD), BF16), pltpu.VMEM((TQ, WIN), F32)],
        sem=("parallel", "arbitrary"), args=(proj, proj, proj, qg, kg, ext), comm=comm)


def _out_proj(x, ya, yb, proj, w, comm=None):
    s, d = x.shape
    ds_ = ya.shape[1]
    tm = min(s, 256)

    def body(x_ref, ya_ref, yb_ref, ga_ref, gb_ref, w_ref, o_ref, mix_ref):
        ma = (ya_ref[...] * _silu_parts(ga_ref[...])[0]).astype(BF16)
        mb = (yb_ref[...] * _silu_parts(gb_ref[...])[0]).astype(BF16)
        mix_ref[:, :ds_] = ma
        mix_ref[:, ds_:] = mb
        o_ref[...] = x_ref[...] + _dot(ma, w_ref[pl.ds(0, ds_), :]) + _dot(mb, w_ref[pl.ds(ds_, ds_), :])

    row = lambda width: pl.BlockSpec((tm, width), lambda i: (i, 0))
    return _call(
        body, name="out_proj", grid=(s // tm,),
        in_specs=[row(d), row(ds_), row(ds_),
                  pl.BlockSpec((tm, ds_), lambda i: (i, 3)), pl.BlockSpec((tm, ds_), lambda i: (i, 7)),
                  pl.BlockSpec((2 * ds_, d), lambda i: (0, 0))],
        out_specs=[row(d), row(2 * ds_)],
        out_shape=[jax.ShapeDtypeStruct((s, d), F32), jax.ShapeDtypeStruct((s, 2 * ds_), BF16)],
        sem=("parallel",), vmem_mb=48, args=(x, ya, yb, proj, proj, w), comm=comm)


def _loss_head(y, tgt):
    s, d = y.shape
    tm = min(s, 256)

    def body(y_ref, t_ref, dy_ref, l_ref):
        @pl.when(pl.program_id(0) == 0)
        def _():
            l_ref[...] = jnp.zeros_like(l_ref)

        err = y_ref[...] - t_ref[...]
        dy_ref[...] = err * (1.0 / d)
        l_ref[...] += 0.5 * jnp.sum(jnp.mean(err * err, axis=1, keepdims=True), axis=0, keepdims=True)

    row = pl.BlockSpec((tm, d), lambda i: (i, 0))
    return pl.pallas_call(
        body, name="loss_head", grid=(s // tm,), in_specs=[row, row],
        out_specs=[row, pl.BlockSpec((8, 128), lambda i: (0, 0))],
        out_shape=[jax.ShapeDtypeStruct((s, d), F32), jax.ShapeDtypeStruct((8, 128), F32)],
        compiler_params=_params(("arbitrary",)))(y, tgt)


def _out_proj_bwd(dxo, ya, yb, proj, w, comm=None):
    s, d = dxo.shape
    ds_ = ya.shape[1]
    tm = min(s, 256)

    def body(dx_ref, ya_ref, yb_ref, ga_ref, gb_ref, w_ref, dya_ref, dyb_ref, dga_ref, dgb_ref):
        dxb = dx_ref[...].astype(BF16)
        for y_ref, g_ref, lo, dy_ref, dg_ref in ((ya_ref, ga_ref, 0, dya_ref, dga_ref),
                                                 (yb_ref, gb_ref, ds_, dyb_ref, dgb_ref)):
            dmix = _dot_nt(dxb, w_ref[pl.ds(lo, ds_), :])
            act, dact = _silu_parts(g_ref[...])
            dy_ref[...] = dmix * act
            dg_ref[...] = (dmix * y_ref[...] * dact).astype(BF16)

    row = lambda width: pl.BlockSpec((tm, width), lambda i: (i, 0))
    return _call(
        body, name="out_proj_bwd", grid=(s // tm,),
        in_specs=[row(d), row(ds_), row(ds_),
                  pl.BlockSpec((tm, ds_), lambda i: (i, 3)), pl.BlockSpec((tm, ds_), lambda i: (i, 7)),
                  pl.BlockSpec((2 * ds_, d), lambda i: (0, 0))],
        out_specs=[row(ds_)] * 4,
        out_shape=[jax.ShapeDtypeStruct((s, ds_), F32)] * 2 + [jax.ShapeDtypeStruct((s, ds_), BF16)] * 2,
        sem=("parallel",), vmem_mb=48, args=(dxo, ya, yb, proj, proj, w), comm=comm)


def _wgrad(a, b, nblk, col_blocks, name):
    s, m = a.shape
    n = b.shape[1]
    if col_blocks:
        tr = min(m, 1024)
        nb = n // nblk
        tn = min(nb, 2048)
        per = nb // tn
        out_shape = (nblk, m, nb)
        out_spec = pl.BlockSpec((None, tr, tn), lambda j, r: (j // per, r, j % per))
    else:
        tn = min(n, 1024)
        tr = m // nblk
        out_shape = (nblk, tr, n)
        out_spec = pl.BlockSpec((None, tr, tn), lambda j, r: (r, 0, j))

    def body(a_ref, b_ref, o_ref):
        o_ref[...] = _dot_tn(a_ref[...].astype(BF16), b_ref[...].astype(BF16)).astype(BF16)

    return pl.pallas_call(
        body, name=name, grid=(n // tn, m // tr),
        in_specs=[pl.BlockSpec((s, tr), lambda j, r: (0, r)), pl.BlockSpec((s, tn), lambda j, r: (0, j))],
        out_specs=out_spec, out_shape=jax.ShapeDtypeStruct(out_shape, BF16),
        compiler_params=_params(("parallel", "parallel"), 48))(a, b)


def _attn_a_bwd(proj, lt, dya, nh, comm=None):
    s = proj.shape[0]
    qb_rows = min(TK, s)
    nq, nt, per = s // qb_rows, s // TK, qb_rows // TK
    hp = _heads_per_step(nh)
    ng = nh // hp
    chunks = qb_rows // ROWS

    def body(q_ref, k_ref, v_ref, lt_ref, do_ref, dq_ref, dk_ref, dv_ref, dq_acc, dk_acc, dv_acc,
             qb_ref, dob_ref, upto_ref, before_ref, cls_ref, cg_ref, sls_ref, sg_ref,
             z_ref, dw_ref, lsig_ref, pre_ref, g_ref, hi_ref, lo_ref, wb_ref, kt_ref, vt_ref, qt_ref, dot_ref):
        i = pl.program_id(1)

        @pl.when(i == 0)
        def _():
            dk_acc[...] = jnp.zeros_like(dk_acc)
            dv_acc[...] = jnp.zeros_like(dv_acc)
            _transpose_tiles(k_ref, kt_ref, hp, nt)
            _transpose_tiles(v_ref, vt_ref, hp, nt)

        dq_acc[...] = jnp.zeros_like(dq_acc)
        qb_ref[...] = q_ref[...].astype(BF16)
        dob_ref[...] = do_ref[...].astype(BF16)
        for hh in range(hp):
            qt_ref[hh] = q_ref[:, _head(hh)].T.astype(BF16)
            dot_ref[hh] = do_ref[:, _head(hh)].T.astype(BF16)
        upto_ref[...] = _tri(lambda r, c: r <= c).astype(BF16)
        before_ref[...] = _tri(lambda r, c: r < c).astype(BF16)
        cls_ref[...] = jnp.zeros_like(cls_ref)
        cg_ref[...] = jnp.zeros_like(cg_ref)

        def tile(j, a):
            off = pl.multiple_of(j * TK, TK)
            zeros = jnp.zeros((ROWS, TK), BF16)

            def scores(hh):
                z_ref[hh] = _dot(qb_ref[:, _head(hh)], kt_ref[hh, j])
                dw_ref[hh] = _dot(dob_ref[:, _head(hh)], vt_ref[hh, j])

            def logs(hh):
                for r in range(chunks):
                    sl, mask = pl.ds(r * ROWS, ROWS), _chunk_mask(r, a)
                    if mask is False:
                        hi_ref[hh, sl, :] = zeros
                        lo_ref[hh, sl, :] = zeros
                        continue
                    ls, lsig = _sb_logs(z_ref[hh, sl, :], mask)
                    lsig_ref[hh, sl, :] = lsig
                    hi = ls.astype(BF16)
                    hi_ref[hh, sl, :] = hi
                    lo_ref[hh, sl, :] = (ls - hi.astype(F32)).astype(BF16)
                    sls_ref[hh, sl, :] = jnp.sum(ls, axis=1, keepdims=True)

            def upto(hh):
                pre_ref[hh] = _dot(hi_ref[hh], upto_ref[...]) + _dot(lo_ref[hh], upto_ref[...])

            def weights(hh):
                for r in range(chunks):
                    sl, mask = pl.ds(r * ROWS, ROWS), _chunk_mask(r, a)
                    if mask is False:
                        wb_ref[hh, sl, :] = zeros
                        hi_ref[hh, sl, :] = zeros
                        continue
                    w = jnp.exp(lsig_ref[hh, sl, :] + (lt_ref[hh, sl, :] - (cls_ref[hh, sl, :] + pre_ref[hh, sl, :])))
                    if mask is not None:
                        w = jnp.where(mask, w, 0.0)
                    wb_ref[hh, sl, :] = w.astype(BF16)
                    g = w * dw_ref[hh, sl, :]
                    g_ref[hh, sl, :] = g
                    hi_ref[hh, sl, :] = g.astype(BF16)
                    sg_ref[hh, sl, :] = jnp.sum(g, axis=1, keepdims=True)

            def earlier(hh):
                dw_ref[hh] = _dot(hi_ref[hh], before_ref[...])

            def logit_grads(hh):
                for r in range(chunks):
                    sl, mask = pl.ds(r * ROWS, ROWS), _chunk_mask(r, a)
                    if mask is False:
                        lo_ref[hh, sl, :] = zeros
                        continue
                    z = z_ref[hh, sl, :] * SCALE
                    e = jnp.exp(-jnp.abs(z))
                    rinv = 1.0 / (1.0 + e)
                    beta = jnp.where(z >= 0.0, rinv, e * rinv)
                    dz = g_ref[hh, sl, :] * (1.0 - beta) - beta * (cg_ref[hh, sl, :] + dw_ref[hh, sl, :])
                    if mask is not None:
                        dz = jnp.where(mask, dz, 0.0)
                    lo_ref[hh, sl, :] = (dz * SCALE).astype(BF16)
                    cls_ref[hh, sl, :] += sls_ref[hh, sl, :]
                    cg_ref[hh, sl, :] += sg_ref[hh, sl, :]

            def grads(hh):
                dq_acc[:, _head(hh)] += _dot(lo_ref[hh], k_ref[pl.ds(off, TK), _head(hh)].astype(BF16))
                dk_acc[hh, j] += _dot(qt_ref[hh], lo_ref[hh])
                dv_acc[hh, j] += _dot(dot_ref[hh], wb_ref[hh])

            _staggered([(scores,), (logs, upto), (weights, earlier), (logit_grads, grads)], hp)

        def step(j, carry):
            tile(j, None)
            return carry

        lax.fori_loop(0, i * per, step, 0)
        for a in range(per):
            tile(i * per + a, a)
        dq_ref[...] = dq_acc[...].astype(BF16)

        @pl.when(i == nq - 1)
        def _():
            for hh in range(hp):
                for t in range(nt):
                    dk_ref[pl.ds(t * TK, TK), _head(hh)] = dk_acc[hh, t].T.astype(BF16)
                    dv_ref[pl.ds(t * TK, TK), _head(hh)] = dv_acc[hh, t].T.astype(BF16)

    wd = hp * HEAD
    sq = lambda dt: pltpu.VMEM((hp, qb_rows, TK), dt)
    tiles = lambda dt: pltpu.VMEM((hp, nt, HEAD, TK), dt)
    blk = pl.BlockSpec((qb_rows, wd), lambda h, i: (i, h))
    col = pl.BlockSpec((s, wd), lambda h, i: (0, h))
    shp = jax.ShapeDtypeStruct((s, nh * HEAD), BF16)
    return _call(
        body, name="attn_a_bwd", grid=(ng, nq),
        in_specs=[blk,
                  pl.BlockSpec((s, wd), lambda h, i: (0, ng + h)),
                  pl.BlockSpec((s, wd), lambda h, i: (0, 2 * ng + h)),
                  pl.BlockSpec((hp, qb_rows, 1), lambda h, i: (h, i, 0)), blk],
        out_specs=[blk, col, col], out_shape=[shp] * 3,
        scratch=[pltpu.VMEM((qb_rows, wd), F32), tiles(F32), tiles(F32),
                 pltpu.VMEM((qb_rows, wd), BF16), pltpu.VMEM((qb_rows, wd), BF16),
                 pltpu.VMEM((TK, TK), BF16), pltpu.VMEM((TK, TK), BF16)]
        + [pltpu.VMEM((hp, qb_rows, 1), F32)] * 4 + [sq(F32)] * 5 + [sq(BF16)] * 3
        + [tiles(BF16), tiles(BF16), pltpu.VMEM((hp, HEAD, qb_rows), BF16), pltpu.VMEM((hp, HEAD, qb_rows), BF16)],
        sem=("parallel", "arbitrary"), args=(proj, proj, proj, lt, dya), comm=comm)


def _attn_b_bwd(proj, dyb, qg, kg, ext, nh, comm=None):
    s = proj.shape[0]
    nq = s // TQ

    def body(q_ref, k_ref, v_ref, do_ref, qg_ref, kg_ref, e_ref,
             dq_ref, dk_ref, dv_ref, dqg_ref, dkg_ref, de_ref,
             kn_pad, v_pad, bias_ref, db_acc, dkn_acc, dv_acc):
        i = pl.program_id(1)

        @pl.when(i == 0)
        def _():
            _band_fill(k_ref, v_ref, kg_ref, kn_pad, v_pad, s)
            _build_bias(e_ref, bias_ref)
            db_acc[...] = jnp.zeros_like(db_acc)
            dkn_acc[...] = jnp.zeros_like(dkn_acc)
            dv_acc[...] = jnp.zeros_like(dv_acc)
            dqg_ref[...] = jnp.zeros_like(dqg_ref)

        rq, qhat, qn, kw, off, pn = _band_probs(q_ref, qg_ref, kn_pad, bias_ref, i)
        dob = do_ref[...].astype(BF16)
        dp = _dot_nt(dob, v_pad[pl.ds(off, WIN), :])
        dsc = pn * (dp - jnp.sum(pn * dp, axis=1, keepdims=True))
        db_acc[:, :WIN] += dsc
        dsb = (dsc * SCALE).astype(BF16)
        dqn = _dot(dsb, kw)
        dkn_acc[pl.ds(off, WIN), :] += _dot_tn(dsb, qn)
        dv_acc[pl.ds(off, WIN), :] += _dot_tn(pn.astype(BF16), dob)
        dqh = dqn * qg_ref[...]
        dq_ref[...] = (rq * (dqh - qhat * jnp.mean(dqh * qhat, axis=1, keepdims=True))).astype(BF16)
        dqg_ref[...] += jnp.sum(dqn * qhat, axis=0, keepdims=True)

        @pl.when(i == nq - 1)
        def _():
            k = k_ref[...]
            rk = lax.rsqrt(jnp.mean(k * k, axis=1, keepdims=True) + NORM_EPS)
            khat = k * rk
            dkn = dkn_acc[pl.ds(PAD, s), :]
            dkh = dkn * kg_ref[...]
            dk_ref[...] = (rk * (dkh - khat * jnp.mean(dkh * khat, axis=1, keepdims=True))).astype(BF16)
            dkg_ref[...] = jnp.sum(dkn * khat, axis=0, keepdims=True)
            dv_ref[...] = dv_acc[pl.ds(PAD, s), :].astype(BF16)
            de_ref[...] = _reduce_bias_grad(db_acc)

    blk = pl.BlockSpec((TQ, HEAD), lambda h, i: (i, h))
    col = pl.BlockSpec((s, HEAD), lambda h, i: (0, h))
    vec = pl.BlockSpec((1, HEAD), lambda h, i: (0, 0))
    hvec = pl.BlockSpec((None, 1, HEAD), lambda h, i: (h, 0, 0))
    hext = pl.BlockSpec((None, 1, EXT), lambda h, i: (h, 0, 0))
    shp = jax.ShapeDtypeStruct((s, nh * HEAD), BF16)
    return _call(
        body, name="attn_b_bwd", grid=(nh, nq),
        in_specs=[pl.BlockSpec((TQ, HEAD), lambda h, i: (i, 4 * nh + h)),
                  pl.BlockSpec((s, HEAD), lambda h, i: (0, 5 * nh + h)),
                  pl.BlockSpec((s, HEAD), lambda h, i: (0, 6 * nh + h)),
                  blk, vec, vec, hext],
        out_specs=[blk, col, col, hvec, hvec, hext],
        out_shape=[shp] * 3 + [jax.ShapeDtypeStruct((nh, 1, HEAD), F32)] * 2
        + [jax.ShapeDtypeStruct((nh, 1, EXT), F32)],
        scratch=[pltpu.VMEM((s + PAD, HEAD), BF16), pltpu.VMEM((s + PAD, HEAD), BF16),
                 pltpu.VMEM((TQ, WIN), F32), pltpu.VMEM((TQ, EXT), F32),
                 pltpu.VMEM((s + PAD, HEAD), F32), pltpu.VMEM((s + PAD, HEAD), F32)],
        sem=("parallel", "arbitrary"), args=(proj, proj, proj, dyb, qg, kg, ext), comm=comm)


def _in_proj_bwd(dproj, wt, x, dxo, g, comm=None):
    s, d = x.shape
    tm, tk = min(s, 512), min(wt.shape[0], 1024)
    nk = wt.shape[0] // tk

    def body(dp_ref, w_ref, x_ref, dxo_ref, g_ref, dx_ref, dg_ref, acc):
        m, k = pl.program_id(0), pl.program_id(1)

        @pl.when(k == 0)
        def _():
            acc[...] = jnp.zeros_like(acc)

        @pl.when((k == 0) & (m == 0))
        def _():
            dg_ref[...] = jnp.zeros_like(dg_ref)

        acc[...] += _dot(dp_ref[...], w_ref[...])

        @pl.when(k == nk - 1)
        def _():
            xv = x_ref[...]
            r = lax.rsqrt(jnp.mean(xv * xv, axis=1, keepdims=True) + NORM_EPS)
            xhat = xv * r
            dh = acc[...]
            dxh = dh * g_ref[...]
            dx_ref[...] = dxo_ref[...] + r * (dxh - xhat * jnp.mean(dxh * xhat, axis=1, keepdims=True))
            dg_ref[...] += jnp.sum(dh * xhat, axis=0, keepdims=True)

    row = pl.BlockSpec((tm, d), lambda m, k: (m, 0))
    return _call(
        body, name="in_proj_bwd", grid=(s // tm, nk),
        in_specs=[pl.BlockSpec((tm, tk), lambda m, k: (m, k)),
                  pl.BlockSpec((tk, d), lambda m, k: (k, 0)),
                  row, row, pl.BlockSpec((1, d), lambda m, k: (0, 0))],
        out_specs=[row, pl.BlockSpec((8, d), lambda m, k: (0, 0))],
        out_shape=[jax.ShapeDtypeStruct((s, d), F32), jax.ShapeDtypeStruct((8, d), F32)],
        scratch=[pltpu.VMEM((tm, d), F32)], sem=("arbitrary", "arbitrary"), vmem_mb=56,
        args=(dproj, wt, x, dxo, g), comm=comm)


def _place():
    x, y, c = lax.axis_index("x"), lax.axis_index("y"), lax.axis_index("c")
    chips = [(1 - x, y), (x, 1 - y), (1 - x, 1 - y)]
    return x, y, c, chips


def _comm_call(body, name, ins, out_shape, n_remote, n_local, aliases=None):
    return pl.pallas_call(
        body, name=name, in_specs=[ANY] * len(ins), out_specs=[ANY] * len(out_shape), out_shape=out_shape,
        input_output_aliases=aliases or {},
        scratch_shapes=[pltpu.SemaphoreType.DMA((n_remote,)), pltpu.SemaphoreType.DMA((n_remote,)),
                        pltpu.SemaphoreType.DMA((n_local,))])(*ins)


def _rcopy(src, dst, send_sems, recv_sems, k, dev):
    return pltpu.make_async_remote_copy(src_ref=src, dst_ref=dst, send_sem=send_sems.at[k], recv_sem=recv_sems.at[k],
                                        device_id=dev, device_id_type=MESH)


def _run_comm(comm, name):
    n_ci, n_co = len(comm.ins), len(comm.outs)

    def body(*refs):
        cins, couts = refs[:n_ci], refs[n_ci:n_ci + n_co]
        send_sems, recv_sems = refs[n_ci + n_co:]
        comm.start(cins, couts, send_sems, recv_sems)
        comm.finish(cins, couts, send_sems, recv_sems)

    return pl.pallas_call(
        body, name=name, in_specs=[ANY] * n_ci, out_specs=[ANY] * n_co, out_shape=list(comm.outs),
        input_output_aliases=dict(comm.aliases),
        scratch_shapes=[pltpu.SemaphoreType.DMA((comm.n_sems,)), pltpu.SemaphoreType.DMA((comm.n_sems,))])(*comm.ins)


MOVES = ((None, 0, 2, 0), (None, 0, 2, 1), (0, 0, 1, 1), (1, 1, 2, 0))
ARRIVALS = ((0, 0, 2), (1, 0, 2), (2, 0, 1), (2, 1, 2))


def _gather_comm(fulls, rbp=None, stage="all", chips_at=(0, 1, 2, 3), pass_on=None):
    n = len(fulls)
    n_ici = len(MOVES) * n
    per_tensor = chips_at if isinstance(chips_at, list) else [chips_at] * n

    def region(full, blk, core, lo, hi):
        quarter = full.shape[1] // 4
        return full.at[blk].at[pl.ds(core * 2 * quarter + lo * quarter, (hi - lo) * quarter)]

    def block_of(chip):
        return 2 * chip[0] + chip[1]

    def ici(couts, send_sems, recv_sems, x, y, c, chips, at):
        out = []
        for t, full in enumerate(couts[:n]):
            for p, (whose, lo, hi, to) in enumerate(MOVES):
                if p in at[t]:
                    src = region(full, 2 * x + y if whose is None else block_of(chips[whose]), c, lo, hi)
                    out.append(_rcopy(src, src, send_sems, recv_sems, len(MOVES) * t + p, (*chips[to], c)))
        return out

    def landed(couts, send_sems, recv_sems, x, y, c, chips, core, first, at=None):
        out = []
        for t, full in enumerate(couts[:n]):
            for p, (whose, lo, hi) in enumerate(ARRIVALS):
                if at is None or p in at[t]:
                    got = region(full, block_of(chips[whose]), core, lo, hi)
                    out.append(_rcopy(got, got, send_sems, recv_sems, first + len(MOVES) * t + p, (x, y, 1 - c)))
        return out

    def small(cins, couts, send_sems, recv_sems, x, y, c, chips):
        b = 2 * x + y
        return ([_rcopy(cins[n], couts[n].at[b], send_sems, recv_sems, 2 * n_ici + j, (*chip, c))
                 for j, chip in enumerate(chips)],
                pltpu.make_async_copy(cins[n], couts[n].at[b], send_sems.at[2 * n_ici + 3]))

    first_hop, second_hop = [(0, 1)] * n, [(2, 3)] * n
    to_sibling = [pass_on] * n if pass_on is not None else (None if stage != "chips" else [()] * n)

    def start(cins, couts, send_sems, recv_sems):
        x, y, c, chips = _place()
        if stage != "all":
            for cp in landed(couts, send_sems, recv_sems, x, y, c, chips, c, n_ici, to_sibling):
                cp.start()
        if stage == "sibling":
            return
        for cp in ici(couts, send_sems, recv_sems, x, y, c, chips, first_hop if stage == "all" else per_tensor):
            cp.start()
        if rbp is not None:
            remote, local = small(cins, couts, send_sems, recv_sems, x, y, c, chips)
            for cp in remote:
                cp.start()
            local.start()

    def finish(cins, couts, send_sems, recv_sems):
        x, y, c, chips = _place()
        place = (couts, send_sems, recv_sems, x, y, c, chips)
        passed = landed(*place, c, n_ici, to_sibling)
        if stage == "chips":
            for cp in landed(*place, c, 0, per_tensor):
                cp.wait_recv()
            for cp in ici(*place, per_tensor):
                cp.wait_send()
        if stage == "all":
            for cp in landed(*place, c, 0, first_hop):
                cp.wait_recv()
            for cp in ici(*place, second_hop):
                cp.start()
            for cp in landed(*place, c, 0, second_hop):
                cp.wait_recv()
            for cp in passed:
                cp.start()
            for cp in ici(*place, first_hop) + ici(*place, second_hop):
                cp.wait_send()
        for cp in landed(*place, 1 - c, n_ici, to_sibling):
            cp.wait_recv()
        for cp in passed:
            cp.wait_send()
        if rbp is not None:
            remote, local = small(cins, couts, send_sems, recv_sems, x, y, c, chips)
            for j, chip in enumerate(chips):
                got = couts[n].at[2 * chip[0] + chip[1]]
                _rcopy(got, got, send_sems, recv_sems, 2 * n_ici + j, (x, y, c)).wait_recv()
            for cp in remote:
                cp.wait_send()
            local.wait()

    outs = [jax.ShapeDtypeStruct(f.shape, f.dtype) for f in fulls]
    ins = list(fulls)
    if rbp is not None:
        ins.append(rbp)
        outs.append(jax.ShapeDtypeStruct((4,) + rbp.shape, F32))
    return _Comm(tuple(ins), tuple(outs), {t: t for t in range(n)}, 2 * n_ici + 4, start, finish)


CHIP_PIECES = ((0, 0, 2), (1, 0, 2), (2, 0, 1), (2, 1, 2))


def _chips_comm(sums, pieces=(0, 1, 2, 3), arrived=None):
    n = len(sums)

    def copies(cins, couts, send_sems, recv_sems):
        x, y, c, chips = _place()
        out = []
        for t in range(n):
            half = cins[t].shape[1] // 2
            for p in pieces:
                j, lo, hi = CHIP_PIECES[p]
                rows = pl.ds(lo * half, (hi - lo) * half)
                out.append(_rcopy(cins[t].at[2 * chips[j][0] + chips[j][1]].at[rows], couts[t].at[j].at[rows],
                                  send_sems, recv_sems, len(CHIP_PIECES) * t + p, (*chips[j], c)))
        return out

    def start(*refs):
        for cp in copies(*refs):
            cp.start()

    def finish(*refs):
        for cp in copies(*refs):
            cp.wait()

    outs = tuple(jax.ShapeDtypeStruct((3,) + p.shape[1:], p.dtype) for p in sums)
    ins = tuple(sums) + tuple(arrived or ())
    return _Comm(ins, outs, {n + t: t for t in range(n)} if arrived else {}, len(CHIP_PIECES) * n, start, finish)


def _pair_comm(ins, outs, aliases, copies):
    def start(*refs):
        for cp in copies(*refs):
            cp.start()

    def finish(*refs):
        for cp in copies(*refs):
            cp.wait()

    return _Comm(tuple(ins), tuple(outs), aliases, len(ins), start, finish)


def _sibling_comm(parts):
    def copies(cins, couts, send_sems, recv_sems):
        x, y, c, _ = _place()
        return [_rcopy(p.at[:, pl.ds((1 - c) * (p.shape[1] // 2), p.shape[1] // 2), :], couts[t],
                       send_sems, recv_sems, t, (x, y, 1 - c)) for t, p in enumerate(cins)]

    half = [jax.ShapeDtypeStruct((p.shape[0], p.shape[1] // 2, p.shape[2]), p.dtype) for p in parts]
    return _pair_comm(parts, half, {}, copies)


def _halves_comm(tots):
    def copies(cins, couts, send_sems, recv_sems):
        x, y, c, _ = _place()
        return [_rcopy(g.at[c], g.at[c], send_sems, recv_sems, t, (x, y, 1 - c)) for t, g in enumerate(couts)]

    return _pair_comm(tots, [jax.ShapeDtypeStruct(t.shape, t.dtype) for t in tots],
                      {t: t for t in range(len(tots))}, copies)


def _gather_small(packed):
    def body(p_ref, all_ref, send_sems, recv_sems, loc_sems):
        x, y, c, _ = _place()
        me = 4 * x + 2 * y + c
        local = pltpu.make_async_copy(p_ref, all_ref.at[me], loc_sems.at[0])
        local.start()
        sent = []
        for k in range(1, 8):
            px, py, pc = x ^ (k >> 2), y ^ ((k >> 1) & 1), c ^ (k & 1)
            cp = _rcopy(p_ref, all_ref.at[me], send_sems, recv_sems, k - 1, (px, py, pc))
            cp.start()
            sent.append(cp)
        for k in range(1, 8):
            px, py, pc = x ^ (k >> 2), y ^ ((k >> 1) & 1), c ^ (k & 1)
            got = all_ref.at[4 * px + 2 * py + pc]
            _rcopy(got, got, send_sems, recv_sems, k - 1, (x, y, c)).wait_recv()
        for cp in sent:
            cp.wait_send()
        local.wait()

    return _comm_call(body, "gather_small", [packed], [jax.ShapeDtypeStruct((8,) + packed.shape, F32)], 7, 1)[0]


def _sum_devices(allp):
    n, r, c = allp.shape

    def body(a_ref, o_ref):
        acc = a_ref[0]
        for k in range(1, n):
            acc = acc + a_ref[k]
        o_ref[...] = acc

    return pl.pallas_call(body, name="sum_devices", out_shape=jax.ShapeDtypeStruct((r, c), F32))(allp)


def _ext_index():
    u = np.arange(EXT)
    dist = np.where(u < WIN, PAD - u, PAD + EXT - u)
    return np.clip(dist, -(CHUNK - 1), REL_CLIP) + (CHUNK - 1)


def _pack(parts, rows):
    flat = jnp.concatenate([p.reshape(-1) for p in parts])
    return jnp.pad(flat, (0, rows * 128 - flat.shape[0])).reshape(rows, 128)


def _unpack(packed, shapes):
    flat, out, at = packed.reshape(-1), [], 0
    for shp in shapes:
        size = int(np.prod(shp))
        out.append(flat[at:at + size].reshape(shp))
        at += size
    return out


def kernel(x, norm_g, w_in, q_norm_g, k_norm_g, rel_bias, w_out, loss_target, m_norm_g, m_w_in, m_q_norm_g, m_k_norm_g, m_rel_bias, m_w_out, v_norm_g, v_w_in, v_q_norm_g, v_k_norm_g, v_rel_bias, v_w_out):
    nl, d, nb = w_in.shape
    s = x.shape[1]
    ds_ = d // 2
    nh = ds_ // HEAD
    rb = w_out.shape[1]
    nrel = rel_bias.shape[2]
    bx = lax.axis_index("x") * 2 + lax.axis_index("y")

    rb_rows = -(-(nl * nh * nrel) // 1024) * 8
    cx = lax.axis_index("c")
    wi_full = [_cast_block(w_in, l, bx, "cast_w_in") for l in range(nl)]
    wo_full = [_cast_block(w_out, l, bx, "cast_w_out") for l in range(nl)]
    wi_full[0], wo_full[0], rel_all = _run_comm(
        _gather_comm([wi_full[0], wo_full[0]], _pack([rel_bias], rb_rows)), "gather_first")
    rel_full = jnp.concatenate(
        [rel_all[j].reshape(-1)[:nl * nh * nrel].reshape(nl, nh, nrel) for j in range(4)], axis=2)
    ext_idx = _ext_index()
    onehot = jnp.asarray(ext_idx[:, None] == np.arange(N_REL)[None, :], F32)
    ext = jnp.einsum("lhr,ur->lhu", rel_full, onehot, precision=lax.Precision.HIGHEST).reshape(nl, nh, 1, EXT)

    xs, hs, projs, yas, lts, ybs, mixes, wi_t = [], [], [], [], [], [], [], []
    xc = x[0]
    for l in range(nl):
        h = _rmsnorm_fwd(xc, norm_g[l:l + 1])
        nxt = l + 1 < nl
        (proj, wt), got = _in_proj(h, wi_full[l], _gather_comm(
            [wo_full[l + 1]], stage="chips", chips_at=(0, 1)) if nxt else None)
        wi_t.append(wt)
        if nxt:
            wo_full[l + 1] = got[0]
        (ya, lt), got = _attn_a_fwd(proj, nh, _gather_comm(
            [wi_full[l + 1]], stage="chips", chips_at=(0, 1)) if nxt else None)
        if nxt:
            wi_full[l + 1] = got[0]
        (yb,), got = _attn_b_fwd(proj, q_norm_g[l:l + 1], k_norm_g[l:l + 1], ext[l], nh, _gather_comm(
            [wi_full[l + 1], wo_full[l + 1]], stage="chips", chips_at=(2, 3)) if nxt else None)
        if nxt:
            wi_full[l + 1], wo_full[l + 1] = got
        xs.append(xc)
        wo_now = wo_full[l].reshape(4 * rb, d)
        (xc, mix), got = _out_proj(xc, ya, yb, proj, wo_now, _gather_comm(
            [wi_full[l + 1], wo_full[l + 1]], stage="sibling") if nxt else None)
        if nxt:
            wi_full[l + 1], wo_full[l + 1] = got
        hs.append(h); projs.append(proj); yas.append(ya); lts.append(lt); ybs.append(yb); mixes.append(mix)
    dx, loss_tile = _loss_head(xc, loss_target[0])

    small, g_wi, g_wo = [None] * nl, [None] * nl, [None] * nl
    pending, tail = None, None

    def keep(lay, shared):
        g_wi[lay], g_wo[lay] = shared[0].reshape(d, nb), shared[1].reshape(rb, d)

    for l in reversed(range(nl)):
        wo = wo_full[l].reshape(4 * rb, d)
        p_wo = _wgrad(mixes[l], dx, 4, False, "wgrad_out")
        last = l == 0 and pending is not None
        (dya, dyb, dga, dgb), theirs_wo = _out_proj_bwd(dx, yas[l], ybs[l], projs[l], wo,
                                                        _sibling_comm([p_wo]) if last else None)
        travelling = list(pending[1]) if pending else []
        if last:
            sums_wo = _add_sibling(p_wo, theirs_wo[0], cx)
            travelling.append(sums_wo)
        (dqa, dka, dva), got = _attn_a_bwd(projs[l], lts[l], dya, nh, _chips_comm(travelling) if pending else None)
        tots = [_add_chips(pending[1][t], got[t], bx, cx) for t in range(2)] if pending else None
        (dqb, dkb, dvb, dqg, dkg, dext), shared = _attn_b_bwd(
            projs[l], dyb, q_norm_g[l:l + 1], k_norm_g[l:l + 1], ext[l], nh, _halves_comm(tots) if pending else None)
        if pending:
            keep(pending[0], shared)
        dproj = jnp.concatenate([dqa, dka, dva, dga, dqb, dkb, dvb, dgb], axis=1)
        parts = [_wgrad(hs[l], dproj, 4, True, "wgrad_in"), p_wo]
        if l > 0:
            (dx, dng), theirs = _in_proj_bwd(dproj, wi_t[l],xs[l], dx, norm_g[l:l + 1], _sibling_comm(parts))
            pending = (l, [_add_sibling(parts[t], theirs[t], cx) for t in range(2)])
        elif last:
            theirs = _run_comm(_sibling_comm(parts[:1]), "reduce_sibling")
            sums_wi = _add_sibling(parts[0], theirs[0], cx)
            (dx, dng), got_wi = _in_proj_bwd(dproj, wi_t[l],xs[l], dx, norm_g[l:l + 1],
                                             _chips_comm([sums_wi], pieces=(0, 1)))
            tail = dict(sums_wi=sums_wi, got_wi=got_wi, sums_wo=sums_wo, got_wo=got[2])
        else:
            theirs = _run_comm(_sibling_comm(parts), "reduce_sibling")
            sums = [_add_sibling(parts[t], theirs[t], cx) for t in range(2)]
            (dx, dng), got = _in_proj_bwd(dproj, wi_t[l],xs[l], dx, norm_g[l:l + 1], _chips_comm(sums))
            keep(0, _run_comm(_halves_comm([_add_chips(sums[t], got[t], bx, cx) for t in range(2)]), "share_halves"))
        small[l] = (dng[0], jnp.sum(dqg, axis=0).reshape(-1), jnp.sum(dkg, axis=0).reshape(-1), dext.reshape(nh, EXT))
    grad_x = dx[None]

    small_shapes = [(nl, d), (nl, HEAD), (nl, HEAD), (nl, nh, EXT), (1,)]
    small_parts = [jnp.stack([sm[i] for sm in small]) for i in range(4)] + [loss_tile[0, :1]]
    rows = -(-sum(int(np.prod(sh)) for sh in small_shapes) // 1024) * 8
    tot = _sum_devices(_gather_small(_pack(small_parts, rows)))
    g_ng, g_qg, g_kg, g_ext, loss = _unpack(tot, small_shapes)
    g_rel_full = jnp.einsum("lhu,ur->lhr", g_ext, onehot, precision=lax.Precision.HIGHEST)
    g_rel = lax.dynamic_slice_in_dim(g_rel_full, bx * nrel, nrel, axis=2)

    def tail_step(k):
        if k < 2:
            return _chips_comm([tail["sums_wi"]], pieces=(2 + k,), arrived=tail["got_wi"])
        return _halves_comm([_add_chips(tail["sums_wi"], tail["got_wi"][0], bx, cx),
                             _add_chips(tail["sums_wo"], tail["got_wo"], bx, cx)])

    def tail_done(k, res):
        if k < 2:
            tail["got_wi"] = res
        else:
            keep(0, res)

    res_wi, res_wo, steps = (), (), 0
    for l in list(range(nl - 1, 0, -1)) + [0]:
        comm = None
        if tail is not None:
            if l == 0:
                for k in range(steps, 3):
                    tail_done(k, _run_comm(tail_step(k), "gradient_tail"))
            elif steps < 3:
                comm = tail_step(steps)
        res_wi, got = _adamw_layer(l, w_in, g_wi[l], m_w_in, v_w_in, res_wi, "adamw_w_in", comm)
        if comm is not None:
            tail_done(steps, got)
            steps += 1
        res_wo, _ = _adamw_layer(l, w_out, g_wo[l], m_w_out, v_w_out, res_wo, "adamw_w_out")
    g_wi, d_wi, nm_wi, nv_wi = res_wi
    g_wo, d_wo, nm_wo, nv_wo = res_wo
    sm_shapes = [(nl, d), (nl, HEAD), (nl, HEAD), (nl, nh, nrel)]
    sm_rows = -(-sum(int(np.prod(sh)) for sh in sm_shapes) // 1024) * 8
    pw, pg, pm, pv = [_pack(group, sm_rows) for group in (
        (norm_g, q_norm_g, k_norm_g, rel_bias), (g_ng, g_qg, g_kg, g_rel),
        (m_norm_g, m_q_norm_g, m_k_norm_g, m_rel_bias), (v_norm_g, v_q_norm_g, v_k_norm_g, v_rel_bias))]
    d_sm, nm_sm, nv_sm = [_unpack(a[0], sm_shapes)
                          for a in _adamw_layer(0, pw[None], pg, pm[None], pv[None], (), "adamw_small")[0][1:]]

    return (loss[0], grad_x, g_ng, g_wi, g_qg, g_kg, g_rel, g_wo,
            d_sm[0], d_wi, d_sm[1], d_sm[2], d_sm[3], d_wo,
            nm_sm[0], nm_wi, nm_sm[1], nm_sm[2], nm_sm[3], nm_wo,
            nv_sm[0], nv_wi, nv_sm[1], nv_sm[2], nv_sm[3], nv_wo)
```

```python
from typing import Callable, NamedTuple

import jax
import jax.numpy as jnp
import numpy as np
from jax import lax
from jax.experimental import pallas as pl
from jax.experimental.pallas import tpu as pltpu

F32 = jnp.float32
BF16 = jnp.bfloat16

HEAD = 128
CHUNK = 64
LEFT_CHUNKS = 8
REL_CLIP = 256
N_REL = REL_CLIP + CHUNK
NORM_EPS = 1e-6
NEG_BIG = -1e30
TQ = 256
TK = TQ
QB = 512
ROWS = 32
PAD = LEFT_CHUNKS * CHUNK
WIN = PAD + TQ
EXT = 1024
SCALE = HEAD ** -0.5

ADAM_LR = 0.001
ADAM_B1 = 0.9
ADAM_B2 = 0.999
ADAM_EPS = 1e-08
ADAM_WD = 0.01
ADAM_STEP = 10

ANY = pl.BlockSpec(memory_space=pl.ANY)
MESH = pl.DeviceIdType.MESH


def _params(sem=None, vmem_mb=None):
    kw = {}
    if sem is not None:
        kw["dimension_semantics"] = sem
    if vmem_mb is not None:
        kw["vmem_limit_bytes"] = vmem_mb << 20
    return pltpu.CompilerParams(**kw)


class _Comm(NamedTuple):
    ins: tuple
    outs: tuple
    aliases: dict
    n_sems: int
    start: Callable
    finish: Callable


def _call(body, *, name, grid, in_specs, out_specs, out_shape, args, scratch=(), sem=None, vmem_mb=None, comm=None):
    if comm is None:
        out = pl.pallas_call(body, name=name, grid=grid, in_specs=in_specs, out_specs=out_specs, out_shape=out_shape,
                             scratch_shapes=list(scratch), compiler_params=_params(sem, vmem_mb))(*args)
        return out, ()
    n_in, n_out, n_ci, n_co = len(in_specs), len(out_shape), len(comm.ins), len(comm.outs)

    def hosted(*refs):
        ins, cins = refs[:n_in], refs[n_in:n_in + n_ci]
        outs, couts = refs[n_in + n_ci:n_in + n_ci + n_out], refs[n_in + n_ci + n_out:n_in + n_ci + n_out + n_co]
        rest = refs[n_in + n_ci + n_out + n_co:]
        send_sems, recv_sems = rest[-2:]
        first, last = None, None
        for ax, size in enumerate(grid):
            at = pl.program_id(ax)
            first = (at == 0) if first is None else first & (at == 0)
            last = (at == size - 1) if last is None else last & (at == size - 1)

        @pl.when(first)
        def _():
            comm.start(cins, couts, send_sems, recv_sems)

        body(*ins, *outs, *rest[:-2])

        @pl.when(last)
        def _():
            comm.finish(cins, couts, send_sems, recv_sems)

    out = pl.pallas_call(
        hosted, name=name, grid=grid, in_specs=list(in_specs) + [ANY] * n_ci, out_specs=list(out_specs) + [ANY] * n_co,
        out_shape=list(out_shape) + list(comm.outs),
        input_output_aliases={n_in + k: n_out + v for k, v in comm.aliases.items()},
        scratch_shapes=list(scratch) + [pltpu.SemaphoreType.DMA((comm.n_sems,)), pltpu.SemaphoreType.DMA((comm.n_sems,))],
        compiler_params=_params(("arbitrary",) * len(grid), vmem_mb))(*args, *comm.ins)
    return out[:n_out], out[n_out:]


def _dot(a, b):
    return jnp.dot(a, b, preferred_element_type=F32)


def _dot_nt(a, b):
    return lax.dot_general(a, b, (((1,), (1,)), ((), ())), preferred_element_type=F32)


def _dot_tn(a, b):
    return lax.dot_general(a, b, (((0,), (0,)), ((), ())), preferred_element_type=F32)


def _split_dot(x, m):
    hi = x.astype(BF16)
    lo = (x - hi.astype(F32)).astype(BF16)
    return _dot(hi, m) + _dot(lo, m)


def _silu_parts(g):
    sg = 1.0 / (1.0 + jnp.exp(-g))
    return g * sg, sg * (1.0 + g * (1.0 - sg))


def _idx(*vals):
    return jnp.stack([jnp.asarray(v, jnp.int32) for v in vals])


def _cast_block(w, l, blk, name):
    _, r, c = w.shape
    tr = min(r, 512)

    def body(b_ref, w_ref, o_ref):
        o_ref[...] = w_ref[...].astype(BF16)

    spec = pltpu.PrefetchScalarGridSpec(
        num_scalar_prefetch=1, grid=(r // tr,),
        in_specs=[pl.BlockSpec((None, tr, c), lambda i, b: (l, i, 0))],
        out_specs=pl.BlockSpec((None, tr, c), lambda i, b: (b[0], i, 0)))
    return pl.pallas_call(body, name=name, grid_spec=spec, out_shape=jax.ShapeDtypeStruct((4, r, c), BF16),
                          compiler_params=_params(("parallel",)))(_idx(blk), w)


def _add_sibling(p, theirs, core):
    nblk, r, c = p.shape
    hr = r // 2
    tr = min(hr, 256)
    per = hr // tr

    def body(c_ref, p_ref, t_ref, o_ref):
        o_ref[...] = (p_ref[...].astype(F32) + t_ref[...].astype(F32)).astype(BF16)

    blk = pl.BlockSpec((None, tr, c), lambda j, i, cr: (j, i, 0))
    spec = pltpu.PrefetchScalarGridSpec(
        num_scalar_prefetch=1, grid=(nblk, per),
        in_specs=[pl.BlockSpec((None, tr, c), lambda j, i, cr: (j, cr[0] * per + i, 0)), blk], out_specs=blk)
    return pl.pallas_call(body, name="add_sibling", grid_spec=spec, out_shape=jax.ShapeDtypeStruct((nblk, hr, c), BF16),
                          compiler_params=_params(("parallel", "parallel")))(_idx(core), p, theirs)


def _add_chips(sums, got, blk, core):
    _, hr, c = sums.shape
    tr = min(hr, 256)

    def body(i_ref, s_ref, g0_ref, g1_ref, g2_ref, o_ref):
        o_ref[...] = ((s_ref[...].astype(F32) + g0_ref[...].astype(F32))
                      + g1_ref[...].astype(F32)) + g2_ref[...].astype(F32)

    at = lambda j: pl.BlockSpec((None, tr, c), lambda i, ir: (j, i, 0))
    spec = pltpu.PrefetchScalarGridSpec(
        num_scalar_prefetch=1, grid=(hr // tr,),
        in_specs=[pl.BlockSpec((None, tr, c), lambda i, ir: (ir[0], i, 0)), at(0), at(1), at(2)],
        out_specs=pl.BlockSpec((None, tr, c), lambda i, ir: (ir[1], i, 0)))
    return pl.pallas_call(body, name="add_chips", grid_spec=spec, out_shape=jax.ShapeDtypeStruct((2, hr, c), F32),
                          compiler_params=_params(("parallel",)))(_idx(blk, core), sums, got, got, got)


def _adamw_layer(l, w, g, m, v, prev, name):
    nl, r, c = w.shape
    tr = min(r, 256)
    c1 = 1.0 / (1.0 - ADAM_B1 ** ADAM_STEP)
    c2 = 1.0 / (1.0 - ADAM_B2 ** ADAM_STEP)

    def body(w_ref, g_ref, m_ref, v_ref, *rest):
        go_ref, d_ref, nm_ref, nv_ref = rest[-4:]
        gg = g_ref[...]
        nm = ADAM_B1 * m_ref[...] + (1.0 - ADAM_B1) * gg
        nv = ADAM_B2 * v_ref[...] + (1.0 - ADAM_B2) * (gg * gg)
        upd = (nm * c1) / (jnp.sqrt(nv * c2) + ADAM_EPS) + ADAM_WD * w_ref[...]
        go_ref[...] = gg
        d_ref[...] = -ADAM_LR * upd
        nm_ref[...] = nm
        nv_ref[...] = nv

    lay = pl.BlockSpec((None, tr, c), lambda i: (l, i, 0))
    shp = jax.ShapeDtypeStruct((nl, r, c), F32)
    return pl.pallas_call(
        body, name=name, grid=(r // tr,),
        in_specs=[lay, pl.BlockSpec((tr, c), lambda i: (i, 0)), lay, lay] + [ANY] * len(prev),
        out_specs=[lay] * 4, out_shape=[shp] * 4, input_output_aliases={4 + k: k for k in range(len(prev))},
        compiler_params=_params(("parallel",), 40))(w, g, m, v, *prev)


def _rmsnorm_fwd(x, g):
    s, d = x.shape
    tm = min(s, 256)

    def body(x_ref, g_ref, h_ref):
        xv = x_ref[...]
        r = lax.rsqrt(jnp.mean(xv * xv, axis=1, keepdims=True) + NORM_EPS)
        h_ref[...] = (xv * r * g_ref[...]).astype(BF16)

    return pl.pallas_call(
        body, name="rmsnorm_fwd", grid=(s // tm,),
        in_specs=[pl.BlockSpec((tm, d), lambda i: (i, 0)), pl.BlockSpec((1, d), lambda i: (0, 0))],
        out_specs=pl.BlockSpec((tm, d), lambda i: (i, 0)),
        out_shape=jax.ShapeDtypeStruct((s, d), BF16),
        compiler_params=_params(("parallel",)))(x, g)


def _in_proj(h, w, comm=None):
    s, d = h.shape
    nblk, _, nb = w.shape
    tm, tn = min(s, 1024), min(nb, 1024)
    per = nb // tn

    def body(h_ref, w_ref, o_ref, wt_ref):
        o_ref[...] = _dot(h_ref[...], w_ref[...])

        @pl.when(pl.program_id(1) == 0)
        def _():
            for c in range(tn // HEAD):
                wt_ref[pl.ds(c * HEAD, HEAD), :] = w_ref[:, c * HEAD:(c + 1) * HEAD].astype(F32).T.astype(BF16)

    return _call(
        body, name="in_proj", grid=(nblk * per, s // tm),
        in_specs=[pl.BlockSpec((tm, d), lambda n, m: (m, 0)),
                  pl.BlockSpec((None, d, tn), lambda n, m: (n // per, 0, n % per))],
        out_specs=[pl.BlockSpec((tm, tn), lambda n, m: (m, n)), pl.BlockSpec((tn, d), lambda n, m: (n, 0))],
        out_shape=[jax.ShapeDtypeStruct((s, nblk * nb), F32), jax.ShapeDtypeStruct((nblk * nb, d), BF16)],
        sem=("parallel", "arbitrary"), vmem_mb=56, args=(h, w), comm=comm)


def _heads_per_step(nh):
    return 2 if nh % 2 == 0 else 1


def _head(hh):
    return slice(hh * HEAD, (hh + 1) * HEAD)


def _tri(op):
    r = lax.broadcasted_iota(jnp.int32, (TQ, TQ), 0)
    c = lax.broadcasted_iota(jnp.int32, (TQ, TQ), 1)
    return op(r, c)


def _staggered(groups, hp):
    for hh in range(hp):
        for fn in groups[0]:
            fn(hh)
    for group in groups[1:]:
        for hh in range(hp):
            for fn in group:
                fn(hh)


def _transpose_tiles(src_ref, dst_ref, hp, nt):
    for hh in range(hp):
        for t in range(nt):
            dst_ref[hh, t] = src_ref[pl.ds(t * TK, TK), _head(hh)].T.astype(BF16)


def _chunk_mask(r, a):
    if a is None or r * ROWS >= (a + 1) * TK:
        return None
    if (r + 1) * ROWS <= a * TK:
        return False
    row = lax.broadcasted_iota(jnp.int32, (ROWS, TK), 0) + r * ROWS
    return row > lax.broadcasted_iota(jnp.int32, (ROWS, TK), 1) + a * TK


def _sb_logs(qk, causal):
    z = qk * SCALE
    l1p = jnp.log(1.0 + jnp.exp(-jnp.abs(z)))
    ls = jnp.minimum(-z, 0.0) - l1p
    if causal is not None:
        ls = jnp.where(causal, ls, 0.0)
    return ls, jnp.minimum(z, 0.0) - l1p


def _attn_a_fwd(proj, nh, comm=None):
    s = proj.shape[0]
    qb_rows = min(QB, s)
    nq, nt, per = s // qb_rows, s // TK, qb_rows // TK
    hp = _heads_per_step(nh)
    ng = nh // hp
    chunks = qb_rows // ROWS

    def body(q_ref, k_ref, v_ref, o_ref, lt_ref, acc_ref, qb_ref, m_ref, car_ref, sum_ref,
             z_ref, lsig_ref, aft_ref, hi_ref, lo_ref, w_ref, kt_ref):
        i = pl.program_id(1)

        @pl.when(i == 0)
        def _():
            _transpose_tiles(k_ref, kt_ref, hp, nt)

        qb_ref[...] = q_ref[...].astype(BF16)
        m_ref[...] = _tri(lambda r, c: r > c).astype(BF16)
        acc_ref[...] = jnp.zeros_like(acc_ref)
        car_ref[...] = jnp.zeros_like(car_ref)

        def tile(j, a):
            off = pl.multiple_of(j * TK, TK)

            def scores(hh):
                z_ref[hh] = _dot(qb_ref[:, _head(hh)], kt_ref[hh, j])

            def logs(hh):
                for r in range(chunks):
                    sl, mask = pl.ds(r * ROWS, ROWS), _chunk_mask(r, a)
                    if mask is False:
                        hi_ref[hh, sl, :] = jnp.zeros((ROWS, TK), BF16)
                        lo_ref[hh, sl, :] = jnp.zeros((ROWS, TK), BF16)
                        continue
                    ls, lsig = _sb_logs(z_ref[hh, sl, :], mask)
                    lsig_ref[hh, sl, :] = lsig
                    hi = ls.astype(BF16)
                    hi_ref[hh, sl, :] = hi
                    lo_ref[hh, sl, :] = (ls - hi.astype(F32)).astype(BF16)
                    sum_ref[hh, sl, :] = jnp.sum(ls, axis=1, keepdims=True)

            def after(hh):
                aft_ref[hh] = _dot(hi_ref[hh], m_ref[...]) + _dot(lo_ref[hh], m_ref[...])

            def weights(hh):
                for r in range(chunks):
                    sl, mask = pl.ds(r * ROWS, ROWS), _chunk_mask(r, a)
                    if mask is False:
                        w_ref[hh, sl, :] = jnp.zeros((ROWS, TK), BF16)
                        continue
                    w = jnp.exp(lsig_ref[hh, sl, :] + aft_ref[hh, sl, :] + car_ref[hh, sl, :])
                    if mask is not None:
                        w = jnp.where(mask, w, 0.0)
                    w_ref[hh, sl, :] = w.astype(BF16)
                    car_ref[hh, sl, :] += sum_ref[hh, sl, :]

            def values(hh):
                vb = v_ref[pl.ds(off, TK), _head(hh)].astype(BF16)
                acc_ref[:, _head(hh)] += _dot(w_ref[hh], vb)

            _staggered([(scores,), (logs, after), (weights, values)], hp)

        for a in reversed(range(per)):
            tile(i * per + a, a)

        def step(t, carry):
            tile(i * per - 1 - t, None)
            return carry

        lax.fori_loop(0, i * per, step, 0)
        o_ref[...] = acc_ref[...]
        lt_ref[...] = car_ref[...]

    wd = hp * HEAD
    sq = lambda dt: pltpu.VMEM((hp, qb_rows, TK), dt)
    return _call(
        body, name="attn_a_fwd", grid=(ng, nq),
        in_specs=[pl.BlockSpec((qb_rows, wd), lambda h, i: (i, h)),
                  pl.BlockSpec((s, wd), lambda h, i: (0, ng + h)),
                  pl.BlockSpec((s, wd), lambda h, i: (0, 2 * ng + h))],
        out_specs=[pl.BlockSpec((qb_rows, wd), lambda h, i: (i, h)),
                   pl.BlockSpec((hp, qb_rows, 1), lambda h, i: (h, i, 0))],
        out_shape=[jax.ShapeDtypeStruct((s, nh * HEAD), F32), jax.ShapeDtypeStruct((nh, s, 1), F32)],
        scratch=[pltpu.VMEM((qb_rows, wd), F32), pltpu.VMEM((qb_rows, wd), BF16), pltpu.VMEM((TK, TK), BF16),
                 pltpu.VMEM((hp, qb_rows, 1), F32), pltpu.VMEM((hp, qb_rows, 1), F32),
                 sq(F32), sq(F32), sq(F32), sq(BF16), sq(BF16), sq(BF16),
                 pltpu.VMEM((hp, nt, HEAD, TK), BF16)],
        sem=("parallel", "arbitrary"), args=(proj, proj, proj), comm=comm)


def _band_valid(i):
    cl = lax.broadcasted_iota(jnp.int32, (TQ, WIN), 0) // CHUNK
    kl = lax.broadcasted_iota(jnp.int32, (TQ, WIN), 1) // CHUNK
    first = LEFT_CHUNKS - (TQ // CHUNK) * i
    return (kl >= cl) & (kl <= cl + LEFT_CHUNKS) & (kl >= first)


def _build_bias(e_ref, bias_ref):
    e8 = jnp.broadcast_to(e_ref[...], (8, EXT))
    row = lax.broadcasted_iota(jnp.int32, (8, EXT), 0)
    t8 = jnp.zeros((8, EXT), F32)
    for b in range(8):
        t8 = jnp.where(row == b, pltpu.roll(e8, b, 1) if b else e8, t8)
    for a in range(TQ // 8):
        sl = pltpu.roll(t8, 8 * a, 1) if a else t8
        bias_ref[pl.ds(8 * a, 8), :] = sl[:, :WIN]


def _reduce_bias_grad(db_ref):
    acc = jnp.zeros((8, EXT), F32)
    for a in range(TQ // 8):
        sl = db_ref[pl.ds(8 * a, 8), :]
        acc = acc + (pltpu.roll(sl, EXT - 8 * a, 1) if a else sl)
    row = lax.broadcasted_iota(jnp.int32, (8, EXT), 0)
    tot = jnp.zeros((8, EXT), F32)
    for b in range(8):
        tot = tot + jnp.where(row == b, pltpu.roll(acc, EXT - b, 1) if b else acc, 0.0)
    return jnp.sum(tot, axis=0, keepdims=True)


def _band_fill(k_ref, v_ref, kg_ref, kn_pad, v_pad, s):
    k = k_ref[...]
    rk = lax.rsqrt(jnp.mean(k * k, axis=1, keepdims=True) + NORM_EPS)
    kn_pad[pl.ds(0, PAD), :] = jnp.zeros((PAD, HEAD), BF16)
    kn_pad[pl.ds(PAD, s), :] = (k * rk * kg_ref[...]).astype(BF16)
    v_pad[pl.ds(0, PAD), :] = jnp.zeros((PAD, HEAD), BF16)
    v_pad[pl.ds(PAD, s), :] = v_ref[...].astype(BF16)


def _band_probs(q_ref, qg_ref, kn_pad, bias_ref, i):
    q = q_ref[...]
    rq = lax.rsqrt(jnp.mean(q * q, axis=1, keepdims=True) + NORM_EPS)
    qhat = q * rq
    qn = (qhat * qg_ref[...]).astype(BF16)
    off = pl.multiple_of(i * TQ, TQ)
    kw = kn_pad[pl.ds(off, WIN), :]
    sc = _dot_nt(qn, kw) * SCALE + bias_ref[...]
    sc = jnp.where(_band_valid(i), sc, NEG_BIG)
    p = jnp.exp(sc - jnp.max(sc, axis=1, keepdims=True))
    pn = p / jnp.sum(p, axis=1, keepdims=True)
    return rq, qhat, qn, kw, off, pn


def _attn_b_fwd(proj, qg, kg, ext, nh, comm=None):
    s = proj.shape[0]
    nq = s // TQ

    def body(q_ref, k_ref, v_ref, qg_ref, kg_ref, e_ref, o_ref, kn_pad, v_pad, bias_ref):
        i = pl.program_id(1)

        @pl.when(i == 0)
        def _():
            _band_fill(k_ref, v_ref, kg_ref, kn_pad, v_pad, s)
            _build_bias(e_ref, bias_ref)

        _, _, _, _, off, pn = _band_probs(q_ref, qg_ref, kn_pad, bias_ref, i)
        o_ref[...] = _dot(pn.astype(BF16), v_pad[pl.ds(off, WIN), :])

    vec = pl.BlockSpec((1, HEAD), lambda h, i: (0, 0))
    return _call(
        body, name="attn_b_fwd", grid=(nh, nq),
        in_specs=[pl.BlockSpec((TQ, HEAD), lambda h, i: (i, 4 * nh + h)),
                  pl.BlockSpec((s, HEAD), lambda h, i: (0, 5 * nh + h)),
                  pl.BlockSpec((s, HEAD), lambda h, i: (0, 6 * nh + h)),
                  vec, vec,
                  pl.BlockSpec((None, 1, EXT), lambda h, i: (h, 0, 0))],
        out_specs=[pl.BlockSpec((TQ, HEAD), lambda h, i: (i, h))],
        out_shape=[jax.ShapeDtypeStruct((s, nh * HEAD), F32)],
        scratch=[pltpu.VMEM((s + PAD, HEAD), BF16), pltpu.VMEM((s + PAD, HEAD), BF16), pltpu.VMEM((TQ, WIN), F32)],
        sem=("parallel", "arbitrary"), args=(proj, proj, proj, qg, kg, ext), comm=comm)


def _out_proj(x, ya, yb, proj, w, then, comm=None):
    s, d = x.shape
    ds_ = ya.shape[1]
    tm = min(s, 256)
    final = then.shape[0] == s

    def body(x_ref, ya_ref, yb_ref, ga_ref, gb_ref, w_ref, then_ref, mix_ref, a_ref, b_ref):
        ma = (ya_ref[...] * _silu_parts(ga_ref[...])[0]).astype(BF16)
        mb = (yb_ref[...] * _silu_parts(gb_ref[...])[0]).astype(BF16)
        mix_ref[:, :ds_] = ma
        mix_ref[:, ds_:] = mb
        y = x_ref[...] + _dot(ma, w_ref[pl.ds(0, ds_), :]) + _dot(mb, w_ref[pl.ds(ds_, ds_), :])
        if final:
            @pl.when(pl.program_id(0) == 0)
            def _():
                b_ref[...] = jnp.zeros_like(b_ref)

            err = y - then_ref[...]
            a_ref[...] = err * (1.0 / d)
            b_ref[...] += 0.5 * jnp.sum(jnp.mean(err * err, axis=1, keepdims=True), axis=0, keepdims=True)
        else:
            a_ref[...] = y
            r = lax.rsqrt(jnp.mean(y * y, axis=1, keepdims=True) + NORM_EPS)
            b_ref[...] = (y * r * then_ref[...]).astype(BF16)

    row = lambda width: pl.BlockSpec((tm, width), lambda i: (i, 0))
    last_spec = pl.BlockSpec((8, 128), lambda i: (0, 0)) if final else row(d)
    last_shape = jax.ShapeDtypeStruct((8, 128), F32) if final else jax.ShapeDtypeStruct((s, d), BF16)
    return _call(
        body, name="out_proj", grid=(s // tm,),
        in_specs=[row(d), row(ds_), row(ds_),
                  pl.BlockSpec((tm, ds_), lambda i: (i, 3)), pl.BlockSpec((tm, ds_), lambda i: (i, 7)),
                  pl.BlockSpec((2 * ds_, d), lambda i: (0, 0)),
                  row(d) if final else pl.BlockSpec((1, d), lambda i: (0, 0))],
        out_specs=[row(2 * ds_), row(d), last_spec],
        out_shape=[jax.ShapeDtypeStruct((s, 2 * ds_), BF16), jax.ShapeDtypeStruct((s, d), F32), last_shape],
        sem=("arbitrary",) if final else ("parallel",), vmem_mb=52, args=(x, ya, yb, proj, proj, w, then), comm=comm)


def _out_proj_bwd(dxo, ya, yb, proj, w, comm=None):
    s, d = dxo.shape
    ds_ = ya.shape[1]
    tm = min(s, 256)

    def body(dx_ref, ya_ref, yb_ref, ga_ref, gb_ref, w_ref, dya_ref, dyb_ref, dga_ref, dgb_ref):
        dxb = dx_ref[...].astype(BF16)
        for y_ref, g_ref, lo, dy_ref, dg_ref in ((ya_ref, ga_ref, 0, dya_ref, dga_ref),
                                                 (yb_ref, gb_ref, ds_, dyb_ref, dgb_ref)):
            dmix = _dot_nt(dxb, w_ref[pl.ds(lo, ds_), :])
            act, dact = _silu_parts(g_ref[...])
            dy_ref[...] = dmix * act
            dg_ref[...] = (dmix * y_ref[...] * dact).astype(BF16)

    row = lambda width: pl.BlockSpec((tm, width), lambda i: (i, 0))
    return _call(
        body, name="out_proj_bwd", grid=(s // tm,),
        in_specs=[row(d), row(ds_), row(ds_),
                  pl.BlockSpec((tm, ds_), lambda i: (i, 3)), pl.BlockSpec((tm, ds_), lambda i: (i, 7)),
                  pl.BlockSpec((2 * ds_, d), lambda i: (0, 0))],
        out_specs=[row(ds_)] * 4,
        out_shape=[jax.ShapeDtypeStruct((s, ds_), F32)] * 2 + [jax.ShapeDtypeStruct((s, ds_), BF16)] * 2,
        sem=("parallel",), vmem_mb=48, args=(dxo, ya, yb, proj, proj, w), comm=comm)


def _wgrad(a, b, nblk, col_blocks, name):
    s, m = a.shape
    n = b.shape[1]
    if col_blocks:
        tr = min(m, 1024)
        nb = n // nblk
        tn = min(nb, 2048)
        per = nb // tn
        out_shape = (nblk, m, nb)
        out_spec = pl.BlockSpec((None, tr, tn), lambda j, r: (j // per, r, j % per))
    else:
        tn = min(n, 1024)
        tr = m // nblk
        out_shape = (nblk, tr, n)
        out_spec = pl.BlockSpec((None, tr, tn), lambda j, r: (r, 0, j))

    def body(a_ref, b_ref, o_ref):
        o_ref[...] = _dot_tn(a_ref[...].astype(BF16), b_ref[...].astype(BF16)).astype(BF16)

    return pl.pallas_call(
        body, name=name, grid=(n // tn, m // tr),
        in_specs=[pl.BlockSpec((s, tr), lambda j, r: (0, r)), pl.BlockSpec((s, tn), lambda j, r: (0, j))],
        out_specs=out_spec, out_shape=jax.ShapeDtypeStruct(out_shape, BF16),
        compiler_params=_params(("parallel", "parallel"), 48))(a, b)


def _attn_a_bwd(proj, lt, dya, nh, comm=None):
    s = proj.shape[0]
    qb_rows = min(TK, s)
    nq, nt, per = s // qb_rows, s // TK, qb_rows // TK
    hp = _heads_per_step(nh)
    ng = nh // hp
    chunks = qb_rows // ROWS

    def body(q_ref, k_ref, v_ref, lt_ref, do_ref, dq_ref, dk_ref, dv_ref, dq_acc, dk_acc, dv_acc,
             qb_ref, dob_ref, upto_ref, before_ref, cls_ref, cg_ref, sls_ref, sg_ref,
             z_ref, dw_ref, lsig_ref, pre_ref, g_ref, hi_ref, lo_ref, wb_ref, kt_ref, vt_ref, qt_ref, dot_ref):
        i = pl.program_id(1)

        @pl.when(i == 0)
        def _():
            dk_acc[...] = jnp.zeros_like(dk_acc)
            dv_acc[...] = jnp.zeros_like(dv_acc)
            _transpose_tiles(k_ref, kt_ref, hp, nt)
            _transpose_tiles(v_ref, vt_ref, hp, nt)

        dq_acc[...] = jnp.zeros_like(dq_acc)
        qb_ref[...] = q_ref[...].astype(BF16)
        dob_ref[...] = do_ref[...].astype(BF16)
        for hh in range(hp):
            qt_ref[hh] = q_ref[:, _head(hh)].T.astype(BF16)
            dot_ref[hh] = do_ref[:, _head(hh)].T.astype(BF16)
        upto_ref[...] = _tri(lambda r, c: r <= c).astype(BF16)
        before_ref[...] = _tri(lambda r, c: r < c).astype(BF16)
        cls_ref[...] = jnp.zeros_like(cls_ref)
        cg_ref[...] = jnp.zeros_like(cg_ref)

        def tile(j, a):
            off = pl.multiple_of(j * TK, TK)
            zeros = jnp.zeros((ROWS, TK), BF16)

            def scores(hh):
                z_ref[hh] = _dot(qb_ref[:, _head(hh)], kt_ref[hh, j])
                dw_ref[hh] = _dot(dob_ref[:, _head(hh)], vt_ref[hh, j])

            def logs(hh):
                for r in range(chunks):
                    sl, mask = pl.ds(r * ROWS, ROWS), _chunk_mask(r, a)
                    if mask is False:
                        hi_ref[hh, sl, :] = zeros
                        lo_ref[hh, sl, :] = zeros
                        continue
                    ls, lsig = _sb_logs(z_ref[hh, sl, :], mask)
                    lsig_ref[hh, sl, :] = lsig
                    hi = ls.astype(BF16)
                    hi_ref[hh, sl, :] = hi
                    lo_ref[hh, sl, :] = (ls - hi.astype(F32)).astype(BF16)
                    sls_ref[hh, sl, :] = jnp.sum(ls, axis=1, keepdims=True)

            def upto(hh):
                pre_ref[hh] = _dot(hi_ref[hh], upto_ref[...]) + _dot(lo_ref[hh], upto_ref[...])

            def weights(hh):
                for r in range(chunks):
                    sl, mask = pl.ds(r * ROWS, ROWS), _chunk_mask(r, a)
                    if mask is False:
                        wb_ref[hh, sl, :] = zeros
                        hi_ref[hh, sl, :] = zeros
                        continue
                    w = jnp.exp(lsig_ref[hh, sl, :] + (lt_ref[hh, sl, :] - (cls_ref[hh, sl, :] + pre_ref[hh, sl, :])))
                    if mask is not None:
                        w = jnp.where(mask, w, 0.0)
                    wb_ref[hh, sl, :] = w.astype(BF16)
                    g = w * dw_ref[hh, sl, :]
                    g_ref[hh, sl, :] = g
                    hi_ref[hh, sl, :] = g.astype(BF16)
                    sg_ref[hh, sl, :] = jnp.sum(g, axis=1, keepdims=True)

            def earlier(hh):
                dw_ref[hh] = _dot(hi_ref[hh], before_ref[...])

            def logit_grads(hh):
                for r in range(chunks):
                    sl, mask = pl.ds(r * ROWS, ROWS), _chunk_mask(r, a)
                    if mask is False:
                        lo_ref[hh, sl, :] = zeros
                        continue
                    z = z_ref[hh, sl, :] * SCALE
                    e = jnp.exp(-jnp.abs(z))
                    rinv = 1.0 / (1.0 + e)
                    beta = jnp.where(z >= 0.0, rinv, e * rinv)
                    dz = g_ref[hh, sl, :] * (1.0 - beta) - beta * (cg_ref[hh, sl, :] + dw_ref[hh, sl, :])
                    if mask is not None:
                        dz = jnp.where(mask, dz, 0.0)
                    lo_ref[hh, sl, :] = (dz * SCALE).astype(BF16)
                    cls_ref[hh, sl, :] += sls_ref[hh, sl, :]
                    cg_ref[hh, sl, :] += sg_ref[hh, sl, :]

            def grads(hh):
                dq_acc[:, _head(hh)] += _dot(lo_ref[hh], k_ref[pl.ds(off, TK), _head(hh)].astype(BF16))
                dk_acc[hh, j] += _dot(qt_ref[hh], lo_ref[hh])
                dv_acc[hh, j] += _dot(dot_ref[hh], wb_ref[hh])

            _staggered([(scores,), (logs, upto), (weights, earlier), (logit_grads, grads)], hp)

        def step(j, carry):
            tile(j, None)
            return carry

        lax.fori_loop(0, i * per, step, 0)
        for a in range(per):
            tile(i * per + a, a)
        dq_ref[...] = dq_acc[...].astype(BF16)

        @pl.when(i == nq - 1)
        def _():
            for hh in range(hp):
                for t in range(nt):
                    dk_ref[pl.ds(t * TK, TK), _head(hh)] = dk_acc[hh, t].T.astype(BF16)
                    dv_ref[pl.ds(t * TK, TK), _head(hh)] = dv_acc[hh, t].T.astype(BF16)

    wd = hp * HEAD
    sq = lambda dt: pltpu.VMEM((hp, qb_rows, TK), dt)
    tiles = lambda dt: pltpu.VMEM((hp, nt, HEAD, TK), dt)
    blk = pl.BlockSpec((qb_rows, wd), lambda h, i: (i, h))
    col = pl.BlockSpec((s, wd), lambda h, i: (0, h))
    shp = jax.ShapeDtypeStruct((s, nh * HEAD), BF16)
    return _call(
        body, name="attn_a_bwd", grid=(ng, nq),
        in_specs=[blk,
                  pl.BlockSpec((s, wd), lambda h, i: (0, ng + h)),
                  pl.BlockSpec((s, wd), lambda h, i: (0, 2 * ng + h)),
                  pl.BlockSpec((hp, qb_rows, 1), lambda h, i: (h, i, 0)), blk],
        out_specs=[blk, col, col], out_shape=[shp] * 3,
        scratch=[pltpu.VMEM((qb_rows, wd), F32), tiles(F32), tiles(F32),
                 pltpu.VMEM((qb_rows, wd), BF16), pltpu.VMEM((qb_rows, wd), BF16),
                 pltpu.VMEM((TK, TK), BF16), pltpu.VMEM((TK, TK), BF16)]
        + [pltpu.VMEM((hp, qb_rows, 1), F32)] * 4 + [sq(F32)] * 5 + [sq(BF16)] * 3
        + [tiles(BF16), tiles(BF16), pltpu.VMEM((hp, HEAD, qb_rows), BF16), pltpu.VMEM((hp, HEAD, qb_rows), BF16)],
        sem=("parallel", "arbitrary"), args=(proj, proj, proj, lt, dya), comm=comm)


def _attn_b_bwd(proj, dyb, qg, kg, ext, nh, comm=None):
    s = proj.shape[0]
    nq = s // TQ

    def body(q_ref, k_ref, v_ref, do_ref, qg_ref, kg_ref, e_ref,
             dq_ref, dk_ref, dv_ref, dqg_ref, dkg_ref, de_ref,
             kn_pad, v_pad, bias_ref, db_acc, dkn_acc, dv_acc):
        i = pl.program_id(1)

        @pl.when(i == 0)
        def _():
            _band_fill(k_ref, v_ref, kg_ref, kn_pad, v_pad, s)
            _build_bias(e_ref, bias_ref)
            db_acc[...] = jnp.zeros_like(db_acc)
            dkn_acc[...] = jnp.zeros_like(dkn_acc)
            dv_acc[...] = jnp.zeros_like(dv_acc)
            dqg_ref[...] = jnp.zeros_like(dqg_ref)

        rq, qhat, qn, kw, off, pn = _band_probs(q_ref, qg_ref, kn_pad, bias_ref, i)
        dob = do_ref[...].astype(BF16)
        dp = _dot_nt(dob, v_pad[pl.ds(off, WIN), :])
        dsc = pn * (dp - jnp.sum(pn * dp, axis=1, keepdims=True))
        db_acc[:, :WIN] += dsc
        dsb = (dsc * SCALE).astype(BF16)
        dqn = _dot(dsb, kw)
        dkn_acc[pl.ds(off, WIN), :] += _dot_tn(dsb, qn)
        dv_acc[pl.ds(off, WIN), :] += _dot_tn(pn.astype(BF16), dob)
        dqh = dqn * qg_ref[...]
        dq_ref[...] = (rq * (dqh - qhat * jnp.mean(dqh * qhat, axis=1, keepdims=True))).astype(BF16)
        dqg_ref[...] += jnp.sum(dqn * qhat, axis=0, keepdims=True)

        @pl.when(i == nq - 1)
        def _():
            k = k_ref[...]
            rk = lax.rsqrt(jnp.mean(k * k, axis=1, keepdims=True) + NORM_EPS)
            khat = k * rk
            dkn = dkn_acc[pl.ds(PAD, s), :]
            dkh = dkn * kg_ref[...]
            dk_ref[...] = (rk * (dkh - khat * jnp.mean(dkh * khat, axis=1, keepdims=True))).astype(BF16)
            dkg_ref[...] = jnp.sum(dkn * khat, axis=0, keepdims=True)
            dv_ref[...] = dv_acc[pl.ds(PAD, s), :].astype(BF16)
            de_ref[...] = _reduce_bias_grad(db_acc)

    blk = pl.BlockSpec((TQ, HEAD), lambda h, i: (i, h))
    col = pl.BlockSpec((s, HEAD), lambda h, i: (0, h))
    vec = pl.BlockSpec((1, HEAD), lambda h, i: (0, 0))
    hvec = pl.BlockSpec((None, 1, HEAD), lambda h, i: (h, 0, 0))
    hext = pl.BlockSpec((None, 1, EXT), lambda h, i: (h, 0, 0))
    shp = jax.ShapeDtypeStruct((s, nh * HEAD), BF16)
    return _call(
        body, name="attn_b_bwd", grid=(nh, nq),
        in_specs=[pl.BlockSpec((TQ, HEAD), lambda h, i: (i, 4 * nh + h)),
                  pl.BlockSpec((s, HEAD), lambda h, i: (0, 5 * nh + h)),
                  pl.BlockSpec((s, HEAD), lambda h, i: (0, 6 * nh + h)),
                  blk, vec, vec, hext],
        out_specs=[blk, col, col, hvec, hvec, hext],
        out_shape=[shp] * 3 + [jax.ShapeDtypeStruct((nh, 1, HEAD), F32)] * 2
        + [jax.ShapeDtypeStruct((nh, 1, EXT), F32)],
        scratch=[pltpu.VMEM((s + PAD, HEAD), BF16), pltpu.VMEM((s + PAD, HEAD), BF16),
                 pltpu.VMEM((TQ, WIN), F32), pltpu.VMEM((TQ, EXT), F32),
                 pltpu.VMEM((s + PAD, HEAD), F32), pltpu.VMEM((s + PAD, HEAD), F32)],
        sem=("parallel", "arbitrary"), args=(proj, proj, proj, dyb, qg, kg, ext), comm=comm)


def _in_proj_bwd(dproj, wt, x, dxo, g, comm=None):
    s, d = x.shape
    tm, tk = min(s, 512), min(wt.shape[0], 1024)
    nk = wt.shape[0] // tk

    def body(dp_ref, w_ref, x_ref, dxo_ref, g_ref, dx_ref, dg_ref, acc):
        m, k = pl.program_id(0), pl.program_id(1)

        @pl.when(k == 0)
        def _():
            acc[...] = jnp.zeros_like(acc)

        @pl.when((k == 0) & (m == 0))
        def _():
            dg_ref[...] = jnp.zeros_like(dg_ref)

        acc[...] += _dot(dp_ref[...], w_ref[...])

        @pl.when(k == nk - 1)
        def _():
            xv = x_ref[...]
            r = lax.rsqrt(jnp.mean(xv * xv, axis=1, keepdims=True) + NORM_EPS)
            xhat = xv * r
            dh = acc[...]
            dxh = dh * g_ref[...]
            dx_ref[...] = dxo_ref[...] + r * (dxh - xhat * jnp.mean(dxh * xhat, axis=1, keepdims=True))
            dg_ref[...] += jnp.sum(dh * xhat, axis=0, keepdims=True)

    row = pl.BlockSpec((tm, d), lambda m, k: (m, 0))
    return _call(
        body, name="in_proj_bwd", grid=(s // tm, nk),
        in_specs=[pl.BlockSpec((tm, tk), lambda m, k: (m, k)),
                  pl.BlockSpec((tk, d), lambda m, k: (k, 0)),
                  row, row, pl.BlockSpec((1, d), lambda m, k: (0, 0))],
        out_specs=[row, pl.BlockSpec((8, d), lambda m, k: (0, 0))],
        out_shape=[jax.ShapeDtypeStruct((s, d), F32), jax.ShapeDtypeStruct((8, d), F32)],
        scratch=[pltpu.VMEM((tm, d), F32)], sem=("arbitrary", "arbitrary"), vmem_mb=56,
        args=(dproj, wt, x, dxo, g), comm=comm)


def _place():
    x, y, c = lax.axis_index("x"), lax.axis_index("y"), lax.axis_index("c")
    chips = [(1 - x, y), (x, 1 - y), (1 - x, 1 - y)]
    return x, y, c, chips


def _comm_call(body, name, ins, out_shape, n_remote, n_local, aliases=None):
    return pl.pallas_call(
        body, name=name, in_specs=[ANY] * len(ins), out_specs=[ANY] * len(out_shape), out_shape=out_shape,
        input_output_aliases=aliases or {},
        scratch_shapes=[pltpu.SemaphoreType.DMA((n_remote,)), pltpu.SemaphoreType.DMA((n_remote,)),
                        pltpu.SemaphoreType.DMA((n_local,))])(*ins)


def _rcopy(src, dst, send_sems, recv_sems, k, dev):
    return pltpu.make_async_remote_copy(src_ref=src, dst_ref=dst, send_sem=send_sems.at[k], recv_sem=recv_sems.at[k],
                                        device_id=dev, device_id_type=MESH)


def _run_comm(comm, name):
    n_ci, n_co = len(comm.ins), len(comm.outs)

    def body(*refs):
        cins, couts = refs[:n_ci], refs[n_ci:n_ci + n_co]
        send_sems, recv_sems = refs[n_ci + n_co:]
        comm.start(cins, couts, send_sems, recv_sems)
        comm.finish(cins, couts, send_sems, recv_sems)

    return pl.pallas_call(
        body, name=name, in_specs=[ANY] * n_ci, out_specs=[ANY] * n_co, out_shape=list(comm.outs),
        input_output_aliases=dict(comm.aliases),
        scratch_shapes=[pltpu.SemaphoreType.DMA((comm.n_sems,)), pltpu.SemaphoreType.DMA((comm.n_sems,))])(*comm.ins)


MOVES = ((None, 0, 2, 0), (None, 0, 2, 1), (0, 0, 1, 1), (1, 1, 2, 0))
ARRIVALS = ((0, 0, 2), (1, 0, 2), (2, 0, 1), (2, 1, 2))


def _gather_comm(fulls, rbp=None, stage="all", chips_at=(0, 1, 2, 3)):
    n = len(fulls)
    n_ici = len(MOVES) * n
    base = n_ici if stage == "all" else 0
    per_tensor = chips_at if isinstance(chips_at, list) else [chips_at] * n

    def region(full, blk, core, lo, hi):
        quarter = full.shape[1] // 4
        return full.at[blk].at[pl.ds(core * 2 * quarter + lo * quarter, (hi - lo) * quarter)]

    def block_of(chip):
        return 2 * chip[0] + chip[1]

    def ici(couts, send_sems, recv_sems, x, y, c, chips, at):
        out = []
        for t, full in enumerate(couts[:n]):
            for p, (whose, lo, hi, to) in enumerate(MOVES):
                if p in at[t]:
                    src = region(full, 2 * x + y if whose is None else block_of(chips[whose]), c, lo, hi)
                    out.append(_rcopy(src, src, send_sems, recv_sems, len(MOVES) * t + p, (*chips[to], c)))
        return out

    def landed(couts, send_sems, recv_sems, x, y, c, chips, core, first, at=None):
        out = []
        for t, full in enumerate(couts[:n]):
            for p, (whose, lo, hi) in enumerate(ARRIVALS):
                if at is None or p in at[t]:
                    got = region(full, block_of(chips[whose]), core, lo, hi)
                    out.append(_rcopy(got, got, send_sems, recv_sems, first + len(MOVES) * t + p, (x, y, 1 - c)))
        return out

    def small(cins, couts, send_sems, recv_sems, x, y, c, chips):
        b = 2 * x + y
        return ([_rcopy(cins[n], couts[n].at[b], send_sems, recv_sems, 2 * n_ici + j, (*chip, c))
                 for j, chip in enumerate(chips)],
                pltpu.make_async_copy(cins[n], couts[n].at[b], send_sems.at[2 * n_ici + 3]))

    first_hop, second_hop = [(0, 1)] * n, [(2, 3)] * n

    def start(cins, couts, send_sems, recv_sems):
        x, y, c, chips = _place()
        if stage == "sibling":
            for cp in landed(couts, send_sems, recv_sems, x, y, c, chips, c, base):
                cp.start()
            return
        for cp in ici(couts, send_sems, recv_sems, x, y, c, chips, first_hop if stage == "all" else per_tensor):
            cp.start()
        if rbp is not None:
            remote, local = small(cins, couts, send_sems, recv_sems, x, y, c, chips)
            for cp in remote:
                cp.start()
            local.start()

    def finish(cins, couts, send_sems, recv_sems):
        x, y, c, chips = _place()
        place = (couts, send_sems, recv_sems, x, y, c, chips)
        passed = landed(*place, c, base)
        if stage == "chips":
            for cp in landed(*place, c, 0, per_tensor):
                cp.wait_recv()
            for cp in ici(*place, per_tensor):
                cp.wait_send()
        if stage == "all":
            for cp in landed(*place, c, 0, first_hop):
                cp.wait_recv()
            for cp in ici(*place, second_hop):
                cp.start()
            for cp in landed(*place, c, 0, second_hop):
                cp.wait_recv()
            for cp in passed:
                cp.start()
            for cp in ici(*place, first_hop) + ici(*place, second_hop):
                cp.wait_send()
        if stage != "chips":
            for cp in landed(*place, 1 - c, base):
                cp.wait_recv()
            for cp in passed:
                cp.wait_send()
        if rbp is not None:
            remote, local = small(cins, couts, send_sems, recv_sems, x, y, c, chips)
            for j, chip in enumerate(chips):
                got = couts[n].at[2 * chip[0] + chip[1]]
                _rcopy(got, got, send_sems, recv_sems, 2 * n_ici + j, (x, y, c)).wait_recv()
            for cp in remote:
                cp.wait_send()
            local.wait()

    outs = [jax.ShapeDtypeStruct(f.shape, f.dtype) for f in fulls]
    ins = list(fulls)
    if rbp is not None:
        ins.append(rbp)
        outs.append(jax.ShapeDtypeStruct((4,) + rbp.shape, F32))
    return _Comm(tuple(ins), tuple(outs), {t: t for t in range(n)}, 2 * n_ici + 4, start, finish)


def _chips_comm(sums):
    n = len(sums)

    def copies(cins, couts, send_sems, recv_sems):
        x, y, c, chips = _place()
        return [_rcopy(cins[t].at[2 * chip[0] + chip[1]], couts[t].at[j], send_sems, recv_sems, 3 * t + j, (*chip, c))
                for t in range(n) for j, chip in enumerate(chips)]

    def start(*refs):
        for cp in copies(*refs):
            cp.start()

    def finish(*refs):
        for cp in copies(*refs):
            cp.wait()

    outs = tuple(jax.ShapeDtypeStruct((3,) + p.shape[1:], p.dtype) for p in sums)
    return _Comm(tuple(sums), outs, {}, 3 * n, start, finish)


def _pair_comm(ins, outs, aliases, copies):
    def start(*refs):
        for cp in copies(*refs):
            cp.start()

    def finish(*refs):
        for cp in copies(*refs):
            cp.wait()

    return _Comm(tuple(ins), tuple(outs), aliases, len(ins), start, finish)


def _sibling_comm(parts):
    def copies(cins, couts, send_sems, recv_sems):
        x, y, c, _ = _place()
        return [_rcopy(p.at[:, pl.ds((1 - c) * (p.shape[1] // 2), p.shape[1] // 2), :], couts[t],
                       send_sems, recv_sems, t, (x, y, 1 - c)) for t, p in enumerate(cins)]

    half = [jax.ShapeDtypeStruct((p.shape[0], p.shape[1] // 2, p.shape[2]), p.dtype) for p in parts]
    return _pair_comm(parts, half, {}, copies)


def _halves_comm(tots):
    def copies(cins, couts, send_sems, recv_sems):
        x, y, c, _ = _place()
        return [_rcopy(g.at[c], g.at[c], send_sems, recv_sems, t, (x, y, 1 - c)) for t, g in enumerate(couts)]

    return _pair_comm(tots, [jax.ShapeDtypeStruct(t.shape, t.dtype) for t in tots],
                      {t: t for t in range(len(tots))}, copies)


def _gather_small(packed):
    def body(p_ref, all_ref, send_sems, recv_sems, loc_sems):
        x, y, c, _ = _place()
        me = 4 * x + 2 * y + c
        local = pltpu.make_async_copy(p_ref, all_ref.at[me], loc_sems.at[0])
        local.start()
        sent = []
        for k in range(1, 8):
            px, py, pc = x ^ (k >> 2), y ^ ((k >> 1) & 1), c ^ (k & 1)
            cp = _rcopy(p_ref, all_ref.at[me], send_sems, recv_sems, k - 1, (px, py, pc))
            cp.start()
            sent.append(cp)
        for k in range(1, 8):
            px, py, pc = x ^ (k >> 2), y ^ ((k >> 1) & 1), c ^ (k & 1)
            got = all_ref.at[4 * px + 2 * py + pc]
            _rcopy(got, got, send_sems, recv_sems, k - 1, (x, y, c)).wait_recv()
        for cp in sent:
            cp.wait_send()
        local.wait()

    return _comm_call(body, "gather_small", [packed], [jax.ShapeDtypeStruct((8,) + packed.shape, F32)], 7, 1)[0]


def _sum_devices(allp):
    n, r, c = allp.shape

    def body(a_ref, o_ref):
        acc = a_ref[0]
        for k in range(1, n):
            acc = acc + a_ref[k]
        o_ref[...] = acc

    return pl.pallas_call(body, name="sum_devices", out_shape=jax.ShapeDtypeStruct((r, c), F32))(allp)


def _ext_index():
    u = np.arange(EXT)
    dist = np.where(u < WIN, PAD - u, PAD + EXT - u)
    return np.clip(dist, -(CHUNK - 1), REL_CLIP) + (CHUNK - 1)


def _pack(parts, rows):
    flat = jnp.concatenate([p.reshape(-1) for p in parts])
    return jnp.pad(flat, (0, rows * 128 - flat.shape[0])).reshape(rows, 128)


def _unpack(packed, shapes):
    flat, out, at = packed.reshape(-1), [], 0
    for shp in shapes:
        size = int(np.prod(shp))
        out.append(flat[at:at + size].reshape(shp))
        at += size
    return out


def kernel(x, norm_g, w_in, q_norm_g, k_norm_g, rel_bias, w_out, loss_target, m_norm_g, m_w_in, m_q_norm_g, m_k_norm_g, m_rel_bias, m_w_out, v_norm_g, v_w_in, v_q_norm_g, v_k_norm_g, v_rel_bias, v_w_out):
    nl, d, nb = w_in.shape
    s = x.shape[1]
    ds_ = d // 2
    nh = ds_ // HEAD
    rb = w_out.shape[1]
    nrel = rel_bias.shape[2]
    bx = lax.axis_index("x") * 2 + lax.axis_index("y")

    rb_rows = -(-(nl * nh * nrel) // 1024) * 8
    cx = lax.axis_index("c")
    wi_full = [_cast_block(w_in, l, bx, "cast_w_in") for l in range(nl)]
    wo_full = [_cast_block(w_out, l, bx, "cast_w_out") for l in range(nl)]
    wi_full[0], wo_full[0], rel_all = _run_comm(
        _gather_comm([wi_full[0], wo_full[0]], _pack([rel_bias], rb_rows)), "gather_first")
    rel_full = jnp.concatenate(
        [rel_all[j].reshape(-1)[:nl * nh * nrel].reshape(nl, nh, nrel) for j in range(4)], axis=2)
    ext_idx = _ext_index()
    onehot = jnp.asarray(ext_idx[:, None] == np.arange(N_REL)[None, :], F32)
    ext = jnp.einsum("lhr,ur->lhu", rel_full, onehot, precision=lax.Precision.HIGHEST).reshape(nl, nh, 1, EXT)

    xs, hs, projs, yas, lts, ybs, mixes, wi_t = [], [], [], [], [], [], [], []
    xc = x[0]
    h = _rmsnorm_fwd(xc, norm_g[0:1])
    for l in range(nl):
        nxt = l + 1 < nl
        (proj, wt), got = _in_proj(h, wi_full[l], _gather_comm(
            [wo_full[l + 1]], stage="chips", chips_at=(0, 1)) if nxt else None)
        wi_t.append(wt)
        if nxt:
            wo_full[l + 1] = got[0]
        (ya, lt), got = _attn_a_fwd(proj, nh, _gather_comm(
            [wi_full[l + 1]], stage="chips", chips_at=(0, 1)) if nxt else None)
        if nxt:
            wi_full[l + 1] = got[0]
        (yb,), got = _attn_b_fwd(proj, q_norm_g[l:l + 1], k_norm_g[l:l + 1], ext[l], nh, _gather_comm(
            [wi_full[l + 1], wo_full[l + 1]], stage="chips", chips_at=(2, 3)) if nxt else None)
        if nxt:
            wi_full[l + 1], wo_full[l + 1] = got
        xs.append(xc)
        hs.append(h); projs.append(proj); yas.append(ya); lts.append(lt); ybs.append(yb)
        wo_now = wo_full[l].reshape(4 * rb, d)
        (mix, after, then), got = _out_proj(
            xc, ya, yb, proj, wo_now, norm_g[l + 1:l + 2] if nxt else loss_target[0],
            _gather_comm([wi_full[l + 1], wo_full[l + 1]], stage="sibling") if nxt else None)
        mixes.append(mix)
        if nxt:
            wi_full[l + 1], wo_full[l + 1] = got
            xc, h = after, then
        else:
            dx, loss_tile = after, then

    small, g_wi, g_wo = [None] * nl, [None] * nl, [None] * nl
    pending = None

    def keep(lay, shared):
        g_wi[lay], g_wo[lay] = shared[0].reshape(d, nb), shared[1].reshape(rb, d)

    for l in reversed(range(nl)):
        wo = wo_full[l].reshape(4 * rb, d)
        p_wo = _wgrad(mixes[l], dx, 4, False, "wgrad_out")
        last = l == 0 and pending is not None
        (dya, dyb, dga, dgb), theirs_wo = _out_proj_bwd(dx, yas[l], ybs[l], projs[l], wo,
                                                        _sibling_comm([p_wo]) if last else None)
        travelling = list(pending[1]) if pending else []
        if last:
            sums_wo = _add_sibling(p_wo, theirs_wo[0], cx)
            travelling.append(sums_wo)
        (dqa, dka, dva), got = _attn_a_bwd(projs[l], lts[l], dya, nh, _chips_comm(travelling) if pending else None)
        tots = [_add_chips(pending[1][t], got[t], bx, cx) for t in range(2)] if pending else None
        (dqb, dkb, dvb, dqg, dkg, dext), shared = _attn_b_bwd(
            projs[l], dyb, q_norm_g[l:l + 1], k_norm_g[l:l + 1], ext[l], nh, _halves_comm(tots) if pending else None)
        if pending:
            keep(pending[0], shared)
        dproj = jnp.concatenate([dqa, dka, dva, dga, dqb, dkb, dvb, dgb], axis=1)
        parts = [_wgrad(hs[l], dproj, 4, True, "wgrad_in"), p_wo]
        if l > 0:
            (dx, dng), theirs = _in_proj_bwd(dproj, wi_t[l],xs[l], dx, norm_g[l:l + 1], _sibling_comm(parts))
            pending = (l, [_add_sibling(parts[t], theirs[t], cx) for t in range(2)])
        elif last:
            theirs = _run_comm(_sibling_comm(parts[:1]), "reduce_sibling")
            sums_wi = _add_sibling(parts[0], theirs[0], cx)
            (dx, dng), got_wi = _in_proj_bwd(dproj, wi_t[l],xs[l], dx, norm_g[l:l + 1], _chips_comm([sums_wi]))
            keep(0, _run_comm(_halves_comm([_add_chips(sums_wi, got_wi[0], bx, cx),
                                            _add_chips(sums_wo, got[2], bx, cx)]), "share_halves"))
        else:
            theirs = _run_comm(_sibling_comm(parts), "reduce_sibling")
            sums = [_add_sibling(parts[t], theirs[t], cx) for t in range(2)]
            (dx, dng), got = _in_proj_bwd(dproj, wi_t[l],xs[l], dx, norm_g[l:l + 1], _chips_comm(sums))
            keep(0, _run_comm(_halves_comm([_add_chips(sums[t], got[t], bx, cx) for t in range(2)]), "share_halves"))
        small[l] = (dng[0], jnp.sum(dqg, axis=0).reshape(-1), jnp.sum(dkg, axis=0).reshape(-1), dext.reshape(nh, EXT))
    grad_x = dx[None]

    small_shapes = [(nl, d), (nl, HEAD), (nl, HEAD), (nl, nh, EXT), (1,)]
    small_parts = [jnp.stack([sm[i] for sm in small]) for i in range(4)] + [loss_tile[0, :1]]
    rows = -(-sum(int(np.prod(sh)) for sh in small_shapes) // 1024) * 8
    tot = _sum_devices(_gather_small(_pack(small_parts, rows)))
    g_ng, g_qg, g_kg, g_ext, loss = _unpack(tot, small_shapes)
    g_rel_full = jnp.einsum("lhu,ur->lhr", g_ext, onehot, precision=lax.Precision.HIGHEST)
    g_rel = lax.dynamic_slice_in_dim(g_rel_full, bx * nrel, nrel, axis=2)

    res_wi, res_wo = (), ()
    for l in range(nl):
        res_wi = _adamw_layer(l, w_in, g_wi[l], m_w_in, v_w_in, res_wi, "adamw_w_in")
        res_wo = _adamw_layer(l, w_out, g_wo[l], m_w_out, v_w_out, res_wo, "adamw_w_out")
    g_wi, d_wi, nm_wi, nv_wi = res_wi
    g_wo, d_wo, nm_wo, nv_wo = res_wo
    sm_shapes = [(nl, d), (nl, HEAD), (nl, HEAD), (nl, nh, nrel)]
    sm_rows = -(-sum(int(np.prod(sh)) for sh in sm_shapes) // 1024) * 8
    pw, pg, pm, pv = [_pack(group, sm_rows) for group in (
        (norm_g, q_norm_g, k_norm_g, rel_bias), (g_ng, g_qg, g_kg, g_rel),
        (m_norm_g, m_q_norm_g, m_k_norm_g, m_rel_bias), (v_norm_g, v_q_norm_g, v_k_norm_g, v_rel_bias))]
    d_sm, nm_sm, nv_sm = [_unpack(a[0], sm_shapes)
                          for a in _adamw_layer(0, pw[None], pg, pm[None], pv[None], (), "adamw_small")[1:]]

    return (loss[0], grad_x, g_ng, g_wi, g_qg, g_kg, g_rel, g_wo,
            d_sm[0], d_wi, d_sm[1], d_sm[2], d_sm[3], d_wo,
            nm_sm[0], nm_wi, nm_sm[1], nm_sm[2], nm_sm[3], nm_wo,
            nv_sm[0], nv_wi, nv_sm[1], nv_sm[2], nv_sm[3], nv_wo)
```

```python
from typing import Callable, NamedTuple

import jax
import jax.numpy as jnp
import numpy as np
from jax import lax
from jax.experimental import pallas as pl
from jax.experimental.pallas import tpu as pltpu

F32 = jnp.float32
BF16 = jnp.bfloat16

HEAD = 128
CHUNK = 64
LEFT_CHUNKS = 8
REL_CLIP = 256
N_REL = REL_CLIP + CHUNK
NORM_EPS = 1e-6
NEG_BIG = -1e30
TQ = 256
TK = TQ
QB = 512
ROWS = 32
PAD = LEFT_CHUNKS * CHUNK
WIN = PAD + TQ
EXT = 1024
SCALE = HEAD ** -0.5

ADAM_LR = 0.001
ADAM_B1 = 0.9
ADAM_B2 = 0.999
ADAM_EPS = 1e-08
ADAM_WD = 0.01
ADAM_STEP = 10

ANY = pl.BlockSpec(memory_space=pl.ANY)
MESH = pl.DeviceIdType.MESH


def _params(sem=None, vmem_mb=None):
    kw = {}
    if sem is not None:
        kw["dimension_semantics"] = sem
    if vmem_mb is not None:
        kw["vmem_limit_bytes"] = vmem_mb << 20
    return pltpu.CompilerParams(**kw)


class _Comm(NamedTuple):
    ins: tuple
    outs: tuple
    aliases: dict
    n_sems: int
    start: Callable
    finish: Callable


def _call(body, *, name, grid, in_specs, out_specs, out_shape, args, scratch=(), sem=None, vmem_mb=None, comm=None):
    if comm is None:
        out = pl.pallas_call(body, name=name, grid=grid, in_specs=in_specs, out_specs=out_specs, out_shape=out_shape,
                             scratch_shapes=list(scratch), compiler_params=_params(sem, vmem_mb))(*args)
        return out, ()
    n_in, n_out, n_ci, n_co = len(in_specs), len(out_shape), len(comm.ins), len(comm.outs)

    def hosted(*refs):
        ins, cins = refs[:n_in], refs[n_in:n_in + n_ci]
        outs, couts = refs[n_in + n_ci:n_in + n_ci + n_out], refs[n_in + n_ci + n_out:n_in + n_ci + n_out + n_co]
        rest = refs[n_in + n_ci + n_out + n_co:]
        send_sems, recv_sems = rest[-2:]
        first, last = None, None
        for ax, size in enumerate(grid):
            at = pl.program_id(ax)
            first = (at == 0) if first is None else first & (at == 0)
            last = (at == size - 1) if last is None else last & (at == size - 1)

        @pl.when(first)
        def _():
            comm.start(cins, couts, send_sems, recv_sems)

        body(*ins, *outs, *rest[:-2])

        @pl.when(last)
        def _():
            comm.finish(cins, couts, send_sems, recv_sems)

    out = pl.pallas_call(
        hosted, name=name, grid=grid, in_specs=list(in_specs) + [ANY] * n_ci, out_specs=list(out_specs) + [ANY] * n_co,
        out_shape=list(out_shape) + list(comm.outs),
        input_output_aliases={n_in + k: n_out + v for k, v in comm.aliases.items()},
        scratch_shapes=list(scratch) + [pltpu.SemaphoreType.DMA((comm.n_sems,)), pltpu.SemaphoreType.DMA((comm.n_sems,))],
        compiler_params=_params(("arbitrary",) * len(grid), vmem_mb))(*args, *comm.ins)
    return out[:n_out], out[n_out:]


def _dot(a, b):
    return jnp.dot(a, b, preferred_element_type=F32)


def _dot_nt(a, b):
    return lax.dot_general(a, b, (((1,), (1,)), ((), ())), preferred_element_type=F32)


def _dot_tn(a, b):
    return lax.dot_general(a, b, (((0,), (0,)), ((), ())), preferred_element_type=F32)


def _split_dot(x, m):
    hi = x.astype(BF16)
    lo = (x - hi.astype(F32)).astype(BF16)
    return _dot(hi, m) + _dot(lo, m)


def _silu_parts(g):
    sg = 1.0 / (1.0 + jnp.exp(-g))
    return g * sg, sg * (1.0 + g * (1.0 - sg))


def _idx(*vals):
    return jnp.stack([jnp.asarray(v, jnp.int32) for v in vals])


def _cast_block(w, l, blk, name):
    _, r, c = w.shape
    tr = min(r, 512)

    def body(b_ref, w_ref, o_ref):
        o_ref[...] = w_ref[...].astype(BF16)

    spec = pltpu.PrefetchScalarGridSpec(
        num_scalar_prefetch=1, grid=(r // tr,),
        in_specs=[pl.BlockSpec((None, tr, c), lambda i, b: (l, i, 0))],
        out_specs=pl.BlockSpec((None, tr, c), lambda i, b: (b[0], i, 0)))
    return pl.pallas_call(body, name=name, grid_spec=spec, out_shape=jax.ShapeDtypeStruct((4, r, c), BF16),
                          compiler_params=_params(("parallel",)))(_idx(blk), w)


def _add_sibling(p, theirs, core):
    nblk, r, c = p.shape
    hr = r // 2
    tr = min(hr, 256)
    per = hr // tr

    def body(c_ref, p_ref, t_ref, o_ref):
        o_ref[...] = (p_ref[...].astype(F32) + t_ref[...].astype(F32)).astype(BF16)

    blk = pl.BlockSpec((None, tr, c), lambda j, i, cr: (j, i, 0))
    spec = pltpu.PrefetchScalarGridSpec(
        num_scalar_prefetch=1, grid=(nblk, per),
        in_specs=[pl.BlockSpec((None, tr, c), lambda j, i, cr: (j, cr[0] * per + i, 0)), blk], out_specs=blk)
    return pl.pallas_call(body, name="add_sibling", grid_spec=spec, out_shape=jax.ShapeDtypeStruct((nblk, hr, c), BF16),
                          compiler_params=_params(("parallel", "parallel")))(_idx(core), p, theirs)


def _add_chips(sums, got, blk, core):
    _, hr, c = sums.shape
    tr = min(hr, 256)

    def body(i_ref, s_ref, g0_ref, g1_ref, g2_ref, o_ref):
        o_ref[...] = ((s_ref[...].astype(F32) + g0_ref[...].astype(F32))
                      + g1_ref[...].astype(F32)) + g2_ref[...].astype(F32)

    at = lambda j: pl.BlockSpec((None, tr, c), lambda i, ir: (j, i, 0))
    spec = pltpu.PrefetchScalarGridSpec(
        num_scalar_prefetch=1, grid=(hr // tr,),
        in_specs=[pl.BlockSpec((None, tr, c), lambda i, ir: (ir[0], i, 0)), at(0), at(1), at(2)],
        out_specs=pl.BlockSpec((None, tr, c), lambda i, ir: (ir[1], i, 0)))
    return pl.pallas_call(body, name="add_chips", grid_spec=spec, out_shape=jax.ShapeDtypeStruct((2, hr, c), F32),
                          compiler_params=_params(("parallel",)))(_idx(blk, core), sums, got, got, got)


def _adamw_layer(l, w, g, m, v, prev, name):
    nl, r, c = w.shape
    tr = min(r, 256)
    c1 = 1.0 / (1.0 - ADAM_B1 ** ADAM_STEP)
    c2 = 1.0 / (1.0 - ADAM_B2 ** ADAM_STEP)

    def body(w_ref, g_ref, m_ref, v_ref, *rest):
        go_ref, d_ref, nm_ref, nv_ref = rest[-4:]
        gg = g_ref[...]
        nm = ADAM_B1 * m_ref[...] + (1.0 - ADAM_B1) * gg
        nv = ADAM_B2 * v_ref[...] + (1.0 - ADAM_B2) * (gg * gg)
        upd = (nm * c1) / (jnp.sqrt(nv * c2) + ADAM_EPS) + ADAM_WD * w_ref[...]
        go_ref[...] = gg
        d_ref[...] = -ADAM_LR * upd
        nm_ref[...] = nm
        nv_ref[...] = nv

    lay = pl.BlockSpec((None, tr, c), lambda i: (l, i, 0))
    shp = jax.ShapeDtypeStruct((nl, r, c), F32)
    return pl.pallas_call(
        body, name=name, grid=(r // tr,),
        in_specs=[lay, pl.BlockSpec((tr, c), lambda i: (i, 0)), lay, lay] + [ANY] * len(prev),
        out_specs=[lay] * 4, out_shape=[shp] * 4, input_output_aliases={4 + k: k for k in range(len(prev))},
        compiler_params=_params(("parallel",), 40))(w, g, m, v, *prev)


def _rmsnorm_fwd(x, g):
    s, d = x.shape
    tm = min(s, 256)

    def body(x_ref, g_ref, h_ref):
        xv = x_ref[...]
        r = lax.rsqrt(jnp.mean(xv * xv, axis=1, keepdims=True) + NORM_EPS)
        h_ref[...] = (xv * r * g_ref[...]).astype(BF16)

    return pl.pallas_call(
        body, name="rmsnorm_fwd", grid=(s // tm,),
        in_specs=[pl.BlockSpec((tm, d), lambda i: (i, 0)), pl.BlockSpec((1, d), lambda i: (0, 0))],
        out_specs=pl.BlockSpec((tm, d), lambda i: (i, 0)),
        out_shape=jax.ShapeDtypeStruct((s, d), BF16),
        compiler_params=_params(("parallel",)))(x, g)


def _in_proj(h, w, comm=None):
    s, d = h.shape
    nblk, _, nb = w.shape
    tm, tn = min(s, 1024), min(nb, 1024)
    per = nb // tn

    def body(h_ref, w_ref, o_ref, wt_ref):
        o_ref[...] = _dot(h_ref[...], w_ref[...])

        @pl.when(pl.program_id(1) == 0)
        def _():
            for c in range(tn // HEAD):
                wt_ref[pl.ds(c * HEAD, HEAD), :] = w_ref[:, c * HEAD:(c + 1) * HEAD].astype(F32).T.astype(BF16)

    return _call(
        body, name="in_proj", grid=(nblk * per, s // tm),
        in_specs=[pl.BlockSpec((tm, d), lambda n, m: (m, 0)),
                  pl.BlockSpec((None, d, tn), lambda n, m: (n // per, 0, n % per))],
        out_specs=[pl.BlockSpec((tm, tn), lambda n, m: (m, n)), pl.BlockSpec((tn, d), lambda n, m: (n, 0))],
        out_shape=[jax.ShapeDtypeStruct((s, nblk * nb), F32), jax.ShapeDtypeStruct((nblk * nb, d), BF16)],
        sem=("parallel", "arbitrary"), vmem_mb=56, args=(h, w), comm=comm)


def _heads_per_step(nh):
    return 2 if nh % 2 == 0 else 1


def _head(hh):
    return slice(hh * HEAD, (hh + 1) * HEAD)


def _tri(op):
    r = lax.broadcasted_iota(jnp.int32, (TQ, TQ), 0)
    c = lax.broadcasted_iota(jnp.int32, (TQ, TQ), 1)
    return op(r, c)


def _staggered(groups, hp):
    for hh in range(hp):
        for fn in groups[0]:
            fn(hh)
    for group in groups[1:]:
        for hh in range(hp):
            for fn in group:
                fn(hh)


def _transpose_tiles(src_ref, dst_ref, hp, nt):
    for hh in range(hp):
        for t in range(nt):
            dst_ref[hh, t] = src_ref[pl.ds(t * TK, TK), _head(hh)].T.astype(BF16)


def _chunk_mask(r, a):
    if a is None or r * ROWS >= (a + 1) * TK:
        return None
    if (r + 1) * ROWS <= a * TK:
        return False
    row = lax.broadcasted_iota(jnp.int32, (ROWS, TK), 0) + r * ROWS
    return row > lax.broadcasted_iota(jnp.int32, (ROWS, TK), 1) + a * TK


def _sb_logs(qk, causal):
    z = qk * SCALE
    l1p = jnp.log(1.0 + jnp.exp(-jnp.abs(z)))
    ls = jnp.minimum(-z, 0.0) - l1p
    if causal is not None:
        ls = jnp.where(causal, ls, 0.0)
    return ls, jnp.minimum(z, 0.0) - l1p


def _attn_a_fwd(proj, nh, comm=None):
    s = proj.shape[0]
    qb_rows = min(QB, s)
    nq, nt, per = s // qb_rows, s // TK, qb_rows // TK
    hp = _heads_per_step(nh)
    ng = nh // hp
    chunks = qb_rows // ROWS

    def body(q_ref, k_ref, v_ref, o_ref, lt_ref, acc_ref, qb_ref, m_ref, car_ref, sum_ref,
             z_ref, lsig_ref, aft_ref, hi_ref, lo_ref, w_ref, kt_ref):
        i = pl.program_id(1)

        @pl.when(i == 0)
        def _():
            _transpose_tiles(k_ref, kt_ref, hp, nt)

        qb_ref[...] = q_ref[...].astype(BF16)
        m_ref[...] = _tri(lambda r, c: r > c).astype(BF16)
        acc_ref[...] = jnp.zeros_like(acc_ref)
        car_ref[...] = jnp.zeros_like(car_ref)

        def tile(j, a):
            off = pl.multiple_of(j * TK, TK)

            def scores(hh):
                z_ref[hh] = _dot(qb_ref[:, _head(hh)], kt_ref[hh, j])

            def logs(hh):
                for r in range(chunks):
                    sl, mask = pl.ds(r * ROWS, ROWS), _chunk_mask(r, a)
                    if mask is False:
                        hi_ref[hh, sl, :] = jnp.zeros((ROWS, TK), BF16)
                        lo_ref[hh, sl, :] = jnp.zeros((ROWS, TK), BF16)
                        continue
                    ls, lsig = _sb_logs(z_ref[hh, sl, :], mask)
                    lsig_ref[hh, sl, :] = lsig
                    hi = ls.astype(BF16)
                    hi_ref[hh, sl, :] = hi
                    lo_ref[hh, sl, :] = (ls - hi.astype(F32)).astype(BF16)
                    sum_ref[hh, sl, :] = jnp.sum(ls, axis=1, keepdims=True)

            def after(hh):
                aft_ref[hh] = _dot(hi_ref[hh], m_ref[...]) + _dot(lo_ref[hh], m_ref[...])

            def weights(hh):
                for r in range(chunks):
                    sl, mask = pl.ds(r * ROWS, ROWS), _chunk_mask(r, a)
                    if mask is False:
                        w_ref[hh, sl, :] = jnp.zeros((ROWS, TK), BF16)
                        continue
                    w = jnp.exp(lsig_ref[hh, sl, :] + aft_ref[hh, sl, :] + car_ref[hh, sl, :])
                    if mask is not None:
                        w = jnp.where(mask, w, 0.0)
                    w_ref[hh, sl, :] = w.astype(BF16)
                    car_ref[hh, sl, :] += sum_ref[hh, sl, :]

            def values(hh):
                vb = v_ref[pl.ds(off, TK), _head(hh)].astype(BF16)
                acc_ref[:, _head(hh)] += _dot(w_ref[hh], vb)

            _staggered([(scores,), (logs, after), (weights, values)], hp)

        for a in reversed(range(per)):
            tile(i * per + a, a)

        def step(t, carry):
            tile(i * per - 1 - t, None)
            return carry

        lax.fori_loop(0, i * per, step, 0)
        o_ref[...] = acc_ref[...]
        lt_ref[...] = car_ref[...]

    wd = hp * HEAD
    sq = lambda dt: pltpu.VMEM((hp, qb_rows, TK), dt)
    return _call(
        body, name="attn_a_fwd", grid=(ng, nq),
        in_specs=[pl.BlockSpec((qb_rows, wd), lambda h, i: (i, h)),
                  pl.BlockSpec((s, wd), lambda h, i: (0, ng + h)),
                  pl.BlockSpec((s, wd), lambda h, i: (0, 2 * ng + h))],
        out_specs=[pl.BlockSpec((qb_rows, wd), lambda h, i: (i, h)),
                   pl.BlockSpec((hp, qb_rows, 1), lambda h, i: (h, i, 0))],
        out_shape=[jax.ShapeDtypeStruct((s, nh * HEAD), F32), jax.ShapeDtypeStruct((nh, s, 1), F32)],
        scratch=[pltpu.VMEM((qb_rows, wd), F32), pltpu.VMEM((qb_rows, wd), BF16), pltpu.VMEM((TK, TK), BF16),
                 pltpu.VMEM((hp, qb_rows, 1), F32), pltpu.VMEM((hp, qb_rows, 1), F32),
                 sq(F32), sq(F32), sq(F32), sq(BF16), sq(BF16), sq(BF16),
                 pltpu.VMEM((hp, nt, HEAD, TK), BF16)],
        sem=("parallel", "arbitrary"), args=(proj, proj, proj), comm=comm)


def _band_valid(i):
    cl = lax.broadcasted_iota(jnp.int32, (TQ, WIN), 0) // CHUNK
    kl = lax.broadcasted_iota(jnp.int32, (TQ, WIN), 1) // CHUNK
    first = LEFT_CHUNKS - (TQ // CHUNK) * i
    return (kl >= cl) & (kl <= cl + LEFT_CHUNKS) & (kl >= first)


def _build_bias(e_ref, bias_ref):
    e8 = jnp.broadcast_to(e_ref[...], (8, EXT))
    row = lax.broadcasted_iota(jnp.int32, (8, EXT), 0)
    t8 = jnp.zeros((8, EXT), F32)
    for b in range(8):
        t8 = jnp.where(row == b, pltpu.roll(e8, b, 1) if b else e8, t8)
    for a in range(TQ // 8):
        sl = pltpu.roll(t8, 8 * a, 1) if a else t8
        bias_ref[pl.ds(8 * a, 8), :] = sl[:, :WIN]


def _reduce_bias_grad(db_ref):
    acc = jnp.zeros((8, EXT), F32)
    for a in range(TQ // 8):
        sl = db_ref[pl.ds(8 * a, 8), :]
        acc = acc + (pltpu.roll(sl, EXT - 8 * a, 1) if a else sl)
    row = lax.broadcasted_iota(jnp.int32, (8, EXT), 0)
    tot = jnp.zeros((8, EXT), F32)
    for b in range(8):
        tot = tot + jnp.where(row == b, pltpu.roll(acc, EXT - b, 1) if b else acc, 0.0)
    return jnp.sum(tot, axis=0, keepdims=True)


def _band_fill(k_ref, v_ref, kg_ref, kn_pad, v_pad, s):
    k = k_ref[...]
    rk = lax.rsqrt(jnp.mean(k * k, axis=1, keepdims=True) + NORM_EPS)
    kn_pad[pl.ds(0, PAD), :] = jnp.zeros((PAD, HEAD), BF16)
    kn_pad[pl.ds(PAD, s), :] = (k * rk * kg_ref[...]).astype(BF16)
    v_pad[pl.ds(0, PAD), :] = jnp.zeros((PAD, HEAD), BF16)
    v_pad[pl.ds(PAD, s), :] = v_ref[...].astype(BF16)


def _band_probs(q_ref, qg_ref, kn_pad, bias_ref, i):
    q = q_ref[...]
    rq = lax.rsqrt(jnp.mean(q * q, axis=1, keepdims=True) + NORM_EPS)
    qhat = q * rq
    qn = (qhat * qg_ref[...]).astype(BF16)
    off = pl.multiple_of(i * TQ, TQ)
    kw = kn_pad[pl.ds(off, WIN), :]
    sc = _dot_nt(qn, kw) * SCALE + bias_ref[...]
    sc = jnp.where(_band_valid(i), sc, NEG_BIG)
    p = jnp.exp(sc - jnp.max(sc, axis=1, keepdims=True))
    pn = p / jnp.sum(p, axis=1, keepdims=True)
    return rq, qhat, qn, kw, off, pn


def _attn_b_fwd(proj, qg, kg, ext, nh, comm=None):
    s = proj.shape[0]
    nq = s // TQ

    def body(q_ref, k_ref, v_ref, qg_ref, kg_ref, e_ref, o_ref, kn_pad, v_pad, bias_ref):
        i = pl.program_id(1)

        @pl.when(i == 0)
        def _():
            _band_fill(k_ref, v_ref, kg_ref, kn_pad, v_pad, s)
            _build_bias(e_ref, bias_ref)

        _, _, _, _, off, pn = _band_probs(q_ref, qg_ref, kn_pad, bias_ref, i)
        o_ref[...] = _dot(pn.astype(BF16), v_pad[pl.ds(off, WIN), :])

    vec = pl.BlockSpec((1, HEAD), lambda h, i: (0, 0))
    return _call(
        body, name="attn_b_fwd", grid=(nh, nq),
        in_specs=[pl.BlockSpec((TQ, HEAD), lambda h, i: (i, 4 * nh + h)),
                  pl.BlockSpec((s, HEAD), lambda h, i: (0, 5 * nh + h)),
                  pl.BlockSpec((s, HEAD), lambda h, i: (0, 6 * nh + h)),
                  vec, vec,
                  pl.BlockSpec((None, 1, EXT), lambda h, i: (h, 0, 0))],
        out_specs=[pl.BlockSpec((TQ, HEAD), lambda h, i: (i, h))],
        out_shape=[jax.ShapeDtypeStruct((s, nh * HEAD), F32)],
        scratch=[pltpu.VMEM((s + PAD, HEAD), BF16), pltpu.VMEM((s + PAD, HEAD), BF16), pltpu.VMEM((TQ, WIN), F32)],
        sem=("parallel", "arbitrary"), args=(proj, proj, proj, qg, kg, ext), comm=comm)


def _out_proj(x, ya, yb, proj, w, then, comm=None):
    s, d = x.shape
    ds_ = ya.shape[1]
    tm = min(s, 256)
    final = then.shape[0] == s

    def body(x_ref, ya_ref, yb_ref, ga_ref, gb_ref, w_ref, then_ref, mix_ref, a_ref, b_ref):
        ma = (ya_ref[...] * _silu_parts(ga_ref[...])[0]).astype(BF16)
        mb = (yb_ref[...] * _silu_parts(gb_ref[...])[0]).astype(BF16)
        mix_ref[:, :ds_] = ma
        mix_ref[:, ds_:] = mb
        y = x_ref[...] + _dot(ma, w_ref[pl.ds(0, ds_), :]) + _dot(mb, w_ref[pl.ds(ds_, ds_), :])
        if final:
            @pl.when(pl.program_id(0) == 0)
            def _():
                b_ref[...] = jnp.zeros_like(b_ref)

            err = y - then_ref[...]
            a_ref[...] = err * (1.0 / d)
            b_ref[...] += 0.5 * jnp.sum(jnp.mean(err * err, axis=1, keepdims=True), axis=0, keepdims=True)
        else:
            a_ref[...] = y
            r = lax.rsqrt(jnp.mean(y * y, axis=1, keepdims=True) + NORM_EPS)
            b_ref[...] = (y * r * then_ref[...]).astype(BF16)

    row = lambda width: pl.BlockSpec((tm, width), lambda i: (i, 0))
    last_spec = pl.BlockSpec((8, 128), lambda i: (0, 0)) if final else row(d)
    last_shape = jax.ShapeDtypeStruct((8, 128), F32) if final else jax.ShapeDtypeStruct((s, d), BF16)
    return _call(
        body, name="out_proj", grid=(s // tm,),
        in_specs=[row(d), row(ds_), row(ds_),
                  pl.BlockSpec((tm, ds_), lambda i: (i, 3)), pl.BlockSpec((tm, ds_), lambda i: (i, 7)),
                  pl.BlockSpec((2 * ds_, d), lambda i: (0, 0)),
                  row(d) if final else pl.BlockSpec((1, d), lambda i: (0, 0))],
        out_specs=[row(2 * ds_), row(d), last_spec],
        out_shape=[jax.ShapeDtypeStruct((s, 2 * ds_), BF16), jax.ShapeDtypeStruct((s, d), F32), last_shape],
        sem=("arbitrary",) if final else ("parallel",), vmem_mb=52, args=(x, ya, yb, proj, proj, w, then), comm=comm)


def _out_proj_bwd(dxo, ya, yb, proj, w, comm=None):
    s, d = dxo.shape
    ds_ = ya.shape[1]
    tm = min(s, 256)

    def body(dx_ref, ya_ref, yb_ref, ga_ref, gb_ref, w_ref, dya_ref, dyb_ref, dga_ref, dgb_ref):
        dxb = dx_ref[...].astype(BF16)
        for y_ref, g_ref, lo, dy_ref, dg_ref in ((ya_ref, ga_ref, 0, dya_ref, dga_ref),
                                                 (yb_ref, gb_ref, ds_, dyb_ref, dgb_ref)):
            dmix = _dot_nt(dxb, w_ref[pl.ds(lo, ds_), :])
            act, dact = _silu_parts(g_ref[...])
            dy_ref[...] = dmix * act
            dg_ref[...] = (dmix * y_ref[...] * dact).astype(BF16)

    row = lambda width: pl.BlockSpec((tm, width), lambda i: (i, 0))
    return _call(
        body, name="out_proj_bwd", grid=(s // tm,),
        in_specs=[row(d), row(ds_), row(ds_),
                  pl.BlockSpec((tm, ds_), lambda i: (i, 3)), pl.BlockSpec((tm, ds_), lambda i: (i, 7)),
                  pl.BlockSpec((2 * ds_, d), lambda i: (0, 0))],
        out_specs=[row(ds_)] * 4,
        out_shape=[jax.ShapeDtypeStruct((s, ds_), F32)] * 2 + [jax.ShapeDtypeStruct((s, ds_), BF16)] * 2,
        sem=("parallel",), vmem_mb=48, args=(dxo, ya, yb, proj, proj, w), comm=comm)


def _wgrad(a, b, nblk, col_blocks, name):
    s, m = a.shape
    n = b.shape[1]
    if col_blocks:
        tr = min(m, 1024)
        nb = n // nblk
        tn = min(nb, 2048)
        per = nb // tn
        out_shape = (nblk, m, nb)
        out_spec = pl.BlockSpec((None, tr, tn), lambda j, r: (j // per, r, j % per))
    else:
        tn = min(n, 1024)
        tr = m // nblk
        out_shape = (nblk, tr, n)
        out_spec = pl.BlockSpec((None, tr, tn), lambda j, r: (r, 0, j))

    def body(a_ref, b_ref, o_ref):
        o_ref[...] = _dot_tn(a_ref[...].astype(BF16), b_ref[...].astype(BF16)).astype(BF16)

    return pl.pallas_call(
        body, name=name, grid=(n // tn, m // tr),
        in_specs=[pl.BlockSpec((s, tr), lambda j, r: (0, r)), pl.BlockSpec((s, tn), lambda j, r: (0, j))],
        out_specs=out_spec, out_shape=jax.ShapeDtypeStruct(out_shape, BF16),
        compiler_params=_params(("parallel", "parallel"), 48))(a, b)


def _attn_a_bwd(proj, lt, dya, nh, comm=None):
    s = proj.shape[0]
    qb_rows = min(TK, s)
    nq, nt, per = s // qb_rows, s // TK, qb_rows // TK
    hp = _heads_per_step(nh)
    ng = nh // hp
    chunks = qb_rows // ROWS

    def body(q_ref, k_ref, v_ref, lt_ref, do_ref, dq_ref, dk_ref, dv_ref, dq_acc, dk_acc, dv_acc,
             qb_ref, dob_ref, upto_ref, before_ref, cls_ref, cg_ref, sls_ref, sg_ref,
             z_ref, dw_ref, lsig_ref, pre_ref, g_ref, hi_ref, lo_ref, wb_ref, kt_ref, vt_ref, qt_ref, dot_ref):
        i = pl.program_id(1)

        @pl.when(i == 0)
        def _():
            dk_acc[...] = jnp.zeros_like(dk_acc)
            dv_acc[...] = jnp.zeros_like(dv_acc)
            _transpose_tiles(k_ref, kt_ref, hp, nt)
            _transpose_tiles(v_ref, vt_ref, hp, nt)

        dq_acc[...] = jnp.zeros_like(dq_acc)
        qb_ref[...] = q_ref[...].astype(BF16)
        dob_ref[...] = do_ref[...].astype(BF16)
        for hh in range(hp):
            qt_ref[hh] = q_ref[:, _head(hh)].T.astype(BF16)
            dot_ref[hh] = do_ref[:, _head(hh)].T.astype(BF16)
        upto_ref[...] = _tri(lambda r, c: r <= c).astype(BF16)
        before_ref[...] = _tri(lambda r, c: r < c).astype(BF16)
        cls_ref[...] = jnp.zeros_like(cls_ref)
        cg_ref[...] = jnp.zeros_like(cg_ref)

        def tile(j, a):
            off = pl.multiple_of(j * TK, TK)
            zeros = jnp.zeros((ROWS, TK), BF16)

            def scores(hh):
                z_ref[hh] = _dot(qb_ref[:, _head(hh)], kt_ref[hh, j])
                dw_ref[hh] = _dot(dob_ref[:, _head(hh)], vt_ref[hh, j])

            def logs(hh):
                for r in range(chunks):
                    sl, mask = pl.ds(r * ROWS, ROWS), _chunk_mask(r, a)
                    if mask is False:
                        hi_ref[hh, sl, :] = zeros
                        lo_ref[hh, sl, :] = zeros
                        continue
                    ls, lsig = _sb_logs(z_ref[hh, sl, :], mask)
                    lsig_ref[hh, sl, :] = lsig
                    hi = ls.astype(BF16)
                    hi_ref[hh, sl, :] = hi
                    lo_ref[hh, sl, :] = (ls - hi.astype(F32)).astype(BF16)
                    sls_ref[hh, sl, :] = jnp.sum(ls, axis=1, keepdims=True)

            def upto(hh):
                pre_ref[hh] = _dot(hi_ref[hh], upto_ref[...]) + _dot(lo_ref[hh], upto_ref[...])

            def weights(hh):
                for r in range(chunks):
                    sl, mask = pl.ds(r * ROWS, ROWS), _chunk_mask(r, a)
                    if mask is False:
                        wb_ref[hh, sl, :] = zeros
                        hi_ref[hh, sl, :] = zeros
                        continue
                    w = jnp.exp(lsig_ref[hh, sl, :] + (lt_ref[hh, sl, :] - (cls_ref[hh, sl, :] + pre_ref[hh, sl, :])))
                    if mask is not None:
                        w = jnp.where(mask, w, 0.0)
                    wb_ref[hh, sl, :] = w.astype(BF16)
                    g = w * dw_ref[hh, sl, :]
                    g_ref[hh, sl, :] = g
                    hi_ref[hh, sl, :] = g.astype(BF16)
                    sg_ref[hh, sl, :] = jnp.sum(g, axis=1, keepdims=True)

            def earlier(hh):
                dw_ref[hh] = _dot(hi_ref[hh], before_ref[...])

            def logit_grads(hh):
                for r in range(chunks):
                    sl, mask = pl.ds(r * ROWS, ROWS), _chunk_mask(r, a)
                    if mask is False:
                        lo_ref[hh, sl, :] = zeros
                        continue
                    beta = jnp.exp(lsig_ref[hh, sl, :])
                    dz = g_ref[hh, sl, :] * (1.0 - beta) - beta * (cg_ref[hh, sl, :] + dw_ref[hh, sl, :])
                    if mask is not None:
                        dz = jnp.where(mask, dz, 0.0)
                    lo_ref[hh, sl, :] = (dz * SCALE).astype(BF16)
                    cls_ref[hh, sl, :] += sls_ref[hh, sl, :]
                    cg_ref[hh, sl, :] += sg_ref[hh, sl, :]

            def grads(hh):
                dq_acc[:, _head(hh)] += _dot(lo_ref[hh], k_ref[pl.ds(off, TK), _head(hh)].astype(BF16))
                dk_acc[hh, j] += _dot(qt_ref[hh], lo_ref[hh])
                dv_acc[hh, j] += _dot(dot_ref[hh], wb_ref[hh])

            _staggered([(scores,), (logs, upto), (weights, earlier), (logit_grads, grads)], hp)

        def step(j, carry):
            tile(j, None)
            return carry

        lax.fori_loop(0, i * per, step, 0)
        for a in range(per):
            tile(i * per + a, a)
        dq_ref[...] = dq_acc[...].astype(BF16)

        @pl.when(i == nq - 1)
        def _():
            for hh in range(hp):
                for t in range(nt):
                    dk_ref[pl.ds(t * TK, TK), _head(hh)] = dk_acc[hh, t].T.astype(BF16)
                    dv_ref[pl.ds(t * TK, TK), _head(hh)] = dv_acc[hh, t].T.astype(BF16)

    wd = hp * HEAD
    sq = lambda dt: pltpu.VMEM((hp, qb_rows, TK), dt)
    tiles = lambda dt: pltpu.VMEM((hp, nt, HEAD, TK), dt)
    blk = pl.BlockSpec((qb_rows, wd), lambda h, i: (i, h))
    col = pl.BlockSpec((s, wd), lambda h, i: (0, h))
    shp = jax.ShapeDtypeStruct((s, nh * HEAD), BF16)
    return _call(
        body, name="attn_a_bwd", grid=(ng, nq),
        in_specs=[blk,
                  pl.BlockSpec((s, wd), lambda h, i: (0, ng + h)),
                  pl.BlockSpec((s, wd), lambda h, i: (0, 2 * ng + h)),
                  pl.BlockSpec((hp, qb_rows, 1), lambda h, i: (h, i, 0)), blk],
        out_specs=[blk, col, col], out_shape=[shp] * 3,
        scratch=[pltpu.VMEM((qb_rows, wd), F32), tiles(F32), tiles(F32),
                 pltpu.VMEM((qb_rows, wd), BF16), pltpu.VMEM((qb_rows, wd), BF16),
                 pltpu.VMEM((TK, TK), BF16), pltpu.VMEM((TK, TK), BF16)]
        + [pltpu.VMEM((hp, qb_rows, 1), F32)] * 4 + [sq(F32)] * 5 + [sq(BF16)] * 3
        + [tiles(BF16), tiles(BF16), pltpu.VMEM((hp, HEAD, qb_rows), BF16), pltpu.VMEM((hp, HEAD, qb_rows), BF16)],
        sem=("parallel", "arbitrary"), args=(proj, proj, proj, lt, dya), comm=comm)


def _attn_b_bwd(proj, dyb, qg, kg, ext, nh, comm=None):
    s = proj.shape[0]
    nq = s // TQ

    def body(q_ref, k_ref, v_ref, do_ref, qg_ref, kg_ref, e_ref,
             dq_ref, dk_ref, dv_ref, dqg_ref, dkg_ref, de_ref,
             kn_pad, v_pad, bias_ref, db_acc, dkn_acc, dv_acc):
        i = pl.program_id(1)

        @pl.when(i == 0)
        def _():
            _band_fill(k_ref, v_ref, kg_ref, kn_pad, v_pad, s)
            _build_bias(e_ref, bias_ref)
            db_acc[...] = jnp.zeros_like(db_acc)
            dkn_acc[...] = jnp.zeros_like(dkn_acc)
            dv_acc[...] = jnp.zeros_like(dv_acc)
            dqg_ref[...] = jnp.zeros_like(dqg_ref)

        rq, qhat, qn, kw, off, pn = _band_probs(q_ref, qg_ref, kn_pad, bias_ref, i)
        dob = do_ref[...].astype(BF16)
        dp = _dot_nt(dob, v_pad[pl.ds(off, WIN), :])
        dsc = pn * (dp - jnp.sum(pn * dp, axis=1, keepdims=True))
        db_acc[:, :WIN] += dsc
        dsb = (dsc * SCALE).astype(BF16)
        dqn = _dot(dsb, kw)
        dkn_acc[pl.ds(off, WIN), :] += _dot_tn(dsb, qn)
        dv_acc[pl.ds(off, WIN), :] += _dot_tn(pn.astype(BF16), dob)
        dqh = dqn * qg_ref[...]
        dq_ref[...] = (rq * (dqh - qhat * jnp.mean(dqh * qhat, axis=1, keepdims=True))).astype(BF16)
        dqg_ref[...] += jnp.sum(dqn * qhat, axis=0, keepdims=True)

        @pl.when(i == nq - 1)
        def _():
            k = k_ref[...]
            rk = lax.rsqrt(jnp.mean(k * k, axis=1, keepdims=True) + NORM_EPS)
            khat = k * rk
            dkn = dkn_acc[pl.ds(PAD, s), :]
            dkh = dkn * kg_ref[...]
            dk_ref[...] = (rk * (dkh - khat * jnp.mean(dkh * khat, axis=1, keepdims=True))).astype(BF16)
            dkg_ref[...] = jnp.sum(dkn * khat, axis=0, keepdims=True)
            dv_ref[...] = dv_acc[pl.ds(PAD, s), :].astype(BF16)
            de_ref[...] = _reduce_bias_grad(db_acc)

    blk = pl.BlockSpec((TQ, HEAD), lambda h, i: (i, h))
    col = pl.BlockSpec((s, HEAD), lambda h, i: (0, h))
    vec = pl.BlockSpec((1, HEAD), lambda h, i: (0, 0))
    hvec = pl.BlockSpec((None, 1, HEAD), lambda h, i: (h, 0, 0))
    hext = pl.BlockSpec((None, 1, EXT), lambda h, i: (h, 0, 0))
    shp = jax.ShapeDtypeStruct((s, nh * HEAD), BF16)
    return _call(
        body, name="attn_b_bwd", grid=(nh, nq),
        in_specs=[pl.BlockSpec((TQ, HEAD), lambda h, i: (i, 4 * nh + h)),
                  pl.BlockSpec((s, HEAD), lambda h, i: (0, 5 * nh + h)),
                  pl.BlockSpec((s, HEAD), lambda h, i: (0, 6 * nh + h)),
                  blk, vec, vec, hext],
        out_specs=[blk, col, col, hvec, hvec, hext],
        out_shape=[shp] * 3 + [jax.ShapeDtypeStruct((nh, 1, HEAD), F32)] * 2
        + [jax.ShapeDtypeStruct((nh, 1, EXT), F32)],
        scratch=[pltpu.VMEM((s + PAD, HEAD), BF16), pltpu.VMEM((s + PAD, HEAD), BF16),
                 pltpu.VMEM((TQ, WIN), F32), pltpu.VMEM((TQ, EXT), F32),
                 pltpu.VMEM((s + PAD, HEAD), F32), pltpu.VMEM((s + PAD, HEAD), F32)],
        sem=("parallel", "arbitrary"), args=(proj, proj, proj, dyb, qg, kg, ext), comm=comm)


def _in_proj_bwd(dproj, wt, x, dxo, g, comm=None):
    s, d = x.shape
    tm, tk = min(s, 512), min(wt.shape[0], 1024)
    nk = wt.shape[0] // tk

    def body(dp_ref, w_ref, x_ref, dxo_ref, g_ref, dx_ref, dg_ref, acc):
        m, k = pl.program_id(0), pl.program_id(1)

        @pl.when(k == 0)
        def _():
            acc[...] = jnp.zeros_like(acc)

        @pl.when((k == 0) & (m == 0))
        def _():
            dg_ref[...] = jnp.zeros_like(dg_ref)

        acc[...] += _dot(dp_ref[...], w_ref[...])

        @pl.when(k == nk - 1)
        def _():
            xv = x_ref[...]
            r = lax.rsqrt(jnp.mean(xv * xv, axis=1, keepdims=True) + NORM_EPS)
            xhat = xv * r
            dh = acc[...]
            dxh = dh * g_ref[...]
            dx_ref[...] = dxo_ref[...] + r * (dxh - xhat * jnp.mean(dxh * xhat, axis=1, keepdims=True))
            dg_ref[...] += jnp.sum(dh * xhat, axis=0, keepdims=True)

    row = pl.BlockSpec((tm, d), lambda m, k: (m, 0))
    return _call(
        body, name="in_proj_bwd", grid=(s // tm, nk),
        in_specs=[pl.BlockSpec((tm, tk), lambda m, k: (m, k)),
                  pl.BlockSpec((tk, d), lambda m, k: (k, 0)),
                  row, row, pl.BlockSpec((1, d), lambda m, k: (0, 0))],
        out_specs=[row, pl.BlockSpec((8, d), lambda m, k: (0, 0))],
        out_shape=[jax.ShapeDtypeStruct((s, d), F32), jax.ShapeDtypeStruct((8, d), F32)],
        scratch=[pltpu.VMEM((tm, d), F32)], sem=("arbitrary", "arbitrary"), vmem_mb=56,
        args=(dproj, wt, x, dxo, g), comm=comm)


def _place():
    x, y, c = lax.axis_index("x"), lax.axis_index("y"), lax.axis_index("c")
    chips = [(1 - x, y), (x, 1 - y), (1 - x, 1 - y)]
    return x, y, c, chips


def _comm_call(body, name, ins, out_shape, n_remote, n_local, aliases=None):
    return pl.pallas_call(
        body, name=name, in_specs=[ANY] * len(ins), out_specs=[ANY] * len(out_shape), out_shape=out_shape,
        input_output_aliases=aliases or {},
        scratch_shapes=[pltpu.SemaphoreType.DMA((n_remote,)), pltpu.SemaphoreType.DMA((n_remote,)),
                        pltpu.SemaphoreType.DMA((n_local,))])(*ins)


def _rcopy(src, dst, send_sems, recv_sems, k, dev):
    return pltpu.make_async_remote_copy(src_ref=src, dst_ref=dst, send_sem=send_sems.at[k], recv_sem=recv_sems.at[k],
                                        device_id=dev, device_id_type=MESH)


def _run_comm(comm, name):
    n_ci, n_co = len(comm.ins), len(comm.outs)

    def body(*refs):
        cins, couts = refs[:n_ci], refs[n_ci:n_ci + n_co]
        send_sems, recv_sems = refs[n_ci + n_co:]
        comm.start(cins, couts, send_sems, recv_sems)
        comm.finish(cins, couts, send_sems, recv_sems)

    return pl.pallas_call(
        body, name=name, in_specs=[ANY] * n_ci, out_specs=[ANY] * n_co, out_shape=list(comm.outs),
        input_output_aliases=dict(comm.aliases),
        scratch_shapes=[pltpu.SemaphoreType.DMA((comm.n_sems,)), pltpu.SemaphoreType.DMA((comm.n_sems,))])(*comm.ins)


MOVES = ((None, 0, 2, 0), (None, 0, 2, 1), (0, 0, 1, 1), (1, 1, 2, 0))
ARRIVALS = ((0, 0, 2), (1, 0, 2), (2, 0, 1), (2, 1, 2))


def _gather_comm(fulls, rbp=None, stage="all", chips_at=(0, 1, 2, 3)):
    n = len(fulls)
    n_ici = len(MOVES) * n
    base = n_ici if stage == "all" else 0
    per_tensor = chips_at if isinstance(chips_at, list) else [chips_at] * n

    def region(full, blk, core, lo, hi):
        quarter = full.shape[1] // 4
        return full.at[blk].at[pl.ds(core * 2 * quarter + lo * quarter, (hi - lo) * quarter)]

    def block_of(chip):
        return 2 * chip[0] + chip[1]

    def ici(couts, send_sems, recv_sems, x, y, c, chips, at):
        out = []
        for t, full in enumerate(couts[:n]):
            for p, (whose, lo, hi, to) in enumerate(MOVES):
                if p in at[t]:
                    src = region(full, 2 * x + y if whose is None else block_of(chips[whose]), c, lo, hi)
                    out.append(_rcopy(src, src, send_sems, recv_sems, len(MOVES) * t + p, (*chips[to], c)))
        return out

    def landed(couts, send_sems, recv_sems, x, y, c, chips, core, first, at=None):
        out = []
        for t, full in enumerate(couts[:n]):
            for p, (whose, lo, hi) in enumerate(ARRIVALS):
                if at is None or p in at[t]:
                    got = region(full, block_of(chips[whose]), core, lo, hi)
                    out.append(_rcopy(got, got, send_sems, recv_sems, first + len(MOVES) * t + p, (x, y, 1 - c)))
        return out

    def small(cins, couts, send_sems, recv_sems, x, y, c, chips):
        b = 2 * x + y
        return ([_rcopy(cins[n], couts[n].at[b], send_sems, recv_sems, 2 * n_ici + j, (*chip, c))
                 for j, chip in enumerate(chips)],
                pltpu.make_async_copy(cins[n], couts[n].at[b], send_sems.at[2 * n_ici + 3]))

    first_hop, second_hop = [(0, 1)] * n, [(2, 3)] * n

    def start(cins, couts, send_sems, recv_sems):
        x, y, c, chips = _place()
        if stage == "sibling":
            for cp in landed(couts, send_sems, recv_sems, x, y, c, chips, c, base):
                cp.start()
            return
        for cp in ici(couts, send_sems, recv_sems, x, y, c, chips, first_hop if stage == "all" else per_tensor):
            cp.start()
        if rbp is not None:
            remote, local = small(cins, couts, send_sems, recv_sems, x, y, c, chips)
            for cp in remote:
                cp.start()
            local.start()

    def finish(cins, couts, send_sems, recv_sems):
        x, y, c, chips = _place()
        place = (couts, send_sems, recv_sems, x, y, c, chips)
        passed = landed(*place, c, base)
        if stage == "chips":
            for cp in landed(*place, c, 0, per_tensor):
                cp.wait_recv()
            for cp in ici(*place, per_tensor):
                cp.wait_send()
        if stage == "all":
            for cp in landed(*place, c, 0, first_hop):
                cp.wait_recv()
            for cp in ici(*place, second_hop):
                cp.start()
            for cp in landed(*place, c, 0, second_hop):
                cp.wait_recv()
            for cp in passed:
                cp.start()
            for cp in ici(*place, first_hop) + ici(*place, second_hop):
                cp.wait_send()
        if stage != "chips":
            for cp in landed(*place, 1 - c, base):
                cp.wait_recv()
            for cp in passed:
                cp.wait_send()
        if rbp is not None:
            remote, local = small(cins, couts, send_sems, recv_sems, x, y, c, chips)
            for j, chip in enumerate(chips):
                got = couts[n].at[2 * chip[0] + chip[1]]
                _rcopy(got, got, send_sems, recv_sems, 2 * n_ici + j, (x, y, c)).wait_recv()
            for cp in remote:
                cp.wait_send()
            local.wait()

    outs = [jax.ShapeDtypeStruct(f.shape, f.dtype) for f in fulls]
    ins = list(fulls)
    if rbp is not None:
        ins.append(rbp)
        outs.append(jax.ShapeDtypeStruct((4,) + rbp.shape, F32))
    return _Comm(tuple(ins), tuple(outs), {t: t for t in range(n)}, 2 * n_ici + 4, start, finish)


def _chips_comm(sums):
    n = len(sums)

    def copies(cins, couts, send_sems, recv_sems):
        x, y, c, chips = _place()
        return [_rcopy(cins[t].at[2 * chip[0] + chip[1]], couts[t].at[j], send_sems, recv_sems, 3 * t + j, (*chip, c))
                for t in range(n) for j, chip in enumerate(chips)]

    def start(*refs):
        for cp in copies(*refs):
            cp.start()

    def finish(*refs):
        for cp in copies(*refs):
            cp.wait()

    outs = tuple(jax.ShapeDtypeStruct((3,) + p.shape[1:], p.dtype) for p in sums)
    return _Comm(tuple(sums), outs, {}, 3 * n, start, finish)


def _pair_comm(ins, outs, aliases, copies):
    def start(*refs):
        for cp in copies(*refs):
            cp.start()

    def finish(*refs):
        for cp in copies(*refs):
            cp.wait()

    return _Comm(tuple(ins), tuple(outs), aliases, len(ins), start, finish)


def _sibling_comm(parts):
    def copies(cins, couts, send_sems, recv_sems):
        x, y, c, _ = _place()
        return [_rcopy(p.at[:, pl.ds((1 - c) * (p.shape[1] // 2), p.shape[1] // 2), :], couts[t],
                       send_sems, recv_sems, t, (x, y, 1 - c)) for t, p in enumerate(cins)]

    half = [jax.ShapeDtypeStruct((p.shape[0], p.shape[1] // 2, p.shape[2]), p.dtype) for p in parts]
    return _pair_comm(parts, half, {}, copies)


def _halves_comm(tots):
    def copies(cins, couts, send_sems, recv_sems):
        x, y, c, _ = _place()
        return [_rcopy(g.at[c], g.at[c], send_sems, recv_sems, t, (x, y, 1 - c)) for t, g in enumerate(couts)]

    return _pair_comm(tots, [jax.ShapeDtypeStruct(t.shape, t.dtype) for t in tots],
                      {t: t for t in range(len(tots))}, copies)


def _gather_small(packed):
    def body(p_ref, all_ref, send_sems, recv_sems, loc_sems):
        x, y, c, _ = _place()
        me = 4 * x + 2 * y + c
        local = pltpu.make_async_copy(p_ref, all_ref.at[me], loc_sems.at[0])
        local.start()
        sent = []
        for k in range(1, 8):
            px, py, pc = x ^ (k >> 2), y ^ ((k >> 1) & 1), c ^ (k & 1)
            cp = _rcopy(p_ref, all_ref.at[me], send_sems, recv_sems, k - 1, (px, py, pc))
            cp.start()
            sent.append(cp)
        for k in range(1, 8):
            px, py, pc = x ^ (k >> 2), y ^ ((k >> 1) & 1), c ^ (k & 1)
            got = all_ref.at[4 * px + 2 * py + pc]
            _rcopy(got, got, send_sems, recv_sems, k - 1, (x, y, c)).wait_recv()
        for cp in sent:
            cp.wait_send()
        local.wait()

    return _comm_call(body, "gather_small", [packed], [jax.ShapeDtypeStruct((8,) + packed.shape, F32)], 7, 1)[0]


def _sum_devices(allp):
    n, r, c = allp.shape

    def body(a_ref, o_ref):
        acc = a_ref[0]
        for k in range(1, n):
            acc = acc + a_ref[k]
        o_ref[...] = acc

    return pl.pallas_call(body, name="sum_devices", out_shape=jax.ShapeDtypeStruct((r, c), F32))(allp)


def _ext_index():
    u = np.arange(EXT)
    dist = np.where(u < WIN, PAD - u, PAD + EXT - u)
    return np.clip(dist, -(CHUNK - 1), REL_CLIP) + (CHUNK - 1)


def _pack(parts, rows):
    flat = jnp.concatenate([p.reshape(-1) for p in parts])
    return jnp.pad(flat, (0, rows * 128 - flat.shape[0])).reshape(rows, 128)


def _unpack(packed, shapes):
    flat, out, at = packed.reshape(-1), [], 0
    for shp in shapes:
        size = int(np.prod(shp))
        out.append(flat[at:at + size].reshape(shp))
        at += size
    return out


def kernel(x, norm_g, w_in, q_norm_g, k_norm_g, rel_bias, w_out, loss_target, m_norm_g, m_w_in, m_q_norm_g, m_k_norm_g, m_rel_bias, m_w_out, v_norm_g, v_w_in, v_q_norm_g, v_k_norm_g, v_rel_bias, v_w_out):
    nl, d, nb = w_in.shape
    s = x.shape[1]
    ds_ = d // 2
    nh = ds_ // HEAD
    rb = w_out.shape[1]
    nrel = rel_bias.shape[2]
    bx = lax.axis_index("x") * 2 + lax.axis_index("y")

    rb_rows = -(-(nl * nh * nrel) // 1024) * 8
    cx = lax.axis_index("c")
    wi_full = [_cast_block(w_in, l, bx, "cast_w_in") for l in range(nl)]
    wo_full = [_cast_block(w_out, l, bx, "cast_w_out") for l in range(nl)]
    wi_full[0], wo_full[0], rel_all = _run_comm(
        _gather_comm([wi_full[0], wo_full[0]], _pack([rel_bias], rb_rows)), "gather_first")
    rel_full = jnp.concatenate(
        [rel_all[j].reshape(-1)[:nl * nh * nrel].reshape(nl, nh, nrel) for j in range(4)], axis=2)
    ext_idx = _ext_index()
    onehot = jnp.asarray(ext_idx[:, None] == np.arange(N_REL)[None, :], F32)
    ext = jnp.einsum("lhr,ur->lhu", rel_full, onehot, precision=lax.Precision.HIGHEST).reshape(nl, nh, 1, EXT)

    xs, hs, projs, yas, lts, ybs, mixes, wi_t = [], [], [], [], [], [], [], []
    xc = x[0]
    h = _rmsnorm_fwd(xc, norm_g[0:1])
    for l in range(nl):
        nxt = l + 1 < nl
        (proj, wt), got = _in_proj(h, wi_full[l], _gather_comm(
            [wo_full[l + 1]], stage="chips", chips_at=(0, 1)) if nxt else None)
        wi_t.append(wt)
        if nxt:
            wo_full[l + 1] = got[0]
        (ya, lt), got = _attn_a_fwd(proj, nh, _gather_comm(
            [wi_full[l + 1]], stage="chips", chips_at=(0, 1)) if nxt else None)
        if nxt:
            wi_full[l + 1] = got[0]
        (yb,), got = _attn_b_fwd(proj, q_norm_g[l:l + 1], k_norm_g[l:l + 1], ext[l], nh, _gather_comm(
            [wi_full[l + 1], wo_full[l + 1]], stage="chips", chips_at=(2, 3)) if nxt else None)
        if nxt:
            wi_full[l + 1], wo_full[l + 1] = got
        xs.append(xc)
        hs.append(h); projs.append(proj); yas.append(ya); lts.append(lt); ybs.append(yb)
        wo_now = wo_full[l].reshape(4 * rb, d)
        (mix, after, then), got = _out_proj(
            xc, ya, yb, proj, wo_now, norm_g[l + 1:l + 2] if nxt else loss_target[0],
            _gather_comm([wi_full[l + 1], wo_full[l + 1]], stage="sibling") if nxt else None)
        mixes.append(mix)
        if nxt:
            wi_full[l + 1], wo_full[l + 1] = got
            xc, h = after, then
        else:
            dx, loss_tile = after, then

    small, g_wi, g_wo = [None] * nl, [None] * nl, [None] * nl
    pending = None

    def keep(lay, shared):
        g_wi[lay], g_wo[lay] = shared[0].reshape(d, nb), shared[1].reshape(rb, d)

    for l in reversed(range(nl)):
        wo = wo_full[l].reshape(4 * rb, d)
        p_wo = _wgrad(mixes[l], dx, 4, False, "wgrad_out")
        last = l == 0 and pending is not None
        (dya, dyb, dga, dgb), theirs_wo = _out_proj_bwd(dx, yas[l], ybs[l], projs[l], wo,
                                                        _sibling_comm([p_wo]) if last else None)
        travelling = list(pending[1]) if pending else []
        if last:
            sums_wo = _add_sibling(p_wo, theirs_wo[0], cx)
            travelling.append(sums_wo)
        (dqa, dka, dva), got = _attn_a_bwd(projs[l], lts[l], dya, nh, _chips_comm(travelling) if pending else None)
        tots = [_add_chips(pending[1][t], got[t], bx, cx) for t in range(2)] if pending else None
        (dqb, dkb, dvb, dqg, dkg, dext), shared = _attn_b_bwd(
            projs[l], dyb, q_norm_g[l:l + 1], k_norm_g[l:l + 1], ext[l], nh, _halves_comm(tots) if pending else None)
        if pending:
            keep(pending[0], shared)
        dproj = jnp.concatenate([dqa, dka, dva, dga, dqb, dkb, dvb, dgb], axis=1)
        parts = [_wgrad(hs[l], dproj, 4, True, "wgrad_in"), p_wo]
        if l > 0:
            (dx, dng), theirs = _in_proj_bwd(dproj, wi_t[l],xs[l], dx, norm_g[l:l + 1], _sibling_comm(parts))
            pending = (l, [_add_sibling(parts[t], theirs[t], cx) for t in range(2)])
        elif last:
            theirs = _run_comm(_sibling_comm(parts[:1]), "reduce_sibling")
            sums_wi = _add_sibling(parts[0], theirs[0], cx)
            (dx, dng), got_wi = _in_proj_bwd(dproj, wi_t[l],xs[l], dx, norm_g[l:l + 1], _chips_comm([sums_wi]))
            keep(0, _run_comm(_halves_comm([_add_chips(sums_wi, got_wi[0], bx, cx),
                                            _add_chips(sums_wo, got[2], bx, cx)]), "share_halves"))
        else:
            theirs = _run_comm(_sibling_comm(parts), "reduce_sibling")
            sums = [_add_sibling(parts[t], theirs[t], cx) for t in range(2)]
            (dx, dng), got = _in_proj_bwd(dproj, wi_t[l],xs[l], dx, norm_g[l:l + 1], _chips_comm(sums))
            keep(0, _run_comm(_halves_comm([_add_chips(sums[t], got[t], bx, cx) for t in range(2)]), "share_halves"))
        small[l] = (dng[0], jnp.sum(dqg, axis=0).reshape(-1), jnp.sum(dkg, axis=0).reshape(-1), dext.reshape(nh, EXT))
    grad_x = dx[None]

    small_shapes = [(nl, d), (nl, HEAD), (nl, HEAD), (nl, nh, EXT), (1,)]
    small_parts = [jnp.stack([sm[i] for sm in small]) for i in range(4)] + [loss_tile[0, :1]]
    rows = -(-sum(int(np.prod(sh)) for sh in small_shapes) // 1024) * 8
    tot = _sum_devices(_gather_small(_pack(small_parts, rows)))
    g_ng, g_qg, g_kg, g_ext, loss = _unpack(tot, small_shapes)
    g_rel_full = jnp.einsum("lhu,ur->lhr", g_ext, onehot, precision=lax.Precision.HIGHEST)
    g_rel = lax.dynamic_slice_in_dim(g_rel_full, bx * nrel, nrel, axis=2)

    res_wi, res_wo = (), ()
    for l in range(nl):
        res_wi = _adamw_layer(l, w_in, g_wi[l], m_w_in, v_w_in, res_wi, "adamw_w_in")
        res_wo = _adamw_layer(l, w_out, g_wo[l], m_w_out, v_w_out, res_wo, "adamw_w_out")
    g_wi, d_wi, nm_wi, nv_wi = res_wi
    g_wo, d_wo, nm_wo, nv_wo = res_wo
    sm_shapes = [(nl, d), (nl, HEAD), (nl, HEAD), (nl, nh, nrel)]
    sm_rows = -(-sum(int(np.prod(sh)) for sh in sm_shapes) // 1024) * 8
    pw, pg, pm, pv = [_pack(group, sm_rows) for group in (
        (norm_g, q_norm_g, k_norm_g, rel_bias), (g_ng, g_qg, g_kg, g_rel),
        (m_norm_g, m_q_norm_g, m_k_norm_g, m_rel_bias), (v_norm_g, v_q_norm_g, v_k_norm_g, v_rel_bias))]
    d_sm, nm_sm, nv_sm = [_unpack(a[0], sm_shapes)
                          for a in _adamw_layer(0, pw[None], pg, pm[None], pv[None], (), "adamw_small")[1:]]

    return (loss[0], grad_x, g_ng, g_wi, g_qg, g_kg, g_rel, g_wo,
            d_sm[0], d_wi, d_sm[1], d_sm[2], d_sm[3], d_wo,
            nm_sm[0], nm_wi, nm_sm[1], nm_sm[2], nm_sm[3], nm_wo,
            nv_sm[0], nv_wi, nv_sm[1], nv_sm[2], nv_sm[3], nv_wo)
```

```python
from typing import Callable, NamedTuple

import jax
import jax.numpy as jnp
import numpy as np
from jax import lax
from jax.experimental import pallas as pl
from jax.experimental.pallas import tpu as pltpu

F32 = jnp.float32
BF16 = jnp.bfloat16

HEAD = 128
CHUNK = 64
LEFT_CHUNKS = 8
REL_CLIP = 256
N_REL = REL_CLIP + CHUNK
NORM_EPS = 1e-6
NEG_BIG = -1e30
TQ = 256
TK = TQ
QB = 512
ROWS = 32
PAD = LEFT_CHUNKS * CHUNK
WIN = PAD + TQ
EXT = 1024
SCALE = HEAD ** -0.5

ADAM_LR = 0.001
ADAM_B1 = 0.9
ADAM_B2 = 0.999
ADAM_EPS = 1e-08
ADAM_WD = 0.01
ADAM_STEP = 10

ANY = pl.BlockSpec(memory_space=pl.ANY)
MESH = pl.DeviceIdType.MESH


def _params(sem=None, vmem_mb=None):
    kw = {}
    if sem is not None:
        kw["dimension_semantics"] = sem
    if vmem_mb is not None:
        kw["vmem_limit_bytes"] = vmem_mb << 20
    return pltpu.CompilerParams(**kw)


class _Comm(NamedTuple):
    ins: tuple
    outs: tuple
    aliases: dict
    n_sems: int
    start: Callable
    finish: Callable


def _call(body, *, name, grid, in_specs, out_specs, out_shape, args, scratch=(), sem=None, vmem_mb=None, comm=None):
    if comm is None:
        out = pl.pallas_call(body, name=name, grid=grid, in_specs=in_specs, out_specs=out_specs, out_shape=out_shape,
                             scratch_shapes=list(scratch), compiler_params=_params(sem, vmem_mb))(*args)
        return out, ()
    n_in, n_out, n_ci, n_co = len(in_specs), len(out_shape), len(comm.ins), len(comm.outs)

    def hosted(*refs):
        ins, cins = refs[:n_in], refs[n_in:n_in + n_ci]
        outs, couts = refs[n_in + n_ci:n_in + n_ci + n_out], refs[n_in + n_ci + n_out:n_in + n_ci + n_out + n_co]
        rest = refs[n_in + n_ci + n_out + n_co:]
        send_sems, recv_sems = rest[-2:]
        first, last = None, None
        for ax, size in enumerate(grid):
            at = pl.program_id(ax)
            first = (at == 0) if first is None else first & (at == 0)
            last = (at == size - 1) if last is None else last & (at == size - 1)

        @pl.when(first)
        def _():
            comm.start(cins, couts, send_sems, recv_sems)

        body(*ins, *outs, *rest[:-2])

        @pl.when(last)
        def _():
            comm.finish(cins, couts, send_sems, recv_sems)

    out = pl.pallas_call(
        hosted, name=name, grid=grid, in_specs=list(in_specs) + [ANY] * n_ci, out_specs=list(out_specs) + [ANY] * n_co,
        out_shape=list(out_shape) + list(comm.outs),
        input_output_aliases={n_in + k: n_out + v for k, v in comm.aliases.items()},
        scratch_shapes=list(scratch) + [pltpu.SemaphoreType.DMA((comm.n_sems,)), pltpu.SemaphoreType.DMA((comm.n_sems,))],
        compiler_params=_params(("arbitrary",) * len(grid), vmem_mb))(*args, *comm.ins)
    return out[:n_out], out[n_out:]


def _dot(a, b):
    return jnp.dot(a, b, preferred_element_type=F32)


def _dot_nt(a, b):
    return lax.dot_general(a, b, (((1,), (1,)), ((), ())), preferred_element_type=F32)


def _dot_tn(a, b):
    return lax.dot_general(a, b, (((0,), (0,)), ((), ())), preferred_element_type=F32)


def _split_dot(x, m):
    hi = x.astype(BF16)
    lo = (x - hi.astype(F32)).astype(BF16)
    return _dot(hi, m) + _dot(lo, m)


def _silu_parts(g):
    sg = 1.0 / (1.0 + jnp.exp(-g))
    return g * sg, sg * (1.0 + g * (1.0 - sg))


def _idx(*vals):
    return jnp.stack([jnp.asarray(v, jnp.int32) for v in vals])


def _cast_block(w, l, blk, name):
    _, r, c = w.shape
    tr = min(r, 512)

    def body(b_ref, w_ref, o_ref):
        o_ref[...] = w_ref[...].astype(BF16)

    spec = pltpu.PrefetchScalarGridSpec(
        num_scalar_prefetch=1, grid=(r // tr,),
        in_specs=[pl.BlockSpec((None, tr, c), lambda i, b: (l, i, 0))],
        out_specs=pl.BlockSpec((None, tr, c), lambda i, b: (b[0], i, 0)))
    return pl.pallas_call(body, name=name, grid_spec=spec, out_shape=jax.ShapeDtypeStruct((4, r, c), BF16),
                          compiler_params=_params(("parallel",)))(_idx(blk), w)


def _add_sibling(p, theirs, core):
    nblk, r, c = p.shape
    hr = r // 2
    tr = min(hr, 256)
    per = hr // tr

    def body(c_ref, p_ref, t_ref, o_ref):
        o_ref[...] = (p_ref[...].astype(F32) + t_ref[...].astype(F32)).astype(BF16)

    blk = pl.BlockSpec((None, tr, c), lambda j, i, cr: (j, i, 0))
    spec = pltpu.PrefetchScalarGridSpec(
        num_scalar_prefetch=1, grid=(nblk, per),
        in_specs=[pl.BlockSpec((None, tr, c), lambda j, i, cr: (j, cr[0] * per + i, 0)), blk], out_specs=blk)
    return pl.pallas_call(body, name="add_sibling", grid_spec=spec, out_shape=jax.ShapeDtypeStruct((nblk, hr, c), BF16),
                          compiler_params=_params(("parallel", "parallel")))(_idx(core), p, theirs)


def _add_chips(sums, got, blk, core):
    _, hr, c = sums.shape
    tr = min(hr, 256)

    def body(i_ref, s_ref, g0_ref, g1_ref, g2_ref, o_ref):
        o_ref[...] = ((s_ref[...].astype(F32) + g0_ref[...].astype(F32))
                      + g1_ref[...].astype(F32)) + g2_ref[...].astype(F32)

    at = lambda j: pl.BlockSpec((None, tr, c), lambda i, ir: (j, i, 0))
    spec = pltpu.PrefetchScalarGridSpec(
        num_scalar_prefetch=1, grid=(hr // tr,),
        in_specs=[pl.BlockSpec((None, tr, c), lambda i, ir: (ir[0], i, 0)), at(0), at(1), at(2)],
        out_specs=pl.BlockSpec((None, tr, c), lambda i, ir: (ir[1], i, 0)))
    return pl.pallas_call(body, name="add_chips", grid_spec=spec, out_shape=jax.ShapeDtypeStruct((2, hr, c), F32),
                          compiler_params=_params(("parallel",)))(_idx(blk, core), sums, got, got, got)


def _adamw_layer(l, w, g, m, v, prev, name):
    nl, r, c = w.shape
    tr = min(r, 256)
    c1 = 1.0 / (1.0 - ADAM_B1 ** ADAM_STEP)
    c2 = 1.0 / (1.0 - ADAM_B2 ** ADAM_STEP)

    def body(w_ref, g_ref, m_ref, v_ref, *rest):
        go_ref, d_ref, nm_ref, nv_ref = rest[-4:]
        gg = g_ref[...]
        nm = ADAM_B1 * m_ref[...] + (1.0 - ADAM_B1) * gg
        nv = ADAM_B2 * v_ref[...] + (1.0 - ADAM_B2) * (gg * gg)
        upd = (nm * c1) / (jnp.sqrt(nv * c2) + ADAM_EPS) + ADAM_WD * w_ref[...]
        go_ref[...] = gg
        d_ref[...] = -ADAM_LR * upd
        nm_ref[...] = nm
        nv_ref[...] = nv

    lay = pl.BlockSpec((None, tr, c), lambda i: (l, i, 0))
    shp = jax.ShapeDtypeStruct((nl, r, c), F32)
    return pl.pallas_call(
        body, name=name, grid=(r // tr,),
        in_specs=[lay, pl.BlockSpec((tr, c), lambda i: (i, 0)), lay, lay] + [ANY] * len(prev),
        out_specs=[lay] * 4, out_shape=[shp] * 4, input_output_aliases={4 + k: k for k in range(len(prev))},
        compiler_params=_params(("parallel",), 40))(w, g, m, v, *prev)


def _rmsnorm_fwd(x, g):
    s, d = x.shape
    tm = min(s, 256)

    def body(x_ref, g_ref, h_ref):
        xv = x_ref[...]
        r = lax.rsqrt(jnp.mean(xv * xv, axis=1, keepdims=True) + NORM_EPS)
        h_ref[...] = (xv * r * g_ref[...]).astype(BF16)

    return pl.pallas_call(
        body, name="rmsnorm_fwd", grid=(s // tm,),
        in_specs=[pl.BlockSpec((tm, d), lambda i: (i, 0)), pl.BlockSpec((1, d), lambda i: (0, 0))],
        out_specs=pl.BlockSpec((tm, d), lambda i: (i, 0)),
        out_shape=jax.ShapeDtypeStruct((s, d), BF16),
        compiler_params=_params(("parallel",)))(x, g)


def _in_proj(h, w, comm=None):
    s, d = h.shape
    nblk, _, nb = w.shape
    tm, tn = min(s, 1024), min(nb, 1024)
    per = nb // tn

    def body(h_ref, w_ref, o_ref, wt_ref):
        o_ref[...] = _dot(h_ref[...], w_ref[...]).astype(BF16)

        @pl.when(pl.program_id(1) == 0)
        def _():
            for c in range(tn // HEAD):
                wt_ref[pl.ds(c * HEAD, HEAD), :] = w_ref[:, c * HEAD:(c + 1) * HEAD].astype(F32).T.astype(BF16)

    return _call(
        body, name="in_proj", grid=(nblk * per, s // tm),
        in_specs=[pl.BlockSpec((tm, d), lambda n, m: (m, 0)),
                  pl.BlockSpec((None, d, tn), lambda n, m: (n // per, 0, n % per))],
        out_specs=[pl.BlockSpec((tm, tn), lambda n, m: (m, n)), pl.BlockSpec((tn, d), lambda n, m: (n, 0))],
        out_shape=[jax.ShapeDtypeStruct((s, nblk * nb), BF16), jax.ShapeDtypeStruct((nblk * nb, d), BF16)],
        sem=("parallel", "arbitrary"), vmem_mb=56, args=(h, w), comm=comm)


def _heads_per_step(nh):
    return 2 if nh % 2 == 0 else 1


def _head(hh):
    return slice(hh * HEAD, (hh + 1) * HEAD)


def _tri(op):
    r = lax.broadcasted_iota(jnp.int32, (TQ, TQ), 0)
    c = lax.broadcasted_iota(jnp.int32, (TQ, TQ), 1)
    return op(r, c)


def _staggered(groups, hp):
    for hh in range(hp):
        for fn in groups[0]:
            fn(hh)
    for group in groups[1:]:
        for hh in range(hp):
            for fn in group:
                fn(hh)


def _transpose_tiles(src_ref, dst_ref, hp, nt):
    for hh in range(hp):
        for t in range(nt):
            dst_ref[hh, t] = src_ref[pl.ds(t * TK, TK), _head(hh)].astype(F32).T.astype(BF16)


def _chunk_mask(r, a):
    if a is None or r * ROWS >= (a + 1) * TK:
        return None
    if (r + 1) * ROWS <= a * TK:
        return False
    row = lax.broadcasted_iota(jnp.int32, (ROWS, TK), 0) + r * ROWS
    return row > lax.broadcasted_iota(jnp.int32, (ROWS, TK), 1) + a * TK


def _sb_logs(qk, causal):
    z = qk * SCALE
    l1p = jnp.log(1.0 + jnp.exp(-jnp.abs(z)))
    ls = jnp.minimum(-z, 0.0) - l1p
    if causal is not None:
        ls = jnp.where(causal, ls, 0.0)
    return ls, jnp.minimum(z, 0.0) - l1p


def _attn_a_fwd(proj, nh, comm=None):
    s = proj.shape[0]
    qb_rows = min(QB, s)
    nq, nt, per = s // qb_rows, s // TK, qb_rows // TK
    hp = _heads_per_step(nh)
    ng = nh // hp
    chunks = qb_rows // ROWS

    def body(q_ref, k_ref, v_ref, o_ref, lt_ref, acc_ref, qb_ref, m_ref, car_ref, sum_ref,
             z_ref, lsig_ref, aft_ref, hi_ref, lo_ref, w_ref, kt_ref):
        i = pl.program_id(1)

        @pl.when(i == 0)
        def _():
            _transpose_tiles(k_ref, kt_ref, hp, nt)

        qb_ref[...] = q_ref[...].astype(BF16)
        m_ref[...] = _tri(lambda r, c: r > c).astype(BF16)
        acc_ref[...] = jnp.zeros_like(acc_ref)
        car_ref[...] = jnp.zeros_like(car_ref)

        def tile(j, a):
            off = pl.multiple_of(j * TK, TK)

            def scores(hh):
                z_ref[hh] = _dot(qb_ref[:, _head(hh)], kt_ref[hh, j])

            def logs(hh):
                for r in range(chunks):
                    sl, mask = pl.ds(r * ROWS, ROWS), _chunk_mask(r, a)
                    if mask is False:
                        hi_ref[hh, sl, :] = jnp.zeros((ROWS, TK), BF16)
                        lo_ref[hh, sl, :] = jnp.zeros((ROWS, TK), BF16)
                        continue
                    ls, lsig = _sb_logs(z_ref[hh, sl, :], mask)
                    lsig_ref[hh, sl, :] = lsig
                    hi = ls.astype(BF16)
                    hi_ref[hh, sl, :] = hi
                    lo_ref[hh, sl, :] = (ls - hi.astype(F32)).astype(BF16)
                    sum_ref[hh, sl, :] = jnp.sum(ls, axis=1, keepdims=True)

            def after(hh):
                aft_ref[hh] = _dot(hi_ref[hh], m_ref[...]) + _dot(lo_ref[hh], m_ref[...])

            def weights(hh):
                for r in range(chunks):
                    sl, mask = pl.ds(r * ROWS, ROWS), _chunk_mask(r, a)
                    if mask is False:
                        w_ref[hh, sl, :] = jnp.zeros((ROWS, TK), BF16)
                        continue
                    w = jnp.exp(lsig_ref[hh, sl, :] + aft_ref[hh, sl, :] + car_ref[hh, sl, :])
                    if mask is not None:
                        w = jnp.where(mask, w, 0.0)
                    w_ref[hh, sl, :] = w.astype(BF16)
                    car_ref[hh, sl, :] += sum_ref[hh, sl, :]

            def values(hh):
                vb = v_ref[pl.ds(off, TK), _head(hh)].astype(BF16)
                acc_ref[:, _head(hh)] += _dot(w_ref[hh], vb)

            _staggered([(scores,), (logs, after), (weights, values)], hp)

        for a in reversed(range(per)):
            tile(i * per + a, a)

        def step(t, carry):
            tile(i * per - 1 - t, None)
            return carry

        lax.fori_loop(0, i * per, step, 0)
        o_ref[...] = acc_ref[...]
        lt_ref[...] = car_ref[...]

    wd = hp * HEAD
    sq = lambda dt: pltpu.VMEM((hp, qb_rows, TK), dt)
    return _call(
        body, name="attn_a_fwd", grid=(ng, nq),
        in_specs=[pl.BlockSpec((qb_rows, wd), lambda h, i: (i, h)),
                  pl.BlockSpec((s, wd), lambda h, i: (0, ng + h)),
                  pl.BlockSpec((s, wd), lambda h, i: (0, 2 * ng + h))],
        out_specs=[pl.BlockSpec((qb_rows, wd), lambda h, i: (i, h)),
                   pl.BlockSpec((hp, qb_rows, 1), lambda h, i: (h, i, 0))],
        out_shape=[jax.ShapeDtypeStruct((s, nh * HEAD), F32), jax.ShapeDtypeStruct((nh, s, 1), F32)],
        scratch=[pltpu.VMEM((qb_rows, wd), F32), pltpu.VMEM((qb_rows, wd), BF16), pltpu.VMEM((TK, TK), BF16),
                 pltpu.VMEM((hp, qb_rows, 1), F32), pltpu.VMEM((hp, qb_rows, 1), F32),
                 sq(F32), sq(F32), sq(F32), sq(BF16), sq(BF16), sq(BF16),
                 pltpu.VMEM((hp, nt, HEAD, TK), BF16)],
        sem=("parallel", "arbitrary"), args=(proj, proj, proj), comm=comm)


def _band_valid(i):
    cl = lax.broadcasted_iota(jnp.int32, (TQ, WIN), 0) // CHUNK
    kl = lax.broadcasted_iota(jnp.int32, (TQ, WIN), 1) // CHUNK
    first = LEFT_CHUNKS - (TQ // CHUNK) * i
    return (kl >= cl) & (kl <= cl + LEFT_CHUNKS) & (kl >= first)


def _build_bias(e_ref, bias_ref):
    e8 = jnp.broadcast_to(e_ref[...], (8, EXT))
    row = lax.broadcasted_iota(jnp.int32, (8, EXT), 0)
    t8 = jnp.zeros((8, EXT), F32)
    for b in range(8):
        t8 = jnp.where(row == b, pltpu.roll(e8, b, 1) if b else e8, t8)
    for a in range(TQ // 8):
        sl = pltpu.roll(t8, 8 * a, 1) if a else t8
        bias_ref[pl.ds(8 * a, 8), :] = sl[:, :WIN]


def _reduce_bias_grad(db_ref):
    acc = jnp.zeros((8, EXT), F32)
    for a in range(TQ // 8):
        sl = db_ref[pl.ds(8 * a, 8), :]
        acc = acc + (pltpu.roll(sl, EXT - 8 * a, 1) if a else sl)
    row = lax.broadcasted_iota(jnp.int32, (8, EXT), 0)
    tot = jnp.zeros((8, EXT), F32)
    for b in range(8):
        tot = tot + jnp.where(row == b, pltpu.roll(acc, EXT - b, 1) if b else acc, 0.0)
    return jnp.sum(tot, axis=0, keepdims=True)


def _band_fill(k_ref, v_ref, kg_ref, kn_pad, v_pad, s):
    k = k_ref[...].astype(F32)
    rk = lax.rsqrt(jnp.mean(k * k, axis=1, keepdims=True) + NORM_EPS)
    kn_pad[pl.ds(0, PAD), :] = jnp.zeros((PAD, HEAD), BF16)
    kn_pad[pl.ds(PAD, s), :] = (k * rk * kg_ref[...]).astype(BF16)
    v_pad[pl.ds(0, PAD), :] = jnp.zeros((PAD, HEAD), BF16)
    v_pad[pl.ds(PAD, s), :] = v_ref[...].astype(BF16)


def _band_probs(q_ref, qg_ref, kn_pad, bias_ref, i):
    q = q_ref[...].astype(F32)
    rq = lax.rsqrt(jnp.mean(q * q, axis=1, keepdims=True) + NORM_EPS)
    qhat = q * rq
    qn = (qhat * qg_ref[...]).astype(BF16)
    off = pl.multiple_of(i * TQ, TQ)
    kw = kn_pad[pl.ds(off, WIN), :]
    sc = _dot_nt(qn, kw) * SCALE + bias_ref[...]
    sc = jnp.where(_band_valid(i), sc, NEG_BIG)
    p = jnp.exp(sc - jnp.max(sc, axis=1, keepdims=True))
    pn = p / jnp.sum(p, axis=1, keepdims=True)
    return rq, qhat, qn, kw, off, pn


def _attn_b_fwd(proj, qg, kg, ext, nh, comm=None):
    s = proj.shape[0]
    nq = s // TQ

    def body(q_ref, k_ref, v_ref, qg_ref, kg_ref, e_ref, o_ref, kn_pad, v_pad, bias_ref):
        i = pl.program_id(1)

        @pl.when(i == 0)
        def _():
            _band_fill(k_ref, v_ref, kg_ref, kn_pad, v_pad, s)
            _build_bias(e_ref, bias_ref)

        _, _, _, _, off, pn = _band_probs(q_ref, qg_ref, kn_pad, bias_ref, i)
        o_ref[...] = _dot(pn.astype(BF16), v_pad[pl.ds(off, WIN), :])

    vec = pl.BlockSpec((1, HEAD), lambda h, i: (0, 0))
    return _call(
        body, name="attn_b_fwd", grid=(nh, nq),
        in_specs=[pl.BlockSpec((TQ, HEAD), lambda h, i: (i, 4 * nh + h)),
                  pl.BlockSpec((s, HEAD), lambda h, i: (0, 5 * nh + h)),
                  pl.BlockSpec((s, HEAD), lambda h, i: (0, 6 * nh + h)),
                  vec, vec,
                  pl.BlockSpec((None, 1, EXT), lambda h, i: (h, 0, 0))],
        out_specs=[pl.BlockSpec((TQ, HEAD), lambda h, i: (i, h))],
        out_shape=[jax.ShapeDtypeStruct((s, nh * HEAD), F32)],
        scratch=[pltpu.VMEM((s + PAD, HEAD), BF16), pltpu.VMEM((s + PAD, HEAD), BF16), pltpu.VMEM((TQ, WIN), F32)],
        sem=("parallel", "arbitrary"), args=(proj, proj, proj, qg, kg, ext), comm=comm)


def _out_proj(x, ya, yb, proj, w, then, comm=None):
    s, d = x.shape
    ds_ = ya.shape[1]
    tm = min(s, 256)
    final = then.shape[0] == s

    def body(x_ref, ya_ref, yb_ref, ga_ref, gb_ref, w_ref, then_ref, mix_ref, a_ref, b_ref):
        ma = (ya_ref[...] * _silu_parts(ga_ref[...].astype(F32))[0]).astype(BF16)
        mb = (yb_ref[...] * _silu_parts(gb_ref[...].astype(F32))[0]).astype(BF16)
        mix_ref[:, :ds_] = ma
        mix_ref[:, ds_:] = mb
        y = x_ref[...] + _dot(ma, w_ref[pl.ds(0, ds_), :]) + _dot(mb, w_ref[pl.ds(ds_, ds_), :])
        if final:
            @pl.when(pl.program_id(0) == 0)
            def _():
                b_ref[...] = jnp.zeros_like(b_ref)

            err = y - then_ref[...]
            a_ref[...] = err * (1.0 / d)
            b_ref[...] += 0.5 * jnp.sum(jnp.mean(err * err, axis=1, keepdims=True), axis=0, keepdims=True)
        else:
            a_ref[...] = y
            r = lax.rsqrt(jnp.mean(y * y, axis=1, keepdims=True) + NORM_EPS)
            b_ref[...] = (y * r * then_ref[...]).astype(BF16)

    row = lambda width: pl.BlockSpec((tm, width), lambda i: (i, 0))
    last_spec = pl.BlockSpec((8, 128), lambda i: (0, 0)) if final else row(d)
    last_shape = jax.ShapeDtypeStruct((8, 128), F32) if final else jax.ShapeDtypeStruct((s, d), BF16)
    return _call(
        body, name="out_proj", grid=(s // tm,),
        in_specs=[row(d), row(ds_), row(ds_),
                  pl.BlockSpec((tm, ds_), lambda i: (i, 3)), pl.BlockSpec((tm, ds_), lambda i: (i, 7)),
                  pl.BlockSpec((2 * ds_, d), lambda i: (0, 0)),
                  row(d) if final else pl.BlockSpec((1, d), lambda i: (0, 0))],
        out_specs=[row(2 * ds_), row(d), last_spec],
        out_shape=[jax.ShapeDtypeStruct((s, 2 * ds_), BF16), jax.ShapeDtypeStruct((s, d), F32), last_shape],
        sem=("arbitrary",) if final else ("parallel",), vmem_mb=52, args=(x, ya, yb, proj, proj, w, then), comm=comm)


def _out_proj_bwd(dxo, ya, yb, proj, w, comm=None):
    s, d = dxo.shape
    ds_ = ya.shape[1]
    tm = min(s, 256)

    def body(dx_ref, ya_ref, yb_ref, ga_ref, gb_ref, w_ref, dya_ref, dyb_ref, dga_ref, dgb_ref):
        dxb = dx_ref[...].astype(BF16)
        for y_ref, g_ref, lo, dy_ref, dg_ref in ((ya_ref, ga_ref, 0, dya_ref, dga_ref),
                                                 (yb_ref, gb_ref, ds_, dyb_ref, dgb_ref)):
            dmix = _dot_nt(dxb, w_ref[pl.ds(lo, ds_), :])
            act, dact = _silu_parts(g_ref[...].astype(F32))
            dy_ref[...] = dmix * act
            dg_ref[...] = (dmix * y_ref[...] * dact).astype(BF16)

    row = lambda width: pl.BlockSpec((tm, width), lambda i: (i, 0))
    return _call(
        body, name="out_proj_bwd", grid=(s // tm,),
        in_specs=[row(d), row(ds_), row(ds_),
                  pl.BlockSpec((tm, ds_), lambda i: (i, 3)), pl.BlockSpec((tm, ds_), lambda i: (i, 7)),
                  pl.BlockSpec((2 * ds_, d), lambda i: (0, 0))],
        out_specs=[row(ds_)] * 4,
        out_shape=[jax.ShapeDtypeStruct((s, ds_), F32)] * 2 + [jax.ShapeDtypeStruct((s, ds_), BF16)] * 2,
        sem=("parallel",), vmem_mb=48, args=(dxo, ya, yb, proj, proj, w), comm=comm)


def _wgrad(a, b, nblk, col_blocks, name):
    s, m = a.shape
    n = b.shape[1]
    if col_blocks:
        tr = min(m, 1024)
        nb = n // nblk
        tn = min(nb, 2048)
        per = nb // tn
        out_shape = (nblk, m, nb)
        out_spec = pl.BlockSpec((None, tr, tn), lambda j, r: (j // per, r, j % per))
    else:
        tn = min(n, 1024)
        tr = m // nblk
        out_shape = (nblk, tr, n)
        out_spec = pl.BlockSpec((None, tr, tn), lambda j, r: (r, 0, j))

    def body(a_ref, b_ref, o_ref):
        o_ref[...] = _dot_tn(a_ref[...].astype(BF16), b_ref[...].astype(BF16)).astype(BF16)

    return pl.pallas_call(
        body, name=name, grid=(n // tn, m // tr),
        in_specs=[pl.BlockSpec((s, tr), lambda j, r: (0, r)), pl.BlockSpec((s, tn), lambda j, r: (0, j))],
        out_specs=out_spec, out_shape=jax.ShapeDtypeStruct(out_shape, BF16),
        compiler_params=_params(("parallel", "parallel"), 48))(a, b)


def _attn_a_bwd(proj, lt, dya, nh, comm=None):
    s = proj.shape[0]
    qb_rows = min(TK, s)
    nq, nt, per = s // qb_rows, s // TK, qb_rows // TK
    hp = _heads_per_step(nh)
    ng = nh // hp
    chunks = qb_rows // ROWS

    def body(q_ref, k_ref, v_ref, lt_ref, do_ref, dq_ref, dk_ref, dv_ref, dq_acc, dk_acc, dv_acc,
             qb_ref, dob_ref, upto_ref, before_ref, cls_ref, cg_ref, sls_ref, sg_ref,
             z_ref, dw_ref, lsig_ref, pre_ref, g_ref, hi_ref, lo_ref, wb_ref, kt_ref, vt_ref, qt_ref, dot_ref):
        i = pl.program_id(1)

        @pl.when(i == 0)
        def _():
            dk_acc[...] = jnp.zeros_like(dk_acc)
            dv_acc[...] = jnp.zeros_like(dv_acc)
            _transpose_tiles(k_ref, kt_ref, hp, nt)
            _transpose_tiles(v_ref, vt_ref, hp, nt)

        dq_acc[...] = jnp.zeros_like(dq_acc)
        qb_ref[...] = q_ref[...].astype(BF16)
        dob_ref[...] = do_ref[...].astype(BF16)
        for hh in range(hp):
            qt_ref[hh] = q_ref[:, _head(hh)].astype(F32).T.astype(BF16)
            dot_ref[hh] = do_ref[:, _head(hh)].T.astype(BF16)
        upto_ref[...] = _tri(lambda r, c: r <= c).astype(BF16)
        before_ref[...] = _tri(lambda r, c: r < c).astype(BF16)
        cls_ref[...] = jnp.zeros_like(cls_ref)
        cg_ref[...] = jnp.zeros_like(cg_ref)

        def tile(j, a):
            off = pl.multiple_of(j * TK, TK)
            zeros = jnp.zeros((ROWS, TK), BF16)

            def scores(hh):
                z_ref[hh] = _dot(qb_ref[:, _head(hh)], kt_ref[hh, j])
                dw_ref[hh] = _dot(dob_ref[:, _head(hh)], vt_ref[hh, j])

            def logs(hh):
                for r in range(chunks):
                    sl, mask = pl.ds(r * ROWS, ROWS), _chunk_mask(r, a)
                    if mask is False:
                        hi_ref[hh, sl, :] = zeros
                        lo_ref[hh, sl, :] = zeros
                        continue
                    ls, lsig = _sb_logs(z_ref[hh, sl, :], mask)
                    lsig_ref[hh, sl, :] = lsig
                    hi = ls.astype(BF16)
                    hi_ref[hh, sl, :] = hi
                    lo_ref[hh, sl, :] = (ls - hi.astype(F32)).astype(BF16)
                    sls_ref[hh, sl, :] = jnp.sum(ls, axis=1, keepdims=True)

            def upto(hh):
                pre_ref[hh] = _dot(hi_ref[hh], upto_ref[...]) + _dot(lo_ref[hh], upto_ref[...])

            def weights(hh):
                for r in range(chunks):
                    sl, mask = pl.ds(r * ROWS, ROWS), _chunk_mask(r, a)
                    if mask is False:
                        wb_ref[hh, sl, :] = zeros
                        hi_ref[hh, sl, :] = zeros
                        continue
                    w = jnp.exp(lsig_ref[hh, sl, :] + (lt_ref[hh, sl, :] - (cls_ref[hh, sl, :] + pre_ref[hh, sl, :])))
                    if mask is not None:
                        w = jnp.where(mask, w, 0.0)
                    wb_ref[hh, sl, :] = w.astype(BF16)
                    g = w * dw_ref[hh, sl, :]
                    g_ref[hh, sl, :] = g
                    hi_ref[hh, sl, :] = g.astype(BF16)
                    sg_ref[hh, sl, :] = jnp.sum(g, axis=1, keepdims=True)

            def earlier(hh):
                dw_ref[hh] = _dot(hi_ref[hh], before_ref[...])

            def logit_grads(hh):
                for r in range(chunks):
                    sl, mask = pl.ds(r * ROWS, ROWS), _chunk_mask(r, a)
                    if mask is False:
                        lo_ref[hh, sl, :] = zeros
                        continue
                    z = z_ref[hh, sl, :] * SCALE
                    e = jnp.exp(-jnp.abs(z))
                    rinv = 1.0 / (1.0 + e)
                    beta = jnp.where(z >= 0.0, rinv, e * rinv)
                    dz = g_ref[hh, sl, :] * (1.0 - beta) - beta * (cg_ref[hh, sl, :] + dw_ref[hh, sl, :])
                    if mask is not None:
                        dz = jnp.where(mask, dz, 0.0)
                    lo_ref[hh, sl, :] = (dz * SCALE).astype(BF16)
                    cls_ref[hh, sl, :] += sls_ref[hh, sl, :]
                    cg_ref[hh, sl, :] += sg_ref[hh, sl, :]

            def grads(hh):
                dq_acc[:, _head(hh)] += _dot(lo_ref[hh], k_ref[pl.ds(off, TK), _head(hh)].astype(BF16))
                dk_acc[hh, j] += _dot(qt_ref[hh], lo_ref[hh])
                dv_acc[hh, j] += _dot(dot_ref[hh], wb_ref[hh])

            _staggered([(scores,), (logs, upto), (weights, earlier), (logit_grads, grads)], hp)

        def step(j, carry):
            tile(j, None)
            return carry

        lax.fori_loop(0, i * per, step, 0)
        for a in range(per):
            tile(i * per + a, a)
        dq_ref[...] = dq_acc[...].astype(BF16)

        @pl.when(i == nq - 1)
        def _():
            for hh in range(hp):
                for t in range(nt):
                    dk_ref[pl.ds(t * TK, TK), _head(hh)] = dk_acc[hh, t].T.astype(BF16)
                    dv_ref[pl.ds(t * TK, TK), _head(hh)] = dv_acc[hh, t].T.astype(BF16)

    wd = hp * HEAD
    sq = lambda dt: pltpu.VMEM((hp, qb_rows, TK), dt)
    tiles = lambda dt: pltpu.VMEM((hp, nt, HEAD, TK), dt)
    blk = pl.BlockSpec((qb_rows, wd), lambda h, i: (i, h))
    col = pl.BlockSpec((s, wd), lambda h, i: (0, h))
    shp = jax.ShapeDtypeStruct((s, nh * HEAD), BF16)
    return _call(
        body, name="attn_a_bwd", grid=(ng, nq),
        in_specs=[blk,
                  pl.BlockSpec((s, wd), lambda h, i: (0, ng + h)),
                  pl.BlockSpec((s, wd), lambda h, i: (0, 2 * ng + h)),
                  pl.BlockSpec((hp, qb_rows, 1), lambda h, i: (h, i, 0)), blk],
        out_specs=[blk, col, col], out_shape=[shp] * 3,
        scratch=[pltpu.VMEM((qb_rows, wd), F32), tiles(F32), tiles(F32),
                 pltpu.VMEM((qb_rows, wd), BF16), pltpu.VMEM((qb_rows, wd), BF16),
                 pltpu.VMEM((TK, TK), BF16), pltpu.VMEM((TK, TK), BF16)]
        + [pltpu.VMEM((hp, qb_rows, 1), F32)] * 4 + [sq(F32)] * 5 + [sq(BF16)] * 3
        + [tiles(BF16), tiles(BF16), pltpu.VMEM((hp, HEAD, qb_rows), BF16), pltpu.VMEM((hp, HEAD, qb_rows), BF16)],
        sem=("parallel", "arbitrary"), args=(proj, proj, proj, lt, dya), comm=comm)


def _attn_b_bwd(proj, dyb, qg, kg, ext, nh, comm=None):
    s = proj.shape[0]
    nq = s // TQ

    def body(q_ref, k_ref, v_ref, do_ref, qg_ref, kg_ref, e_ref,
             dq_ref, dk_ref, dv_ref, dqg_ref, dkg_ref, de_ref,
             kn_pad, v_pad, bias_ref, db_acc, dkn_acc, dv_acc):
        i = pl.program_id(1)

        @pl.when(i == 0)
        def _():
            _band_fill(k_ref, v_ref, kg_ref, kn_pad, v_pad, s)
            _build_bias(e_ref, bias_ref)
            db_acc[...] = jnp.zeros_like(db_acc)
            dkn_acc[...] = jnp.zeros_like(dkn_acc)
            dv_acc[...] = jnp.zeros_like(dv_acc)
            dqg_ref[...] = jnp.zeros_like(dqg_ref)

        rq, qhat, qn, kw, off, pn = _band_probs(q_ref, qg_ref, kn_pad, bias_ref, i)
        dob = do_ref[...].astype(BF16)
        dp = _dot_nt(dob, v_pad[pl.ds(off, WIN), :])
        dsc = pn * (dp - jnp.sum(pn * dp, axis=1, keepdims=True))
        db_acc[:, :WIN] += dsc
        dsb = (dsc * SCALE).astype(BF16)
        dqn = _dot(dsb, kw)
        dkn_acc[pl.ds(off, WIN), :] += _dot_tn(dsb, qn)
        dv_acc[pl.ds(off, WIN), :] += _dot_tn(pn.astype(BF16), dob)
        dqh = dqn * qg_ref[...]
        dq_ref[...] = (rq * (dqh - qhat * jnp.mean(dqh * qhat, axis=1, keepdims=True))).astype(BF16)
        dqg_ref[...] += jnp.sum(dqn * qhat, axis=0, keepdims=True)

        @pl.when(i == nq - 1)
        def _():
            k = k_ref[...].astype(F32)
            rk = lax.rsqrt(jnp.mean(k * k, axis=1, keepdims=True) + NORM_EPS)
            khat = k * rk
            dkn = dkn_acc[pl.ds(PAD, s), :]
            dkh = dkn * kg_ref[...]
            dk_ref[...] = (rk * (dkh - khat * jnp.mean(dkh * khat, axis=1, keepdims=True))).astype(BF16)
            dkg_ref[...] = jnp.sum(dkn * khat, axis=0, keepdims=True)
            dv_ref[...] = dv_acc[pl.ds(PAD, s), :].astype(BF16)
            de_ref[...] = _reduce_bias_grad(db_acc)

    blk = pl.BlockSpec((TQ, HEAD), lambda h, i: (i, h))
    col = pl.BlockSpec((s, HEAD), lambda h, i: (0, h))
    vec = pl.BlockSpec((1, HEAD), lambda h, i: (0, 0))
    hvec = pl.BlockSpec((None, 1, HEAD), lambda h, i: (h, 0, 0))
    hext = pl.BlockSpec((None, 1, EXT), lambda h, i: (h, 0, 0))
    shp = jax.ShapeDtypeStruct((s, nh * HEAD), BF16)
    return _call(
        body, name="attn_b_bwd", grid=(nh, nq),
        in_specs=[pl.BlockSpec((TQ, HEAD), lambda h, i: (i, 4 * nh + h)),
                  pl.BlockSpec((s, HEAD), lambda h, i: (0, 5 * nh + h)),
                  pl.BlockSpec((s, HEAD), lambda h, i: (0, 6 * nh + h)),
                  blk, vec, vec, hext],
        out_specs=[blk, col, col, hvec, hvec, hext],
        out_shape=[shp] * 3 + [jax.ShapeDtypeStruct((nh, 1, HEAD), F32)] * 2
        + [jax.ShapeDtypeStruct((nh, 1, EXT), F32)],
        scratch=[pltpu.VMEM((s + PAD, HEAD), BF16), pltpu.VMEM((s + PAD, HEAD), BF16),
                 pltpu.VMEM((TQ, WIN), F32), pltpu.VMEM((TQ, EXT), F32),
                 pltpu.VMEM((s + PAD, HEAD), F32), pltpu.VMEM((s + PAD, HEAD), F32)],
        sem=("parallel", "arbitrary"), args=(proj, proj, proj, dyb, qg, kg, ext), comm=comm)


def _in_proj_bwd(dproj, wt, x, dxo, g, comm=None):
    s, d = x.shape
    tm, tk = min(s, 512), min(wt.shape[0], 1024)
    nk = wt.shape[0] // tk

    def body(dp_ref, w_ref, x_ref, dxo_ref, g_ref, dx_ref, dg_ref, acc):
        m, k = pl.program_id(0), pl.program_id(1)

        @pl.when(k == 0)
        def _():
            acc[...] = jnp.zeros_like(acc)

        @pl.when((k == 0) & (m == 0))
        def _():
            dg_ref[...] = jnp.zeros_like(dg_ref)

        acc[...] += _dot(dp_ref[...], w_ref[...])

        @pl.when(k == nk - 1)
        def _():
            xv = x_ref[...]
            r = lax.rsqrt(jnp.mean(xv * xv, axis=1, keepdims=True) + NORM_EPS)
            xhat = xv * r
            dh = acc[...]
            dxh = dh * g_ref[...]
            dx_ref[...] = dxo_ref[...] + r * (dxh - xhat * jnp.mean(dxh * xhat, axis=1, keepdims=True))
            dg_ref[...] += jnp.sum(dh * xhat, axis=0, keepdims=True)

    row = pl.BlockSpec((tm, d), lambda m, k: (m, 0))
    return _call(
        body, name="in_proj_bwd", grid=(s // tm, nk),
        in_specs=[pl.BlockSpec((tm, tk), lambda m, k: (m, k)),
                  pl.BlockSpec((tk, d), lambda m, k: (k, 0)),
                  row, row, pl.BlockSpec((1, d), lambda m, k: (0, 0))],
        out_specs=[row, pl.BlockSpec((8, d), lambda m, k: (0, 0))],
        out_shape=[jax.ShapeDtypeStruct((s, d), F32), jax.ShapeDtypeStruct((8, d), F32)],
        scratch=[pltpu.VMEM((tm, d), F32)], sem=("arbitrary", "arbitrary"), vmem_mb=56,
        args=(dproj, wt, x, dxo, g), comm=comm)


def _place():
    x, y, c = lax.axis_index("x"), lax.axis_index("y"), lax.axis_index("c")
    chips = [(1 - x, y), (x, 1 - y), (1 - x, 1 - y)]
    return x, y, c, chips


def _comm_call(body, name, ins, out_shape, n_remote, n_local, aliases=None):
    return pl.pallas_call(
        body, name=name, in_specs=[ANY] * len(ins), out_specs=[ANY] * len(out_shape), out_shape=out_shape,
        input_output_aliases=aliases or {},
        scratch_shapes=[pltpu.SemaphoreType.DMA((n_remote,)), pltpu.SemaphoreType.DMA((n_remote,)),
                        pltpu.SemaphoreType.DMA((n_local,))])(*ins)


def _rcopy(src, dst, send_sems, recv_sems, k, dev):
    return pltpu.make_async_remote_copy(src_ref=src, dst_ref=dst, send_sem=send_sems.at[k], recv_sem=recv_sems.at[k],
                                        device_id=dev, device_id_type=MESH)


def _run_comm(comm, name):
    n_ci, n_co = len(comm.ins), len(comm.outs)

    def body(*refs):
        cins, couts = refs[:n_ci], refs[n_ci:n_ci + n_co]
        send_sems, recv_sems = refs[n_ci + n_co:]
        comm.start(cins, couts, send_sems, recv_sems)
        comm.finish(cins, couts, send_sems, recv_sems)

    return pl.pallas_call(
        body, name=name, in_specs=[ANY] * n_ci, out_specs=[ANY] * n_co, out_shape=list(comm.outs),
        input_output_aliases=dict(comm.aliases),
        scratch_shapes=[pltpu.SemaphoreType.DMA((comm.n_sems,)), pltpu.SemaphoreType.DMA((comm.n_sems,))])(*comm.ins)


MOVES = ((None, 0, 2, 0), (None, 0, 2, 1), (0, 0, 1, 1), (1, 1, 2, 0))
ARRIVALS = ((0, 0, 2), (1, 0, 2), (2, 0, 1), (2, 1, 2))


def _gather_comm(fulls, rbp=None, stage="all", chips_at=(0, 1, 2, 3)):
    n = len(fulls)
    n_ici = len(MOVES) * n
    base = n_ici if stage == "all" else 0
    per_tensor = chips_at if isinstance(chips_at, list) else [chips_at] * n

    def region(full, blk, core, lo, hi):
        quarter = full.shape[1] // 4
        return full.at[blk].at[pl.ds(core * 2 * quarter + lo * quarter, (hi - lo) * quarter)]

    def block_of(chip):
        return 2 * chip[0] + chip[1]

    def ici(couts, send_sems, recv_sems, x, y, c, chips, at):
        out = []
        for t, full in enumerate(couts[:n]):
            for p, (whose, lo, hi, to) in enumerate(MOVES):
                if p in at[t]:
                    src = region(full, 2 * x + y if whose is None else block_of(chips[whose]), c, lo, hi)
                    out.append(_rcopy(src, src, send_sems, recv_sems, len(MOVES) * t + p, (*chips[to], c)))
        return out

    def landed(couts, send_sems, recv_sems, x, y, c, chips, core, first, at=None):
        out = []
        for t, full in enumerate(couts[:n]):
            for p, (whose, lo, hi) in enumerate(ARRIVALS):
                if at is None or p in at[t]:
                    got = region(full, block_of(chips[whose]), core, lo, hi)
                    out.append(_rcopy(got, got, send_sems, recv_sems, first + len(MOVES) * t + p, (x, y, 1 - c)))
        return out

    def small(cins, couts, send_sems, recv_sems, x, y, c, chips):
        b = 2 * x + y
        return ([_rcopy(cins[n], couts[n].at[b], send_sems, recv_sems, 2 * n_ici + j, (*chip, c))
                 for j, chip in enumerate(chips)],
                pltpu.make_async_copy(cins[n], couts[n].at[b], send_sems.at[2 * n_ici + 3]))

    first_hop, second_hop = [(0, 1)] * n, [(2, 3)] * n

    def start(cins, couts, send_sems, recv_sems):
        x, y, c, chips = _place()
        if stage == "sibling":
            for cp in landed(couts, send_sems, recv_sems, x, y, c, chips, c, base):
                cp.start()
            return
        for cp in ici(couts, send_sems, recv_sems, x, y, c, chips, first_hop if stage == "all" else per_tensor):
            cp.start()
        if rbp is not None:
            remote, local = small(cins, couts, send_sems, recv_sems, x, y, c, chips)
            for cp in remote:
                cp.start()
            local.start()

    def finish(cins, couts, send_sems, recv_sems):
        x, y, c, chips = _place()
        place = (couts, send_sems, recv_sems, x, y, c, chips)
        passed = landed(*place, c, base)
        if stage == "chips":
            for cp in landed(*place, c, 0, per_tensor):
                cp.wait_recv()
            for cp in ici(*place, per_tensor):
                cp.wait_send()
        if stage == "all":
            for cp in landed(*place, c, 0, first_hop):
                cp.wait_recv()
            for cp in ici(*place, second_hop):
                cp.start()
            for cp in landed(*place, c, 0, second_hop):
                cp.wait_recv()
            for cp in passed:
                cp.start()
            for cp in ici(*place, first_hop) + ici(*place, second_hop):
                cp.wait_send()
        if stage != "chips":
            for cp in landed(*place, 1 - c, base):
                cp.wait_recv()
            for cp in passed:
                cp.wait_send()
        if rbp is not None:
            remote, local = small(cins, couts, send_sems, recv_sems, x, y, c, chips)
            for j, chip in enumerate(chips):
                got = couts[n].at[2 * chip[0] + chip[1]]
                _rcopy(got, got, send_sems, recv_sems, 2 * n_ici + j, (x, y, c)).wait_recv()
            for cp in remote:
                cp.wait_send()
            local.wait()

    outs = [jax.ShapeDtypeStruct(f.shape, f.dtype) for f in fulls]
    ins = list(fulls)
    if rbp is not None:
        ins.append(rbp)
        outs.append(jax.ShapeDtypeStruct((4,) + rbp.shape, F32))
    return _Comm(tuple(ins), tuple(outs), {t: t for t in range(n)}, 2 * n_ici + 4, start, finish)


def _chips_comm(sums):
    n = len(sums)

    def copies(cins, couts, send_sems, recv_sems):
        x, y, c, chips = _place()
        return [_rcopy(cins[t].at[2 * chip[0] + chip[1]], couts[t].at[j], send_sems, recv_sems, 3 * t + j, (*chip, c))
                for t in range(n) for j, chip in enumerate(chips)]

    def start(*refs):
        for cp in copies(*refs):
            cp.start()

    def finish(*refs):
        for cp in copies(*refs):
            cp.wait()

    outs = tuple(jax.ShapeDtypeStruct((3,) + p.shape[1:], p.dtype) for p in sums)
    return _Comm(tuple(sums), outs, {}, 3 * n, start, finish)


def _pair_comm(ins, outs, aliases, copies):
    def start(*refs):
        for cp in copies(*refs):
            cp.start()

    def finish(*refs):
        for cp in copies(*refs):
            cp.wait()

    return _Comm(tuple(ins), tuple(outs), aliases, len(ins), start, finish)


def _sibling_comm(parts):
    def copies(cins, couts, send_sems, recv_sems):
        x, y, c, _ = _place()
        return [_rcopy(p.at[:, pl.ds((1 - c) * (p.shape[1] // 2), p.shape[1] // 2), :], couts[t],
                       send_sems, recv_sems, t, (x, y, 1 - c)) for t, p in enumerate(cins)]

    half = [jax.ShapeDtypeStruct((p.shape[0], p.shape[1] // 2, p.shape[2]), p.dtype) for p in parts]
    return _pair_comm(parts, half, {}, copies)


def _halves_comm(tots):
    def copies(cins, couts, send_sems, recv_sems):
        x, y, c, _ = _place()
        return [_rcopy(g.at[c], g.at[c], send_sems, recv_sems, t, (x, y, 1 - c)) for t, g in enumerate(couts)]

    return _pair_comm(tots, [jax.ShapeDtypeStruct(t.shape, t.dtype) for t in tots],
                      {t: t for t in range(len(tots))}, copies)


def _gather_small(packed):
    def body(p_ref, all_ref, send_sems, recv_sems, loc_sems):
        x, y, c, _ = _place()
        me = 4 * x + 2 * y + c
        local = pltpu.make_async_copy(p_ref, all_ref.at[me], loc_sems.at[0])
        local.start()
        sent = []
        for k in range(1, 8):
            px, py, pc = x ^ (k >> 2), y ^ ((k >> 1) & 1), c ^ (k & 1)
            cp = _rcopy(p_ref, all_ref.at[me], send_sems, recv_sems, k - 1, (px, py, pc))
            cp.start()
            sent.append(cp)
        for k in range(1, 8):
            px, py, pc = x ^ (k >> 2), y ^ ((k >> 1) & 1), c ^ (k & 1)
            got = all_ref.at[4 * px + 2 * py + pc]
            _rcopy(got, got, send_sems, recv_sems, k - 1, (x, y, c)).wait_recv()
        for cp in sent:
            cp.wait_send()
        local.wait()

    return _comm_call(body, "gather_small", [packed], [jax.ShapeDtypeStruct((8,) + packed.shape, F32)], 7, 1)[0]


def _sum_devices(allp):
    n, r, c = allp.shape

    def body(a_ref, o_ref):
        acc = a_ref[0]
        for k in range(1, n):
            acc = acc + a_ref[k]
        o_ref[...] = acc

    return pl.pallas_call(body, name="sum_devices", out_shape=jax.ShapeDtypeStruct((r, c), F32))(allp)


def _ext_index():
    u = np.arange(EXT)
    dist = np.where(u < WIN, PAD - u, PAD + EXT - u)
    return np.clip(dist, -(CHUNK - 1), REL_CLIP) + (CHUNK - 1)


def _pack(parts, rows):
    flat = jnp.concatenate([p.reshape(-1) for p in parts])
    return jnp.pad(flat, (0, rows * 128 - flat.shape[0])).reshape(rows, 128)


def _unpack(packed, shapes):
    flat, out, at = packed.reshape(-1), [], 0
    for shp in shapes:
        size = int(np.prod(shp))
        out.append(flat[at:at + size].reshape(shp))
        at += size
    return out


def kernel(x, norm_g, w_in, q_norm_g, k_norm_g, rel_bias, w_out, loss_target, m_norm_g, m_w_in, m_q_norm_g, m_k_norm_g, m_rel_bias, m_w_out, v_norm_g, v_w_in, v_q_norm_g, v_k_norm_g, v_rel_bias, v_w_out):
    nl, d, nb = w_in.shape
    s = x.shape[1]
    ds_ = d // 2
    nh = ds_ // HEAD
    rb = w_out.shape[1]
    nrel = rel_bias.shape[2]
    bx = lax.axis_index("x") * 2 + lax.axis_index("y")

    rb_rows = -(-(nl * nh * nrel) // 1024) * 8
    cx = lax.axis_index("c")
    wi_full = [_cast_block(w_in, l, bx, "cast_w_in") for l in range(nl)]
    wo_full = [_cast_block(w_out, l, bx, "cast_w_out") for l in range(nl)]
    wi_full[0], wo_full[0], rel_all = _run_comm(
        _gather_comm([wi_full[0], wo_full[0]], _pack([rel_bias], rb_rows)), "gather_first")
    rel_full = jnp.concatenate(
        [rel_all[j].reshape(-1)[:nl * nh * nrel].reshape(nl, nh, nrel) for j in range(4)], axis=2)
    ext_idx = _ext_index()
    onehot = jnp.asarray(ext_idx[:, None] == np.arange(N_REL)[None, :], F32)
    ext = jnp.einsum("lhr,ur->lhu", rel_full, onehot, precision=lax.Precision.HIGHEST).reshape(nl, nh, 1, EXT)

    xs, hs, projs, yas, lts, ybs, mixes, wi_t = [], [], [], [], [], [], [], []
    xc = x[0]
    h = _rmsnorm_fwd(xc, norm_g[0:1])
    for l in range(nl):
        nxt = l + 1 < nl
        (proj, wt), got = _in_proj(h, wi_full[l], _gather_comm(
            [wo_full[l + 1]], stage="chips", chips_at=(0, 1)) if nxt else None)
        wi_t.append(wt)
        if nxt:
            wo_full[l + 1] = got[0]
        (ya, lt), got = _attn_a_fwd(proj, nh, _gather_comm(
            [wi_full[l + 1]], stage="chips", chips_at=(0, 1)) if nxt else None)
        if nxt:
            wi_full[l + 1] = got[0]
        (yb,), got = _attn_b_fwd(proj, q_norm_g[l:l + 1], k_norm_g[l:l + 1], ext[l], nh, _gather_comm(
            [wi_full[l + 1], wo_full[l + 1]], stage="chips", chips_at=(2, 3)) if nxt else None)
        if nxt:
            wi_full[l + 1], wo_full[l + 1] = got
        xs.append(xc)
        hs.append(h); projs.append(proj); yas.append(ya); lts.append(lt); ybs.append(yb)
        wo_now = wo_full[l].reshape(4 * rb, d)
        (mix, after, then), got = _out_proj(
            xc, ya, yb, proj, wo_now, norm_g[l + 1:l + 2] if nxt else loss_target[0],
            _gather_comm([wi_full[l + 1], wo_full[l + 1]], stage="sibling") if nxt else None)
        mixes.append(mix)
        if nxt:
            wi_full[l + 1], wo_full[l + 1] = got
            xc, h = after, then
        else:
            dx, loss_tile = after, then

    small, g_wi, g_wo = [None] * nl, [None] * nl, [None] * nl
    pending = None

    def keep(lay, shared):
        g_wi[lay], g_wo[lay] = shared[0].reshape(d, nb), shared[1].reshape(rb, d)

    for l in reversed(range(nl)):
        wo = wo_full[l].reshape(4 * rb, d)
        p_wo = _wgrad(mixes[l], dx, 4, False, "wgrad_out")
        last = l == 0 and pending is not None
        (dya, dyb, dga, dgb), theirs_wo = _out_proj_bwd(dx, yas[l], ybs[l], projs[l], wo,
                                                        _sibling_comm([p_wo]) if last else None)
        travelling = list(pending[1]) if pending else []
        if last:
            sums_wo = _add_sibling(p_wo, theirs_wo[0], cx)
            travelling.append(sums_wo)
        (dqa, dka, dva), got = _attn_a_bwd(projs[l], lts[l], dya, nh, _chips_comm(travelling) if pending else None)
        tots = [_add_chips(pending[1][t], got[t], bx, cx) for t in range(2)] if pending else None
        (dqb, dkb, dvb, dqg, dkg, dext), shared = _attn_b_bwd(
            projs[l], dyb, q_norm_g[l:l + 1], k_norm_g[l:l + 1], ext[l], nh, _halves_comm(tots) if pending else None)
        if pending:
            keep(pending[0], shared)
        dproj = jnp.concatenate([dqa, dka, dva, dga, dqb, dkb, dvb, dgb], axis=1)
        parts = [_wgrad(hs[l], dproj, 4, True, "wgrad_in"), p_wo]
        if l > 0:
            (dx, dng), theirs = _in_proj_bwd(dproj, wi_t[l],xs[l], dx, norm_g[l:l + 1], _sibling_comm(parts))
            pending = (l, [_add_sibling(parts[t], theirs[t], cx) for t in range(2)])
        elif last:
            theirs = _run_comm(_sibling_comm(parts[:1]), "reduce_sibling")
            sums_wi = _add_sibling(parts[0], theirs[0], cx)
            (dx, dng), got_wi = _in_proj_bwd(dproj, wi_t[l],xs[l], dx, norm_g[l:l + 1], _chips_comm([sums_wi]))
            keep(0, _run_comm(_halves_comm([_add_chips(sums_wi, got_wi[0], bx, cx),
                                            _add_chips(sums_wo, got[2], bx, cx)]), "share_halves"))
        else:
            theirs = _run_comm(_sibling_comm(parts), "reduce_sibling")
            sums = [_add_sibling(parts[t], theirs[t], cx) for t in range(2)]
            (dx, dng), got = _in_proj_bwd(dproj, wi_t[l],xs[l], dx, norm_g[l:l + 1], _chips_comm(sums))
            keep(0, _run_comm(_halves_comm([_add_chips(sums[t], got[t], bx, cx) for t in range(2)]), "share_halves"))
        small[l] = (dng[0], jnp.sum(dqg, axis=0).reshape(-1), jnp.sum(dkg, axis=0).reshape(-1), dext.reshape(nh, EXT))
    grad_x = dx[None]

    small_shapes = [(nl, d), (nl, HEAD), (nl, HEAD), (nl, nh, EXT), (1,)]
    small_parts = [jnp.stack([sm[i] for sm in small]) for i in range(4)] + [loss_tile[0, :1]]
    rows = -(-sum(int(np.prod(sh)) for sh in small_shapes) // 1024) * 8
    tot = _sum_devices(_gather_small(_pack(small_parts, rows)))
    g_ng, g_qg, g_kg, g_ext, loss = _unpack(tot, small_shapes)
    g_rel_full = jnp.einsum("lhu,ur->lhr", g_ext, onehot, precision=lax.Precision.HIGHEST)
    g_rel = lax.dynamic_slice_in_dim(g_rel_full, bx * nrel, nrel, axis=2)

    res_wi, res_wo = (), ()
    for l in range(nl):
        res_wi = _adamw_layer(l, w_in, g_wi[l], m_w_in, v_w_in, res_wi, "adamw_w_in")
        res_wo = _adamw_layer(l, w_out, g_wo[l], m_w_out, v_w_out, res_wo, "adamw_w_out")
    g_wi, d_wi, nm_wi, nv_wi = res_wi
    g_wo, d_wo, nm_wo, nv_wo = res_wo
    sm_shapes = [(nl, d), (nl, HEAD), (nl, HEAD), (nl, nh, nrel)]
    sm_rows = -(-sum(int(np.prod(sh)) for sh in sm_shapes) // 1024) * 8
    pw, pg, pm, pv = [_pack(group, sm_rows) for group in (
        (norm_g, q_norm_g, k_norm_g, rel_bias), (g_ng, g_qg, g_kg, g_rel),
        (m_norm_g, m_q_norm_g, m_k_norm_g, m_rel_bias), (v_norm_g, v_q_norm_g, v_k_norm_g, v_rel_bias))]
    d_sm, nm_sm, nv_sm = [_unpack(a[0], sm_shapes)
                          for a in _adamw_layer(0, pw[None], pg, pm[None], pv[None], (), "adamw_small")[1:]]

    return (loss[0], grad_x, g_ng, g_wi, g_qg, g_kg, g_rel, g_wo,
            d_sm[0], d_wi, d_sm[1], d_sm[2], d_sm[3], d_wo,
            nm_sm[0], nm_wi, nm_sm[1], nm_sm[2], nm_sm[3], nm_wo,
            nv_sm[0], nv_wi, nv_sm[1], nv_sm[2], nv_sm[3], nv_wo)
```
